```python
import jax, jax.numpy as jnp
from jax import lax
import numpy as np

D_MODEL = 1024
BATCH = 16
SEQ = 2048
DEPTH = 4

CONV_WIDTH = D_MODEL
CONV_K = 3
SGU_WIDTH = D_MODEL
SGU_CHUNK = 128
SGU_HEADS = 8
SGU_HEAD_DIM = SGU_WIDTH // SGU_HEADS
LRU_WIDTH = D_MODEL
LRU_HEADS = 16
LRU_HEAD_DIM = LRU_WIDTH // LRU_HEADS
LRU_CONV_K = 4
LRU_C = 8.0
N_BRANCH = 3
IN_COLS = 4 * CONV_WIDTH + 3 * SGU_WIDTH + 2 * LRU_WIDTH + N_BRANCH * D_MODEL
EPS = 1e-6

kernel_name = "hybrid_gated_conv_sgu_rglru"


def rms_norm(x, g):
    xf = x.astype(jnp.float32)
    y = xf * lax.rsqrt(jnp.mean(xf * xf, axis=-1, keepdims=True) + EPS)
    return (y * g.astype(jnp.float32)).astype(x.dtype)


def causal_dwconv(u, w):
    k = w.shape[0]
    s = u.shape[1]
    up = jnp.pad(u, ((0, 0), (k - 1, 0), (0, 0)))
    return sum(w[j] * up[:, j:j + s] for j in range(k))


def short_conv_mixer(x_a, b_gate, c_gate, w_conv):
    return b_gate * causal_dwconv(c_gate * x_a, w_conv)


def sgu_mixer(u, v, w_s, b_s):
    bsz, s, _ = v.shape
    nc = s // SGU_CHUNK
    vh = v.reshape(bsz, nc, SGU_CHUNK, SGU_HEADS, SGU_HEAD_DIM)
    vf = vh.astype(jnp.float32)
    mu = jnp.mean(vf, axis=-1, keepdims=True)
    var = jnp.mean(jnp.square(vf - mu), axis=-1, keepdims=True)
    vn = ((vf - mu) * lax.rsqrt(var + EPS)).astype(v.dtype)
    causal = jnp.tril(jnp.ones((SGU_CHUNK, SGU_CHUNK), dtype=bool))
    ws = jnp.where(causal[None], w_s, jnp.zeros_like(w_s))
    z = jnp.einsum("hts,bnshd->bnthd", ws, vn) + b_s.T[None, None, :, :, None]
    return u * z.reshape(bsz, s, SGU_WIDTH)


def rg_lru_mixer(x_r, w_conv, b_conv, w_a, b_a, w_x, b_x, lam):
    bsz, s, _ = x_r.shape
    xc = causal_dwconv(x_r, w_conv) + b_conv
    xh = xc.reshape(bsz, s, LRU_HEADS, LRU_HEAD_DIM)
    r = jax.nn.sigmoid(jnp.einsum("bshd,hde->bshe", xh, w_a) + b_a).reshape(bsz, s, LRU_WIDTH)
    i = jax.nn.sigmoid(jnp.einsum("bshd,hde->bshe", xh, w_x) + b_x).reshape(bsz, s, LRU_WIDTH)
    log_a = -LRU_C * r.astype(jnp.float32) * jax.nn.softplus(-lam.astype(jnp.float32))
    a = jnp.exp(log_a)
    mult = jnp.sqrt(-jnp.expm1(2.0 * log_a))
    b = mult * (i * xc).astype(jnp.float32)

    def combine(left, right):
        a_l, b_l = left
        a_r, b_r = right
        return a_l * a_r, a_r * b_l + b_r

    _, h = lax.associative_scan(combine, (a, b), axis=1)
    return h.astype(x_r.dtype)


def hybrid_layer(x, c_act, gain, w_mod, b_mod, w_in, w_out, conv_a_w, sgu_w, sgu_b,
                 lru_conv_w, lru_conv_b, lru_wa, lru_ba, lru_wx, lru_bx, lru_lambda):
    mod = c_act @ w_mod + b_mod
    shift, scale, gate = jnp.split(mod, 3, axis=-1)
    h = rms_norm(x, gain) * (1.0 + scale[:, None, :]) + shift[:, None, :]
    proj = h @ w_in
    sizes = (CONV_WIDTH,) * 4 + (SGU_WIDTH,) * 3 + (LRU_WIDTH,) * 2 + (D_MODEL,) * N_BRANCH
    splits = np.cumsum(sizes)[:-1].tolist()
    (a_x, a_b, a_c, a_z, s_u, s_v, s_z, r_x, r_z, g_a, g_s, g_r) = jnp.split(proj, splits, axis=-1)
    y_a = jax.nn.silu(a_z) * short_conv_mixer(a_x, a_b, a_c, conv_a_w)
    y_s = jax.nn.silu(s_z) * sgu_mixer(s_u, s_v, sgu_w, sgu_b)
    y_r = jax.nn.silu(r_z) * rg_lru_mixer(r_x, lru_conv_w, lru_conv_b, lru_wa, lru_ba,
                                          lru_wx, lru_bx, lru_lambda)
    merged = jax.nn.sigmoid(g_a) * y_a + jax.nn.sigmoid(g_s) * y_s + jax.nn.sigmoid(g_r) * y_r
    return x + gate[:, None, :] * (merged @ w_out)


def _fwd_setup_inputs(seed: int = 0) -> dict:
    key = jax.random.key(seed)
    ks = jax.random.split(key, 20)
    nrm = jax.random.normal
    d = D_MODEL
    x = nrm(ks[0], (BATCH, SEQ, d), jnp.float32)
    c = nrm(ks[1], (BATCH, d), jnp.float32)
    norm_gain = 1.0 + 0.05 * nrm(ks[2], (DEPTH, d), jnp.float32)
    w_mod = 0.1 * d ** -0.5 * nrm(ks[3], (DEPTH, d, 3 * d), jnp.float32)
    b_mod = 0.02 * nrm(ks[4], (DEPTH, 3 * d), jnp.float32)
    w_in = d ** -0.5 * nrm(ks[5], (DEPTH, d, IN_COLS), jnp.float32)
    w_out = d ** -0.5 * nrm(ks[6], (DEPTH, d, d), jnp.float32)
    conv_a_w = CONV_K ** -0.5 * nrm(ks[7], (DEPTH, CONV_K, CONV_WIDTH), jnp.float32)
    sgu_w = SGU_CHUNK ** -0.5 * nrm(ks[8], (DEPTH, SGU_HEADS, SGU_CHUNK, SGU_CHUNK), jnp.float32)
    sgu_b = 1.0 + 0.1 * nrm(ks[9], (DEPTH, SGU_HEADS, SGU_CHUNK), jnp.float32)
    lru_conv_w = LRU_CONV_K ** -0.5 * nrm(ks[10], (DEPTH, LRU_CONV_K, LRU_WIDTH), jnp.float32)
    lru_conv_b = 0.02 * nrm(ks[11], (DEPTH, LRU_WIDTH), jnp.float32)
    lru_wa = LRU_HEAD_DIM ** -0.5 * nrm(ks[12], (DEPTH, LRU_HEADS, LRU_HEAD_DIM, LRU_HEAD_DIM), jnp.float32)
    lru_ba = 0.02 * nrm(ks[13], (DEPTH, LRU_HEADS, LRU_HEAD_DIM), jnp.float32)
    lru_wx = LRU_HEAD_DIM ** -0.5 * nrm(ks[14], (DEPTH, LRU_HEADS, LRU_HEAD_DIM, LRU_HEAD_DIM), jnp.float32)
    lru_bx = 0.02 * nrm(ks[15], (DEPTH, LRU_HEADS, LRU_HEAD_DIM), jnp.float32)
    a_c = jax.random.uniform(ks[16], (DEPTH, LRU_WIDTH), jnp.float32, 0.9, 0.999)
    a_base = a_c ** (1.0 / LRU_C)
    lru_lambda = jnp.log(a_base) - jnp.log1p(-a_base)
    final_gain = 1.0 + 0.05 * nrm(ks[17], (d,), jnp.float32)
    return {"x": x, "c": c, "norm_gain": norm_gain, "w_mod": w_mod, "b_mod": b_mod,
            "w_in": w_in, "w_out": w_out, "conv_a_w": conv_a_w, "sgu_w": sgu_w, "sgu_b": sgu_b,
            "lru_conv_w": lru_conv_w, "lru_conv_b": lru_conv_b, "lru_wa": lru_wa, "lru_ba": lru_ba,
            "lru_wx": lru_wx, "lru_bx": lru_bx, "lru_lambda": lru_lambda, "final_gain": final_gain}


def _fwd_reference(x, c, norm_gain, w_mod, b_mod, w_in, w_out, conv_a_w, sgu_w, sgu_b,
              lru_conv_w, lru_conv_b, lru_wa, lru_ba, lru_wx, lru_bx, lru_lambda, final_gain):
    c_act = jax.nn.silu(c)
    for l in range(DEPTH):
        x = hybrid_layer(x, c_act, norm_gain[l], w_mod[l], b_mod[l], w_in[l], w_out[l],
                         conv_a_w[l], sgu_w[l], sgu_b[l], lru_conv_w[l], lru_conv_b[l],
                         lru_wa[l], lru_ba[l], lru_wx[l], lru_bx[l], lru_lambda[l])
    return rms_norm(x, final_gain)


import jax as _jax
import jax.numpy as _jnp

TWIN_FORMAT = 'train_step'
FWD_PARAMS = ['x', 'c', 'norm_gain', 'w_mod', 'b_mod', 'w_in', 'w_out', 'conv_a_w', 'sgu_w', 'sgu_b', 'lru_conv_w', 'lru_conv_b', 'lru_wa', 'lru_ba', 'lru_wx', 'lru_bx', 'lru_lambda', 'final_gain']
TWIN_WEIGHTS = ['norm_gain', 'w_mod', 'b_mod', 'w_in', 'w_out', 'conv_a_w', 'sgu_w', 'sgu_b', 'lru_conv_w', 'lru_conv_b', 'lru_wa', 'lru_ba', 'lru_wx', 'lru_bx', 'lru_lambda', 'final_gain']
TWIN_DIFF_INPUT = 'x'
TWIN_INPUTS = ['x', 'c', 'norm_gain', 'w_mod', 'b_mod', 'w_in', 'w_out', 'conv_a_w', 'sgu_w', 'sgu_b', 'lru_conv_w', 'lru_conv_b', 'lru_wa', 'lru_ba', 'lru_wx', 'lru_bx', 'lru_lambda', 'final_gain', 'loss_target', 'm_norm_gain', 'm_w_mod', 'm_b_mod', 'm_w_in', 'm_w_out', 'm_conv_a_w', 'm_sgu_w', 'm_sgu_b', 'm_lru_conv_w', 'm_lru_conv_b', 'm_lru_wa', 'm_lru_ba', 'm_lru_wx', 'm_lru_bx', 'm_lru_lambda', 'm_final_gain', 'v_norm_gain', 'v_w_mod', 'v_b_mod', 'v_w_in', 'v_w_out', 'v_conv_a_w', 'v_sgu_w', 'v_sgu_b', 'v_lru_conv_w', 'v_lru_conv_b', 'v_lru_wa', 'v_lru_ba', 'v_lru_wx', 'v_lru_bx', 'v_lru_lambda', 'v_final_gain']
TWIN_OUTPUTS = ['loss', 'grad_x', 'grad_norm_gain', 'grad_w_mod', 'grad_b_mod', 'grad_w_in', 'grad_w_out', 'grad_conv_a_w', 'grad_sgu_w', 'grad_sgu_b', 'grad_lru_conv_w', 'grad_lru_conv_b', 'grad_lru_wa', 'grad_lru_ba', 'grad_lru_wx', 'grad_lru_bx', 'grad_lru_lambda', 'grad_final_gain', 'delta_norm_gain', 'delta_w_mod', 'delta_b_mod', 'delta_w_in', 'delta_w_out', 'delta_conv_a_w', 'delta_sgu_w', 'delta_sgu_b', 'delta_lru_conv_w', 'delta_lru_conv_b', 'delta_lru_wa', 'delta_lru_ba', 'delta_lru_wx', 'delta_lru_bx', 'delta_lru_lambda', 'delta_final_gain', 'new_m_norm_gain', 'new_m_w_mod', 'new_m_b_mod', 'new_m_w_in', 'new_m_w_out', 'new_m_conv_a_w', 'new_m_sgu_w', 'new_m_sgu_b', 'new_m_lru_conv_w', 'new_m_lru_conv_b', 'new_m_lru_wa', 'new_m_lru_ba', 'new_m_lru_wx', 'new_m_lru_bx', 'new_m_lru_lambda', 'new_m_final_gain', 'new_v_norm_gain', 'new_v_w_mod', 'new_v_b_mod', 'new_v_w_in', 'new_v_w_out', 'new_v_conv_a_w', 'new_v_sgu_w', 'new_v_sgu_b', 'new_v_lru_conv_w', 'new_v_lru_conv_b', 'new_v_lru_wa', 'new_v_lru_ba', 'new_v_lru_wx', 'new_v_lru_bx', 'new_v_lru_lambda', 'new_v_final_gain']
TWIN_LEAF_KINDS = {'loss': 'loss', 'grad_x': 'grad_x', 'grad_norm_gain': 'grad_w', 'grad_w_mod': 'grad_w', 'grad_b_mod': 'grad_w', 'grad_w_in': 'grad_w', 'grad_w_out': 'grad_w', 'grad_conv_a_w': 'grad_w', 'grad_sgu_w': 'grad_w', 'grad_sgu_b': 'grad_w', 'grad_lru_conv_w': 'grad_w', 'grad_lru_conv_b': 'grad_w', 'grad_lru_wa': 'grad_w', 'grad_lru_ba': 'grad_w', 'grad_lru_wx': 'grad_w', 'grad_lru_bx': 'grad_w', 'grad_lru_lambda': 'grad_w', 'grad_final_gain': 'grad_w', 'delta_norm_gain': 'delta_w', 'delta_w_mod': 'delta_w', 'delta_b_mod': 'delta_w', 'delta_w_in': 'delta_w', 'delta_w_out': 'delta_w', 'delta_conv_a_w': 'delta_w', 'delta_sgu_w': 'delta_w', 'delta_sgu_b': 'delta_w', 'delta_lru_conv_w': 'delta_w', 'delta_lru_conv_b': 'delta_w', 'delta_lru_wa': 'delta_w', 'delta_lru_ba': 'delta_w', 'delta_lru_wx': 'delta_w', 'delta_lru_bx': 'delta_w', 'delta_lru_lambda': 'delta_w', 'delta_final_gain': 'delta_w', 'new_m_norm_gain': 'new_m', 'new_m_w_mod': 'new_m', 'new_m_b_mod': 'new_m', 'new_m_w_in': 'new_m', 'new_m_w_out': 'new_m', 'new_m_conv_a_w': 'new_m', 'new_m_sgu_w': 'new_m', 'new_m_sgu_b': 'new_m', 'new_m_lru_conv_w': 'new_m', 'new_m_lru_conv_b': 'new_m', 'new_m_lru_wa': 'new_m', 'new_m_lru_ba': 'new_m', 'new_m_lru_wx': 'new_m', 'new_m_lru_bx': 'new_m', 'new_m_lru_lambda': 'new_m', 'new_m_final_gain': 'new_m', 'new_v_norm_gain': 'new_v', 'new_v_w_mod': 'new_v', 'new_v_b_mod': 'new_v', 'new_v_w_in': 'new_v', 'new_v_w_out': 'new_v', 'new_v_conv_a_w': 'new_v', 'new_v_sgu_w': 'new_v', 'new_v_sgu_b': 'new_v', 'new_v_lru_conv_w': 'new_v', 'new_v_lru_conv_b': 'new_v', 'new_v_lru_wa': 'new_v', 'new_v_lru_ba': 'new_v', 'new_v_lru_wx': 'new_v', 'new_v_lru_bx': 'new_v', 'new_v_lru_lambda': 'new_v', 'new_v_final_gain': 'new_v'}


def _forward(args):
    return _fwd_reference(*[args[k] for k in FWD_PARAMS])


def _output_shape():
    out = _jax.eval_shape(lambda: _forward(_fwd_setup_inputs(0)))
    return out.shape, out.dtype

N_MICROBATCH = 1
ADAM_LR = 0.001
ADAM_B1 = 0.9
ADAM_B2 = 0.999
ADAM_EPS = 1e-08
ADAM_WD = 0.01
ADAM_STEP = 10
PER_EXAMPLE_BATCH_AXIS = {'x': 0, 'c': 0, 'loss_target': 0}
SHARED_INPUTS = []
_WEIGHT_DTYPES = {'norm_gain': _jnp.float32, 'w_mod': _jnp.float32, 'b_mod': _jnp.float32, 'w_in': _jnp.float32, 'w_out': _jnp.float32, 'conv_a_w': _jnp.float32, 'sgu_w': _jnp.float32, 'sgu_b': _jnp.float32, 'lru_conv_w': _jnp.float32, 'lru_conv_b': _jnp.float32, 'lru_wa': _jnp.float32, 'lru_ba': _jnp.float32, 'lru_wx': _jnp.float32, 'lru_bx': _jnp.float32, 'lru_lambda': _jnp.float32, 'final_gain': _jnp.float32}
MOMENT_SCALE = {'norm_gain': 1.160716e-02, 'w_mod': 3.982917e-02, 'b_mod': 6.678035e-02, 'w_in': 3.390152e-03, 'w_out': 6.799427e-03, 'conv_a_w': 3.871194e-03, 'sgu_w': 2.708153e-03, 'sgu_b': 3.810824e-03, 'lru_conv_w': 3.355662e-03, 'lru_conv_b': 2.946021e-02, 'lru_wa': 1.046952e-03, 'lru_ba': 9.054472e-04, 'lru_wx': 1.912681e-03, 'lru_bx': 1.266938e-03, 'lru_lambda': 1.649288e-03, 'final_gain': 3.208008e+01}


def _to_microbatches(a, axis):
    t = _jnp.moveaxis(a, axis, 0)
    t = t.reshape((N_MICROBATCH, t.shape[0] // N_MICROBATCH) + t.shape[1:])
    return _jnp.moveaxis(t, 1, axis + 1)


def setup_inputs(seed: int = 0) -> dict:
    inp = _fwd_setup_inputs(seed)
    key = _jax.random.fold_in(_jax.random.key(seed), 7919)
    shape, _ = _output_shape()
    out = dict(inp)
    out["loss_target"] = _jax.random.normal(_jax.random.fold_in(key, 0), shape, _jnp.float32)
    for i, name in enumerate(TWIN_WEIGHTS):
        w = inp[name].astype(_jnp.float32)
        if MOMENT_SCALE is None:
            s = _jnp.sqrt(_jnp.mean(_jnp.square(w)) + 1e-30)
        else:
            s = MOMENT_SCALE[name]
        km, kv = _jax.random.split(_jax.random.fold_in(key, i + 1))
        out[name] = w
        out["m_" + name] = s * _jax.random.normal(km, w.shape, _jnp.float32)
        out["v_" + name] = (s * s) * _jax.random.uniform(kv, w.shape, _jnp.float32, 0.5, 1.5)
    if N_MICROBATCH > 1:
        for name, axis in PER_EXAMPLE_BATCH_AXIS.items():
            out[name] = _to_microbatches(out[name], axis)
    return {'x': out['x'], 'c': out['c'], 'norm_gain': out['norm_gain'], 'w_mod': out['w_mod'], 'b_mod': out['b_mod'], 'w_in': out['w_in'], 'w_out': out['w_out'], 'conv_a_w': out['conv_a_w'], 'sgu_w': out['sgu_w'], 'sgu_b': out['sgu_b'], 'lru_conv_w': out['lru_conv_w'], 'lru_conv_b': out['lru_conv_b'], 'lru_wa': out['lru_wa'], 'lru_ba': out['lru_ba'], 'lru_wx': out['lru_wx'], 'lru_bx': out['lru_bx'], 'lru_lambda': out['lru_lambda'], 'final_gain': out['final_gain'], 'loss_target': out['loss_target'], 'm_norm_gain': out['m_norm_gain'], 'm_w_mod': out['m_w_mod'], 'm_b_mod': out['m_b_mod'], 'm_w_in': out['m_w_in'], 'm_w_out': out['m_w_out'], 'm_conv_a_w': out['m_conv_a_w'], 'm_sgu_w': out['m_sgu_w'], 'm_sgu_b': out['m_sgu_b'], 'm_lru_conv_w': out['m_lru_conv_w'], 'm_lru_conv_b': out['m_lru_conv_b'], 'm_lru_wa': out['m_lru_wa'], 'm_lru_ba': out['m_lru_ba'], 'm_lru_wx': out['m_lru_wx'], 'm_lru_bx': out['m_lru_bx'], 'm_lru_lambda': out['m_lru_lambda'], 'm_final_gain': out['m_final_gain'], 'v_norm_gain': out['v_norm_gain'], 'v_w_mod': out['v_w_mod'], 'v_b_mod': out['v_b_mod'], 'v_w_in': out['v_w_in'], 'v_w_out': out['v_w_out'], 'v_conv_a_w': out['v_conv_a_w'], 'v_sgu_w': out['v_sgu_w'], 'v_sgu_b': out['v_sgu_b'], 'v_lru_conv_w': out['v_lru_conv_w'], 'v_lru_conv_b': out['v_lru_conv_b'], 'v_lru_wa': out['v_lru_wa'], 'v_lru_ba': out['v_lru_ba'], 'v_lru_wx': out['v_lru_wx'], 'v_lru_bx': out['v_lru_bx'], 'v_lru_lambda': out['v_lru_lambda'], 'v_final_gain': out['v_final_gain']}


def _loss(weights, diff, rest, loss_target):
    with _jax.named_scope("forward"):
        args = {**rest, TWIN_DIFF_INPUT: diff, **{k: w.astype(_WEIGHT_DTYPES[k]) for k, w in weights.items()}}
        y = _forward(args)
    with _jax.named_scope("loss_head"):
        err = _jnp.square(y.astype(_jnp.float32) - loss_target)
        return 0.5 * _jnp.sum(_jnp.mean(err, axis=-1)) if err.ndim else 0.5 * err


def _adamw(w, g, m, v):
    m = ADAM_B1 * m + (1.0 - ADAM_B1) * g
    v = ADAM_B2 * v + (1.0 - ADAM_B2) * _jnp.square(g)
    m_hat = m / (1.0 - ADAM_B1 ** ADAM_STEP)
    v_hat = v / (1.0 - ADAM_B2 ** ADAM_STEP)
    delta = -ADAM_LR * (m_hat / (_jnp.sqrt(v_hat) + ADAM_EPS) + ADAM_WD * w)
    return delta, m, v


def reference(x, c, norm_gain, w_mod, b_mod, w_in, w_out, conv_a_w, sgu_w, sgu_b, lru_conv_w, lru_conv_b, lru_wa, lru_ba, lru_wx, lru_bx, lru_lambda, final_gain, loss_target, m_norm_gain, m_w_mod, m_b_mod, m_w_in, m_w_out, m_conv_a_w, m_sgu_w, m_sgu_b, m_lru_conv_w, m_lru_conv_b, m_lru_wa, m_lru_ba, m_lru_wx, m_lru_bx, m_lru_lambda, m_final_gain, v_norm_gain, v_w_mod, v_b_mod, v_w_in, v_w_out, v_conv_a_w, v_sgu_w, v_sgu_b, v_lru_conv_w, v_lru_conv_b, v_lru_wa, v_lru_ba, v_lru_wx, v_lru_bx, v_lru_lambda, v_final_gain):
    given = dict(x=x, c=c, norm_gain=norm_gain, w_mod=w_mod, b_mod=b_mod, w_in=w_in, w_out=w_out, conv_a_w=conv_a_w, sgu_w=sgu_w, sgu_b=sgu_b, lru_conv_w=lru_conv_w, lru_conv_b=lru_conv_b, lru_wa=lru_wa, lru_ba=lru_ba, lru_wx=lru_wx, lru_bx=lru_bx, lru_lambda=lru_lambda, final_gain=final_gain, loss_target=loss_target, m_norm_gain=m_norm_gain, m_w_mod=m_w_mod, m_b_mod=m_b_mod, m_w_in=m_w_in, m_w_out=m_w_out, m_conv_a_w=m_conv_a_w, m_sgu_w=m_sgu_w, m_sgu_b=m_sgu_b, m_lru_conv_w=m_lru_conv_w, m_lru_conv_b=m_lru_conv_b, m_lru_wa=m_lru_wa, m_lru_ba=m_lru_ba, m_lru_wx=m_lru_wx, m_lru_bx=m_lru_bx, m_lru_lambda=m_lru_lambda, m_final_gain=m_final_gain, v_norm_gain=v_norm_gain, v_w_mod=v_w_mod, v_b_mod=v_b_mod, v_w_in=v_w_in, v_w_out=v_w_out, v_conv_a_w=v_conv_a_w, v_sgu_w=v_sgu_w, v_sgu_b=v_sgu_b, v_lru_conv_w=v_lru_conv_w, v_lru_conv_b=v_lru_conv_b, v_lru_wa=v_lru_wa, v_lru_ba=v_lru_ba, v_lru_wx=v_lru_wx, v_lru_bx=v_lru_bx, v_lru_lambda=v_lru_lambda, v_final_gain=v_final_gain)
    weights = {n: given[n] for n in TWIN_WEIGHTS}
    shared = {n: given[n] for n in SHARED_INPUTS}
    per_example = {n: given[n] for n in ['x', 'c']}
    grad_fn = _jax.value_and_grad(_loss, argnums=(0, 1))

    def one_microbatch(ex, loss_target):
        ex = dict(ex)
        diff = ex.pop(TWIN_DIFF_INPUT)
        return grad_fn(weights, diff, {**shared, **ex}, loss_target)

    if N_MICROBATCH == 1:
        loss, (grad_w, grad_x) = one_microbatch(per_example, given["loss_target"])
    else:
        def body(carry, xs):
            loss_sum, grad_sum = carry
            l_k, (gw_k, gx_k) = one_microbatch(xs[0], xs[1])
            with _jax.named_scope("update"):
                return (loss_sum + l_k, _jax.tree.map(_jnp.add, grad_sum, gw_k)), gx_k

        init = (_jnp.zeros((), _jnp.float32), _jax.tree.map(_jnp.zeros_like, weights))
        (loss, grad_w), grad_x = _jax.lax.scan(body, init, (per_example, given["loss_target"]))
    with _jax.named_scope("update"):
        delta_w, new_m, new_v = {}, {}, {}
        for n in TWIN_WEIGHTS:
            delta_w[n], new_m[n], new_v[n] = _adamw(weights[n], grad_w[n], given["m_" + n], given["v_" + n])
    return (loss, grad_x, *[grad_w[n] for n in TWIN_WEIGHTS], *[delta_w[n] for n in TWIN_WEIGHTS],
            *[new_m[n] for n in TWIN_WEIGHTS], *[new_v[n] for n in TWIN_WEIGHTS])
```

```python
import functools

import jax
import jax.numpy as jnp
from jax import lax
from jax.experimental import pallas as pl
from jax.experimental.pallas import tpu as pltpu

F32 = jnp.float32
BF16 = jnp.bfloat16

D = 1024
N_DEV = 8
N_SEG = 12
LANES = 128
CHUNK = 128
HALO = 16
UNIT = 512
UNITS_PER_DEV = 3
EPS = 1e-6
LRU_C = 8.0
ADAM_LR, ADAM_B1, ADAM_B2, ADAM_EPS, ADAM_WD, ADAM_STEP = 0.001, 0.9, 0.999, 1e-08, 0.01, 10
VMEM_LIMIT = 56 * 1024 * 1024

AX, AB, AC, AZ, SU, SV, SZ, RX, RZ, GA, GS, GR = range(N_SEG)
MESH = pl.DeviceIdType.MESH


def _params(sem=None):
    return pltpu.CompilerParams(dimension_semantics=sem, vmem_limit_bytes=VMEM_LIMIT)


def _my_index():
    return 4 * lax.axis_index("x") + 2 * lax.axis_index("y") + lax.axis_index("c")


def _all_gather(block, name):
    def body(x_ref, out_ref, send_sems, recv_sems, local_sem):
        x, y, c = lax.axis_index("x"), lax.axis_index("y"), lax.axis_index("c")
        me, sibling = (x, y, c), (x, y, 1 - c)
        chips = [(1 - x, y), (x, 1 - y), (1 - x, 1 - y)]

        def rows(px, py, pc):
            return out_ref.at[4 * px + 2 * py + pc]

        def copy(k, blk, to, src=None):
            return pltpu.make_async_remote_copy(
                src_ref=rows(*blk) if src is None else src, dst_ref=rows(*blk),
                send_sem=send_sems.at[k], recv_sem=recv_sems.at[k], device_id=to, device_id_type=MESH)

        mine = pltpu.make_async_copy(x_ref, rows(*me), local_sem)
        mine.start()
        first = [copy(0, me, sibling, src=x_ref)]
        first += [copy(1 + j, me, (*chip, c), src=x_ref) for j, chip in enumerate(chips)]
        for cp in first:
            cp.start()
        passed = [copy(4 + j, (*chip, c), sibling) for j, chip in enumerate(chips)]
        for j, chip in enumerate(chips):
            copy(1 + j, (*chip, c), me).wait_recv()
            passed[j].start()
        copy(0, sibling, me).wait_recv()
        for j, chip in enumerate(chips):
            copy(4 + j, (*chip, 1 - c), me).wait_recv()
        for cp in first + passed:
            cp.wait_send()
        mine.wait()

    return pl.pallas_call(
        body, name=name,
        out_shape=jax.ShapeDtypeStruct((N_DEV,) + block.shape, block.dtype),
        in_specs=[pl.BlockSpec(memory_space=pl.ANY)],
        out_specs=pl.BlockSpec(memory_space=pl.ANY),
        scratch_shapes=[pltpu.SemaphoreType.DMA((7,)), pltpu.SemaphoreType.DMA((7,)), pltpu.SemaphoreType.DMA],
    )(block)


def _all_to_all(blocks, name):
    def body(x_ref, out_ref, send_sems, recv_sems, local_sem):
        x, y, c = lax.axis_index("x"), lax.axis_index("y"), lax.axis_index("c")
        my = 4 * x + 2 * y + c
        mine = pltpu.make_async_copy(x_ref.at[my], out_ref.at[my], local_sem)
        mine.start()
        peers = []
        for r in range(1, N_DEV):
            px = 1 - x if r & 4 else x
            py = 1 - y if r & 2 else y
            pc = 1 - c if r & 1 else c
            peers.append((r - 1, 4 * px + 2 * py + pc, (px, py, pc)))

        def copy(k, src_slot, dst_slot, to):
            return pltpu.make_async_remote_copy(
                src_ref=x_ref.at[src_slot], dst_ref=out_ref.at[dst_slot],
                send_sem=send_sems.at[k], recv_sem=recv_sems.at[k], device_id=to, device_id_type=MESH)

        sends = [copy(k, pid, my, to) for k, pid, to in peers]
        for cp in sends:
            cp.start()
        for k, pid, to in peers:
            copy(k, pid, pid, to).wait_recv()
        for cp in sends:
            cp.wait_send()
        mine.wait()

    return pl.pallas_call(
        body, name=name,
        out_shape=jax.ShapeDtypeStruct(blocks.shape, blocks.dtype),
        in_specs=[pl.BlockSpec(memory_space=pl.ANY)],
        out_specs=pl.BlockSpec(memory_space=pl.ANY),
        scratch_shapes=[pltpu.SemaphoreType.DMA((7,)), pltpu.SemaphoreType.DMA((7,)), pltpu.SemaphoreType.DMA],
    )(blocks)


def _dsilu(x, s):
    return s * (1.0 + x * (1.0 - s))


def _log1p(x):
    u = 1.0 + x
    d = u - 1.0
    return jnp.where(d == 0.0, x, jnp.log(u) * (x / jnp.where(d == 0.0, 1.0, d)))


def _softplus_neg(lam):
    return jnp.maximum(-lam, 0.0) + _log1p(jnp.exp(-jnp.abs(lam)))


def _neg_expm1(y):
    poly = -y * (1.0 + y * (0.5 + y * (1.0 / 6.0 + y * (1.0 / 24.0 + y * (1.0 / 120.0)))))
    return jnp.where(y > -0.1, poly, 1.0 - jnp.exp(y))


def _shift_dn(cur, prev, k):
    ext = jnp.concatenate([prev, cur], axis=0)
    return pltpu.roll(ext, k, 0)[HALO:, :]


def _shift_up(cur, nxt, k):
    n = cur.shape[0]
    ext = jnp.concatenate([cur, nxt], axis=0)
    return pltpu.roll(ext, n + HALO - k, 0)[:n, :]


def _scan_fwd(a, b, h_prev):
    ea = jnp.concatenate([jnp.zeros_like(h_prev), a], axis=0)
    eb = jnp.concatenate([h_prev, b], axis=0)
    n = ea.shape[0]
    row = lax.broadcasted_iota(jnp.int32, ea.shape, 0)
    k = 1
    while k < n:
        a_sh = jnp.where(row >= k, pltpu.roll(ea, k, 0), 1.0)
        b_sh = jnp.where(row >= k, pltpu.roll(eb, k, 0), 0.0)
        eb = ea * b_sh + eb
        ea = ea * a_sh
        k *= 2
    return eb[HALO:, :]


def _scan_rev(a_next, g, lam_next):
    ea = jnp.concatenate([a_next, jnp.zeros_like(lam_next)], axis=0)
    eg = jnp.concatenate([g, lam_next], axis=0)
    n = ea.shape[0]
    row = lax.broadcasted_iota(jnp.int32, ea.shape, 0)
    k = 1
    while k < n:
        ok = row < n - k
        a_sh = jnp.where(ok, pltpu.roll(ea, n - k, 0), 1.0)
        g_sh = jnp.where(ok, pltpu.roll(eg, n - k, 0), 0.0)
        eg = eg + ea * g_sh
        ea = ea * a_sh
        k *= 2
    return eg[:g.shape[0], :]


def _rowsum(v):
    return jnp.sum(v, axis=0, keepdims=True)


def _dot(a, b):
    return jnp.dot(a, b, preferred_element_type=F32)


def _dot_nt(a, b):
    return lax.dot_general(a, b, (((1,), (1,)), ((), ())), preferred_element_type=F32)


def _dot_tn(a, b):
    return lax.dot_general(a, b, (((0,), (0,)), ((), ())), preferred_element_type=F32)


class _MixerWeights:
    def __init__(self, caw_ref, sw_ref, sb_ref, lcw_ref, lcb_ref, wa_ref, wx_ref, ba_ref, bx_ref, lam_ref):
        self.caw = [caw_ref[j:j + 1, :] for j in range(3)]
        self.lcw = [lcw_ref[j:j + 1, :] for j in range(4)]
        self.lcb = lcb_ref[...]
        row = lax.broadcasted_iota(jnp.int32, (CHUNK, CHUNK), 0)
        col = lax.broadcasted_iota(jnp.int32, (CHUNK, CHUNK), 1)
        self.tril = col <= row
        self.sw = jnp.where(self.tril, sw_ref[...], 0.0).astype(BF16)
        self.sb = sb_ref[...]
        self.wa = wa_ref[...]
        self.wx = wx_ref[...]
        self.ba = ba_ref[...]
        self.bx = bx_ref[...]
        lam = lam_ref[...]
        self.sp = _softplus_neg(lam)
        self.dsp_dlam = -jax.nn.sigmoid(-lam)


def _mixer_pre_scan(ld, ldp, w):
    t = {}
    a_x, a_c = ld(AX), ld(AC)
    t["a_x"], t["a_c"], t["a_b"], t["a_z"] = a_x, a_c, ld(AB), ld(AZ)
    ca = a_c * a_x
    ca_p = ldp(AC) * ldp(AX)
    t["ca"], t["ca1"], t["ca2"] = ca, _shift_dn(ca, ca_p, 1), _shift_dn(ca, ca_p, 2)
    t["cv"] = w.caw[2] * ca + w.caw[1] * t["ca1"] + w.caw[0] * t["ca2"]
    t["sa"] = jax.nn.sigmoid(t["a_z"])
    t["silu_az"] = t["a_z"] * t["sa"]
    t["y_a"] = t["silu_az"] * t["a_b"] * t["cv"]

    v = ld(SV)
    vc = v - jnp.mean(v, axis=1, keepdims=True)
    t["rstd"] = lax.rsqrt(jnp.mean(vc * vc, axis=1, keepdims=True) + EPS)
    t["vn"] = vc * t["rstd"]
    t["z"] = _dot(w.sw, t["vn"].astype(BF16)) + w.sb
    t["s_u"], t["s_z"] = ld(SU), ld(SZ)
    t["ss"] = jax.nn.sigmoid(t["s_z"])
    t["silu_sz"] = t["s_z"] * t["ss"]
    t["y_s"] = t["silu_sz"] * t["s_u"] * t["z"]

    r_x, r_xp = ld(RX), ldp(RX)
    t["rx"] = [_shift_dn(r_x, r_xp, 3), _shift_dn(r_x, r_xp, 2), _shift_dn(r_x, r_xp, 1), r_x]
    xc = w.lcb + w.lcw[0] * t["rx"][0] + w.lcw[1] * t["rx"][1] + w.lcw[2] * t["rx"][2] + w.lcw[3] * r_x
    t["xc"] = xc
    xcb = xc.astype(BF16)
    t["r"] = jax.nn.sigmoid(_dot(xcb, w.wa) + w.ba)
    t["i"] = jax.nn.sigmoid(_dot(xcb, w.wx) + w.bx)
    la = -LRU_C * t["r"] * w.sp
    t["a"] = jnp.exp(la)
    t["em"] = _neg_expm1(2.0 * la)
    t["mult"] = jnp.sqrt(t["em"])
    t["b"] = t["mult"] * (t["i"] * xc)
    t["r_z"] = ld(RZ)
    t["sr"] = jax.nn.sigmoid(t["r_z"])
    t["silu_rz"] = t["r_z"] * t["sr"]
    t["ga"], t["gs"], t["gr"] = jax.nn.sigmoid(ld(GA)), jax.nn.sigmoid(ld(GS)), jax.nn.sigmoid(ld(GR))
    return t


def _weight_specs(n_cb_axis):
    def at(fn):
        return lambda *g: fn(g[n_cb_axis])
    return [
        pl.BlockSpec((3, LANES), at(lambda cb: (0, cb))),
        pl.BlockSpec((None, CHUNK, CHUNK), at(lambda cb: (cb, 0, 0))),
        pl.BlockSpec((None, CHUNK, LANES), at(lambda cb: (cb, 0, 0))),
        pl.BlockSpec((4, LANES), at(lambda cb: (0, cb))),
        pl.BlockSpec((1, LANES), at(lambda cb: (0, cb))),
        pl.BlockSpec((None, LANES, LANES), at(lambda cb: (cb, 0, 0))),
        pl.BlockSpec((None, LANES, LANES), at(lambda cb: (cb, 0, 0))),
        pl.BlockSpec((1, LANES), at(lambda cb: (0, cb))),
        pl.BlockSpec((1, LANES), at(lambda cb: (0, cb))),
        pl.BlockSpec((1, LANES), at(lambda cb: (0, cb))),
    ]


def _chunk_loaders(p_ref, c):
    r0 = pl.multiple_of(c * CHUNK, CHUNK)
    rp = pl.multiple_of(jnp.maximum(c * CHUNK - HALO, 0), HALO)

    def ld(j):
        return p_ref[j, pl.ds(r0, CHUNK), :].astype(F32)

    def ldp(j):
        return jnp.where(c > 0, p_ref[j, pl.ds(rp, HALO), :].astype(F32), 0.0)

    return r0, rp, ld, ldp


def _mixer_fwd(proj, mw):
    _, nb, s, _ = proj.shape
    n_chunks = s // CHUNK

    def body(p_ref, *refs):
        w = _MixerWeights(*refs[:10])
        merged_ref, hs_ref = refs[10:]

        def chunk(c, h_prev):
            r0, _, ld, ldp = _chunk_loaders(p_ref, c)
            t = _mixer_pre_scan(ld, ldp, w)
            h = _scan_fwd(t["a"], t["b"], h_prev)
            y_r = t["silu_rz"] * h
            merged = t["ga"] * t["y_a"] + t["gs"] * t["y_s"] + t["gr"] * y_r
            merged_ref[pl.ds(r0, CHUNK), :] = merged.astype(BF16)
            hs_ref[pl.ds(r0, CHUNK), :] = h
            return h[CHUNK - HALO:, :]

        lax.fori_loop(0, n_chunks, chunk, jnp.zeros((HALO, LANES), F32))

    slab = pl.BlockSpec((None, s, LANES), lambda cb, b: (b, 0, cb))
    return pl.pallas_call(
        body, name="mixer_fwd", grid=(D // LANES, nb),
        in_specs=[pl.BlockSpec((N_SEG, None, s, LANES), lambda cb, b: (0, b, 0, cb))] + _weight_specs(0),
        out_specs=[slab, slab],
        out_shape=[jax.ShapeDtypeStruct((nb, s, D), BF16), jax.ShapeDtypeStruct((nb, s, D), F32)],
        compiler_params=_params(("arbitrary", "arbitrary")),
    )(proj, *mw)


def _mixer_bwd(proj, dmerged, hs, mw):
    _, nb, s, _ = proj.shape
    n_chunks = s // CHUNK

    def body(p_ref, dm_ref, hs_ref, *refs):
        w = _MixerWeights(*refs[:10])
        dp_ref, g_caw, g_sw, g_sb, g_lcw, g_vec, g_wa, g_wx = refs[10:]

        @pl.when(pl.program_id(1) == 0)
        def _():
            for ref in (g_caw, g_sw, g_sb, g_lcw, g_vec, g_wa, g_wx):
                ref[...] = jnp.zeros_like(ref)

        def chunk(i, carry):
            dcv_n, dxc_n, lam_n, a_n = carry
            c = n_chunks - 1 - i
            r0, rp, ld, ldp = _chunk_loaders(p_ref, c)
            t = _mixer_pre_scan(ld, ldp, w)
            h = hs_ref[pl.ds(r0, CHUNK), :]
            h_p = jnp.where(c > 0, hs_ref[pl.ds(rp, HALO), :], 0.0)
            h_prev = _shift_dn(h, h_p, 1)
            dm = dm_ref[pl.ds(r0, CHUNK), :].astype(F32)
            y_r = t["silu_rz"] * h

            def out(j, val):
                dp_ref[j, pl.ds(r0, CHUNK), :] = val.astype(BF16)

            ga, gs, gr = t["ga"], t["gs"], t["gr"]
            out(GA, dm * t["y_a"] * ga * (1.0 - ga))
            out(GS, dm * t["y_s"] * gs * (1.0 - gs))
            out(GR, dm * y_r * gr * (1.0 - gr))

            dy_a = dm * ga
            out(AZ, dy_a * t["a_b"] * t["cv"] * _dsilu(t["a_z"], t["sa"]))
            out(AB, dy_a * t["silu_az"] * t["cv"])
            dcv = dy_a * t["silu_az"] * t["a_b"]
            dca = w.caw[2] * dcv + w.caw[1] * _shift_up(dcv, dcv_n, 1) + w.caw[0] * _shift_up(dcv, dcv_n, 2)
            out(AC, dca * t["a_x"])
            out(AX, dca * t["a_c"])
            g_caw[2:3, :] += _rowsum(dcv * t["ca"])
            g_caw[1:2, :] += _rowsum(dcv * t["ca1"])
            g_caw[0:1, :] += _rowsum(dcv * t["ca2"])

            dy_s = dm * gs
            out(SZ, dy_s * t["s_u"] * t["z"] * _dsilu(t["s_z"], t["ss"]))
            out(SU, dy_s * t["silu_sz"] * t["z"])
            dz = dy_s * t["silu_sz"] * t["s_u"]
            dzb = dz.astype(BF16)
            g_sb[...] += jnp.broadcast_to(jnp.sum(dz, axis=1, keepdims=True), (CHUNK, LANES))
            g_sw[...] += _dot_nt(dzb, t["vn"].astype(BF16))
            dvn = _dot_tn(w.sw, dzb)
            vn = t["vn"]
            out(SV, t["rstd"] * (dvn - jnp.mean(dvn, axis=1, keepdims=True)
                                 - vn * jnp.mean(dvn * vn, axis=1, keepdims=True)))

            dy_r = dm * gr
            out(RZ, dy_r * h * _dsilu(t["r_z"], t["sr"]))
            lam = _scan_rev(_shift_up(t["a"], a_n, 1), dy_r * t["silu_rz"], lam_n)
            a, r, ig, xc, mult = t["a"], t["r"], t["i"], t["xc"], t["mult"]
            d_i = lam * mult * xc
            d_mult = lam * ig * xc
            dxc = lam * mult * ig
            dla = lam * h_prev * a - d_mult * ((1.0 - t["em"]) / mult)
            g_vec[3:4, :] += _rowsum(dla * (-LRU_C * r)) * w.dsp_dlam
            dpr = (dla * (-LRU_C * w.sp)) * r * (1.0 - r)
            dpi = d_i * ig * (1.0 - ig)
            dprb, dpib, xcb = dpr.astype(BF16), dpi.astype(BF16), xc.astype(BF16)
            g_wa[...] += _dot_tn(xcb, dprb)
            g_wx[...] += _dot_tn(xcb, dpib)
            g_vec[1:2, :] += _rowsum(dpr)
            g_vec[2:3, :] += _rowsum(dpi)
            dxc = dxc + _dot_nt(dprb, w.wa) + _dot_nt(dpib, w.wx)
            g_vec[0:1, :] += _rowsum(dxc)
            out(RX, w.lcw[3] * dxc + w.lcw[2] * _shift_up(dxc, dxc_n, 1)
                + w.lcw[1] * _shift_up(dxc, dxc_n, 2) + w.lcw[0] * _shift_up(dxc, dxc_n, 3))
            for j in range(4):
                g_lcw[j:j + 1, :] += _rowsum(dxc * t["rx"][j])
            return dcv[:HALO, :], dxc[:HALO, :], lam[:HALO, :], a[:HALO, :]

        zero = jnp.zeros((HALO, LANES), F32)
        lax.fori_loop(0, n_chunks, chunk, (zero, zero, zero, zero))

        @pl.when(pl.program_id(1) == nb - 1)
        def _():
            g_sw[...] = jnp.where(w.tril, g_sw[...], 0.0)

    slab = lambda dt: pl.BlockSpec((None, s, LANES), lambda cb, b: (b, 0, cb))
    seg = pl.BlockSpec((N_SEG, None, s, LANES), lambda cb, b: (0, b, 0, cb))
    rows = lambda n: pl.BlockSpec((n, LANES), lambda cb, b: (0, cb))
    sq = pl.BlockSpec((None, LANES, LANES), lambda cb, b: (cb, 0, 0))
    n_cb = D // LANES
    return pl.pallas_call(
        body, name="mixer_bwd", grid=(n_cb, nb),
        in_specs=[seg, slab(BF16), slab(F32)] + _weight_specs(0),
        out_specs=[seg, rows(3), sq, sq, rows(4), rows(8), sq, sq],
        out_shape=[
            jax.ShapeDtypeStruct(proj.shape, BF16),
            jax.ShapeDtypeStruct((3, D), F32),
            jax.ShapeDtypeStruct((n_cb, CHUNK, CHUNK), F32),
            jax.ShapeDtypeStruct((n_cb, CHUNK, LANES), F32),
            jax.ShapeDtypeStruct((4, D), F32),
            jax.ShapeDtypeStruct((8, D), F32),
            jax.ShapeDtypeStruct((n_cb, LANES, LANES), F32),
            jax.ShapeDtypeStruct((n_cb, LANES, LANES), F32),
        ],
        compiler_params=_params(("arbitrary", "arbitrary")),
    )(proj, dmerged, hs, *mw)


def _row_tile(s, want):
    return want if s % want == 0 else s


def _norm_mod(x, gain, shift, scale):
    nb, s, _ = x.shape
    tm = _row_tile(s, 512)

    def body(x_ref, g_ref, sh_ref, sc_ref, h_ref):
        xv = x_ref[...]
        r = lax.rsqrt(jnp.mean(xv * xv, axis=1, keepdims=True) + EPS)
        h_ref[...] = ((xv * r) * g_ref[...] * (1.0 + sc_ref[...]) + sh_ref[...]).astype(BF16)

    tile = pl.BlockSpec((None, tm, D), lambda b, m: (b, m, 0))
    vec = pl.BlockSpec((None, 1, D), lambda b, m: (b, 0, 0))
    return pl.pallas_call(
        body, name="norm_mod", grid=(nb, s // tm),
        in_specs=[tile, pl.BlockSpec((1, D), lambda b, m: (0, 0)), vec, vec],
        out_specs=tile, out_shape=jax.ShapeDtypeStruct(x.shape, BF16),
        compiler_params=_params(("arbitrary", "arbitrary")),
    )(x, gain, shift, scale)


def _in_proj(h, wg):
    nb, s, _ = h.shape

    def body(h_ref, w_ref, o_ref):
        o_ref[...] = _dot(h_ref[...], w_ref[...]).astype(BF16)

    return pl.pallas_call(
        body, name="in_proj", grid=(nb, N_DEV * UNITS_PER_DEV),
        in_specs=[pl.BlockSpec((None, s, D), lambda b, u: (b, 0, 0)),
                  pl.BlockSpec((None, D, UNIT), lambda b, u: (u // UNITS_PER_DEV, 0, u % UNITS_PER_DEV))],
        out_specs=pl.BlockSpec((None, None, s, UNIT), lambda b, u: (u // 2, b, 0, u % 2)),
        out_shape=jax.ShapeDtypeStruct((N_SEG, nb, s, D), BF16),
        compiler_params=_params(("arbitrary", "arbitrary")),
    )(h, wg)


def _out_proj(x, merged, wout, gate):
    nb, s, _ = x.shape
    tm = _row_tile(s, 512)

    def body(x_ref, m_ref, w_ref, g_ref, o_ref):
        o_ref[...] = x_ref[...] + g_ref[...] * _dot(m_ref[...], w_ref[...])

    tile = pl.BlockSpec((None, tm, D), lambda b, m: (b, m, 0))
    return pl.pallas_call(
        body, name="out_proj", grid=(nb, s // tm),
        in_specs=[tile, tile, pl.BlockSpec((D, D), lambda b, m: (0, 0)),
                  pl.BlockSpec((None, 1, D), lambda b, m: (b, 0, 0))],
        out_specs=tile, out_shape=jax.ShapeDtypeStruct(x.shape, F32),
        compiler_params=_params(("arbitrary", "arbitrary")),
    )(x, merged, wout, gate)


def _loss_head(x, gain, target):
    nb, s, _ = x.shape
    tm = _row_tile(s, 512)

    def body(x_ref, g_ref, t_ref, loss_ref, dx_ref, dg_ref):
        first = (pl.program_id(0) == 0) & (pl.program_id(1) == 0)
        last = (pl.program_id(0) == nb - 1) & (pl.program_id(1) == s // tm - 1)

        @pl.when(first)
        def _():
            loss_ref[...] = jnp.zeros_like(loss_ref)
            dg_ref[...] = jnp.zeros_like(dg_ref)

        xv = x_ref[...]
        r = lax.rsqrt(jnp.mean(xv * xv, axis=1, keepdims=True) + EPS)
        xn = xv * r
        g = g_ref[...]
        e = xn * g - t_ref[...]
        loss_ref[...] += _rowsum(e * e) * (0.5 / D)
        dy = e * (1.0 / D)
        dg_ref[...] += _rowsum(dy * xn)
        dxn = dy * g
        dx_ref[...] = r * (dxn - xn * jnp.mean(dxn * xn, axis=1, keepdims=True))

        @pl.when(last)
        def _():
            loss_ref[...] = jnp.broadcast_to(jnp.sum(loss_ref[...], axis=1, keepdims=True), (1, D))

    tile = pl.BlockSpec((None, tm, D), lambda b, m: (b, m, 0))
    vec = pl.BlockSpec((1, D), lambda b, m: (0, 0))
    return pl.pallas_call(
        body, name="loss_head", grid=(nb, s // tm),
        in_specs=[tile, vec, tile], out_specs=[vec, tile, vec],
        out_shape=[jax.ShapeDtypeStruct((1, D), F32), jax.ShapeDtypeStruct(x.shape, F32),
                   jax.ShapeDtypeStruct((1, D), F32)],
        compiler_params=_params(("arbitrary", "arbitrary")),
    )(x, gain, target)


def _out_proj_bwd(dxo, merged, wout, gate):
    nb, s, _ = dxo.shape
    tm = _row_tile(s, 512)

    def body(d_ref, m_ref, w_ref, g_ref, dm_ref, gw_ref, dg_ref):
        @pl.when((pl.program_id(0) == 0) & (pl.program_id(1) == 0))
        def _():
            gw_ref[...] = jnp.zeros_like(gw_ref)

        @pl.when(pl.program_id(1) == 0)
        def _():
            dg_ref[...] = jnp.zeros_like(dg_ref)

        d = d_ref[...]
        m = m_ref[...]
        wv = w_ref[...]
        dg_ref[...] += _rowsum(d * _dot(m, wv))
        dout = (d * g_ref[...]).astype(BF16)
        dm_ref[...] = _dot_nt(dout, wv).astype(BF16)
        gw_ref[...] += _dot_tn(m, dout)

    tile = pl.BlockSpec((None, tm, D), lambda b, m: (b, m, 0))
    vec = pl.BlockSpec((None, 1, D), lambda b, m: (b, 0, 0))
    full = pl.BlockSpec((D, D), lambda b, m: (0, 0))
    return pl.pallas_call(
        body, name="out_proj_bwd", grid=(nb, s // tm),
        in_specs=[tile, tile, full, vec], out_specs=[tile, full, vec],
        out_shape=[jax.ShapeDtypeStruct(dxo.shape, BF16), jax.ShapeDtypeStruct((D, D), F32),
                   jax.ShapeDtypeStruct((nb, 1, D), F32)],
        compiler_params=_params(("arbitrary", "arbitrary")),
    )(dxo, merged, wout, gate)


def _in_proj_bwd_x(dproj, wg, x, dxo, gain, scale):
    _, nb, s, _ = dproj.shape
    tm = _row_tile(s, 512)
    n_u = N_DEV * UNITS_PER_DEV

    def body(dp_ref, w_ref, x_ref, dxo_ref, g_ref, sc_ref, dx_ref, dsh_ref, dsc_ref, dg_ref, acc_ref):
        b, m, u = pl.program_id(0), pl.program_id(1), pl.program_id(2)

        @pl.when(u == 0)
        def _():
            acc_ref[...] = jnp.zeros_like(acc_ref)

        acc_ref[...] += _dot_nt(dp_ref[...], w_ref[...])

        @pl.when(u == n_u - 1)
        def _():
            @pl.when((b == 0) & (m == 0))
            def _():
                dg_ref[...] = jnp.zeros_like(dg_ref)

            @pl.when(m == 0)
            def _():
                dsh_ref[...] = jnp.zeros_like(dsh_ref)
                dsc_ref[...] = jnp.zeros_like(dsc_ref)

            dh = acc_ref[...]
            xv = x_ref[...]
            r = lax.rsqrt(jnp.mean(xv * xv, axis=1, keepdims=True) + EPS)
            xn = xv * r
            g = g_ref[...]
            one_sc = 1.0 + sc_ref[...]
            dsh_ref[...] += _rowsum(dh)
            dsc_ref[...] += _rowsum(dh * (xn * g))
            dg_ref[...] += _rowsum(dh * one_sc * xn)
            dxn = dh * (g * one_sc)
            dx_ref[...] = dxo_ref[...] + r * (dxn - xn * jnp.mean(dxn * xn, axis=1, keepdims=True))

    tile = pl.BlockSpec((None, tm, D), lambda b, m, u: (b, m, 0))
    vec = pl.BlockSpec((None, 1, D), lambda b, m, u: (b, 0, 0))
    one = pl.BlockSpec((1, D), lambda b, m, u: (0, 0))
    return pl.pallas_call(
        body, name="in_proj_bwd_x", grid=(nb, s // tm, n_u),
        in_specs=[pl.BlockSpec((None, None, tm, UNIT), lambda b, m, u: (u // 2, b, m, u % 2)),
                  pl.BlockSpec((None, D, UNIT), lambda b, m, u: (u // UNITS_PER_DEV, 0, u % UNITS_PER_DEV)),
                  tile, tile, one, vec],
        out_specs=[tile, vec, vec, one],
        out_shape=[jax.ShapeDtypeStruct(x.shape, F32), jax.ShapeDtypeStruct((nb, 1, D), F32),
                   jax.ShapeDtypeStruct((nb, 1, D), F32), jax.ShapeDtypeStruct((1, D), F32)],
        scratch_shapes=[pltpu.VMEM((tm, D), F32)],
        compiler_params=_params(("arbitrary", "arbitrary", "arbitrary")),
    )(dproj, wg, x, dxo, gain, scale)


def _in_proj_bwd_w(h, dproj):
    nb, s, _ = h.shape
    tm = _row_tile(s, 1024)
    n_m = s // tm

    def body(h_ref, dp_ref, o_ref, acc_ref):
        b, m = pl.program_id(1), pl.program_id(2)

        @pl.when((b == 0) & (m == 0))
        def _():
            acc_ref[...] = jnp.zeros_like(acc_ref)

        acc_ref[...] += _dot_tn(h_ref[...], dp_ref[...])

        @pl.when((b == nb - 1) & (m == n_m - 1))
        def _():
            o_ref[0] = acc_ref[:, :UNIT].astype(BF16)
            o_ref[1] = acc_ref[:, UNIT:].astype(BF16)

    return pl.pallas_call(
        body, name="in_proj_bwd_w", grid=(N_SEG, nb, n_m),
        in_specs=[pl.BlockSpec((None, tm, D), lambda j, b, m: (b, m, 0)),
                  pl.BlockSpec((None, None, tm, D), lambda j, b, m: (j, b, m, 0))],
        out_specs=pl.BlockSpec((2, D, UNIT), lambda j, b, m: (j, 0, 0)),
        out_shape=jax.ShapeDtypeStruct((2 * N_SEG, D, UNIT), BF16),
        scratch_shapes=[pltpu.VMEM((D, D), F32)],
        compiler_params=_params(("arbitrary", "arbitrary", "arbitrary")),
    )(h, dproj)


def _mod_proj(c_all, w_mod, b_mod_mine):
    nl, _, ncol = w_mod.shape
    nbg = c_all.shape[0]

    def body(c_ref, w_ref, b_ref, o_ref):
        cv = c_ref[...]
        o_ref[...] = jnp.dot(cv * jax.nn.sigmoid(cv), w_ref[...], preferred_element_type=F32,
                             precision=lax.Precision.HIGHEST) + b_ref[...]

    return pl.pallas_call(
        body, name="mod_proj", grid=(nl,),
        in_specs=[pl.BlockSpec((nbg, D), lambda l: (0, 0)), pl.BlockSpec((None, D, ncol), lambda l: (l, 0, 0)),
                  pl.BlockSpec((None, 1, ncol), lambda l: (l, 0, 0))],
        out_specs=pl.BlockSpec((None, nbg, ncol), lambda l: (l, 0, 0)),
        out_shape=jax.ShapeDtypeStruct((nl, nbg, ncol), F32),
        compiler_params=_params(("arbitrary",)),
    )(c_all, w_mod, b_mod_mine)


def _mod_grad(c_all, dmod_all, dmod_mine):
    nl, nbg, ncol = dmod_mine.shape

    def body(c_ref, da_ref, dm_ref, gw_ref, gb_ref):
        cv = c_ref[...]
        gw_ref[...] = lax.dot_general(cv * jax.nn.sigmoid(cv), dm_ref[...], (((0,), (0,)), ((), ())),
                                      preferred_element_type=F32, precision=lax.Precision.HIGHEST)
        gb_ref[...] = _rowsum(da_ref[...])

    return pl.pallas_call(
        body, name="mod_grad", grid=(nl,),
        in_specs=[pl.BlockSpec((nbg, D), lambda l: (0, 0)), pl.BlockSpec((None, nbg, 3 * D), lambda l: (l, 0, 0)),
                  pl.BlockSpec((None, nbg, ncol), lambda l: (l, 0, 0))],
        out_specs=[pl.BlockSpec((None, D, ncol), lambda l: (l, 0, 0)),
                   pl.BlockSpec((None, 1, 3 * D), lambda l: (l, 0, 0))],
        out_shape=[jax.ShapeDtypeStruct((nl, D, ncol), F32), jax.ShapeDtypeStruct((nl, 1, 3 * D), F32)],
        compiler_params=_params(("arbitrary",)),
    )(c_all, dmod_all, dmod_mine)


def _adamw(parts, w, m, v, name):
    n_parts, n_u, n_r, cu = parts.shape
    assert w.shape == (n_r, n_u * cu), (parts.shape, w.shape)
    tr = n_r
    for cand in (512, 256, 128):
        if n_r > cand and n_r % cand == 0:
            tr = cand
            break

    def body(p_ref, w_ref, m_ref, v_ref, g_ref, d_ref, nm_ref, nv_ref):
        g = p_ref[0].astype(F32)
        for k in range(1, n_parts):
            g = g + p_ref[k].astype(F32)
        m2 = ADAM_B1 * m_ref[...] + (1.0 - ADAM_B1) * g
        v2 = ADAM_B2 * v_ref[...] + (1.0 - ADAM_B2) * (g * g)
        m_hat = m2 / (1.0 - ADAM_B1 ** ADAM_STEP)
        v_hat = v2 / (1.0 - ADAM_B2 ** ADAM_STEP)
        g_ref[...] = g
        d_ref[...] = -ADAM_LR * (m_hat / (jnp.sqrt(v_hat) + ADAM_EPS) + ADAM_WD * w_ref[...])
        nm_ref[...] = m2
        nv_ref[...] = v2

    tile = pl.BlockSpec((tr, cu), lambda u, i: (i, u))
    shp = jax.ShapeDtypeStruct(w.shape, F32)
    return pl.pallas_call(
        body, name=name, grid=(n_u, n_r // tr),
        in_specs=[pl.BlockSpec((n_parts, None, tr, cu), lambda u, i: (0, u, i, 0)), tile, tile, tile],
        out_specs=[tile, tile, tile, tile], out_shape=[shp, shp, shp, shp],
        compiler_params=_params(("arbitrary", "arbitrary")),
    )(parts, w, m, v)


def _gathered_cols(g, inner):
    k = len(inner)
    perm = tuple(range(1, k + 1)) + (0, k + 1)
    t = jnp.transpose(g, perm)
    return t.reshape(tuple(inner) + (g.shape[0] * g.shape[-1],))


def _pair_blocks(wh):
    z = jnp.zeros((8, 64, 64), wh.dtype)
    w2 = wh.reshape(8, 2, 64, 64)
    top = jnp.concatenate([w2[:, 0], z], axis=2)
    bot = jnp.concatenate([z, w2[:, 1]], axis=2)
    return jnp.concatenate([top, bot], axis=1).astype(BF16)


def _unpair_blocks(g):
    return jnp.stack([g[:, :64, :64], g[:, 64:, 64:]], axis=1).reshape(16, 64, 64)


FLAT_ROWS = 512


def _pack_flat(arrays):
    flat = jnp.concatenate([a.reshape(-1) for a in arrays])
    n = flat.shape[0]
    per = FLAT_ROWS * LANES
    pad = (-n) % per
    return jnp.pad(flat, (0, pad)).reshape(-1, LANES)


def _unpack_flat(packed, like):
    flat = packed.reshape(-1)
    out, off = [], 0
    for a in like:
        out.append(flat[off:off + a.size].reshape(a.shape))
        off += a.size
    return out


def kernel(x, c, norm_gain, w_mod, b_mod, w_in, w_out, conv_a_w, sgu_w, sgu_b, lru_conv_w, lru_conv_b, lru_wa, lru_ba, lru_wx, lru_bx, lru_lambda, final_gain, loss_target, m_norm_gain, m_w_mod, m_b_mod, m_w_in, m_w_out, m_conv_a_w, m_sgu_w, m_sgu_b, m_lru_conv_w, m_lru_conv_b, m_lru_wa, m_lru_ba, m_lru_wx, m_lru_bx, m_lru_lambda, m_final_gain, v_norm_gain, v_w_mod, v_b_mod, v_w_in, v_w_out, v_conv_a_w, v_sgu_w, v_sgu_b, v_lru_conv_w, v_lru_conv_b, v_lru_wa, v_lru_ba, v_lru_wx, v_lru_bx, v_lru_lambda, v_final_gain):
    nl = w_in.shape[0]
    nb, s, _ = x.shape
    me = _my_index()
    mod_cols = w_mod.shape[2]

    small = jnp.concatenate([c.reshape(-1, LANES), conv_a_w.reshape(-1, LANES), lru_conv_w.reshape(-1, LANES)])
    n_c, n_ca = nb * D // LANES, nl * 3
    n_small = small.shape[0]
    small = jnp.pad(small, ((0, (-n_small) % 8), (0, 0)))
    small_all = _all_gather(small, "gather_small")
    c_all = small_all[:, :n_c].reshape(N_DEV * nb, D)
    conv_a_full = _gathered_cols(small_all[:, n_c:n_c + n_ca].reshape(N_DEV, nl, 3, LANES), (nl, 3))
    lru_conv_full = _gathered_cols(small_all[:, n_c + n_ca:n_small].reshape(N_DEV, nl, 4, LANES), (nl, 4))

    b_mod_mine = lax.dynamic_slice_in_dim(b_mod, me * mod_cols, mod_cols, axis=1)[:, None, :]
    mod_mine = _mod_proj(c_all, w_mod, b_mod_mine)
    mod_all = _all_gather(mod_mine.reshape(nl * N_DEV * nb, mod_cols), "gather_mod")
    mod_full = _gathered_cols(mod_all.reshape(N_DEV, nl, N_DEV * nb, mod_cols), (nl, N_DEV * nb))
    mod_loc = lax.dynamic_slice_in_dim(mod_full, me * nb, nb, axis=1)
    shift, scale, gate = [mod_loc[:, :, j * D:(j + 1) * D][:, :, None, :] for j in range(3)]

    w_in_b = w_in.astype(BF16)
    w_out_b = w_out.astype(BF16)
    wg = [_all_gather(w_in_b[l], "gather_w_in") for l in range(nl)]
    wo = [_all_gather(w_out_b[l], "gather_w_out").reshape(D, D) for l in range(nl)]

    sgu_b_lanes = jnp.broadcast_to(sgu_b[..., None], sgu_b.shape + (LANES,))
    mws = []
    for l in range(nl):
        mws.append((conv_a_full[l], sgu_w[l], sgu_b_lanes[l], lru_conv_full[l], lru_conv_b[l][None, :],
                    _pair_blocks(lru_wa[l]), _pair_blocks(lru_wx[l]), lru_ba[l].reshape(1, D),
                    lru_bx[l].reshape(1, D), lru_lambda[l][None, :]))

    xs, hs_bf, projs, mergeds, states = [], [], [], [], []
    xl = x
    for l in range(nl):
        h = _norm_mod(xl, norm_gain[l][None, :], shift[l], scale[l])
        proj = _in_proj(h, wg[l])
        merged, st = _mixer_fwd(proj, mws[l])
        xs.append(xl), hs_bf.append(h), projs.append(proj), mergeds.append(merged), states.append(st)
        xl = _out_proj(xl, merged, wo[l], gate[l])

    loss_row, dx, g_final = _loss_head(xl, final_gain[None, :], loss_target)
    loss = lax.psum(loss_row[0, 0], ("x", "y", "c"))

    rep_grads = [None] * nl
    g_w_in, g_w_out, g_conv = [None] * nl, [None] * nl, [None] * nl
    dmods = [None] * nl
    for l in reversed(range(nl)):
        dmerged, gw_out, dgate = _out_proj_bwd(dx, mergeds[l], wo[l], gate[l])
        dproj, g_caw, g_sw, g_sb, g_lcw, g_vec, g_wa, g_wx = _mixer_bwd(projs[l], dmerged, states[l], mws[l])
        dx, dshift, dscale, g_gain = _in_proj_bwd_x(dproj, wg[l], xs[l], dx, norm_gain[l][None, :], scale[l])
        gw_in = _in_proj_bwd_w(hs_bf[l], dproj)
        dmods[l] = jnp.concatenate([dshift, dscale, dgate], axis=2)[:, 0, :]

        recv = _all_to_all(gw_in.reshape(N_DEV, UNITS_PER_DEV, D, UNIT), "scatter_w_in")
        g_w_in[l] = _adamw(recv, w_in[l], m_w_in[l], v_w_in[l], "adamw_w_in")
        recv = _all_to_all(gw_out.reshape(N_DEV, 1, D // N_DEV, D), "scatter_w_out")
        g_w_out[l] = _adamw(recv, w_out[l], m_w_out[l], v_w_out[l], "adamw_w_out")
        g_conv[l] = (g_caw, g_lcw)
        rep_grads[l] = dict(
            norm_gain=g_gain[0], sgu_w=g_sw, sgu_b=g_sb[:, :, 0], lru_conv_b=g_vec[0],
            lru_wa=_unpair_blocks(g_wa), lru_ba=g_vec[1].reshape(16, 64), lru_wx=_unpair_blocks(g_wx),
            lru_bx=g_vec[2].reshape(16, 64), lru_lambda=g_vec[3])

    dmod_loc = jnp.stack(dmods)
    dmod_g = _all_gather(dmod_loc.reshape(nl * nb, 3 * D), "gather_dmod")
    dmod_all = jnp.transpose(dmod_g.reshape(N_DEV, nl, nb, 3 * D), (1, 0, 2, 3)).reshape(nl, N_DEV * nb, 3 * D)
    dmod_mine = lax.dynamic_slice_in_dim(dmod_all, me * mod_cols, mod_cols, axis=2)
    gw_mod, gb_mod = _mod_grad(c_all, dmod_all, dmod_mine)
    res_w_mod = _adamw(gw_mod.reshape(1, 1, nl * D, mod_cols), w_mod.reshape(nl * D, mod_cols),
                       m_w_mod.reshape(nl * D, mod_cols), v_w_mod.reshape(nl * D, mod_cols), "adamw_w_mod")
    res_w_mod = [a.reshape(nl, D, mod_cols) for a in res_w_mod]
    res_b_mod = _adamw(gb_mod.reshape(1, 1, nl, 3 * D), b_mod, m_b_mod, v_b_mod, "adamw_b_mod")

    conv_parts = jnp.concatenate(
        [jnp.stack([g_conv[l][0] for l in range(nl)]).reshape(nl * 3, N_DEV, LANES),
         jnp.stack([g_conv[l][1] for l in range(nl)]).reshape(nl * 4, N_DEV, LANES)], axis=0)
    conv_parts = jnp.transpose(conv_parts, (1, 0, 2))[:, None]
    conv_recv = _all_to_all(conv_parts, "scatter_conv")
    cat = lambda a, b: jnp.concatenate([a.reshape(nl * 3, LANES), b.reshape(nl * 4, LANES)], axis=0)
    res_conv = _adamw(conv_recv, cat(conv_a_w, lru_conv_w), cat(m_conv_a_w, m_lru_conv_w),
                      cat(v_conv_a_w, v_lru_conv_w), "adamw_conv")
    res_conv_a = [a[:nl * 3].reshape(nl, 3, LANES) for a in res_conv]
    res_lru_conv = [a[nl * 3:].reshape(nl, 4, LANES) for a in res_conv]

    rep_names = ["norm_gain", "sgu_w", "sgu_b", "lru_conv_b", "lru_wa", "lru_ba", "lru_wx", "lru_bx", "lru_lambda"]
    rep_w = dict(norm_gain=norm_gain, sgu_w=sgu_w, sgu_b=sgu_b, lru_conv_b=lru_conv_b, lru_wa=lru_wa, lru_ba=lru_ba,
                 lru_wx=lru_wx, lru_bx=lru_bx, lru_lambda=lru_lambda)
    rep_m = dict(norm_gain=m_norm_gain, sgu_w=m_sgu_w, sgu_b=m_sgu_b, lru_conv_b=m_lru_conv_b, lru_wa=m_lru_wa,
                 lru_ba=m_lru_ba, lru_wx=m_lru_wx, lru_bx=m_lru_bx, lru_lambda=m_lru_lambda)
    rep_v = dict(norm_gain=v_norm_gain, sgu_w=v_sgu_w, sgu_b=v_sgu_b, lru_conv_b=v_lru_conv_b, lru_wa=v_lru_wa,
                 lru_ba=v_lru_ba, lru_wx=v_lru_wx, lru_bx=v_lru_bx, lru_lambda=v_lru_lambda)
    g_list = [jnp.stack([rep_grads[l][n] for l in range(nl)]) for n in rep_names] + [g_final[0]]
    w_list = [rep_w[n] for n in rep_names] + [final_gain]
    m_list = [rep_m[n] for n in rep_names] + [m_final_gain]
    v_list = [rep_v[n] for n in rep_names] + [v_final_gain]
    g_packed = _pack_flat(g_list)
    g_all = _all_gather(g_packed, "gather_rep_grads")[:, None]
    res_rep = _adamw(g_all, _pack_flat(w_list), _pack_flat(m_list), _pack_flat(v_list), "adamw_rep")
    res_rep = [dict(zip(rep_names + ["final_gain"], _unpack_flat(a, w_list))) for a in res_rep]

    def stacked(per_layer, k):
        return jnp.stack([per_layer[l][k] for l in range(nl)])

    def leaf(k, name):
        if name == "w_mod":
            return res_w_mod[k]
        if name == "b_mod":
            return res_b_mod[k]
        if name == "w_in":
            return stacked(g_w_in, k)
        if name == "w_out":
            return stacked(g_w_out, k)
        if name == "conv_a_w":
            return res_conv_a[k]
        if name == "lru_conv_w":
            return res_lru_conv[k]
        return res_rep[k][name]

    order = ["norm_gain", "w_mod", "b_mod", "w_in", "w_out", "conv_a_w", "sgu_w", "sgu_b", "lru_conv_w",
             "lru_conv_b", "lru_wa", "lru_ba", "lru_wx", "lru_bx", "lru_lambda", "final_gain"]
    outs = [loss, dx]
    for k in range(4):
        outs += [leaf(k, n) for n in order]
    return tuple(outs)
```

```python
import functools

import jax
import jax.numpy as jnp
from jax import lax
from jax.experimental import pallas as pl
from jax.experimental.pallas import tpu as pltpu

F32 = jnp.float32
BF16 = jnp.bfloat16

D = 1024
N_DEV = 8
N_SEG = 12
LANES = 128
CHUNK = 128
HALO = 16
UNIT = 512
UNITS_PER_DEV = 3
EPS = 1e-6
LRU_C = 8.0
ADAM_LR, ADAM_B1, ADAM_B2, ADAM_EPS, ADAM_WD, ADAM_STEP = 0.001, 0.9, 0.999, 1e-08, 0.01, 10
VMEM_LIMIT = 56 * 1024 * 1024

AX, AB, AC, AZ, SU, SV, SZ, RX, RZ, GA, GS, GR = range(N_SEG)
MESH = pl.DeviceIdType.MESH


def _params(sem=None):
    return pltpu.CompilerParams(dimension_semantics=sem, vmem_limit_bytes=VMEM_LIMIT)


def _my_index():
    return 4 * lax.axis_index("x") + 2 * lax.axis_index("y") + lax.axis_index("c")


def _all_gather(block, name):
    def body(x_ref, out_ref, send_sems, recv_sems, local_sem):
        x, y, c = lax.axis_index("x"), lax.axis_index("y"), lax.axis_index("c")
        me, sibling = (x, y, c), (x, y, 1 - c)
        chips = [(1 - x, y), (x, 1 - y), (1 - x, 1 - y)]

        def rows(px, py, pc):
            return out_ref.at[4 * px + 2 * py + pc]

        def copy(k, blk, to, src=None):
            return pltpu.make_async_remote_copy(
                src_ref=rows(*blk) if src is None else src, dst_ref=rows(*blk),
                send_sem=send_sems.at[k], recv_sem=recv_sems.at[k], device_id=to, device_id_type=MESH)

        mine = pltpu.make_async_copy(x_ref, rows(*me), local_sem)
        mine.start()
        first = [copy(0, me, sibling, src=x_ref)]
        first += [copy(1 + j, me, (*chip, c), src=x_ref) for j, chip in enumerate(chips)]
        for cp in first:
            cp.start()
        passed = [copy(4 + j, (*chip, c), sibling) for j, chip in enumerate(chips)]
        for j, chip in enumerate(chips):
            copy(1 + j, (*chip, c), me).wait_recv()
            passed[j].start()
        copy(0, sibling, me).wait_recv()
        for j, chip in enumerate(chips):
            copy(4 + j, (*chip, 1 - c), me).wait_recv()
        for cp in first + passed:
            cp.wait_send()
        mine.wait()

    return pl.pallas_call(
        body, name=name,
        out_shape=jax.ShapeDtypeStruct((N_DEV,) + block.shape, block.dtype),
        in_specs=[pl.BlockSpec(memory_space=pl.ANY)],
        out_specs=pl.BlockSpec(memory_space=pl.ANY),
        scratch_shapes=[pltpu.SemaphoreType.DMA((7,)), pltpu.SemaphoreType.DMA((7,)), pltpu.SemaphoreType.DMA],
    )(block)


def _all_to_all(blocks, name):
    def body(x_ref, out_ref, send_sems, recv_sems, local_sem):
        x, y, c = lax.axis_index("x"), lax.axis_index("y"), lax.axis_index("c")
        my = 4 * x + 2 * y + c
        mine = pltpu.make_async_copy(x_ref.at[my], out_ref.at[my], local_sem)
        mine.start()
        peers = []
        for r in range(1, N_DEV):
            px = 1 - x if r & 4 else x
            py = 1 - y if r & 2 else y
            pc = 1 - c if r & 1 else c
            peers.append((r - 1, 4 * px + 2 * py + pc, (px, py, pc)))

        def copy(k, src_slot, dst_slot, to):
            return pltpu.make_async_remote_copy(
                src_ref=x_ref.at[src_slot], dst_ref=out_ref.at[dst_slot],
                send_sem=send_sems.at[k], recv_sem=recv_sems.at[k], device_id=to, device_id_type=MESH)

        sends = [copy(k, pid, my, to) for k, pid, to in peers]
        for cp in sends:
            cp.start()
        for k, pid, to in peers:
            copy(k, pid, pid, to).wait_recv()
        for cp in sends:
            cp.wait_send()
        mine.wait()

    return pl.pallas_call(
        body, name=name,
        out_shape=jax.ShapeDtypeStruct(blocks.shape, blocks.dtype),
        in_specs=[pl.BlockSpec(memory_space=pl.ANY)],
        out_specs=pl.BlockSpec(memory_space=pl.ANY),
        scratch_shapes=[pltpu.SemaphoreType.DMA((7,)), pltpu.SemaphoreType.DMA((7,)), pltpu.SemaphoreType.DMA],
    )(blocks)


_HBM = pl.BlockSpec(memory_space=pltpu.HBM)
_SEM = pl.BlockSpec(memory_space=pltpu.SEMAPHORE)
_EFFECT = pltpu.SideEffectType.DATAFLOW_SIDE_EFFECTING


def _peers_all(x, y, c):
    out = []
    for r in range(1, N_DEV):
        px = 1 - x if r & 4 else x
        py = 1 - y if r & 2 else y
        pc = 1 - c if r & 1 else c
        out.append((r - 1, 4 * px + 2 * py + pc, (px, py, pc)))
    return out


def _peers_same_core(x, y, c):
    return [(k, 4 * px + 2 * py + c, (px, py, c))
            for k, (px, py) in enumerate([(1 - x, y), (x, 1 - y), (1 - x, 1 - y)])]


def _split_start(src, peers_fn, scatter, name):
    blk = src.shape[1:] if scatter else src.shape
    land_shape = (N_DEV,) + tuple(blk)
    n = len(peers_fn(0, 0, 0))

    def body(x_ref, land_ref, send_sems, recv_sems, x_thru, land_thru, token):
        x, y, c = lax.axis_index("x"), lax.axis_index("y"), lax.axis_index("c")
        my = 4 * x + 2 * y + c
        for k, pid, to in peers_fn(x, y, c):
            pltpu.make_async_remote_copy(
                src_ref=x_ref.at[pid] if scatter else x_ref, dst_ref=land_ref.at[my],
                send_sem=send_sems.at[k], recv_sem=recv_sems.at[k], device_id=to, device_id_type=MESH).start()
        token[...] = jnp.zeros_like(token)

    return pl.pallas_call(
        body, name=name,
        out_shape=(pltpu.SemaphoreType.DMA((n,)), pltpu.SemaphoreType.DMA((n,)),
                   pltpu.HBM(src.shape, src.dtype), pltpu.HBM(land_shape, src.dtype),
                   jax.ShapeDtypeStruct((8, LANES), F32)),
        in_specs=(_HBM, _HBM),
        out_specs=(_SEM, _SEM, _HBM, _HBM, pl.BlockSpec(memory_space=pltpu.VMEM)),
        input_output_aliases={0: 2, 1: 3},
        compiler_params=pltpu.CompilerParams(has_side_effects=_EFFECT),
    )(pltpu.with_memory_space_constraint(src, pltpu.HBM),
      pltpu.with_memory_space_constraint(lax.empty(land_shape, src.dtype), pltpu.HBM))


def _split_wait(handles, after, peers_fn, scatter, own, name):
    send_sems, recv_sems, src_thru, land_thru, _ = handles

    def body(x_ref, land_ref, send_sems, recv_sems, after_ref, x_dead, got_ref, *local_sem):
        x, y, c = lax.axis_index("x"), lax.axis_index("y"), lax.axis_index("c")
        if own:
            my = 4 * x + 2 * y + c
            mine = pltpu.make_async_copy(x_ref.at[my] if scatter else x_ref, land_ref.at[my], local_sem[0])
            mine.start()
        for k, pid, to in peers_fn(x, y, c):
            cp = pltpu.make_async_remote_copy(
                src_ref=x_ref.at[pid] if scatter else x_ref, dst_ref=land_ref.at[pid],
                send_sem=send_sems.at[k], recv_sem=recv_sems.at[k], device_id=to, device_id_type=MESH)
            cp.wait_send()
            cp.wait_recv()
        if own:
            mine.wait()

    return pl.pallas_call(
        body, name=name,
        out_shape=(pltpu.HBM(src_thru.shape, src_thru.dtype), pltpu.HBM(land_thru.shape, land_thru.dtype)),
        in_specs=(_HBM, _HBM, _SEM, _SEM, pl.BlockSpec(memory_space=pl.ANY)),
        out_specs=(_HBM, _HBM),
        input_output_aliases={0: 0, 1: 1},
        scratch_shapes=[pltpu.SemaphoreType.DMA] if own else [],
        compiler_params=pltpu.CompilerParams(has_side_effects=_EFFECT),
    )(src_thru, land_thru, send_sems, recv_sems, after)


def _gather_finish(block, land, name):
    def body(x_ref, land_ref, out_ref, send_sems, recv_sems, local_sem):
        x, y, c = lax.axis_index("x"), lax.axis_index("y"), lax.axis_index("c")
        my, sib_id, sibling = 4 * x + 2 * y + c, 4 * x + 2 * y + 1 - c, (x, y, 1 - c)
        mine = pltpu.make_async_copy(x_ref, out_ref.at[my], local_sem)
        mine.start()

        def copy(k, slot, src=None):
            return pltpu.make_async_remote_copy(
                src_ref=land_ref.at[slot] if src is None else src, dst_ref=out_ref.at[slot],
                send_sem=send_sems.at[k], recv_sem=recv_sems.at[k], device_id=sibling, device_id_type=MESH)

        chips = _peers_same_core(x, y, c)
        sends = [copy(0, my, src=x_ref)] + [copy(1 + k, pid) for k, pid, _ in chips]
        for cp in sends:
            cp.start()
        copy(0, sib_id).wait_recv()
        for k, pid, _ in chips:
            copy(1 + k, pid + 1 - 2 * c).wait_recv()
        for cp in sends:
            cp.wait_send()
        mine.wait()

    return pl.pallas_call(
        body, name=name,
        out_shape=jax.ShapeDtypeStruct(land.shape, land.dtype),
        in_specs=[pl.BlockSpec(memory_space=pl.ANY), pl.BlockSpec(memory_space=pl.ANY)],
        out_specs=pl.BlockSpec(memory_space=pl.ANY),
        input_output_aliases={1: 0},
        scratch_shapes=[pltpu.SemaphoreType.DMA((4,)), pltpu.SemaphoreType.DMA((4,)), pltpu.SemaphoreType.DMA],
    )(block, land)


def _after(v, handles):
    return v + handles[4][0, 0]


def _dsilu(x, s):
    return s * (1.0 + x * (1.0 - s))


def _log1p(x):
    u = 1.0 + x
    d = u - 1.0
    return jnp.where(d == 0.0, x, jnp.log(u) * (x / jnp.where(d == 0.0, 1.0, d)))


def _softplus_neg(lam):
    return jnp.maximum(-lam, 0.0) + _log1p(jnp.exp(-jnp.abs(lam)))


def _neg_expm1(y):
    poly = -y * (1.0 + y * (0.5 + y * (1.0 / 6.0 + y * (1.0 / 24.0 + y * (1.0 / 120.0)))))
    return jnp.where(y > -0.1, poly, 1.0 - jnp.exp(y))


def _shift_dn(cur, prev, k):
    ext = jnp.concatenate([prev, cur], axis=0)
    return pltpu.roll(ext, k, 0)[HALO:, :]


def _shift_up(cur, nxt, k):
    n = cur.shape[0]
    ext = jnp.concatenate([cur, nxt], axis=0)
    return pltpu.roll(ext, n + HALO - k, 0)[:n, :]


def _scan_fwd(a, b, h_prev):
    ea = jnp.concatenate([jnp.zeros_like(h_prev), a], axis=0)
    eb = jnp.concatenate([h_prev, b], axis=0)
    n = ea.shape[0]
    row = lax.broadcasted_iota(jnp.int32, ea.shape, 0)
    k = 1
    while k < n:
        a_sh = jnp.where(row >= k, pltpu.roll(ea, k, 0), 1.0)
        b_sh = jnp.where(row >= k, pltpu.roll(eb, k, 0), 0.0)
        eb = ea * b_sh + eb
        ea = ea * a_sh
        k *= 2
    return eb[HALO:, :]


def _scan_rev(a_next, g, lam_next):
    ea = jnp.concatenate([a_next, jnp.zeros_like(lam_next)], axis=0)
    eg = jnp.concatenate([g, lam_next], axis=0)
    n = ea.shape[0]
    row = lax.broadcasted_iota(jnp.int32, ea.shape, 0)
    k = 1
    while k < n:
        ok = row < n - k
        a_sh = jnp.where(ok, pltpu.roll(ea, n - k, 0), 1.0)
        g_sh = jnp.where(ok, pltpu.roll(eg, n - k, 0), 0.0)
        eg = eg + ea * g_sh
        ea = ea * a_sh
        k *= 2
    return eg[:g.shape[0], :]


def _rowsum(v):
    return jnp.sum(v, axis=0, keepdims=True)


def _dot(a, b):
    return jnp.dot(a, b, preferred_element_type=F32)


def _dot_nt(a, b):
    return lax.dot_general(a, b, (((1,), (1,)), ((), ())), preferred_element_type=F32)


def _dot_tn(a, b):
    return lax.dot_general(a, b, (((0,), (0,)), ((), ())), preferred_element_type=F32)


class _MixerWeights:
    def __init__(self, caw_ref, sw_ref, sb_ref, lcw_ref, lcb_ref, wa_ref, wx_ref, ba_ref, bx_ref, lam_ref):
        self.caw = [caw_ref[j:j + 1, :] for j in range(3)]
        self.lcw = [lcw_ref[j:j + 1, :] for j in range(4)]
        self.lcb = lcb_ref[...]
        row = lax.broadcasted_iota(jnp.int32, (CHUNK, CHUNK), 0)
        col = lax.broadcasted_iota(jnp.int32, (CHUNK, CHUNK), 1)
        self.tril = col <= row
        self.sw = jnp.where(self.tril, sw_ref[...], 0.0).astype(BF16)
        self.sb = sb_ref[...]
        self.wa = wa_ref[...]
        self.wx = wx_ref[...]
        self.ba = ba_ref[...]
        self.bx = bx_ref[...]
        lam = lam_ref[...]
        self.sp = _softplus_neg(lam)
        self.dsp_dlam = -jax.nn.sigmoid(-lam)


def _mixer_pre_scan(ld, ldp, w):
    t = {}
    a_x, a_c = ld(AX), ld(AC)
    t["a_x"], t["a_c"], t["a_b"], t["a_z"] = a_x, a_c, ld(AB), ld(AZ)
    ca = a_c * a_x
    ca_p = ldp(AC) * ldp(AX)
    t["ca"], t["ca1"], t["ca2"] = ca, _shift_dn(ca, ca_p, 1), _shift_dn(ca, ca_p, 2)
    t["cv"] = w.caw[2] * ca + w.caw[1] * t["ca1"] + w.caw[0] * t["ca2"]
    t["sa"] = jax.nn.sigmoid(t["a_z"])
    t["silu_az"] = t["a_z"] * t["sa"]
    t["y_a"] = t["silu_az"] * t["a_b"] * t["cv"]

    v = ld(SV)
    vc = v - jnp.mean(v, axis=1, keepdims=True)
    t["rstd"] = lax.rsqrt(jnp.mean(vc * vc, axis=1, keepdims=True) + EPS)
    t["vn"] = vc * t["rstd"]
    t["z"] = _dot(w.sw, t["vn"].astype(BF16)) + w.sb
    t["s_u"], t["s_z"] = ld(SU), ld(SZ)
    t["ss"] = jax.nn.sigmoid(t["s_z"])
    t["silu_sz"] = t["s_z"] * t["ss"]
    t["y_s"] = t["silu_sz"] * t["s_u"] * t["z"]

    r_x, r_xp = ld(RX), ldp(RX)
    t["rx"] = [_shift_dn(r_x, r_xp, 3), _shift_dn(r_x, r_xp, 2), _shift_dn(r_x, r_xp, 1), r_x]
    xc = w.lcb + w.lcw[0] * t["rx"][0] + w.lcw[1] * t["rx"][1] + w.lcw[2] * t["rx"][2] + w.lcw[3] * r_x
    t["xc"] = xc
    xcb = xc.astype(BF16)
    t["r"] = jax.nn.sigmoid(_dot(xcb, w.wa) + w.ba)
    t["i"] = jax.nn.sigmoid(_dot(xcb, w.wx) + w.bx)
    la = -LRU_C * t["r"] * w.sp
    t["a"] = jnp.exp(la)
    t["em"] = _neg_expm1(2.0 * la)
    t["mult"] = jnp.sqrt(t["em"])
    t["b"] = t["mult"] * (t["i"] * xc)
    t["r_z"] = ld(RZ)
    t["sr"] = jax.nn.sigmoid(t["r_z"])
    t["silu_rz"] = t["r_z"] * t["sr"]
    t["ga"], t["gs"], t["gr"] = jax.nn.sigmoid(ld(GA)), jax.nn.sigmoid(ld(GS)), jax.nn.sigmoid(ld(GR))
    return t


def _weight_specs(n_cb_axis):
    def at(fn):
        return lambda *g: fn(g[n_cb_axis])
    return [
        pl.BlockSpec((3, LANES), at(lambda cb: (0, cb))),
        pl.BlockSpec((None, CHUNK, CHUNK), at(lambda cb: (cb, 0, 0))),
        pl.BlockSpec((None, CHUNK, LANES), at(lambda cb: (cb, 0, 0))),
        pl.BlockSpec((4, LANES), at(lambda cb: (0, cb))),
        pl.BlockSpec((1, LANES), at(lambda cb: (0, cb))),
        pl.BlockSpec((None, LANES, LANES), at(lambda cb: (cb, 0, 0))),
        pl.BlockSpec((None, LANES, LANES), at(lambda cb: (cb, 0, 0))),
        pl.BlockSpec((1, LANES), at(lambda cb: (0, cb))),
        pl.BlockSpec((1, LANES), at(lambda cb: (0, cb))),
        pl.BlockSpec((1, LANES), at(lambda cb: (0, cb))),
    ]


def _chunk_loaders(p_ref, c):
    r0 = pl.multiple_of(c * CHUNK, CHUNK)
    rp = pl.multiple_of(jnp.maximum(c * CHUNK - HALO, 0), HALO)

    def ld(j):
        return p_ref[j, pl.ds(r0, CHUNK), :].astype(F32)

    def ldp(j):
        return jnp.where(c > 0, p_ref[j, pl.ds(rp, HALO), :].astype(F32), 0.0)

    return r0, rp, ld, ldp


def _mixer_fwd(proj, mw):
    _, nb, s, _ = proj.shape
    n_chunks = s // CHUNK

    def body(p_ref, *refs):
        w = _MixerWeights(*refs[:10])
        merged_ref, hs_ref = refs[10:]

        def chunk(c, h_prev):
            r0, _, ld, ldp = _chunk_loaders(p_ref, c)
            t = _mixer_pre_scan(ld, ldp, w)
            h = _scan_fwd(t["a"], t["b"], h_prev)
            y_r = t["silu_rz"] * h
            merged = t["ga"] * t["y_a"] + t["gs"] * t["y_s"] + t["gr"] * y_r
            merged_ref[pl.ds(r0, CHUNK), :] = merged.astype(BF16)
            hs_ref[pl.ds(r0, CHUNK), :] = h
            return h[CHUNK - HALO:, :]

        lax.fori_loop(0, n_chunks, chunk, jnp.zeros((HALO, LANES), F32))

    slab = pl.BlockSpec((None, s, LANES), lambda cb, b: (b, 0, cb))
    return pl.pallas_call(
        body, name="mixer_fwd", grid=(D // LANES, nb),
        in_specs=[pl.BlockSpec((N_SEG, None, s, LANES), lambda cb, b: (0, b, 0, cb))] + _weight_specs(0),
        out_specs=[slab, slab],
        out_shape=[jax.ShapeDtypeStruct((nb, s, D), BF16), jax.ShapeDtypeStruct((nb, s, D), F32)],
        compiler_params=_params(("arbitrary", "arbitrary")),
    )(proj, *mw)


def _mixer_bwd(proj, dmerged, hs, mw):
    _, nb, s, _ = proj.shape
    n_chunks = s // CHUNK

    def body(p_ref, dm_ref, hs_ref, *refs):
        w = _MixerWeights(*refs[:10])
        dp_ref, g_caw, g_sw, g_sb, g_lcw, g_vec, g_wa, g_wx = refs[10:]

        @pl.when(pl.program_id(1) == 0)
        def _():
            for ref in (g_caw, g_sw, g_sb, g_lcw, g_vec, g_wa, g_wx):
                ref[...] = jnp.zeros_like(ref)

        def chunk(i, carry):
            dcv_n, dxc_n, lam_n, a_n = carry
            c = n_chunks - 1 - i
            r0, rp, ld, ldp = _chunk_loaders(p_ref, c)
            t = _mixer_pre_scan(ld, ldp, w)
            h = hs_ref[pl.ds(r0, CHUNK), :]
            h_p = jnp.where(c > 0, hs_ref[pl.ds(rp, HALO), :], 0.0)
            h_prev = _shift_dn(h, h_p, 1)
            dm = dm_ref[pl.ds(r0, CHUNK), :].astype(F32)
            y_r = t["silu_rz"] * h

            def out(j, val):
                dp_ref[j, pl.ds(r0, CHUNK), :] = val.astype(BF16)

            ga, gs, gr = t["ga"], t["gs"], t["gr"]
            out(GA, dm * t["y_a"] * ga * (1.0 - ga))
            out(GS, dm * t["y_s"] * gs * (1.0 - gs))
            out(GR, dm * y_r * gr * (1.0 - gr))

            dy_a = dm * ga
            out(AZ, dy_a * t["a_b"] * t["cv"] * _dsilu(t["a_z"], t["sa"]))
            out(AB, dy_a * t["silu_az"] * t["cv"])
            dcv = dy_a * t["silu_az"] * t["a_b"]
            dca = w.caw[2] * dcv + w.caw[1] * _shift_up(dcv, dcv_n, 1) + w.caw[0] * _shift_up(dcv, dcv_n, 2)
            out(AC, dca * t["a_x"])
            out(AX, dca * t["a_c"])
            g_caw[2:3, :] += _rowsum(dcv * t["ca"])
            g_caw[1:2, :] += _rowsum(dcv * t["ca1"])
            g_caw[0:1, :] += _rowsum(dcv * t["ca2"])

            dy_s = dm * gs
            out(SZ, dy_s * t["s_u"] * t["z"] * _dsilu(t["s_z"], t["ss"]))
            out(SU, dy_s * t["silu_sz"] * t["z"])
            dz = dy_s * t["silu_sz"] * t["s_u"]
            dzb = dz.astype(BF16)
            g_sb[...] += jnp.broadcast_to(jnp.sum(dz, axis=1, keepdims=True), (CHUNK, LANES))
            g_sw[...] += _dot_nt(dzb, t["vn"].astype(BF16))
            dvn = _dot_tn(w.sw, dzb)
            vn = t["vn"]
            out(SV, t["rstd"] * (dvn - jnp.mean(dvn, axis=1, keepdims=True)
                                 - vn * jnp.mean(dvn * vn, axis=1, keepdims=True)))

            dy_r = dm * gr
            out(RZ, dy_r * h * _dsilu(t["r_z"], t["sr"]))
            lam = _scan_rev(_shift_up(t["a"], a_n, 1), dy_r * t["silu_rz"], lam_n)
            a, r, ig, xc, mult = t["a"], t["r"], t["i"], t["xc"], t["mult"]
            d_i = lam * mult * xc
            d_mult = lam * ig * xc
            dxc = lam * mult * ig
            dla = lam * h_prev * a - d_mult * ((1.0 - t["em"]) / mult)
            g_vec[3:4, :] += _rowsum(dla * (-LRU_C * r)) * w.dsp_dlam
            dpr = (dla * (-LRU_C * w.sp)) * r * (1.0 - r)
            dpi = d_i * ig * (1.0 - ig)
            dprb, dpib, xcb = dpr.astype(BF16), dpi.astype(BF16), xc.astype(BF16)
            g_wa[...] += _dot_tn(xcb, dprb)
            g_wx[...] += _dot_tn(xcb, dpib)
            g_vec[1:2, :] += _rowsum(dpr)
            g_vec[2:3, :] += _rowsum(dpi)
            dxc = dxc + _dot_nt(dprb, w.wa) + _dot_nt(dpib, w.wx)
            g_vec[0:1, :] += _rowsum(dxc)
            out(RX, w.lcw[3] * dxc + w.lcw[2] * _shift_up(dxc, dxc_n, 1)
                + w.lcw[1] * _shift_up(dxc, dxc_n, 2) + w.lcw[0] * _shift_up(dxc, dxc_n, 3))
            for j in range(4):
                g_lcw[j:j + 1, :] += _rowsum(dxc * t["rx"][j])
            return dcv[:HALO, :], dxc[:HALO, :], lam[:HALO, :], a[:HALO, :]

        zero = jnp.zeros((HALO, LANES), F32)
        lax.fori_loop(0, n_chunks, chunk, (zero, zero, zero, zero))

        @pl.when(pl.program_id(1) == nb - 1)
        def _():
            g_sw[...] = jnp.where(w.tril, g_sw[...], 0.0)

    slab = lambda dt: pl.BlockSpec((None, s, LANES), lambda cb, b: (b, 0, cb))
    seg = pl.BlockSpec((N_SEG, None, s, LANES), lambda cb, b: (0, b, 0, cb))
    rows = lambda n: pl.BlockSpec((n, LANES), lambda cb, b: (0, cb))
    sq = pl.BlockSpec((None, LANES, LANES), lambda cb, b: (cb, 0, 0))
    n_cb = D // LANES
    return pl.pallas_call(
        body, name="mixer_bwd", grid=(n_cb, nb),
        in_specs=[seg, slab(BF16), slab(F32)] + _weight_specs(0),
        out_specs=[seg, rows(3), sq, sq, rows(4), rows(8), sq, sq],
        out_shape=[
            jax.ShapeDtypeStruct(proj.shape, BF16),
            jax.ShapeDtypeStruct((3, D), F32),
            jax.ShapeDtypeStruct((n_cb, CHUNK, CHUNK), F32),
            jax.ShapeDtypeStruct((n_cb, CHUNK, LANES), F32),
            jax.ShapeDtypeStruct((4, D), F32),
            jax.ShapeDtypeStruct((8, D), F32),
            jax.ShapeDtypeStruct((n_cb, LANES, LANES), F32),
            jax.ShapeDtypeStruct((n_cb, LANES, LANES), F32),
        ],
        compiler_params=_params(("arbitrary", "arbitrary")),
    )(proj, dmerged, hs, *mw)


def _row_tile(s, want):
    return want if s % want == 0 else s


def _norm_mod(x, gain, shift, scale):
    nb, s, _ = x.shape
    tm = _row_tile(s, 512)

    def body(x_ref, g_ref, sh_ref, sc_ref, h_ref):
        xv = x_ref[...]
        r = lax.rsqrt(jnp.mean(xv * xv, axis=1, keepdims=True) + EPS)
        h_ref[...] = ((xv * r) * g_ref[...] * (1.0 + sc_ref[...]) + sh_ref[...]).astype(BF16)

    tile = pl.BlockSpec((None, tm, D), lambda b, m: (b, m, 0))
    vec = pl.BlockSpec((None, 1, D), lambda b, m: (b, 0, 0))
    return pl.pallas_call(
        body, name="norm_mod", grid=(nb, s // tm),
        in_specs=[tile, pl.BlockSpec((1, D), lambda b, m: (0, 0)), vec, vec],
        out_specs=tile, out_shape=jax.ShapeDtypeStruct(x.shape, BF16),
        compiler_params=_params(("arbitrary", "arbitrary")),
    )(x, gain, shift, scale)


def _in_proj(h, wg):
    nb, s, _ = h.shape

    def body(h_ref, w_ref, o_ref):
        o_ref[...] = _dot(h_ref[...], w_ref[...]).astype(BF16)

    return pl.pallas_call(
        body, name="in_proj", grid=(nb, N_DEV * UNITS_PER_DEV),
        in_specs=[pl.BlockSpec((None, s, D), lambda b, u: (b, 0, 0)),
                  pl.BlockSpec((None, D, UNIT), lambda b, u: (u // UNITS_PER_DEV, 0, u % UNITS_PER_DEV))],
        out_specs=pl.BlockSpec((None, None, s, UNIT), lambda b, u: (u // 2, b, 0, u % 2)),
        out_shape=jax.ShapeDtypeStruct((N_SEG, nb, s, D), BF16),
        compiler_params=_params(("arbitrary", "arbitrary")),
    )(h, wg)


def _out_proj(x, merged, wout, gate):
    nb, s, _ = x.shape
    tm = _row_tile(s, 512)

    def body(x_ref, m_ref, w_ref, g_ref, o_ref):
        o_ref[...] = x_ref[...] + g_ref[...] * _dot(m_ref[...], w_ref[...])

    tile = pl.BlockSpec((None, tm, D), lambda b, m: (b, m, 0))
    return pl.pallas_call(
        body, name="out_proj", grid=(nb, s // tm),
        in_specs=[tile, tile, pl.BlockSpec((D, D), lambda b, m: (0, 0)),
                  pl.BlockSpec((None, 1, D), lambda b, m: (b, 0, 0))],
        out_specs=tile, out_shape=jax.ShapeDtypeStruct(x.shape, F32),
        compiler_params=_params(("arbitrary", "arbitrary")),
    )(x, merged, wout, gate)


def _loss_head(x, gain, target):
    nb, s, _ = x.shape
    tm = _row_tile(s, 512)

    def body(x_ref, g_ref, t_ref, loss_ref, dx_ref, dg_ref):
        first = (pl.program_id(0) == 0) & (pl.program_id(1) == 0)
        last = (pl.program_id(0) == nb - 1) & (pl.program_id(1) == s // tm - 1)

        @pl.when(first)
        def _():
            loss_ref[...] = jnp.zeros_like(loss_ref)
            dg_ref[...] = jnp.zeros_like(dg_ref)

        xv = x_ref[...]
        r = lax.rsqrt(jnp.mean(xv * xv, axis=1, keepdims=True) + EPS)
        xn = xv * r
        g = g_ref[...]
        e = xn * g - t_ref[...]
        loss_ref[...] += _rowsum(e * e) * (0.5 / D)
        dy = e * (1.0 / D)
        dg_ref[...] += _rowsum(dy * xn)
        dxn = dy * g
        dx_ref[...] = r * (dxn - xn * jnp.mean(dxn * xn, axis=1, keepdims=True))

        @pl.when(last)
        def _():
            loss_ref[...] = jnp.broadcast_to(jnp.sum(loss_ref[...], axis=1, keepdims=True), (1, D))

    tile = pl.BlockSpec((None, tm, D), lambda b, m: (b, m, 0))
    vec = pl.BlockSpec((1, D), lambda b, m: (0, 0))
    return pl.pallas_call(
        body, name="loss_head", grid=(nb, s // tm),
        in_specs=[tile, vec, tile], out_specs=[vec, tile, vec],
        out_shape=[jax.ShapeDtypeStruct((1, D), F32), jax.ShapeDtypeStruct(x.shape, F32),
                   jax.ShapeDtypeStruct((1, D), F32)],
        compiler_params=_params(("arbitrary", "arbitrary")),
    )(x, gain, target)


def _out_proj_bwd(dxo, merged, wout, gate):
    nb, s, _ = dxo.shape
    tm = _row_tile(s, 512)

    def body(d_ref, m_ref, w_ref, g_ref, dm_ref, gw_ref, dg_ref):
        @pl.when((pl.program_id(0) == 0) & (pl.program_id(1) == 0))
        def _():
            gw_ref[...] = jnp.zeros_like(gw_ref)

        @pl.when(pl.program_id(1) == 0)
        def _():
            dg_ref[...] = jnp.zeros_like(dg_ref)

        d = d_ref[...]
        m = m_ref[...]
        wv = w_ref[...]
        dg_ref[...] += _rowsum(d * _dot(m, wv))
        dout = (d * g_ref[...]).astype(BF16)
        dm_ref[...] = _dot_nt(dout, wv).astype(BF16)
        gw_ref[...] += _dot_tn(m, dout)

    tile = pl.BlockSpec((None, tm, D), lambda b, m: (b, m, 0))
    vec = pl.BlockSpec((None, 1, D), lambda b, m: (b, 0, 0))
    full = pl.BlockSpec((D, D), lambda b, m: (0, 0))
    return pl.pallas_call(
        body, name="out_proj_bwd", grid=(nb, s // tm),
        in_specs=[tile, tile, full, vec], out_specs=[tile, full, vec],
        out_shape=[jax.ShapeDtypeStruct(dxo.shape, BF16), jax.ShapeDtypeStruct((D, D), F32),
                   jax.ShapeDtypeStruct((nb, 1, D), F32)],
        compiler_params=_params(("arbitrary", "arbitrary")),
    )(dxo, merged, wout, gate)


def _in_proj_bwd_x(dproj, wg, x, dxo, gain, scale):
    _, nb, s, _ = dproj.shape
    tm = _row_tile(s, 512)
    n_u = N_DEV * UNITS_PER_DEV

    def body(dp_ref, w_ref, x_ref, dxo_ref, g_ref, sc_ref, dx_ref, dsh_ref, dsc_ref, dg_ref, acc_ref):
        b, m, u = pl.program_id(0), pl.program_id(1), pl.program_id(2)

        @pl.when(u == 0)
        def _():
            acc_ref[...] = jnp.zeros_like(acc_ref)

        acc_ref[...] += _dot_nt(dp_ref[...], w_ref[...])

        @pl.when(u == n_u - 1)
        def _():
            @pl.when((b == 0) & (m == 0))
            def _():
                dg_ref[...] = jnp.zeros_like(dg_ref)

            @pl.when(m == 0)
            def _():
                dsh_ref[...] = jnp.zeros_like(dsh_ref)
                dsc_ref[...] = jnp.zeros_like(dsc_ref)

            dh = acc_ref[...]
            xv = x_ref[...]
            r = lax.rsqrt(jnp.mean(xv * xv, axis=1, keepdims=True) + EPS)
            xn = xv * r
            g = g_ref[...]
            one_sc = 1.0 + sc_ref[...]
            dsh_ref[...] += _rowsum(dh)
            dsc_ref[...] += _rowsum(dh * (xn * g))
            dg_ref[...] += _rowsum(dh * one_sc * xn)
            dxn = dh * (g * one_sc)
            dx_ref[...] = dxo_ref[...] + r * (dxn - xn * jnp.mean(dxn * xn, axis=1, keepdims=True))

    tile = pl.BlockSpec((None, tm, D), lambda b, m, u: (b, m, 0))
    vec = pl.BlockSpec((None, 1, D), lambda b, m, u: (b, 0, 0))
    one = pl.BlockSpec((1, D), lambda b, m, u: (0, 0))
    return pl.pallas_call(
        body, name="in_proj_bwd_x", grid=(nb, s // tm, n_u),
        in_specs=[pl.BlockSpec((None, None, tm, UNIT), lambda b, m, u: (u // 2, b, m, u % 2)),
                  pl.BlockSpec((None, D, UNIT), lambda b, m, u: (u // UNITS_PER_DEV, 0, u % UNITS_PER_DEV)),
                  tile, tile, one, vec],
        out_specs=[tile, vec, vec, one],
        out_shape=[jax.ShapeDtypeStruct(x.shape, F32), jax.ShapeDtypeStruct((nb, 1, D), F32),
                   jax.ShapeDtypeStruct((nb, 1, D), F32), jax.ShapeDtypeStruct((1, D), F32)],
        scratch_shapes=[pltpu.VMEM((tm, D), F32)],
        compiler_params=_params(("arbitrary", "arbitrary", "arbitrary")),
    )(dproj, wg, x, dxo, gain, scale)


def _in_proj_bwd_w(h, dproj):
    nb, s, _ = h.shape
    tm = _row_tile(s, 1024)
    n_m = s // tm

    def body(h_ref, dp_ref, o_ref, acc_ref):
        b, m = pl.program_id(1), pl.program_id(2)

        @pl.when((b == 0) & (m == 0))
        def _():
            acc_ref[...] = jnp.zeros_like(acc_ref)

        acc_ref[...] += _dot_tn(h_ref[...], dp_ref[...])

        @pl.when((b == nb - 1) & (m == n_m - 1))
        def _():
            o_ref[0] = acc_ref[:, :UNIT].astype(BF16)
            o_ref[1] = acc_ref[:, UNIT:].astype(BF16)

    return pl.pallas_call(
        body, name="in_proj_bwd_w", grid=(N_SEG, nb, n_m),
        in_specs=[pl.BlockSpec((None, tm, D), lambda j, b, m: (b, m, 0)),
                  pl.BlockSpec((None, None, tm, D), lambda j, b, m: (j, b, m, 0))],
        out_specs=pl.BlockSpec((2, D, UNIT), lambda j, b, m: (j, 0, 0)),
        out_shape=jax.ShapeDtypeStruct((2 * N_SEG, D, UNIT), BF16),
        scratch_shapes=[pltpu.VMEM((D, D), F32)],
        compiler_params=_params(("arbitrary", "arbitrary", "arbitrary")),
    )(h, dproj)


def _mod_proj(c_all, w_mod, b_mod_mine):
    nl, _, ncol = w_mod.shape
    nbg = c_all.shape[0]

    def body(c_ref, w_ref, b_ref, o_ref):
        cv = c_ref[...]
        o_ref[...] = jnp.dot(cv * jax.nn.sigmoid(cv), w_ref[...], preferred_element_type=F32,
                             precision=lax.Precision.HIGHEST) + b_ref[...]

    return pl.pallas_call(
        body, name="mod_proj", grid=(nl,),
        in_specs=[pl.BlockSpec((nbg, D), lambda l: (0, 0)), pl.BlockSpec((None, D, ncol), lambda l: (l, 0, 0)),
                  pl.BlockSpec((None, 1, ncol), lambda l: (l, 0, 0))],
        out_specs=pl.BlockSpec((None, nbg, ncol), lambda l: (l, 0, 0)),
        out_shape=jax.ShapeDtypeStruct((nl, nbg, ncol), F32),
        compiler_params=_params(("arbitrary",)),
    )(c_all, w_mod, b_mod_mine)


def _mod_grad(c_all, dmod_all, dmod_mine):
    nl, nbg, ncol = dmod_mine.shape

    def body(c_ref, da_ref, dm_ref, gw_ref, gb_ref):
        cv = c_ref[...]
        gw_ref[...] = lax.dot_general(cv * jax.nn.sigmoid(cv), dm_ref[...], (((0,), (0,)), ((), ())),
                                      preferred_element_type=F32, precision=lax.Precision.HIGHEST)
        gb_ref[...] = _rowsum(da_ref[...])

    return pl.pallas_call(
        body, name="mod_grad", grid=(nl,),
        in_specs=[pl.BlockSpec((nbg, D), lambda l: (0, 0)), pl.BlockSpec((None, nbg, 3 * D), lambda l: (l, 0, 0)),
                  pl.BlockSpec((None, nbg, ncol), lambda l: (l, 0, 0))],
        out_specs=[pl.BlockSpec((None, D, ncol), lambda l: (l, 0, 0)),
                   pl.BlockSpec((None, 1, 3 * D), lambda l: (l, 0, 0))],
        out_shape=[jax.ShapeDtypeStruct((nl, D, ncol), F32), jax.ShapeDtypeStruct((nl, 1, 3 * D), F32)],
        compiler_params=_params(("arbitrary",)),
    )(c_all, dmod_all, dmod_mine)


def _adamw(parts, w, m, v, name):
    n_parts, n_u, n_r, cu = parts.shape
    assert w.shape == (n_r, n_u * cu), (parts.shape, w.shape)
    tr = n_r
    for cand in (512, 256, 128):
        if n_r > cand and n_r % cand == 0:
            tr = cand
            break

    def body(p_ref, w_ref, m_ref, v_ref, g_ref, d_ref, nm_ref, nv_ref):
        g = p_ref[0].astype(F32)
        for k in range(1, n_parts):
            g = g + p_ref[k].astype(F32)
        m2 = ADAM_B1 * m_ref[...] + (1.0 - ADAM_B1) * g
        v2 = ADAM_B2 * v_ref[...] + (1.0 - ADAM_B2) * (g * g)
        m_hat = m2 / (1.0 - ADAM_B1 ** ADAM_STEP)
        v_hat = v2 / (1.0 - ADAM_B2 ** ADAM_STEP)
        g_ref[...] = g
        d_ref[...] = -ADAM_LR * (m_hat / (jnp.sqrt(v_hat) + ADAM_EPS) + ADAM_WD * w_ref[...])
        nm_ref[...] = m2
        nv_ref[...] = v2

    tile = pl.BlockSpec((tr, cu), lambda u, i: (i, u))
    shp = jax.ShapeDtypeStruct(w.shape, F32)
    return pl.pallas_call(
        body, name=name, grid=(n_u, n_r // tr),
        in_specs=[pl.BlockSpec((n_parts, None, tr, cu), lambda u, i: (0, u, i, 0)), tile, tile, tile],
        out_specs=[tile, tile, tile, tile], out_shape=[shp, shp, shp, shp],
        compiler_params=_params(("arbitrary", "arbitrary")),
    )(parts, w, m, v)


def _gathered_cols(g, inner):
    k = len(inner)
    perm = tuple(range(1, k + 1)) + (0, k + 1)
    t = jnp.transpose(g, perm)
    return t.reshape(tuple(inner) + (g.shape[0] * g.shape[-1],))


def _pair_blocks(wh):
    z = jnp.zeros((8, 64, 64), wh.dtype)
    w2 = wh.reshape(8, 2, 64, 64)
    top = jnp.concatenate([w2[:, 0], z], axis=2)
    bot = jnp.concatenate([z, w2[:, 1]], axis=2)
    return jnp.concatenate([top, bot], axis=1).astype(BF16)


def _unpair_blocks(g):
    return jnp.stack([g[:, :64, :64], g[:, 64:, 64:]], axis=1).reshape(16, 64, 64)


FLAT_ROWS = 512


def _pack_flat(arrays):
    flat = jnp.concatenate([a.reshape(-1) for a in arrays])
    n = flat.shape[0]
    per = FLAT_ROWS * LANES
    pad = (-n) % per
    return jnp.pad(flat, (0, pad)).reshape(-1, LANES)


def _unpack_flat(packed, like):
    flat = packed.reshape(-1)
    out, off = [], 0
    for a in like:
        out.append(flat[off:off + a.size].reshape(a.shape))
        off += a.size
    return out


def kernel(x, c, norm_gain, w_mod, b_mod, w_in, w_out, conv_a_w, sgu_w, sgu_b, lru_conv_w, lru_conv_b, lru_wa, lru_ba, lru_wx, lru_bx, lru_lambda, final_gain, loss_target, m_norm_gain, m_w_mod, m_b_mod, m_w_in, m_w_out, m_conv_a_w, m_sgu_w, m_sgu_b, m_lru_conv_w, m_lru_conv_b, m_lru_wa, m_lru_ba, m_lru_wx, m_lru_bx, m_lru_lambda, m_final_gain, v_norm_gain, v_w_mod, v_b_mod, v_w_in, v_w_out, v_conv_a_w, v_sgu_w, v_sgu_b, v_lru_conv_w, v_lru_conv_b, v_lru_wa, v_lru_ba, v_lru_wx, v_lru_bx, v_lru_lambda, v_final_gain):
    nl = w_in.shape[0]
    nb, s, _ = x.shape
    me = _my_index()
    mod_cols = w_mod.shape[2]

    w_in_b = w_in.astype(BF16)
    w_out_b = w_out.astype(BF16)
    wg_started = [_split_start(w_in_b[l], _peers_same_core, False, "gather_w_in_start") for l in range(nl)]

    small = jnp.concatenate([c.reshape(-1, LANES), conv_a_w.reshape(-1, LANES), lru_conv_w.reshape(-1, LANES)])
    n_c, n_ca = nb * D // LANES, nl * 3
    n_small = small.shape[0]
    small = jnp.pad(small, ((0, (-n_small) % 8), (0, 0)))
    for h in wg_started:
        small = _after(small, h)
    small_all = _all_gather(small, "gather_small")
    c_all = small_all[:, :n_c].reshape(N_DEV * nb, D)
    conv_a_full = _gathered_cols(small_all[:, n_c:n_c + n_ca].reshape(N_DEV, nl, 3, LANES), (nl, 3))
    lru_conv_full = _gathered_cols(small_all[:, n_c + n_ca:n_small].reshape(N_DEV, nl, 4, LANES), (nl, 4))

    b_mod_mine = lax.dynamic_slice_in_dim(b_mod, me * mod_cols, mod_cols, axis=1)[:, None, :]
    mod_mine = _mod_proj(c_all, w_mod, b_mod_mine)
    mod_all = _all_gather(mod_mine.reshape(nl * N_DEV * nb, mod_cols), "gather_mod")
    mod_full = _gathered_cols(mod_all.reshape(N_DEV, nl, N_DEV * nb, mod_cols), (nl, N_DEV * nb))
    mod_loc = lax.dynamic_slice_in_dim(mod_full, me * nb, nb, axis=1)
    shift, scale, gate = [mod_loc[:, :, j * D:(j + 1) * D][:, :, None, :] for j in range(3)]

    wo = [_all_gather(w_out_b[l], "gather_w_out").reshape(D, D) for l in range(nl)]

    def gathered_w_in(l, after):
        block, land = _split_wait(wg_started[l], after, _peers_same_core, False, False, "gather_w_in_wait")
        return _gather_finish(block, land, "gather_w_in_finish")

    sgu_b_lanes = jnp.broadcast_to(sgu_b[..., None], sgu_b.shape + (LANES,))
    mws = []
    for l in range(nl):
        mws.append((conv_a_full[l], sgu_w[l], sgu_b_lanes[l], lru_conv_full[l], lru_conv_b[l][None, :],
                    _pair_blocks(lru_wa[l]), _pair_blocks(lru_wx[l]), lru_ba[l].reshape(1, D),
                    lru_bx[l].reshape(1, D), lru_lambda[l][None, :]))

    xs, hs_bf, projs, mergeds, states, wg = [], [], [], [], [], []
    xl = x
    for l in range(nl):
        h = _norm_mod(xl, norm_gain[l][None, :], shift[l], scale[l])
        wg.append(gathered_w_in(l, h))
        proj = _in_proj(h, wg[l])
        merged, st = _mixer_fwd(proj, mws[l])
        xs.append(xl), hs_bf.append(h), projs.append(proj), mergeds.append(merged), states.append(st)
        xl = _out_proj(xl, merged, wo[l], gate[l])

    loss_row, dx, g_final = _loss_head(xl, final_gain[None, :], loss_target)
    loss = lax.psum(loss_row[0, 0], ("x", "y", "c"))

    rep_names = ["norm_gain", "sgu_w", "sgu_b", "lru_conv_b", "lru_wa", "lru_ba", "lru_wx", "lru_bx", "lru_lambda"]
    rep_w = dict(norm_gain=norm_gain, sgu_w=sgu_w, sgu_b=sgu_b, lru_conv_b=lru_conv_b, lru_wa=lru_wa, lru_ba=lru_ba,
                 lru_wx=lru_wx, lru_bx=lru_bx, lru_lambda=lru_lambda)
    rep_m = dict(norm_gain=m_norm_gain, sgu_w=m_sgu_w, sgu_b=m_sgu_b, lru_conv_b=m_lru_conv_b, lru_wa=m_lru_wa,
                 lru_ba=m_lru_ba, lru_wx=m_lru_wx, lru_bx=m_lru_bx, lru_lambda=m_lru_lambda)
    rep_v = dict(norm_gain=v_norm_gain, sgu_w=v_sgu_w, sgu_b=v_sgu_b, lru_conv_b=v_lru_conv_b, lru_wa=v_lru_wa,
                 lru_ba=v_lru_ba, lru_wx=v_lru_wx, lru_bx=v_lru_bx, lru_lambda=v_lru_lambda)

    def rep_pack(src, l, final):
        tail = final if l == nl - 1 else jnp.zeros_like(final)
        return _pack_flat([src[n][l] for n in rep_names] + [tail])

    rep_like = [rep_w[n][0] for n in rep_names] + [final_gain]
    g_w_in, g_w_out, g_conv, res_rep_l = [None] * nl, [None] * nl, [None] * nl, [None] * nl
    dmods = [None] * nl

    def finish_exchange(pending, after):
        l, h_in, h_out, h_rep = pending
        _, recv = _split_wait(h_in, after, _peers_all, True, True, "scatter_w_in_wait")
        g_w_in[l] = _adamw(recv, w_in[l], m_w_in[l], v_w_in[l], "adamw_w_in")
        _, recv = _split_wait(h_out, after, _peers_all, True, True, "scatter_w_out_wait")
        g_w_out[l] = _adamw(recv, w_out[l], m_w_out[l], v_w_out[l], "adamw_w_out")
        _, recv = _split_wait(h_rep, after, _peers_all, False, True, "gather_rep_wait")
        res = _adamw(recv[:, None], rep_pack(rep_w, l, final_gain), rep_pack(rep_m, l, m_final_gain),
                     rep_pack(rep_v, l, v_final_gain), "adamw_rep")
        res_rep_l[l] = [_unpack_flat(a, rep_like) for a in res]

    pending = None
    for l in reversed(range(nl)):
        gate_l = gate[l]
        if pending is not None:
            for h in pending[1:]:
                gate_l = _after(gate_l, h)
        dmerged, gw_out, dgate = _out_proj_bwd(dx, mergeds[l], wo[l], gate_l)
        dproj, g_caw, g_sw, g_sb, g_lcw, g_vec, g_wa, g_wx = _mixer_bwd(projs[l], dmerged, states[l], mws[l])
        dx, dshift, dscale, g_gain = _in_proj_bwd_x(dproj, wg[l], xs[l], dx, norm_gain[l][None, :], scale[l])
        gw_in = _in_proj_bwd_w(hs_bf[l], dproj)
        dmods[l] = jnp.concatenate([dshift, dscale, dgate], axis=2)[:, 0, :]
        g_conv[l] = (g_caw, g_lcw)
        rep_g = dict(
            norm_gain=[g_gain[0]], sgu_w=[g_sw], sgu_b=[g_sb[:, :, 0]], lru_conv_b=[g_vec[0]],
            lru_wa=[_unpair_blocks(g_wa)], lru_ba=[g_vec[1].reshape(16, 64)], lru_wx=[_unpair_blocks(g_wx)],
            lru_bx=[g_vec[2].reshape(16, 64)], lru_lambda=[g_vec[3]])
        rep_block = _pack_flat([rep_g[n][0] for n in rep_names]
                               + [g_final[0] if l == nl - 1 else jnp.zeros_like(g_final[0])])
        started = (l,
                   _split_start(gw_in.reshape(N_DEV, UNITS_PER_DEV, D, UNIT), _peers_all, True, "scatter_w_in_start"),
                   _split_start(gw_out.reshape(N_DEV, 1, D // N_DEV, D), _peers_all, True, "scatter_w_out_start"),
                   _split_start(rep_block, _peers_all, False, "gather_rep_start"))
        if pending is not None:
            finish_exchange(pending, started[3][4])
        pending = started
    finish_exchange(pending, dx)

    dmod_loc = jnp.stack(dmods)
    dmod_g = _all_gather(dmod_loc.reshape(nl * nb, 3 * D), "gather_dmod")
    dmod_all = jnp.transpose(dmod_g.reshape(N_DEV, nl, nb, 3 * D), (1, 0, 2, 3)).reshape(nl, N_DEV * nb, 3 * D)
    dmod_mine = lax.dynamic_slice_in_dim(dmod_all, me * mod_cols, mod_cols, axis=2)
    gw_mod, gb_mod = _mod_grad(c_all, dmod_all, dmod_mine)
    res_w_mod = _adamw(gw_mod.reshape(1, 1, nl * D, mod_cols), w_mod.reshape(nl * D, mod_cols),
                       m_w_mod.reshape(nl * D, mod_cols), v_w_mod.reshape(nl * D, mod_cols), "adamw_w_mod")
    res_w_mod = [a.reshape(nl, D, mod_cols) for a in res_w_mod]
    res_b_mod = _adamw(gb_mod.reshape(1, 1, nl, 3 * D), b_mod, m_b_mod, v_b_mod, "adamw_b_mod")

    conv_parts = jnp.concatenate(
        [jnp.stack([g_conv[l][0] for l in range(nl)]).reshape(nl * 3, N_DEV, LANES),
         jnp.stack([g_conv[l][1] for l in range(nl)]).reshape(nl * 4, N_DEV, LANES)], axis=0)
    conv_parts = jnp.transpose(conv_parts, (1, 0, 2))[:, None]
    conv_recv = _all_to_all(conv_parts, "scatter_conv")
    cat = lambda a, b: jnp.concatenate([a.reshape(nl * 3, LANES), b.reshape(nl * 4, LANES)], axis=0)
    res_conv = _adamw(conv_recv, cat(conv_a_w, lru_conv_w), cat(m_conv_a_w, m_lru_conv_w),
                      cat(v_conv_a_w, v_lru_conv_w), "adamw_conv")
    res_conv_a = [a[:nl * 3].reshape(nl, 3, LANES) for a in res_conv]
    res_lru_conv = [a[nl * 3:].reshape(nl, 4, LANES) for a in res_conv]

    res_rep = []
    for k in range(4):
        d = {n: jnp.stack([res_rep_l[l][k][i] for l in range(nl)]) for i, n in enumerate(rep_names)}
        d["final_gain"] = res_rep_l[nl - 1][k][len(rep_names)]
        res_rep.append(d)

    def stacked(per_layer, k):
        return jnp.stack([per_layer[l][k] for l in range(nl)])

    def leaf(k, name):
        if name == "w_mod":
            return res_w_mod[k]
        if name == "b_mod":
            return res_b_mod[k]
        if name == "w_in":
            return stacked(g_w_in, k)
        if name == "w_out":
            return stacked(g_w_out, k)
        if name == "conv_a_w":
            return res_conv_a[k]
        if name == "lru_conv_w":
            return res_lru_conv[k]
        return res_rep[k][name]

    order = ["norm_gain", "w_mod", "b_mod", "w_in", "w_out", "conv_a_w", "sgu_w", "sgu_b", "lru_conv_w",
             "lru_conv_b", "lru_wa", "lru_ba", "lru_wx", "lru_bx", "lru_lambda", "final_gain"]
    outs = [loss, dx]
    for k in range(4):
        outs += [leaf(k, n) for n in order]
    return tuple(outs)
```

```python
import functools

import jax
import jax.numpy as jnp
from jax import lax
from jax.experimental import pallas as pl
from jax.experimental.pallas import tpu as pltpu

F32 = jnp.float32
BF16 = jnp.bfloat16

D = 1024
N_DEV = 8
N_SEG = 12
LANES = 128
CHUNK = 128
HALO = 16
UNIT = 512
UNITS_PER_DEV = 3
EPS = 1e-6
LRU_C = 8.0
ADAM_LR, ADAM_B1, ADAM_B2, ADAM_EPS, ADAM_WD, ADAM_STEP = 0.001, 0.9, 0.999, 1e-08, 0.01, 10
VMEM_LIMIT = 56 * 1024 * 1024

AX, AB, AC, AZ, SU, SV, SZ, RX, RZ, GA, GS, GR = range(N_SEG)
MESH = pl.DeviceIdType.MESH


def _params(sem=None):
    return pltpu.CompilerParams(dimension_semantics=sem, vmem_limit_bytes=VMEM_LIMIT)


def _my_index():
    return 4 * lax.axis_index("x") + 2 * lax.axis_index("y") + lax.axis_index("c")


def _all_gather(block, name):
    def body(x_ref, out_ref, token, send_sems, recv_sems, local_sem):
        x, y, c = lax.axis_index("x"), lax.axis_index("y"), lax.axis_index("c")
        me, sibling = (x, y, c), (x, y, 1 - c)
        chips = [(1 - x, y), (x, 1 - y), (1 - x, 1 - y)]
        token[...] = jnp.zeros_like(token)

        def rows(px, py, pc):
            return out_ref.at[4 * px + 2 * py + pc]

        def copy(k, blk, to, src=None):
            return pltpu.make_async_remote_copy(
                src_ref=rows(*blk) if src is None else src, dst_ref=rows(*blk),
                send_sem=send_sems.at[k], recv_sem=recv_sems.at[k], device_id=to, device_id_type=MESH)

        mine = pltpu.make_async_copy(x_ref, rows(*me), local_sem)
        mine.start()
        first = [copy(0, me, sibling, src=x_ref)]
        first += [copy(1 + j, me, (*chip, c), src=x_ref) for j, chip in enumerate(chips)]
        for cp in first:
            cp.start()
        passed = [copy(4 + j, (*chip, c), sibling) for j, chip in enumerate(chips)]
        for j, chip in enumerate(chips):
            copy(1 + j, (*chip, c), me).wait_recv()
            passed[j].start()
        copy(0, sibling, me).wait_recv()
        for j, chip in enumerate(chips):
            copy(4 + j, (*chip, 1 - c), me).wait_recv()
        for cp in first + passed:
            cp.wait_send()
        mine.wait()

    return pl.pallas_call(
        body, name=name,
        out_shape=[jax.ShapeDtypeStruct((N_DEV,) + block.shape, block.dtype), jax.ShapeDtypeStruct((8, LANES), F32)],
        in_specs=[pl.BlockSpec(memory_space=pltpu.VMEM)],
        out_specs=[pl.BlockSpec(memory_space=pl.ANY), pl.BlockSpec(memory_space=pltpu.VMEM)],
        scratch_shapes=[pltpu.SemaphoreType.DMA((7,)), pltpu.SemaphoreType.DMA((7,)), pltpu.SemaphoreType.DMA],
    )(block)


def _all_to_all(blocks, name):
    def body(x_ref, out_ref, send_sems, recv_sems, local_sem):
        x, y, c = lax.axis_index("x"), lax.axis_index("y"), lax.axis_index("c")
        my = 4 * x + 2 * y + c
        mine = pltpu.make_async_copy(x_ref.at[my], out_ref.at[my], local_sem)
        mine.start()
        peers = []
        for r in range(1, N_DEV):
            px = 1 - x if r & 4 else x
            py = 1 - y if r & 2 else y
            pc = 1 - c if r & 1 else c
            peers.append((r - 1, 4 * px + 2 * py + pc, (px, py, pc)))

        def copy(k, src_slot, dst_slot, to):
            return pltpu.make_async_remote_copy(
                src_ref=x_ref.at[src_slot], dst_ref=out_ref.at[dst_slot],
                send_sem=send_sems.at[k], recv_sem=recv_sems.at[k], device_id=to, device_id_type=MESH)

        sends = [copy(k, pid, my, to) for k, pid, to in peers]
        for cp in sends:
            cp.start()
        for k, pid, to in peers:
            copy(k, pid, pid, to).wait_recv()
        for cp in sends:
            cp.wait_send()
        mine.wait()

    return pl.pallas_call(
        body, name=name,
        out_shape=jax.ShapeDtypeStruct(blocks.shape, blocks.dtype),
        in_specs=[pl.BlockSpec(memory_space=pltpu.VMEM)],
        out_specs=pl.BlockSpec(memory_space=pl.ANY),
        scratch_shapes=[pltpu.SemaphoreType.DMA((7,)), pltpu.SemaphoreType.DMA((7,)), pltpu.SemaphoreType.DMA],
    )(blocks)


_HBM = pl.BlockSpec(memory_space=pltpu.HBM)
_SEM = pl.BlockSpec(memory_space=pltpu.SEMAPHORE)
_EFFECT = pltpu.SideEffectType.DATAFLOW_SIDE_EFFECTING


def _peers_all(x, y, c):
    out = []
    for r in range(1, N_DEV):
        px = 1 - x if r & 4 else x
        py = 1 - y if r & 2 else y
        pc = 1 - c if r & 1 else c
        out.append((r - 1, 4 * px + 2 * py + pc, (px, py, pc)))
    return out


def _peers_same_core(x, y, c):
    return [(k, 4 * px + 2 * py + c, (px, py, c))
            for k, (px, py) in enumerate([(1 - x, y), (x, 1 - y), (1 - x, 1 - y)])]


def _split_start(src, peers_fn, scatter, name):
    blk = src.shape[1:] if scatter else src.shape
    land_shape = (N_DEV,) + tuple(blk)
    n = len(peers_fn(0, 0, 0))

    def body(x_ref, land_ref, send_sems, recv_sems, x_thru, land_thru, token):
        x, y, c = lax.axis_index("x"), lax.axis_index("y"), lax.axis_index("c")
        my = 4 * x + 2 * y + c
        for k, pid, to in peers_fn(x, y, c):
            pltpu.make_async_remote_copy(
                src_ref=x_ref.at[pid] if scatter else x_ref, dst_ref=land_ref.at[my],
                send_sem=send_sems.at[k], recv_sem=recv_sems.at[k], device_id=to, device_id_type=MESH).start()
        token[...] = jnp.zeros_like(token)

    return pl.pallas_call(
        body, name=name,
        out_shape=(pltpu.SemaphoreType.DMA((n,)), pltpu.SemaphoreType.DMA((n,)),
                   pltpu.HBM(src.shape, src.dtype), pltpu.HBM(land_shape, src.dtype),
                   jax.ShapeDtypeStruct((8, LANES), F32)),
        in_specs=(_HBM, _HBM),
        out_specs=(_SEM, _SEM, _HBM, _HBM, pl.BlockSpec(memory_space=pltpu.VMEM)),
        input_output_aliases={0: 2, 1: 3},
        compiler_params=pltpu.CompilerParams(has_side_effects=_EFFECT),
    )(pltpu.with_memory_space_constraint(src, pltpu.HBM),
      pltpu.with_memory_space_constraint(lax.empty(land_shape, src.dtype), pltpu.HBM))


def _split_wait(handles, after, peers_fn, scatter, own, name):
    send_sems, recv_sems, src_thru, land_thru, _ = handles
    blk = land_thru.shape[1:]

    def body(x_ref, land_ref, send_sems, recv_sems, after_ref, x_dead, got_ref, *stage):
        x, y, c = lax.axis_index("x"), lax.axis_index("y"), lax.axis_index("c")
        if own:
            my = 4 * x + 2 * y + c
            mine = _staged_copy(x_ref.at[my] if scatter else x_ref, land_ref.at[my], *stage)
        for k, pid, to in peers_fn(x, y, c):
            cp = pltpu.make_async_remote_copy(
                src_ref=x_ref.at[pid] if scatter else x_ref, dst_ref=land_ref.at[pid],
                send_sem=send_sems.at[k], recv_sem=recv_sems.at[k], device_id=to, device_id_type=MESH)
            cp.wait_send()
            cp.wait_recv()
        if own:
            mine.wait()

    return pl.pallas_call(
        body, name=name,
        out_shape=(pltpu.HBM(src_thru.shape, src_thru.dtype), pltpu.HBM(land_thru.shape, land_thru.dtype)),
        in_specs=(_HBM, _HBM, _SEM, _SEM, pl.BlockSpec(memory_space=pl.ANY)),
        out_specs=(_HBM, _HBM),
        input_output_aliases={0: 0, 1: 1},
        scratch_shapes=[pltpu.VMEM(blk, land_thru.dtype), pltpu.SemaphoreType.DMA((2,))] if own else [],
        compiler_params=pltpu.CompilerParams(has_side_effects=_EFFECT, vmem_limit_bytes=VMEM_LIMIT),
    )(src_thru, land_thru, send_sems, recv_sems, after)


def _staged_copy(src_ref, dst_ref, buf, sems):
    leg = pltpu.make_async_copy(src_ref, buf, sems.at[0])
    leg.start()
    leg.wait()
    leg = pltpu.make_async_copy(buf, dst_ref, sems.at[1])
    leg.start()
    return leg


def _gather_finish(block, land, name):
    def body(x_ref, land_ref, out_ref, token, send_sems, recv_sems, buf, local_sems):
        x, y, c = lax.axis_index("x"), lax.axis_index("y"), lax.axis_index("c")
        my, sib_id, sibling = 4 * x + 2 * y + c, 4 * x + 2 * y + 1 - c, (x, y, 1 - c)
        token[...] = jnp.zeros_like(token)

        def copy(k, slot, src=None):
            return pltpu.make_async_remote_copy(
                src_ref=land_ref.at[slot] if src is None else src, dst_ref=out_ref.at[slot],
                send_sem=send_sems.at[k], recv_sem=recv_sems.at[k], device_id=sibling, device_id_type=MESH)

        chips = _peers_same_core(x, y, c)
        sends = [copy(0, my, src=x_ref)] + [copy(1 + k, pid) for k, pid, _ in chips]
        for cp in sends:
            cp.start()
        mine = _staged_copy(x_ref, out_ref.at[my], buf, local_sems)
        copy(0, sib_id).wait_recv()
        for k, pid, _ in chips:
            copy(1 + k, pid + 1 - 2 * c).wait_recv()
        for cp in sends:
            cp.wait_send()
        mine.wait()

    return pl.pallas_call(
        body, name=name,
        out_shape=[jax.ShapeDtypeStruct(land.shape, land.dtype), jax.ShapeDtypeStruct((8, LANES), F32)],
        in_specs=[pl.BlockSpec(memory_space=pl.ANY), pl.BlockSpec(memory_space=pl.ANY)],
        out_specs=[pl.BlockSpec(memory_space=pl.ANY), pl.BlockSpec(memory_space=pltpu.VMEM)],
        input_output_aliases={1: 0},
        scratch_shapes=[pltpu.SemaphoreType.DMA((4,)), pltpu.SemaphoreType.DMA((4,)),
                        pltpu.VMEM(block.shape, block.dtype), pltpu.SemaphoreType.DMA((2,))],
        compiler_params=pltpu.CompilerParams(vmem_limit_bytes=VMEM_LIMIT),
    )(block, land)


def _after(v, token):
    return v + token[0, 0].astype(v.dtype)


def _dsilu(x, s):
    return s * (1.0 + x * (1.0 - s))


def _log1p(x):
    u = 1.0 + x
    d = u - 1.0
    return jnp.where(d == 0.0, x, jnp.log(u) * (x / jnp.where(d == 0.0, 1.0, d)))


def _softplus_neg(lam):
    return jnp.maximum(-lam, 0.0) + _log1p(jnp.exp(-jnp.abs(lam)))


def _neg_expm1(y):
    poly = -y * (1.0 + y * (0.5 + y * (1.0 / 6.0 + y * (1.0 / 24.0 + y * (1.0 / 120.0)))))
    return jnp.where(y > -0.1, poly, 1.0 - jnp.exp(y))


def _shift_dn(cur, prev, k):
    ext = jnp.concatenate([prev, cur], axis=0)
    return pltpu.roll(ext, k, 0)[HALO:, :]


def _shift_up(cur, nxt, k):
    n = cur.shape[0]
    ext = jnp.concatenate([cur, nxt], axis=0)
    return pltpu.roll(ext, n + HALO - k, 0)[:n, :]


def _scan_fwd(a, b, h_prev):
    ea = jnp.concatenate([jnp.zeros_like(h_prev), a], axis=0)
    eb = jnp.concatenate([h_prev, b], axis=0)
    n = ea.shape[0]
    row = lax.broadcasted_iota(jnp.int32, ea.shape, 0)
    k = 1
    while k < n:
        a_sh = jnp.where(row >= k, pltpu.roll(ea, k, 0), 1.0)
        b_sh = jnp.where(row >= k, pltpu.roll(eb, k, 0), 0.0)
        eb = ea * b_sh + eb
        ea = ea * a_sh
        k *= 2
    return eb[HALO:, :]


def _scan_rev(a_next, g, lam_next):
    ea = jnp.concatenate([a_next, jnp.zeros_like(lam_next)], axis=0)
    eg = jnp.concatenate([g, lam_next], axis=0)
    n = ea.shape[0]
    row = lax.broadcasted_iota(jnp.int32, ea.shape, 0)
    k = 1
    while k < n:
        ok = row < n - k
        a_sh = jnp.where(ok, pltpu.roll(ea, n - k, 0), 1.0)
        g_sh = jnp.where(ok, pltpu.roll(eg, n - k, 0), 0.0)
        eg = eg + ea * g_sh
        ea = ea * a_sh
        k *= 2
    return eg[:g.shape[0], :]


def _rowsum(v):
    return jnp.sum(v, axis=0, keepdims=True)


def _dot(a, b):
    return jnp.dot(a, b, preferred_element_type=F32)


def _dot_nt(a, b):
    return lax.dot_general(a, b, (((1,), (1,)), ((), ())), preferred_element_type=F32)


def _dot_tn(a, b):
    return lax.dot_general(a, b, (((0,), (0,)), ((), ())), preferred_element_type=F32)


class _MixerWeights:
    def __init__(self, caw_ref, sw_ref, sb_ref, lcw_ref, lcb_ref, wa_ref, wx_ref, ba_ref, bx_ref, lam_ref):
        self.caw = [caw_ref[j:j + 1, :] for j in range(3)]
        self.lcw = [lcw_ref[j:j + 1, :] for j in range(4)]
        self.lcb = lcb_ref[...]
        row = lax.broadcasted_iota(jnp.int32, (CHUNK, CHUNK), 0)
        col = lax.broadcasted_iota(jnp.int32, (CHUNK, CHUNK), 1)
        self.tril = col <= row
        self.sw = jnp.where(self.tril, sw_ref[...], 0.0).astype(BF16)
        self.sb = sb_ref[...]
        self.wa = wa_ref[...]
        self.wx = wx_ref[...]
        self.ba = ba_ref[...]
        self.bx = bx_ref[...]
        lam = lam_ref[...]
        self.sp = _softplus_neg(lam)
        self.dsp_dlam = -jax.nn.sigmoid(-lam)


def _mixer_pre_scan(ld, ldp, w):
    t = {}
    a_x, a_c = ld(AX), ld(AC)
    t["a_x"], t["a_c"], t["a_b"], t["a_z"] = a_x, a_c, ld(AB), ld(AZ)
    ca = a_c * a_x
    ca_p = ldp(AC) * ldp(AX)
    t["ca"], t["ca1"], t["ca2"] = ca, _shift_dn(ca, ca_p, 1), _shift_dn(ca, ca_p, 2)
    t["cv"] = w.caw[2] * ca + w.caw[1] * t["ca1"] + w.caw[0] * t["ca2"]
    t["sa"] = jax.nn.sigmoid(t["a_z"])
    t["silu_az"] = t["a_z"] * t["sa"]
    t["y_a"] = t["silu_az"] * t["a_b"] * t["cv"]

    v = ld(SV)
    vc = v - jnp.mean(v, axis=1, keepdims=True)
    t["rstd"] = lax.rsqrt(jnp.mean(vc * vc, axis=1, keepdims=True) + EPS)
    t["vn"] = vc * t["rstd"]
    t["z"] = _dot(w.sw, t["vn"].astype(BF16)) + w.sb
    t["s_u"], t["s_z"] = ld(SU), ld(SZ)
    t["ss"] = jax.nn.sigmoid(t["s_z"])
    t["silu_sz"] = t["s_z"] * t["ss"]
    t["y_s"] = t["silu_sz"] * t["s_u"] * t["z"]

    r_x, r_xp = ld(RX), ldp(RX)
    t["rx"] = [_shift_dn(r_x, r_xp, 3), _shift_dn(r_x, r_xp, 2), _shift_dn(r_x, r_xp, 1), r_x]
    xc = w.lcb + w.lcw[0] * t["rx"][0] + w.lcw[1] * t["rx"][1] + w.lcw[2] * t["rx"][2] + w.lcw[3] * r_x
    t["xc"] = xc
    xcb = xc.astype(BF16)
    t["r"] = jax.nn.sigmoid(_dot(xcb, w.wa) + w.ba)
    t["i"] = jax.nn.sigmoid(_dot(xcb, w.wx) + w.bx)
    la = -LRU_C * t["r"] * w.sp
    t["a"] = jnp.exp(la)
    t["em"] = _neg_expm1(2.0 * la)
    t["mult"] = jnp.sqrt(t["em"])
    t["b"] = t["mult"] * (t["i"] * xc)
    t["r_z"] = ld(RZ)
    t["sr"] = jax.nn.sigmoid(t["r_z"])
    t["silu_rz"] = t["r_z"] * t["sr"]
    t["ga"], t["gs"], t["gr"] = jax.nn.sigmoid(ld(GA)), jax.nn.sigmoid(ld(GS)), jax.nn.sigmoid(ld(GR))
    return t


def _weight_specs(n_cb_axis):
    def at(fn):
        return lambda *g: fn(g[n_cb_axis])
    return [
        pl.BlockSpec((3, LANES), at(lambda cb: (0, cb))),
        pl.BlockSpec((None, CHUNK, CHUNK), at(lambda cb: (cb, 0, 0))),
        pl.BlockSpec((None, CHUNK, LANES), at(lambda cb: (cb, 0, 0))),
        pl.BlockSpec((4, LANES), at(lambda cb: (0, cb))),
        pl.BlockSpec((1, LANES), at(lambda cb: (0, cb))),
        pl.BlockSpec((None, LANES, LANES), at(lambda cb: (cb, 0, 0))),
        pl.BlockSpec((None, LANES, LANES), at(lambda cb: (cb, 0, 0))),
        pl.BlockSpec((1, LANES), at(lambda cb: (0, cb))),
        pl.BlockSpec((1, LANES), at(lambda cb: (0, cb))),
        pl.BlockSpec((1, LANES), at(lambda cb: (0, cb))),
    ]


def _chunk_loaders(p_ref, c):
    r0 = pl.multiple_of(c * CHUNK, CHUNK)
    rp = pl.multiple_of(jnp.maximum(c * CHUNK - HALO, 0), HALO)

    def ld(j):
        return p_ref[j, pl.ds(r0, CHUNK), :].astype(F32)

    def ldp(j):
        return jnp.where(c > 0, p_ref[j, pl.ds(rp, HALO), :].astype(F32), 0.0)

    return r0, rp, ld, ldp


def _mixer_fwd(proj, mw):
    _, nb, s, _ = proj.shape
    n_chunks = s // CHUNK

    def body(p_ref, *refs):
        w = _MixerWeights(*refs[:10])
        merged_ref, hs_ref = refs[10:]

        def chunk(c, h_prev):
            r0, _, ld, ldp = _chunk_loaders(p_ref, c)
            t = _mixer_pre_scan(ld, ldp, w)
            h = _scan_fwd(t["a"], t["b"], h_prev)
            y_r = t["silu_rz"] * h
            merged = t["ga"] * t["y_a"] + t["gs"] * t["y_s"] + t["gr"] * y_r
            merged_ref[pl.ds(r0, CHUNK), :] = merged.astype(BF16)
            hs_ref[pl.ds(r0, CHUNK), :] = h
            return h[CHUNK - HALO:, :]

        lax.fori_loop(0, n_chunks, chunk, jnp.zeros((HALO, LANES), F32))

    slab = pl.BlockSpec((None, s, LANES), lambda cb, b: (b, 0, cb))
    return pl.pallas_call(
        body, name="mixer_fwd", grid=(D // LANES, nb),
        in_specs=[pl.BlockSpec((N_SEG, None, s, LANES), lambda cb, b: (0, b, 0, cb))] + _weight_specs(0),
        out_specs=[slab, slab],
        out_shape=[jax.ShapeDtypeStruct((nb, s, D), BF16), jax.ShapeDtypeStruct((nb, s, D), F32)],
        compiler_params=_params(("arbitrary", "arbitrary")),
    )(proj, *mw)


def _mixer_bwd(proj, dmerged, hs, mw):
    _, nb, s, _ = proj.shape
    n_chunks = s // CHUNK

    def body(p_ref, dm_ref, hs_ref, *refs):
        w = _MixerWeights(*refs[:10])
        dp_ref, g_caw, g_sw, g_sb, g_lcw, g_vec, g_wa, g_wx = refs[10:]

        @pl.when(pl.program_id(1) == 0)
        def _():
            for ref in (g_caw, g_sw, g_sb, g_lcw, g_vec, g_wa, g_wx):
                ref[...] = jnp.zeros_like(ref)

        def chunk(i, carry):
            dcv_n, dxc_n, lam_n, a_n = carry
            c = n_chunks - 1 - i
            r0, rp, ld, ldp = _chunk_loaders(p_ref, c)
            t = _mixer_pre_scan(ld, ldp, w)
            h = hs_ref[pl.ds(r0, CHUNK), :]
            h_p = jnp.where(c > 0, hs_ref[pl.ds(rp, HALO), :], 0.0)
            h_prev = _shift_dn(h, h_p, 1)
            dm = dm_ref[pl.ds(r0, CHUNK), :].astype(F32)
            y_r = t["silu_rz"] * h

            def out(j, val):
                dp_ref[j, pl.ds(r0, CHUNK), :] = val.astype(BF16)

            ga, gs, gr = t["ga"], t["gs"], t["gr"]
            out(GA, dm * t["y_a"] * ga * (1.0 - ga))
            out(GS, dm * t["y_s"] * gs * (1.0 - gs))
            out(GR, dm * y_r * gr * (1.0 - gr))

            dy_a = dm * ga
            out(AZ, dy_a * t["a_b"] * t["cv"] * _dsilu(t["a_z"], t["sa"]))
            out(AB, dy_a * t["silu_az"] * t["cv"])
            dcv = dy_a * t["silu_az"] * t["a_b"]
            dca = w.caw[2] * dcv + w.caw[1] * _shift_up(dcv, dcv_n, 1) + w.caw[0] * _shift_up(dcv, dcv_n, 2)
            out(AC, dca * t["a_x"])
            out(AX, dca * t["a_c"])
            g_caw[2:3, :] += _rowsum(dcv * t["ca"])
            g_caw[1:2, :] += _rowsum(dcv * t["ca1"])
            g_caw[0:1, :] += _rowsum(dcv * t["ca2"])

            dy_s = dm * gs
            out(SZ, dy_s * t["s_u"] * t["z"] * _dsilu(t["s_z"], t["ss"]))
            out(SU, dy_s * t["silu_sz"] * t["z"])
            dz = dy_s * t["silu_sz"] * t["s_u"]
            dzb = dz.astype(BF16)
            g_sb[...] += jnp.broadcast_to(jnp.sum(dz, axis=1, keepdims=True), (CHUNK, LANES))
            g_sw[...] += _dot_nt(dzb, t["vn"].astype(BF16))
            dvn = _dot_tn(w.sw, dzb)
            vn = t["vn"]
            out(SV, t["rstd"] * (dvn - jnp.mean(dvn, axis=1, keepdims=True)
                                 - vn * jnp.mean(dvn * vn, axis=1, keepdims=True)))

            dy_r = dm * gr
            out(RZ, dy_r * h * _dsilu(t["r_z"], t["sr"]))
            lam = _scan_rev(_shift_up(t["a"], a_n, 1), dy_r * t["silu_rz"], lam_n)
            a, r, ig, xc, mult = t["a"], t["r"], t["i"], t["xc"], t["mult"]
            d_i = lam * mult * xc
            d_mult = lam * ig * xc
            dxc = lam * mult * ig
            dla = lam * h_prev * a - d_mult * ((1.0 - t["em"]) / mult)
            g_vec[3:4, :] += _rowsum(dla * (-LRU_C * r)) * w.dsp_dlam
            dpr = (dla * (-LRU_C * w.sp)) * r * (1.0 - r)
            dpi = d_i * ig * (1.0 - ig)
            dprb, dpib, xcb = dpr.astype(BF16), dpi.astype(BF16), xc.astype(BF16)
            g_wa[...] += _dot_tn(xcb, dprb)
            g_wx[...] += _dot_tn(xcb, dpib)
            g_vec[1:2, :] += _rowsum(dpr)
            g_vec[2:3, :] += _rowsum(dpi)
            dxc = dxc + _dot_nt(dprb, w.wa) + _dot_nt(dpib, w.wx)
            g_vec[0:1, :] += _rowsum(dxc)
            out(RX, w.lcw[3] * dxc + w.lcw[2] * _shift_up(dxc, dxc_n, 1)
                + w.lcw[1] * _shift_up(dxc, dxc_n, 2) + w.lcw[0] * _shift_up(dxc, dxc_n, 3))
            for j in range(4):
                g_lcw[j:j + 1, :] += _rowsum(dxc * t["rx"][j])
            return dcv[:HALO, :], dxc[:HALO, :], lam[:HALO, :], a[:HALO, :]

        zero = jnp.zeros((HALO, LANES), F32)
        lax.fori_loop(0, n_chunks, chunk, (zero, zero, zero, zero))

        @pl.when(pl.program_id(1) == nb - 1)
        def _():
            g_sw[...] = jnp.where(w.tril, g_sw[...], 0.0)

    slab = lambda dt: pl.BlockSpec((None, s, LANES), lambda cb, b: (b, 0, cb))
    seg = pl.BlockSpec((N_SEG, None, s, LANES), lambda cb, b: (0, b, 0, cb))
    rows = lambda n: pl.BlockSpec((n, LANES), lambda cb, b: (0, cb))
    sq = pl.BlockSpec((None, LANES, LANES), lambda cb, b: (cb, 0, 0))
    n_cb = D // LANES
    return pl.pallas_call(
        body, name="mixer_bwd", grid=(n_cb, nb),
        in_specs=[seg, slab(BF16), slab(F32)] + _weight_specs(0),
        out_specs=[seg, rows(3), sq, sq, rows(4), rows(8), sq, sq],
        out_shape=[
            jax.ShapeDtypeStruct(proj.shape, BF16),
            jax.ShapeDtypeStruct((3, D), F32),
            jax.ShapeDtypeStruct((n_cb, CHUNK, CHUNK), F32),
            jax.ShapeDtypeStruct((n_cb, CHUNK, LANES), F32),
            jax.ShapeDtypeStruct((4, D), F32),
            jax.ShapeDtypeStruct((8, D), F32),
            jax.ShapeDtypeStruct((n_cb, LANES, LANES), F32),
            jax.ShapeDtypeStruct((n_cb, LANES, LANES), F32),
        ],
        compiler_params=_params(("arbitrary", "arbitrary")),
    )(proj, dmerged, hs, *mw)


def _row_tile(s, want):
    return want if s % want == 0 else s


def _norm_mod(x, gain, shift, scale):
    nb, s, _ = x.shape
    tm = _row_tile(s, 512)

    def body(x_ref, g_ref, sh_ref, sc_ref, h_ref):
        xv = x_ref[...]
        r = lax.rsqrt(jnp.mean(xv * xv, axis=1, keepdims=True) + EPS)
        h_ref[...] = ((xv * r) * g_ref[...] * (1.0 + sc_ref[...]) + sh_ref[...]).astype(BF16)

    tile = pl.BlockSpec((None, tm, D), lambda b, m: (b, m, 0))
    vec = pl.BlockSpec((None, 1, D), lambda b, m: (b, 0, 0))
    return pl.pallas_call(
        body, name="norm_mod", grid=(nb, s // tm),
        in_specs=[tile, pl.BlockSpec((1, D), lambda b, m: (0, 0)), vec, vec],
        out_specs=tile, out_shape=jax.ShapeDtypeStruct(x.shape, BF16),
        compiler_params=_params(("arbitrary", "arbitrary")),
    )(x, gain, shift, scale)


def _in_proj(h, wg, dep):
    nb, s, _ = h.shape

    def body(h_ref, w_ref, dep_ref, o_ref):
        o_ref[...] = _dot(h_ref[...], w_ref[...]).astype(BF16)

    return pl.pallas_call(
        body, name="in_proj", grid=(nb, N_DEV * UNITS_PER_DEV),
        in_specs=[pl.BlockSpec((None, s, D), lambda b, u: (b, 0, 0)),
                  pl.BlockSpec((None, D, UNIT), lambda b, u: (u // UNITS_PER_DEV, 0, u % UNITS_PER_DEV)),
                  pl.BlockSpec((8, LANES), lambda b, u: (0, 0))],
        out_specs=pl.BlockSpec((None, None, s, UNIT), lambda b, u: (u // 2, b, 0, u % 2)),
        out_shape=jax.ShapeDtypeStruct((N_SEG, nb, s, D), BF16),
        compiler_params=_params(("arbitrary", "arbitrary")),
    )(h, wg, dep)


def _out_proj(x, merged, wout, gate):
    nb, s, _ = x.shape
    tm = _row_tile(s, 512)

    def body(x_ref, m_ref, w_ref, g_ref, o_ref):
        o_ref[...] = x_ref[...] + g_ref[...] * _dot(m_ref[...], w_ref[...])

    tile = pl.BlockSpec((None, tm, D), lambda b, m: (b, m, 0))
    return pl.pallas_call(
        body, name="out_proj", grid=(nb, s // tm),
        in_specs=[tile, tile, pl.BlockSpec((D, D), lambda b, m: (0, 0)),
                  pl.BlockSpec((None, 1, D), lambda b, m: (b, 0, 0))],
        out_specs=tile, out_shape=jax.ShapeDtypeStruct(x.shape, F32),
        compiler_params=_params(("arbitrary", "arbitrary")),
    )(x, merged, wout, gate)


def _loss_head(x, gain, target):
    nb, s, _ = x.shape
    tm = _row_tile(s, 512)

    def body(x_ref, g_ref, t_ref, loss_ref, dx_ref, dg_ref):
        first = (pl.program_id(0) == 0) & (pl.program_id(1) == 0)
        last = (pl.program_id(0) == nb - 1) & (pl.program_id(1) == s // tm - 1)

        @pl.when(first)
        def _():
            loss_ref[...] = jnp.zeros_like(loss_ref)
            dg_ref[...] = jnp.zeros_like(dg_ref)

        xv = x_ref[...]
        r = lax.rsqrt(jnp.mean(xv * xv, axis=1, keepdims=True) + EPS)
        xn = xv * r
        g = g_ref[...]
        e = xn * g - t_ref[...]
        loss_ref[...] += _rowsum(e * e) * (0.5 / D)
        dy = e * (1.0 / D)
        dg_ref[...] += _rowsum(dy * xn)
        dxn = dy * g
        dx_ref[...] = r * (dxn - xn * jnp.mean(dxn * xn, axis=1, keepdims=True))

        @pl.when(last)
        def _():
            loss_ref[...] = jnp.broadcast_to(jnp.sum(loss_ref[...], axis=1, keepdims=True), (1, D))

    tile = pl.BlockSpec((None, tm, D), lambda b, m: (b, m, 0))
    vec = pl.BlockSpec((1, D), lambda b, m: (0, 0))
    return pl.pallas_call(
        body, name="loss_head", grid=(nb, s // tm),
        in_specs=[tile, vec, tile], out_specs=[vec, tile, vec],
        out_shape=[jax.ShapeDtypeStruct((1, D), F32), jax.ShapeDtypeStruct(x.shape, F32),
                   jax.ShapeDtypeStruct((1, D), F32)],
        compiler_params=_params(("arbitrary", "arbitrary")),
    )(x, gain, target)


def _out_proj_bwd(dxo, merged, wout, gate):
    nb, s, _ = dxo.shape
    tm = _row_tile(s, 512)

    def body(d_ref, m_ref, w_ref, g_ref, dm_ref, gw_ref, dg_ref):
        @pl.when((pl.program_id(0) == 0) & (pl.program_id(1) == 0))
        def _():
            gw_ref[...] = jnp.zeros_like(gw_ref)

        @pl.when(pl.program_id(1) == 0)
        def _():
            dg_ref[...] = jnp.zeros_like(dg_ref)

        d = d_ref[...]
        m = m_ref[...]
        wv = w_ref[...]
        dg_ref[...] += _rowsum(d * _dot(m, wv))
        dout = (d * g_ref[...]).astype(BF16)
        dm_ref[...] = _dot_nt(dout, wv).astype(BF16)
        gw_ref[...] += _dot_tn(m, dout)

    tile = pl.BlockSpec((None, tm, D), lambda b, m: (b, m, 0))
    vec = pl.BlockSpec((None, 1, D), lambda b, m: (b, 0, 0))
    full = pl.BlockSpec((D, D), lambda b, m: (0, 0))
    return pl.pallas_call(
        body, name="out_proj_bwd", grid=(nb, s // tm),
        in_specs=[tile, tile, full, vec], out_specs=[tile, full, vec],
        out_shape=[jax.ShapeDtypeStruct(dxo.shape, BF16), jax.ShapeDtypeStruct((D, D), F32),
                   jax.ShapeDtypeStruct((nb, 1, D), F32)],
        compiler_params=_params(("arbitrary", "arbitrary")),
    )(dxo, merged, wout, gate)


def _in_proj_bwd_x(dproj, wgt, x, dxo, gain, scale):
    _, nb, s, _ = dproj.shape
    tm = _row_tile(s, 512)
    n_u = N_DEV * UNITS_PER_DEV

    def body(dp_ref, w_ref, x_ref, dxo_ref, g_ref, sc_ref, dx_ref, dsh_ref, dsc_ref, dg_ref, acc_ref):
        b, m, u = pl.program_id(0), pl.program_id(1), pl.program_id(2)

        @pl.when(u == 0)
        def _():
            acc_ref[...] = jnp.zeros_like(acc_ref)

        acc_ref[...] += _dot(dp_ref[...], w_ref[...])

        @pl.when(u == n_u - 1)
        def _():
            @pl.when((b == 0) & (m == 0))
            def _():
                dg_ref[...] = jnp.zeros_like(dg_ref)

            @pl.when(m == 0)
            def _():
                dsh_ref[...] = jnp.zeros_like(dsh_ref)
                dsc_ref[...] = jnp.zeros_like(dsc_ref)

            dh = acc_ref[...]
            xv = x_ref[...]
            r = lax.rsqrt(jnp.mean(xv * xv, axis=1, keepdims=True) + EPS)
            xn = xv * r
            g = g_ref[...]
            one_sc = 1.0 + sc_ref[...]
            dsh_ref[...] += _rowsum(dh)
            dsc_ref[...] += _rowsum(dh * (xn * g))
            dg_ref[...] += _rowsum(dh * one_sc * xn)
            dxn = dh * (g * one_sc)
            dx_ref[...] = dxo_ref[...] + r * (dxn - xn * jnp.mean(dxn * xn, axis=1, keepdims=True))

    tile = pl.BlockSpec((None, tm, D), lambda b, m, u: (b, m, 0))
    vec = pl.BlockSpec((None, 1, D), lambda b, m, u: (b, 0, 0))
    one = pl.BlockSpec((1, D), lambda b, m, u: (0, 0))
    return pl.pallas_call(
        body, name="in_proj_bwd_x", grid=(nb, s // tm, n_u),
        in_specs=[pl.BlockSpec((None, None, tm, UNIT), lambda b, m, u: (u // 2, b, m, u % 2)),
                  pl.BlockSpec((None, UNIT, D), lambda b, m, u: (u // UNITS_PER_DEV, u % UNITS_PER_DEV, 0)),
                  tile, tile, one, vec],
        out_specs=[tile, vec, vec, one],
        out_shape=[jax.ShapeDtypeStruct(x.shape, F32), jax.ShapeDtypeStruct((nb, 1, D), F32),
                   jax.ShapeDtypeStruct((nb, 1, D), F32), jax.ShapeDtypeStruct((1, D), F32)],
        scratch_shapes=[pltpu.VMEM((tm, D), F32)],
        compiler_params=_params(("arbitrary", "arbitrary", "arbitrary")),
    )(dproj, wgt, x, dxo, gain, scale)


def _in_proj_bwd_w(h, dproj):
    nb, s, _ = h.shape
    tm = _row_tile(s, 1024)
    n_m = s // tm

    def body(h_ref, dp_ref, o_ref, acc_ref):
        b, m = pl.program_id(1), pl.program_id(2)

        @pl.when((b == 0) & (m == 0))
        def _():
            acc_ref[...] = jnp.zeros_like(acc_ref)

        acc_ref[...] += _dot_tn(h_ref[...], dp_ref[...])

        @pl.when((b == nb - 1) & (m == n_m - 1))
        def _():
            o_ref[0] = acc_ref[:, :UNIT].astype(BF16)
            o_ref[1] = acc_ref[:, UNIT:].astype(BF16)

    return pl.pallas_call(
        body, name="in_proj_bwd_w", grid=(N_SEG, nb, n_m),
        in_specs=[pl.BlockSpec((None, tm, D), lambda j, b, m: (b, m, 0)),
                  pl.BlockSpec((None, None, tm, D), lambda j, b, m: (j, b, m, 0))],
        out_specs=pl.BlockSpec((2, D, UNIT), lambda j, b, m: (j, 0, 0)),
        out_shape=jax.ShapeDtypeStruct((2 * N_SEG, D, UNIT), BF16),
        scratch_shapes=[pltpu.VMEM((D, D), F32)],
        compiler_params=_params(("arbitrary", "arbitrary", "arbitrary")),
    )(h, dproj)


def _mod_proj(c_all, w_mod, b_mod_mine):
    nl, _, ncol = w_mod.shape
    nbg = c_all.shape[0]

    def body(c_ref, w_ref, b_ref, o_ref):
        cv = c_ref[...]
        o_ref[...] = jnp.dot(cv * jax.nn.sigmoid(cv), w_ref[...], preferred_element_type=F32,
                             precision=lax.Precision.HIGHEST) + b_ref[...]

    return pl.pallas_call(
        body, name="mod_proj", grid=(nl,),
        in_specs=[pl.BlockSpec((nbg, D), lambda l: (0, 0)), pl.BlockSpec((None, D, ncol), lambda l: (l, 0, 0)),
                  pl.BlockSpec((None, 1, ncol), lambda l: (l, 0, 0))],
        out_specs=pl.BlockSpec((None, nbg, ncol), lambda l: (l, 0, 0)),
        out_shape=jax.ShapeDtypeStruct((nl, nbg, ncol), F32),
        compiler_params=_params(("arbitrary",)),
    )(c_all, w_mod, b_mod_mine)


def _mod_grad(c_all, dmod_all, dmod_mine):
    nl, nbg, ncol = dmod_mine.shape

    def body(c_ref, da_ref, dm_ref, gw_ref, gb_ref):
        cv = c_ref[...]
        gw_ref[...] = lax.dot_general(cv * jax.nn.sigmoid(cv), dm_ref[...], (((0,), (0,)), ((), ())),
                                      preferred_element_type=F32, precision=lax.Precision.HIGHEST)
        gb_ref[...] = _rowsum(da_ref[...])

    return pl.pallas_call(
        body, name="mod_grad", grid=(nl,),
        in_specs=[pl.BlockSpec((nbg, D), lambda l: (0, 0)), pl.BlockSpec((None, nbg, 3 * D), lambda l: (l, 0, 0)),
                  pl.BlockSpec((None, nbg, ncol), lambda l: (l, 0, 0))],
        out_specs=[pl.BlockSpec((None, D, ncol), lambda l: (l, 0, 0)),
                   pl.BlockSpec((None, 1, 3 * D), lambda l: (l, 0, 0))],
        out_shape=[jax.ShapeDtypeStruct((nl, D, ncol), F32), jax.ShapeDtypeStruct((nl, 1, 3 * D), F32)],
        compiler_params=_params(("arbitrary",)),
    )(c_all, dmod_all, dmod_mine)


def _adamw(parts, w, m, v, name):
    n_parts, n_u, n_r, cu = parts.shape
    assert w.shape == (n_r, n_u * cu), (parts.shape, w.shape)
    tr = n_r
    for cand in (512, 256, 128):
        if n_r > cand and n_r % cand == 0:
            tr = cand
            break

    def body(p_ref, w_ref, m_ref, v_ref, g_ref, d_ref, nm_ref, nv_ref):
        g = p_ref[0].astype(F32)
        for k in range(1, n_parts):
            g = g + p_ref[k].astype(F32)
        m2 = ADAM_B1 * m_ref[...] + (1.0 - ADAM_B1) * g
        v2 = ADAM_B2 * v_ref[...] + (1.0 - ADAM_B2) * (g * g)
        m_hat = m2 / (1.0 - ADAM_B1 ** ADAM_STEP)
        v_hat = v2 / (1.0 - ADAM_B2 ** ADAM_STEP)
        g_ref[...] = g
        d_ref[...] = -ADAM_LR * (m_hat / (jnp.sqrt(v_hat) + ADAM_EPS) + ADAM_WD * w_ref[...])
        nm_ref[...] = m2
        nv_ref[...] = v2

    tile = pl.BlockSpec((tr, cu), lambda u, i: (i, u))
    shp = jax.ShapeDtypeStruct(w.shape, F32)
    return pl.pallas_call(
        body, name=name, grid=(n_u, n_r // tr),
        in_specs=[pl.BlockSpec((n_parts, None, tr, cu), lambda u, i: (0, u, i, 0)), tile, tile, tile],
        out_specs=[tile, tile, tile, tile], out_shape=[shp, shp, shp, shp],
        compiler_params=_params(("arbitrary", "arbitrary")),
    )(parts, w, m, v)


def _gathered_cols(g, inner):
    k = len(inner)
    perm = tuple(range(1, k + 1)) + (0, k + 1)
    t = jnp.transpose(g, perm)
    return t.reshape(tuple(inner) + (g.shape[0] * g.shape[-1],))


def _pair_blocks(wh):
    z = jnp.zeros((8, 64, 64), wh.dtype)
    w2 = wh.reshape(8, 2, 64, 64)
    top = jnp.concatenate([w2[:, 0], z], axis=2)
    bot = jnp.concatenate([z, w2[:, 1]], axis=2)
    return jnp.concatenate([top, bot], axis=1).astype(BF16)


def _unpair_blocks(g):
    return jnp.stack([g[:, :64, :64], g[:, 64:, 64:]], axis=1).reshape(16, 64, 64)


FLAT_ROWS = 512


def _pack_flat(arrays):
    flat = jnp.concatenate([a.reshape(-1) for a in arrays])
    n = flat.shape[0]
    per = FLAT_ROWS * LANES
    pad = (-n) % per
    return jnp.pad(flat, (0, pad)).reshape(-1, LANES)


def _unpack_flat(packed, like):
    flat = packed.reshape(-1)
    out, off = [], 0
    for a in like:
        out.append(flat[off:off + a.size].reshape(a.shape))
        off += a.size
    return out


def kernel(x, c, norm_gain, w_mod, b_mod, w_in, w_out, conv_a_w, sgu_w, sgu_b, lru_conv_w, lru_conv_b, lru_wa, lru_ba, lru_wx, lru_bx, lru_lambda, final_gain, loss_target, m_norm_gain, m_w_mod, m_b_mod, m_w_in, m_w_out, m_conv_a_w, m_sgu_w, m_sgu_b, m_lru_conv_w, m_lru_conv_b, m_lru_wa, m_lru_ba, m_lru_wx, m_lru_bx, m_lru_lambda, m_final_gain, v_norm_gain, v_w_mod, v_b_mod, v_w_in, v_w_out, v_conv_a_w, v_sgu_w, v_sgu_b, v_lru_conv_w, v_lru_conv_b, v_lru_wa, v_lru_ba, v_lru_wx, v_lru_bx, v_lru_lambda, v_final_gain):
    nl = w_in.shape[0]
    nb, s, _ = x.shape
    me = _my_index()
    mod_cols = w_mod.shape[2]


    small = jnp.concatenate([c.reshape(-1, LANES), conv_a_w.reshape(-1, LANES), lru_conv_w.reshape(-1, LANES)])
    n_c, n_ca = nb * D // LANES, nl * 3
    n_small = small.shape[0]
    small = jnp.pad(small, ((0, (-n_small) % 8), (0, 0)))
    small_all, _ = _all_gather(small, "gather_small")
    c_all = small_all[:, :n_c].reshape(N_DEV * nb, D)
    conv_a_full = _gathered_cols(small_all[:, n_c:n_c + n_ca].reshape(N_DEV, nl, 3, LANES), (nl, 3))
    lru_conv_full = _gathered_cols(small_all[:, n_c + n_ca:n_small].reshape(N_DEV, nl, 4, LANES), (nl, 4))

    b_mod_mine = lax.dynamic_slice_in_dim(b_mod, me * mod_cols, mod_cols, axis=1)[:, None, :]
    mod_mine = _mod_proj(c_all, w_mod, b_mod_mine)
    mod_all, mod_token = _all_gather(mod_mine.reshape(nl * N_DEV * nb, mod_cols), "gather_mod")
    mod_full = _gathered_cols(mod_all.reshape(N_DEV, nl, N_DEV * nb, mod_cols), (nl, N_DEV * nb))
    mod_loc = lax.dynamic_slice_in_dim(mod_full, me * nb, nb, axis=1)
    shift, scale, gate = [mod_loc[:, :, j * D:(j + 1) * D][:, :, None, :] for j in range(3)]

    def start_w_in(l, token):
        return _split_start(_after(w_in[l], token).astype(BF16), _peers_same_core, False, "gather_w_in_start")

    def gathered_w_in(started, after):
        block, land = _split_wait(started, after, _peers_same_core, False, False, "gather_w_in_wait")
        return _gather_finish(block, land, "gather_w_in_finish")

    sgu_b_lanes = jnp.broadcast_to(sgu_b[..., None], sgu_b.shape + (LANES,))
    mws = []
    for l in range(nl):
        mws.append((conv_a_full[l], sgu_w[l], sgu_b_lanes[l], lru_conv_full[l], lru_conv_b[l][None, :],
                    _pair_blocks(lru_wa[l]), _pair_blocks(lru_wx[l]), lru_ba[l].reshape(1, D),
                    lru_bx[l].reshape(1, D), lru_lambda[l][None, :]))

    xs, hs_bf, projs, mergeds, states, wg = [], [], [], [], [], []
    xl = x
    wg_started = start_w_in(0, mod_token)
    wo_started = _split_start(_after(w_out, wg_started[4]).astype(BF16).reshape(nl * (D // N_DEV), D),
                              _peers_all, False, "gather_w_out_start")
    wo = None
    for l in range(nl):
        h = _norm_mod(xl, _after(norm_gain[l][None, :], wo_started[4]), shift[l], scale[l])
        wg_l, token = gathered_w_in(wg_started, h)
        wg.append(wg_l)
        if l + 1 < nl:
            wg_started = start_w_in(l + 1, token)
            token = wg_started[4]
        proj = _in_proj(h, wg_l, token)
        merged, st = _mixer_fwd(proj, mws[l])
        xs.append(xl), hs_bf.append(h), projs.append(proj), mergeds.append(merged), states.append(st)
        if wo is None:
            _, wo_all = _split_wait(wo_started, merged, _peers_all, False, True, "gather_w_out_wait")
            wo = jnp.transpose(wo_all.reshape(N_DEV, nl, D // N_DEV, D), (1, 0, 2, 3)).reshape(nl, D, D)
        xl = _out_proj(xl, merged, wo[l], gate[l])

    loss_row, dx, g_final = _loss_head(xl, final_gain[None, :], loss_target)
    loss = lax.psum(loss_row[0, 0], ("x", "y", "c"))

    rep_names = ["sgu_w", "sgu_b", "lru_conv_b", "lru_wa", "lru_ba", "lru_wx", "lru_bx", "lru_lambda"]
    rep_w = dict(norm_gain=norm_gain, sgu_w=sgu_w, sgu_b=sgu_b, lru_conv_b=lru_conv_b, lru_wa=lru_wa, lru_ba=lru_ba,
                 lru_wx=lru_wx, lru_bx=lru_bx, lru_lambda=lru_lambda)
    rep_m = dict(norm_gain=m_norm_gain, sgu_w=m_sgu_w, sgu_b=m_sgu_b, lru_conv_b=m_lru_conv_b, lru_wa=m_lru_wa,
                 lru_ba=m_lru_ba, lru_wx=m_lru_wx, lru_bx=m_lru_bx, lru_lambda=m_lru_lambda)
    rep_v = dict(norm_gain=v_norm_gain, sgu_w=v_sgu_w, sgu_b=v_sgu_b, lru_conv_b=v_lru_conv_b, lru_wa=v_lru_wa,
                 lru_ba=v_lru_ba, lru_wx=v_lru_wx, lru_bx=v_lru_bx, lru_lambda=v_lru_lambda)

    def rep_pack(src, l, final):
        tail = final if l == nl - 1 else jnp.zeros_like(final)
        return _pack_flat([src[n][l] for n in rep_names] + [tail])

    rep_like = [rep_w[n][0] for n in rep_names] + [final_gain]
    g_w_in, g_w_out, g_conv, res_rep_l = [None] * nl, [None] * nl, [None] * nl, [None] * nl
    dmods = [None] * nl

    def finish_exchange(pending, after):
        l, h_in, h_out, h_rep = pending
        _, recv = _split_wait(h_in, after, _peers_all, True, True, "scatter_w_in_wait")
        g_w_in[l] = _adamw(recv, w_in[l], m_w_in[l], v_w_in[l], "adamw_w_in")
        _, recv = _split_wait(h_out, after, _peers_all, True, True, "scatter_w_out_wait")
        g_w_out[l] = _adamw(recv, w_out[l], m_w_out[l], v_w_out[l], "adamw_w_out")
        _, recv = _split_wait(h_rep, after, _peers_all, False, True, "gather_rep_wait")
        res = _adamw(recv[:, None], rep_pack(rep_w, l, final_gain), rep_pack(rep_m, l, m_final_gain),
                     rep_pack(rep_v, l, v_final_gain), "adamw_rep")
        res_rep_l[l] = [_unpack_flat(a, rep_like) for a in res]

    pending = None
    g_gains = [None] * nl
    for l in reversed(range(nl)):
        dmerged, gw_out, dgate = _out_proj_bwd(dx, mergeds[l], wo[l], gate[l])
        dproj, g_caw, g_sw, g_sb, g_lcw, g_vec, g_wa, g_wx = _mixer_bwd(projs[l], dmerged, states[l], mws[l])
        gw_in = _in_proj_bwd_w(hs_bf[l], dproj)
        g_conv[l] = (g_caw, g_lcw)
        rep_g = dict(
            sgu_w=g_sw, sgu_b=g_sb[:, :, 0], lru_conv_b=g_vec[0],
            lru_wa=_unpair_blocks(g_wa), lru_ba=g_vec[1].reshape(16, 64), lru_wx=_unpair_blocks(g_wx),
            lru_bx=g_vec[2].reshape(16, 64), lru_lambda=g_vec[3])
        rep_block = _pack_flat([rep_g[n] for n in rep_names]
                               + [g_final[0] if l == nl - 1 else jnp.zeros_like(g_final[0])])
        started = (l,
                   _split_start(gw_in.reshape(N_DEV, UNITS_PER_DEV, D, UNIT), _peers_all, True, "scatter_w_in_start"),
                   _split_start(gw_out.reshape(N_DEV, 1, D // N_DEV, D), _peers_all, True, "scatter_w_out_start"),
                   _split_start(rep_block, _peers_all, False, "gather_rep_start"))
        gain_l = norm_gain[l][None, :]
        for h in started[1:]:
            gain_l = _after(gain_l, h[4])
        dx, dshift, dscale, g_gain = _in_proj_bwd_x(dproj, jnp.transpose(wg[l], (0, 2, 1)), xs[l], dx, gain_l, scale[l])
        g_gains[l] = g_gain
        dmods[l] = jnp.concatenate([dshift, dscale, dgate], axis=2)[:, 0, :]
        if pending is not None:
            finish_exchange(pending, dx)
        pending = started

    conv_parts = jnp.concatenate(
        [jnp.stack([g_conv[l][0] for l in range(nl)]).reshape(nl * 3, N_DEV, LANES),
         jnp.stack([g_conv[l][1] for l in range(nl)]).reshape(nl * 4, N_DEV, LANES)], axis=0)
    conv_parts = jnp.transpose(conv_parts, (1, 0, 2))[:, None]
    conv_recv = _all_to_all(conv_parts, "scatter_conv")

    dmod_loc = jnp.stack(dmods).reshape(nl * nb, 3 * D)
    gain_rows = jnp.pad(jnp.concatenate(g_gains, axis=0), ((0, (-nl) % 8), (0, 2 * D)))
    tail_g, _ = _all_gather(jnp.concatenate([dmod_loc, gain_rows], axis=0), "gather_dmod")
    dmod_g = tail_g[:, :nl * nb]
    gain_parts = tail_g[:, nl * nb:nl * nb + nl, :D][:, None]
    res_gain = _adamw(gain_parts, norm_gain, m_norm_gain, v_norm_gain, "adamw_gain")
    dmod_all = jnp.transpose(dmod_g.reshape(N_DEV, nl, nb, 3 * D), (1, 0, 2, 3)).reshape(nl, N_DEV * nb, 3 * D)
    dmod_mine = lax.dynamic_slice_in_dim(dmod_all, me * mod_cols, mod_cols, axis=2)
    gw_mod, gb_mod = _mod_grad(c_all, dmod_all, dmod_mine)
    res_w_mod = _adamw(gw_mod.reshape(1, 1, nl * D, mod_cols), w_mod.reshape(nl * D, mod_cols),
                       m_w_mod.reshape(nl * D, mod_cols), v_w_mod.reshape(nl * D, mod_cols), "adamw_w_mod")
    res_w_mod = [a.reshape(nl, D, mod_cols) for a in res_w_mod]
    res_b_mod = _adamw(gb_mod.reshape(1, 1, nl, 3 * D), b_mod, m_b_mod, v_b_mod, "adamw_b_mod")
    finish_exchange(pending, res_b_mod[1])

    cat = lambda a, b: jnp.concatenate([a.reshape(nl * 3, LANES), b.reshape(nl * 4, LANES)], axis=0)
    res_conv = _adamw(conv_recv, cat(conv_a_w, lru_conv_w), cat(m_conv_a_w, m_lru_conv_w),
                      cat(v_conv_a_w, v_lru_conv_w), "adamw_conv")
    res_conv_a = [a[:nl * 3].reshape(nl, 3, LANES) for a in res_conv]
    res_lru_conv = [a[nl * 3:].reshape(nl, 4, LANES) for a in res_conv]

    res_rep = []
    for k in range(4):
        d = {n: jnp.stack([res_rep_l[l][k][i] for l in range(nl)]) for i, n in enumerate(rep_names)}
        d["final_gain"] = res_rep_l[nl - 1][k][len(rep_names)]
        res_rep.append(d)

    def stacked(per_layer, k):
        return jnp.stack([per_layer[l][k] for l in range(nl)])

    def leaf(k, name):
        if name == "norm_gain":
            return res_gain[k]
        if name == "w_mod":
            return res_w_mod[k]
        if name == "b_mod":
            return res_b_mod[k]
        if name == "w_in":
            return stacked(g_w_in, k)
        if name == "w_out":
            return stacked(g_w_out, k)
        if name == "conv_a_w":
            return res_conv_a[k]
        if name == "lru_conv_w":
            return res_lru_conv[k]
        return res_rep[k][name]

    order = ["norm_gain", "w_mod", "b_mod", "w_in", "w_out", "conv_a_w", "sgu_w", "sgu_b", "lru_conv_w",
             "lru_conv_b", "lru_wa", "lru_ba", "lru_wx", "lru_bx", "lru_lambda", "final_gain"]
    outs = [loss, dx]
    for k in range(4):
        outs += [leaf(k, n) for n in order]
    return tuple(outs)
```

```python
import functools

import jax
import jax.numpy as jnp
from jax import lax
from jax.experimental import pallas as pl
from jax.experimental.pallas import tpu as pltpu

F32 = jnp.float32
BF16 = jnp.bfloat16

D = 1024
N_DEV = 8
N_SEG = 12
LANES = 128
SUBLANES = 8
CHUNK = 128
HALO = 16
UNIT = 512
UNITS_PER_DEV = 3
EPS = 1e-6
LRU_C = 8.0
ADAM_LR, ADAM_B1, ADAM_B2, ADAM_EPS, ADAM_WD, ADAM_STEP = 0.001, 0.9, 0.999, 1e-08, 0.01, 10
VMEM_LIMIT = 56 * 1024 * 1024

AX, AB, AC, AZ, SU, SV, SZ, RX, RZ, GA, GS, GR = range(N_SEG)
MESH = pl.DeviceIdType.MESH


def _params(sem=None):
    return pltpu.CompilerParams(dimension_semantics=sem, vmem_limit_bytes=VMEM_LIMIT)


def _my_index():
    return 4 * lax.axis_index("x") + 2 * lax.axis_index("y") + lax.axis_index("c")


def _all_gather(block, name):
    def body(x_ref, out_ref, token, send_sems, recv_sems, local_sem):
        x, y, c = lax.axis_index("x"), lax.axis_index("y"), lax.axis_index("c")
        me, sibling = (x, y, c), (x, y, 1 - c)
        chips = [(1 - x, y), (x, 1 - y), (1 - x, 1 - y)]
        token[...] = jnp.zeros_like(token)

        def rows(px, py, pc):
            return out_ref.at[4 * px + 2 * py + pc]

        def copy(k, blk, to, src=None):
            return pltpu.make_async_remote_copy(
                src_ref=rows(*blk) if src is None else src, dst_ref=rows(*blk),
                send_sem=send_sems.at[k], recv_sem=recv_sems.at[k], device_id=to, device_id_type=MESH)

        mine = pltpu.make_async_copy(x_ref, rows(*me), local_sem)
        mine.start()
        first = [copy(0, me, sibling, src=x_ref)]
        first += [copy(1 + j, me, (*chip, c), src=x_ref) for j, chip in enumerate(chips)]
        for cp in first:
            cp.start()
        passed = [copy(4 + j, (*chip, c), sibling) for j, chip in enumerate(chips)]
        for j, chip in enumerate(chips):
            copy(1 + j, (*chip, c), me).wait_recv()
            passed[j].start()
        copy(0, sibling, me).wait_recv()
        for j, chip in enumerate(chips):
            copy(4 + j, (*chip, 1 - c), me).wait_recv()
        for cp in first + passed:
            cp.wait_send()
        mine.wait()

    return pl.pallas_call(
        body, name=name,
        out_shape=[jax.ShapeDtypeStruct((N_DEV,) + block.shape, block.dtype), jax.ShapeDtypeStruct((8, LANES), F32)],
        in_specs=[pl.BlockSpec(memory_space=pltpu.VMEM)],
        out_specs=[pl.BlockSpec(memory_space=pl.ANY), pl.BlockSpec(memory_space=pltpu.VMEM)],
        scratch_shapes=[pltpu.SemaphoreType.DMA((7,)), pltpu.SemaphoreType.DMA((7,)), pltpu.SemaphoreType.DMA],
    )(block)


def _all_to_all(blocks, name):
    def body(x_ref, out_ref, send_sems, recv_sems, local_sem):
        x, y, c = lax.axis_index("x"), lax.axis_index("y"), lax.axis_index("c")
        my = 4 * x + 2 * y + c
        mine = pltpu.make_async_copy(x_ref.at[my], out_ref.at[my], local_sem)
        mine.start()
        peers = []
        for r in range(1, N_DEV):
            px = 1 - x if r & 4 else x
            py = 1 - y if r & 2 else y
            pc = 1 - c if r & 1 else c
            peers.append((r - 1, 4 * px + 2 * py + pc, (px, py, pc)))

        def copy(k, src_slot, dst_slot, to):
            return pltpu.make_async_remote_copy(
                src_ref=x_ref.at[src_slot], dst_ref=out_ref.at[dst_slot],
                send_sem=send_sems.at[k], recv_sem=recv_sems.at[k], device_id=to, device_id_type=MESH)

        sends = [copy(k, pid, my, to) for k, pid, to in peers]
        for cp in sends:
            cp.start()
        for k, pid, to in peers:
            copy(k, pid, pid, to).wait_recv()
        for cp in sends:
            cp.wait_send()
        mine.wait()

    return pl.pallas_call(
        body, name=name,
        out_shape=jax.ShapeDtypeStruct(blocks.shape, blocks.dtype),
        in_specs=[pl.BlockSpec(memory_space=pltpu.VMEM)],
        out_specs=pl.BlockSpec(memory_space=pl.ANY),
        scratch_shapes=[pltpu.SemaphoreType.DMA((7,)), pltpu.SemaphoreType.DMA((7,)), pltpu.SemaphoreType.DMA],
    )(blocks)


_HBM = pl.BlockSpec(memory_space=pltpu.HBM)
_SEM = pl.BlockSpec(memory_space=pltpu.SEMAPHORE)
_EFFECT = pltpu.SideEffectType.DATAFLOW_SIDE_EFFECTING


def _peers_all(x, y, c):
    out = []
    for r in range(1, N_DEV):
        px = 1 - x if r & 4 else x
        py = 1 - y if r & 2 else y
        pc = 1 - c if r & 1 else c
        out.append((r - 1, 4 * px + 2 * py + pc, (px, py, pc)))
    return out


def _peers_same_core(x, y, c):
    return [(k, 4 * px + 2 * py + c, (px, py, c))
            for k, (px, py) in enumerate([(1 - x, y), (x, 1 - y), (1 - x, 1 - y)])]


def _split_start(src, peers_fn, scatter, name):
    blk = src.shape[1:] if scatter else src.shape
    land_shape = (N_DEV,) + tuple(blk)
    n = len(peers_fn(0, 0, 0))

    def body(x_ref, land_ref, send_sems, recv_sems, x_thru, land_thru, token):
        x, y, c = lax.axis_index("x"), lax.axis_index("y"), lax.axis_index("c")
        my = 4 * x + 2 * y + c
        for k, pid, to in peers_fn(x, y, c):
            pltpu.make_async_remote_copy(
                src_ref=x_ref.at[pid] if scatter else x_ref, dst_ref=land_ref.at[my],
                send_sem=send_sems.at[k], recv_sem=recv_sems.at[k], device_id=to, device_id_type=MESH).start()
        token[...] = jnp.zeros_like(token)

    return pl.pallas_call(
        body, name=name,
        out_shape=(pltpu.SemaphoreType.DMA((n,)), pltpu.SemaphoreType.DMA((n,)),
                   pltpu.HBM(src.shape, src.dtype), pltpu.HBM(land_shape, src.dtype),
                   jax.ShapeDtypeStruct((8, LANES), F32)),
        in_specs=(_HBM, _HBM),
        out_specs=(_SEM, _SEM, _HBM, _HBM, pl.BlockSpec(memory_space=pltpu.VMEM)),
        input_output_aliases={0: 2, 1: 3},
        compiler_params=pltpu.CompilerParams(has_side_effects=_EFFECT),
    )(pltpu.with_memory_space_constraint(src, pltpu.HBM),
      pltpu.with_memory_space_constraint(lax.empty(land_shape, src.dtype), pltpu.HBM))


def _split_wait(handles, after, peers_fn, scatter, own, name):
    send_sems, recv_sems, src_thru, land_thru, _ = handles
    blk = land_thru.shape[1:]

    def body(x_ref, land_ref, send_sems, recv_sems, after_ref, x_dead, got_ref, *stage):
        x, y, c = lax.axis_index("x"), lax.axis_index("y"), lax.axis_index("c")
        if own:
            my = 4 * x + 2 * y + c
            mine = _staged_copy(x_ref.at[my] if scatter else x_ref, land_ref.at[my], *stage)
        for k, pid, to in peers_fn(x, y, c):
            cp = pltpu.make_async_remote_copy(
                src_ref=x_ref.at[pid] if scatter else x_ref, dst_ref=land_ref.at[pid],
                send_sem=send_sems.at[k], recv_sem=recv_sems.at[k], device_id=to, device_id_type=MESH)
            cp.wait_send()
            cp.wait_recv()
        if own:
            mine.wait()

    return pl.pallas_call(
        body, name=name,
        out_shape=(pltpu.HBM(src_thru.shape, src_thru.dtype), pltpu.HBM(land_thru.shape, land_thru.dtype)),
        in_specs=(_HBM, _HBM, _SEM, _SEM, pl.BlockSpec(memory_space=pl.ANY)),
        out_specs=(_HBM, _HBM),
        input_output_aliases={0: 0, 1: 1},
        scratch_shapes=[pltpu.VMEM(blk, land_thru.dtype), pltpu.SemaphoreType.DMA((2,))] if own else [],
        compiler_params=pltpu.CompilerParams(has_side_effects=_EFFECT, vmem_limit_bytes=VMEM_LIMIT),
    )(src_thru, land_thru, send_sems, recv_sems, after)


def _staged_copy(src_ref, dst_ref, buf, sems):
    leg = pltpu.make_async_copy(src_ref, buf, sems.at[0])
    leg.start()
    leg.wait()
    leg = pltpu.make_async_copy(buf, dst_ref, sems.at[1])
    leg.start()
    return leg


def _gather_finish(block, land, name):
    def body(x_ref, land_ref, out_ref, token, send_sems, recv_sems, buf, local_sems):
        x, y, c = lax.axis_index("x"), lax.axis_index("y"), lax.axis_index("c")
        my, sib_id, sibling = 4 * x + 2 * y + c, 4 * x + 2 * y + 1 - c, (x, y, 1 - c)
        token[...] = jnp.zeros_like(token)

        def copy(k, slot, src=None):
            return pltpu.make_async_remote_copy(
                src_ref=land_ref.at[slot] if src is None else src, dst_ref=out_ref.at[slot],
                send_sem=send_sems.at[k], recv_sem=recv_sems.at[k], device_id=sibling, device_id_type=MESH)

        chips = _peers_same_core(x, y, c)
        sends = [copy(0, my, src=x_ref)] + [copy(1 + k, pid) for k, pid, _ in chips]
        for cp in sends:
            cp.start()
        mine = _staged_copy(x_ref, out_ref.at[my], buf, local_sems)
        copy(0, sib_id).wait_recv()
        for k, pid, _ in chips:
            copy(1 + k, pid + 1 - 2 * c).wait_recv()
        for cp in sends:
            cp.wait_send()
        mine.wait()

    return pl.pallas_call(
        body, name=name,
        out_shape=[jax.ShapeDtypeStruct(land.shape, land.dtype), jax.ShapeDtypeStruct((8, LANES), F32)],
        in_specs=[pl.BlockSpec(memory_space=pl.ANY), pl.BlockSpec(memory_space=pl.ANY)],
        out_specs=[pl.BlockSpec(memory_space=pl.ANY), pl.BlockSpec(memory_space=pltpu.VMEM)],
        input_output_aliases={1: 0},
        scratch_shapes=[pltpu.SemaphoreType.DMA((4,)), pltpu.SemaphoreType.DMA((4,)),
                        pltpu.VMEM(block.shape, block.dtype), pltpu.SemaphoreType.DMA((2,))],
        compiler_params=pltpu.CompilerParams(vmem_limit_bytes=VMEM_LIMIT),
    )(block, land)


def _after(v, token):
    return v + token[0, 0].astype(v.dtype)


def _dsilu(x, s):
    return s * (1.0 + x * (1.0 - s))


def _log1p(x):
    u = 1.0 + x
    d = u - 1.0
    return jnp.where(d == 0.0, x, jnp.log(u) * (x / jnp.where(d == 0.0, 1.0, d)))


def _softplus_neg(lam):
    return jnp.maximum(-lam, 0.0) + _log1p(jnp.exp(-jnp.abs(lam)))


def _neg_expm1(y):
    poly = -y * (1.0 + y * (0.5 + y * (1.0 / 6.0 + y * (1.0 / 24.0 + y * (1.0 / 120.0)))))
    return jnp.where(y > -0.1, poly, 1.0 - jnp.exp(y))


def _shift_dn(cur, prev, k):
    ext = jnp.concatenate([prev, cur], axis=0)
    return pltpu.roll(ext, k, 0)[HALO:, :]


def _shift_up(cur, nxt, k):
    n = cur.shape[0]
    ext = jnp.concatenate([cur, nxt], axis=0)
    return pltpu.roll(ext, n + HALO - k, 0)[:n, :]


def _scan_fwd(a, b, h_prev):
    groups = a.shape[0] // SUBLANES
    a3 = a.reshape(groups, SUBLANES, LANES)
    b3 = b.reshape(groups, SUBLANES, LANES)
    row = lax.broadcasted_iota(jnp.int32, a3.shape, 1)
    k = 1
    while k < SUBLANES:
        a_sh = jnp.where(row >= k, pltpu.roll(a3, k, 1), 1.0)
        b_sh = jnp.where(row >= k, pltpu.roll(b3, k, 1), 0.0)
        b3 = a3 * b_sh + b3
        a3 = a3 * a_sh
        k *= 2
    carry = h_prev[HALO - 1:HALO, :]
    out = []
    for i in range(groups):
        hg = b3[i] + a3[i] * carry
        out.append(hg)
        carry = hg[SUBLANES - 1:SUBLANES, :]
    return jnp.concatenate(out, axis=0)


def _scan_rev(a_next, g, lam_next):
    groups = g.shape[0] // SUBLANES
    a3 = a_next.reshape(groups, SUBLANES, LANES)
    g3 = g.reshape(groups, SUBLANES, LANES)
    row = lax.broadcasted_iota(jnp.int32, a3.shape, 1)
    k = 1
    while k < SUBLANES:
        ok = row < SUBLANES - k
        a_sh = jnp.where(ok, pltpu.roll(a3, SUBLANES - k, 1), 1.0)
        g_sh = jnp.where(ok, pltpu.roll(g3, SUBLANES - k, 1), 0.0)
        g3 = g3 + a3 * g_sh
        a3 = a3 * a_sh
        k *= 2
    carry = lam_next[0:1, :]
    out = [None] * groups
    for i in reversed(range(groups)):
        lg = g3[i] + a3[i] * carry
        out[i] = lg
        carry = lg[0:1, :]
    return jnp.concatenate(out, axis=0)


def _rowsum(v):
    return jnp.sum(v, axis=0, keepdims=True)


def _dot(a, b):
    return jnp.dot(a, b, preferred_element_type=F32)


def _dot_nt(a, b):
    return lax.dot_general(a, b, (((1,), (1,)), ((), ())), preferred_element_type=F32)


def _dot_tn(a, b):
    return lax.dot_general(a, b, (((0,), (0,)), ((), ())), preferred_element_type=F32)


class _MixerWeights:
    def __init__(self, caw_ref, sw_ref, sb_ref, lcw_ref, lcb_ref, wa_ref, wx_ref, ba_ref, bx_ref, lam_ref):
        self.caw = [caw_ref[j:j + 1, :] for j in range(3)]
        self.lcw = [lcw_ref[j:j + 1, :] for j in range(4)]
        self.lcb = lcb_ref[...]
        row = lax.broadcasted_iota(jnp.int32, (CHUNK, CHUNK), 0)
        col = lax.broadcasted_iota(jnp.int32, (CHUNK, CHUNK), 1)
        self.tril = col <= row
        self.sw = jnp.where(self.tril, sw_ref[...], 0.0).astype(BF16)
        self.sb = sb_ref[...]
        self.wa = wa_ref[...]
        self.wx = wx_ref[...]
        self.ba = ba_ref[...]
        self.bx = bx_ref[...]
        lam = lam_ref[...]
        self.sp = _softplus_neg(lam)
        self.dsp_dlam = -jax.nn.sigmoid(-lam)


def _mixer_pre_scan(ld, ldp, w):
    t = {}
    a_x, a_c = ld(AX), ld(AC)
    t["a_x"], t["a_c"], t["a_b"], t["a_z"] = a_x, a_c, ld(AB), ld(AZ)
    ca = a_c * a_x
    ca_p = ldp(AC) * ldp(AX)
    t["ca"], t["ca1"], t["ca2"] = ca, _shift_dn(ca, ca_p, 1), _shift_dn(ca, ca_p, 2)
    t["cv"] = w.caw[2] * ca + w.caw[1] * t["ca1"] + w.caw[0] * t["ca2"]
    t["sa"] = jax.nn.sigmoid(t["a_z"])
    t["silu_az"] = t["a_z"] * t["sa"]
    t["y_a"] = t["silu_az"] * t["a_b"] * t["cv"]

    v = ld(SV)
    vc = v - jnp.mean(v, axis=1, keepdims=True)
    t["rstd"] = lax.rsqrt(jnp.mean(vc * vc, axis=1, keepdims=True) + EPS)
    t["vn"] = vc * t["rstd"]
    t["z"] = _dot(w.sw, t["vn"].astype(BF16)) + w.sb
    t["s_u"], t["s_z"] = ld(SU), ld(SZ)
    t["ss"] = jax.nn.sigmoid(t["s_z"])
    t["silu_sz"] = t["s_z"] * t["ss"]
    t["y_s"] = t["silu_sz"] * t["s_u"] * t["z"]

    r_x, r_xp = ld(RX), ldp(RX)
    t["rx"] = [_shift_dn(r_x, r_xp, 3), _shift_dn(r_x, r_xp, 2), _shift_dn(r_x, r_xp, 1), r_x]
    xc = w.lcb + w.lcw[0] * t["rx"][0] + w.lcw[1] * t["rx"][1] + w.lcw[2] * t["rx"][2] + w.lcw[3] * r_x
    t["xc"] = xc
    xcb = xc.astype(BF16)
    t["r"] = jax.nn.sigmoid(_dot(xcb, w.wa) + w.ba)
    t["i"] = jax.nn.sigmoid(_dot(xcb, w.wx) + w.bx)
    la = -LRU_C * t["r"] * w.sp
    t["a"] = jnp.exp(la)
    t["em"] = _neg_expm1(2.0 * la)
    t["mult"] = jnp.sqrt(t["em"])
    t["b"] = t["mult"] * (t["i"] * xc)
    t["r_z"] = ld(RZ)
    t["sr"] = jax.nn.sigmoid(t["r_z"])
    t["silu_rz"] = t["r_z"] * t["sr"]
    t["ga"], t["gs"], t["gr"] = jax.nn.sigmoid(ld(GA)), jax.nn.sigmoid(ld(GS)), jax.nn.sigmoid(ld(GR))
    return t


def _weight_specs(n_cb_axis):
    def at(fn):
        return lambda *g: fn(g[n_cb_axis])
    return [
        pl.BlockSpec((3, LANES), at(lambda cb: (0, cb))),
        pl.BlockSpec((None, CHUNK, CHUNK), at(lambda cb: (cb, 0, 0))),
        pl.BlockSpec((None, CHUNK, LANES), at(lambda cb: (cb, 0, 0))),
        pl.BlockSpec((4, LANES), at(lambda cb: (0, cb))),
        pl.BlockSpec((1, LANES), at(lambda cb: (0, cb))),
        pl.BlockSpec((None, LANES, LANES), at(lambda cb: (cb, 0, 0))),
        pl.BlockSpec((None, LANES, LANES), at(lambda cb: (cb, 0, 0))),
        pl.BlockSpec((1, LANES), at(lambda cb: (0, cb))),
        pl.BlockSpec((1, LANES), at(lambda cb: (0, cb))),
        pl.BlockSpec((1, LANES), at(lambda cb: (0, cb))),
    ]


def _chunk_loaders(p_ref, c):
    r0 = pl.multiple_of(c * CHUNK, CHUNK)
    rp = pl.multiple_of(jnp.maximum(c * CHUNK - HALO, 0), HALO)

    def ld(j):
        return p_ref[j, pl.ds(r0, CHUNK), :].astype(F32)

    def ldp(j):
        return jnp.where(c > 0, p_ref[j, pl.ds(rp, HALO), :].astype(F32), 0.0)

    return r0, rp, ld, ldp


def _mixer_fwd(proj, mw):
    _, nb, s, _ = proj.shape
    n_chunks = s // CHUNK

    def body(p_ref, *refs):
        w = _MixerWeights(*refs[:10])
        merged_ref, hs_ref = refs[10:]

        def chunk(c, h_prev):
            r0, _, ld, ldp = _chunk_loaders(p_ref, c)
            t = _mixer_pre_scan(ld, ldp, w)
            h = _scan_fwd(t["a"], t["b"], h_prev)
            y_r = t["silu_rz"] * h
            merged = t["ga"] * t["y_a"] + t["gs"] * t["y_s"] + t["gr"] * y_r
            merged_ref[pl.ds(r0, CHUNK), :] = merged.astype(BF16)
            hs_ref[pl.ds(r0, CHUNK), :] = h
            return h[CHUNK - HALO:, :]

        lax.fori_loop(0, n_chunks, chunk, jnp.zeros((HALO, LANES), F32))

    slab = pl.BlockSpec((None, s, LANES), lambda cb, b: (b, 0, cb))
    return pl.pallas_call(
        body, name="mixer_fwd", grid=(D // LANES, nb),
        in_specs=[pl.BlockSpec((N_SEG, None, s, LANES), lambda cb, b: (0, b, 0, cb))] + _weight_specs(0),
        out_specs=[slab, slab],
        out_shape=[jax.ShapeDtypeStruct((nb, s, D), BF16), jax.ShapeDtypeStruct((nb, s, D), F32)],
        compiler_params=_params(("arbitrary", "arbitrary")),
    )(proj, *mw)


def _mixer_bwd(proj, dmerged, hs, mw):
    _, nb, s, _ = proj.shape
    n_chunks = s // CHUNK

    def body(p_ref, dm_ref, hs_ref, *refs):
        w = _MixerWeights(*refs[:10])
        dp_ref, g_caw, g_sw, g_sb, g_lcw, g_vec, g_wa, g_wx = refs[10:]

        @pl.when(pl.program_id(1) == 0)
        def _():
            for ref in (g_caw, g_sw, g_sb, g_lcw, g_vec, g_wa, g_wx):
                ref[...] = jnp.zeros_like(ref)

        def chunk(i, carry):
            dcv_n, dxc_n, lam_n, a_n = carry
            c = n_chunks - 1 - i
            r0, rp, ld, ldp = _chunk_loaders(p_ref, c)
            t = _mixer_pre_scan(ld, ldp, w)
            h = hs_ref[pl.ds(r0, CHUNK), :]
            h_p = jnp.where(c > 0, hs_ref[pl.ds(rp, HALO), :], 0.0)
            h_prev = _shift_dn(h, h_p, 1)
            dm = dm_ref[pl.ds(r0, CHUNK), :].astype(F32)
            y_r = t["silu_rz"] * h

            def out(j, val):
                dp_ref[j, pl.ds(r0, CHUNK), :] = val.astype(BF16)

            ga, gs, gr = t["ga"], t["gs"], t["gr"]
            out(GA, dm * t["y_a"] * ga * (1.0 - ga))
            out(GS, dm * t["y_s"] * gs * (1.0 - gs))
            out(GR, dm * y_r * gr * (1.0 - gr))

            dy_a = dm * ga
            out(AZ, dy_a * t["a_b"] * t["cv"] * _dsilu(t["a_z"], t["sa"]))
            out(AB, dy_a * t["silu_az"] * t["cv"])
            dcv = dy_a * t["silu_az"] * t["a_b"]
            dca = w.caw[2] * dcv + w.caw[1] * _shift_up(dcv, dcv_n, 1) + w.caw[0] * _shift_up(dcv, dcv_n, 2)
            out(AC, dca * t["a_x"])
            out(AX, dca * t["a_c"])
            g_caw[2:3, :] += _rowsum(dcv * t["ca"])
            g_caw[1:2, :] += _rowsum(dcv * t["ca1"])
            g_caw[0:1, :] += _rowsum(dcv * t["ca2"])

            dy_s = dm * gs
            out(SZ, dy_s * t["s_u"] * t["z"] * _dsilu(t["s_z"], t["ss"]))
            out(SU, dy_s * t["silu_sz"] * t["z"])
            dz = dy_s * t["silu_sz"] * t["s_u"]
            dzb = dz.astype(BF16)
            g_sb[...] += jnp.broadcast_to(jnp.sum(dz, axis=1, keepdims=True), (CHUNK, LANES))
            g_sw[...] += _dot_nt(dzb, t["vn"].astype(BF16))
            dvn = _dot_tn(w.sw, dzb)
            vn = t["vn"]
            out(SV, t["rstd"] * (dvn - jnp.mean(dvn, axis=1, keepdims=True)
                                 - vn * jnp.mean(dvn * vn, axis=1, keepdims=True)))

            dy_r = dm * gr
            out(RZ, dy_r * h * _dsilu(t["r_z"], t["sr"]))
            lam = _scan_rev(_shift_up(t["a"], a_n, 1), dy_r * t["silu_rz"], lam_n)
            a, r, ig, xc, mult = t["a"], t["r"], t["i"], t["xc"], t["mult"]
            d_i = lam * mult * xc
            d_mult = lam * ig * xc
            dxc = lam * mult * ig
            dla = lam * h_prev * a - d_mult * ((1.0 - t["em"]) / mult)
            g_vec[3:4, :] += _rowsum(dla * (-LRU_C * r)) * w.dsp_dlam
            dpr = (dla * (-LRU_C * w.sp)) * r * (1.0 - r)
            dpi = d_i * ig * (1.0 - ig)
            dprb, dpib, xcb = dpr.astype(BF16), dpi.astype(BF16), xc.astype(BF16)
            g_wa[...] += _dot_tn(xcb, dprb)
            g_wx[...] += _dot_tn(xcb, dpib)
            g_vec[1:2, :] += _rowsum(dpr)
            g_vec[2:3, :] += _rowsum(dpi)
            dxc = dxc + _dot_nt(dprb, w.wa) + _dot_nt(dpib, w.wx)
            g_vec[0:1, :] += _rowsum(dxc)
            out(RX, w.lcw[3] * dxc + w.lcw[2] * _shift_up(dxc, dxc_n, 1)
                + w.lcw[1] * _shift_up(dxc, dxc_n, 2) + w.lcw[0] * _shift_up(dxc, dxc_n, 3))
            for j in range(4):
                g_lcw[j:j + 1, :] += _rowsum(dxc * t["rx"][j])
            return dcv[:HALO, :], dxc[:HALO, :], lam[:HALO, :], a[:HALO, :]

        zero = jnp.zeros((HALO, LANES), F32)
        lax.fori_loop(0, n_chunks, chunk, (zero, zero, zero, zero))

        @pl.when(pl.program_id(1) == nb - 1)
        def _():
            g_sw[...] = jnp.where(w.tril, g_sw[...], 0.0)

    slab = lambda dt: pl.BlockSpec((None, s, LANES), lambda cb, b: (b, 0, cb))
    seg = pl.BlockSpec((N_SEG, None, s, LANES), lambda cb, b: (0, b, 0, cb))
    rows = lambda n: pl.BlockSpec((n, LANES), lambda cb, b: (0, cb))
    sq = pl.BlockSpec((None, LANES, LANES), lambda cb, b: (cb, 0, 0))
    n_cb = D // LANES
    return pl.pallas_call(
        body, name="mixer_bwd", grid=(n_cb, nb),
        in_specs=[seg, slab(BF16), slab(F32)] + _weight_specs(0),
        out_specs=[seg, rows(3), sq, sq, rows(4), rows(8), sq, sq],
        out_shape=[
            jax.ShapeDtypeStruct(proj.shape, BF16),
            jax.ShapeDtypeStruct((3, D), F32),
            jax.ShapeDtypeStruct((n_cb, CHUNK, CHUNK), F32),
            jax.ShapeDtypeStruct((n_cb, CHUNK, LANES), F32),
            jax.ShapeDtypeStruct((4, D), F32),
            jax.ShapeDtypeStruct((8, D), F32),
            jax.ShapeDtypeStruct((n_cb, LANES, LANES), F32),
            jax.ShapeDtypeStruct((n_cb, LANES, LANES), F32),
        ],
        compiler_params=_params(("arbitrary", "arbitrary")),
    )(proj, dmerged, hs, *mw)


def _row_tile(s, want):
    return want if s % want == 0 else s


def _norm_mod(x, gain, shift, scale):
    nb, s, _ = x.shape
    tm = _row_tile(s, 512)

    def body(x_ref, g_ref, sh_ref, sc_ref, h_ref):
        xv = x_ref[...]
        r = lax.rsqrt(jnp.mean(xv * xv, axis=1, keepdims=True) + EPS)
        h_ref[...] = ((xv * r) * g_ref[...] * (1.0 + sc_ref[...]) + sh_ref[...]).astype(BF16)

    tile = pl.BlockSpec((None, tm, D), lambda b, m: (b, m, 0))
    vec = pl.BlockSpec((None, 1, D), lambda b, m: (b, 0, 0))
    return pl.pallas_call(
        body, name="norm_mod", grid=(nb, s // tm),
        in_specs=[tile, pl.BlockSpec((1, D), lambda b, m: (0, 0)), vec, vec],
        out_specs=tile, out_shape=jax.ShapeDtypeStruct(x.shape, BF16),
        compiler_params=_params(("arbitrary", "arbitrary")),
    )(x, gain, shift, scale)


def _in_proj(h, wg, dep):
    nb, s, _ = h.shape

    def body(h_ref, w_ref, dep_ref, o_ref):
        o_ref[...] = _dot(h_ref[...], w_ref[...]).astype(BF16)

    return pl.pallas_call(
        body, name="in_proj", grid=(nb, N_DEV * UNITS_PER_DEV),
        in_specs=[pl.BlockSpec((None, s, D), lambda b, u: (b, 0, 0)),
                  pl.BlockSpec((None, D, UNIT), lambda b, u: (u // UNITS_PER_DEV, 0, u % UNITS_PER_DEV)),
                  pl.BlockSpec((8, LANES), lambda b, u: (0, 0))],
        out_specs=pl.BlockSpec((None, None, s, UNIT), lambda b, u: (u // 2, b, 0, u % 2)),
        out_shape=jax.ShapeDtypeStruct((N_SEG, nb, s, D), BF16),
        compiler_params=_params(("arbitrary", "arbitrary")),
    )(h, wg, dep)


def _out_proj(x, merged, wout, gate):
    nb, s, _ = x.shape
    tm = _row_tile(s, 512)

    def body(x_ref, m_ref, w_ref, g_ref, o_ref):
        o_ref[...] = x_ref[...] + g_ref[...] * _dot(m_ref[...], w_ref[...])

    tile = pl.BlockSpec((None, tm, D), lambda b, m: (b, m, 0))
    return pl.pallas_call(
        body, name="out_proj", grid=(nb, s // tm),
        in_specs=[tile, tile, pl.BlockSpec((D, D), lambda b, m: (0, 0)),
                  pl.BlockSpec((None, 1, D), lambda b, m: (b, 0, 0))],
        out_specs=tile, out_shape=jax.ShapeDtypeStruct(x.shape, F32),
        compiler_params=_params(("arbitrary", "arbitrary")),
    )(x, merged, wout, gate)


def _loss_head(x, gain, target):
    nb, s, _ = x.shape
    tm = _row_tile(s, 512)

    def body(x_ref, g_ref, t_ref, loss_ref, dx_ref, dg_ref):
        first = (pl.program_id(0) == 0) & (pl.program_id(1) == 0)
        last = (pl.program_id(0) == nb - 1) & (pl.program_id(1) == s // tm - 1)

        @pl.when(first)
        def _():
            loss_ref[...] = jnp.zeros_like(loss_ref)
            dg_ref[...] = jnp.zeros_like(dg_ref)

        xv = x_ref[...]
        r = lax.rsqrt(jnp.mean(xv * xv, axis=1, keepdims=True) + EPS)
        xn = xv * r
        g = g_ref[...]
        e = xn * g - t_ref[...]
        loss_ref[...] += _rowsum(e * e) * (0.5 / D)
        dy = e * (1.0 / D)
        dg_ref[...] += _rowsum(dy * xn)
        dxn = dy * g
        dx_ref[...] = r * (dxn - xn * jnp.mean(dxn * xn, axis=1, keepdims=True))

        @pl.when(last)
        def _():
            loss_ref[...] = jnp.broadcast_to(jnp.sum(loss_ref[...], axis=1, keepdims=True), (1, D))

    tile = pl.BlockSpec((None, tm, D), lambda b, m: (b, m, 0))
    vec = pl.BlockSpec((1, D), lambda b, m: (0, 0))
    return pl.pallas_call(
        body, name="loss_head", grid=(nb, s // tm),
        in_specs=[tile, vec, tile], out_specs=[vec, tile, vec],
        out_shape=[jax.ShapeDtypeStruct((1, D), F32), jax.ShapeDtypeStruct(x.shape, F32),
                   jax.ShapeDtypeStruct((1, D), F32)],
        compiler_params=_params(("arbitrary", "arbitrary")),
    )(x, gain, target)


def _out_proj_bwd(dxo, merged, wout, gate):
    nb, s, _ = dxo.shape
    tm = _row_tile(s, 512)

    def body(d_ref, m_ref, w_ref, g_ref, dm_ref, gw_ref, dg_ref):
        @pl.when((pl.program_id(0) == 0) & (pl.program_id(1) == 0))
        def _():
            gw_ref[...] = jnp.zeros_like(gw_ref)

        @pl.when(pl.program_id(1) == 0)
        def _():
            dg_ref[...] = jnp.zeros_like(dg_ref)

        d = d_ref[...]
        m = m_ref[...]
        wv = w_ref[...]
        dg_ref[...] += _rowsum(d * _dot(m, wv))
        dout = (d * g_ref[...]).astype(BF16)
        dm_ref[...] = _dot_nt(dout, wv).astype(BF16)
        gw_ref[...] += _dot_tn(m, dout)

    tile = pl.BlockSpec((None, tm, D), lambda b, m: (b, m, 0))
    vec = pl.BlockSpec((None, 1, D), lambda b, m: (b, 0, 0))
    full = pl.BlockSpec((D, D), lambda b, m: (0, 0))
    return pl.pallas_call(
        body, name="out_proj_bwd", grid=(nb, s // tm),
        in_specs=[tile, tile, full, vec], out_specs=[tile, full, vec],
        out_shape=[jax.ShapeDtypeStruct(dxo.shape, BF16), jax.ShapeDtypeStruct((D, D), F32),
                   jax.ShapeDtypeStruct((nb, 1, D), F32)],
        compiler_params=_params(("arbitrary", "arbitrary")),
    )(dxo, merged, wout, gate)


def _in_proj_bwd_h(dproj, wg, dep):
    _, nb, s, _ = dproj.shape
    n_u = N_DEV * UNITS_PER_DEV

    def body(dp_ref, w_ref, dep_ref, dh_ref):
        u = pl.program_id(1)
        part = _dot_nt(dp_ref[...], w_ref[...])

        @pl.when(u == 0)
        def _():
            dh_ref[...] = part

        @pl.when(u > 0)
        def _():
            dh_ref[...] += part

    return pl.pallas_call(
        body, name="in_proj_bwd_h", grid=(nb, n_u),
        in_specs=[pl.BlockSpec((None, None, s, UNIT), lambda b, u: (u // 2, b, 0, u % 2)),
                  pl.BlockSpec((None, D, UNIT), lambda b, u: (u // UNITS_PER_DEV, 0, u % UNITS_PER_DEV)),
                  pl.BlockSpec((8, LANES), lambda b, u: (0, 0))],
        out_specs=pl.BlockSpec((None, s, D), lambda b, u: (b, 0, 0)),
        out_shape=jax.ShapeDtypeStruct((nb, s, D), F32),
        compiler_params=_params(("arbitrary", "arbitrary")),
    )(dproj, wg, dep)


def _norm_mod_bwd(dh, x, dxo, gain, scale):
    nb, s, _ = x.shape
    tm = _row_tile(s, 512)

    def body(dh_ref, x_ref, dxo_ref, g_ref, sc_ref, dx_ref, dsh_ref, dsc_ref, dg_ref):
        b, m = pl.program_id(0), pl.program_id(1)

        @pl.when((b == 0) & (m == 0))
        def _():
            dg_ref[...] = jnp.zeros_like(dg_ref)

        @pl.when(m == 0)
        def _():
            dsh_ref[...] = jnp.zeros_like(dsh_ref)
            dsc_ref[...] = jnp.zeros_like(dsc_ref)

        dh = dh_ref[...]
        xv = x_ref[...]
        r = lax.rsqrt(jnp.mean(xv * xv, axis=1, keepdims=True) + EPS)
        xn = xv * r
        g = g_ref[...]
        one_sc = 1.0 + sc_ref[...]
        dsh_ref[...] += _rowsum(dh)
        dsc_ref[...] += _rowsum(dh * (xn * g))
        dg_ref[...] += _rowsum(dh * one_sc * xn)
        dxn = dh * (g * one_sc)
        dx_ref[...] = dxo_ref[...] + r * (dxn - xn * jnp.mean(dxn * xn, axis=1, keepdims=True))

    tile = pl.BlockSpec((None, tm, D), lambda b, m: (b, m, 0))
    vec = pl.BlockSpec((None, 1, D), lambda b, m: (b, 0, 0))
    one = pl.BlockSpec((1, D), lambda b, m: (0, 0))
    return pl.pallas_call(
        body, name="norm_mod_bwd", grid=(nb, s // tm),
        in_specs=[tile, tile, tile, one, vec],
        out_specs=[tile, vec, vec, one],
        out_shape=[jax.ShapeDtypeStruct(x.shape, F32), jax.ShapeDtypeStruct((nb, 1, D), F32),
                   jax.ShapeDtypeStruct((nb, 1, D), F32), jax.ShapeDtypeStruct((1, D), F32)],
        compiler_params=_params(("arbitrary", "arbitrary")),
    )(dh, x, dxo, gain, scale)


def _in_proj_bwd_w(h, dproj):
    nb, s, _ = h.shape
    tm = _row_tile(s, 1024)
    n_m = s // tm

    def body(h_ref, dp_ref, o_ref, acc_ref):
        b, m = pl.program_id(1), pl.program_id(2)

        @pl.when((b == 0) & (m == 0))
        def _():
            acc_ref[...] = jnp.zeros_like(acc_ref)

        acc_ref[...] += _dot_tn(h_ref[...], dp_ref[...])

        @pl.when((b == nb - 1) & (m == n_m - 1))
        def _():
            o_ref[0] = acc_ref[:, :UNIT].astype(BF16)
            o_ref[1] = acc_ref[:, UNIT:].astype(BF16)

    return pl.pallas_call(
        body, name="in_proj_bwd_w", grid=(N_SEG, nb, n_m),
        in_specs=[pl.BlockSpec((None, tm, D), lambda j, b, m: (b, m, 0)),
                  pl.BlockSpec((None, None, tm, D), lambda j, b, m: (j, b, m, 0))],
        out_specs=pl.BlockSpec((2, D, UNIT), lambda j, b, m: (j, 0, 0)),
        out_shape=jax.ShapeDtypeStruct((2 * N_SEG, D, UNIT), BF16),
        scratch_shapes=[pltpu.VMEM((D, D), F32)],
        compiler_params=_params(("arbitrary", "arbitrary", "arbitrary")),
    )(h, dproj)


def _mod_proj(c_all, w_mod, b_mod_mine):
    nl, _, ncol = w_mod.shape
    nbg = c_all.shape[0]

    def body(c_ref, w_ref, b_ref, o_ref):
        cv = c_ref[...]
        o_ref[...] = jnp.dot(cv * jax.nn.sigmoid(cv), w_ref[...], preferred_element_type=F32,
                             precision=lax.Precision.HIGHEST) + b_ref[...]

    return pl.pallas_call(
        body, name="mod_proj", grid=(nl,),
        in_specs=[pl.BlockSpec((nbg, D), lambda l: (0, 0)), pl.BlockSpec((None, D, ncol), lambda l: (l, 0, 0)),
                  pl.BlockSpec((None, 1, ncol), lambda l: (l, 0, 0))],
        out_specs=pl.BlockSpec((None, nbg, ncol), lambda l: (l, 0, 0)),
        out_shape=jax.ShapeDtypeStruct((nl, nbg, ncol), F32),
        compiler_params=_params(("arbitrary",)),
    )(c_all, w_mod, b_mod_mine)


def _mod_grad(c_all, dmod_all, dmod_mine):
    nl, nbg, ncol = dmod_mine.shape

    def body(c_ref, da_ref, dm_ref, gw_ref, gb_ref):
        cv = c_ref[...]
        gw_ref[...] = lax.dot_general(cv * jax.nn.sigmoid(cv), dm_ref[...], (((0,), (0,)), ((), ())),
                                      preferred_element_type=F32, precision=lax.Precision.HIGHEST)
        gb_ref[...] = _rowsum(da_ref[...])

    return pl.pallas_call(
        body, name="mod_grad", grid=(nl,),
        in_specs=[pl.BlockSpec((nbg, D), lambda l: (0, 0)), pl.BlockSpec((None, nbg, 3 * D), lambda l: (l, 0, 0)),
                  pl.BlockSpec((None, nbg, ncol), lambda l: (l, 0, 0))],
        out_specs=[pl.BlockSpec((None, D, ncol), lambda l: (l, 0, 0)),
                   pl.BlockSpec((None, 1, 3 * D), lambda l: (l, 0, 0))],
        out_shape=[jax.ShapeDtypeStruct((nl, D, ncol), F32), jax.ShapeDtypeStruct((nl, 1, 3 * D), F32)],
        compiler_params=_params(("arbitrary",)),
    )(c_all, dmod_all, dmod_mine)


def _adamw(parts, w, m, v, name, layer=None, prev=None):
    n_parts, n_u, n_r, cu = parts.shape
    assert w.shape[-2:] == (n_r, n_u * cu), (parts.shape, w.shape)
    tr = n_r
    for cand in (512, 256, 128):
        if n_r > cand and n_r % cand == 0:
            tr = cand
            break
    n_prev = 0 if prev is None else 4

    def body(p_ref, w_ref, m_ref, v_ref, *rest):
        g_ref, d_ref, nm_ref, nv_ref = rest[n_prev:]
        g = p_ref[0].astype(F32)
        for k in range(1, n_parts):
            g = g + p_ref[k].astype(F32)
        m2 = ADAM_B1 * m_ref[...] + (1.0 - ADAM_B1) * g
        v2 = ADAM_B2 * v_ref[...] + (1.0 - ADAM_B2) * (g * g)
        m_hat = m2 / (1.0 - ADAM_B1 ** ADAM_STEP)
        v_hat = v2 / (1.0 - ADAM_B2 ** ADAM_STEP)
        g_ref[...] = g
        d_ref[...] = -ADAM_LR * (m_hat / (jnp.sqrt(v_hat) + ADAM_EPS) + ADAM_WD * w_ref[...])
        nm_ref[...] = m2
        nv_ref[...] = v2

    if layer is None:
        tile = pl.BlockSpec((tr, cu), lambda u, i: (i, u))
    else:
        tile = pl.BlockSpec((None, tr, cu), lambda u, i: (layer, i, u))
    shp = jax.ShapeDtypeStruct(w.shape, F32)
    return pl.pallas_call(
        body, name=name, grid=(n_u, n_r // tr),
        in_specs=[pl.BlockSpec((n_parts, None, tr, cu), lambda u, i: (0, u, i, 0)), tile, tile, tile]
        + [pl.BlockSpec(memory_space=pl.ANY)] * n_prev,
        out_specs=[tile, tile, tile, tile], out_shape=[shp, shp, shp, shp],
        input_output_aliases={4 + k: k for k in range(n_prev)},
        compiler_params=_params(("arbitrary", "arbitrary")),
    )(parts, w, m, v, *(prev or ()))


def _gathered_cols(g, inner):
    k = len(inner)
    perm = tuple(range(1, k + 1)) + (0, k + 1)
    t = jnp.transpose(g, perm)
    return t.reshape(tuple(inner) + (g.shape[0] * g.shape[-1],))


def _pair_blocks(wh):
    z = jnp.zeros((8, 64, 64), wh.dtype)
    w2 = wh.reshape(8, 2, 64, 64)
    top = jnp.concatenate([w2[:, 0], z], axis=2)
    bot = jnp.concatenate([z, w2[:, 1]], axis=2)
    return jnp.concatenate([top, bot], axis=1).astype(BF16)


def _unpair_blocks(g):
    return jnp.stack([g[:, :64, :64], g[:, 64:, 64:]], axis=1).reshape(16, 64, 64)


FLAT_ROWS = 512


def _pack_flat(arrays):
    flat = jnp.concatenate([a.reshape(-1) for a in arrays])
    n = flat.shape[0]
    per = FLAT_ROWS * LANES
    pad = (-n) % per
    return jnp.pad(flat, (0, pad)).reshape(-1, LANES)


def _unpack_flat(packed, like):
    flat = packed.reshape(-1)
    out, off = [], 0
    for a in like:
        out.append(flat[off:off + a.size].reshape(a.shape))
        off += a.size
    return out


def kernel(x, c, norm_gain, w_mod, b_mod, w_in, w_out, conv_a_w, sgu_w, sgu_b, lru_conv_w, lru_conv_b, lru_wa, lru_ba, lru_wx, lru_bx, lru_lambda, final_gain, loss_target, m_norm_gain, m_w_mod, m_b_mod, m_w_in, m_w_out, m_conv_a_w, m_sgu_w, m_sgu_b, m_lru_conv_w, m_lru_conv_b, m_lru_wa, m_lru_ba, m_lru_wx, m_lru_bx, m_lru_lambda, m_final_gain, v_norm_gain, v_w_mod, v_b_mod, v_w_in, v_w_out, v_conv_a_w, v_sgu_w, v_sgu_b, v_lru_conv_w, v_lru_conv_b, v_lru_wa, v_lru_ba, v_lru_wx, v_lru_bx, v_lru_lambda, v_final_gain):
    nl = w_in.shape[0]
    nb, s, _ = x.shape
    me = _my_index()
    mod_cols = w_mod.shape[2]


    small = jnp.concatenate([c.reshape(-1, LANES), conv_a_w.reshape(-1, LANES), lru_conv_w.reshape(-1, LANES)])
    n_c, n_ca = nb * D // LANES, nl * 3
    n_small = small.shape[0]
    small = jnp.pad(small, ((0, (-n_small) % 8), (0, 0)))
    small_all, _ = _all_gather(small, "gather_small")
    c_all = small_all[:, :n_c].reshape(N_DEV * nb, D)
    conv_a_full = _gathered_cols(small_all[:, n_c:n_c + n_ca].reshape(N_DEV, nl, 3, LANES), (nl, 3))
    lru_conv_full = _gathered_cols(small_all[:, n_c + n_ca:n_small].reshape(N_DEV, nl, 4, LANES), (nl, 4))

    b_mod_mine = lax.dynamic_slice_in_dim(b_mod, me * mod_cols, mod_cols, axis=1)[:, None, :]
    mod_mine = _mod_proj(c_all, w_mod, b_mod_mine)
    mod_all, mod_token = _all_gather(mod_mine.reshape(nl * N_DEV * nb, mod_cols), "gather_mod")
    mod_full = _gathered_cols(mod_all.reshape(N_DEV, nl, N_DEV * nb, mod_cols), (nl, N_DEV * nb))
    mod_loc = lax.dynamic_slice_in_dim(mod_full, me * nb, nb, axis=1)
    shift, scale, gate = [mod_loc[:, :, j * D:(j + 1) * D][:, :, None, :] for j in range(3)]

    def start_w_in(l, token):
        return _split_start(_after(w_in[l], token).astype(BF16), _peers_same_core, False, "gather_w_in_start")

    def gathered_w_in(started, after):
        block, land = _split_wait(started, after, _peers_same_core, False, False, "gather_w_in_wait")
        return _gather_finish(block, land, "gather_w_in_finish")

    sgu_b_lanes = jnp.broadcast_to(sgu_b[..., None], sgu_b.shape + (LANES,))
    mws = []
    for l in range(nl):
        mws.append((conv_a_full[l], sgu_w[l], sgu_b_lanes[l], lru_conv_full[l], lru_conv_b[l][None, :],
                    _pair_blocks(lru_wa[l]), _pair_blocks(lru_wx[l]), lru_ba[l].reshape(1, D),
                    lru_bx[l].reshape(1, D), lru_lambda[l][None, :]))

    xs, hs_bf, projs, mergeds, states, wg = [], [], [], [], [], []
    xl = x
    wg_started = start_w_in(0, mod_token)
    wo_started = _split_start(_after(w_out, wg_started[4]).astype(BF16).reshape(nl * (D // N_DEV), D),
                              _peers_all, False, "gather_w_out_start")
    wo = None
    for l in range(nl):
        h = _norm_mod(xl, _after(norm_gain[l][None, :], wo_started[4]), shift[l], scale[l])
        wg_l, token = gathered_w_in(wg_started, h)
        wg.append(wg_l)
        if l + 1 < nl:
            wg_started = start_w_in(l + 1, token)
            token = wg_started[4]
        proj = _in_proj(h, wg_l, token)
        merged, st = _mixer_fwd(proj, mws[l])
        xs.append(xl), hs_bf.append(h), projs.append(proj), mergeds.append(merged), states.append(st)
        if wo is None:
            _, wo_all = _split_wait(wo_started, merged, _peers_all, False, True, "gather_w_out_wait")
            wo = jnp.transpose(wo_all.reshape(N_DEV, nl, D // N_DEV, D), (1, 0, 2, 3)).reshape(nl, D, D)
        xl = _out_proj(xl, merged, wo[l], gate[l])

    loss_row, dx, g_final = _loss_head(xl, final_gain[None, :], loss_target)
    loss = lax.psum(loss_row[0, 0], ("x", "y", "c"))

    rep_names = ["sgu_w", "sgu_b", "lru_conv_b", "lru_wa", "lru_ba", "lru_wx", "lru_bx", "lru_lambda"]
    rep_w = dict(norm_gain=norm_gain, sgu_w=sgu_w, sgu_b=sgu_b, lru_conv_b=lru_conv_b, lru_wa=lru_wa, lru_ba=lru_ba,
                 lru_wx=lru_wx, lru_bx=lru_bx, lru_lambda=lru_lambda)
    rep_m = dict(norm_gain=m_norm_gain, sgu_w=m_sgu_w, sgu_b=m_sgu_b, lru_conv_b=m_lru_conv_b, lru_wa=m_lru_wa,
                 lru_ba=m_lru_ba, lru_wx=m_lru_wx, lru_bx=m_lru_bx, lru_lambda=m_lru_lambda)
    rep_v = dict(norm_gain=v_norm_gain, sgu_w=v_sgu_w, sgu_b=v_sgu_b, lru_conv_b=v_lru_conv_b, lru_wa=v_lru_wa,
                 lru_ba=v_lru_ba, lru_wx=v_lru_wx, lru_bx=v_lru_bx, lru_lambda=v_lru_lambda)

    def rep_pack(src, l, final):
        tail = final if l == nl - 1 else jnp.zeros_like(final)
        return _pack_flat([src[n][l] for n in rep_names] + [tail])

    rep_like = [rep_w[n][0] for n in rep_names] + [final_gain]
    res_big, g_conv, res_rep_l = {}, [None] * nl, [None] * nl
    dmods = [None] * nl

    def finish_exchange(pending, after):
        l, h_in, h_out, h_rep = pending
        _, recv = _split_wait(h_in, after, _peers_all, True, True, "scatter_w_in_wait")
        res_big["w_in"] = _adamw(recv, w_in, m_w_in, v_w_in, "adamw_w_in", l, res_big.get("w_in"))
        _, recv = _split_wait(h_out, after, _peers_all, True, True, "scatter_w_out_wait")
        res_big["w_out"] = _adamw(recv, w_out, m_w_out, v_w_out, "adamw_w_out", l, res_big.get("w_out"))
        _, recv = _split_wait(h_rep, after, _peers_all, False, True, "gather_rep_wait")
        res = _adamw(recv[:, None], rep_pack(rep_w, l, final_gain), rep_pack(rep_m, l, m_final_gain),
                     rep_pack(rep_v, l, v_final_gain), "adamw_rep")
        res_rep_l[l] = [_unpack_flat(a, rep_like) for a in res]

    pending = None
    g_gains = [None] * nl
    for l in reversed(range(nl)):
        dmerged, gw_out, dgate = _out_proj_bwd(dx, mergeds[l], wo[l], gate[l])
        dproj, g_caw, g_sw, g_sb, g_lcw, g_vec, g_wa, g_wx = _mixer_bwd(projs[l], dmerged, states[l], mws[l])
        gw_in = _in_proj_bwd_w(hs_bf[l], dproj)
        g_conv[l] = (g_caw, g_lcw)
        rep_g = dict(
            sgu_w=g_sw, sgu_b=g_sb[:, :, 0], lru_conv_b=g_vec[0],
            lru_wa=_unpair_blocks(g_wa), lru_ba=g_vec[1].reshape(16, 64), lru_wx=_unpair_blocks(g_wx),
            lru_bx=g_vec[2].reshape(16, 64), lru_lambda=g_vec[3])
        rep_block = _pack_flat([rep_g[n] for n in rep_names]
                               + [g_final[0] if l == nl - 1 else jnp.zeros_like(g_final[0])])
        started = (l,
                   _split_start(gw_in.reshape(N_DEV, UNITS_PER_DEV, D, UNIT), _peers_all, True, "scatter_w_in_start"),
                   _split_start(gw_out.reshape(N_DEV, 1, D // N_DEV, D), _peers_all, True, "scatter_w_out_start"),
                   _split_start(rep_block, _peers_all, False, "gather_rep_start"))
        dep = started[1][4] + started[2][4] + started[3][4]
        dh = _in_proj_bwd_h(dproj, wg[l], dep)
        dx, dshift, dscale, g_gain = _norm_mod_bwd(dh, xs[l], dx, norm_gain[l][None, :], scale[l])
        g_gains[l] = g_gain
        dmods[l] = jnp.concatenate([dshift, dscale, dgate], axis=2)[:, 0, :]
        if pending is not None:
            finish_exchange(pending, dx)
        pending = started

    conv_parts = jnp.concatenate(
        [jnp.stack([g_conv[l][0] for l in range(nl)]).reshape(nl * 3, N_DEV, LANES),
         jnp.stack([g_conv[l][1] for l in range(nl)]).reshape(nl * 4, N_DEV, LANES)], axis=0)
    conv_parts = jnp.transpose(conv_parts, (1, 0, 2))[:, None]
    conv_recv = _all_to_all(conv_parts, "scatter_conv")

    dmod_loc = jnp.stack(dmods).reshape(nl * nb, 3 * D)
    gain_rows = jnp.pad(jnp.concatenate(g_gains, axis=0), ((0, (-nl) % 8), (0, 2 * D)))
    tail_g, _ = _all_gather(jnp.concatenate([dmod_loc, gain_rows], axis=0), "gather_dmod")
    dmod_g = tail_g[:, :nl * nb]
    gain_parts = tail_g[:, nl * nb:nl * nb + nl, :D][:, None]
    res_gain = _adamw(gain_parts, norm_gain, m_norm_gain, v_norm_gain, "adamw_gain")
    dmod_all = jnp.transpose(dmod_g.reshape(N_DEV, nl, nb, 3 * D), (1, 0, 2, 3)).reshape(nl, N_DEV * nb, 3 * D)
    dmod_mine = lax.dynamic_slice_in_dim(dmod_all, me * mod_cols, mod_cols, axis=2)
    gw_mod, gb_mod = _mod_grad(c_all, dmod_all, dmod_mine)
    res_w_mod = _adamw(gw_mod.reshape(1, 1, nl * D, mod_cols), w_mod.reshape(nl * D, mod_cols),
                       m_w_mod.reshape(nl * D, mod_cols), v_w_mod.reshape(nl * D, mod_cols), "adamw_w_mod")
    res_w_mod = [a.reshape(nl, D, mod_cols) for a in res_w_mod]
    res_b_mod = _adamw(gb_mod.reshape(1, 1, nl, 3 * D), b_mod, m_b_mod, v_b_mod, "adamw_b_mod")
    finish_exchange(pending, res_b_mod[1])

    cat = lambda a, b: jnp.concatenate([a.reshape(nl * 3, LANES), b.reshape(nl * 4, LANES)], axis=0)
    res_conv = _adamw(conv_recv, cat(conv_a_w, lru_conv_w), cat(m_conv_a_w, m_lru_conv_w),
                      cat(v_conv_a_w, v_lru_conv_w), "adamw_conv")
    res_conv_a = [a[:nl * 3].reshape(nl, 3, LANES) for a in res_conv]
    res_lru_conv = [a[nl * 3:].reshape(nl, 4, LANES) for a in res_conv]

    res_rep = []
    for k in range(4):
        d = {n: jnp.stack([res_rep_l[l][k][i] for l in range(nl)]) for i, n in enumerate(rep_names)}
        d["final_gain"] = res_rep_l[nl - 1][k][len(rep_names)]
        res_rep.append(d)

    def leaf(k, name):
        if name == "norm_gain":
            return res_gain[k]
        if name == "w_mod":
            return res_w_mod[k]
        if name == "b_mod":
            return res_b_mod[k]
        if name in ("w_in", "w_out"):
            return res_big[name][k]
        if name == "conv_a_w":
            return res_conv_a[k]
        if name == "lru_conv_w":
            return res_lru_conv[k]
        return res_rep[k][name]

    order = ["norm_gain", "w_mod", "b_mod", "w_in", "w_out", "conv_a_w", "sgu_w", "sgu_b", "lru_conv_w",
             "lru_conv_b", "lru_wa", "lru_ba", "lru_wx", "lru_bx", "lru_lambda", "final_gain"]
    outs = [loss, dx]
    for k in range(4):
        outs += [leaf(k, n) for n in order]
    return tuple(outs)
```

```python
import functools

import jax
import jax.numpy as jnp
from jax import lax
from jax.experimental import pallas as pl
from jax.experimental.pallas import tpu as pltpu

F32 = jnp.float32
BF16 = jnp.bfloat16

D = 1024
N_DEV = 8
N_SEG = 12
LANES = 128
SUBLANES = 8
CHUNK = 128
HALO = 16
UNIT = 512
UNITS_PER_DEV = 3
EPS = 1e-6
LRU_C = 8.0
ADAM_LR, ADAM_B1, ADAM_B2, ADAM_EPS, ADAM_WD, ADAM_STEP = 0.001, 0.9, 0.999, 1e-08, 0.01, 10
VMEM_LIMIT = 56 * 1024 * 1024

AX, AB, AC, AZ, SU, SV, SZ, RX, RZ, GA, GS, GR = range(N_SEG)
MESH = pl.DeviceIdType.MESH


def _params(sem=None):
    return pltpu.CompilerParams(dimension_semantics=sem, vmem_limit_bytes=VMEM_LIMIT)


def _my_index():
    return 4 * lax.axis_index("x") + 2 * lax.axis_index("y") + lax.axis_index("c")


def _all_gather(block, name, dep=None):
    def body(x_ref, *refs):
        out_ref, token, send_sems, recv_sems, local_sem = refs[-5:]
        x, y, c = lax.axis_index("x"), lax.axis_index("y"), lax.axis_index("c")
        me, sibling = (x, y, c), (x, y, 1 - c)
        chips = [(1 - x, y), (x, 1 - y), (1 - x, 1 - y)]
        token[...] = jnp.zeros_like(token)

        def rows(px, py, pc):
            return out_ref.at[4 * px + 2 * py + pc]

        def copy(k, blk, to, src=None):
            return pltpu.make_async_remote_copy(
                src_ref=rows(*blk) if src is None else src, dst_ref=rows(*blk),
                send_sem=send_sems.at[k], recv_sem=recv_sems.at[k], device_id=to, device_id_type=MESH)

        mine = pltpu.make_async_copy(x_ref, rows(*me), local_sem)
        mine.start()
        first = [copy(0, me, sibling, src=x_ref)]
        first += [copy(1 + j, me, (*chip, c), src=x_ref) for j, chip in enumerate(chips)]
        for cp in first:
            cp.start()
        passed = [copy(4 + j, (*chip, c), sibling) for j, chip in enumerate(chips)]
        for j, chip in enumerate(chips):
            copy(1 + j, (*chip, c), me).wait_recv()
            passed[j].start()
        copy(0, sibling, me).wait_recv()
        for j, chip in enumerate(chips):
            copy(4 + j, (*chip, 1 - c), me).wait_recv()
        for cp in first + passed:
            cp.wait_send()
        mine.wait()

    return pl.pallas_call(
        body, name=name,
        out_shape=[jax.ShapeDtypeStruct((N_DEV,) + block.shape, block.dtype), jax.ShapeDtypeStruct((8, LANES), F32)],
        in_specs=[pl.BlockSpec(memory_space=pltpu.VMEM)]
        + [pl.BlockSpec(memory_space=pl.ANY)] * (dep is not None),
        out_specs=[pl.BlockSpec(memory_space=pl.ANY), pl.BlockSpec(memory_space=pltpu.VMEM)],
        scratch_shapes=[pltpu.SemaphoreType.DMA((7,)), pltpu.SemaphoreType.DMA((7,)), pltpu.SemaphoreType.DMA],
    )(block, *([dep] if dep is not None else []))


def _all_to_all(blocks, name):
    def body(x_ref, out_ref, send_sems, recv_sems, local_sem):
        x, y, c = lax.axis_index("x"), lax.axis_index("y"), lax.axis_index("c")
        my = 4 * x + 2 * y + c
        mine = pltpu.make_async_copy(x_ref.at[my], out_ref.at[my], local_sem)
        mine.start()
        peers = []
        for r in range(1, N_DEV):
            px = 1 - x if r & 4 else x
            py = 1 - y if r & 2 else y
            pc = 1 - c if r & 1 else c
            peers.append((r - 1, 4 * px + 2 * py + pc, (px, py, pc)))

        def copy(k, src_slot, dst_slot, to):
            return pltpu.make_async_remote_copy(
                src_ref=x_ref.at[src_slot], dst_ref=out_ref.at[dst_slot],
                send_sem=send_sems.at[k], recv_sem=recv_sems.at[k], device_id=to, device_id_type=MESH)

        sends = [copy(k, pid, my, to) for k, pid, to in peers]
        for cp in sends:
            cp.start()
        for k, pid, to in peers:
            copy(k, pid, pid, to).wait_recv()
        for cp in sends:
            cp.wait_send()
        mine.wait()

    return pl.pallas_call(
        body, name=name,
        out_shape=jax.ShapeDtypeStruct(blocks.shape, blocks.dtype),
        in_specs=[pl.BlockSpec(memory_space=pltpu.VMEM)],
        out_specs=pl.BlockSpec(memory_space=pl.ANY),
        scratch_shapes=[pltpu.SemaphoreType.DMA((7,)), pltpu.SemaphoreType.DMA((7,)), pltpu.SemaphoreType.DMA],
    )(blocks)


_HBM = pl.BlockSpec(memory_space=pltpu.HBM)
_SEM = pl.BlockSpec(memory_space=pltpu.SEMAPHORE)
_EFFECT = pltpu.SideEffectType.DATAFLOW_SIDE_EFFECTING


def _peers_all(x, y, c):
    out = []
    for r in range(1, N_DEV):
        px = 1 - x if r & 4 else x
        py = 1 - y if r & 2 else y
        pc = 1 - c if r & 1 else c
        out.append((r - 1, 4 * px + 2 * py + pc, (px, py, pc)))
    return out


def _peers_same_core(x, y, c):
    return [(k, 4 * px + 2 * py + c, (px, py, c))
            for k, (px, py) in enumerate([(1 - x, y), (x, 1 - y), (1 - x, 1 - y)])]


def _split_start(src, peers_fn, scatter, name):
    blk = src.shape[1:] if scatter else src.shape
    land_shape = (N_DEV,) + tuple(blk)
    n = len(peers_fn(0, 0, 0))

    def body(x_ref, land_ref, send_sems, recv_sems, x_thru, land_thru, token):
        x, y, c = lax.axis_index("x"), lax.axis_index("y"), lax.axis_index("c")
        my = 4 * x + 2 * y + c
        for k, pid, to in peers_fn(x, y, c):
            pltpu.make_async_remote_copy(
                src_ref=x_ref.at[pid] if scatter else x_ref, dst_ref=land_ref.at[my],
                send_sem=send_sems.at[k], recv_sem=recv_sems.at[k], device_id=to, device_id_type=MESH).start()
        token[...] = jnp.zeros_like(token)

    return pl.pallas_call(
        body, name=name,
        out_shape=(pltpu.SemaphoreType.DMA((n,)), pltpu.SemaphoreType.DMA((n,)),
                   pltpu.HBM(src.shape, src.dtype), pltpu.HBM(land_shape, src.dtype),
                   jax.ShapeDtypeStruct((8, LANES), F32)),
        in_specs=(_HBM, _HBM),
        out_specs=(_SEM, _SEM, _HBM, _HBM, pl.BlockSpec(memory_space=pltpu.VMEM)),
        input_output_aliases={0: 2, 1: 3},
        compiler_params=pltpu.CompilerParams(has_side_effects=_EFFECT),
    )(pltpu.with_memory_space_constraint(src, pltpu.HBM),
      pltpu.with_memory_space_constraint(lax.empty(land_shape, src.dtype), pltpu.HBM))


def _split_wait(handles, after, peers_fn, scatter, own, name):
    send_sems, recv_sems, src_thru, land_thru, _ = handles
    blk = land_thru.shape[1:]

    def body(x_ref, land_ref, send_sems, recv_sems, after_ref, x_dead, got_ref, *stage):
        x, y, c = lax.axis_index("x"), lax.axis_index("y"), lax.axis_index("c")
        if own:
            my = 4 * x + 2 * y + c
            mine = _staged_copy(x_ref.at[my] if scatter else x_ref, land_ref.at[my], *stage)
        for k, pid, to in peers_fn(x, y, c):
            cp = pltpu.make_async_remote_copy(
                src_ref=x_ref.at[pid] if scatter else x_ref, dst_ref=land_ref.at[pid],
                send_sem=send_sems.at[k], recv_sem=recv_sems.at[k], device_id=to, device_id_type=MESH)
            cp.wait_send()
            cp.wait_recv()
        if own:
            mine.wait()

    return pl.pallas_call(
        body, name=name,
        out_shape=(pltpu.HBM(src_thru.shape, src_thru.dtype), pltpu.HBM(land_thru.shape, land_thru.dtype)),
        in_specs=(_HBM, _HBM, _SEM, _SEM, pl.BlockSpec(memory_space=pl.ANY)),
        out_specs=(_HBM, _HBM),
        input_output_aliases={0: 0, 1: 1},
        scratch_shapes=[pltpu.VMEM(blk, land_thru.dtype), pltpu.SemaphoreType.DMA((2,))] if own else [],
        compiler_params=pltpu.CompilerParams(has_side_effects=_EFFECT, vmem_limit_bytes=VMEM_LIMIT),
    )(src_thru, land_thru, send_sems, recv_sems, after)


def _staged_copy(src_ref, dst_ref, buf, sems):
    leg = pltpu.make_async_copy(src_ref, buf, sems.at[0])
    leg.start()
    leg.wait()
    leg = pltpu.make_async_copy(buf, dst_ref, sems.at[1])
    leg.start()
    return leg


def _gather_finish(block, land, name):
    def body(x_ref, land_ref, out_ref, token, send_sems, recv_sems, buf, local_sems):
        x, y, c = lax.axis_index("x"), lax.axis_index("y"), lax.axis_index("c")
        my, sib_id, sibling = 4 * x + 2 * y + c, 4 * x + 2 * y + 1 - c, (x, y, 1 - c)
        token[...] = jnp.zeros_like(token)

        def copy(k, slot, src=None):
            return pltpu.make_async_remote_copy(
                src_ref=land_ref.at[slot] if src is None else src, dst_ref=out_ref.at[slot],
                send_sem=send_sems.at[k], recv_sem=recv_sems.at[k], device_id=sibling, device_id_type=MESH)

        chips = _peers_same_core(x, y, c)
        sends = [copy(0, my, src=x_ref)] + [copy(1 + k, pid) for k, pid, _ in chips]
        for cp in sends:
            cp.start()
        mine = _staged_copy(x_ref, out_ref.at[my], buf, local_sems)
        copy(0, sib_id).wait_recv()
        for k, pid, _ in chips:
            copy(1 + k, pid + 1 - 2 * c).wait_recv()
        for cp in sends:
            cp.wait_send()
        mine.wait()

    return pl.pallas_call(
        body, name=name,
        out_shape=[jax.ShapeDtypeStruct(land.shape, land.dtype), jax.ShapeDtypeStruct((8, LANES), F32)],
        in_specs=[pl.BlockSpec(memory_space=pl.ANY), pl.BlockSpec(memory_space=pl.ANY)],
        out_specs=[pl.BlockSpec(memory_space=pl.ANY), pl.BlockSpec(memory_space=pltpu.VMEM)],
        input_output_aliases={1: 0},
        scratch_shapes=[pltpu.SemaphoreType.DMA((4,)), pltpu.SemaphoreType.DMA((4,)),
                        pltpu.VMEM(block.shape, block.dtype), pltpu.SemaphoreType.DMA((2,))],
        compiler_params=pltpu.CompilerParams(vmem_limit_bytes=VMEM_LIMIT),
    )(block, land)


def _after(v, token):
    return v + token[0, 0].astype(v.dtype)


def _dsilu(x, s):
    return s * (1.0 + x * (1.0 - s))


def _log1p(x):
    u = 1.0 + x
    d = u - 1.0
    return jnp.where(d == 0.0, x, jnp.log(u) * (x / jnp.where(d == 0.0, 1.0, d)))


def _softplus_neg(lam):
    return jnp.maximum(-lam, 0.0) + _log1p(jnp.exp(-jnp.abs(lam)))


def _neg_expm1(y):
    poly = -y * (1.0 + y * (0.5 + y * (1.0 / 6.0 + y * (1.0 / 24.0 + y * (1.0 / 120.0)))))
    return jnp.where(y > -0.1, poly, 1.0 - jnp.exp(y))


def _shift_dn(cur, prev, k):
    ext = jnp.concatenate([prev, cur], axis=0)
    return pltpu.roll(ext, k, 0)[HALO:, :]


def _shift_up(cur, nxt, k):
    n = cur.shape[0]
    ext = jnp.concatenate([cur, nxt], axis=0)
    return pltpu.roll(ext, n + HALO - k, 0)[:n, :]


def _scan_fwd(a, b, h_prev):
    groups = a.shape[0] // SUBLANES
    a3 = a.reshape(groups, SUBLANES, LANES)
    b3 = b.reshape(groups, SUBLANES, LANES)
    row = lax.broadcasted_iota(jnp.int32, a3.shape, 1)
    k = 1
    while k < SUBLANES:
        a_sh = jnp.where(row >= k, pltpu.roll(a3, k, 1), 1.0)
        b_sh = jnp.where(row >= k, pltpu.roll(b3, k, 1), 0.0)
        b3 = a3 * b_sh + b3
        a3 = a3 * a_sh
        k *= 2
    carry = h_prev[HALO - 1:HALO, :]
    out = []
    for i in range(groups):
        hg = b3[i] + a3[i] * carry
        out.append(hg)
        carry = hg[SUBLANES - 1:SUBLANES, :]
    return jnp.concatenate(out, axis=0)


def _scan_rev(a_next, g, lam_next):
    groups = g.shape[0] // SUBLANES
    a3 = a_next.reshape(groups, SUBLANES, LANES)
    g3 = g.reshape(groups, SUBLANES, LANES)
    row = lax.broadcasted_iota(jnp.int32, a3.shape, 1)
    k = 1
    while k < SUBLANES:
        ok = row < SUBLANES - k
        a_sh = jnp.where(ok, pltpu.roll(a3, SUBLANES - k, 1), 1.0)
        g_sh = jnp.where(ok, pltpu.roll(g3, SUBLANES - k, 1), 0.0)
        g3 = g3 + a3 * g_sh
        a3 = a3 * a_sh
        k *= 2
    carry = lam_next[0:1, :]
    out = [None] * groups
    for i in reversed(range(groups)):
        lg = g3[i] + a3[i] * carry
        out[i] = lg
        carry = lg[0:1, :]
    return jnp.concatenate(out, axis=0)


def _rowsum(v):
    return jnp.sum(v, axis=0, keepdims=True)


def _dot(a, b):
    return jnp.dot(a, b, preferred_element_type=F32)


def _dot_nt(a, b):
    return lax.dot_general(a, b, (((1,), (1,)), ((), ())), preferred_element_type=F32)


def _dot_tn(a, b):
    return lax.dot_general(a, b, (((0,), (0,)), ((), ())), preferred_element_type=F32)


class _MixerWeights:
    def __init__(self, caw_ref, sw_ref, sb_ref, lcw_ref, lcb_ref, wa_ref, wx_ref, ba_ref, bx_ref, lam_ref):
        self.caw = [caw_ref[j:j + 1, :] for j in range(3)]
        self.lcw = [lcw_ref[j:j + 1, :] for j in range(4)]
        self.lcb = lcb_ref[...]
        row = lax.broadcasted_iota(jnp.int32, (CHUNK, CHUNK), 0)
        col = lax.broadcasted_iota(jnp.int32, (CHUNK, CHUNK), 1)
        self.tril = col <= row
        self.sw = jnp.where(self.tril, sw_ref[...], 0.0).astype(BF16)
        self.sb = sb_ref[...]
        self.wa = wa_ref[...]
        self.wx = wx_ref[...]
        self.ba = ba_ref[...]
        self.bx = bx_ref[...]
        lam = lam_ref[...]
        self.sp = _softplus_neg(lam)
        self.dsp_dlam = -jax.nn.sigmoid(-lam)


def _mixer_pre_scan(ld, ldp, w):
    t = {}
    a_x, a_c = ld(AX), ld(AC)
    t["a_x"], t["a_c"], t["a_b"], t["a_z"] = a_x, a_c, ld(AB), ld(AZ)
    ca = a_c * a_x
    ca_p = ldp(AC) * ldp(AX)
    t["ca"], t["ca1"], t["ca2"] = ca, _shift_dn(ca, ca_p, 1), _shift_dn(ca, ca_p, 2)
    t["cv"] = w.caw[2] * ca + w.caw[1] * t["ca1"] + w.caw[0] * t["ca2"]
    t["sa"] = jax.nn.sigmoid(t["a_z"])
    t["silu_az"] = t["a_z"] * t["sa"]
    t["y_a"] = t["silu_az"] * t["a_b"] * t["cv"]

    v = ld(SV)
    vc = v - jnp.mean(v, axis=1, keepdims=True)
    t["rstd"] = lax.rsqrt(jnp.mean(vc * vc, axis=1, keepdims=True) + EPS)
    t["vn"] = vc * t["rstd"]
    t["z"] = _dot(w.sw, t["vn"].astype(BF16)) + w.sb
    t["s_u"], t["s_z"] = ld(SU), ld(SZ)
    t["ss"] = jax.nn.sigmoid(t["s_z"])
    t["silu_sz"] = t["s_z"] * t["ss"]
    t["y_s"] = t["silu_sz"] * t["s_u"] * t["z"]

    r_x, r_xp = ld(RX), ldp(RX)
    t["rx"] = [_shift_dn(r_x, r_xp, 3), _shift_dn(r_x, r_xp, 2), _shift_dn(r_x, r_xp, 1), r_x]
    xc = w.lcb + w.lcw[0] * t["rx"][0] + w.lcw[1] * t["rx"][1] + w.lcw[2] * t["rx"][2] + w.lcw[3] * r_x
    t["xc"] = xc
    xcb = xc.astype(BF16)
    t["r"] = jax.nn.sigmoid(_dot(xcb, w.wa) + w.ba)
    t["i"] = jax.nn.sigmoid(_dot(xcb, w.wx) + w.bx)
    la = -LRU_C * t["r"] * w.sp
    t["a"] = jnp.exp(la)
    t["em"] = _neg_expm1(2.0 * la)
    t["mult"] = jnp.sqrt(t["em"])
    t["b"] = t["mult"] * (t["i"] * xc)
    t["r_z"] = ld(RZ)
    t["sr"] = jax.nn.sigmoid(t["r_z"])
    t["silu_rz"] = t["r_z"] * t["sr"]
    t["ga"], t["gs"], t["gr"] = jax.nn.sigmoid(ld(GA)), jax.nn.sigmoid(ld(GS)), jax.nn.sigmoid(ld(GR))
    return t


def _weight_specs(n_cb_axis):
    def at(fn):
        return lambda *g: fn(g[n_cb_axis])
    return [
        pl.BlockSpec((3, LANES), at(lambda cb: (0, cb))),
        pl.BlockSpec((None, CHUNK, CHUNK), at(lambda cb: (cb, 0, 0))),
        pl.BlockSpec((None, CHUNK, LANES), at(lambda cb: (cb, 0, 0))),
        pl.BlockSpec((4, LANES), at(lambda cb: (0, cb))),
        pl.BlockSpec((1, LANES), at(lambda cb: (0, cb))),
        pl.BlockSpec((None, LANES, LANES), at(lambda cb: (cb, 0, 0))),
        pl.BlockSpec((None, LANES, LANES), at(lambda cb: (cb, 0, 0))),
        pl.BlockSpec((1, LANES), at(lambda cb: (0, cb))),
        pl.BlockSpec((1, LANES), at(lambda cb: (0, cb))),
        pl.BlockSpec((1, LANES), at(lambda cb: (0, cb))),
    ]


def _chunk_loaders(p_ref, c):
    r0 = pl.multiple_of(c * CHUNK, CHUNK)
    rp = pl.multiple_of(jnp.maximum(c * CHUNK - HALO, 0), HALO)

    def ld(j):
        return p_ref[j, pl.ds(r0, CHUNK), :].astype(F32)

    def ldp(j):
        return jnp.where(c > 0, p_ref[j, pl.ds(rp, HALO), :].astype(F32), 0.0)

    return r0, rp, ld, ldp


def _mixer_fwd(proj, mw):
    _, nb, s, _ = proj.shape
    n_chunks = s // CHUNK

    def body(p_ref, *refs):
        w = _MixerWeights(*refs[:10])
        merged_ref, hs_ref = refs[10:]

        def chunk(c, h_prev):
            r0, _, ld, ldp = _chunk_loaders(p_ref, c)
            t = _mixer_pre_scan(ld, ldp, w)
            h = _scan_fwd(t["a"], t["b"], h_prev)
            y_r = t["silu_rz"] * h
            merged = t["ga"] * t["y_a"] + t["gs"] * t["y_s"] + t["gr"] * y_r
            merged_ref[pl.ds(r0, CHUNK), :] = merged.astype(BF16)
            hs_ref[pl.ds(r0, CHUNK), :] = h
            return h[CHUNK - HALO:, :]

        lax.fori_loop(0, n_chunks, chunk, jnp.zeros((HALO, LANES), F32))

    slab = pl.BlockSpec((None, s, LANES), lambda cb, b: (b, 0, cb))
    return pl.pallas_call(
        body, name="mixer_fwd", grid=(D // LANES, nb),
        in_specs=[pl.BlockSpec((N_SEG, None, s, LANES), lambda cb, b: (0, b, 0, cb))] + _weight_specs(0),
        out_specs=[slab, slab],
        out_shape=[jax.ShapeDtypeStruct((nb, s, D), BF16), jax.ShapeDtypeStruct((nb, s, D), F32)],
        compiler_params=_params(("arbitrary", "arbitrary")),
    )(proj, *mw)


def _mixer_bwd(proj, dmerged, hs, mw):
    _, nb, s, _ = proj.shape
    n_chunks = s // CHUNK

    def body(p_ref, dm_ref, hs_ref, *refs):
        w = _MixerWeights(*refs[:10])
        dp_ref, g_caw, g_sw, g_sb, g_lcw, g_vec, g_wa, g_wx = refs[10:]

        @pl.when(pl.program_id(1) == 0)
        def _():
            for ref in (g_caw, g_sw, g_sb, g_lcw, g_vec, g_wa, g_wx):
                ref[...] = jnp.zeros_like(ref)

        def chunk(i, carry):
            dcv_n, dxc_n, lam_n, a_n = carry
            c = n_chunks - 1 - i
            r0, rp, ld, ldp = _chunk_loaders(p_ref, c)
            t = _mixer_pre_scan(ld, ldp, w)
            h = hs_ref[pl.ds(r0, CHUNK), :]
            h_p = jnp.where(c > 0, hs_ref[pl.ds(rp, HALO), :], 0.0)
            h_prev = _shift_dn(h, h_p, 1)
            dm = dm_ref[pl.ds(r0, CHUNK), :].astype(F32)
            y_r = t["silu_rz"] * h

            def out(j, val):
                dp_ref[j, pl.ds(r0, CHUNK), :] = val.astype(BF16)

            ga, gs, gr = t["ga"], t["gs"], t["gr"]
            out(GA, dm * t["y_a"] * ga * (1.0 - ga))
            out(GS, dm * t["y_s"] * gs * (1.0 - gs))
            out(GR, dm * y_r * gr * (1.0 - gr))

            dy_a = dm * ga
            out(AZ, dy_a * t["a_b"] * t["cv"] * _dsilu(t["a_z"], t["sa"]))
            out(AB, dy_a * t["silu_az"] * t["cv"])
            dcv = dy_a * t["silu_az"] * t["a_b"]
            dca = w.caw[2] * dcv + w.caw[1] * _shift_up(dcv, dcv_n, 1) + w.caw[0] * _shift_up(dcv, dcv_n, 2)
            out(AC, dca * t["a_x"])
            out(AX, dca * t["a_c"])
            g_caw[2:3, :] += _rowsum(dcv * t["ca"])
            g_caw[1:2, :] += _rowsum(dcv * t["ca1"])
            g_caw[0:1, :] += _rowsum(dcv * t["ca2"])

            dy_s = dm * gs
            out(SZ, dy_s * t["s_u"] * t["z"] * _dsilu(t["s_z"], t["ss"]))
            out(SU, dy_s * t["silu_sz"] * t["z"])
            dz = dy_s * t["silu_sz"] * t["s_u"]
            dzb = dz.astype(BF16)
            g_sb[...] += jnp.broadcast_to(jnp.sum(dz, axis=1, keepdims=True), (CHUNK, LANES))
            g_sw[...] += _dot_nt(dzb, t["vn"].astype(BF16))
            dvn = _dot_tn(w.sw, dzb)
            vn = t["vn"]
            out(SV, t["rstd"] * (dvn - jnp.mean(dvn, axis=1, keepdims=True)
                                 - vn * jnp.mean(dvn * vn, axis=1, keepdims=True)))

            dy_r = dm * gr
            out(RZ, dy_r * h * _dsilu(t["r_z"], t["sr"]))
            lam = _scan_rev(_shift_up(t["a"], a_n, 1), dy_r * t["silu_rz"], lam_n)
            a, r, ig, xc, mult = t["a"], t["r"], t["i"], t["xc"], t["mult"]
            d_i = lam * mult * xc
            d_mult = lam * ig * xc
            dxc = lam * mult * ig
            dla = lam * h_prev * a - d_mult * ((1.0 - t["em"]) / mult)
            g_vec[3:4, :] += _rowsum(dla * (-LRU_C * r)) * w.dsp_dlam
            dpr = (dla * (-LRU_C * w.sp)) * r * (1.0 - r)
            dpi = d_i * ig * (1.0 - ig)
            dprb, dpib, xcb = dpr.astype(BF16), dpi.astype(BF16), xc.astype(BF16)
            g_wa[...] += _dot_tn(xcb, dprb)
            g_wx[...] += _dot_tn(xcb, dpib)
            g_vec[1:2, :] += _rowsum(dpr)
            g_vec[2:3, :] += _rowsum(dpi)
            dxc = dxc + _dot_nt(dprb, w.wa) + _dot_nt(dpib, w.wx)
            g_vec[0:1, :] += _rowsum(dxc)
            out(RX, w.lcw[3] * dxc + w.lcw[2] * _shift_up(dxc, dxc_n, 1)
                + w.lcw[1] * _shift_up(dxc, dxc_n, 2) + w.lcw[0] * _shift_up(dxc, dxc_n, 3))
            for j in range(4):
                g_lcw[j:j + 1, :] += _rowsum(dxc * t["rx"][j])
            return dcv[:HALO, :], dxc[:HALO, :], lam[:HALO, :], a[:HALO, :]

        zero = jnp.zeros((HALO, LANES), F32)
        lax.fori_loop(0, n_chunks, chunk, (zero, zero, zero, zero))

        @pl.when(pl.program_id(1) == nb - 1)
        def _():
            g_sw[...] = jnp.where(w.tril, g_sw[...], 0.0)

    slab = lambda dt: pl.BlockSpec((None, s, LANES), lambda cb, b: (b, 0, cb))
    seg = pl.BlockSpec((N_SEG, None, s, LANES), lambda cb, b: (0, b, 0, cb))
    rows = lambda n: pl.BlockSpec((n, LANES), lambda cb, b: (0, cb))
    sq = pl.BlockSpec((None, LANES, LANES), lambda cb, b: (cb, 0, 0))
    n_cb = D // LANES
    return pl.pallas_call(
        body, name="mixer_bwd", grid=(n_cb, nb),
        in_specs=[seg, slab(BF16), slab(F32)] + _weight_specs(0),
        out_specs=[seg, rows(3), sq, sq, rows(4), rows(8), sq, sq],
        out_shape=[
            jax.ShapeDtypeStruct(proj.shape, BF16),
            jax.ShapeDtypeStruct((3, D), F32),
            jax.ShapeDtypeStruct((n_cb, CHUNK, CHUNK), F32),
            jax.ShapeDtypeStruct((n_cb, CHUNK, LANES), F32),
            jax.ShapeDtypeStruct((4, D), F32),
            jax.ShapeDtypeStruct((8, D), F32),
            jax.ShapeDtypeStruct((n_cb, LANES, LANES), F32),
            jax.ShapeDtypeStruct((n_cb, LANES, LANES), F32),
        ],
        compiler_params=_params(("arbitrary", "arbitrary")),
    )(proj, dmerged, hs, *mw)


def _row_tile(s, want):
    return want if s % want == 0 else s


def _norm_mod(x, gain, shift, scale):
    nb, s, _ = x.shape
    tm = _row_tile(s, 512)

    def body(x_ref, g_ref, sh_ref, sc_ref, h_ref):
        xv = x_ref[...]
        r = lax.rsqrt(jnp.mean(xv * xv, axis=1, keepdims=True) + EPS)
        h_ref[...] = ((xv * r) * g_ref[...] * (1.0 + sc_ref[...]) + sh_ref[...]).astype(BF16)

    tile = pl.BlockSpec((None, tm, D), lambda b, m: (b, m, 0))
    vec = pl.BlockSpec((None, 1, D), lambda b, m: (b, 0, 0))
    return pl.pallas_call(
        body, name="norm_mod", grid=(nb, s // tm),
        in_specs=[tile, pl.BlockSpec((1, D), lambda b, m: (0, 0)), vec, vec],
        out_specs=tile, out_shape=jax.ShapeDtypeStruct(x.shape, BF16),
        compiler_params=_params(("arbitrary", "arbitrary")),
    )(x, gain, shift, scale)


def _in_proj(h, wg, dep):
    nb, s, _ = h.shape

    def body(h_ref, w_ref, dep_ref, o_ref):
        o_ref[...] = _dot(h_ref[...], w_ref[...]).astype(BF16)

    return pl.pallas_call(
        body, name="in_proj", grid=(nb, N_DEV * UNITS_PER_DEV),
        in_specs=[pl.BlockSpec((None, s, D), lambda b, u: (b, 0, 0)),
                  pl.BlockSpec((None, D, UNIT), lambda b, u: (u // UNITS_PER_DEV, 0, u % UNITS_PER_DEV)),
                  pl.BlockSpec((8, LANES), lambda b, u: (0, 0))],
        out_specs=pl.BlockSpec((None, None, s, UNIT), lambda b, u: (u // 2, b, 0, u % 2)),
        out_shape=jax.ShapeDtypeStruct((N_SEG, nb, s, D), BF16),
        compiler_params=_params(("arbitrary", "arbitrary")),
    )(h, wg, dep)


def _out_proj(x, merged, wout, gate):
    nb, s, _ = x.shape
    tm = _row_tile(s, 512)

    def body(x_ref, m_ref, w_ref, g_ref, o_ref):
        o_ref[...] = x_ref[...] + g_ref[...] * _dot(m_ref[...], w_ref[...])

    tile = pl.BlockSpec((None, tm, D), lambda b, m: (b, m, 0))
    return pl.pallas_call(
        body, name="out_proj", grid=(nb, s // tm),
        in_specs=[tile, tile, pl.BlockSpec((D, D), lambda b, m: (0, 0)),
                  pl.BlockSpec((None, 1, D), lambda b, m: (b, 0, 0))],
        out_specs=tile, out_shape=jax.ShapeDtypeStruct(x.shape, F32),
        compiler_params=_params(("arbitrary", "arbitrary")),
    )(x, merged, wout, gate)


def _loss_head(x, gain, target):
    nb, s, _ = x.shape
    tm = _row_tile(s, 512)

    def body(x_ref, g_ref, t_ref, loss_ref, dx_ref, dg_ref):
        first = (pl.program_id(0) == 0) & (pl.program_id(1) == 0)
        last = (pl.program_id(0) == nb - 1) & (pl.program_id(1) == s // tm - 1)

        @pl.when(first)
        def _():
            loss_ref[...] = jnp.zeros_like(loss_ref)
            dg_ref[...] = jnp.zeros_like(dg_ref)

        xv = x_ref[...]
        r = lax.rsqrt(jnp.mean(xv * xv, axis=1, keepdims=True) + EPS)
        xn = xv * r
        g = g_ref[...]
        e = xn * g - t_ref[...]
        loss_ref[...] += _rowsum(e * e) * (0.5 / D)
        dy = e * (1.0 / D)
        dg_ref[...] += _rowsum(dy * xn)
        dxn = dy * g
        dx_ref[...] = r * (dxn - xn * jnp.mean(dxn * xn, axis=1, keepdims=True))

        @pl.when(last)
        def _():
            loss_ref[...] = jnp.broadcast_to(jnp.sum(loss_ref[...], axis=1, keepdims=True), (1, D))

    tile = pl.BlockSpec((None, tm, D), lambda b, m: (b, m, 0))
    vec = pl.BlockSpec((1, D), lambda b, m: (0, 0))
    return pl.pallas_call(
        body, name="loss_head", grid=(nb, s // tm),
        in_specs=[tile, vec, tile], out_specs=[vec, tile, vec],
        out_shape=[jax.ShapeDtypeStruct((1, D), F32), jax.ShapeDtypeStruct(x.shape, F32),
                   jax.ShapeDtypeStruct((1, D), F32)],
        compiler_params=_params(("arbitrary", "arbitrary")),
    )(x, gain, target)


def _out_proj_bwd(dxo, merged, wout, gate):
    nb, s, _ = dxo.shape
    tm = _row_tile(s, 512)

    def body(d_ref, m_ref, w_ref, g_ref, dm_ref, gw_ref, dg_ref):
        @pl.when((pl.program_id(0) == 0) & (pl.program_id(1) == 0))
        def _():
            gw_ref[...] = jnp.zeros_like(gw_ref)

        @pl.when(pl.program_id(1) == 0)
        def _():
            dg_ref[...] = jnp.zeros_like(dg_ref)

        d = d_ref[...]
        m = m_ref[...]
        wv = w_ref[...]
        dg_ref[...] += _rowsum(d * _dot(m, wv))
        dout = (d * g_ref[...]).astype(BF16)
        dm_ref[...] = _dot_nt(dout, wv).astype(BF16)
        gw_ref[...] += _dot_tn(m, dout)

    tile = pl.BlockSpec((None, tm, D), lambda b, m: (b, m, 0))
    vec = pl.BlockSpec((None, 1, D), lambda b, m: (b, 0, 0))
    full = pl.BlockSpec((D, D), lambda b, m: (0, 0))
    return pl.pallas_call(
        body, name="out_proj_bwd", grid=(nb, s // tm),
        in_specs=[tile, tile, full, vec], out_specs=[tile, full, vec],
        out_shape=[jax.ShapeDtypeStruct(dxo.shape, BF16), jax.ShapeDtypeStruct((D, D), F32),
                   jax.ShapeDtypeStruct((nb, 1, D), F32)],
        compiler_params=_params(("arbitrary", "arbitrary")),
    )(dxo, merged, wout, gate)


def _in_proj_bwd_h(dproj, wg, dep):
    _, nb, s, _ = dproj.shape

    def body(dp_ref, w0_ref, w1_ref, dep_ref, dh_ref):
        j = pl.program_id(1)
        part = _dot_nt(dp_ref[...], jnp.concatenate([w0_ref[...], w1_ref[...]], axis=1))

        @pl.when(j == 0)
        def _():
            dh_ref[...] = part

        @pl.when(j > 0)
        def _():
            dh_ref[...] += part

    def unit(k):
        return pl.BlockSpec((None, D, UNIT),
                            lambda b, j: ((2 * j + k) // UNITS_PER_DEV, 0, (2 * j + k) % UNITS_PER_DEV))

    return pl.pallas_call(
        body, name="in_proj_bwd_h", grid=(nb, N_SEG),
        in_specs=[pl.BlockSpec((None, None, s, D), lambda b, j: (j, b, 0, 0)), unit(0), unit(1),
                  pl.BlockSpec((8, LANES), lambda b, j: (0, 0))],
        out_specs=pl.BlockSpec((None, s, D), lambda b, j: (b, 0, 0)),
        out_shape=jax.ShapeDtypeStruct((nb, s, D), F32),
        compiler_params=_params(("arbitrary", "arbitrary")),
    )(dproj, wg, wg, dep)


def _norm_mod_bwd(dh, x, dxo, gain, scale):
    nb, s, _ = x.shape
    tm = _row_tile(s, 512)

    def body(dh_ref, x_ref, dxo_ref, g_ref, sc_ref, dx_ref, dsh_ref, dsc_ref, dg_ref):
        b, m = pl.program_id(0), pl.program_id(1)

        @pl.when((b == 0) & (m == 0))
        def _():
            dg_ref[...] = jnp.zeros_like(dg_ref)

        @pl.when(m == 0)
        def _():
            dsh_ref[...] = jnp.zeros_like(dsh_ref)
            dsc_ref[...] = jnp.zeros_like(dsc_ref)

        dh = dh_ref[...]
        xv = x_ref[...]
        r = lax.rsqrt(jnp.mean(xv * xv, axis=1, keepdims=True) + EPS)
        xn = xv * r
        g = g_ref[...]
        one_sc = 1.0 + sc_ref[...]
        dsh_ref[...] += _rowsum(dh)
        dsc_ref[...] += _rowsum(dh * (xn * g))
        dg_ref[...] += _rowsum(dh * one_sc * xn)
        dxn = dh * (g * one_sc)
        dx_ref[...] = dxo_ref[...] + r * (dxn - xn * jnp.mean(dxn * xn, axis=1, keepdims=True))

    tile = pl.BlockSpec((None, tm, D), lambda b, m: (b, m, 0))
    vec = pl.BlockSpec((None, 1, D), lambda b, m: (b, 0, 0))
    one = pl.BlockSpec((1, D), lambda b, m: (0, 0))
    return pl.pallas_call(
        body, name="norm_mod_bwd", grid=(nb, s // tm),
        in_specs=[tile, tile, tile, one, vec],
        out_specs=[tile, vec, vec, one],
        out_shape=[jax.ShapeDtypeStruct(x.shape, F32), jax.ShapeDtypeStruct((nb, 1, D), F32),
                   jax.ShapeDtypeStruct((nb, 1, D), F32), jax.ShapeDtypeStruct((1, D), F32)],
        compiler_params=_params(("arbitrary", "arbitrary")),
    )(dh, x, dxo, gain, scale)


def _in_proj_bwd_w(h, dproj):
    nb, s, _ = h.shape
    tm = _row_tile(s, 1024)
    n_m = s // tm

    def body(h_ref, dp_ref, o_ref, acc_ref):
        b, m = pl.program_id(1), pl.program_id(2)

        @pl.when((b == 0) & (m == 0))
        def _():
            acc_ref[...] = jnp.zeros_like(acc_ref)

        acc_ref[...] += _dot_tn(h_ref[...], dp_ref[...])

        @pl.when((b == nb - 1) & (m == n_m - 1))
        def _():
            o_ref[0] = acc_ref[:, :UNIT].astype(BF16)
            o_ref[1] = acc_ref[:, UNIT:].astype(BF16)

    return pl.pallas_call(
        body, name="in_proj_bwd_w", grid=(N_SEG, nb, n_m),
        in_specs=[pl.BlockSpec((None, tm, D), lambda j, b, m: (b, m, 0)),
                  pl.BlockSpec((None, None, tm, D), lambda j, b, m: (j, b, m, 0))],
        out_specs=pl.BlockSpec((2, D, UNIT), lambda j, b, m: (j, 0, 0)),
        out_shape=jax.ShapeDtypeStruct((2 * N_SEG, D, UNIT), BF16),
        scratch_shapes=[pltpu.VMEM((D, D), F32)],
        compiler_params=_params(("arbitrary", "arbitrary", "arbitrary")),
    )(h, dproj)


def _mod_proj(c_all, w_mod, b_mod_mine):
    nl, _, ncol = w_mod.shape
    nbg = c_all.shape[0]

    def body(c_ref, w_ref, b_ref, o_ref):
        cv = c_ref[...]
        o_ref[...] = jnp.dot(cv * jax.nn.sigmoid(cv), w_ref[...], preferred_element_type=F32,
                             precision=lax.Precision.HIGHEST) + b_ref[...]

    return pl.pallas_call(
        body, name="mod_proj", grid=(nl,),
        in_specs=[pl.BlockSpec((nbg, D), lambda l: (0, 0)), pl.BlockSpec((None, D, ncol), lambda l: (l, 0, 0)),
                  pl.BlockSpec((None, 1, ncol), lambda l: (l, 0, 0))],
        out_specs=pl.BlockSpec((None, nbg, ncol), lambda l: (l, 0, 0)),
        out_shape=jax.ShapeDtypeStruct((nl, nbg, ncol), F32),
        compiler_params=_params(("arbitrary",)),
    )(c_all, w_mod, b_mod_mine)


def _mod_grad(c_all, dmod_all, dmod_mine):
    nl, nbg, ncol = dmod_mine.shape

    def body(c_ref, da_ref, dm_ref, gw_ref, gb_ref):
        cv = c_ref[...]
        gw_ref[...] = lax.dot_general(cv * jax.nn.sigmoid(cv), dm_ref[...], (((0,), (0,)), ((), ())),
                                      preferred_element_type=F32, precision=lax.Precision.HIGHEST)
        gb_ref[...] = _rowsum(da_ref[...])

    return pl.pallas_call(
        body, name="mod_grad", grid=(nl,),
        in_specs=[pl.BlockSpec((nbg, D), lambda l: (0, 0)), pl.BlockSpec((None, nbg, 3 * D), lambda l: (l, 0, 0)),
                  pl.BlockSpec((None, nbg, ncol), lambda l: (l, 0, 0))],
        out_specs=[pl.BlockSpec((None, D, ncol), lambda l: (l, 0, 0)),
                   pl.BlockSpec((None, 1, 3 * D), lambda l: (l, 0, 0))],
        out_shape=[jax.ShapeDtypeStruct((nl, D, ncol), F32), jax.ShapeDtypeStruct((nl, 1, 3 * D), F32)],
        compiler_params=_params(("arbitrary",)),
    )(c_all, dmod_all, dmod_mine)


def _adamw(parts, w, m, v, name, layer=None, prev=None):
    n_parts, n_u, n_r, cu = parts.shape
    assert w.shape[-2:] == (n_r, n_u * cu), (parts.shape, w.shape)
    tr = n_r
    for cand in (512, 256, 128):
        if n_r > cand and n_r % cand == 0:
            tr = cand
            break
    n_prev = 0 if prev is None else 4

    def body(p_ref, w_ref, m_ref, v_ref, *rest):
        g_ref, d_ref, nm_ref, nv_ref = rest[n_prev:]
        g = p_ref[0].astype(F32)
        for k in range(1, n_parts):
            g = g + p_ref[k].astype(F32)
        m2 = ADAM_B1 * m_ref[...] + (1.0 - ADAM_B1) * g
        v2 = ADAM_B2 * v_ref[...] + (1.0 - ADAM_B2) * (g * g)
        m_hat = m2 / (1.0 - ADAM_B1 ** ADAM_STEP)
        v_hat = v2 / (1.0 - ADAM_B2 ** ADAM_STEP)
        g_ref[...] = g
        d_ref[...] = -ADAM_LR * (m_hat / (jnp.sqrt(v_hat) + ADAM_EPS) + ADAM_WD * w_ref[...])
        nm_ref[...] = m2
        nv_ref[...] = v2

    if layer is None:
        tile = pl.BlockSpec((tr, cu), lambda u, i: (i, u))
    else:
        tile = pl.BlockSpec((None, tr, cu), lambda u, i: (layer, i, u))
    shp = jax.ShapeDtypeStruct(w.shape, F32)
    return pl.pallas_call(
        body, name=name, grid=(n_u, n_r // tr),
        in_specs=[pl.BlockSpec((n_parts, None, tr, cu), lambda u, i: (0, u, i, 0)), tile, tile, tile]
        + [pl.BlockSpec(memory_space=pl.ANY)] * n_prev,
        out_specs=[tile, tile, tile, tile], out_shape=[shp, shp, shp, shp],
        input_output_aliases={4 + k: k for k in range(n_prev)},
        compiler_params=_params(("arbitrary", "arbitrary")),
    )(parts, w, m, v, *(prev or ()))


def _gathered_cols(g, inner):
    k = len(inner)
    perm = tuple(range(1, k + 1)) + (0, k + 1)
    t = jnp.transpose(g, perm)
    return t.reshape(tuple(inner) + (g.shape[0] * g.shape[-1],))


def _pair_blocks(wh):
    z = jnp.zeros((8, 64, 64), wh.dtype)
    w2 = wh.reshape(8, 2, 64, 64)
    top = jnp.concatenate([w2[:, 0], z], axis=2)
    bot = jnp.concatenate([z, w2[:, 1]], axis=2)
    return jnp.concatenate([top, bot], axis=1).astype(BF16)


def _unpair_blocks(g):
    return jnp.stack([g[:, :64, :64], g[:, 64:, 64:]], axis=1).reshape(16, 64, 64)


FLAT_ROWS = 512


def _pack_flat(arrays):
    flat = jnp.concatenate([a.reshape(-1) for a in arrays])
    n = flat.shape[0]
    per = FLAT_ROWS * LANES
    pad = (-n) % per
    return jnp.pad(flat, (0, pad)).reshape(-1, LANES)


def kernel(x, c, norm_gain, w_mod, b_mod, w_in, w_out, conv_a_w, sgu_w, sgu_b, lru_conv_w, lru_conv_b, lru_wa, lru_ba, lru_wx, lru_bx, lru_lambda, final_gain, loss_target, m_norm_gain, m_w_mod, m_b_mod, m_w_in, m_w_out, m_conv_a_w, m_sgu_w, m_sgu_b, m_lru_conv_w, m_lru_conv_b, m_lru_wa, m_lru_ba, m_lru_wx, m_lru_bx, m_lru_lambda, m_final_gain, v_norm_gain, v_w_mod, v_b_mod, v_w_in, v_w_out, v_conv_a_w, v_sgu_w, v_sgu_b, v_lru_conv_w, v_lru_conv_b, v_lru_wa, v_lru_ba, v_lru_wx, v_lru_bx, v_lru_lambda, v_final_gain):
    nl = w_in.shape[0]
    nb, s, _ = x.shape
    me = _my_index()
    mod_cols = w_mod.shape[2]


    small = jnp.concatenate([c.reshape(-1, LANES), conv_a_w.reshape(-1, LANES), lru_conv_w.reshape(-1, LANES)])
    n_c, n_ca = nb * D // LANES, nl * 3
    n_small = small.shape[0]
    small = jnp.pad(small, ((0, (-n_small) % 8), (0, 0)))
    small_all, _ = _all_gather(small, "gather_small")
    c_all = small_all[:, :n_c].reshape(N_DEV * nb, D)
    conv_a_full = _gathered_cols(small_all[:, n_c:n_c + n_ca].reshape(N_DEV, nl, 3, LANES), (nl, 3))
    lru_conv_full = _gathered_cols(small_all[:, n_c + n_ca:n_small].reshape(N_DEV, nl, 4, LANES), (nl, 4))

    b_mod_mine = lax.dynamic_slice_in_dim(b_mod, me * mod_cols, mod_cols, axis=1)[:, None, :]
    mod_mine = _mod_proj(c_all, w_mod, b_mod_mine)
    mod_all, mod_token = _all_gather(mod_mine.reshape(nl * N_DEV * nb, mod_cols), "gather_mod")
    mod_full = _gathered_cols(mod_all.reshape(N_DEV, nl, N_DEV * nb, mod_cols), (nl, N_DEV * nb))
    mod_loc = lax.dynamic_slice_in_dim(mod_full, me * nb, nb, axis=1)
    shift, scale, gate = [mod_loc[:, :, j * D:(j + 1) * D][:, :, None, :] for j in range(3)]

    def start_w_in(l, token):
        return _split_start(_after(w_in[l], token).astype(BF16), _peers_same_core, False, "gather_w_in_start")

    def gathered_w_in(started, after):
        block, land = _split_wait(started, after, _peers_same_core, False, False, "gather_w_in_wait")
        return _gather_finish(block, land, "gather_w_in_finish")

    sgu_b_lanes = jnp.broadcast_to(sgu_b[..., None], sgu_b.shape + (LANES,))
    mws = []
    for l in range(nl):
        mws.append((conv_a_full[l], sgu_w[l], sgu_b_lanes[l], lru_conv_full[l], lru_conv_b[l][None, :],
                    _pair_blocks(lru_wa[l]), _pair_blocks(lru_wx[l]), lru_ba[l].reshape(1, D),
                    lru_bx[l].reshape(1, D), lru_lambda[l][None, :]))

    xs, hs_bf, projs, mergeds, states, wg = [], [], [], [], [], []
    xl = x
    wg_started = start_w_in(0, mod_token)
    wo_started = _split_start(_after(w_out, wg_started[4]).astype(BF16).reshape(nl * (D // N_DEV), D),
                              _peers_all, False, "gather_w_out_start")
    wo = None
    for l in range(nl):
        h = _norm_mod(xl, _after(norm_gain[l][None, :], wo_started[4]), shift[l], scale[l])
        wg_l, token = gathered_w_in(wg_started, h)
        wg.append(wg_l)
        if l + 1 < nl:
            wg_started = start_w_in(l + 1, token)
            token = wg_started[4]
        proj = _in_proj(h, wg_l, token)
        merged, st = _mixer_fwd(proj, mws[l])
        xs.append(xl), hs_bf.append(h), projs.append(proj), mergeds.append(merged), states.append(st)
        if wo is None:
            _, wo_all = _split_wait(wo_started, merged, _peers_all, False, True, "gather_w_out_wait")
            wo = jnp.transpose(wo_all.reshape(N_DEV, nl, D // N_DEV, D), (1, 0, 2, 3)).reshape(nl, D, D)
        xl = _out_proj(xl, merged, wo[l], gate[l])

    loss_row, dx, g_final = _loss_head(xl, final_gain[None, :], loss_target)
    loss = lax.psum(loss_row[0, 0], ("x", "y", "c"))

    rep_names = ["sgu_w", "sgu_b", "lru_conv_b", "lru_wa", "lru_ba", "lru_wx", "lru_bx", "lru_lambda"]
    rep_w = dict(norm_gain=norm_gain, sgu_w=sgu_w, sgu_b=sgu_b, lru_conv_b=lru_conv_b, lru_wa=lru_wa, lru_ba=lru_ba,
                 lru_wx=lru_wx, lru_bx=lru_bx, lru_lambda=lru_lambda)
    rep_m = dict(norm_gain=m_norm_gain, sgu_w=m_sgu_w, sgu_b=m_sgu_b, lru_conv_b=m_lru_conv_b, lru_wa=m_lru_wa,
                 lru_ba=m_lru_ba, lru_wx=m_lru_wx, lru_bx=m_lru_bx, lru_lambda=m_lru_lambda)
    rep_v = dict(norm_gain=v_norm_gain, sgu_w=v_sgu_w, sgu_b=v_sgu_b, lru_conv_b=v_lru_conv_b, lru_wa=v_lru_wa,
                 lru_ba=v_lru_ba, lru_wx=v_lru_wx, lru_bx=v_lru_bx, lru_lambda=v_lru_lambda)

    def rep_pack_all(src):
        flat = jnp.concatenate([src[n].reshape(nl, -1) for n in rep_names], axis=1)
        pad = (-flat.shape[1]) % (FLAT_ROWS * LANES)
        return jnp.pad(flat, ((0, 0), (0, pad))).reshape(nl, -1, LANES)

    rep_w_all, rep_m_all, rep_v_all = rep_pack_all(rep_w), rep_pack_all(rep_m), rep_pack_all(rep_v)
    res_big, g_conv = {}, [None] * nl
    dmods = [None] * nl

    def finish_exchange(pending, after):
        l, h_in, h_out, h_rep = pending
        _, recv = _split_wait(h_in, after, _peers_all, True, True, "scatter_w_in_wait")
        res_big["w_in"] = _adamw(recv, w_in, m_w_in, v_w_in, "adamw_w_in", l, res_big.get("w_in"))
        _, recv = _split_wait(h_out, after, _peers_all, True, True, "scatter_w_out_wait")
        res_big["w_out"] = _adamw(recv, w_out, m_w_out, v_w_out, "adamw_w_out", l, res_big.get("w_out"))
        _, recv = _split_wait(h_rep, after, _peers_all, False, True, "gather_rep_wait")
        res_big["rep"] = _adamw(recv[:, None], rep_w_all, rep_m_all, rep_v_all, "adamw_rep", l, res_big.get("rep"))

    pending = None
    g_gains = [None] * nl
    for l in reversed(range(nl)):
        dmerged, gw_out, dgate = _out_proj_bwd(dx, mergeds[l], wo[l], gate[l])
        dproj, g_caw, g_sw, g_sb, g_lcw, g_vec, g_wa, g_wx = _mixer_bwd(projs[l], dmerged, states[l], mws[l])
        gw_in = _in_proj_bwd_w(hs_bf[l], dproj)
        g_conv[l] = (g_caw, g_lcw)
        rep_g = dict(
            sgu_w=g_sw, sgu_b=g_sb[:, :, 0], lru_conv_b=g_vec[0],
            lru_wa=_unpair_blocks(g_wa), lru_ba=g_vec[1].reshape(16, 64), lru_wx=_unpair_blocks(g_wx),
            lru_bx=g_vec[2].reshape(16, 64), lru_lambda=g_vec[3])
        rep_block = _pack_flat([rep_g[n] for n in rep_names])
        started = (l,
                   _split_start(gw_in.reshape(N_DEV, UNITS_PER_DEV, D, UNIT), _peers_all, True, "scatter_w_in_start"),
                   _split_start(gw_out.reshape(N_DEV, 1, D // N_DEV, D), _peers_all, True, "scatter_w_out_start"),
                   _split_start(rep_block, _peers_all, False, "gather_rep_start"))
        dep = started[1][4] + started[2][4] + started[3][4]
        dh = _in_proj_bwd_h(dproj, wg[l], dep)
        dx, dshift, dscale, g_gain = _norm_mod_bwd(dh, xs[l], dx, norm_gain[l][None, :], scale[l])
        g_gains[l] = g_gain
        dmods[l] = jnp.concatenate([dshift, dscale, dgate], axis=2)[:, 0, :]
        if pending is not None:
            finish_exchange(pending, dx)
        pending = started

    conv_parts = jnp.concatenate(
        [jnp.stack([g_conv[l][0] for l in range(nl)]).reshape(nl * 3, N_DEV, LANES),
         jnp.stack([g_conv[l][1] for l in range(nl)]).reshape(nl * 4, N_DEV, LANES)], axis=0)
    conv_parts = jnp.transpose(conv_parts, (1, 0, 2))[:, None]
    conv_recv = _all_to_all(conv_parts, "scatter_conv")

    dmod_loc = jnp.stack(dmods).reshape(nl * nb, 3 * D)
    gain_rows = jnp.pad(jnp.concatenate(g_gains + [g_final], axis=0), ((0, (-(nl + 1)) % 8), (0, 2 * D)))
    tail_g, _ = _all_gather(jnp.concatenate([dmod_loc, gain_rows], axis=0), "gather_dmod",
                            dep=res_big["w_in"][3])
    dmod_g = tail_g[:, :nl * nb]
    gain_parts = tail_g[:, nl * nb:nl * nb + nl + 1, :D][:, None]
    gain_cat = lambda a, b: jnp.concatenate([a, b[None, :]], axis=0)
    res_gain = _adamw(gain_parts, gain_cat(norm_gain, final_gain), gain_cat(m_norm_gain, m_final_gain),
                      gain_cat(v_norm_gain, v_final_gain), "adamw_gain")
    dmod_all = jnp.transpose(dmod_g.reshape(N_DEV, nl, nb, 3 * D), (1, 0, 2, 3)).reshape(nl, N_DEV * nb, 3 * D)
    dmod_mine = lax.dynamic_slice_in_dim(dmod_all, me * mod_cols, mod_cols, axis=2)
    gw_mod, gb_mod = _mod_grad(c_all, dmod_all, dmod_mine)
    res_w_mod = _adamw(gw_mod.reshape(1, 1, nl * D, mod_cols), w_mod.reshape(nl * D, mod_cols),
                       m_w_mod.reshape(nl * D, mod_cols), v_w_mod.reshape(nl * D, mod_cols), "adamw_w_mod")
    res_w_mod = [a.reshape(nl, D, mod_cols) for a in res_w_mod]
    res_b_mod = _adamw(gb_mod.reshape(1, 1, nl, 3 * D), b_mod, m_b_mod, v_b_mod, "adamw_b_mod")
    finish_exchange(pending, res_b_mod[1])

    cat = lambda a, b: jnp.concatenate([a.reshape(nl * 3, LANES), b.reshape(nl * 4, LANES)], axis=0)
    res_conv = _adamw(conv_recv, cat(conv_a_w, lru_conv_w), cat(m_conv_a_w, m_lru_conv_w),
                      cat(v_conv_a_w, v_lru_conv_w), "adamw_conv")
    res_conv_a = [a[:nl * 3].reshape(nl, 3, LANES) for a in res_conv]
    res_lru_conv = [a[nl * 3:].reshape(nl, 4, LANES) for a in res_conv]

    res_rep = []
    for k in range(4):
        flat, off, d = res_big["rep"][k].reshape(nl, -1), 0, {}
        for n in rep_names:
            size = rep_w[n][0].size
            d[n] = flat[:, off:off + size].reshape(rep_w[n].shape)
            off += size
        res_rep.append(d)

    def leaf(k, name):
        if name == "norm_gain":
            return res_gain[k][:nl]
        if name == "final_gain":
            return res_gain[k][nl]
        if name == "w_mod":
            return res_w_mod[k]
        if name == "b_mod":
            return res_b_mod[k]
        if name in ("w_in", "w_out"):
            return res_big[name][k]
        if name == "conv_a_w":
            return res_conv_a[k]
        if name == "lru_conv_w":
            return res_lru_conv[k]
        return res_rep[k][name]

    order = ["norm_gain", "w_mod", "b_mod", "w_in", "w_out", "conv_a_w", "sgu_w", "sgu_b", "lru_conv_w",
             "lru_conv_b", "lru_wa", "lru_ba", "lru_wx", "lru_bx", "lru_lambda", "final_gain"]
    outs = [loss, dx]
    for k in range(4):
        outs += [leaf(k, n) for n in order]
    return tuple(outs)
```

```python
import functools

import jax
import jax.numpy as jnp
from jax import lax
from jax.experimental import pallas as pl
from jax.experimental.pallas import tpu as pltpu

F32 = jnp.float32
BF16 = jnp.bfloat16

D = 1024
N_DEV = 8
N_SEG = 12
LANES = 128
SUBLANES = 8
CHUNK = 128
HALO = 16
UNIT = 512
UNITS_PER_DEV = 3
EPS = 1e-6
LRU_C = 8.0
ADAM_LR, ADAM_B1, ADAM_B2, ADAM_EPS, ADAM_WD, ADAM_STEP = 0.001, 0.9, 0.999, 1e-08, 0.01, 10
VMEM_LIMIT = 56 * 1024 * 1024

AX, AB, AC, AZ, SU, SV, SZ, RX, RZ, GA, GS, GR = range(N_SEG)
MESH = pl.DeviceIdType.MESH


def _params(sem=None):
    return pltpu.CompilerParams(dimension_semantics=sem, vmem_limit_bytes=VMEM_LIMIT)


def _my_index():
    return 4 * lax.axis_index("x") + 2 * lax.axis_index("y") + lax.axis_index("c")


def _all_gather(block, name, dep=None):
    def body(x_ref, *refs):
        out_ref, token, send_sems, recv_sems, local_sem = refs[-5:]
        x, y, c = lax.axis_index("x"), lax.axis_index("y"), lax.axis_index("c")
        me, sibling = (x, y, c), (x, y, 1 - c)
        chips = [(1 - x, y), (x, 1 - y), (1 - x, 1 - y)]
        token[...] = jnp.zeros_like(token)

        def rows(px, py, pc):
            return out_ref.at[4 * px + 2 * py + pc]

        def copy(k, blk, to, src=None):
            return pltpu.make_async_remote_copy(
                src_ref=rows(*blk) if src is None else src, dst_ref=rows(*blk),
                send_sem=send_sems.at[k], recv_sem=recv_sems.at[k], device_id=to, device_id_type=MESH)

        mine = pltpu.make_async_copy(x_ref, rows(*me), local_sem)
        mine.start()
        first = [copy(0, me, sibling, src=x_ref)]
        first += [copy(1 + j, me, (*chip, c), src=x_ref) for j, chip in enumerate(chips)]
        for cp in first:
            cp.start()
        passed = [copy(4 + j, (*chip, c), sibling) for j, chip in enumerate(chips)]
        for j, chip in enumerate(chips):
            copy(1 + j, (*chip, c), me).wait_recv()
            passed[j].start()
        copy(0, sibling, me).wait_recv()
        for j, chip in enumerate(chips):
            copy(4 + j, (*chip, 1 - c), me).wait_recv()
        for cp in first + passed:
            cp.wait_send()
        mine.wait()

    return pl.pallas_call(
        body, name=name,
        out_shape=[jax.ShapeDtypeStruct((N_DEV,) + block.shape, block.dtype), jax.ShapeDtypeStruct((8, LANES), F32)],
        in_specs=[pl.BlockSpec(memory_space=pltpu.VMEM)]
        + [pl.BlockSpec(memory_space=pl.ANY)] * (dep is not None),
        out_specs=[pl.BlockSpec(memory_space=pl.ANY), pl.BlockSpec(memory_space=pltpu.VMEM)],
        scratch_shapes=[pltpu.SemaphoreType.DMA((7,)), pltpu.SemaphoreType.DMA((7,)), pltpu.SemaphoreType.DMA],
    )(block, *([dep] if dep is not None else []))


def _all_to_all(blocks, name):
    def body(x_ref, out_ref, send_sems, recv_sems, local_sem):
        x, y, c = lax.axis_index("x"), lax.axis_index("y"), lax.axis_index("c")
        my = 4 * x + 2 * y + c
        mine = pltpu.make_async_copy(x_ref.at[my], out_ref.at[my], local_sem)
        mine.start()
        peers = []
        for r in range(1, N_DEV):
            px = 1 - x if r & 4 else x
            py = 1 - y if r & 2 else y
            pc = 1 - c if r & 1 else c
            peers.append((r - 1, 4 * px + 2 * py + pc, (px, py, pc)))

        def copy(k, src_slot, dst_slot, to):
            return pltpu.make_async_remote_copy(
                src_ref=x_ref.at[src_slot], dst_ref=out_ref.at[dst_slot],
                send_sem=send_sems.at[k], recv_sem=recv_sems.at[k], device_id=to, device_id_type=MESH)

        sends = [copy(k, pid, my, to) for k, pid, to in peers]
        for cp in sends:
            cp.start()
        for k, pid, to in peers:
            copy(k, pid, pid, to).wait_recv()
        for cp in sends:
            cp.wait_send()
        mine.wait()

    return pl.pallas_call(
        body, name=name,
        out_shape=jax.ShapeDtypeStruct(blocks.shape, blocks.dtype),
        in_specs=[pl.BlockSpec(memory_space=pltpu.VMEM)],
        out_specs=pl.BlockSpec(memory_space=pl.ANY),
        scratch_shapes=[pltpu.SemaphoreType.DMA((7,)), pltpu.SemaphoreType.DMA((7,)), pltpu.SemaphoreType.DMA],
    )(blocks)


_HBM = pl.BlockSpec(memory_space=pltpu.HBM)
_SEM = pl.BlockSpec(memory_space=pltpu.SEMAPHORE)
_EFFECT = pltpu.SideEffectType.DATAFLOW_SIDE_EFFECTING


def _peers_all(x, y, c):
    out = []
    for r in range(1, N_DEV):
        px = 1 - x if r & 4 else x
        py = 1 - y if r & 2 else y
        pc = 1 - c if r & 1 else c
        out.append((r - 1, 4 * px + 2 * py + pc, (px, py, pc)))
    return out


def _peers_same_core(x, y, c):
    return [(k, 4 * px + 2 * py + c, (px, py, c))
            for k, (px, py) in enumerate([(1 - x, y), (x, 1 - y), (1 - x, 1 - y)])]


def _split_start(src, peers_fn, scatter, name):
    blk = src.shape[1:] if scatter else src.shape
    land_shape = (N_DEV,) + tuple(blk)
    n = len(peers_fn(0, 0, 0))

    def body(x_ref, land_ref, send_sems, recv_sems, x_thru, land_thru, token):
        x, y, c = lax.axis_index("x"), lax.axis_index("y"), lax.axis_index("c")
        my = 4 * x + 2 * y + c
        for k, pid, to in peers_fn(x, y, c):
            pltpu.make_async_remote_copy(
                src_ref=x_ref.at[pid] if scatter else x_ref, dst_ref=land_ref.at[my],
                send_sem=send_sems.at[k], recv_sem=recv_sems.at[k], device_id=to, device_id_type=MESH).start()
        token[...] = jnp.zeros_like(token)

    return pl.pallas_call(
        body, name=name,
        out_shape=(pltpu.SemaphoreType.DMA((n,)), pltpu.SemaphoreType.DMA((n,)),
                   pltpu.HBM(src.shape, src.dtype), pltpu.HBM(land_shape, src.dtype),
                   jax.ShapeDtypeStruct((8, LANES), F32)),
        in_specs=(_HBM, _HBM),
        out_specs=(_SEM, _SEM, _HBM, _HBM, pl.BlockSpec(memory_space=pltpu.VMEM)),
        input_output_aliases={0: 2, 1: 3},
        compiler_params=pltpu.CompilerParams(has_side_effects=_EFFECT),
    )(pltpu.with_memory_space_constraint(src, pltpu.HBM),
      pltpu.with_memory_space_constraint(lax.empty(land_shape, src.dtype), pltpu.HBM))


def _split_wait(handles, after, peers_fn, scatter, own, name):
    send_sems, recv_sems, src_thru, land_thru, _ = handles
    blk = land_thru.shape[1:]
    after = list(after) if isinstance(after, (list, tuple)) else [after]

    def body(x_ref, land_ref, send_sems, recv_sems, *rest):
        stage = rest[len(after) + 2:]
        x, y, c = lax.axis_index("x"), lax.axis_index("y"), lax.axis_index("c")
        if own:
            my = 4 * x + 2 * y + c
            mine = _staged_copy(x_ref.at[my] if scatter else x_ref, land_ref.at[my], *stage)
        for k, pid, to in peers_fn(x, y, c):
            cp = pltpu.make_async_remote_copy(
                src_ref=x_ref.at[pid] if scatter else x_ref, dst_ref=land_ref.at[pid],
                send_sem=send_sems.at[k], recv_sem=recv_sems.at[k], device_id=to, device_id_type=MESH)
            cp.wait_send()
            cp.wait_recv()
        if own:
            mine.wait()

    return pl.pallas_call(
        body, name=name,
        out_shape=(pltpu.HBM(src_thru.shape, src_thru.dtype), pltpu.HBM(land_thru.shape, land_thru.dtype)),
        in_specs=(_HBM, _HBM, _SEM, _SEM) + (pl.BlockSpec(memory_space=pl.ANY),) * len(after),
        out_specs=(_HBM, _HBM),
        input_output_aliases={0: 0, 1: 1},
        scratch_shapes=[pltpu.VMEM(blk, land_thru.dtype), pltpu.SemaphoreType.DMA((2,))] if own else [],
        compiler_params=pltpu.CompilerParams(has_side_effects=_EFFECT, vmem_limit_bytes=VMEM_LIMIT),
    )(src_thru, land_thru, send_sems, recv_sems, *after)


def _staged_copy(src_ref, dst_ref, buf, sems):
    leg = pltpu.make_async_copy(src_ref, buf, sems.at[0])
    leg.start()
    leg.wait()
    leg = pltpu.make_async_copy(buf, dst_ref, sems.at[1])
    leg.start()
    return leg


def _gather_finish(block, land, name):
    def body(x_ref, land_ref, out_ref, token, send_sems, recv_sems, buf, local_sems):
        x, y, c = lax.axis_index("x"), lax.axis_index("y"), lax.axis_index("c")
        my, sib_id, sibling = 4 * x + 2 * y + c, 4 * x + 2 * y + 1 - c, (x, y, 1 - c)
        token[...] = jnp.zeros_like(token)

        def copy(k, slot, src=None):
            return pltpu.make_async_remote_copy(
                src_ref=land_ref.at[slot] if src is None else src, dst_ref=out_ref.at[slot],
                send_sem=send_sems.at[k], recv_sem=recv_sems.at[k], device_id=sibling, device_id_type=MESH)

        chips = _peers_same_core(x, y, c)
        sends = [copy(0, my, src=x_ref)] + [copy(1 + k, pid) for k, pid, _ in chips]
        for cp in sends:
            cp.start()
        mine = _staged_copy(x_ref, out_ref.at[my], buf, local_sems)
        copy(0, sib_id).wait_recv()
        for k, pid, _ in chips:
            copy(1 + k, pid + 1 - 2 * c).wait_recv()
        for cp in sends:
            cp.wait_send()
        mine.wait()

    return pl.pallas_call(
        body, name=name,
        out_shape=[jax.ShapeDtypeStruct(land.shape, land.dtype), jax.ShapeDtypeStruct((8, LANES), F32)],
        in_specs=[pl.BlockSpec(memory_space=pl.ANY), pl.BlockSpec(memory_space=pl.ANY)],
        out_specs=[pl.BlockSpec(memory_space=pl.ANY), pl.BlockSpec(memory_space=pltpu.VMEM)],
        input_output_aliases={1: 0},
        scratch_shapes=[pltpu.SemaphoreType.DMA((4,)), pltpu.SemaphoreType.DMA((4,)),
                        pltpu.VMEM(block.shape, block.dtype), pltpu.SemaphoreType.DMA((2,))],
        compiler_params=pltpu.CompilerParams(vmem_limit_bytes=VMEM_LIMIT),
    )(block, land)


def _after(v, token):
    return v + token[0, 0].astype(v.dtype)


def _dsilu(x, s):
    return s * (1.0 + x * (1.0 - s))


def _log1p(x):
    u = 1.0 + x
    d = u - 1.0
    return jnp.where(d == 0.0, x, jnp.log(u) * (x / jnp.where(d == 0.0, 1.0, d)))


def _softplus_neg(lam):
    return jnp.maximum(-lam, 0.0) + _log1p(jnp.exp(-jnp.abs(lam)))


def _neg_expm1(y, exp_y):
    poly = -y * (1.0 + y * (0.5 + y * (1.0 / 6.0 + y * (1.0 / 24.0))))
    return jnp.where(y > -0.05, poly, 1.0 - exp_y)


def _sigmoid(x):
    return 0.5 * jnp.tanh(0.5 * x) + 0.5


def _shift_dn(cur, prev, k):
    ext = jnp.concatenate([prev, cur], axis=0)
    return pltpu.roll(ext, k, 0)[HALO:, :]


def _shift_up(cur, nxt, k):
    n = cur.shape[0]
    ext = jnp.concatenate([cur, nxt], axis=0)
    return pltpu.roll(ext, n + HALO - k, 0)[:n, :]


def _scan_fwd(a, b, h_prev):
    groups = a.shape[0] // SUBLANES
    a3 = a.reshape(groups, SUBLANES, LANES)
    b3 = b.reshape(groups, SUBLANES, LANES)
    row = lax.broadcasted_iota(jnp.int32, a3.shape, 1)
    k = 1
    while k < SUBLANES:
        a_sh = jnp.where(row >= k, pltpu.roll(a3, k, 1), 1.0)
        b_sh = jnp.where(row >= k, pltpu.roll(b3, k, 1), 0.0)
        b3 = a3 * b_sh + b3
        a3 = a3 * a_sh
        k *= 2
    carry = h_prev[HALO - 1:HALO, :]
    out = []
    for i in range(groups):
        hg = b3[i] + a3[i] * carry
        out.append(hg)
        carry = hg[SUBLANES - 1:SUBLANES, :]
    return jnp.concatenate(out, axis=0)


def _scan_rev(a_next, g, lam_next):
    groups = g.shape[0] // SUBLANES
    a3 = a_next.reshape(groups, SUBLANES, LANES)
    g3 = g.reshape(groups, SUBLANES, LANES)
    row = lax.broadcasted_iota(jnp.int32, a3.shape, 1)
    k = 1
    while k < SUBLANES:
        ok = row < SUBLANES - k
        a_sh = jnp.where(ok, pltpu.roll(a3, SUBLANES - k, 1), 1.0)
        g_sh = jnp.where(ok, pltpu.roll(g3, SUBLANES - k, 1), 0.0)
        g3 = g3 + a3 * g_sh
        a3 = a3 * a_sh
        k *= 2
    carry = lam_next[0:1, :]
    out = [None] * groups
    for i in reversed(range(groups)):
        lg = g3[i] + a3[i] * carry
        out[i] = lg
        carry = lg[0:1, :]
    return jnp.concatenate(out, axis=0)


def _rowsum(v):
    return jnp.sum(v, axis=0, keepdims=True)


def _dot(a, b):
    return jnp.dot(a, b, preferred_element_type=F32)


def _dot_nt(a, b):
    return lax.dot_general(a, b, (((1,), (1,)), ((), ())), preferred_element_type=F32)


def _dot_tn(a, b):
    return lax.dot_general(a, b, (((0,), (0,)), ((), ())), preferred_element_type=F32)


class _MixerWeights:
    def __init__(self, caw_ref, sw_ref, sb_ref, lcw_ref, lcb_ref, wa_ref, wx_ref, ba_ref, bx_ref, lam_ref):
        self.caw = [caw_ref[j:j + 1, :] for j in range(3)]
        self.lcw = [lcw_ref[j:j + 1, :] for j in range(4)]
        self.lcb = lcb_ref[...]
        row = lax.broadcasted_iota(jnp.int32, (CHUNK, CHUNK), 0)
        col = lax.broadcasted_iota(jnp.int32, (CHUNK, CHUNK), 1)
        self.tril = col <= row
        self.sw = jnp.where(self.tril, sw_ref[...], 0.0).astype(BF16)
        self.sb = sb_ref[...]
        self.wa = wa_ref[...]
        self.wx = wx_ref[...]
        self.ba = ba_ref[...]
        self.bx = bx_ref[...]
        lam = lam_ref[...]
        self.neg_c_sp = -LRU_C * _softplus_neg(lam)
        self.dsp_dlam = -_sigmoid(-lam)


def _mixer_pre_scan(ld, ldp, w, backward):
    t = {}
    a_x, a_c = ld(AX), ld(AC)
    t["a_x"], t["a_c"], t["a_b"], t["a_z"] = a_x, a_c, ld(AB), ld(AZ)
    ca = a_c * a_x
    ca_p = ldp(AC) * ldp(AX)
    t["ca"], t["ca1"], t["ca2"] = ca, _shift_dn(ca, ca_p, 1), _shift_dn(ca, ca_p, 2)
    t["cv"] = w.caw[2] * ca + w.caw[1] * t["ca1"] + w.caw[0] * t["ca2"]
    t["sa"] = _sigmoid(t["a_z"])
    t["silu_az"] = t["a_z"] * t["sa"]
    t["y_a"] = t["silu_az"] * t["a_b"] * t["cv"]

    v = ld(SV)
    vc = v - jnp.mean(v, axis=1, keepdims=True)
    t["rstd"] = lax.rsqrt(jnp.mean(vc * vc, axis=1, keepdims=True) + EPS)
    t["vn"] = vc * t["rstd"]
    t["z"] = _dot(w.sw, t["vn"].astype(BF16)) + w.sb
    t["s_u"], t["s_z"] = ld(SU), ld(SZ)
    t["ss"] = _sigmoid(t["s_z"])
    t["silu_sz"] = t["s_z"] * t["ss"]
    t["y_s"] = t["silu_sz"] * t["s_u"] * t["z"]

    r_x, r_xp = ld(RX), ldp(RX)
    t["rx"] = [_shift_dn(r_x, r_xp, 3), _shift_dn(r_x, r_xp, 2), _shift_dn(r_x, r_xp, 1), r_x]
    xc = w.lcb + w.lcw[0] * t["rx"][0] + w.lcw[1] * t["rx"][1] + w.lcw[2] * t["rx"][2] + w.lcw[3] * r_x
    t["xc"] = xc
    xcb = xc.astype(BF16)
    t["r"] = _sigmoid(_dot(xcb, w.wa) + w.ba)
    t["i"] = _sigmoid(_dot(xcb, w.wx) + w.bx)
    la = t["r"] * w.neg_c_sp
    t["a"] = jnp.exp(la)
    t["em"] = _neg_expm1(2.0 * la, t["a"] * t["a"])
    if backward:
        t["inv_mult"] = lax.rsqrt(t["em"])
        t["mult"] = t["em"] * t["inv_mult"]
    else:
        t["mult"] = jnp.sqrt(t["em"])
    t["b"] = t["mult"] * (t["i"] * xc)
    t["r_z"] = ld(RZ)
    t["sr"] = _sigmoid(t["r_z"])
    t["silu_rz"] = t["r_z"] * t["sr"]
    t["ga"], t["gs"], t["gr"] = _sigmoid(ld(GA)), _sigmoid(ld(GS)), _sigmoid(ld(GR))
    return t


def _weight_specs(n_cb_axis):
    def at(fn):
        return lambda *g: fn(g[n_cb_axis])
    return [
        pl.BlockSpec((3, LANES), at(lambda cb: (0, cb))),
        pl.BlockSpec((None, CHUNK, CHUNK), at(lambda cb: (cb, 0, 0))),
        pl.BlockSpec((None, CHUNK, LANES), at(lambda cb: (cb, 0, 0))),
        pl.BlockSpec((4, LANES), at(lambda cb: (0, cb))),
        pl.BlockSpec((1, LANES), at(lambda cb: (0, cb))),
        pl.BlockSpec((None, LANES, LANES), at(lambda cb: (cb, 0, 0))),
        pl.BlockSpec((None, LANES, LANES), at(lambda cb: (cb, 0, 0))),
        pl.BlockSpec((1, LANES), at(lambda cb: (0, cb))),
        pl.BlockSpec((1, LANES), at(lambda cb: (0, cb))),
        pl.BlockSpec((1, LANES), at(lambda cb: (0, cb))),
    ]


def _chunk_loaders(p_ref, c):
    r0 = pl.multiple_of(c * CHUNK, CHUNK)
    rp = pl.multiple_of(jnp.maximum(c * CHUNK - HALO, 0), HALO)

    def ld(j):
        return p_ref[j, pl.ds(r0, CHUNK), :].astype(F32)

    def ldp(j):
        return jnp.where(c > 0, p_ref[j, pl.ds(rp, HALO), :].astype(F32), 0.0)

    return r0, rp, ld, ldp


def _mixer_fwd(proj, mw):
    _, nb, s, _ = proj.shape
    n_chunks = s // CHUNK

    def body(p_ref, *refs):
        w = _MixerWeights(*refs[:10])
        merged_ref, hs_ref = refs[10:]

        def chunk(c, h_prev):
            r0, _, ld, ldp = _chunk_loaders(p_ref, c)
            t = _mixer_pre_scan(ld, ldp, w, False)
            h = _scan_fwd(t["a"], t["b"], h_prev)
            y_r = t["silu_rz"] * h
            merged = t["ga"] * t["y_a"] + t["gs"] * t["y_s"] + t["gr"] * y_r
            merged_ref[pl.ds(r0, CHUNK), :] = merged.astype(BF16)
            hs_ref[pl.ds(r0, CHUNK), :] = h
            return h[CHUNK - HALO:, :]

        lax.fori_loop(0, n_chunks, chunk, jnp.zeros((HALO, LANES), F32))

    slab = pl.BlockSpec((None, s, LANES), lambda cb, b: (b, 0, cb))
    return pl.pallas_call(
        body, name="mixer_fwd", grid=(D // LANES, nb),
        in_specs=[pl.BlockSpec((N_SEG, None, s, LANES), lambda cb, b: (0, b, 0, cb))] + _weight_specs(0),
        out_specs=[slab, slab],
        out_shape=[jax.ShapeDtypeStruct((nb, s, D), BF16), jax.ShapeDtypeStruct((nb, s, D), F32)],
        compiler_params=_params(("arbitrary", "arbitrary")),
    )(proj, *mw)


def _mixer_bwd(proj, dmerged, hs, mw):
    _, nb, s, _ = proj.shape
    n_chunks = s // CHUNK

    def body(p_ref, dm_ref, hs_ref, *refs):
        w = _MixerWeights(*refs[:10])
        dp_ref, g_caw, g_sw, g_sb, g_lcw, g_vec, g_wa, g_wx = refs[10:]

        @pl.when(pl.program_id(1) == 0)
        def _():
            for ref in (g_caw, g_sw, g_sb, g_lcw, g_vec, g_wa, g_wx):
                ref[...] = jnp.zeros_like(ref)

        def chunk(i, carry):
            dcv_n, dxc_n, lam_n, a_n = carry
            c = n_chunks - 1 - i
            r0, rp, ld, ldp = _chunk_loaders(p_ref, c)
            t = _mixer_pre_scan(ld, ldp, w, True)
            h = hs_ref[pl.ds(r0, CHUNK), :]
            h_p = jnp.where(c > 0, hs_ref[pl.ds(rp, HALO), :], 0.0)
            h_prev = _shift_dn(h, h_p, 1)
            dm = dm_ref[pl.ds(r0, CHUNK), :].astype(F32)
            y_r = t["silu_rz"] * h

            def out(j, val):
                dp_ref[j, pl.ds(r0, CHUNK), :] = val.astype(BF16)

            ga, gs, gr = t["ga"], t["gs"], t["gr"]
            out(GA, dm * t["y_a"] * ga * (1.0 - ga))
            out(GS, dm * t["y_s"] * gs * (1.0 - gs))
            out(GR, dm * y_r * gr * (1.0 - gr))

            dy_a = dm * ga
            out(AZ, dy_a * t["a_b"] * t["cv"] * _dsilu(t["a_z"], t["sa"]))
            out(AB, dy_a * t["silu_az"] * t["cv"])
            dcv = dy_a * t["silu_az"] * t["a_b"]
            dca = w.caw[2] * dcv + w.caw[1] * _shift_up(dcv, dcv_n, 1) + w.caw[0] * _shift_up(dcv, dcv_n, 2)
            out(AC, dca * t["a_x"])
            out(AX, dca * t["a_c"])
            g_caw[2:3, :] += _rowsum(dcv * t["ca"])
            g_caw[1:2, :] += _rowsum(dcv * t["ca1"])
            g_caw[0:1, :] += _rowsum(dcv * t["ca2"])

            dy_s = dm * gs
            out(SZ, dy_s * t["s_u"] * t["z"] * _dsilu(t["s_z"], t["ss"]))
            out(SU, dy_s * t["silu_sz"] * t["z"])
            dz = dy_s * t["silu_sz"] * t["s_u"]
            dzb = dz.astype(BF16)
            g_sb[...] += jnp.broadcast_to(jnp.sum(dz, axis=1, keepdims=True), (CHUNK, LANES))
            g_sw[...] += _dot_nt(dzb, t["vn"].astype(BF16))
            dvn = _dot_tn(w.sw, dzb)
            vn = t["vn"]
            out(SV, t["rstd"] * (dvn - jnp.mean(dvn, axis=1, keepdims=True)
                                 - vn * jnp.mean(dvn * vn, axis=1, keepdims=True)))

            dy_r = dm * gr
            out(RZ, dy_r * h * _dsilu(t["r_z"], t["sr"]))
            lam = _scan_rev(_shift_up(t["a"], a_n, 1), dy_r * t["silu_rz"], lam_n)
            a, r, ig, xc, mult = t["a"], t["r"], t["i"], t["xc"], t["mult"]
            d_i = lam * mult * xc
            d_mult = lam * ig * xc
            dxc = lam * mult * ig
            dla = lam * h_prev * a - d_mult * ((1.0 - t["em"]) * t["inv_mult"])
            g_vec[3:4, :] += _rowsum(dla * r) * (-LRU_C * w.dsp_dlam)
            dpr = (dla * w.neg_c_sp) * r * (1.0 - r)
            dpi = d_i * ig * (1.0 - ig)
            dprb, dpib, xcb = dpr.astype(BF16), dpi.astype(BF16), xc.astype(BF16)
            g_wa[...] += _dot_tn(xcb, dprb)
            g_wx[...] += _dot_tn(xcb, dpib)
            g_vec[1:2, :] += _rowsum(dpr)
            g_vec[2:3, :] += _rowsum(dpi)
            dxc = dxc + _dot_nt(dprb, w.wa) + _dot_nt(dpib, w.wx)
            g_vec[0:1, :] += _rowsum(dxc)
            out(RX, w.lcw[3] * dxc + w.lcw[2] * _shift_up(dxc, dxc_n, 1)
                + w.lcw[1] * _shift_up(dxc, dxc_n, 2) + w.lcw[0] * _shift_up(dxc, dxc_n, 3))
            for j in range(4):
                g_lcw[j:j + 1, :] += _rowsum(dxc * t["rx"][j])
            return dcv[:HALO, :], dxc[:HALO, :], lam[:HALO, :], a[:HALO, :]

        zero = jnp.zeros((HALO, LANES), F32)
        lax.fori_loop(0, n_chunks, chunk, (zero, zero, zero, zero))

        @pl.when(pl.program_id(1) == nb - 1)
        def _():
            g_sw[...] = jnp.where(w.tril, g_sw[...], 0.0)

    slab = lambda dt: pl.BlockSpec((None, s, LANES), lambda cb, b: (b, 0, cb))
    seg = pl.BlockSpec((N_SEG, None, s, LANES), lambda cb, b: (0, b, 0, cb))
    rows = lambda n: pl.BlockSpec((n, LANES), lambda cb, b: (0, cb))
    sq = pl.BlockSpec((None, LANES, LANES), lambda cb, b: (cb, 0, 0))
    n_cb = D // LANES
    return pl.pallas_call(
        body, name="mixer_bwd", grid=(n_cb, nb),
        in_specs=[seg, slab(BF16), slab(F32)] + _weight_specs(0),
        out_specs=[seg, rows(3), sq, sq, rows(4), rows(8), sq, sq],
        out_shape=[
            jax.ShapeDtypeStruct(proj.shape, BF16),
            jax.ShapeDtypeStruct((3, D), F32),
            jax.ShapeDtypeStruct((n_cb, CHUNK, CHUNK), F32),
            jax.ShapeDtypeStruct((n_cb, CHUNK, LANES), F32),
            jax.ShapeDtypeStruct((4, D), F32),
            jax.ShapeDtypeStruct((8, D), F32),
            jax.ShapeDtypeStruct((n_cb, LANES, LANES), F32),
            jax.ShapeDtypeStruct((n_cb, LANES, LANES), F32),
        ],
        compiler_params=_params(("arbitrary", "arbitrary")),
    )(proj, dmerged, hs, *mw)


def _row_tile(s, want):
    return want if s % want == 0 else s


def _norm_mod(x, gain, shift, scale):
    nb, s, _ = x.shape
    tm = _row_tile(s, 512)

    def body(x_ref, g_ref, sh_ref, sc_ref, h_ref):
        xv = x_ref[...]
        r = lax.rsqrt(jnp.mean(xv * xv, axis=1, keepdims=True) + EPS)
        h_ref[...] = ((xv * r) * g_ref[...] * (1.0 + sc_ref[...]) + sh_ref[...]).astype(BF16)

    tile = pl.BlockSpec((None, tm, D), lambda b, m: (b, m, 0))
    vec = pl.BlockSpec((None, 1, D), lambda b, m: (b, 0, 0))
    return pl.pallas_call(
        body, name="norm_mod", grid=(nb, s // tm),
        in_specs=[tile, pl.BlockSpec((1, D), lambda b, m: (0, 0)), vec, vec],
        out_specs=tile, out_shape=jax.ShapeDtypeStruct(x.shape, BF16),
        compiler_params=_params(("arbitrary", "arbitrary")),
    )(x, gain, shift, scale)


def _in_proj(h, wg, dep):
    nb, s, _ = h.shape

    def body(h_ref, w_ref, dep_ref, o_ref):
        o_ref[...] = _dot(h_ref[...], w_ref[...]).astype(BF16)

    return pl.pallas_call(
        body, name="in_proj", grid=(nb, N_DEV * UNITS_PER_DEV),
        in_specs=[pl.BlockSpec((None, s, D), lambda b, u: (b, 0, 0)),
                  pl.BlockSpec((None, D, UNIT), lambda b, u: (u // UNITS_PER_DEV, 0, u % UNITS_PER_DEV)),
                  pl.BlockSpec((8, LANES), lambda b, u: (0, 0))],
        out_specs=pl.BlockSpec((None, None, s, UNIT), lambda b, u: (u // 2, b, 0, u % 2)),
        out_shape=jax.ShapeDtypeStruct((N_SEG, nb, s, D), BF16),
        compiler_params=_params(("arbitrary", "arbitrary")),
    )(h, wg, dep)


def _out_proj(x, merged, wout, gate):
    nb, s, _ = x.shape
    tm = _row_tile(s, 512)

    def body(x_ref, m_ref, w_ref, g_ref, o_ref):
        o_ref[...] = x_ref[...] + g_ref[...] * _dot(m_ref[...], w_ref[...])

    tile = pl.BlockSpec((None, tm, D), lambda b, m: (b, m, 0))
    return pl.pallas_call(
        body, name="out_proj", grid=(nb, s // tm),
        in_specs=[tile, tile, pl.BlockSpec((D, D), lambda b, m: (0, 0)),
                  pl.BlockSpec((None, 1, D), lambda b, m: (b, 0, 0))],
        out_specs=tile, out_shape=jax.ShapeDtypeStruct(x.shape, F32),
        compiler_params=_params(("arbitrary", "arbitrary")),
    )(x, merged, wout, gate)


def _loss_head(x, gain, target):
    nb, s, _ = x.shape
    tm = _row_tile(s, 512)

    def body(x_ref, g_ref, t_ref, loss_ref, dx_ref, dg_ref):
        first = (pl.program_id(0) == 0) & (pl.program_id(1) == 0)
        last = (pl.program_id(0) == nb - 1) & (pl.program_id(1) == s // tm - 1)

        @pl.when(first)
        def _():
            loss_ref[...] = jnp.zeros_like(loss_ref)
            dg_ref[...] = jnp.zeros_like(dg_ref)

        xv = x_ref[...]
        r = lax.rsqrt(jnp.mean(xv * xv, axis=1, keepdims=True) + EPS)
        xn = xv * r
        g = g_ref[...]
        e = xn * g - t_ref[...]
        loss_ref[...] += _rowsum(e * e) * (0.5 / D)
        dy = e * (1.0 / D)
        dg_ref[...] += _rowsum(dy * xn)
        dxn = dy * g
        dx_ref[...] = r * (dxn - xn * jnp.mean(dxn * xn, axis=1, keepdims=True))

        @pl.when(last)
        def _():
            loss_ref[...] = jnp.broadcast_to(jnp.sum(loss_ref[...], axis=1, keepdims=True), (1, D))

    tile = pl.BlockSpec((None, tm, D), lambda b, m: (b, m, 0))
    vec = pl.BlockSpec((1, D), lambda b, m: (0, 0))
    return pl.pallas_call(
        body, name="loss_head", grid=(nb, s // tm),
        in_specs=[tile, vec, tile], out_specs=[vec, tile, vec],
        out_shape=[jax.ShapeDtypeStruct((1, D), F32), jax.ShapeDtypeStruct(x.shape, F32),
                   jax.ShapeDtypeStruct((1, D), F32)],
        compiler_params=_params(("arbitrary", "arbitrary")),
    )(x, gain, target)


def _out_proj_bwd(dxo, merged, wout, gate):
    nb, s, _ = dxo.shape
    tm = _row_tile(s, 512)

    def body(d_ref, m_ref, w_ref, g_ref, dm_ref, gw_ref, dg_ref):
        @pl.when((pl.program_id(0) == 0) & (pl.program_id(1) == 0))
        def _():
            gw_ref[...] = jnp.zeros_like(gw_ref)

        @pl.when(pl.program_id(1) == 0)
        def _():
            dg_ref[...] = jnp.zeros_like(dg_ref)

        d = d_ref[...]
        m = m_ref[...]
        wv = w_ref[...]
        dg_ref[...] += _rowsum(d * _dot(m, wv))
        dout = (d * g_ref[...]).astype(BF16)
        dm_ref[...] = _dot_nt(dout, wv).astype(BF16)
        gw_ref[...] += _dot_tn(m, dout)

    tile = pl.BlockSpec((None, tm, D), lambda b, m: (b, m, 0))
    vec = pl.BlockSpec((None, 1, D), lambda b, m: (b, 0, 0))
    full = pl.BlockSpec((D, D), lambda b, m: (0, 0))
    return pl.pallas_call(
        body, name="out_proj_bwd", grid=(nb, s // tm),
        in_specs=[tile, tile, full, vec], out_specs=[tile, full, vec],
        out_shape=[jax.ShapeDtypeStruct(dxo.shape, BF16), jax.ShapeDtypeStruct((D, D), F32),
                   jax.ShapeDtypeStruct((nb, 1, D), F32)],
        compiler_params=_params(("arbitrary", "arbitrary")),
    )(dxo, merged, wout, gate)


def _in_proj_bwd_h(dproj, wg, dep):
    _, nb, s, _ = dproj.shape

    def body(dp_ref, w0_ref, w1_ref, dep_ref, dh_ref):
        j = pl.program_id(1)
        part = _dot_nt(dp_ref[...], jnp.concatenate([w0_ref[...], w1_ref[...]], axis=1))

        @pl.when(j == 0)
        def _():
            dh_ref[...] = part

        @pl.when(j > 0)
        def _():
            dh_ref[...] += part

    def unit(k):
        return pl.BlockSpec((None, D, UNIT),
                            lambda b, j: ((2 * j + k) // UNITS_PER_DEV, 0, (2 * j + k) % UNITS_PER_DEV))

    return pl.pallas_call(
        body, name="in_proj_bwd_h", grid=(nb, N_SEG),
        in_specs=[pl.BlockSpec((None, None, s, D), lambda b, j: (j, b, 0, 0)), unit(0), unit(1),
                  pl.BlockSpec((8, LANES), lambda b, j: (0, 0))],
        out_specs=pl.BlockSpec((None, s, D), lambda b, j: (b, 0, 0)),
        out_shape=jax.ShapeDtypeStruct((nb, s, D), F32),
        compiler_params=_params(("arbitrary", "arbitrary")),
    )(dproj, wg, wg, dep)


def _norm_mod_bwd(dh, x, dxo, gain, scale):
    nb, s, _ = x.shape
    tm = _row_tile(s, 512)

    def body(dh_ref, x_ref, dxo_ref, g_ref, sc_ref, dx_ref, dsh_ref, dsc_ref, dg_ref):
        b, m = pl.program_id(0), pl.program_id(1)

        @pl.when((b == 0) & (m == 0))
        def _():
            dg_ref[...] = jnp.zeros_like(dg_ref)

        @pl.when(m == 0)
        def _():
            dsh_ref[...] = jnp.zeros_like(dsh_ref)
            dsc_ref[...] = jnp.zeros_like(dsc_ref)

        dh = dh_ref[...]
        xv = x_ref[...]
        r = lax.rsqrt(jnp.mean(xv * xv, axis=1, keepdims=True) + EPS)
        xn = xv * r
        g = g_ref[...]
        one_sc = 1.0 + sc_ref[...]
        dsh_ref[...] += _rowsum(dh)
        dsc_ref[...] += _rowsum(dh * (xn * g))
        dg_ref[...] += _rowsum(dh * one_sc * xn)
        dxn = dh * (g * one_sc)
        dx_ref[...] = dxo_ref[...] + r * (dxn - xn * jnp.mean(dxn * xn, axis=1, keepdims=True))

    tile = pl.BlockSpec((None, tm, D), lambda b, m: (b, m, 0))
    vec = pl.BlockSpec((None, 1, D), lambda b, m: (b, 0, 0))
    one = pl.BlockSpec((1, D), lambda b, m: (0, 0))
    return pl.pallas_call(
        body, name="norm_mod_bwd", grid=(nb, s // tm),
        in_specs=[tile, tile, tile, one, vec],
        out_specs=[tile, vec, vec, one],
        out_shape=[jax.ShapeDtypeStruct(x.shape, F32), jax.ShapeDtypeStruct((nb, 1, D), F32),
                   jax.ShapeDtypeStruct((nb, 1, D), F32), jax.ShapeDtypeStruct((1, D), F32)],
        compiler_params=_params(("arbitrary", "arbitrary")),
    )(dh, x, dxo, gain, scale)


def _in_proj_bwd_w(h, dproj, dep):
    nb, s, _ = h.shape
    tm = _row_tile(s, 2048)
    n_m = s // tm

    def body(h_ref, dp_ref, dep_ref, o_ref, acc_ref):
        b, m = pl.program_id(1), pl.program_id(2)

        @pl.when((b == 0) & (m == 0))
        def _():
            acc_ref[...] = jnp.zeros_like(acc_ref)

        acc_ref[...] += _dot_tn(h_ref[...], dp_ref[...])

        @pl.when((b == nb - 1) & (m == n_m - 1))
        def _():
            o_ref[0] = acc_ref[:, :UNIT].astype(BF16)
            o_ref[1] = acc_ref[:, UNIT:].astype(BF16)

    return pl.pallas_call(
        body, name="in_proj_bwd_w", grid=(N_SEG, nb, n_m),
        in_specs=[pl.BlockSpec((None, tm, D), lambda j, b, m: (b, m, 0)),
                  pl.BlockSpec((None, None, tm, D), lambda j, b, m: (j, b, m, 0)),
                  pl.BlockSpec((8, LANES), lambda j, b, m: (0, 0))],
        out_specs=pl.BlockSpec((2, D, UNIT), lambda j, b, m: (j, 0, 0)),
        out_shape=jax.ShapeDtypeStruct((2 * N_SEG, D, UNIT), BF16),
        scratch_shapes=[pltpu.VMEM((D, D), F32)],
        compiler_params=_params(("arbitrary", "arbitrary", "arbitrary")),
    )(h, dproj, dep)


def _mod_proj(c_all, w_mod, b_mod_mine):
    nl, _, ncol = w_mod.shape
    nbg = c_all.shape[0]

    def body(c_ref, w_ref, b_ref, o_ref):
        cv = c_ref[...]
        o_ref[...] = jnp.dot(cv * jax.nn.sigmoid(cv), w_ref[...], preferred_element_type=F32,
                             precision=lax.Precision.HIGHEST) + b_ref[...]

    return pl.pallas_call(
        body, name="mod_proj", grid=(nl,),
        in_specs=[pl.BlockSpec((nbg, D), lambda l: (0, 0)), pl.BlockSpec((None, D, ncol), lambda l: (l, 0, 0)),
                  pl.BlockSpec((None, 1, ncol), lambda l: (l, 0, 0))],
        out_specs=pl.BlockSpec((None, nbg, ncol), lambda l: (l, 0, 0)),
        out_shape=jax.ShapeDtypeStruct((nl, nbg, ncol), F32),
        compiler_params=_params(("arbitrary",)),
    )(c_all, w_mod, b_mod_mine)


def _mod_grad(c_all, dmod_all, dmod_mine):
    nl, nbg, ncol = dmod_mine.shape

    def body(c_ref, da_ref, dm_ref, gw_ref, gb_ref):
        cv = c_ref[...]
        gw_ref[...] = lax.dot_general(cv * jax.nn.sigmoid(cv), dm_ref[...], (((0,), (0,)), ((), ())),
                                      preferred_element_type=F32, precision=lax.Precision.HIGHEST)
        gb_ref[...] = _rowsum(da_ref[...])

    return pl.pallas_call(
        body, name="mod_grad", grid=(nl,),
        in_specs=[pl.BlockSpec((nbg, D), lambda l: (0, 0)), pl.BlockSpec((None, nbg, 3 * D), lambda l: (l, 0, 0)),
                  pl.BlockSpec((None, nbg, ncol), lambda l: (l, 0, 0))],
        out_specs=[pl.BlockSpec((None, D, ncol), lambda l: (l, 0, 0)),
                   pl.BlockSpec((None, 1, 3 * D), lambda l: (l, 0, 0))],
        out_shape=[jax.ShapeDtypeStruct((nl, D, ncol), F32), jax.ShapeDtypeStruct((nl, 1, 3 * D), F32)],
        compiler_params=_params(("arbitrary",)),
    )(c_all, dmod_all, dmod_mine)


def _adamw(parts, w, m, v, name, layer=None, prev=None):
    n_parts, n_u, n_r, cu = parts.shape
    assert w.shape[-2:] == (n_r, n_u * cu), (parts.shape, w.shape)
    tr = n_r
    for cand in (512, 256, 128):
        if n_r > cand and n_r % cand == 0:
            tr = cand
            break
    n_prev = 0 if prev is None else 4

    def body(p_ref, w_ref, m_ref, v_ref, *rest):
        g_ref, d_ref, nm_ref, nv_ref = rest[n_prev:]
        g = p_ref[0].astype(F32)
        for k in range(1, n_parts):
            g = g + p_ref[k].astype(F32)
        m2 = ADAM_B1 * m_ref[...] + (1.0 - ADAM_B1) * g
        v2 = ADAM_B2 * v_ref[...] + (1.0 - ADAM_B2) * (g * g)
        m_hat = m2 / (1.0 - ADAM_B1 ** ADAM_STEP)
        v_hat = v2 / (1.0 - ADAM_B2 ** ADAM_STEP)
        g_ref[...] = g
        d_ref[...] = -ADAM_LR * (m_hat / (jnp.sqrt(v_hat) + ADAM_EPS) + ADAM_WD * w_ref[...])
        nm_ref[...] = m2
        nv_ref[...] = v2

    if layer is None:
        tile = pl.BlockSpec((tr, cu), lambda u, i: (i, u))
    else:
        tile = pl.BlockSpec((None, tr, cu), lambda u, i: (layer, i, u))
    shp = jax.ShapeDtypeStruct(w.shape, F32)
    return pl.pallas_call(
        body, name=name, grid=(n_u, n_r // tr),
        in_specs=[pl.BlockSpec((n_parts, None, tr, cu), lambda u, i: (0, u, i, 0)), tile, tile, tile]
        + [pl.BlockSpec(memory_space=pl.ANY)] * n_prev,
        out_specs=[tile, tile, tile, tile], out_shape=[shp, shp, shp, shp],
        input_output_aliases={4 + k: k for k in range(n_prev)},
        compiler_params=_params(("arbitrary", "arbitrary")),
    )(parts, w, m, v, *(prev or ()))


def _gathered_cols(g, inner):
    k = len(inner)
    perm = tuple(range(1, k + 1)) + (0, k + 1)
    t = jnp.transpose(g, perm)
    return t.reshape(tuple(inner) + (g.shape[0] * g.shape[-1],))


def _pair_blocks(wh):
    z = jnp.zeros((8, 64, 64), wh.dtype)
    w2 = wh.reshape(8, 2, 64, 64)
    top = jnp.concatenate([w2[:, 0], z], axis=2)
    bot = jnp.concatenate([z, w2[:, 1]], axis=2)
    return jnp.concatenate([top, bot], axis=1).astype(BF16)


def _unpair_blocks(g):
    return jnp.stack([g[:, :64, :64], g[:, 64:, 64:]], axis=1).reshape(16, 64, 64)


FLAT_ROWS = 512


def _pack_flat(arrays):
    flat = jnp.concatenate([a.reshape(-1) for a in arrays])
    n = flat.shape[0]
    per = FLAT_ROWS * LANES
    pad = (-n) % per
    return jnp.pad(flat, (0, pad)).reshape(-1, LANES)


def kernel(x, c, norm_gain, w_mod, b_mod, w_in, w_out, conv_a_w, sgu_w, sgu_b, lru_conv_w, lru_conv_b, lru_wa, lru_ba, lru_wx, lru_bx, lru_lambda, final_gain, loss_target, m_norm_gain, m_w_mod, m_b_mod, m_w_in, m_w_out, m_conv_a_w, m_sgu_w, m_sgu_b, m_lru_conv_w, m_lru_conv_b, m_lru_wa, m_lru_ba, m_lru_wx, m_lru_bx, m_lru_lambda, m_final_gain, v_norm_gain, v_w_mod, v_b_mod, v_w_in, v_w_out, v_conv_a_w, v_sgu_w, v_sgu_b, v_lru_conv_w, v_lru_conv_b, v_lru_wa, v_lru_ba, v_lru_wx, v_lru_bx, v_lru_lambda, v_final_gain):
    nl = w_in.shape[0]
    nb, s, _ = x.shape
    me = _my_index()
    mod_cols = w_mod.shape[2]


    small = jnp.concatenate([c.reshape(-1, LANES), conv_a_w.reshape(-1, LANES), lru_conv_w.reshape(-1, LANES)])
    n_c, n_ca = nb * D // LANES, nl * 3
    n_small = small.shape[0]
    small = jnp.pad(small, ((0, (-n_small) % 8), (0, 0)))
    small_all, _ = _all_gather(small, "gather_small")
    c_all = small_all[:, :n_c].reshape(N_DEV * nb, D)
    conv_a_full = _gathered_cols(small_all[:, n_c:n_c + n_ca].reshape(N_DEV, nl, 3, LANES), (nl, 3))
    lru_conv_full = _gathered_cols(small_all[:, n_c + n_ca:n_small].reshape(N_DEV, nl, 4, LANES), (nl, 4))

    b_mod_mine = lax.dynamic_slice_in_dim(b_mod, me * mod_cols, mod_cols, axis=1)[:, None, :]
    mod_mine = _mod_proj(c_all, w_mod, b_mod_mine)
    mod_all, mod_token = _all_gather(mod_mine.reshape(nl * N_DEV * nb, mod_cols), "gather_mod")
    mod_full = _gathered_cols(mod_all.reshape(N_DEV, nl, N_DEV * nb, mod_cols), (nl, N_DEV * nb))
    mod_loc = lax.dynamic_slice_in_dim(mod_full, me * nb, nb, axis=1)
    shift, scale, gate = [mod_loc[:, :, j * D:(j + 1) * D][:, :, None, :] for j in range(3)]

    def start_w_in(l, token):
        return _split_start(_after(w_in[l], token).astype(BF16), _peers_same_core, False, "gather_w_in_start")

    def gathered_w_in(started, after):
        block, land = _split_wait(started, after, _peers_same_core, False, False, "gather_w_in_wait")
        return _gather_finish(block, land, "gather_w_in_finish")

    sgu_b_lanes = jnp.broadcast_to(sgu_b[..., None], sgu_b.shape + (LANES,))
    mws = []
    for l in range(nl):
        mws.append((conv_a_full[l], sgu_w[l], sgu_b_lanes[l], lru_conv_full[l], lru_conv_b[l][None, :],
                    _pair_blocks(lru_wa[l]), _pair_blocks(lru_wx[l]), lru_ba[l].reshape(1, D),
                    lru_bx[l].reshape(1, D), lru_lambda[l][None, :]))

    rep_names = ["sgu_w", "sgu_b", "lru_conv_b", "lru_wa", "lru_ba", "lru_wx", "lru_bx", "lru_lambda"]
    rep_w = dict(sgu_w=sgu_w, sgu_b=sgu_b, lru_conv_b=lru_conv_b, lru_wa=lru_wa, lru_ba=lru_ba,
                 lru_wx=lru_wx, lru_bx=lru_bx, lru_lambda=lru_lambda)
    rep_m = dict(sgu_w=m_sgu_w, sgu_b=m_sgu_b, lru_conv_b=m_lru_conv_b, lru_wa=m_lru_wa,
                 lru_ba=m_lru_ba, lru_wx=m_lru_wx, lru_bx=m_lru_bx, lru_lambda=m_lru_lambda)
    rep_v = dict(sgu_w=v_sgu_w, sgu_b=v_sgu_b, lru_conv_b=v_lru_conv_b, lru_wa=v_lru_wa,
                 lru_ba=v_lru_ba, lru_wx=v_lru_wx, lru_bx=v_lru_bx, lru_lambda=v_lru_lambda)

    def rep_pack_all(src):
        flat = jnp.concatenate([src[n].reshape(nl, -1) for n in rep_names], axis=1)
        pad = (-flat.shape[1]) % (FLAT_ROWS * LANES)
        return jnp.pad(flat, ((0, 0), (0, pad))).reshape(nl, -1, LANES)

    rep_w_all, rep_m_all, rep_v_all = rep_pack_all(rep_w), rep_pack_all(rep_m), rep_pack_all(rep_v)
    early = [rep_w_all, rep_m_all, rep_v_all] + [a for mw in mws for a in mw]

    xs, hs_bf, projs, mergeds, states, wg = [], [], [], [], [], []
    xl = x
    wg_started = start_w_in(0, mod_token)
    wo_started = _split_start(_after(w_out, wg_started[4]).astype(BF16).reshape(nl * (D // N_DEV), D),
                              _peers_all, False, "gather_w_out_start")
    wo = None
    for l in range(nl):
        h = _norm_mod(xl, _after(norm_gain[l][None, :], wo_started[4]), shift[l], scale[l])
        wg_l, token = gathered_w_in(wg_started, [h] + early if l == 0 else h)
        wg.append(wg_l)
        if l + 1 < nl:
            wg_started = start_w_in(l + 1, token)
            token = wg_started[4]
        proj = _in_proj(h, wg_l, token)
        merged, st = _mixer_fwd(proj, mws[l])
        xs.append(xl), hs_bf.append(h), projs.append(proj), mergeds.append(merged), states.append(st)
        if wo is None:
            _, wo_all = _split_wait(wo_started, merged, _peers_all, False, True, "gather_w_out_wait")
            wo = jnp.transpose(wo_all.reshape(N_DEV, nl, D // N_DEV, D), (1, 0, 2, 3)).reshape(nl, D, D)
        xl = _out_proj(xl, merged, wo[l], gate[l])

    loss_row, dx, g_final = _loss_head(xl, final_gain[None, :], loss_target)
    loss = lax.psum(loss_row[0, 0], ("x", "y", "c"))

    res_big, g_conv = {}, [None] * nl
    dmods = [None] * nl

    def finish_exchange(pending, after):
        l, h_in, h_out, h_rep = pending
        _, recv = _split_wait(h_in, after, _peers_all, True, True, "scatter_w_in_wait")
        res_big["w_in"] = _adamw(recv, w_in, m_w_in, v_w_in, "adamw_w_in", l, res_big.get("w_in"))
        _, recv = _split_wait(h_out, after, _peers_all, True, True, "scatter_w_out_wait")
        res_big["w_out"] = _adamw(recv, w_out, m_w_out, v_w_out, "adamw_w_out", l, res_big.get("w_out"))
        _, recv = _split_wait(h_rep, after, _peers_all, False, True, "gather_rep_wait")
        res_big["rep"] = _adamw(recv[:, None], rep_w_all, rep_m_all, rep_v_all, "adamw_rep", l, res_big.get("rep"))

    pending = None
    g_gains = [None] * nl
    for l in reversed(range(nl)):
        dmerged, gw_out, dgate = _out_proj_bwd(dx, mergeds[l], wo[l], gate[l])
        dproj, g_caw, g_sw, g_sb, g_lcw, g_vec, g_wa, g_wx = _mixer_bwd(projs[l], dmerged, states[l], mws[l])
        g_conv[l] = (g_caw, g_lcw)
        rep_g = dict(
            sgu_w=g_sw, sgu_b=g_sb[:, :, 0], lru_conv_b=g_vec[0],
            lru_wa=_unpair_blocks(g_wa), lru_ba=g_vec[1].reshape(16, 64), lru_wx=_unpair_blocks(g_wx),
            lru_bx=g_vec[2].reshape(16, 64), lru_lambda=g_vec[3])
        rep_block = _pack_flat([rep_g[n] for n in rep_names])
        h_out = _split_start(gw_out.reshape(N_DEV, 1, D // N_DEV, D), _peers_all, True, "scatter_w_out_start")
        h_rep = _split_start(rep_block, _peers_all, False, "gather_rep_start")
        gw_in = _in_proj_bwd_w(hs_bf[l], dproj, h_out[4] + h_rep[4])
        h_in = _split_start(gw_in.reshape(N_DEV, UNITS_PER_DEV, D, UNIT), _peers_all, True, "scatter_w_in_start")
        started = (l, h_in, h_out, h_rep)
        dh = _in_proj_bwd_h(dproj, wg[l], h_in[4])
        dx, dshift, dscale, g_gain = _norm_mod_bwd(dh, xs[l], dx, norm_gain[l][None, :], scale[l])
        g_gains[l] = g_gain
        dmods[l] = jnp.concatenate([dshift, dscale, dgate], axis=2)[:, 0, :]
        if pending is not None:
            finish_exchange(pending, dx)
        pending = started

    conv_parts = jnp.concatenate(
        [jnp.stack([g_conv[l][0] for l in range(nl)]).reshape(nl * 3, N_DEV, LANES),
         jnp.stack([g_conv[l][1] for l in range(nl)]).reshape(nl * 4, N_DEV, LANES)], axis=0)
    conv_parts = jnp.transpose(conv_parts, (1, 0, 2))[:, None]
    conv_recv = _all_to_all(conv_parts, "scatter_conv")

    dmod_loc = jnp.stack(dmods).reshape(nl * nb, 3 * D)
    gain_rows = jnp.pad(jnp.concatenate(g_gains + [g_final], axis=0), ((0, (-(nl + 1)) % 8), (0, 2 * D)))
    tail_g, _ = _all_gather(jnp.concatenate([dmod_loc, gain_rows], axis=0), "gather_dmod",
                            dep=res_big["w_in"][3])
    dmod_g = tail_g[:, :nl * nb]
    gain_parts = tail_g[:, nl * nb:nl * nb + nl + 1, :D][:, None]
    gain_cat = lambda a, b: jnp.concatenate([a, b[None, :]], axis=0)
    res_gain = _adamw(gain_parts, gain_cat(norm_gain, final_gain), gain_cat(m_norm_gain, m_final_gain),
                      gain_cat(v_norm_gain, v_final_gain), "adamw_gain")
    dmod_all = jnp.transpose(dmod_g.reshape(N_DEV, nl, nb, 3 * D), (1, 0, 2, 3)).reshape(nl, N_DEV * nb, 3 * D)
    dmod_mine = lax.dynamic_slice_in_dim(dmod_all, me * mod_cols, mod_cols, axis=2)
    gw_mod, gb_mod = _mod_grad(c_all, dmod_all, dmod_mine)
    res_w_mod = _adamw(gw_mod.reshape(1, 1, nl * D, mod_cols), w_mod.reshape(nl * D, mod_cols),
                       m_w_mod.reshape(nl * D, mod_cols), v_w_mod.reshape(nl * D, mod_cols), "adamw_w_mod")
    res_w_mod = [a.reshape(nl, D, mod_cols) for a in res_w_mod]
    res_b_mod = _adamw(gb_mod.reshape(1, 1, nl, 3 * D), b_mod, m_b_mod, v_b_mod, "adamw_b_mod")
    finish_exchange(pending, res_b_mod[1])

    cat = lambda a, b: jnp.concatenate([a.reshape(nl * 3, LANES), b.reshape(nl * 4, LANES)], axis=0)
    res_conv = _adamw(conv_recv, cat(conv_a_w, lru_conv_w), cat(m_conv_a_w, m_lru_conv_w),
                      cat(v_conv_a_w, v_lru_conv_w), "adamw_conv")
    res_conv_a = [a[:nl * 3].reshape(nl, 3, LANES) for a in res_conv]
    res_lru_conv = [a[nl * 3:].reshape(nl, 4, LANES) for a in res_conv]

    res_rep = []
    for k in range(4):
        flat, off, d = res_big["rep"][k].reshape(nl, -1), 0, {}
        for n in rep_names:
            size = rep_w[n][0].size
            d[n] = flat[:, off:off + size].reshape(rep_w[n].shape)
            off += size
        res_rep.append(d)

    def leaf(k, name):
        if name == "norm_gain":
            return res_gain[k][:nl]
        if name == "final_gain":
            return res_gain[k][nl]
        if name == "w_mod":
            return res_w_mod[k]
        if name == "b_mod":
            return res_b_mod[k]
        if name in ("w_in", "w_out"):
            return res_big[name][k]
        if name == "conv_a_w":
            return res_conv_a[k]
        if name == "lru_conv_w":
            return res_lru_conv[k]
        return res_rep[k][name]

    order = ["norm_gain", "w_mod", "b_mod", "w_in", "w_out", "conv_a_w", "sgu_w", "sgu_b", "lru_conv_w",
             "lru_conv_b", "lru_wa", "lru_ba", "lru_wx", "lru_bx", "lru_lambda", "final_gain"]
    outs = [loss, dx]
    for k in range(4):
        outs += [leaf(k, n) for n in order]
    return tuple(outs)
```

```python
import functools

import jax
import jax.numpy as jnp
from jax import lax
from jax.experimental import pallas as pl
from jax.experimental.pallas import tpu as pltpu

F32 = jnp.float32
BF16 = jnp.bfloat16

D = 1024
N_DEV = 8
N_SEG = 12
LANES = 128
SUBLANES = 8
CHUNK = 128
HALO = 16
UNIT = 512
UNITS_PER_DEV = 3
EPS = 1e-6
LRU_C = 8.0
ADAM_LR, ADAM_B1, ADAM_B2, ADAM_EPS, ADAM_WD, ADAM_STEP = 0.001, 0.9, 0.999, 1e-08, 0.01, 10
VMEM_LIMIT = 56 * 1024 * 1024

AX, AB, AC, AZ, SU, SV, SZ, RX, RZ, GA, GS, GR = range(N_SEG)
MESH = pl.DeviceIdType.MESH


def _params(sem=None):
    return pltpu.CompilerParams(dimension_semantics=sem, vmem_limit_bytes=VMEM_LIMIT)


def _my_index():
    return 4 * lax.axis_index("x") + 2 * lax.axis_index("y") + lax.axis_index("c")


def _all_gather(block, name, dep=None):
    def body(x_ref, *refs):
        out_ref, token, send_sems, recv_sems, local_sem = refs[-5:]
        x, y, c = lax.axis_index("x"), lax.axis_index("y"), lax.axis_index("c")
        me, sibling = (x, y, c), (x, y, 1 - c)
        chips = [(1 - x, y), (x, 1 - y), (1 - x, 1 - y)]
        token[...] = jnp.zeros_like(token)

        def rows(px, py, pc):
            return out_ref.at[4 * px + 2 * py + pc]

        def copy(k, blk, to, src=None):
            return pltpu.make_async_remote_copy(
                src_ref=rows(*blk) if src is None else src, dst_ref=rows(*blk),
                send_sem=send_sems.at[k], recv_sem=recv_sems.at[k], device_id=to, device_id_type=MESH)

        mine = pltpu.make_async_copy(x_ref, rows(*me), local_sem)
        mine.start()
        first = [copy(0, me, sibling, src=x_ref)]
        first += [copy(1 + j, me, (*chip, c), src=x_ref) for j, chip in enumerate(chips)]
        for cp in first:
            cp.start()
        passed = [copy(4 + j, (*chip, c), sibling) for j, chip in enumerate(chips)]
        for j, chip in enumerate(chips):
            copy(1 + j, (*chip, c), me).wait_recv()
            passed[j].start()
        copy(0, sibling, me).wait_recv()
        for j, chip in enumerate(chips):
            copy(4 + j, (*chip, 1 - c), me).wait_recv()
        for cp in first + passed:
            cp.wait_send()
        mine.wait()

    return pl.pallas_call(
        body, name=name,
        out_shape=[jax.ShapeDtypeStruct((N_DEV,) + block.shape, block.dtype), jax.ShapeDtypeStruct((8, LANES), F32)],
        in_specs=[pl.BlockSpec(memory_space=pltpu.VMEM)]
        + [pl.BlockSpec(memory_space=pl.ANY)] * (dep is not None),
        out_specs=[pl.BlockSpec(memory_space=pl.ANY), pl.BlockSpec(memory_space=pltpu.VMEM)],
        scratch_shapes=[pltpu.SemaphoreType.DMA((7,)), pltpu.SemaphoreType.DMA((7,)), pltpu.SemaphoreType.DMA],
    )(block, *([dep] if dep is not None else []))


def _all_to_all(blocks, name):
    def body(x_ref, out_ref, send_sems, recv_sems, local_sem):
        x, y, c = lax.axis_index("x"), lax.axis_index("y"), lax.axis_index("c")
        my = 4 * x + 2 * y + c
        mine = pltpu.make_async_copy(x_ref.at[my], out_ref.at[my], local_sem)
        mine.start()
        peers = []
        for r in range(1, N_DEV):
            px = 1 - x if r & 4 else x
            py = 1 - y if r & 2 else y
            pc = 1 - c if r & 1 else c
            peers.append((r - 1, 4 * px + 2 * py + pc, (px, py, pc)))

        def copy(k, src_slot, dst_slot, to):
            return pltpu.make_async_remote_copy(
                src_ref=x_ref.at[src_slot], dst_ref=out_ref.at[dst_slot],
                send_sem=send_sems.at[k], recv_sem=recv_sems.at[k], device_id=to, device_id_type=MESH)

        sends = [copy(k, pid, my, to) for k, pid, to in peers]
        for cp in sends:
            cp.start()
        for k, pid, to in peers:
            copy(k, pid, pid, to).wait_recv()
        for cp in sends:
            cp.wait_send()
        mine.wait()

    return pl.pallas_call(
        body, name=name,
        out_shape=jax.ShapeDtypeStruct(blocks.shape, blocks.dtype),
        in_specs=[pl.BlockSpec(memory_space=pltpu.VMEM)],
        out_specs=pl.BlockSpec(memory_space=pl.ANY),
        scratch_shapes=[pltpu.SemaphoreType.DMA((7,)), pltpu.SemaphoreType.DMA((7,)), pltpu.SemaphoreType.DMA],
    )(blocks)


_HBM = pl.BlockSpec(memory_space=pltpu.HBM)
_SEM = pl.BlockSpec(memory_space=pltpu.SEMAPHORE)
_EFFECT = pltpu.SideEffectType.DATAFLOW_SIDE_EFFECTING


def _peers_all(x, y, c):
    out = []
    for r in range(1, N_DEV):
        px = 1 - x if r & 4 else x
        py = 1 - y if r & 2 else y
        pc = 1 - c if r & 1 else c
        out.append((r - 1, 4 * px + 2 * py + pc, (px, py, pc)))
    return out


def _peers_same_core(x, y, c):
    return [(k, 4 * px + 2 * py + c, (px, py, c))
            for k, (px, py) in enumerate([(1 - x, y), (x, 1 - y), (1 - x, 1 - y)])]


def _split_start(src, peers_fn, scatter, name):
    blk = src.shape[1:] if scatter else src.shape
    land_shape = (N_DEV,) + tuple(blk)
    n = len(peers_fn(0, 0, 0))

    def body(x_ref, land_ref, send_sems, recv_sems, x_thru, land_thru, token):
        x, y, c = lax.axis_index("x"), lax.axis_index("y"), lax.axis_index("c")
        my = 4 * x + 2 * y + c
        for k, pid, to in peers_fn(x, y, c):
            pltpu.make_async_remote_copy(
                src_ref=x_ref.at[pid] if scatter else x_ref, dst_ref=land_ref.at[my],
                send_sem=send_sems.at[k], recv_sem=recv_sems.at[k], device_id=to, device_id_type=MESH).start()
        token[...] = jnp.zeros_like(token)

    return pl.pallas_call(
        body, name=name,
        out_shape=(pltpu.SemaphoreType.DMA((n,)), pltpu.SemaphoreType.DMA((n,)),
                   pltpu.HBM(src.shape, src.dtype), pltpu.HBM(land_shape, src.dtype),
                   jax.ShapeDtypeStruct((8, LANES), F32)),
        in_specs=(_HBM, _HBM),
        out_specs=(_SEM, _SEM, _HBM, _HBM, pl.BlockSpec(memory_space=pltpu.VMEM)),
        input_output_aliases={0: 2, 1: 3},
        compiler_params=pltpu.CompilerParams(has_side_effects=_EFFECT),
    )(pltpu.with_memory_space_constraint(src, pltpu.HBM),
      pltpu.with_memory_space_constraint(lax.empty(land_shape, src.dtype), pltpu.HBM))


def _split_wait(handles, after, peers_fn, scatter, own, name):
    send_sems, recv_sems, src_thru, land_thru, _ = handles
    blk = land_thru.shape[1:]
    after = list(after) if isinstance(after, (list, tuple)) else [after]

    def body(x_ref, land_ref, send_sems, recv_sems, *rest):
        stage = rest[len(after) + 2:]
        x, y, c = lax.axis_index("x"), lax.axis_index("y"), lax.axis_index("c")
        if own:
            my = 4 * x + 2 * y + c
            mine = _staged_copy(x_ref.at[my] if scatter else x_ref, land_ref.at[my], *stage)
        for k, pid, to in peers_fn(x, y, c):
            cp = pltpu.make_async_remote_copy(
                src_ref=x_ref.at[pid] if scatter else x_ref, dst_ref=land_ref.at[pid],
                send_sem=send_sems.at[k], recv_sem=recv_sems.at[k], device_id=to, device_id_type=MESH)
            cp.wait_send()
            cp.wait_recv()
        if own:
            mine.wait()

    return pl.pallas_call(
        body, name=name,
        out_shape=(pltpu.HBM(src_thru.shape, src_thru.dtype), pltpu.HBM(land_thru.shape, land_thru.dtype)),
        in_specs=(_HBM, _HBM, _SEM, _SEM) + (pl.BlockSpec(memory_space=pl.ANY),) * len(after),
        out_specs=(_HBM, _HBM),
        input_output_aliases={0: 0, 1: 1},
        scratch_shapes=[pltpu.VMEM(blk, land_thru.dtype), pltpu.SemaphoreType.DMA((2,))] if own else [],
        compiler_params=pltpu.CompilerParams(has_side_effects=_EFFECT, vmem_limit_bytes=VMEM_LIMIT),
    )(src_thru, land_thru, send_sems, recv_sems, *after)


def _staged_copy(src_ref, dst_ref, buf, sems):
    leg = pltpu.make_async_copy(src_ref, buf, sems.at[0])
    leg.start()
    leg.wait()
    leg = pltpu.make_async_copy(buf, dst_ref, sems.at[1])
    leg.start()
    return leg


def _gather_finish(block, land, name):
    def body(x_ref, land_ref, out_ref, token, send_sems, recv_sems, buf, local_sems):
        x, y, c = lax.axis_index("x"), lax.axis_index("y"), lax.axis_index("c")
        my, sib_id, sibling = 4 * x + 2 * y + c, 4 * x + 2 * y + 1 - c, (x, y, 1 - c)
        token[...] = jnp.zeros_like(token)

        def copy(k, slot, src=None):
            return pltpu.make_async_remote_copy(
                src_ref=land_ref.at[slot] if src is None else src, dst_ref=out_ref.at[slot],
                send_sem=send_sems.at[k], recv_sem=recv_sems.at[k], device_id=sibling, device_id_type=MESH)

        chips = _peers_same_core(x, y, c)
        sends = [copy(0, my, src=x_ref)] + [copy(1 + k, pid) for k, pid, _ in chips]
        for cp in sends:
            cp.start()
        mine = _staged_copy(x_ref, out_ref.at[my], buf, local_sems)
        copy(0, sib_id).wait_recv()
        for k, pid, _ in chips:
            copy(1 + k, pid + 1 - 2 * c).wait_recv()
        for cp in sends:
            cp.wait_send()
        mine.wait()

    return pl.pallas_call(
        body, name=name,
        out_shape=[jax.ShapeDtypeStruct(land.shape, land.dtype), jax.ShapeDtypeStruct((8, LANES), F32)],
        in_specs=[pl.BlockSpec(memory_space=pl.ANY), pl.BlockSpec(memory_space=pl.ANY)],
        out_specs=[pl.BlockSpec(memory_space=pl.ANY), pl.BlockSpec(memory_space=pltpu.VMEM)],
        input_output_aliases={1: 0},
        scratch_shapes=[pltpu.SemaphoreType.DMA((4,)), pltpu.SemaphoreType.DMA((4,)),
                        pltpu.VMEM(block.shape, block.dtype), pltpu.SemaphoreType.DMA((2,))],
        compiler_params=pltpu.CompilerParams(vmem_limit_bytes=VMEM_LIMIT),
    )(block, land)


def _forward_start(block, land, name):
    def body(x_ref, land_ref, send_sems, recv_sems, x_thru, land_thru, token):
        x, y, c = lax.axis_index("x"), lax.axis_index("y"), lax.axis_index("c")
        my, sibling = 4 * x + 2 * y + c, (x, y, 1 - c)
        slots = [(0, my, x_ref)] + [(1 + k, pid, land_ref.at[pid]) for k, pid, _ in _peers_same_core(x, y, c)]
        for k, slot, src in slots:
            pltpu.make_async_remote_copy(
                src_ref=src, dst_ref=land_ref.at[slot], send_sem=send_sems.at[k], recv_sem=recv_sems.at[k],
                device_id=sibling, device_id_type=MESH).start()
        token[...] = jnp.zeros_like(token)

    return pl.pallas_call(
        body, name=name,
        out_shape=(pltpu.SemaphoreType.DMA((4,)), pltpu.SemaphoreType.DMA((4,)),
                   pltpu.HBM(block.shape, block.dtype), pltpu.HBM(land.shape, land.dtype),
                   jax.ShapeDtypeStruct((8, LANES), F32)),
        in_specs=(_HBM, _HBM),
        out_specs=(_SEM, _SEM, _HBM, _HBM, pl.BlockSpec(memory_space=pltpu.VMEM)),
        input_output_aliases={0: 2, 1: 3},
        compiler_params=pltpu.CompilerParams(has_side_effects=_EFFECT),
    )(block, land)


def _forward_wait(handles, after, name):
    send_sems, recv_sems, block_thru, land_thru, _ = handles

    def body(x_ref, land_ref, send_sems, recv_sems, after_ref, x_dead, got_ref, buf, local_sems):
        x, y, c = lax.axis_index("x"), lax.axis_index("y"), lax.axis_index("c")
        my, sib_id, sibling = 4 * x + 2 * y + c, 4 * x + 2 * y + 1 - c, (x, y, 1 - c)
        mine = _staged_copy(x_ref, land_ref.at[my], buf, local_sems)
        slots = [(0, my, sib_id)] + [(1 + k, pid, pid + 1 - 2 * c) for k, pid, _ in _peers_same_core(x, y, c)]
        for k, sent, got in slots:
            cp = pltpu.make_async_remote_copy(
                src_ref=land_ref.at[sent], dst_ref=land_ref.at[got], send_sem=send_sems.at[k],
                recv_sem=recv_sems.at[k], device_id=sibling, device_id_type=MESH)
            cp.wait_send()
            cp.wait_recv()
        mine.wait()

    return pl.pallas_call(
        body, name=name,
        out_shape=(pltpu.HBM(block_thru.shape, block_thru.dtype), pltpu.HBM(land_thru.shape, land_thru.dtype)),
        in_specs=(_HBM, _HBM, _SEM, _SEM, pl.BlockSpec(memory_space=pl.ANY)),
        out_specs=(_HBM, _HBM),
        input_output_aliases={0: 0, 1: 1},
        scratch_shapes=[pltpu.VMEM(block_thru.shape, block_thru.dtype), pltpu.SemaphoreType.DMA((2,))],
        compiler_params=pltpu.CompilerParams(has_side_effects=_EFFECT, vmem_limit_bytes=VMEM_LIMIT),
    )(block_thru, land_thru, send_sems, recv_sems, after)[1]


def _after(v, token):
    return v + token[0, 0].astype(v.dtype)


def _dsilu(x, s):
    return s * (1.0 + x * (1.0 - s))


def _log1p(x):
    u = 1.0 + x
    d = u - 1.0
    return jnp.where(d == 0.0, x, jnp.log(u) * (x / jnp.where(d == 0.0, 1.0, d)))


def _softplus_neg(lam):
    return jnp.maximum(-lam, 0.0) + _log1p(jnp.exp(-jnp.abs(lam)))


def _neg_expm1(y, exp_y):
    poly = -y * (1.0 + y * (0.5 + y * (1.0 / 6.0 + y * (1.0 / 24.0))))
    return jnp.where(y > -0.05, poly, 1.0 - exp_y)


def _sigmoid(x):
    return 0.5 * jnp.tanh(0.5 * x) + 0.5


def _shift_dn(cur, prev, k):
    ext = jnp.concatenate([prev, cur], axis=0)
    return pltpu.roll(ext, k, 0)[HALO:, :]


def _shift_up(cur, nxt, k):
    n = cur.shape[0]
    ext = jnp.concatenate([cur, nxt], axis=0)
    return pltpu.roll(ext, n + HALO - k, 0)[:n, :]


def _scan_fwd(a, b, h_prev):
    groups = a.shape[0] // SUBLANES
    a3 = a.reshape(groups, SUBLANES, LANES)
    b3 = b.reshape(groups, SUBLANES, LANES)
    row = lax.broadcasted_iota(jnp.int32, a3.shape, 1)
    k = 1
    while k < SUBLANES:
        a_sh = jnp.where(row >= k, pltpu.roll(a3, k, 1), 1.0)
        b_sh = jnp.where(row >= k, pltpu.roll(b3, k, 1), 0.0)
        b3 = a3 * b_sh + b3
        a3 = a3 * a_sh
        k *= 2
    carry = h_prev[HALO - 1:HALO, :]
    out = []
    for i in range(groups):
        hg = b3[i] + a3[i] * carry
        out.append(hg)
        carry = hg[SUBLANES - 1:SUBLANES, :]
    return jnp.concatenate(out, axis=0)


def _scan_rev(a_next, g, lam_next):
    groups = g.shape[0] // SUBLANES
    a3 = a_next.reshape(groups, SUBLANES, LANES)
    g3 = g.reshape(groups, SUBLANES, LANES)
    row = lax.broadcasted_iota(jnp.int32, a3.shape, 1)
    k = 1
    while k < SUBLANES:
        ok = row < SUBLANES - k
        a_sh = jnp.where(ok, pltpu.roll(a3, SUBLANES - k, 1), 1.0)
        g_sh = jnp.where(ok, pltpu.roll(g3, SUBLANES - k, 1), 0.0)
        g3 = g3 + a3 * g_sh
        a3 = a3 * a_sh
        k *= 2
    carry = lam_next[0:1, :]
    out = [None] * groups
    for i in reversed(range(groups)):
        lg = g3[i] + a3[i] * carry
        out[i] = lg
        carry = lg[0:1, :]
    return jnp.concatenate(out, axis=0)


def _rowsum(v):
    return jnp.sum(v, axis=0, keepdims=True)


def _dot(a, b):
    return jnp.dot(a, b, preferred_element_type=F32)


def _dot_nt(a, b):
    return lax.dot_general(a, b, (((1,), (1,)), ((), ())), preferred_element_type=F32)


def _dot_tn(a, b):
    return lax.dot_general(a, b, (((0,), (0,)), ((), ())), preferred_element_type=F32)


class _MixerWeights:
    def __init__(self, caw_ref, sw_ref, sb_ref, lcw_ref, lcb_ref, wa_ref, wx_ref, ba_ref, bx_ref, lam_ref):
        self.caw = [caw_ref[j:j + 1, :] for j in range(3)]
        self.lcw = [lcw_ref[j:j + 1, :] for j in range(4)]
        self.lcb = lcb_ref[...]
        row = lax.broadcasted_iota(jnp.int32, (CHUNK, CHUNK), 0)
        col = lax.broadcasted_iota(jnp.int32, (CHUNK, CHUNK), 1)
        self.tril = col <= row
        self.sw = jnp.where(self.tril, sw_ref[...], 0.0).astype(BF16)
        self.sb = sb_ref[...]
        self.wa = wa_ref[...]
        self.wx = wx_ref[...]
        self.ba = ba_ref[...]
        self.bx = bx_ref[...]
        lam = lam_ref[...]
        self.neg_c_sp = -LRU_C * _softplus_neg(lam)
        self.dsp_dlam = -_sigmoid(-lam)


def _mixer_a(ld, ldp, w):
    t = {}
    a_x, a_c = ld(AX), ld(AC)
    t["a_x"], t["a_c"], t["a_b"], t["a_z"] = a_x, a_c, ld(AB), ld(AZ)
    ca = a_c * a_x
    ca_p = ldp(AC) * ldp(AX)
    t["ca"], t["ca1"], t["ca2"] = ca, _shift_dn(ca, ca_p, 1), _shift_dn(ca, ca_p, 2)
    t["cv"] = w.caw[2] * ca + w.caw[1] * t["ca1"] + w.caw[0] * t["ca2"]
    t["sa"] = _sigmoid(t["a_z"])
    t["silu_az"] = t["a_z"] * t["sa"]
    t["y_a"] = t["silu_az"] * t["a_b"] * t["cv"]
    return t


def _mixer_b(ld, w):
    t = {}
    v = ld(SV)
    vc = v - jnp.mean(v, axis=1, keepdims=True)
    t["rstd"] = lax.rsqrt(jnp.mean(vc * vc, axis=1, keepdims=True) + EPS)
    t["vn"] = vc * t["rstd"]
    t["z"] = _dot(w.sw, t["vn"].astype(BF16)) + w.sb
    t["s_u"], t["s_z"] = ld(SU), ld(SZ)
    t["ss"] = _sigmoid(t["s_z"])
    t["silu_sz"] = t["s_z"] * t["ss"]
    t["y_s"] = t["silu_sz"] * t["s_u"] * t["z"]
    return t


def _mixer_c(ld, ldp, w, backward):
    t = {}
    r_x, r_xp = ld(RX), ldp(RX)
    t["rx"] = [_shift_dn(r_x, r_xp, 3), _shift_dn(r_x, r_xp, 2), _shift_dn(r_x, r_xp, 1), r_x]
    xc = w.lcb + w.lcw[0] * t["rx"][0] + w.lcw[1] * t["rx"][1] + w.lcw[2] * t["rx"][2] + w.lcw[3] * r_x
    t["xc"] = xc
    xcb = xc.astype(BF16)
    t["r"] = _sigmoid(_dot(xcb, w.wa) + w.ba)
    t["i"] = _sigmoid(_dot(xcb, w.wx) + w.bx)
    la = t["r"] * w.neg_c_sp
    t["a"] = jnp.exp(la)
    t["em"] = _neg_expm1(2.0 * la, t["a"] * t["a"])
    if backward:
        t["inv_mult"] = lax.rsqrt(t["em"])
        t["mult"] = t["em"] * t["inv_mult"]
    else:
        t["mult"] = jnp.sqrt(t["em"])
    t["b"] = t["mult"] * (t["i"] * xc)
    t["r_z"] = ld(RZ)
    t["sr"] = _sigmoid(t["r_z"])
    t["silu_rz"] = t["r_z"] * t["sr"]
    return t


def _mixer_pre_scan(ld, ldp, w, backward):
    t = {**_mixer_a(ld, ldp, w), **_mixer_b(ld, w), **_mixer_c(ld, ldp, w, backward)}
    t["ga"], t["gs"], t["gr"] = _sigmoid(ld(GA)), _sigmoid(ld(GS)), _sigmoid(ld(GR))
    return t


def _weight_specs(n_cb_axis):
    def at(fn):
        return lambda *g: fn(g[n_cb_axis])
    return [
        pl.BlockSpec((3, LANES), at(lambda cb: (0, cb))),
        pl.BlockSpec((None, CHUNK, CHUNK), at(lambda cb: (cb, 0, 0))),
        pl.BlockSpec((None, CHUNK, LANES), at(lambda cb: (cb, 0, 0))),
        pl.BlockSpec((4, LANES), at(lambda cb: (0, cb))),
        pl.BlockSpec((1, LANES), at(lambda cb: (0, cb))),
        pl.BlockSpec((None, LANES, LANES), at(lambda cb: (cb, 0, 0))),
        pl.BlockSpec((None, LANES, LANES), at(lambda cb: (cb, 0, 0))),
        pl.BlockSpec((1, LANES), at(lambda cb: (0, cb))),
        pl.BlockSpec((1, LANES), at(lambda cb: (0, cb))),
        pl.BlockSpec((1, LANES), at(lambda cb: (0, cb))),
    ]


def _chunk_loaders(p_ref, c):
    r0 = pl.multiple_of(c * CHUNK, CHUNK)
    rp = pl.multiple_of(jnp.maximum(c * CHUNK - HALO, 0), HALO)

    def ld(j):
        return p_ref[j, pl.ds(r0, CHUNK), :].astype(F32)

    def ldp(j):
        return jnp.where(c > 0, p_ref[j, pl.ds(rp, HALO), :].astype(F32), 0.0)

    return r0, rp, ld, ldp


def _mixer_fwd(proj, mw):
    _, nb, s, _ = proj.shape
    n_chunks = s // CHUNK

    def body(p_ref, *refs):
        w = _MixerWeights(*refs[:10])
        merged_ref, hs_ref = refs[10:]

        def chunk(c, h_prev):
            r0, _, ld, ldp = _chunk_loaders(p_ref, c)
            t = _mixer_pre_scan(ld, ldp, w, False)
            h = _scan_fwd(t["a"], t["b"], h_prev)
            y_r = t["silu_rz"] * h
            merged = t["ga"] * t["y_a"] + t["gs"] * t["y_s"] + t["gr"] * y_r
            merged_ref[pl.ds(r0, CHUNK), :] = merged.astype(BF16)
            hs_ref[pl.ds(r0, CHUNK), :] = h
            return h[CHUNK - HALO:, :]

        lax.fori_loop(0, n_chunks, chunk, jnp.zeros((HALO, LANES), F32))

    slab = pl.BlockSpec((None, s, LANES), lambda cb, b: (b, 0, cb))
    return pl.pallas_call(
        body, name="mixer_fwd", grid=(D // LANES, nb),
        in_specs=[pl.BlockSpec((N_SEG, None, s, LANES), lambda cb, b: (0, b, 0, cb))] + _weight_specs(0),
        out_specs=[slab, slab],
        out_shape=[jax.ShapeDtypeStruct((nb, s, D), BF16), jax.ShapeDtypeStruct((nb, s, D), F32)],
        compiler_params=_params(("arbitrary", "arbitrary")),
    )(proj, *mw)


def _mixer_bwd(proj, dmerged, hs, mw):
    _, nb, s, _ = proj.shape
    n_chunks = s // CHUNK

    def body(p_ref, dm_ref, hs_ref, *refs):
        w = _MixerWeights(*refs[:10])
        dp_ref, g_caw, g_sw, g_sb, g_lcw, g_vec, g_wa, g_wx = refs[10:]

        @pl.when(pl.program_id(1) == 0)
        def _():
            for ref in (g_caw, g_sw, g_sb, g_lcw, g_vec, g_wa, g_wx):
                ref[...] = jnp.zeros_like(ref)

        def chunk(i, carry):
            dcv_n, dxc_n, lam_n, a_n = carry
            c = n_chunks - 1 - i
            r0, rp, ld, ldp = _chunk_loaders(p_ref, c)
            t = _mixer_pre_scan(ld, ldp, w, True)
            h = hs_ref[pl.ds(r0, CHUNK), :]
            h_p = jnp.where(c > 0, hs_ref[pl.ds(rp, HALO), :], 0.0)
            h_prev = _shift_dn(h, h_p, 1)
            dm = dm_ref[pl.ds(r0, CHUNK), :].astype(F32)
            y_r = t["silu_rz"] * h

            def out(j, val):
                dp_ref[j, pl.ds(r0, CHUNK), :] = val.astype(BF16)

            ga, gs, gr = t["ga"], t["gs"], t["gr"]
            out(GA, dm * t["y_a"] * ga * (1.0 - ga))
            out(GS, dm * t["y_s"] * gs * (1.0 - gs))
            out(GR, dm * y_r * gr * (1.0 - gr))

            dy_a = dm * ga
            out(AZ, dy_a * t["a_b"] * t["cv"] * _dsilu(t["a_z"], t["sa"]))
            out(AB, dy_a * t["silu_az"] * t["cv"])
            dcv = dy_a * t["silu_az"] * t["a_b"]
            dca = w.caw[2] * dcv + w.caw[1] * _shift_up(dcv, dcv_n, 1) + w.caw[0] * _shift_up(dcv, dcv_n, 2)
            out(AC, dca * t["a_x"])
            out(AX, dca * t["a_c"])
            g_caw[2:3, :] += _rowsum(dcv * t["ca"])
            g_caw[1:2, :] += _rowsum(dcv * t["ca1"])
            g_caw[0:1, :] += _rowsum(dcv * t["ca2"])

            dy_s = dm * gs
            out(SZ, dy_s * t["s_u"] * t["z"] * _dsilu(t["s_z"], t["ss"]))
            out(SU, dy_s * t["silu_sz"] * t["z"])
            dz = dy_s * t["silu_sz"] * t["s_u"]
            dzb = dz.astype(BF16)
            g_sb[...] += jnp.broadcast_to(jnp.sum(dz, axis=1, keepdims=True), (CHUNK, LANES))
            g_sw[...] += _dot_nt(dzb, t["vn"].astype(BF16))
            dvn = _dot_tn(w.sw, dzb)
            vn = t["vn"]
            out(SV, t["rstd"] * (dvn - jnp.mean(dvn, axis=1, keepdims=True)
                                 - vn * jnp.mean(dvn * vn, axis=1, keepdims=True)))

            dy_r = dm * gr
            out(RZ, dy_r * h * _dsilu(t["r_z"], t["sr"]))
            lam = _scan_rev(_shift_up(t["a"], a_n, 1), dy_r * t["silu_rz"], lam_n)
            a, r, ig, xc, mult = t["a"], t["r"], t["i"], t["xc"], t["mult"]
            d_i = lam * mult * xc
            d_mult = lam * ig * xc
            dxc = lam * mult * ig
            dla = lam * h_prev * a - d_mult * ((1.0 - t["em"]) * t["inv_mult"])
            g_vec[3:4, :] += _rowsum(dla * r) * (-LRU_C * w.dsp_dlam)
            dpr = (dla * w.neg_c_sp) * r * (1.0 - r)
            dpi = d_i * ig * (1.0 - ig)
            dprb, dpib, xcb = dpr.astype(BF16), dpi.astype(BF16), xc.astype(BF16)
            g_wa[...] += _dot_tn(xcb, dprb)
            g_wx[...] += _dot_tn(xcb, dpib)
            g_vec[1:2, :] += _rowsum(dpr)
            g_vec[2:3, :] += _rowsum(dpi)
            dxc = dxc + _dot_nt(dprb, w.wa) + _dot_nt(dpib, w.wx)
            g_vec[0:1, :] += _rowsum(dxc)
            out(RX, w.lcw[3] * dxc + w.lcw[2] * _shift_up(dxc, dxc_n, 1)
                + w.lcw[1] * _shift_up(dxc, dxc_n, 2) + w.lcw[0] * _shift_up(dxc, dxc_n, 3))
            for j in range(4):
                g_lcw[j:j + 1, :] += _rowsum(dxc * t["rx"][j])
            return dcv[:HALO, :], dxc[:HALO, :], lam[:HALO, :], a[:HALO, :]

        zero = jnp.zeros((HALO, LANES), F32)
        lax.fori_loop(0, n_chunks, chunk, (zero, zero, zero, zero))

        @pl.when(pl.program_id(1) == nb - 1)
        def _():
            g_sw[...] = jnp.where(w.tril, g_sw[...], 0.0)

    slab = lambda dt: pl.BlockSpec((None, s, LANES), lambda cb, b: (b, 0, cb))
    seg = pl.BlockSpec((N_SEG, None, s, LANES), lambda cb, b: (0, b, 0, cb))
    rows = lambda n: pl.BlockSpec((n, LANES), lambda cb, b: (0, cb))
    sq = pl.BlockSpec((None, LANES, LANES), lambda cb, b: (cb, 0, 0))
    n_cb = D // LANES
    return pl.pallas_call(
        body, name="mixer_bwd", grid=(n_cb, nb),
        in_specs=[seg, slab(BF16), slab(F32)] + _weight_specs(0),
        out_specs=[seg, rows(3), sq, sq, rows(4), rows(8), sq, sq],
        out_shape=[
            jax.ShapeDtypeStruct(proj.shape, BF16),
            jax.ShapeDtypeStruct((3, D), F32),
            jax.ShapeDtypeStruct((n_cb, CHUNK, CHUNK), F32),
            jax.ShapeDtypeStruct((n_cb, CHUNK, LANES), F32),
            jax.ShapeDtypeStruct((4, D), F32),
            jax.ShapeDtypeStruct((8, D), F32),
            jax.ShapeDtypeStruct((n_cb, LANES, LANES), F32),
            jax.ShapeDtypeStruct((n_cb, LANES, LANES), F32),
        ],
        compiler_params=_params(("arbitrary", "arbitrary")),
    )(proj, dmerged, hs, *mw)


def _row_tile(s, want):
    return want if s % want == 0 else s


def _norm_mod(x, gain, shift, scale):
    nb, s, _ = x.shape
    tm = _row_tile(s, 512)

    def body(x_ref, g_ref, sh_ref, sc_ref, h_ref):
        xv = x_ref[...]
        r = lax.rsqrt(jnp.mean(xv * xv, axis=1, keepdims=True) + EPS)
        h_ref[...] = ((xv * r) * g_ref[...] * (1.0 + sc_ref[...]) + sh_ref[...]).astype(BF16)

    tile = pl.BlockSpec((None, tm, D), lambda b, m: (b, m, 0))
    vec = pl.BlockSpec((None, 1, D), lambda b, m: (b, 0, 0))
    return pl.pallas_call(
        body, name="norm_mod", grid=(nb, s // tm),
        in_specs=[tile, pl.BlockSpec((1, D), lambda b, m: (0, 0)), vec, vec],
        out_specs=tile, out_shape=jax.ShapeDtypeStruct(x.shape, BF16),
        compiler_params=_params(("arbitrary", "arbitrary")),
    )(x, gain, shift, scale)


def _in_proj(h, wg, dep):
    nb, s, _ = h.shape

    def body(h_ref, w_ref, dep_ref, o_ref):
        o_ref[...] = _dot(h_ref[...], w_ref[...]).astype(BF16)

    return pl.pallas_call(
        body, name="in_proj", grid=(nb, N_DEV * UNITS_PER_DEV),
        in_specs=[pl.BlockSpec((None, s, D), lambda b, u: (b, 0, 0)),
                  pl.BlockSpec((None, D, UNIT), lambda b, u: (u // UNITS_PER_DEV, 0, u % UNITS_PER_DEV)),
                  pl.BlockSpec((8, LANES), lambda b, u: (0, 0))],
        out_specs=pl.BlockSpec((None, None, s, UNIT), lambda b, u: (u // 2, b, 0, u % 2)),
        out_shape=jax.ShapeDtypeStruct((N_SEG, nb, s, D), BF16),
        compiler_params=_params(("arbitrary", "arbitrary")),
    )(h, wg, dep)


def _out_proj(x, merged, wout, gate):
    nb, s, _ = x.shape
    tm = _row_tile(s, 512)

    def body(x_ref, m_ref, w_ref, g_ref, o_ref):
        o_ref[...] = x_ref[...] + g_ref[...] * _dot(m_ref[...], w_ref[...])

    tile = pl.BlockSpec((None, tm, D), lambda b, m: (b, m, 0))
    return pl.pallas_call(
        body, name="out_proj", grid=(nb, s // tm),
        in_specs=[tile, tile, pl.BlockSpec((D, D), lambda b, m: (0, 0)),
                  pl.BlockSpec((None, 1, D), lambda b, m: (b, 0, 0))],
        out_specs=tile, out_shape=jax.ShapeDtypeStruct(x.shape, F32),
        compiler_params=_params(("arbitrary", "arbitrary")),
    )(x, merged, wout, gate)


def _loss_head(x, gain, target):
    nb, s, _ = x.shape
    tm = _row_tile(s, 512)

    def body(x_ref, g_ref, t_ref, loss_ref, dx_ref, dg_ref):
        first = (pl.program_id(0) == 0) & (pl.program_id(1) == 0)
        last = (pl.program_id(0) == nb - 1) & (pl.program_id(1) == s // tm - 1)

        @pl.when(first)
        def _():
            loss_ref[...] = jnp.zeros_like(loss_ref)
            dg_ref[...] = jnp.zeros_like(dg_ref)

        xv = x_ref[...]
        r = lax.rsqrt(jnp.mean(xv * xv, axis=1, keepdims=True) + EPS)
        xn = xv * r
        g = g_ref[...]
        e = xn * g - t_ref[...]
        loss_ref[...] += _rowsum(e * e) * (0.5 / D)
        dy = e * (1.0 / D)
        dg_ref[...] += _rowsum(dy * xn)
        dxn = dy * g
        dx_ref[...] = r * (dxn - xn * jnp.mean(dxn * xn, axis=1, keepdims=True))

        @pl.when(last)
        def _():
            loss_ref[...] = jnp.broadcast_to(jnp.sum(loss_ref[...], axis=1, keepdims=True), (1, D))

    tile = pl.BlockSpec((None, tm, D), lambda b, m: (b, m, 0))
    vec = pl.BlockSpec((1, D), lambda b, m: (0, 0))
    return pl.pallas_call(
        body, name="loss_head", grid=(nb, s // tm),
        in_specs=[tile, vec, tile], out_specs=[vec, tile, vec],
        out_shape=[jax.ShapeDtypeStruct((1, D), F32), jax.ShapeDtypeStruct(x.shape, F32),
                   jax.ShapeDtypeStruct((1, D), F32)],
        compiler_params=_params(("arbitrary", "arbitrary")),
    )(x, gain, target)


def _out_proj_bwd(dxo, merged, wout, gate):
    nb, s, _ = dxo.shape
    tm = _row_tile(s, 512)

    def body(d_ref, m_ref, w_ref, g_ref, dm_ref, gw_ref, dg_ref):
        @pl.when((pl.program_id(0) == 0) & (pl.program_id(1) == 0))
        def _():
            gw_ref[...] = jnp.zeros_like(gw_ref)

        @pl.when(pl.program_id(1) == 0)
        def _():
            dg_ref[...] = jnp.zeros_like(dg_ref)

        d = d_ref[...]
        m = m_ref[...]
        wv = w_ref[...]
        dg_ref[...] += _rowsum(d * _dot(m, wv))
        dout = (d * g_ref[...]).astype(BF16)
        dm_ref[...] = _dot_nt(dout, wv).astype(BF16)
        gw_ref[...] += _dot_tn(m, dout)

    tile = pl.BlockSpec((None, tm, D), lambda b, m: (b, m, 0))
    vec = pl.BlockSpec((None, 1, D), lambda b, m: (b, 0, 0))
    full = pl.BlockSpec((D, D), lambda b, m: (0, 0))
    return pl.pallas_call(
        body, name="out_proj_bwd", grid=(nb, s // tm),
        in_specs=[tile, tile, full, vec], out_specs=[tile, full, vec],
        out_shape=[jax.ShapeDtypeStruct(dxo.shape, BF16), jax.ShapeDtypeStruct((D, D), F32),
                   jax.ShapeDtypeStruct((nb, 1, D), F32)],
        compiler_params=_params(("arbitrary", "arbitrary")),
    )(dxo, merged, wout, gate)


def _in_proj_bwd_h(dproj, wg, dep):
    _, nb, s, _ = dproj.shape

    def body(dp_ref, w0_ref, w1_ref, dep_ref, dh_ref):
        j = pl.program_id(1)
        part = _dot_nt(dp_ref[...], jnp.concatenate([w0_ref[...], w1_ref[...]], axis=1))

        @pl.when(j == 0)
        def _():
            dh_ref[...] = part

        @pl.when(j > 0)
        def _():
            dh_ref[...] += part

    def unit(k):
        return pl.BlockSpec((None, D, UNIT),
                            lambda b, j: ((2 * j + k) // UNITS_PER_DEV, 0, (2 * j + k) % UNITS_PER_DEV))

    return pl.pallas_call(
        body, name="in_proj_bwd_h", grid=(nb, N_SEG),
        in_specs=[pl.BlockSpec((None, None, s, D), lambda b, j: (j, b, 0, 0)), unit(0), unit(1),
                  pl.BlockSpec((8, LANES), lambda b, j: (0, 0))],
        out_specs=pl.BlockSpec((None, s, D), lambda b, j: (b, 0, 0)),
        out_shape=jax.ShapeDtypeStruct((nb, s, D), F32),
        compiler_params=_params(("arbitrary", "arbitrary")),
    )(dproj, wg, wg, dep)


def _norm_mod_bwd(dh, x, dxo, gain, scale):
    nb, s, _ = x.shape
    tm = _row_tile(s, 512)

    def body(dh_ref, x_ref, dxo_ref, g_ref, sc_ref, dx_ref, dsh_ref, dsc_ref, dg_ref):
        b, m = pl.program_id(0), pl.program_id(1)

        @pl.when((b == 0) & (m == 0))
        def _():
            dg_ref[...] = jnp.zeros_like(dg_ref)

        @pl.when(m == 0)
        def _():
            dsh_ref[...] = jnp.zeros_like(dsh_ref)
            dsc_ref[...] = jnp.zeros_like(dsc_ref)

        dh = dh_ref[...]
        xv = x_ref[...]
        r = lax.rsqrt(jnp.mean(xv * xv, axis=1, keepdims=True) + EPS)
        xn = xv * r
        g = g_ref[...]
        one_sc = 1.0 + sc_ref[...]
        dsh_ref[...] += _rowsum(dh)
        dsc_ref[...] += _rowsum(dh * (xn * g))
        dg_ref[...] += _rowsum(dh * one_sc * xn)
        dxn = dh * (g * one_sc)
        dx_ref[...] = dxo_ref[...] + r * (dxn - xn * jnp.mean(dxn * xn, axis=1, keepdims=True))

    tile = pl.BlockSpec((None, tm, D), lambda b, m: (b, m, 0))
    vec = pl.BlockSpec((None, 1, D), lambda b, m: (b, 0, 0))
    one = pl.BlockSpec((1, D), lambda b, m: (0, 0))
    return pl.pallas_call(
        body, name="norm_mod_bwd", grid=(nb, s // tm),
        in_specs=[tile, tile, tile, one, vec],
        out_specs=[tile, vec, vec, one],
        out_shape=[jax.ShapeDtypeStruct(x.shape, F32), jax.ShapeDtypeStruct((nb, 1, D), F32),
                   jax.ShapeDtypeStruct((nb, 1, D), F32), jax.ShapeDtypeStruct((1, D), F32)],
        compiler_params=_params(("arbitrary", "arbitrary")),
    )(dh, x, dxo, gain, scale)


def _in_proj_bwd_w(h, dproj, dep):
    nb, s, _ = h.shape
    tm = _row_tile(s, 2048)
    n_m = s // tm

    def body(h_ref, dp_ref, dep_ref, o_ref, acc_ref):
        b, m = pl.program_id(1), pl.program_id(2)

        @pl.when((b == 0) & (m == 0))
        def _():
            acc_ref[...] = jnp.zeros_like(acc_ref)

        acc_ref[...] += _dot_tn(h_ref[...], dp_ref[...])

        @pl.when((b == nb - 1) & (m == n_m - 1))
        def _():
            o_ref[0] = acc_ref[:, :UNIT].astype(BF16)
            o_ref[1] = acc_ref[:, UNIT:].astype(BF16)

    return pl.pallas_call(
        body, name="in_proj_bwd_w", grid=(N_SEG, nb, n_m),
        in_specs=[pl.BlockSpec((None, tm, D), lambda j, b, m: (b, m, 0)),
                  pl.BlockSpec((None, None, tm, D), lambda j, b, m: (j, b, m, 0)),
                  pl.BlockSpec((8, LANES), lambda j, b, m: (0, 0))],
        out_specs=pl.BlockSpec((2, D, UNIT), lambda j, b, m: (j, 0, 0)),
        out_shape=jax.ShapeDtypeStruct((2 * N_SEG, D, UNIT), BF16),
        scratch_shapes=[pltpu.VMEM((D, D), F32)],
        compiler_params=_params(("arbitrary", "arbitrary", "arbitrary")),
    )(h, dproj, dep)


def _mod_proj(c_all, w_mod, b_mod_mine):
    nl, _, ncol = w_mod.shape
    nbg = c_all.shape[0]

    def body(c_ref, w_ref, b_ref, o_ref):
        cv = c_ref[...]
        o_ref[...] = jnp.dot(cv * jax.nn.sigmoid(cv), w_ref[...], preferred_element_type=F32,
                             precision=lax.Precision.HIGHEST) + b_ref[...]

    return pl.pallas_call(
        body, name="mod_proj", grid=(nl,),
        in_specs=[pl.BlockSpec((nbg, D), lambda l: (0, 0)), pl.BlockSpec((None, D, ncol), lambda l: (l, 0, 0)),
                  pl.BlockSpec((None, 1, ncol), lambda l: (l, 0, 0))],
        out_specs=pl.BlockSpec((None, nbg, ncol), lambda l: (l, 0, 0)),
        out_shape=jax.ShapeDtypeStruct((nl, nbg, ncol), F32),
        compiler_params=_params(("arbitrary",)),
    )(c_all, w_mod, b_mod_mine)


def _mod_grad(c_all, dmod_all, dmod_mine):
    nl, nbg, ncol = dmod_mine.shape

    def body(c_ref, da_ref, dm_ref, gw_ref, gb_ref):
        cv = c_ref[...]
        gw_ref[...] = lax.dot_general(cv * jax.nn.sigmoid(cv), dm_ref[...], (((0,), (0,)), ((), ())),
                                      preferred_element_type=F32, precision=lax.Precision.HIGHEST)
        gb_ref[...] = _rowsum(da_ref[...])

    return pl.pallas_call(
        body, name="mod_grad", grid=(nl,),
        in_specs=[pl.BlockSpec((nbg, D), lambda l: (0, 0)), pl.BlockSpec((None, nbg, 3 * D), lambda l: (l, 0, 0)),
                  pl.BlockSpec((None, nbg, ncol), lambda l: (l, 0, 0))],
        out_specs=[pl.BlockSpec((None, D, ncol), lambda l: (l, 0, 0)),
                   pl.BlockSpec((None, 1, 3 * D), lambda l: (l, 0, 0))],
        out_shape=[jax.ShapeDtypeStruct((nl, D, ncol), F32), jax.ShapeDtypeStruct((nl, 1, 3 * D), F32)],
        compiler_params=_params(("arbitrary",)),
    )(c_all, dmod_all, dmod_mine)


def _adamw(parts, w, m, v, name, layer=None, prev=None):
    n_parts, n_u, n_r, cu = parts.shape
    assert w.shape[-2:] == (n_r, n_u * cu), (parts.shape, w.shape)
    tr = n_r
    for cand in (512, 256, 128):
        if n_r > cand and n_r % cand == 0:
            tr = cand
            break
    n_prev = 0 if prev is None else 4

    def body(p_ref, w_ref, m_ref, v_ref, *rest):
        g_ref, d_ref, nm_ref, nv_ref = rest[n_prev:]
        g = p_ref[0].astype(F32)
        for k in range(1, n_parts):
            g = g + p_ref[k].astype(F32)
        m2 = ADAM_B1 * m_ref[...] + (1.0 - ADAM_B1) * g
        v2 = ADAM_B2 * v_ref[...] + (1.0 - ADAM_B2) * (g * g)
        m_hat = m2 / (1.0 - ADAM_B1 ** ADAM_STEP)
        v_hat = v2 / (1.0 - ADAM_B2 ** ADAM_STEP)
        g_ref[...] = g
        d_ref[...] = -ADAM_LR * (m_hat / (jnp.sqrt(v_hat) + ADAM_EPS) + ADAM_WD * w_ref[...])
        nm_ref[...] = m2
        nv_ref[...] = v2

    if layer is None:
        tile = pl.BlockSpec((tr, cu), lambda u, i: (i, u))
    else:
        tile = pl.BlockSpec((None, tr, cu), lambda u, i: (layer, i, u))
    shp = jax.ShapeDtypeStruct(w.shape, F32)
    return pl.pallas_call(
        body, name=name, grid=(n_u, n_r // tr),
        in_specs=[pl.BlockSpec((n_parts, None, tr, cu), lambda u, i: (0, u, i, 0)), tile, tile, tile]
        + [pl.BlockSpec(memory_space=pl.ANY)] * n_prev,
        out_specs=[tile, tile, tile, tile], out_shape=[shp, shp, shp, shp],
        input_output_aliases={4 + k: k for k in range(n_prev)},
        compiler_params=_params(("arbitrary", "arbitrary")),
    )(parts, w, m, v, *(prev or ()))


def _gathered_cols(g, inner):
    k = len(inner)
    perm = tuple(range(1, k + 1)) + (0, k + 1)
    t = jnp.transpose(g, perm)
    return t.reshape(tuple(inner) + (g.shape[0] * g.shape[-1],))


def _pair_blocks(wh):
    z = jnp.zeros((8, 64, 64), wh.dtype)
    w2 = wh.reshape(8, 2, 64, 64)
    top = jnp.concatenate([w2[:, 0], z], axis=2)
    bot = jnp.concatenate([z, w2[:, 1]], axis=2)
    return jnp.concatenate([top, bot], axis=1).astype(BF16)


def _unpair_blocks(g):
    return jnp.stack([g[:, :64, :64], g[:, 64:, 64:]], axis=1).reshape(16, 64, 64)


FLAT_ROWS = 512


def _pack_rows(arrays, lead=0):
    parts = [a.reshape(a.shape[:lead] + (-1, LANES)) for a in arrays]
    rows = jnp.concatenate(parts, axis=lead)
    pad = [(0, 0)] * rows.ndim
    pad[lead] = (0, (-rows.shape[lead]) % FLAT_ROWS)
    return jnp.pad(rows, pad)


def kernel(x, c, norm_gain, w_mod, b_mod, w_in, w_out, conv_a_w, sgu_w, sgu_b, lru_conv_w, lru_conv_b, lru_wa, lru_ba, lru_wx, lru_bx, lru_lambda, final_gain, loss_target, m_norm_gain, m_w_mod, m_b_mod, m_w_in, m_w_out, m_conv_a_w, m_sgu_w, m_sgu_b, m_lru_conv_w, m_lru_conv_b, m_lru_wa, m_lru_ba, m_lru_wx, m_lru_bx, m_lru_lambda, m_final_gain, v_norm_gain, v_w_mod, v_b_mod, v_w_in, v_w_out, v_conv_a_w, v_sgu_w, v_sgu_b, v_lru_conv_w, v_lru_conv_b, v_lru_wa, v_lru_ba, v_lru_wx, v_lru_bx, v_lru_lambda, v_final_gain):
    nl = w_in.shape[0]
    nb, s, _ = x.shape
    me = _my_index()
    mod_cols = w_mod.shape[2]


    small = jnp.concatenate([c.reshape(-1, LANES), conv_a_w.reshape(-1, LANES), lru_conv_w.reshape(-1, LANES)])
    n_c, n_ca = nb * D // LANES, nl * 3
    n_small = small.shape[0]
    small = jnp.pad(small, ((0, (-n_small) % 8), (0, 0)))
    small_all, _ = _all_gather(small, "gather_small")
    c_all = small_all[:, :n_c].reshape(N_DEV * nb, D)
    conv_a_full = _gathered_cols(small_all[:, n_c:n_c + n_ca].reshape(N_DEV, nl, 3, LANES), (nl, 3))
    lru_conv_full = _gathered_cols(small_all[:, n_c + n_ca:n_small].reshape(N_DEV, nl, 4, LANES), (nl, 4))

    b_mod_mine = lax.dynamic_slice_in_dim(b_mod, me * mod_cols, mod_cols, axis=1)[:, None, :]
    mod_mine = _mod_proj(c_all, w_mod, b_mod_mine)
    mod_all, mod_token = _all_gather(mod_mine.reshape(nl * N_DEV * nb, mod_cols), "gather_mod")
    mod_full = _gathered_cols(mod_all.reshape(N_DEV, nl, N_DEV * nb, mod_cols), (nl, N_DEV * nb))
    mod_loc = lax.dynamic_slice_in_dim(mod_full, me * nb, nb, axis=1)
    shift, scale, gate = [mod_loc[:, :, j * D:(j + 1) * D][:, :, None, :] for j in range(3)]

    def start_w_in(l, token):
        return _split_start(_after(w_in[l], token).astype(BF16), _peers_same_core, False, "gather_w_in_start")

    def gathered_w_in(started, after):
        block, land = _split_wait(started, after, _peers_same_core, False, False, "gather_w_in_wait")
        return _gather_finish(block, land, "gather_w_in_finish")

    sgu_b_lanes = jnp.broadcast_to(sgu_b[..., None], sgu_b.shape + (LANES,))
    mws = []
    for l in range(nl):
        mws.append((conv_a_full[l], sgu_w[l], sgu_b_lanes[l], lru_conv_full[l], lru_conv_b[l][None, :],
                    _pair_blocks(lru_wa[l]), _pair_blocks(lru_wx[l]), lru_ba[l].reshape(1, D),
                    lru_bx[l].reshape(1, D), lru_lambda[l][None, :]))

    rep_names = ["sgu_w", "sgu_b", "lru_conv_b", "lru_wa", "lru_ba", "lru_wx", "lru_bx", "lru_lambda"]
    rep_w = dict(sgu_w=sgu_w, sgu_b=sgu_b, lru_conv_b=lru_conv_b, lru_wa=lru_wa, lru_ba=lru_ba,
                 lru_wx=lru_wx, lru_bx=lru_bx, lru_lambda=lru_lambda)
    rep_m = dict(sgu_w=m_sgu_w, sgu_b=m_sgu_b, lru_conv_b=m_lru_conv_b, lru_wa=m_lru_wa,
                 lru_ba=m_lru_ba, lru_wx=m_lru_wx, lru_bx=m_lru_bx, lru_lambda=m_lru_lambda)
    rep_v = dict(sgu_w=v_sgu_w, sgu_b=v_sgu_b, lru_conv_b=v_lru_conv_b, lru_wa=v_lru_wa,
                 lru_ba=v_lru_ba, lru_wx=v_lru_wx, lru_bx=v_lru_bx, lru_lambda=v_lru_lambda)

    rep_w_all, rep_m_all, rep_v_all = [_pack_rows([src[n] for n in rep_names], lead=1)
                                       for src in (rep_w, rep_m, rep_v)]
    early = [rep_w_all, rep_m_all, rep_v_all] + [a for mw in mws for a in mw]

    xs, hs_bf, projs, mergeds, states, wg = [], [], [], [], [], []
    xl = x
    ici = {0: start_w_in(0, mod_token)}
    d2d = {}
    wo_started = _split_start(_after(w_out, ici[0][4]).astype(BF16).reshape(nl * (D // N_DEV), D),
                              _peers_all, False, "gather_w_out_start")
    wo = None
    for l in range(nl):
        h = _norm_mod(xl, _after(norm_gain[l][None, :], wo_started[4]), shift[l], scale[l])
        if l == 0:
            wg_l, token = gathered_w_in(ici[0], [h] + early)
            ici[1] = start_w_in(1, token)
            dep = ici[1][4]
        else:
            wg_l = _forward_wait(d2d[l], h, "gather_w_in_d2d_wait")
            dep = d2d[l][4]
        wg.append(wg_l)
        proj = _in_proj(h, wg_l, dep)
        merged, st = _mixer_fwd(proj, mws[l])
        xs.append(xl), hs_bf.append(h), projs.append(proj), mergeds.append(merged), states.append(st)
        gate_l = gate[l]
        if l + 1 < nl:
            block, land = _split_wait(ici[l + 1], merged, _peers_same_core, False, False, "gather_w_in_wait")
            d2d[l + 1] = _forward_start(block, land, "gather_w_in_d2d_start")
            gate_l = _after(gate_l, d2d[l + 1][4])
            if l + 2 < nl:
                ici[l + 2] = start_w_in(l + 2, d2d[l + 1][4])
                gate_l = _after(gate_l, ici[l + 2][4])
        if wo is None:
            _, wo_all = _split_wait(wo_started, merged, _peers_all, False, True, "gather_w_out_wait")
            wo = jnp.transpose(wo_all.reshape(N_DEV, nl, D // N_DEV, D), (1, 0, 2, 3)).reshape(nl, D, D)
        xl = _out_proj(xl, merged, wo[l], gate_l)

    loss_row, dx, g_final = _loss_head(xl, final_gain[None, :], loss_target)
    loss = lax.psum(loss_row[0, 0], ("x", "y", "c"))

    res_big, g_conv = {}, [None] * nl
    dmods = [None] * nl

    def finish_exchange(pending, after):
        l, h_in, h_out, h_rep = pending
        _, recv = _split_wait(h_in, after, _peers_all, True, True, "scatter_w_in_wait")
        res_big["w_in"] = _adamw(recv, w_in, m_w_in, v_w_in, "adamw_w_in", l, res_big.get("w_in"))
        _, recv = _split_wait(h_out, after, _peers_all, True, True, "scatter_w_out_wait")
        res_big["w_out"] = _adamw(recv, w_out, m_w_out, v_w_out, "adamw_w_out", l, res_big.get("w_out"))
        _, recv = _split_wait(h_rep, after, _peers_all, False, True, "gather_rep_wait")
        res_big["rep"] = _adamw(recv[:, None], rep_w_all, rep_m_all, rep_v_all, "adamw_rep", l, res_big.get("rep"))

    pending = None
    g_gains = [None] * nl
    for l in reversed(range(nl)):
        dmerged, gw_out, dgate = _out_proj_bwd(dx, mergeds[l], wo[l], gate[l])
        dproj, g_caw, g_sw, g_sb, g_lcw, g_vec, g_wa, g_wx = _mixer_bwd(projs[l], dmerged, states[l], mws[l])
        g_conv[l] = (g_caw, g_lcw)
        rep_g = dict(
            sgu_w=g_sw, sgu_b=g_sb[:, :, 0], lru_conv_b=g_vec[0],
            lru_wa=_unpair_blocks(g_wa), lru_ba=g_vec[1].reshape(16, 64), lru_wx=_unpair_blocks(g_wx),
            lru_bx=g_vec[2].reshape(16, 64), lru_lambda=g_vec[3])
        rep_block = _pack_rows([rep_g[n] for n in rep_names])
        h_out = _split_start(gw_out.reshape(N_DEV, 1, D // N_DEV, D), _peers_all, True, "scatter_w_out_start")
        h_rep = _split_start(rep_block, _peers_all, False, "gather_rep_start")
        gw_in = _in_proj_bwd_w(hs_bf[l], dproj, h_out[4] + h_rep[4])
        h_in = _split_start(gw_in.reshape(N_DEV, UNITS_PER_DEV, D, UNIT), _peers_all, True, "scatter_w_in_start")
        started = (l, h_in, h_out, h_rep)
        dh = _in_proj_bwd_h(dproj, wg[l], h_in[4])
        dx, dshift, dscale, g_gain = _norm_mod_bwd(dh, xs[l], dx, norm_gain[l][None, :], scale[l])
        g_gains[l] = g_gain
        dmods[l] = jnp.concatenate([dshift, dscale, dgate], axis=2)[:, 0, :]
        if pending is not None:
            finish_exchange(pending, dx)
        pending = started

    conv_parts = jnp.concatenate(
        [jnp.stack([g_conv[l][0] for l in range(nl)]).reshape(nl * 3, N_DEV, LANES),
         jnp.stack([g_conv[l][1] for l in range(nl)]).reshape(nl * 4, N_DEV, LANES)], axis=0)
    conv_parts = jnp.transpose(conv_parts, (1, 0, 2))[:, None]
    conv_recv = _all_to_all(conv_parts, "scatter_conv")

    dmod_loc = jnp.stack(dmods).reshape(nl * nb, 3 * D)
    gain_rows = jnp.pad(jnp.concatenate(g_gains + [g_final], axis=0), ((0, (-(nl + 1)) % 8), (0, 2 * D)))
    tail_g, _ = _all_gather(jnp.concatenate([dmod_loc, gain_rows], axis=0), "gather_dmod",
                            dep=res_big["w_in"][3])
    dmod_g = tail_g[:, :nl * nb]
    gain_parts = tail_g[:, nl * nb:nl * nb + nl + 1, :D][:, None]
    gain_cat = lambda a, b: jnp.concatenate([a, b[None, :]], axis=0)
    res_gain = _adamw(gain_parts, gain_cat(norm_gain, final_gain), gain_cat(m_norm_gain, m_final_gain),
                      gain_cat(v_norm_gain, v_final_gain), "adamw_gain")
    dmod_all = jnp.transpose(dmod_g.reshape(N_DEV, nl, nb, 3 * D), (1, 0, 2, 3)).reshape(nl, N_DEV * nb, 3 * D)
    dmod_mine = lax.dynamic_slice_in_dim(dmod_all, me * mod_cols, mod_cols, axis=2)
    gw_mod, gb_mod = _mod_grad(c_all, dmod_all, dmod_mine)
    res_w_mod = _adamw(gw_mod.reshape(1, 1, nl * D, mod_cols), w_mod.reshape(nl * D, mod_cols),
                       m_w_mod.reshape(nl * D, mod_cols), v_w_mod.reshape(nl * D, mod_cols), "adamw_w_mod")
    res_w_mod = [a.reshape(nl, D, mod_cols) for a in res_w_mod]
    res_b_mod = _adamw(gb_mod.reshape(1, 1, nl, 3 * D), b_mod, m_b_mod, v_b_mod, "adamw_b_mod")
    finish_exchange(pending, res_b_mod[1])

    cat = lambda a, b: jnp.concatenate([a.reshape(nl * 3, LANES), b.reshape(nl * 4, LANES)], axis=0)
    res_conv = _adamw(conv_recv, cat(conv_a_w, lru_conv_w), cat(m_conv_a_w, m_lru_conv_w),
                      cat(v_conv_a_w, v_lru_conv_w), "adamw_conv")
    res_conv_a = [a[:nl * 3].reshape(nl, 3, LANES) for a in res_conv]
    res_lru_conv = [a[nl * 3:].reshape(nl, 4, LANES) for a in res_conv]

    res_rep = []
    for k in range(4):
        off, d = 0, {}
        for n in rep_names:
            n_rows = rep_w[n][0].size // LANES
            d[n] = res_big["rep"][k][:, off:off + n_rows].reshape(rep_w[n].shape)
            off += n_rows
        res_rep.append(d)

    def leaf(k, name):
        if name == "norm_gain":
            return res_gain[k][:nl]
        if name == "final_gain":
            return res_gain[k][nl]
        if name == "w_mod":
            return res_w_mod[k]
        if name == "b_mod":
            return res_b_mod[k]
        if name in ("w_in", "w_out"):
            return res_big[name][k]
        if name == "conv_a_w":
            return res_conv_a[k]
        if name == "lru_conv_w":
            return res_lru_conv[k]
        return res_rep[k][name]

    order = ["norm_gain", "w_mod", "b_mod", "w_in", "w_out", "conv_a_w", "sgu_w", "sgu_b", "lru_conv_w",
             "lru_conv_b", "lru_wa", "lru_ba", "lru_wx", "lru_bx", "lru_lambda", "final_gain"]
    outs = [loss, dx]
    for k in range(4):
        outs += [leaf(k, n) for n in order]
    return tuple(outs)
```

```python
import functools

import jax
import jax.numpy as jnp
from jax import lax
from jax.experimental import pallas as pl
from jax.experimental.pallas import tpu as pltpu

F32 = jnp.float32
BF16 = jnp.bfloat16

D = 1024
N_DEV = 8
N_SEG = 12
LANES = 128
SUBLANES = 8
CHUNK = 128
HALO = 16
UNIT = 512
UNITS_PER_DEV = 3
EPS = 1e-6
LRU_C = 8.0
ADAM_LR, ADAM_B1, ADAM_B2, ADAM_EPS, ADAM_WD, ADAM_STEP = 0.001, 0.9, 0.999, 1e-08, 0.01, 10
VMEM_LIMIT = 56 * 1024 * 1024

AX, AB, AC, AZ, SU, SV, SZ, RX, RZ, GA, GS, GR = range(N_SEG)
MESH = pl.DeviceIdType.MESH


def _params(sem=None):
    return pltpu.CompilerParams(dimension_semantics=sem, vmem_limit_bytes=VMEM_LIMIT)


def _my_index():
    return 4 * lax.axis_index("x") + 2 * lax.axis_index("y") + lax.axis_index("c")


def _all_gather(block, name, dep=None):
    def body(x_ref, *refs):
        out_ref, token, send_sems, recv_sems, local_sem = refs[-5:]
        x, y, c = lax.axis_index("x"), lax.axis_index("y"), lax.axis_index("c")
        me, sibling = (x, y, c), (x, y, 1 - c)
        chips = [(1 - x, y), (x, 1 - y), (1 - x, 1 - y)]
        token[...] = jnp.zeros_like(token)

        def rows(px, py, pc):
            return out_ref.at[4 * px + 2 * py + pc]

        def copy(k, blk, to, src=None):
            return pltpu.make_async_remote_copy(
                src_ref=rows(*blk) if src is None else src, dst_ref=rows(*blk),
                send_sem=send_sems.at[k], recv_sem=recv_sems.at[k], device_id=to, device_id_type=MESH)

        mine = pltpu.make_async_copy(x_ref, rows(*me), local_sem)
        mine.start()
        first = [copy(0, me, sibling, src=x_ref)]
        first += [copy(1 + j, me, (*chip, c), src=x_ref) for j, chip in enumerate(chips)]
        for cp in first:
            cp.start()
        passed = [copy(4 + j, (*chip, c), sibling) for j, chip in enumerate(chips)]
        for j, chip in enumerate(chips):
            copy(1 + j, (*chip, c), me).wait_recv()
            passed[j].start()
        copy(0, sibling, me).wait_recv()
        for j, chip in enumerate(chips):
            copy(4 + j, (*chip, 1 - c), me).wait_recv()
        for cp in first + passed:
            cp.wait_send()
        mine.wait()

    return pl.pallas_call(
        body, name=name,
        out_shape=[jax.ShapeDtypeStruct((N_DEV,) + block.shape, block.dtype), jax.ShapeDtypeStruct((8, LANES), F32)],
        in_specs=[pl.BlockSpec(memory_space=pltpu.VMEM)]
        + [pl.BlockSpec(memory_space=pl.ANY)] * (dep is not None),
        out_specs=[pl.BlockSpec(memory_space=pl.ANY), pl.BlockSpec(memory_space=pltpu.VMEM)],
        scratch_shapes=[pltpu.SemaphoreType.DMA((7,)), pltpu.SemaphoreType.DMA((7,)), pltpu.SemaphoreType.DMA],
    )(block, *([dep] if dep is not None else []))


def _all_to_all(blocks, name):
    def body(x_ref, out_ref, send_sems, recv_sems, local_sem):
        x, y, c = lax.axis_index("x"), lax.axis_index("y"), lax.axis_index("c")
        my = 4 * x + 2 * y + c
        mine = pltpu.make_async_copy(x_ref.at[my], out_ref.at[my], local_sem)
        mine.start()
        peers = []
        for r in range(1, N_DEV):
            px = 1 - x if r & 4 else x
            py = 1 - y if r & 2 else y
            pc = 1 - c if r & 1 else c
            peers.append((r - 1, 4 * px + 2 * py + pc, (px, py, pc)))

        def copy(k, src_slot, dst_slot, to):
            return pltpu.make_async_remote_copy(
                src_ref=x_ref.at[src_slot], dst_ref=out_ref.at[dst_slot],
                send_sem=send_sems.at[k], recv_sem=recv_sems.at[k], device_id=to, device_id_type=MESH)

        sends = [copy(k, pid, my, to) for k, pid, to in peers]
        for cp in sends:
            cp.start()
        for k, pid, to in peers:
            copy(k, pid, pid, to).wait_recv()
        for cp in sends:
            cp.wait_send()
        mine.wait()

    return pl.pallas_call(
        body, name=name,
        out_shape=jax.ShapeDtypeStruct(blocks.shape, blocks.dtype),
        in_specs=[pl.BlockSpec(memory_space=pltpu.VMEM)],
        out_specs=pl.BlockSpec(memory_space=pl.ANY),
        scratch_shapes=[pltpu.SemaphoreType.DMA((7,)), pltpu.SemaphoreType.DMA((7,)), pltpu.SemaphoreType.DMA],
    )(blocks)


_HBM = pl.BlockSpec(memory_space=pltpu.HBM)
_SEM = pl.BlockSpec(memory_space=pltpu.SEMAPHORE)
_EFFECT = pltpu.SideEffectType.DATAFLOW_SIDE_EFFECTING


def _peers_all(x, y, c):
    out = []
    for r in range(1, N_DEV):
        px = 1 - x if r & 4 else x
        py = 1 - y if r & 2 else y
        pc = 1 - c if r & 1 else c
        out.append((r - 1, 4 * px + 2 * py + pc, (px, py, pc)))
    return out


def _peers_same_core(x, y, c):
    return [(k, 4 * px + 2 * py + c, (px, py, c))
            for k, (px, py) in enumerate([(1 - x, y), (x, 1 - y), (1 - x, 1 - y)])]


def _split_start(src, peers_fn, scatter, name, dep=None):
    blk = src.shape[1:] if scatter else src.shape
    land_shape = (N_DEV,) + tuple(blk)
    n = len(peers_fn(0, 0, 0))
    deps = [] if dep is None else [dep]

    def body(x_ref, land_ref, *rest):
        send_sems, recv_sems, x_thru, land_thru, token = rest[len(deps):]
        x, y, c = lax.axis_index("x"), lax.axis_index("y"), lax.axis_index("c")
        my = 4 * x + 2 * y + c
        for k, pid, to in peers_fn(x, y, c):
            pltpu.make_async_remote_copy(
                src_ref=x_ref.at[pid] if scatter else x_ref, dst_ref=land_ref.at[my],
                send_sem=send_sems.at[k], recv_sem=recv_sems.at[k], device_id=to, device_id_type=MESH).start()
        token[...] = jnp.zeros_like(token)

    return pl.pallas_call(
        body, name=name,
        out_shape=(pltpu.SemaphoreType.DMA((n,)), pltpu.SemaphoreType.DMA((n,)),
                   pltpu.HBM(src.shape, src.dtype), pltpu.HBM(land_shape, src.dtype),
                   jax.ShapeDtypeStruct((8, LANES), F32)),
        in_specs=(_HBM, _HBM) + (pl.BlockSpec(memory_space=pl.ANY),) * len(deps),
        out_specs=(_SEM, _SEM, _HBM, _HBM, pl.BlockSpec(memory_space=pltpu.VMEM)),
        input_output_aliases={0: 2, 1: 3},
        compiler_params=pltpu.CompilerParams(has_side_effects=_EFFECT),
    )(pltpu.with_memory_space_constraint(src, pltpu.HBM),
      pltpu.with_memory_space_constraint(lax.empty(land_shape, src.dtype), pltpu.HBM), *deps)


def _split_wait(handles, after, peers_fn, scatter, own, name):
    send_sems, recv_sems, src_thru, land_thru, _ = handles
    blk = land_thru.shape[1:]
    after = list(after) if isinstance(after, (list, tuple)) else [after]

    def body(x_ref, land_ref, send_sems, recv_sems, *rest):
        stage = rest[len(after) + 2:]
        x, y, c = lax.axis_index("x"), lax.axis_index("y"), lax.axis_index("c")
        if own:
            my = 4 * x + 2 * y + c
            mine = _staged_copy(x_ref.at[my] if scatter else x_ref, land_ref.at[my], *stage)
        for k, pid, to in peers_fn(x, y, c):
            cp = pltpu.make_async_remote_copy(
                src_ref=x_ref.at[pid] if scatter else x_ref, dst_ref=land_ref.at[pid],
                send_sem=send_sems.at[k], recv_sem=recv_sems.at[k], device_id=to, device_id_type=MESH)
            cp.wait_send()
            cp.wait_recv()
        if own:
            mine.wait()

    return pl.pallas_call(
        body, name=name,
        out_shape=(pltpu.HBM(src_thru.shape, src_thru.dtype), pltpu.HBM(land_thru.shape, land_thru.dtype)),
        in_specs=(_HBM, _HBM, _SEM, _SEM) + (pl.BlockSpec(memory_space=pl.ANY),) * len(after),
        out_specs=(_HBM, _HBM),
        input_output_aliases={0: 0, 1: 1},
        scratch_shapes=[pltpu.VMEM(blk, land_thru.dtype), pltpu.SemaphoreType.DMA((2,))] if own else [],
        compiler_params=pltpu.CompilerParams(has_side_effects=_EFFECT, vmem_limit_bytes=VMEM_LIMIT),
    )(src_thru, land_thru, send_sems, recv_sems, *after)


def _staged_copy(src_ref, dst_ref, buf, sems):
    leg = pltpu.make_async_copy(src_ref, buf, sems.at[0])
    leg.start()
    leg.wait()
    leg = pltpu.make_async_copy(buf, dst_ref, sems.at[1])
    leg.start()
    return leg


def _gather_finish(block, land, name):
    def body(x_ref, land_ref, out_ref, token, send_sems, recv_sems, buf, local_sems):
        x, y, c = lax.axis_index("x"), lax.axis_index("y"), lax.axis_index("c")
        my, sib_id, sibling = 4 * x + 2 * y + c, 4 * x + 2 * y + 1 - c, (x, y, 1 - c)
        token[...] = jnp.zeros_like(token)

        def copy(k, slot, src=None):
            return pltpu.make_async_remote_copy(
                src_ref=land_ref.at[slot] if src is None else src, dst_ref=out_ref.at[slot],
                send_sem=send_sems.at[k], recv_sem=recv_sems.at[k], device_id=sibling, device_id_type=MESH)

        chips = _peers_same_core(x, y, c)
        sends = [copy(0, my, src=x_ref)] + [copy(1 + k, pid) for k, pid, _ in chips]
        for cp in sends:
            cp.start()
        mine = _staged_copy(x_ref, out_ref.at[my], buf, local_sems)
        copy(0, sib_id).wait_recv()
        for k, pid, _ in chips:
            copy(1 + k, pid + 1 - 2 * c).wait_recv()
        for cp in sends:
            cp.wait_send()
        mine.wait()

    return pl.pallas_call(
        body, name=name,
        out_shape=[jax.ShapeDtypeStruct(land.shape, land.dtype), jax.ShapeDtypeStruct((8, LANES), F32)],
        in_specs=[pl.BlockSpec(memory_space=pl.ANY), pl.BlockSpec(memory_space=pl.ANY)],
        out_specs=[pl.BlockSpec(memory_space=pl.ANY), pl.BlockSpec(memory_space=pltpu.VMEM)],
        input_output_aliases={1: 0},
        scratch_shapes=[pltpu.SemaphoreType.DMA((4,)), pltpu.SemaphoreType.DMA((4,)),
                        pltpu.VMEM(block.shape, block.dtype), pltpu.SemaphoreType.DMA((2,))],
        compiler_params=pltpu.CompilerParams(vmem_limit_bytes=VMEM_LIMIT),
    )(block, land)


def _forward_start(block, land, name):
    def body(x_ref, land_ref, send_sems, recv_sems, x_thru, land_thru, token):
        x, y, c = lax.axis_index("x"), lax.axis_index("y"), lax.axis_index("c")
        my, sibling = 4 * x + 2 * y + c, (x, y, 1 - c)
        slots = [(0, my, x_ref)] + [(1 + k, pid, land_ref.at[pid]) for k, pid, _ in _peers_same_core(x, y, c)]
        for k, slot, src in slots:
            pltpu.make_async_remote_copy(
                src_ref=src, dst_ref=land_ref.at[slot], send_sem=send_sems.at[k], recv_sem=recv_sems.at[k],
                device_id=sibling, device_id_type=MESH).start()
        token[...] = jnp.zeros_like(token)

    return pl.pallas_call(
        body, name=name,
        out_shape=(pltpu.SemaphoreType.DMA((4,)), pltpu.SemaphoreType.DMA((4,)),
                   pltpu.HBM(block.shape, block.dtype), pltpu.HBM(land.shape, land.dtype),
                   jax.ShapeDtypeStruct((8, LANES), F32)),
        in_specs=(_HBM, _HBM),
        out_specs=(_SEM, _SEM, _HBM, _HBM, pl.BlockSpec(memory_space=pltpu.VMEM)),
        input_output_aliases={0: 2, 1: 3},
        compiler_params=pltpu.CompilerParams(has_side_effects=_EFFECT),
    )(block, land)


def _forward_wait(handles, after, name):
    send_sems, recv_sems, block_thru, land_thru, _ = handles

    def body(x_ref, land_ref, send_sems, recv_sems, after_ref, x_dead, got_ref, buf, local_sems):
        x, y, c = lax.axis_index("x"), lax.axis_index("y"), lax.axis_index("c")
        my, sib_id, sibling = 4 * x + 2 * y + c, 4 * x + 2 * y + 1 - c, (x, y, 1 - c)
        mine = _staged_copy(x_ref, land_ref.at[my], buf, local_sems)
        slots = [(0, my, sib_id)] + [(1 + k, pid, pid + 1 - 2 * c) for k, pid, _ in _peers_same_core(x, y, c)]
        for k, sent, got in slots:
            cp = pltpu.make_async_remote_copy(
                src_ref=land_ref.at[sent], dst_ref=land_ref.at[got], send_sem=send_sems.at[k],
                recv_sem=recv_sems.at[k], device_id=sibling, device_id_type=MESH)
            cp.wait_send()
            cp.wait_recv()
        mine.wait()

    return pl.pallas_call(
        body, name=name,
        out_shape=(pltpu.HBM(block_thru.shape, block_thru.dtype), pltpu.HBM(land_thru.shape, land_thru.dtype)),
        in_specs=(_HBM, _HBM, _SEM, _SEM, pl.BlockSpec(memory_space=pl.ANY)),
        out_specs=(_HBM, _HBM),
        input_output_aliases={0: 0, 1: 1},
        scratch_shapes=[pltpu.VMEM(block_thru.shape, block_thru.dtype), pltpu.SemaphoreType.DMA((2,))],
        compiler_params=pltpu.CompilerParams(has_side_effects=_EFFECT, vmem_limit_bytes=VMEM_LIMIT),
    )(block_thru, land_thru, send_sems, recv_sems, after)[1]


def _after(v, token):
    return v + token[0, 0].astype(v.dtype)


def _dsilu(x, s):
    return s * (1.0 + x * (1.0 - s))


def _log1p(x):
    u = 1.0 + x
    d = u - 1.0
    return jnp.where(d == 0.0, x, jnp.log(u) * (x / jnp.where(d == 0.0, 1.0, d)))


def _softplus_neg(lam):
    return jnp.maximum(-lam, 0.0) + _log1p(jnp.exp(-jnp.abs(lam)))


def _neg_expm1(y, exp_y):
    poly = -y * (1.0 + y * (0.5 + y * (1.0 / 6.0 + y * (1.0 / 24.0))))
    return jnp.where(y > -0.05, poly, 1.0 - exp_y)


def _sigmoid(x):
    return 0.5 * jnp.tanh(0.5 * x) + 0.5


def _shift_dn(cur, prev, k):
    ext = jnp.concatenate([prev, cur], axis=0)
    return pltpu.roll(ext, k, 0)[HALO:, :]


def _shift_up(cur, nxt, k):
    n = cur.shape[0]
    ext = jnp.concatenate([cur, nxt], axis=0)
    return pltpu.roll(ext, n + HALO - k, 0)[:n, :]


def _scan_fwd(a, b, h_prev):
    groups = a.shape[0] // SUBLANES
    a3 = a.reshape(groups, SUBLANES, LANES)
    b3 = b.reshape(groups, SUBLANES, LANES)
    row = lax.broadcasted_iota(jnp.int32, a3.shape, 1)
    k = 1
    while k < SUBLANES:
        a_sh = jnp.where(row >= k, pltpu.roll(a3, k, 1), 1.0)
        b_sh = jnp.where(row >= k, pltpu.roll(b3, k, 1), 0.0)
        b3 = a3 * b_sh + b3
        a3 = a3 * a_sh
        k *= 2
    carry = h_prev[HALO - 1:HALO, :]
    out = []
    for i in range(groups):
        hg = b3[i] + a3[i] * carry
        out.append(hg)
        carry = hg[SUBLANES - 1:SUBLANES, :]
    return jnp.concatenate(out, axis=0)


def _scan_rev(a_next, g, lam_next):
    groups = g.shape[0] // SUBLANES
    a3 = a_next.reshape(groups, SUBLANES, LANES)
    g3 = g.reshape(groups, SUBLANES, LANES)
    row = lax.broadcasted_iota(jnp.int32, a3.shape, 1)
    k = 1
    while k < SUBLANES:
        ok = row < SUBLANES - k
        a_sh = jnp.where(ok, pltpu.roll(a3, SUBLANES - k, 1), 1.0)
        g_sh = jnp.where(ok, pltpu.roll(g3, SUBLANES - k, 1), 0.0)
        g3 = g3 + a3 * g_sh
        a3 = a3 * a_sh
        k *= 2
    carry = lam_next[0:1, :]
    out = [None] * groups
    for i in reversed(range(groups)):
        lg = g3[i] + a3[i] * carry
        out[i] = lg
        carry = lg[0:1, :]
    return jnp.concatenate(out, axis=0)


def _rowsum(v):
    return jnp.sum(v, axis=0, keepdims=True)


def _dot(a, b):
    return jnp.dot(a, b, preferred_element_type=F32)


def _dot_nt(a, b):
    return lax.dot_general(a, b, (((1,), (1,)), ((), ())), preferred_element_type=F32)


def _dot_tn(a, b):
    return lax.dot_general(a, b, (((0,), (0,)), ((), ())), preferred_element_type=F32)


class _MixerWeights:
    def __init__(self, caw_ref, sw_ref, sb_ref, lcw_ref, lcb_ref, wa_ref, wx_ref, ba_ref, bx_ref, lam_ref):
        self.caw = [caw_ref[j:j + 1, :] for j in range(3)]
        self.lcw = [lcw_ref[j:j + 1, :] for j in range(4)]
        self.lcb = lcb_ref[...]
        row = lax.broadcasted_iota(jnp.int32, (CHUNK, CHUNK), 0)
        col = lax.broadcasted_iota(jnp.int32, (CHUNK, CHUNK), 1)
        self.tril = col <= row
        self.sw = jnp.where(self.tril, sw_ref[...], 0.0).astype(BF16)
        self.sb = sb_ref[...]
        self.wa = wa_ref[...]
        self.wx = wx_ref[...]
        self.ba = ba_ref[...]
        self.bx = bx_ref[...]
        lam = lam_ref[...]
        self.neg_c_sp = -LRU_C * _softplus_neg(lam)
        self.dsp_dlam = -_sigmoid(-lam)


def _mixer_a(ld, ldp, w):
    t = {}
    a_x, a_c = ld(AX), ld(AC)
    t["a_x"], t["a_c"], t["a_b"], t["a_z"] = a_x, a_c, ld(AB), ld(AZ)
    ca = a_c * a_x
    ca_p = ldp(AC) * ldp(AX)
    t["ca"], t["ca1"], t["ca2"] = ca, _shift_dn(ca, ca_p, 1), _shift_dn(ca, ca_p, 2)
    t["cv"] = w.caw[2] * ca + w.caw[1] * t["ca1"] + w.caw[0] * t["ca2"]
    t["sa"] = _sigmoid(t["a_z"])
    t["silu_az"] = t["a_z"] * t["sa"]
    t["y_a"] = t["silu_az"] * t["a_b"] * t["cv"]
    return t


def _mixer_b(ld, w):
    t = {}
    v = ld(SV)
    vc = v - jnp.mean(v, axis=1, keepdims=True)
    t["rstd"] = lax.rsqrt(jnp.mean(vc * vc, axis=1, keepdims=True) + EPS)
    t["vn"] = vc * t["rstd"]
    t["z"] = _dot(w.sw, t["vn"].astype(BF16)) + w.sb
    t["s_u"], t["s_z"] = ld(SU), ld(SZ)
    t["ss"] = _sigmoid(t["s_z"])
    t["silu_sz"] = t["s_z"] * t["ss"]
    t["y_s"] = t["silu_sz"] * t["s_u"] * t["z"]
    return t


def _mixer_c(ld, ldp, w, backward):
    t = {}
    r_x, r_xp = ld(RX), ldp(RX)
    t["rx"] = [_shift_dn(r_x, r_xp, 3), _shift_dn(r_x, r_xp, 2), _shift_dn(r_x, r_xp, 1), r_x]
    xc = w.lcb + w.lcw[0] * t["rx"][0] + w.lcw[1] * t["rx"][1] + w.lcw[2] * t["rx"][2] + w.lcw[3] * r_x
    t["xc"] = xc
    xcb = xc.astype(BF16)
    t["r"] = _sigmoid(_dot(xcb, w.wa) + w.ba)
    t["i"] = _sigmoid(_dot(xcb, w.wx) + w.bx)
    la = t["r"] * w.neg_c_sp
    t["a"] = jnp.exp(la)
    t["em"] = _neg_expm1(2.0 * la, t["a"] * t["a"])
    if backward:
        t["inv_mult"] = lax.rsqrt(t["em"])
        t["mult"] = t["em"] * t["inv_mult"]
    else:
        t["mult"] = jnp.sqrt(t["em"])
    t["b"] = t["mult"] * (t["i"] * xc)
    t["r_z"] = ld(RZ)
    t["sr"] = _sigmoid(t["r_z"])
    t["silu_rz"] = t["r_z"] * t["sr"]
    return t


def _mixer_pre_scan(ld, ldp, w, backward):
    t = {**_mixer_a(ld, ldp, w), **_mixer_b(ld, w), **_mixer_c(ld, ldp, w, backward)}
    t["ga"], t["gs"], t["gr"] = _sigmoid(ld(GA)), _sigmoid(ld(GS)), _sigmoid(ld(GR))
    return t


def _weight_specs(n_cb_axis):
    def at(fn):
        return lambda *g: fn(g[n_cb_axis])
    return [
        pl.BlockSpec((3, LANES), at(lambda cb: (0, cb))),
        pl.BlockSpec((None, CHUNK, CHUNK), at(lambda cb: (cb, 0, 0))),
        pl.BlockSpec((None, CHUNK, LANES), at(lambda cb: (cb, 0, 0))),
        pl.BlockSpec((4, LANES), at(lambda cb: (0, cb))),
        pl.BlockSpec((1, LANES), at(lambda cb: (0, cb))),
        pl.BlockSpec((None, LANES, LANES), at(lambda cb: (cb, 0, 0))),
        pl.BlockSpec((None, LANES, LANES), at(lambda cb: (cb, 0, 0))),
        pl.BlockSpec((1, LANES), at(lambda cb: (0, cb))),
        pl.BlockSpec((1, LANES), at(lambda cb: (0, cb))),
        pl.BlockSpec((1, LANES), at(lambda cb: (0, cb))),
    ]


def _chunk_loaders(p_ref, c):
    r0 = pl.multiple_of(c * CHUNK, CHUNK)
    rp = pl.multiple_of(jnp.maximum(c * CHUNK - HALO, 0), HALO)

    def ld(j):
        return p_ref[j, pl.ds(r0, CHUNK), :].astype(F32)

    def ldp(j):
        return jnp.where(c > 0, p_ref[j, pl.ds(rp, HALO), :].astype(F32), 0.0)

    return r0, rp, ld, ldp


def _mixer_fwd(proj, mw):
    _, nb, s, _ = proj.shape
    n_chunks = s // CHUNK

    def body(p_ref, *refs):
        w = _MixerWeights(*refs[:10])
        merged_ref, hs_ref = refs[10:]

        def chunk(c, h_prev):
            r0, _, ld, ldp = _chunk_loaders(p_ref, c)
            t = _mixer_pre_scan(ld, ldp, w, False)
            h = _scan_fwd(t["a"], t["b"], h_prev)
            y_r = t["silu_rz"] * h
            merged = t["ga"] * t["y_a"] + t["gs"] * t["y_s"] + t["gr"] * y_r
            merged_ref[pl.ds(r0, CHUNK), :] = merged.astype(BF16)
            hs_ref[pl.ds(r0, CHUNK), :] = h
            return h[CHUNK - HALO:, :]

        lax.fori_loop(0, n_chunks, chunk, jnp.zeros((HALO, LANES), F32))

    slab = pl.BlockSpec((None, s, LANES), lambda cb, b: (b, 0, cb))
    return pl.pallas_call(
        body, name="mixer_fwd", grid=(D // LANES, nb),
        in_specs=[pl.BlockSpec((N_SEG, None, s, LANES), lambda cb, b: (0, b, 0, cb))] + _weight_specs(0),
        out_specs=[slab, slab],
        out_shape=[jax.ShapeDtypeStruct((nb, s, D), BF16), jax.ShapeDtypeStruct((nb, s, D), F32)],
        compiler_params=_params(("arbitrary", "arbitrary")),
    )(proj, *mw)


def _mixer_bwd(proj, dmerged, hs, mw):
    _, nb, s, _ = proj.shape
    n_chunks = s // CHUNK

    def body(p_ref, dm_ref, hs_ref, *refs):
        w = _MixerWeights(*refs[:10])
        dp_ref, g_caw, g_sw, g_sb, g_lcw, g_vec, g_wa, g_wx = refs[10:]

        @pl.when(pl.program_id(1) == 0)
        def _():
            for ref in (g_caw, g_sw, g_sb, g_lcw, g_vec, g_wa, g_wx):
                ref[...] = jnp.zeros_like(ref)

        def chunk(i, carry):
            dcv_n, dxc_n, lam_n, a_n = carry
            c = n_chunks - 1 - i
            r0, rp, ld, ldp = _chunk_loaders(p_ref, c)
            t = _mixer_pre_scan(ld, ldp, w, True)
            h = hs_ref[pl.ds(r0, CHUNK), :]
            h_p = jnp.where(c > 0, hs_ref[pl.ds(rp, HALO), :], 0.0)
            h_prev = _shift_dn(h, h_p, 1)
            dm = dm_ref[pl.ds(r0, CHUNK), :].astype(F32)
            y_r = t["silu_rz"] * h

            def out(j, val):
                dp_ref[j, pl.ds(r0, CHUNK), :] = val.astype(BF16)

            ga, gs, gr = t["ga"], t["gs"], t["gr"]
            out(GA, dm * t["y_a"] * ga * (1.0 - ga))
            out(GS, dm * t["y_s"] * gs * (1.0 - gs))
            out(GR, dm * y_r * gr * (1.0 - gr))

            dy_a = dm * ga
            out(AZ, dy_a * t["a_b"] * t["cv"] * _dsilu(t["a_z"], t["sa"]))
            out(AB, dy_a * t["silu_az"] * t["cv"])
            dcv = dy_a * t["silu_az"] * t["a_b"]
            dca = w.caw[2] * dcv + w.caw[1] * _shift_up(dcv, dcv_n, 1) + w.caw[0] * _shift_up(dcv, dcv_n, 2)
            out(AC, dca * t["a_x"])
            out(AX, dca * t["a_c"])
            g_caw[2:3, :] += _rowsum(dcv * t["ca"])
            g_caw[1:2, :] += _rowsum(dcv * t["ca1"])
            g_caw[0:1, :] += _rowsum(dcv * t["ca2"])

            dy_s = dm * gs
            out(SZ, dy_s * t["s_u"] * t["z"] * _dsilu(t["s_z"], t["ss"]))
            out(SU, dy_s * t["silu_sz"] * t["z"])
            dz = dy_s * t["silu_sz"] * t["s_u"]
            dzb = dz.astype(BF16)
            g_sb[...] += jnp.broadcast_to(jnp.sum(dz, axis=1, keepdims=True), (CHUNK, LANES))
            g_sw[...] += _dot_nt(dzb, t["vn"].astype(BF16))
            dvn = _dot_tn(w.sw, dzb)
            vn = t["vn"]
            out(SV, t["rstd"] * (dvn - jnp.mean(dvn, axis=1, keepdims=True)
                                 - vn * jnp.mean(dvn * vn, axis=1, keepdims=True)))

            dy_r = dm * gr
            out(RZ, dy_r * h * _dsilu(t["r_z"], t["sr"]))
            lam = _scan_rev(_shift_up(t["a"], a_n, 1), dy_r * t["silu_rz"], lam_n)
            a, r, ig, xc, mult = t["a"], t["r"], t["i"], t["xc"], t["mult"]
            d_i = lam * mult * xc
            d_mult = lam * ig * xc
            dxc = lam * mult * ig
            dla = lam * h_prev * a - d_mult * ((1.0 - t["em"]) * t["inv_mult"])
            g_vec[3:4, :] += _rowsum(dla * r) * (-LRU_C * w.dsp_dlam)
            dpr = (dla * w.neg_c_sp) * r * (1.0 - r)
            dpi = d_i * ig * (1.0 - ig)
            dprb, dpib, xcb = dpr.astype(BF16), dpi.astype(BF16), xc.astype(BF16)
            g_wa[...] += _dot_tn(xcb, dprb)
            g_wx[...] += _dot_tn(xcb, dpib)
            g_vec[1:2, :] += _rowsum(dpr)
            g_vec[2:3, :] += _rowsum(dpi)
            dxc = dxc + _dot_nt(dprb, w.wa) + _dot_nt(dpib, w.wx)
            g_vec[0:1, :] += _rowsum(dxc)
            out(RX, w.lcw[3] * dxc + w.lcw[2] * _shift_up(dxc, dxc_n, 1)
                + w.lcw[1] * _shift_up(dxc, dxc_n, 2) + w.lcw[0] * _shift_up(dxc, dxc_n, 3))
            for j in range(4):
                g_lcw[j:j + 1, :] += _rowsum(dxc * t["rx"][j])
            return dcv[:HALO, :], dxc[:HALO, :], lam[:HALO, :], a[:HALO, :]

        zero = jnp.zeros((HALO, LANES), F32)
        lax.fori_loop(0, n_chunks, chunk, (zero, zero, zero, zero))

        @pl.when(pl.program_id(1) == nb - 1)
        def _():
            g_sw[...] = jnp.where(w.tril, g_sw[...], 0.0)

    slab = lambda dt: pl.BlockSpec((None, s, LANES), lambda cb, b: (b, 0, cb))
    seg = pl.BlockSpec((N_SEG, None, s, LANES), lambda cb, b: (0, b, 0, cb))
    rows = lambda n: pl.BlockSpec((n, LANES), lambda cb, b: (0, cb))
    sq = pl.BlockSpec((None, LANES, LANES), lambda cb, b: (cb, 0, 0))
    n_cb = D // LANES
    return pl.pallas_call(
        body, name="mixer_bwd", grid=(n_cb, nb),
        in_specs=[seg, slab(BF16), slab(F32)] + _weight_specs(0),
        out_specs=[seg, rows(3), sq, sq, rows(4), rows(8), sq, sq],
        out_shape=[
            jax.ShapeDtypeStruct(proj.shape, BF16),
            jax.ShapeDtypeStruct((3, D), F32),
            jax.ShapeDtypeStruct((n_cb, CHUNK, CHUNK), F32),
            jax.ShapeDtypeStruct((n_cb, CHUNK, LANES), F32),
            jax.ShapeDtypeStruct((4, D), F32),
            jax.ShapeDtypeStruct((8, D), F32),
            jax.ShapeDtypeStruct((n_cb, LANES, LANES), F32),
            jax.ShapeDtypeStruct((n_cb, LANES, LANES), F32),
        ],
        compiler_params=_params(("arbitrary", "arbitrary")),
    )(proj, dmerged, hs, *mw)


def _row_tile(s, want):
    return want if s % want == 0 else s


def _norm_mod(x, gain, shift, scale):
    nb, s, _ = x.shape
    tm = _row_tile(s, 512)

    def body(x_ref, g_ref, sh_ref, sc_ref, h_ref):
        xv = x_ref[...]
        r = lax.rsqrt(jnp.mean(xv * xv, axis=1, keepdims=True) + EPS)
        h_ref[...] = ((xv * r) * g_ref[...] * (1.0 + sc_ref[...]) + sh_ref[...]).astype(BF16)

    tile = pl.BlockSpec((None, tm, D), lambda b, m: (b, m, 0))
    vec = pl.BlockSpec((None, 1, D), lambda b, m: (b, 0, 0))
    return pl.pallas_call(
        body, name="norm_mod", grid=(nb, s // tm),
        in_specs=[tile, pl.BlockSpec((1, D), lambda b, m: (0, 0)), vec, vec],
        out_specs=tile, out_shape=jax.ShapeDtypeStruct(x.shape, BF16),
        compiler_params=_params(("arbitrary", "arbitrary")),
    )(x, gain, shift, scale)


def _in_proj(h, wg, dep):
    nb, s, _ = h.shape

    def body(h_ref, w_ref, dep_ref, o_ref):
        o_ref[...] = _dot(h_ref[...], w_ref[...]).astype(BF16)

    return pl.pallas_call(
        body, name="in_proj", grid=(nb, N_DEV * UNITS_PER_DEV),
        in_specs=[pl.BlockSpec((None, s, D), lambda b, u: (b, 0, 0)),
                  pl.BlockSpec((None, D, UNIT), lambda b, u: (u // UNITS_PER_DEV, 0, u % UNITS_PER_DEV)),
                  pl.BlockSpec((8, LANES), lambda b, u: (0, 0))],
        out_specs=pl.BlockSpec((None, None, s, UNIT), lambda b, u: (u // 2, b, 0, u % 2)),
        out_shape=jax.ShapeDtypeStruct((N_SEG, nb, s, D), BF16),
        compiler_params=_params(("arbitrary", "arbitrary")),
    )(h, wg, dep)


def _out_proj(x, merged, wout, gate):
    nb, s, _ = x.shape
    tm = _row_tile(s, 512)

    def body(x_ref, m_ref, w_ref, g_ref, o_ref):
        o_ref[...] = x_ref[...] + g_ref[...] * _dot(m_ref[...], w_ref[...])

    tile = pl.BlockSpec((None, tm, D), lambda b, m: (b, m, 0))
    return pl.pallas_call(
        body, name="out_proj", grid=(nb, s // tm),
        in_specs=[tile, tile, pl.BlockSpec((D, D), lambda b, m: (0, 0)),
                  pl.BlockSpec((None, 1, D), lambda b, m: (b, 0, 0))],
        out_specs=tile, out_shape=jax.ShapeDtypeStruct(x.shape, F32),
        compiler_params=_params(("arbitrary", "arbitrary")),
    )(x, merged, wout, gate)


def _loss_head(x, gain, target):
    nb, s, _ = x.shape
    tm = _row_tile(s, 512)

    def body(x_ref, g_ref, t_ref, loss_ref, dx_ref, dg_ref):
        first = (pl.program_id(0) == 0) & (pl.program_id(1) == 0)
        last = (pl.program_id(0) == nb - 1) & (pl.program_id(1) == s // tm - 1)

        @pl.when(first)
        def _():
            loss_ref[...] = jnp.zeros_like(loss_ref)
            dg_ref[...] = jnp.zeros_like(dg_ref)

        xv = x_ref[...]
        r = lax.rsqrt(jnp.mean(xv * xv, axis=1, keepdims=True) + EPS)
        xn = xv * r
        g = g_ref[...]
        e = xn * g - t_ref[...]
        loss_ref[...] += _rowsum(e * e) * (0.5 / D)
        dy = e * (1.0 / D)
        dg_ref[...] += _rowsum(dy * xn)
        dxn = dy * g
        dx_ref[...] = r * (dxn - xn * jnp.mean(dxn * xn, axis=1, keepdims=True))

        @pl.when(last)
        def _():
            loss_ref[...] = jnp.broadcast_to(jnp.sum(loss_ref[...], axis=1, keepdims=True), (1, D))

    tile = pl.BlockSpec((None, tm, D), lambda b, m: (b, m, 0))
    vec = pl.BlockSpec((1, D), lambda b, m: (0, 0))
    return pl.pallas_call(
        body, name="loss_head", grid=(nb, s // tm),
        in_specs=[tile, vec, tile], out_specs=[vec, tile, vec],
        out_shape=[jax.ShapeDtypeStruct((1, D), F32), jax.ShapeDtypeStruct(x.shape, F32),
                   jax.ShapeDtypeStruct((1, D), F32)],
        compiler_params=_params(("arbitrary", "arbitrary")),
    )(x, gain, target)


def _out_proj_bwd(dxo, merged, wout, gate):
    nb, s, _ = dxo.shape
    tm = _row_tile(s, 512)

    def body(d_ref, m_ref, w_ref, g_ref, dm_ref, gw_ref, dg_ref):
        @pl.when((pl.program_id(0) == 0) & (pl.program_id(1) == 0))
        def _():
            gw_ref[...] = jnp.zeros_like(gw_ref)

        @pl.when(pl.program_id(1) == 0)
        def _():
            dg_ref[...] = jnp.zeros_like(dg_ref)

        d = d_ref[...]
        m = m_ref[...]
        wv = w_ref[...]
        dg_ref[...] += _rowsum(d * _dot(m, wv))
        dout = (d * g_ref[...]).astype(BF16)
        dm_ref[...] = _dot_nt(dout, wv).astype(BF16)
        gw_ref[...] += _dot_tn(m, dout)

    tile = pl.BlockSpec((None, tm, D), lambda b, m: (b, m, 0))
    vec = pl.BlockSpec((None, 1, D), lambda b, m: (b, 0, 0))
    full = pl.BlockSpec((D, D), lambda b, m: (0, 0))
    return pl.pallas_call(
        body, name="out_proj_bwd", grid=(nb, s // tm),
        in_specs=[tile, tile, full, vec], out_specs=[tile, full, vec],
        out_shape=[jax.ShapeDtypeStruct(dxo.shape, BF16), jax.ShapeDtypeStruct((D, D), F32),
                   jax.ShapeDtypeStruct((nb, 1, D), F32)],
        compiler_params=_params(("arbitrary", "arbitrary")),
    )(dxo, merged, wout, gate)


def _in_proj_bwd_h(dproj, wg, dep):
    _, nb, s, _ = dproj.shape
    tm = _row_tile(s, 1024)

    def body(dp0_ref, dp1_ref, w0_ref, w1_ref, w2_ref, w3_ref, dep_ref, dh_ref):
        j = pl.program_id(2)
        part = (_dot_nt(dp0_ref[...], jnp.concatenate([w0_ref[...], w1_ref[...]], axis=1))
                + _dot_nt(dp1_ref[...], jnp.concatenate([w2_ref[...], w3_ref[...]], axis=1)))

        @pl.when(j == 0)
        def _():
            dh_ref[...] = part

        @pl.when(j > 0)
        def _():
            dh_ref[...] += part

    def seg(k):
        return pl.BlockSpec((None, None, tm, D), lambda b, m, j: (2 * j + k, b, m, 0))

    def unit(k):
        return pl.BlockSpec((None, D, UNIT),
                            lambda b, m, j: ((4 * j + k) // UNITS_PER_DEV, 0, (4 * j + k) % UNITS_PER_DEV))

    return pl.pallas_call(
        body, name="in_proj_bwd_h", grid=(nb, s // tm, N_SEG // 2),
        in_specs=[seg(0), seg(1), unit(0), unit(1), unit(2), unit(3),
                  pl.BlockSpec((8, LANES), lambda b, m, j: (0, 0))],
        out_specs=pl.BlockSpec((None, tm, D), lambda b, m, j: (b, m, 0)),
        out_shape=jax.ShapeDtypeStruct((nb, s, D), F32),
        compiler_params=_params(("arbitrary", "arbitrary", "arbitrary")),
    )(dproj, dproj, wg, wg, wg, wg, dep)


def _norm_mod_bwd(dh, x, dxo, gain, scale):
    nb, s, _ = x.shape
    tm = _row_tile(s, 512)

    def body(dh_ref, x_ref, dxo_ref, g_ref, sc_ref, dx_ref, dsh_ref, dsc_ref, dg_ref):
        b, m = pl.program_id(0), pl.program_id(1)

        @pl.when((b == 0) & (m == 0))
        def _():
            dg_ref[...] = jnp.zeros_like(dg_ref)

        @pl.when(m == 0)
        def _():
            dsh_ref[...] = jnp.zeros_like(dsh_ref)
            dsc_ref[...] = jnp.zeros_like(dsc_ref)

        dh = dh_ref[...]
        xv = x_ref[...]
        r = lax.rsqrt(jnp.mean(xv * xv, axis=1, keepdims=True) + EPS)
        xn = xv * r
        g = g_ref[...]
        one_sc = 1.0 + sc_ref[...]
        dsh_ref[...] += _rowsum(dh)
        dsc_ref[...] += _rowsum(dh * (xn * g))
        dg_ref[...] += _rowsum(dh * one_sc * xn)
        dxn = dh * (g * one_sc)
        dx_ref[...] = dxo_ref[...] + r * (dxn - xn * jnp.mean(dxn * xn, axis=1, keepdims=True))

    tile = pl.BlockSpec((None, tm, D), lambda b, m: (b, m, 0))
    vec = pl.BlockSpec((None, 1, D), lambda b, m: (b, 0, 0))
    one = pl.BlockSpec((1, D), lambda b, m: (0, 0))
    return pl.pallas_call(
        body, name="norm_mod_bwd", grid=(nb, s // tm),
        in_specs=[tile, tile, tile, one, vec],
        out_specs=[tile, vec, vec, one],
        out_shape=[jax.ShapeDtypeStruct(x.shape, F32), jax.ShapeDtypeStruct((nb, 1, D), F32),
                   jax.ShapeDtypeStruct((nb, 1, D), F32), jax.ShapeDtypeStruct((1, D), F32)],
        compiler_params=_params(("arbitrary", "arbitrary")),
    )(dh, x, dxo, gain, scale)


def _in_proj_bwd_w(h, dproj, dep):
    nb, s, _ = h.shape
    tm = _row_tile(s, 2048)
    n_m = s // tm

    def body(h_ref, dp_ref, dep_ref, o_ref, acc_ref):
        b, m = pl.program_id(1), pl.program_id(2)

        @pl.when((b == 0) & (m == 0))
        def _():
            acc_ref[...] = jnp.zeros_like(acc_ref)

        acc_ref[...] += _dot_tn(h_ref[...], dp_ref[...])

        @pl.when((b == nb - 1) & (m == n_m - 1))
        def _():
            o_ref[0] = acc_ref[:, :UNIT].astype(BF16)
            o_ref[1] = acc_ref[:, UNIT:].astype(BF16)

    return pl.pallas_call(
        body, name="in_proj_bwd_w", grid=(N_SEG, nb, n_m),
        in_specs=[pl.BlockSpec((None, tm, D), lambda j, b, m: (b, m, 0)),
                  pl.BlockSpec((None, None, tm, D), lambda j, b, m: (j, b, m, 0)),
                  pl.BlockSpec((8, LANES), lambda j, b, m: (0, 0))],
        out_specs=pl.BlockSpec((2, D, UNIT), lambda j, b, m: (j, 0, 0)),
        out_shape=jax.ShapeDtypeStruct((2 * N_SEG, D, UNIT), BF16),
        scratch_shapes=[pltpu.VMEM((D, D), F32)],
        compiler_params=_params(("arbitrary", "arbitrary", "arbitrary")),
    )(h, dproj, dep)


def _mod_proj(c_all, w_mod, b_mod_mine):
    nl, _, ncol = w_mod.shape
    nbg = c_all.shape[0]

    def body(c_ref, w_ref, b_ref, o_ref):
        cv = c_ref[...]
        o_ref[...] = jnp.dot(cv * jax.nn.sigmoid(cv), w_ref[...], preferred_element_type=F32,
                             precision=lax.Precision.HIGHEST) + b_ref[...]

    return pl.pallas_call(
        body, name="mod_proj", grid=(nl,),
        in_specs=[pl.BlockSpec((nbg, D), lambda l: (0, 0)), pl.BlockSpec((None, D, ncol), lambda l: (l, 0, 0)),
                  pl.BlockSpec((None, 1, ncol), lambda l: (l, 0, 0))],
        out_specs=pl.BlockSpec((None, nbg, ncol), lambda l: (l, 0, 0)),
        out_shape=jax.ShapeDtypeStruct((nl, nbg, ncol), F32),
        compiler_params=_params(("arbitrary",)),
    )(c_all, w_mod, b_mod_mine)


def _mod_grad(c_all, dmod_all, dmod_mine):
    nl, nbg, ncol = dmod_mine.shape

    def body(c_ref, da_ref, dm_ref, gw_ref, gb_ref):
        cv = c_ref[...]
        gw_ref[...] = lax.dot_general(cv * jax.nn.sigmoid(cv), dm_ref[...], (((0,), (0,)), ((), ())),
                                      preferred_element_type=F32, precision=lax.Precision.HIGHEST)
        gb_ref[...] = _rowsum(da_ref[...])

    return pl.pallas_call(
        body, name="mod_grad", grid=(nl,),
        in_specs=[pl.BlockSpec((nbg, D), lambda l: (0, 0)), pl.BlockSpec((None, nbg, 3 * D), lambda l: (l, 0, 0)),
                  pl.BlockSpec((None, nbg, ncol), lambda l: (l, 0, 0))],
        out_specs=[pl.BlockSpec((None, D, ncol), lambda l: (l, 0, 0)),
                   pl.BlockSpec((None, 1, 3 * D), lambda l: (l, 0, 0))],
        out_shape=[jax.ShapeDtypeStruct((nl, D, ncol), F32), jax.ShapeDtypeStruct((nl, 1, 3 * D), F32)],
        compiler_params=_params(("arbitrary",)),
    )(c_all, dmod_all, dmod_mine)


def _adamw(parts, w, m, v, name, layer=None, prev=None):
    n_parts, n_u, n_r, cu = parts.shape
    assert w.shape[-2:] == (n_r, n_u * cu), (parts.shape, w.shape)
    tr = n_r
    for cand in (512, 256, 128):
        if n_r > cand and n_r % cand == 0:
            tr = cand
            break
    n_prev = 0 if prev is None else 4

    def body(p_ref, w_ref, m_ref, v_ref, *rest):
        g_ref, d_ref, nm_ref, nv_ref = rest[n_prev:]
        g = p_ref[0].astype(F32)
        for k in range(1, n_parts):
            g = g + p_ref[k].astype(F32)
        m2 = ADAM_B1 * m_ref[...] + (1.0 - ADAM_B1) * g
        v2 = ADAM_B2 * v_ref[...] + (1.0 - ADAM_B2) * (g * g)
        m_hat = m2 / (1.0 - ADAM_B1 ** ADAM_STEP)
        v_hat = v2 / (1.0 - ADAM_B2 ** ADAM_STEP)
        g_ref[...] = g
        d_ref[...] = -ADAM_LR * (m_hat / (jnp.sqrt(v_hat) + ADAM_EPS) + ADAM_WD * w_ref[...])
        nm_ref[...] = m2
        nv_ref[...] = v2

    if layer is None:
        tile = pl.BlockSpec((tr, cu), lambda u, i: (i, u))
    else:
        tile = pl.BlockSpec((None, tr, cu), lambda u, i: (layer, i, u))
    shp = jax.ShapeDtypeStruct(w.shape, F32)
    return pl.pallas_call(
        body, name=name, grid=(n_u, n_r // tr),
        in_specs=[pl.BlockSpec((n_parts, None, tr, cu), lambda u, i: (0, u, i, 0)), tile, tile, tile]
        + [pl.BlockSpec(memory_space=pl.ANY)] * n_prev,
        out_specs=[tile, tile, tile, tile], out_shape=[shp, shp, shp, shp],
        input_output_aliases={4 + k: k for k in range(n_prev)},
        compiler_params=_params(("arbitrary", "arbitrary")),
    )(parts, w, m, v, *(prev or ()))


def _gathered_cols(g, inner):
    k = len(inner)
    perm = tuple(range(1, k + 1)) + (0, k + 1)
    t = jnp.transpose(g, perm)
    return t.reshape(tuple(inner) + (g.shape[0] * g.shape[-1],))


def _pair_blocks(wh):
    z = jnp.zeros((8, 64, 64), wh.dtype)
    w2 = wh.reshape(8, 2, 64, 64)
    top = jnp.concatenate([w2[:, 0], z], axis=2)
    bot = jnp.concatenate([z, w2[:, 1]], axis=2)
    return jnp.concatenate([top, bot], axis=1).astype(BF16)


def _unpair_blocks(g):
    return jnp.stack([g[:, :64, :64], g[:, 64:, 64:]], axis=1).reshape(16, 64, 64)


FLAT_ROWS = 512


def _pack_rows(arrays, lead=0):
    parts = [a.reshape(a.shape[:lead] + (-1, LANES)) for a in arrays]
    rows = jnp.concatenate(parts, axis=lead)
    pad = [(0, 0)] * rows.ndim
    pad[lead] = (0, (-rows.shape[lead]) % FLAT_ROWS)
    return jnp.pad(rows, pad)


def kernel(x, c, norm_gain, w_mod, b_mod, w_in, w_out, conv_a_w, sgu_w, sgu_b, lru_conv_w, lru_conv_b, lru_wa, lru_ba, lru_wx, lru_bx, lru_lambda, final_gain, loss_target, m_norm_gain, m_w_mod, m_b_mod, m_w_in, m_w_out, m_conv_a_w, m_sgu_w, m_sgu_b, m_lru_conv_w, m_lru_conv_b, m_lru_wa, m_lru_ba, m_lru_wx, m_lru_bx, m_lru_lambda, m_final_gain, v_norm_gain, v_w_mod, v_b_mod, v_w_in, v_w_out, v_conv_a_w, v_sgu_w, v_sgu_b, v_lru_conv_w, v_lru_conv_b, v_lru_wa, v_lru_ba, v_lru_wx, v_lru_bx, v_lru_lambda, v_final_gain):
    nl = w_in.shape[0]
    nb, s, _ = x.shape
    me = _my_index()
    mod_cols = w_mod.shape[2]


    small = jnp.concatenate([c.reshape(-1, LANES), conv_a_w.reshape(-1, LANES), lru_conv_w.reshape(-1, LANES)])
    n_c, n_ca = nb * D // LANES, nl * 3
    n_small = small.shape[0]
    small = jnp.pad(small, ((0, (-n_small) % 8), (0, 0)))
    small_all, small_token = _all_gather(small, "gather_small")
    c_all = small_all[:, :n_c].reshape(N_DEV * nb, D)

    w_in_b = [w_in[0].astype(BF16)] + list(w_in[1:].astype(BF16))

    def start_w_in(l, dep):
        return _split_start(w_in_b[l], _peers_same_core, False, "gather_w_in_start", dep)

    ici = {0: start_w_in(0, small_token)}
    conv_a_full = _gathered_cols(small_all[:, n_c:n_c + n_ca].reshape(N_DEV, nl, 3, LANES), (nl, 3))
    lru_conv_full = _gathered_cols(small_all[:, n_c + n_ca:n_small].reshape(N_DEV, nl, 4, LANES), (nl, 4))

    b_mod_mine = lax.dynamic_slice_in_dim(b_mod, me * mod_cols, mod_cols, axis=1)[:, None, :]
    mod_mine = _mod_proj(c_all, w_mod, b_mod_mine)
    mod_all, mod_token = _all_gather(mod_mine.reshape(nl * N_DEV * nb, mod_cols), "gather_mod", dep=ici[0][4])
    mod_full = _gathered_cols(mod_all.reshape(N_DEV, nl, N_DEV * nb, mod_cols), (nl, N_DEV * nb))
    mod_loc = lax.dynamic_slice_in_dim(mod_full, me * nb, nb, axis=1)
    shift, scale, gate = [mod_loc[:, :, j * D:(j + 1) * D][:, :, None, :] for j in range(3)]

    def gathered_w_in(started, after):
        block, land = _split_wait(started, after, _peers_same_core, False, False, "gather_w_in_wait")
        return _gather_finish(block, land, "gather_w_in_finish")

    sgu_b_lanes = jnp.broadcast_to(sgu_b[..., None], sgu_b.shape + (LANES,))
    mws = []
    for l in range(nl):
        mws.append((conv_a_full[l], sgu_w[l], sgu_b_lanes[l], lru_conv_full[l], lru_conv_b[l][None, :],
                    _pair_blocks(lru_wa[l]), _pair_blocks(lru_wx[l]), lru_ba[l].reshape(1, D),
                    lru_bx[l].reshape(1, D), lru_lambda[l][None, :]))

    rep_names = ["sgu_w", "sgu_b", "lru_conv_b", "lru_wa", "lru_ba", "lru_wx", "lru_bx", "lru_lambda"]
    rep_w = dict(sgu_w=sgu_w, sgu_b=sgu_b, lru_conv_b=lru_conv_b, lru_wa=lru_wa, lru_ba=lru_ba,
                 lru_wx=lru_wx, lru_bx=lru_bx, lru_lambda=lru_lambda)
    rep_m = dict(sgu_w=m_sgu_w, sgu_b=m_sgu_b, lru_conv_b=m_lru_conv_b, lru_wa=m_lru_wa,
                 lru_ba=m_lru_ba, lru_wx=m_lru_wx, lru_bx=m_lru_bx, lru_lambda=m_lru_lambda)
    rep_v = dict(sgu_w=v_sgu_w, sgu_b=v_sgu_b, lru_conv_b=v_lru_conv_b, lru_wa=v_lru_wa,
                 lru_ba=v_lru_ba, lru_wx=v_lru_wx, lru_bx=v_lru_bx, lru_lambda=v_lru_lambda)

    rep_w_all, rep_m_all, rep_v_all = [_pack_rows([src[n] for n in rep_names], lead=1)
                                       for src in (rep_w, rep_m, rep_v)]
    early = [rep_w_all, rep_m_all, rep_v_all] + w_in_b[1:] + [a for mw in mws for a in mw]

    xs, hs_bf, projs, mergeds, states, wg = [], [], [], [], [], []
    xl = x
    d2d = {}
    wo_started = _split_start(w_out.astype(BF16).reshape(nl * (D // N_DEV), D), _peers_all, False,
                              "gather_w_out_start", mod_token)
    wo = None
    for l in range(nl):
        h = _norm_mod(xl, _after(norm_gain[l][None, :], wo_started[4]), shift[l], scale[l])
        if l == 0:
            wg_l, token = gathered_w_in(ici[0], [h] + early)
            ici[1] = start_w_in(1, token)
            dep = ici[1][4]
        else:
            wg_l = _forward_wait(d2d[l], h, "gather_w_in_d2d_wait")
            dep = d2d[l][4]
        wg.append(wg_l)
        proj = _in_proj(h, wg_l, dep)
        merged, st = _mixer_fwd(proj, mws[l])
        xs.append(xl), hs_bf.append(h), projs.append(proj), mergeds.append(merged), states.append(st)
        gate_l = gate[l]
        if l + 1 < nl:
            block, land = _split_wait(ici[l + 1], merged, _peers_same_core, False, False, "gather_w_in_wait")
            d2d[l + 1] = _forward_start(block, land, "gather_w_in_d2d_start")
            gate_l = _after(gate_l, d2d[l + 1][4])
            if l + 2 < nl:
                ici[l + 2] = start_w_in(l + 2, d2d[l + 1][4])
                gate_l = _after(gate_l, ici[l + 2][4])
        if wo is None:
            _, wo_all = _split_wait(wo_started, merged, _peers_all, False, True, "gather_w_out_wait")
            wo = jnp.transpose(wo_all.reshape(N_DEV, nl, D // N_DEV, D), (1, 0, 2, 3)).reshape(nl, D, D)
        xl = _out_proj(xl, merged, wo[l], gate_l)

    loss_row, dx, g_final = _loss_head(xl, final_gain[None, :], loss_target)
    loss = lax.psum(loss_row[0, 0], ("x", "y", "c"))

    res_big, g_conv = {}, [None] * nl
    dmods = [None] * nl

    def finish_exchange(pending, after):
        l, h_in, h_out, h_rep = pending
        _, recv = _split_wait(h_in, after, _peers_all, True, True, "scatter_w_in_wait")
        res_big["w_in"] = _adamw(recv, w_in, m_w_in, v_w_in, "adamw_w_in", l, res_big.get("w_in"))
        _, recv = _split_wait(h_out, after, _peers_all, True, True, "scatter_w_out_wait")
        res_big["w_out"] = _adamw(recv, w_out, m_w_out, v_w_out, "adamw_w_out", l, res_big.get("w_out"))
        _, recv = _split_wait(h_rep, after, _peers_all, False, True, "gather_rep_wait")
        res_big["rep"] = _adamw(recv[:, None], rep_w_all, rep_m_all, rep_v_all, "adamw_rep", l, res_big.get("rep"))

    pending = None
    g_gains = [None] * nl
    for l in reversed(range(nl)):
        dmerged, gw_out, dgate = _out_proj_bwd(dx, mergeds[l], wo[l], gate[l])
        dproj, g_caw, g_sw, g_sb, g_lcw, g_vec, g_wa, g_wx = _mixer_bwd(projs[l], dmerged, states[l], mws[l])
        g_conv[l] = (g_caw, g_lcw)
        rep_g = dict(
            sgu_w=g_sw, sgu_b=g_sb[:, :, 0], lru_conv_b=g_vec[0],
            lru_wa=_unpair_blocks(g_wa), lru_ba=g_vec[1].reshape(16, 64), lru_wx=_unpair_blocks(g_wx),
            lru_bx=g_vec[2].reshape(16, 64), lru_lambda=g_vec[3])
        rep_block = _pack_rows([rep_g[n] for n in rep_names])
        h_out = _split_start(gw_out.reshape(N_DEV, 1, D // N_DEV, D), _peers_all, True, "scatter_w_out_start")
        h_rep = _split_start(rep_block, _peers_all, False, "gather_rep_start")
        gw_in = _in_proj_bwd_w(hs_bf[l], dproj, h_out[4] + h_rep[4])
        h_in = _split_start(gw_in.reshape(N_DEV, UNITS_PER_DEV, D, UNIT), _peers_all, True, "scatter_w_in_start")
        started = (l, h_in, h_out, h_rep)
        dh = _in_proj_bwd_h(dproj, wg[l], h_in[4])
        dx, dshift, dscale, g_gain = _norm_mod_bwd(dh, xs[l], dx, norm_gain[l][None, :], scale[l])
        g_gains[l] = g_gain
        dmods[l] = jnp.concatenate([dshift, dscale, dgate], axis=2)[:, 0, :]
        if pending is not None:
            finish_exchange(pending, dx)
        pending = started

    conv_parts = jnp.concatenate(
        [jnp.stack([g_conv[l][0] for l in range(nl)]).reshape(nl * 3, N_DEV, LANES),
         jnp.stack([g_conv[l][1] for l in range(nl)]).reshape(nl * 4, N_DEV, LANES)], axis=0)
    conv_parts = jnp.transpose(conv_parts, (1, 0, 2))[:, None]
    conv_recv = _all_to_all(conv_parts, "scatter_conv")

    dmod_loc = jnp.stack(dmods).reshape(nl * nb, 3 * D)
    gain_rows = jnp.pad(jnp.concatenate(g_gains + [g_final], axis=0), ((0, (-(nl + 1)) % 8), (0, 2 * D)))
    tail_g, _ = _all_gather(jnp.concatenate([dmod_loc, gain_rows], axis=0), "gather_dmod",
                            dep=res_big["w_in"][3])
    dmod_g = tail_g[:, :nl * nb]
    gain_parts = tail_g[:, nl * nb:nl * nb + nl + 1, :D][:, None]
    gain_cat = lambda a, b: jnp.concatenate([a, b[None, :]], axis=0)
    res_gain = _adamw(gain_parts, gain_cat(norm_gain, final_gain), gain_cat(m_norm_gain, m_final_gain),
                      gain_cat(v_norm_gain, v_final_gain), "adamw_gain")
    dmod_all = jnp.transpose(dmod_g.reshape(N_DEV, nl, nb, 3 * D), (1, 0, 2, 3)).reshape(nl, N_DEV * nb, 3 * D)
    dmod_mine = lax.dynamic_slice_in_dim(dmod_all, me * mod_cols, mod_cols, axis=2)
    gw_mod, gb_mod = _mod_grad(c_all, dmod_all, dmod_mine)
    res_w_mod = _adamw(gw_mod.reshape(1, 1, nl * D, mod_cols), w_mod.reshape(nl * D, mod_cols),
                       m_w_mod.reshape(nl * D, mod_cols), v_w_mod.reshape(nl * D, mod_cols), "adamw_w_mod")
    res_w_mod = [a.reshape(nl, D, mod_cols) for a in res_w_mod]
    res_b_mod = _adamw(gb_mod.reshape(1, 1, nl, 3 * D), b_mod, m_b_mod, v_b_mod, "adamw_b_mod")
    finish_exchange(pending, res_b_mod[1])

    cat = lambda a, b: jnp.concatenate([a.reshape(nl * 3, LANES), b.reshape(nl * 4, LANES)], axis=0)
    res_conv = _adamw(conv_recv, cat(conv_a_w, lru_conv_w), cat(m_conv_a_w, m_lru_conv_w),
                      cat(v_conv_a_w, v_lru_conv_w), "adamw_conv")
    res_conv_a = [a[:nl * 3].reshape(nl, 3, LANES) for a in res_conv]
    res_lru_conv = [a[nl * 3:].reshape(nl, 4, LANES) for a in res_conv]

    res_rep = []
    for k in range(4):
        off, d = 0, {}
        for n in rep_names:
            n_rows = rep_w[n][0].size // LANES
            d[n] = res_big["rep"][k][:, off:off + n_rows].reshape(rep_w[n].shape)
            off += n_rows
        res_rep.append(d)

    def leaf(k, name):
        if name == "norm_gain":
            return res_gain[k][:nl]
        if name == "final_gain":
            return res_gain[k][nl]
        if name == "w_mod":
            return res_w_mod[k]
        if name == "b_mod":
            return res_b_mod[k]
        if name in ("w_in", "w_out"):
            return res_big[name][k]
        if name == "conv_a_w":
            return res_conv_a[k]
        if name == "lru_conv_w":
            return res_lru_conv[k]
        return res_rep[k][name]

    order = ["norm_gain", "w_mod", "b_mod", "w_in", "w_out", "conv_a_w", "sgu_w", "sgu_b", "lru_conv_w",
             "lru_conv_b", "lru_wa", "lru_ba", "lru_wx", "lru_bx", "lru_lambda", "final_gain"]
    outs = [loss, dx]
    for k in range(4):
        outs += [leaf(k, n) for n in order]
    return tuple(outs)
```

```python
import functools

import jax
import jax.numpy as jnp
from jax import lax
from jax.experimental import pallas as pl
from jax.experimental.pallas import tpu as pltpu

F32 = jnp.float32
BF16 = jnp.bfloat16

D = 1024
N_DEV = 8
N_SEG = 12
LANES = 128
SUBLANES = 8
CHUNK = 128
HALO = 16
UNIT = 512
UNITS_PER_DEV = 3
EPS = 1e-6
LRU_C = 8.0
ADAM_LR, ADAM_B1, ADAM_B2, ADAM_EPS, ADAM_WD, ADAM_STEP = 0.001, 0.9, 0.999, 1e-08, 0.01, 10
VMEM_LIMIT = 56 * 1024 * 1024

AX, AB, AC, AZ, SU, SV, SZ, RX, RZ, GA, GS, GR = range(N_SEG)
MESH = pl.DeviceIdType.MESH


def _params(sem=None):
    return pltpu.CompilerParams(dimension_semantics=sem, vmem_limit_bytes=VMEM_LIMIT)


def _my_index():
    return 4 * lax.axis_index("x") + 2 * lax.axis_index("y") + lax.axis_index("c")


def _all_gather(block, name, dep=None):
    deps = [] if dep is None else list(dep) if isinstance(dep, (list, tuple)) else [dep]

    def body(x_ref, *refs):
        out_ref, token, send_sems, recv_sems, local_sem = refs[-5:]
        x, y, c = lax.axis_index("x"), lax.axis_index("y"), lax.axis_index("c")
        me, sibling = (x, y, c), (x, y, 1 - c)
        chips = [(1 - x, y), (x, 1 - y), (1 - x, 1 - y)]
        token[...] = jnp.zeros_like(token)

        def rows(px, py, pc):
            return out_ref.at[4 * px + 2 * py + pc]

        def copy(k, blk, to, src=None):
            return pltpu.make_async_remote_copy(
                src_ref=rows(*blk) if src is None else src, dst_ref=rows(*blk),
                send_sem=send_sems.at[k], recv_sem=recv_sems.at[k], device_id=to, device_id_type=MESH)

        mine = pltpu.make_async_copy(x_ref, rows(*me), local_sem)
        mine.start()
        first = [copy(0, me, sibling, src=x_ref)]
        first += [copy(1 + j, me, (*chip, c), src=x_ref) for j, chip in enumerate(chips)]
        for cp in first:
            cp.start()
        passed = [copy(4 + j, (*chip, c), sibling) for j, chip in enumerate(chips)]
        for j, chip in enumerate(chips):
            copy(1 + j, (*chip, c), me).wait_recv()
            passed[j].start()
        copy(0, sibling, me).wait_recv()
        for j, chip in enumerate(chips):
            copy(4 + j, (*chip, 1 - c), me).wait_recv()
        for cp in first + passed:
            cp.wait_send()
        mine.wait()

    return pl.pallas_call(
        body, name=name,
        out_shape=[jax.ShapeDtypeStruct((N_DEV,) + block.shape, block.dtype), jax.ShapeDtypeStruct((8, LANES), F32)],
        in_specs=[pl.BlockSpec(memory_space=pltpu.VMEM)]
        + [pl.BlockSpec(memory_space=pl.ANY)] * len(deps),
        out_specs=[pl.BlockSpec(memory_space=pl.ANY), pl.BlockSpec(memory_space=pltpu.VMEM)],
        scratch_shapes=[pltpu.SemaphoreType.DMA((7,)), pltpu.SemaphoreType.DMA((7,)), pltpu.SemaphoreType.DMA],
    )(block, *deps)


def _all_to_all(blocks, name):
    def body(x_ref, out_ref, send_sems, recv_sems, local_sem):
        x, y, c = lax.axis_index("x"), lax.axis_index("y"), lax.axis_index("c")
        my = 4 * x + 2 * y + c
        mine = pltpu.make_async_copy(x_ref.at[my], out_ref.at[my], local_sem)
        mine.start()
        peers = []
        for r in range(1, N_DEV):
            px = 1 - x if r & 4 else x
            py = 1 - y if r & 2 else y
            pc = 1 - c if r & 1 else c
            peers.append((r - 1, 4 * px + 2 * py + pc, (px, py, pc)))

        def copy(k, src_slot, dst_slot, to):
            return pltpu.make_async_remote_copy(
                src_ref=x_ref.at[src_slot], dst_ref=out_ref.at[dst_slot],
                send_sem=send_sems.at[k], recv_sem=recv_sems.at[k], device_id=to, device_id_type=MESH)

        sends = [copy(k, pid, my, to) for k, pid, to in peers]
        for cp in sends:
            cp.start()
        for k, pid, to in peers:
            copy(k, pid, pid, to).wait_recv()
        for cp in sends:
            cp.wait_send()
        mine.wait()

    return pl.pallas_call(
        body, name=name,
        out_shape=jax.ShapeDtypeStruct(blocks.shape, blocks.dtype),
        in_specs=[pl.BlockSpec(memory_space=pltpu.VMEM)],
        out_specs=pl.BlockSpec(memory_space=pl.ANY),
        scratch_shapes=[pltpu.SemaphoreType.DMA((7,)), pltpu.SemaphoreType.DMA((7,)), pltpu.SemaphoreType.DMA],
    )(blocks)


_HBM = pl.BlockSpec(memory_space=pltpu.HBM)
_SEM = pl.BlockSpec(memory_space=pltpu.SEMAPHORE)
_EFFECT = pltpu.SideEffectType.DATAFLOW_SIDE_EFFECTING


def _peers_all(x, y, c):
    out = []
    for r in range(1, N_DEV):
        px = 1 - x if r & 4 else x
        py = 1 - y if r & 2 else y
        pc = 1 - c if r & 1 else c
        out.append((r - 1, 4 * px + 2 * py + pc, (px, py, pc)))
    return out


def _peers_same_core(x, y, c):
    return [(k, 4 * px + 2 * py + c, (px, py, c))
            for k, (px, py) in enumerate([(1 - x, y), (x, 1 - y), (1 - x, 1 - y)])]


def _split_start(src, peers_fn, scatter, name, dep=None):
    blk = src.shape[1:] if scatter else src.shape
    land_shape = (N_DEV,) + tuple(blk)
    n = len(peers_fn(0, 0, 0))
    deps = [] if dep is None else [dep]

    def body(x_ref, land_ref, *rest):
        send_sems, recv_sems, x_thru, land_thru, token = rest[len(deps):]
        x, y, c = lax.axis_index("x"), lax.axis_index("y"), lax.axis_index("c")
        my = 4 * x + 2 * y + c
        for k, pid, to in peers_fn(x, y, c):
            pltpu.make_async_remote_copy(
                src_ref=x_ref.at[pid] if scatter else x_ref, dst_ref=land_ref.at[my],
                send_sem=send_sems.at[k], recv_sem=recv_sems.at[k], device_id=to, device_id_type=MESH).start()
        token[...] = jnp.zeros_like(token)

    return pl.pallas_call(
        body, name=name,
        out_shape=(pltpu.SemaphoreType.DMA((n,)), pltpu.SemaphoreType.DMA((n,)),
                   pltpu.HBM(src.shape, src.dtype), pltpu.HBM(land_shape, src.dtype),
                   jax.ShapeDtypeStruct((8, LANES), F32)),
        in_specs=(_HBM, _HBM) + (pl.BlockSpec(memory_space=pl.ANY),) * len(deps),
        out_specs=(_SEM, _SEM, _HBM, _HBM, pl.BlockSpec(memory_space=pltpu.VMEM)),
        input_output_aliases={0: 2, 1: 3},
        compiler_params=pltpu.CompilerParams(has_side_effects=_EFFECT),
    )(pltpu.with_memory_space_constraint(src, pltpu.HBM),
      pltpu.with_memory_space_constraint(lax.empty(land_shape, src.dtype), pltpu.HBM), *deps)


def _split_wait(handles, after, peers_fn, scatter, own, name):
    send_sems, recv_sems, src_thru, land_thru, _ = handles
    blk = land_thru.shape[1:]
    after = list(after) if isinstance(after, (list, tuple)) else [after]

    def body(x_ref, land_ref, send_sems, recv_sems, *rest):
        stage = rest[len(after) + 2:]
        x, y, c = lax.axis_index("x"), lax.axis_index("y"), lax.axis_index("c")
        if own:
            my = 4 * x + 2 * y + c
            mine = _staged_copy(x_ref.at[my] if scatter else x_ref, land_ref.at[my], *stage)
        for k, pid, to in peers_fn(x, y, c):
            cp = pltpu.make_async_remote_copy(
                src_ref=x_ref.at[pid] if scatter else x_ref, dst_ref=land_ref.at[pid],
                send_sem=send_sems.at[k], recv_sem=recv_sems.at[k], device_id=to, device_id_type=MESH)
            cp.wait_send()
            cp.wait_recv()
        if own:
            mine.wait()

    return pl.pallas_call(
        body, name=name,
        out_shape=(pltpu.HBM(src_thru.shape, src_thru.dtype), pltpu.HBM(land_thru.shape, land_thru.dtype)),
        in_specs=(_HBM, _HBM, _SEM, _SEM) + (pl.BlockSpec(memory_space=pl.ANY),) * len(after),
        out_specs=(_HBM, _HBM),
        input_output_aliases={0: 0, 1: 1},
        scratch_shapes=[pltpu.VMEM(blk, land_thru.dtype), pltpu.SemaphoreType.DMA((2,))] if own else [],
        compiler_params=pltpu.CompilerParams(has_side_effects=_EFFECT, vmem_limit_bytes=VMEM_LIMIT),
    )(src_thru, land_thru, send_sems, recv_sems, *after)


def _staged_copy(src_ref, dst_ref, buf, sems):
    leg = pltpu.make_async_copy(src_ref, buf, sems.at[0])
    leg.start()
    leg.wait()
    leg = pltpu.make_async_copy(buf, dst_ref, sems.at[1])
    leg.start()
    return leg


def _gather_finish(block, land, name):
    def body(x_ref, land_ref, out_ref, token, send_sems, recv_sems, buf, local_sems):
        x, y, c = lax.axis_index("x"), lax.axis_index("y"), lax.axis_index("c")
        my, sib_id, sibling = 4 * x + 2 * y + c, 4 * x + 2 * y + 1 - c, (x, y, 1 - c)
        token[...] = jnp.zeros_like(token)

        def copy(k, slot, src=None):
            return pltpu.make_async_remote_copy(
                src_ref=land_ref.at[slot] if src is None else src, dst_ref=out_ref.at[slot],
                send_sem=send_sems.at[k], recv_sem=recv_sems.at[k], device_id=sibling, device_id_type=MESH)

        chips = _peers_same_core(x, y, c)
        sends = [copy(0, my, src=x_ref)] + [copy(1 + k, pid) for k, pid, _ in chips]
        for cp in sends:
            cp.start()
        mine = _staged_copy(x_ref, out_ref.at[my], buf, local_sems)
        copy(0, sib_id).wait_recv()
        for k, pid, _ in chips:
            copy(1 + k, pid + 1 - 2 * c).wait_recv()
        for cp in sends:
            cp.wait_send()
        mine.wait()

    return pl.pallas_call(
        body, name=name,
        out_shape=[jax.ShapeDtypeStruct(land.shape, land.dtype), jax.ShapeDtypeStruct((8, LANES), F32)],
        in_specs=[pl.BlockSpec(memory_space=pl.ANY), pl.BlockSpec(memory_space=pl.ANY)],
        out_specs=[pl.BlockSpec(memory_space=pl.ANY), pl.BlockSpec(memory_space=pltpu.VMEM)],
        input_output_aliases={1: 0},
        scratch_shapes=[pltpu.SemaphoreType.DMA((4,)), pltpu.SemaphoreType.DMA((4,)),
                        pltpu.VMEM(block.shape, block.dtype), pltpu.SemaphoreType.DMA((2,))],
        compiler_params=pltpu.CompilerParams(vmem_limit_bytes=VMEM_LIMIT),
    )(block, land)


def _forward_start(block, land, name):
    def body(x_ref, land_ref, send_sems, recv_sems, x_thru, land_thru, token):
        x, y, c = lax.axis_index("x"), lax.axis_index("y"), lax.axis_index("c")
        my, sibling = 4 * x + 2 * y + c, (x, y, 1 - c)
        slots = [(0, my, x_ref)] + [(1 + k, pid, land_ref.at[pid]) for k, pid, _ in _peers_same_core(x, y, c)]
        for k, slot, src in slots:
            pltpu.make_async_remote_copy(
                src_ref=src, dst_ref=land_ref.at[slot], send_sem=send_sems.at[k], recv_sem=recv_sems.at[k],
                device_id=sibling, device_id_type=MESH).start()
        token[...] = jnp.zeros_like(token)

    return pl.pallas_call(
        body, name=name,
        out_shape=(pltpu.SemaphoreType.DMA((4,)), pltpu.SemaphoreType.DMA((4,)),
                   pltpu.HBM(block.shape, block.dtype), pltpu.HBM(land.shape, land.dtype),
                   jax.ShapeDtypeStruct((8, LANES), F32)),
        in_specs=(_HBM, _HBM),
        out_specs=(_SEM, _SEM, _HBM, _HBM, pl.BlockSpec(memory_space=pltpu.VMEM)),
        input_output_aliases={0: 2, 1: 3},
        compiler_params=pltpu.CompilerParams(has_side_effects=_EFFECT),
    )(block, land)


def _forward_wait(handles, after, name):
    send_sems, recv_sems, block_thru, land_thru, _ = handles

    def body(x_ref, land_ref, send_sems, recv_sems, after_ref, x_dead, got_ref, buf, local_sems):
        x, y, c = lax.axis_index("x"), lax.axis_index("y"), lax.axis_index("c")
        my, sib_id, sibling = 4 * x + 2 * y + c, 4 * x + 2 * y + 1 - c, (x, y, 1 - c)
        mine = _staged_copy(x_ref, land_ref.at[my], buf, local_sems)
        slots = [(0, my, sib_id)] + [(1 + k, pid, pid + 1 - 2 * c) for k, pid, _ in _peers_same_core(x, y, c)]
        for k, sent, got in slots:
            cp = pltpu.make_async_remote_copy(
                src_ref=land_ref.at[sent], dst_ref=land_ref.at[got], send_sem=send_sems.at[k],
                recv_sem=recv_sems.at[k], device_id=sibling, device_id_type=MESH)
            cp.wait_send()
            cp.wait_recv()
        mine.wait()

    return pl.pallas_call(
        body, name=name,
        out_shape=(pltpu.HBM(block_thru.shape, block_thru.dtype), pltpu.HBM(land_thru.shape, land_thru.dtype)),
        in_specs=(_HBM, _HBM, _SEM, _SEM, pl.BlockSpec(memory_space=pl.ANY)),
        out_specs=(_HBM, _HBM),
        input_output_aliases={0: 0, 1: 1},
        scratch_shapes=[pltpu.VMEM(block_thru.shape, block_thru.dtype), pltpu.SemaphoreType.DMA((2,))],
        compiler_params=pltpu.CompilerParams(has_side_effects=_EFFECT, vmem_limit_bytes=VMEM_LIMIT),
    )(block_thru, land_thru, send_sems, recv_sems, after)[1]


def _after(v, token):
    return v + token[0, 0].astype(v.dtype)


def _dsilu(x, s):
    return s * (1.0 + x * (1.0 - s))


def _log1p(x):
    u = 1.0 + x
    d = u - 1.0
    return jnp.where(d == 0.0, x, jnp.log(u) * (x / jnp.where(d == 0.0, 1.0, d)))


def _softplus_neg(lam):
    return jnp.maximum(-lam, 0.0) + _log1p(jnp.exp(-jnp.abs(lam)))


def _neg_expm1(y, exp_y):
    poly = -y * (1.0 + y * (0.5 + y * (1.0 / 6.0 + y * (1.0 / 24.0))))
    return jnp.where(y > -0.05, poly, 1.0 - exp_y)


def _sigmoid(x):
    return 0.5 * jnp.tanh(0.5 * x) + 0.5


def _shift_dn(cur, prev, k):
    ext = jnp.concatenate([prev, cur], axis=0)
    return pltpu.roll(ext, k, 0)[HALO:, :]


def _shift_up(cur, nxt, k):
    n = cur.shape[0]
    ext = jnp.concatenate([cur, nxt], axis=0)
    return pltpu.roll(ext, n + HALO - k, 0)[:n, :]


def _scan_fwd(a, b, h_prev):
    groups = a.shape[0] // SUBLANES
    a3 = a.reshape(groups, SUBLANES, LANES)
    b3 = b.reshape(groups, SUBLANES, LANES)
    row = lax.broadcasted_iota(jnp.int32, a3.shape, 1)
    k = 1
    while k < SUBLANES:
        a_sh = jnp.where(row >= k, pltpu.roll(a3, k, 1), 1.0)
        b_sh = jnp.where(row >= k, pltpu.roll(b3, k, 1), 0.0)
        b3 = a3 * b_sh + b3
        a3 = a3 * a_sh
        k *= 2
    carry = h_prev[HALO - 1:HALO, :]
    out = []
    for i in range(groups):
        hg = b3[i] + a3[i] * carry
        out.append(hg)
        carry = hg[SUBLANES - 1:SUBLANES, :]
    return jnp.concatenate(out, axis=0)


def _scan_rev(a_next, g, lam_next):
    groups = g.shape[0] // SUBLANES
    a3 = a_next.reshape(groups, SUBLANES, LANES)
    g3 = g.reshape(groups, SUBLANES, LANES)
    row = lax.broadcasted_iota(jnp.int32, a3.shape, 1)
    k = 1
    while k < SUBLANES:
        ok = row < SUBLANES - k
        a_sh = jnp.where(ok, pltpu.roll(a3, SUBLANES - k, 1), 1.0)
        g_sh = jnp.where(ok, pltpu.roll(g3, SUBLANES - k, 1), 0.0)
        g3 = g3 + a3 * g_sh
        a3 = a3 * a_sh
        k *= 2
    carry = lam_next[0:1, :]
    out = [None] * groups
    for i in reversed(range(groups)):
        lg = g3[i] + a3[i] * carry
        out[i] = lg
        carry = lg[0:1, :]
    return jnp.concatenate(out, axis=0)


def _rowsum(v):
    return jnp.sum(v, axis=0, keepdims=True)


def _dot(a, b):
    return jnp.dot(a, b, preferred_element_type=F32)


def _dot_nt(a, b):
    return lax.dot_general(a, b, (((1,), (1,)), ((), ())), preferred_element_type=F32)


def _dot_tn(a, b):
    return lax.dot_general(a, b, (((0,), (0,)), ((), ())), preferred_element_type=F32)


class _MixerWeights:
    def __init__(self, caw_ref, sw_ref, sb_ref, lcw_ref, lcb_ref, wa_ref, wx_ref, ba_ref, bx_ref, lam_ref):
        self.caw = [caw_ref[j:j + 1, :] for j in range(3)]
        self.lcw = [lcw_ref[j:j + 1, :] for j in range(4)]
        self.lcb = lcb_ref[...]
        row = lax.broadcasted_iota(jnp.int32, (CHUNK, CHUNK), 0)
        col = lax.broadcasted_iota(jnp.int32, (CHUNK, CHUNK), 1)
        self.tril = col <= row
        self.sw = jnp.where(self.tril, sw_ref[...], 0.0).astype(BF16)
        self.sb = sb_ref[...]
        self.wa = wa_ref[...]
        self.wx = wx_ref[...]
        self.ba = ba_ref[...]
        self.bx = bx_ref[...]
        lam = lam_ref[...]
        self.neg_c_sp = -LRU_C * _softplus_neg(lam)
        self.dsp_dlam = -_sigmoid(-lam)


def _mixer_a(ld, ldp, w):
    t = {}
    a_x, a_c = ld(AX), ld(AC)
    t["a_x"], t["a_c"], t["a_b"], t["a_z"] = a_x, a_c, ld(AB), ld(AZ)
    ca = a_c * a_x
    ca_p = ldp(AC) * ldp(AX)
    t["ca"], t["ca1"], t["ca2"] = ca, _shift_dn(ca, ca_p, 1), _shift_dn(ca, ca_p, 2)
    t["cv"] = w.caw[2] * ca + w.caw[1] * t["ca1"] + w.caw[0] * t["ca2"]
    t["sa"] = _sigmoid(t["a_z"])
    t["silu_az"] = t["a_z"] * t["sa"]
    t["y_a"] = t["silu_az"] * t["a_b"] * t["cv"]
    return t


def _mixer_b(ld, w):
    t = {}
    v = ld(SV)
    vc = v - jnp.mean(v, axis=1, keepdims=True)
    t["rstd"] = lax.rsqrt(jnp.mean(vc * vc, axis=1, keepdims=True) + EPS)
    t["vn"] = vc * t["rstd"]
    t["z"] = _dot(w.sw, t["vn"].astype(BF16)) + w.sb
    t["s_u"], t["s_z"] = ld(SU), ld(SZ)
    t["ss"] = _sigmoid(t["s_z"])
    t["silu_sz"] = t["s_z"] * t["ss"]
    t["y_s"] = t["silu_sz"] * t["s_u"] * t["z"]
    return t


def _mixer_c(ld, ldp, w, backward):
    t = {}
    r_x, r_xp = ld(RX), ldp(RX)
    t["rx"] = [_shift_dn(r_x, r_xp, 3), _shift_dn(r_x, r_xp, 2), _shift_dn(r_x, r_xp, 1), r_x]
    xc = w.lcb + w.lcw[0] * t["rx"][0] + w.lcw[1] * t["rx"][1] + w.lcw[2] * t["rx"][2] + w.lcw[3] * r_x
    t["xc"] = xc
    xcb = xc.astype(BF16)
    t["r"] = _sigmoid(_dot(xcb, w.wa) + w.ba)
    t["i"] = _sigmoid(_dot(xcb, w.wx) + w.bx)
    la = t["r"] * w.neg_c_sp
    t["a"] = jnp.exp(la)
    t["em"] = _neg_expm1(2.0 * la, t["a"] * t["a"])
    if backward:
        t["inv_mult"] = lax.rsqrt(t["em"])
        t["mult"] = t["em"] * t["inv_mult"]
    else:
        t["mult"] = jnp.sqrt(t["em"])
    t["b"] = t["mult"] * (t["i"] * xc)
    t["r_z"] = ld(RZ)
    t["sr"] = _sigmoid(t["r_z"])
    t["silu_rz"] = t["r_z"] * t["sr"]
    return t


def _mixer_pre_scan(ld, ldp, w, backward):
    t = {**_mixer_a(ld, ldp, w), **_mixer_b(ld, w), **_mixer_c(ld, ldp, w, backward)}
    t["ga"], t["gs"], t["gr"] = _sigmoid(ld(GA)), _sigmoid(ld(GS)), _sigmoid(ld(GR))
    return t


def _weight_specs(n_cb_axis):
    def at(fn):
        return lambda *g: fn(g[n_cb_axis])
    return [
        pl.BlockSpec((3, LANES), at(lambda cb: (0, cb))),
        pl.BlockSpec((None, CHUNK, CHUNK), at(lambda cb: (cb, 0, 0))),
        pl.BlockSpec((None, CHUNK, LANES), at(lambda cb: (cb, 0, 0))),
        pl.BlockSpec((4, LANES), at(lambda cb: (0, cb))),
        pl.BlockSpec((1, LANES), at(lambda cb: (0, cb))),
        pl.BlockSpec((None, LANES, LANES), at(lambda cb: (cb, 0, 0))),
        pl.BlockSpec((None, LANES, LANES), at(lambda cb: (cb, 0, 0))),
        pl.BlockSpec((1, LANES), at(lambda cb: (0, cb))),
        pl.BlockSpec((1, LANES), at(lambda cb: (0, cb))),
        pl.BlockSpec((1, LANES), at(lambda cb: (0, cb))),
    ]


def _chunk_loaders(p_ref, c):
    r0 = pl.multiple_of(c * CHUNK, CHUNK)
    rp = pl.multiple_of(jnp.maximum(c * CHUNK - HALO, 0), HALO)

    def ld(j):
        return p_ref[j, pl.ds(r0, CHUNK), :].astype(F32)

    def ldp(j):
        return jnp.where(c > 0, p_ref[j, pl.ds(rp, HALO), :].astype(F32), 0.0)

    return r0, rp, ld, ldp


def _mixer_fwd(proj, mw):
    _, nb, s, _ = proj.shape
    n_chunks = s // CHUNK

    def body(p_ref, *refs):
        w = _MixerWeights(*refs[:10])
        merged_ref, hs_ref = refs[10:]

        def chunk(c, h_prev):
            r0, _, ld, ldp = _chunk_loaders(p_ref, c)
            t = _mixer_pre_scan(ld, ldp, w, False)
            h = _scan_fwd(t["a"], t["b"], h_prev)
            y_r = t["silu_rz"] * h
            merged = t["ga"] * t["y_a"] + t["gs"] * t["y_s"] + t["gr"] * y_r
            merged_ref[pl.ds(r0, CHUNK), :] = merged.astype(BF16)
            hs_ref[pl.ds(r0, CHUNK), :] = h
            return h[CHUNK - HALO:, :]

        lax.fori_loop(0, n_chunks, chunk, jnp.zeros((HALO, LANES), F32))

    slab = pl.BlockSpec((None, s, LANES), lambda cb, b: (b, 0, cb))
    return pl.pallas_call(
        body, name="mixer_fwd", grid=(D // LANES, nb),
        in_specs=[pl.BlockSpec((N_SEG, None, s, LANES), lambda cb, b: (0, b, 0, cb))] + _weight_specs(0),
        out_specs=[slab, slab],
        out_shape=[jax.ShapeDtypeStruct((nb, s, D), BF16), jax.ShapeDtypeStruct((nb, s, D), F32)],
        compiler_params=_params(("arbitrary", "arbitrary")),
    )(proj, *mw)


def _mixer_bwd(proj, dmerged, hs, mw):
    _, nb, s, _ = proj.shape
    n_chunks = s // CHUNK

    def body(p_ref, dm_ref, hs_ref, *refs):
        w = _MixerWeights(*refs[:10])
        dp_ref, g_caw, g_sw, g_sb, g_lcw, g_vec, g_wa, g_wx = refs[10:]

        @pl.when(pl.program_id(1) == 0)
        def _():
            for ref in (g_caw, g_sw, g_sb, g_lcw, g_vec, g_wa, g_wx):
                ref[...] = jnp.zeros_like(ref)

        def chunk(i, carry):
            dcv_n, dxc_n, lam_n, a_n = carry
            c = n_chunks - 1 - i
            r0, rp, ld, ldp = _chunk_loaders(p_ref, c)
            t = _mixer_pre_scan(ld, ldp, w, True)
            h = hs_ref[pl.ds(r0, CHUNK), :]
            h_p = jnp.where(c > 0, hs_ref[pl.ds(rp, HALO), :], 0.0)
            h_prev = _shift_dn(h, h_p, 1)
            dm = dm_ref[pl.ds(r0, CHUNK), :].astype(F32)
            y_r = t["silu_rz"] * h

            def out(j, val):
                dp_ref[j, pl.ds(r0, CHUNK), :] = val.astype(BF16)

            ga, gs, gr = t["ga"], t["gs"], t["gr"]
            out(GA, dm * t["y_a"] * ga * (1.0 - ga))
            out(GS, dm * t["y_s"] * gs * (1.0 - gs))
            out(GR, dm * y_r * gr * (1.0 - gr))

            dy_a = dm * ga
            out(AZ, dy_a * t["a_b"] * t["cv"] * _dsilu(t["a_z"], t["sa"]))
            out(AB, dy_a * t["silu_az"] * t["cv"])
            dcv = dy_a * t["silu_az"] * t["a_b"]
            dca = w.caw[2] * dcv + w.caw[1] * _shift_up(dcv, dcv_n, 1) + w.caw[0] * _shift_up(dcv, dcv_n, 2)
            out(AC, dca * t["a_x"])
            out(AX, dca * t["a_c"])
            g_caw[2:3, :] += _rowsum(dcv * t["ca"])
            g_caw[1:2, :] += _rowsum(dcv * t["ca1"])
            g_caw[0:1, :] += _rowsum(dcv * t["ca2"])

            dy_s = dm * gs
            out(SZ, dy_s * t["s_u"] * t["z"] * _dsilu(t["s_z"], t["ss"]))
            out(SU, dy_s * t["silu_sz"] * t["z"])
            dz = dy_s * t["silu_sz"] * t["s_u"]
            dzb = dz.astype(BF16)
            g_sb[...] += jnp.broadcast_to(jnp.sum(dz, axis=1, keepdims=True), (CHUNK, LANES))
            g_sw[...] += _dot_nt(dzb, t["vn"].astype(BF16))
            dvn = _dot_tn(w.sw, dzb)
            vn = t["vn"]
            out(SV, t["rstd"] * (dvn - jnp.mean(dvn, axis=1, keepdims=True)
                                 - vn * jnp.mean(dvn * vn, axis=1, keepdims=True)))

            dy_r = dm * gr
            out(RZ, dy_r * h * _dsilu(t["r_z"], t["sr"]))
            lam = _scan_rev(_shift_up(t["a"], a_n, 1), dy_r * t["silu_rz"], lam_n)
            a, r, ig, xc, mult = t["a"], t["r"], t["i"], t["xc"], t["mult"]
            d_i = lam * mult * xc
            d_mult = lam * ig * xc
            dxc = lam * mult * ig
            dla = lam * h_prev * a - d_mult * ((1.0 - t["em"]) * t["inv_mult"])
            g_vec[3:4, :] += _rowsum(dla * r) * (-LRU_C * w.dsp_dlam)
            dpr = (dla * w.neg_c_sp) * r * (1.0 - r)
            dpi = d_i * ig * (1.0 - ig)
            dprb, dpib, xcb = dpr.astype(BF16), dpi.astype(BF16), xc.astype(BF16)
            g_wa[...] += _dot_tn(xcb, dprb)
            g_wx[...] += _dot_tn(xcb, dpib)
            g_vec[1:2, :] += _rowsum(dpr)
            g_vec[2:3, :] += _rowsum(dpi)
            dxc = dxc + _dot_nt(dprb, w.wa) + _dot_nt(dpib, w.wx)
            g_vec[0:1, :] += _rowsum(dxc)
            out(RX, w.lcw[3] * dxc + w.lcw[2] * _shift_up(dxc, dxc_n, 1)
                + w.lcw[1] * _shift_up(dxc, dxc_n, 2) + w.lcw[0] * _shift_up(dxc, dxc_n, 3))
            for j in range(4):
                g_lcw[j:j + 1, :] += _rowsum(dxc * t["rx"][j])
            return dcv[:HALO, :], dxc[:HALO, :], lam[:HALO, :], a[:HALO, :]

        zero = jnp.zeros((HALO, LANES), F32)
        lax.fori_loop(0, n_chunks, chunk, (zero, zero, zero, zero))

        @pl.when(pl.program_id(1) == nb - 1)
        def _():
            g_sw[...] = jnp.where(w.tril, g_sw[...], 0.0)

    slab = lambda dt: pl.BlockSpec((None, s, LANES), lambda cb, b: (b, 0, cb))
    seg = pl.BlockSpec((N_SEG, None, s, LANES), lambda cb, b: (0, b, 0, cb))
    rows = lambda n: pl.BlockSpec((n, LANES), lambda cb, b: (0, cb))
    sq = pl.BlockSpec((None, LANES, LANES), lambda cb, b: (cb, 0, 0))
    n_cb = D // LANES
    return pl.pallas_call(
        body, name="mixer_bwd", grid=(n_cb, nb),
        in_specs=[seg, slab(BF16), slab(F32)] + _weight_specs(0),
        out_specs=[seg, rows(3), sq, sq, rows(4), rows(8), sq, sq],
        out_shape=[
            jax.ShapeDtypeStruct(proj.shape, BF16),
            jax.ShapeDtypeStruct((3, D), F32),
            jax.ShapeDtypeStruct((n_cb, CHUNK, CHUNK), F32),
            jax.ShapeDtypeStruct((n_cb, CHUNK, LANES), F32),
            jax.ShapeDtypeStruct((4, D), F32),
            jax.ShapeDtypeStruct((8, D), F32),
            jax.ShapeDtypeStruct((n_cb, LANES, LANES), F32),
            jax.ShapeDtypeStruct((n_cb, LANES, LANES), F32),
        ],
        compiler_params=_params(("arbitrary", "arbitrary")),
    )(proj, dmerged, hs, *mw)


def _row_tile(s, want):
    return want if s % want == 0 else s


def _norm_mod(x, gain, shift, scale):
    nb, s, _ = x.shape
    tm = _row_tile(s, 512)

    def body(x_ref, g_ref, sh_ref, sc_ref, h_ref):
        xv = x_ref[...]
        r = lax.rsqrt(jnp.mean(xv * xv, axis=1, keepdims=True) + EPS)
        h_ref[...] = ((xv * r) * g_ref[...] * (1.0 + sc_ref[...]) + sh_ref[...]).astype(BF16)

    tile = pl.BlockSpec((None, tm, D), lambda b, m: (b, m, 0))
    vec = pl.BlockSpec((None, 1, D), lambda b, m: (b, 0, 0))
    return pl.pallas_call(
        body, name="norm_mod", grid=(nb, s // tm),
        in_specs=[tile, pl.BlockSpec((1, D), lambda b, m: (0, 0)), vec, vec],
        out_specs=tile, out_shape=jax.ShapeDtypeStruct(x.shape, BF16),
        compiler_params=_params(("arbitrary", "arbitrary")),
    )(x, gain, shift, scale)


def _in_proj(h, wg, dep):
    nb, s, _ = h.shape

    def body(h_ref, w_ref, dep_ref, o_ref):
        o_ref[...] = _dot(h_ref[...], w_ref[...]).astype(BF16)

    return pl.pallas_call(
        body, name="in_proj", grid=(nb, N_DEV * UNITS_PER_DEV),
        in_specs=[pl.BlockSpec((None, s, D), lambda b, u: (b, 0, 0)),
                  pl.BlockSpec((None, D, UNIT), lambda b, u: (u // UNITS_PER_DEV, 0, u % UNITS_PER_DEV)),
                  pl.BlockSpec((8, LANES), lambda b, u: (0, 0))],
        out_specs=pl.BlockSpec((None, None, s, UNIT), lambda b, u: (u // 2, b, 0, u % 2)),
        out_shape=jax.ShapeDtypeStruct((N_SEG, nb, s, D), BF16),
        compiler_params=_params(("arbitrary", "arbitrary")),
    )(h, wg, dep)


def _out_proj(x, merged, wout, gate):
    nb, s, _ = x.shape
    tm = _row_tile(s, 512)

    def body(x_ref, m_ref, w_ref, g_ref, o_ref):
        o_ref[...] = x_ref[...] + g_ref[...] * _dot(m_ref[...], w_ref[...])

    tile = pl.BlockSpec((None, tm, D), lambda b, m: (b, m, 0))
    return pl.pallas_call(
        body, name="out_proj", grid=(nb, s // tm),
        in_specs=[tile, tile, pl.BlockSpec((D, D), lambda b, m: (0, 0)),
                  pl.BlockSpec((None, 1, D), lambda b, m: (b, 0, 0))],
        out_specs=tile, out_shape=jax.ShapeDtypeStruct(x.shape, F32),
        compiler_params=_params(("arbitrary", "arbitrary")),
    )(x, merged, wout, gate)


def _loss_head(x, gain, target):
    nb, s, _ = x.shape
    tm = _row_tile(s, 512)

    def body(x_ref, g_ref, t_ref, loss_ref, dx_ref, dg_ref):
        first = (pl.program_id(0) == 0) & (pl.program_id(1) == 0)
        last = (pl.program_id(0) == nb - 1) & (pl.program_id(1) == s // tm - 1)

        @pl.when(first)
        def _():
            loss_ref[...] = jnp.zeros_like(loss_ref)
            dg_ref[...] = jnp.zeros_like(dg_ref)

        xv = x_ref[...]
        r = lax.rsqrt(jnp.mean(xv * xv, axis=1, keepdims=True) + EPS)
        xn = xv * r
        g = g_ref[...]
        e = xn * g - t_ref[...]
        loss_ref[...] += _rowsum(e * e) * (0.5 / D)
        dy = e * (1.0 / D)
        dg_ref[...] += _rowsum(dy * xn)
        dxn = dy * g
        dx_ref[...] = r * (dxn - xn * jnp.mean(dxn * xn, axis=1, keepdims=True))

        @pl.when(last)
        def _():
            loss_ref[...] = jnp.broadcast_to(jnp.sum(loss_ref[...], axis=1, keepdims=True), (1, D))

    tile = pl.BlockSpec((None, tm, D), lambda b, m: (b, m, 0))
    vec = pl.BlockSpec((1, D), lambda b, m: (0, 0))
    return pl.pallas_call(
        body, name="loss_head", grid=(nb, s // tm),
        in_specs=[tile, vec, tile], out_specs=[vec, tile, vec],
        out_shape=[jax.ShapeDtypeStruct((1, D), F32), jax.ShapeDtypeStruct(x.shape, F32),
                   jax.ShapeDtypeStruct((1, D), F32)],
        compiler_params=_params(("arbitrary", "arbitrary")),
    )(x, gain, target)


def _out_proj_bwd(dxo, merged, wout, gate):
    nb, s, _ = dxo.shape
    tm = _row_tile(s, 512)

    def body(d_ref, m_ref, w_ref, g_ref, dm_ref, gw_ref, dg_ref):
        @pl.when((pl.program_id(0) == 0) & (pl.program_id(1) == 0))
        def _():
            gw_ref[...] = jnp.zeros_like(gw_ref)

        @pl.when(pl.program_id(1) == 0)
        def _():
            dg_ref[...] = jnp.zeros_like(dg_ref)

        d = d_ref[...]
        m = m_ref[...]
        wv = w_ref[...]
        dg_ref[...] += _rowsum(d * _dot(m, wv))
        dout = (d * g_ref[...]).astype(BF16)
        dm_ref[...] = _dot_nt(dout, wv).astype(BF16)
        gw_ref[...] += _dot_tn(m, dout)

    tile = pl.BlockSpec((None, tm, D), lambda b, m: (b, m, 0))
    vec = pl.BlockSpec((None, 1, D), lambda b, m: (b, 0, 0))
    full = pl.BlockSpec((D, D), lambda b, m: (0, 0))
    return pl.pallas_call(
        body, name="out_proj_bwd", grid=(nb, s // tm),
        in_specs=[tile, tile, full, vec], out_specs=[tile, full, vec],
        out_shape=[jax.ShapeDtypeStruct(dxo.shape, BF16), jax.ShapeDtypeStruct((D, D), F32),
                   jax.ShapeDtypeStruct((nb, 1, D), F32)],
        compiler_params=_params(("arbitrary", "arbitrary")),
    )(dxo, merged, wout, gate)


def _in_proj_bwd_h(dproj, wg, dep):
    _, nb, s, _ = dproj.shape
    tm = _row_tile(s, 1024)

    def body(dp0_ref, dp1_ref, w0_ref, w1_ref, w2_ref, w3_ref, dep_ref, dh_ref):
        j = pl.program_id(2)
        part = (_dot_nt(dp0_ref[...], jnp.concatenate([w0_ref[...], w1_ref[...]], axis=1))
                + _dot_nt(dp1_ref[...], jnp.concatenate([w2_ref[...], w3_ref[...]], axis=1)))

        @pl.when(j == 0)
        def _():
            dh_ref[...] = part

        @pl.when(j > 0)
        def _():
            dh_ref[...] += part

    def seg(k):
        return pl.BlockSpec((None, None, tm, D), lambda b, m, j: (2 * j + k, b, m, 0))

    def unit(k):
        return pl.BlockSpec((None, D, UNIT),
                            lambda b, m, j: ((4 * j + k) // UNITS_PER_DEV, 0, (4 * j + k) % UNITS_PER_DEV))

    return pl.pallas_call(
        body, name="in_proj_bwd_h", grid=(nb, s // tm, N_SEG // 2),
        in_specs=[seg(0), seg(1), unit(0), unit(1), unit(2), unit(3),
                  pl.BlockSpec((8, LANES), lambda b, m, j: (0, 0))],
        out_specs=pl.BlockSpec((None, tm, D), lambda b, m, j: (b, m, 0)),
        out_shape=jax.ShapeDtypeStruct((nb, s, D), F32),
        compiler_params=_params(("arbitrary", "arbitrary", "arbitrary")),
    )(dproj, dproj, wg, wg, wg, wg, dep)


def _norm_mod_bwd(dh, x, dxo, gain, scale):
    nb, s, _ = x.shape
    tm = _row_tile(s, 512)

    def body(dh_ref, x_ref, dxo_ref, g_ref, sc_ref, dx_ref, dsh_ref, dsc_ref, dg_ref):
        b, m = pl.program_id(0), pl.program_id(1)

        @pl.when((b == 0) & (m == 0))
        def _():
            dg_ref[...] = jnp.zeros_like(dg_ref)

        @pl.when(m == 0)
        def _():
            dsh_ref[...] = jnp.zeros_like(dsh_ref)
            dsc_ref[...] = jnp.zeros_like(dsc_ref)

        dh = dh_ref[...]
        xv = x_ref[...]
        r = lax.rsqrt(jnp.mean(xv * xv, axis=1, keepdims=True) + EPS)
        xn = xv * r
        g = g_ref[...]
        one_sc = 1.0 + sc_ref[...]
        dsh_ref[...] += _rowsum(dh)
        dsc_ref[...] += _rowsum(dh * (xn * g))
        dg_ref[...] += _rowsum(dh * one_sc * xn)
        dxn = dh * (g * one_sc)
        dx_ref[...] = dxo_ref[...] + r * (dxn - xn * jnp.mean(dxn * xn, axis=1, keepdims=True))

    tile = pl.BlockSpec((None, tm, D), lambda b, m: (b, m, 0))
    vec = pl.BlockSpec((None, 1, D), lambda b, m: (b, 0, 0))
    one = pl.BlockSpec((1, D), lambda b, m: (0, 0))
    return pl.pallas_call(
        body, name="norm_mod_bwd", grid=(nb, s // tm),
        in_specs=[tile, tile, tile, one, vec],
        out_specs=[tile, vec, vec, one],
        out_shape=[jax.ShapeDtypeStruct(x.shape, F32), jax.ShapeDtypeStruct((nb, 1, D), F32),
                   jax.ShapeDtypeStruct((nb, 1, D), F32), jax.ShapeDtypeStruct((1, D), F32)],
        compiler_params=_params(("arbitrary", "arbitrary")),
    )(dh, x, dxo, gain, scale)


def _in_proj_bwd_w(h, dproj, dep):
    nb, s, _ = h.shape
    tm = _row_tile(s, 2048)
    n_m = s // tm

    def body(h_ref, dp_ref, dep_ref, o_ref, acc_ref):
        b, m = pl.program_id(1), pl.program_id(2)

        @pl.when((b == 0) & (m == 0))
        def _():
            acc_ref[...] = jnp.zeros_like(acc_ref)

        acc_ref[...] += _dot_tn(h_ref[...], dp_ref[...])

        @pl.when((b == nb - 1) & (m == n_m - 1))
        def _():
            o_ref[0] = acc_ref[:, :UNIT].astype(BF16)
            o_ref[1] = acc_ref[:, UNIT:].astype(BF16)

    return pl.pallas_call(
        body, name="in_proj_bwd_w", grid=(N_SEG, nb, n_m),
        in_specs=[pl.BlockSpec((None, tm, D), lambda j, b, m: (b, m, 0)),
                  pl.BlockSpec((None, None, tm, D), lambda j, b, m: (j, b, m, 0)),
                  pl.BlockSpec((8, LANES), lambda j, b, m: (0, 0))],
        out_specs=pl.BlockSpec((2, D, UNIT), lambda j, b, m: (j, 0, 0)),
        out_shape=jax.ShapeDtypeStruct((2 * N_SEG, D, UNIT), BF16),
        scratch_shapes=[pltpu.VMEM((D, D), F32)],
        compiler_params=_params(("arbitrary", "arbitrary", "arbitrary")),
    )(h, dproj, dep)


def _mod_proj(c_all, w_mod, b_mod_mine):
    nl, _, ncol = w_mod.shape
    nbg = c_all.shape[0]

    def body(c_ref, w_ref, b_ref, o_ref):
        cv = c_ref[...]
        o_ref[...] = jnp.dot(cv * jax.nn.sigmoid(cv), w_ref[...], preferred_element_type=F32,
                             precision=lax.Precision.HIGHEST) + b_ref[...]

    return pl.pallas_call(
        body, name="mod_proj", grid=(nl,),
        in_specs=[pl.BlockSpec((nbg, D), lambda l: (0, 0)), pl.BlockSpec((None, D, ncol), lambda l: (l, 0, 0)),
                  pl.BlockSpec((None, 1, ncol), lambda l: (l, 0, 0))],
        out_specs=pl.BlockSpec((None, nbg, ncol), lambda l: (l, 0, 0)),
        out_shape=jax.ShapeDtypeStruct((nl, nbg, ncol), F32),
        compiler_params=_params(("arbitrary",)),
    )(c_all, w_mod, b_mod_mine)


def _mod_grad(c_all, dmod_all, dmod_mine):
    nl, nbg, ncol = dmod_mine.shape

    def body(c_ref, da_ref, dm_ref, gw_ref, gb_ref):
        cv = c_ref[...]
        gw_ref[...] = lax.dot_general(cv * jax.nn.sigmoid(cv), dm_ref[...], (((0,), (0,)), ((), ())),
                                      preferred_element_type=F32, precision=lax.Precision.HIGHEST)
        gb_ref[...] = _rowsum(da_ref[...])

    return pl.pallas_call(
        body, name="mod_grad", grid=(nl,),
        in_specs=[pl.BlockSpec((nbg, D), lambda l: (0, 0)), pl.BlockSpec((None, nbg, 3 * D), lambda l: (l, 0, 0)),
                  pl.BlockSpec((None, nbg, ncol), lambda l: (l, 0, 0))],
        out_specs=[pl.BlockSpec((None, D, ncol), lambda l: (l, 0, 0)),
                   pl.BlockSpec((None, 1, 3 * D), lambda l: (l, 0, 0))],
        out_shape=[jax.ShapeDtypeStruct((nl, D, ncol), F32), jax.ShapeDtypeStruct((nl, 1, 3 * D), F32)],
        compiler_params=_params(("arbitrary",)),
    )(c_all, dmod_all, dmod_mine)


def _adamw(parts, w, m, v, name, layer=None, prev=None):
    n_parts, n_u, n_r, cu = parts.shape
    assert w.shape[-2:] == (n_r, n_u * cu), (parts.shape, w.shape)
    tr = n_r
    for cand in (512, 256, 128):
        if n_r > cand and n_r % cand == 0:
            tr = cand
            break
    n_prev = 0 if prev is None else 4

    def body(p_ref, w_ref, m_ref, v_ref, *rest):
        g_ref, d_ref, nm_ref, nv_ref = rest[n_prev:]
        g = p_ref[0].astype(F32)
        for k in range(1, n_parts):
            g = g + p_ref[k].astype(F32)
        m2 = ADAM_B1 * m_ref[...] + (1.0 - ADAM_B1) * g
        v2 = ADAM_B2 * v_ref[...] + (1.0 - ADAM_B2) * (g * g)
        m_hat = m2 / (1.0 - ADAM_B1 ** ADAM_STEP)
        v_hat = v2 / (1.0 - ADAM_B2 ** ADAM_STEP)
        g_ref[...] = g
        d_ref[...] = -ADAM_LR * (m_hat / (jnp.sqrt(v_hat) + ADAM_EPS) + ADAM_WD * w_ref[...])
        nm_ref[...] = m2
        nv_ref[...] = v2

    if layer is None:
        tile = pl.BlockSpec((tr, cu), lambda u, i: (i, u))
    else:
        tile = pl.BlockSpec((None, tr, cu), lambda u, i: (layer, i, u))
    shp = jax.ShapeDtypeStruct(w.shape, F32)
    return pl.pallas_call(
        body, name=name, grid=(n_u, n_r // tr),
        in_specs=[pl.BlockSpec((n_parts, None, tr, cu), lambda u, i: (0, u, i, 0)), tile, tile, tile]
        + [pl.BlockSpec(memory_space=pl.ANY)] * n_prev,
        out_specs=[tile, tile, tile, tile], out_shape=[shp, shp, shp, shp],
        input_output_aliases={4 + k: k for k in range(n_prev)},
        compiler_params=_params(("arbitrary", "arbitrary")),
    )(parts, w, m, v, *(prev or ()))


def _gathered_cols(g, inner):
    k = len(inner)
    perm = tuple(range(1, k + 1)) + (0, k + 1)
    t = jnp.transpose(g, perm)
    return t.reshape(tuple(inner) + (g.shape[0] * g.shape[-1],))


def _pair_blocks(wh):
    z = jnp.zeros((8, 64, 64), wh.dtype)
    w2 = wh.reshape(8, 2, 64, 64)
    top = jnp.concatenate([w2[:, 0], z], axis=2)
    bot = jnp.concatenate([z, w2[:, 1]], axis=2)
    return jnp.concatenate([top, bot], axis=1).astype(BF16)


def _unpair_blocks(g):
    return jnp.stack([g[:, :64, :64], g[:, 64:, 64:]], axis=1).reshape(16, 64, 64)


FLAT_ROWS = 512


def _pack_rows(arrays, lead=0):
    parts = [a.reshape(a.shape[:lead] + (-1, LANES)) for a in arrays]
    rows = jnp.concatenate(parts, axis=lead)
    pad = [(0, 0)] * rows.ndim
    pad[lead] = (0, (-rows.shape[lead]) % FLAT_ROWS)
    return jnp.pad(rows, pad)


def kernel(x, c, norm_gain, w_mod, b_mod, w_in, w_out, conv_a_w, sgu_w, sgu_b, lru_conv_w, lru_conv_b, lru_wa, lru_ba, lru_wx, lru_bx, lru_lambda, final_gain, loss_target, m_norm_gain, m_w_mod, m_b_mod, m_w_in, m_w_out, m_conv_a_w, m_sgu_w, m_sgu_b, m_lru_conv_w, m_lru_conv_b, m_lru_wa, m_lru_ba, m_lru_wx, m_lru_bx, m_lru_lambda, m_final_gain, v_norm_gain, v_w_mod, v_b_mod, v_w_in, v_w_out, v_conv_a_w, v_sgu_w, v_sgu_b, v_lru_conv_w, v_lru_conv_b, v_lru_wa, v_lru_ba, v_lru_wx, v_lru_bx, v_lru_lambda, v_final_gain):
    nl = w_in.shape[0]
    nb, s, _ = x.shape
    me = _my_index()
    mod_cols = w_mod.shape[2]


    small = jnp.concatenate([c.reshape(-1, LANES), conv_a_w.reshape(-1, LANES), lru_conv_w.reshape(-1, LANES)])
    n_c, n_ca = nb * D // LANES, nl * 3
    n_small = small.shape[0]
    small = jnp.pad(small, ((0, (-n_small) % 8), (0, 0)))
    small_all, small_token = _all_gather(small, "gather_small")
    c_all = small_all[:, :n_c].reshape(N_DEV * nb, D)

    w_in_b = [w_in[0].astype(BF16)] + list(w_in[1:].astype(BF16))

    def start_w_in(l, dep):
        return _split_start(w_in_b[l], _peers_same_core, False, "gather_w_in_start", dep)

    ici = {0: start_w_in(0, small_token)}
    conv_a_full = _gathered_cols(small_all[:, n_c:n_c + n_ca].reshape(N_DEV, nl, 3, LANES), (nl, 3))
    lru_conv_full = _gathered_cols(small_all[:, n_c + n_ca:n_small].reshape(N_DEV, nl, 4, LANES), (nl, 4))

    def gathered_w_in(started, after):
        block, land = _split_wait(started, after, _peers_same_core, False, False, "gather_w_in_wait")
        return _gather_finish(block, land, "gather_w_in_finish")

    sgu_b_lanes = jnp.broadcast_to(sgu_b[..., None], sgu_b.shape + (LANES,))
    mws = []
    for l in range(nl):
        mws.append((conv_a_full[l], sgu_w[l], sgu_b_lanes[l], lru_conv_full[l], lru_conv_b[l][None, :],
                    _pair_blocks(lru_wa[l]), _pair_blocks(lru_wx[l]), lru_ba[l].reshape(1, D),
                    lru_bx[l].reshape(1, D), lru_lambda[l][None, :]))

    rep_names = ["sgu_w", "sgu_b", "lru_conv_b", "lru_wa", "lru_ba", "lru_wx", "lru_bx", "lru_lambda"]
    rep_w = dict(sgu_w=sgu_w, sgu_b=sgu_b, lru_conv_b=lru_conv_b, lru_wa=lru_wa, lru_ba=lru_ba,
                 lru_wx=lru_wx, lru_bx=lru_bx, lru_lambda=lru_lambda)
    rep_m = dict(sgu_w=m_sgu_w, sgu_b=m_sgu_b, lru_conv_b=m_lru_conv_b, lru_wa=m_lru_wa,
                 lru_ba=m_lru_ba, lru_wx=m_lru_wx, lru_bx=m_lru_bx, lru_lambda=m_lru_lambda)
    rep_v = dict(sgu_w=v_sgu_w, sgu_b=v_sgu_b, lru_conv_b=v_lru_conv_b, lru_wa=v_lru_wa,
                 lru_ba=v_lru_ba, lru_wx=v_lru_wx, lru_bx=v_lru_bx, lru_lambda=v_lru_lambda)

    rep_w_all, rep_m_all, rep_v_all = [_pack_rows([src[n] for n in rep_names], lead=1)
                                       for src in (rep_w, rep_m, rep_v)]
    early = [rep_w_all, rep_m_all, rep_v_all] + w_in_b[1:] + [a for mw in mws for a in mw]

    b_mod_mine = lax.dynamic_slice_in_dim(b_mod, me * mod_cols, mod_cols, axis=1)[:, None, :]
    mod_mine = _mod_proj(c_all, w_mod, b_mod_mine)
    mod_all, mod_token = _all_gather(mod_mine.reshape(nl * N_DEV * nb, mod_cols), "gather_mod",
                                     dep=[ici[0][4]] + early)
    mod_full = _gathered_cols(mod_all.reshape(N_DEV, nl, N_DEV * nb, mod_cols), (nl, N_DEV * nb))
    mod_loc = lax.dynamic_slice_in_dim(mod_full, me * nb, nb, axis=1)
    shift, scale, gate = [mod_loc[:, :, j * D:(j + 1) * D][:, :, None, :] for j in range(3)]

    xs, hs_bf, projs, mergeds, states, wg = [], [], [], [], [], []
    xl = x
    d2d = {}
    wo_started = _split_start(w_out.astype(BF16).reshape(nl * (D // N_DEV), D), _peers_all, False,
                              "gather_w_out_start", mod_token)
    wo = None
    for l in range(nl):
        h = _norm_mod(xl, _after(norm_gain[l][None, :], wo_started[4]), shift[l], scale[l])
        if l == 0:
            wg_l, token = gathered_w_in(ici[0], [h] + early)
            ici[1] = start_w_in(1, token)
            dep = ici[1][4]
        else:
            wg_l = _forward_wait(d2d[l], h, "gather_w_in_d2d_wait")
            dep = d2d[l][4]
        wg.append(wg_l)
        proj = _in_proj(h, wg_l, dep)
        merged, st = _mixer_fwd(proj, mws[l])
        xs.append(xl), hs_bf.append(h), projs.append(proj), mergeds.append(merged), states.append(st)
        gate_l = gate[l]
        if l + 1 < nl:
            block, land = _split_wait(ici[l + 1], merged, _peers_same_core, False, False, "gather_w_in_wait")
            d2d[l + 1] = _forward_start(block, land, "gather_w_in_d2d_start")
            gate_l = _after(gate_l, d2d[l + 1][4])
            if l + 2 < nl:
                ici[l + 2] = start_w_in(l + 2, d2d[l + 1][4])
                gate_l = _after(gate_l, ici[l + 2][4])
        if wo is None:
            _, wo_all = _split_wait(wo_started, merged, _peers_all, False, True, "gather_w_out_wait")
            wo = jnp.transpose(wo_all.reshape(N_DEV, nl, D // N_DEV, D), (1, 0, 2, 3)).reshape(nl, D, D)
        xl = _out_proj(xl, merged, wo[l], gate_l)

    loss_row, dx, g_final = _loss_head(xl, final_gain[None, :], loss_target)
    loss = lax.psum(loss_row[0, 0], ("x", "y", "c"))

    res_big, g_conv = {}, [None] * nl
    dmods = [None] * nl

    def finish_exchange(pending, after):
        l, h_in, h_out, h_rep = pending
        _, recv = _split_wait(h_in, after, _peers_all, True, True, "scatter_w_in_wait")
        res_big["w_in"] = _adamw(recv, w_in, m_w_in, v_w_in, "adamw_w_in", l, res_big.get("w_in"))
        _, recv = _split_wait(h_out, after, _peers_all, True, True, "scatter_w_out_wait")
        res_big["w_out"] = _adamw(recv, w_out, m_w_out, v_w_out, "adamw_w_out", l, res_big.get("w_out"))
        _, recv = _split_wait(h_rep, after, _peers_all, False, True, "gather_rep_wait")
        res_big["rep"] = _adamw(recv[:, None], rep_w_all, rep_m_all, rep_v_all, "adamw_rep", l, res_big.get("rep"))

    pending = None
    g_gains = [None] * nl
    for l in reversed(range(nl)):
        dmerged, gw_out, dgate = _out_proj_bwd(dx, mergeds[l], wo[l], gate[l])
        dproj, g_caw, g_sw, g_sb, g_lcw, g_vec, g_wa, g_wx = _mixer_bwd(projs[l], dmerged, states[l], mws[l])
        g_conv[l] = (g_caw, g_lcw)
        rep_g = dict(
            sgu_w=g_sw, sgu_b=g_sb[:, :, 0], lru_conv_b=g_vec[0],
            lru_wa=_unpair_blocks(g_wa), lru_ba=g_vec[1].reshape(16, 64), lru_wx=_unpair_blocks(g_wx),
            lru_bx=g_vec[2].reshape(16, 64), lru_lambda=g_vec[3])
        rep_block = _pack_rows([rep_g[n] for n in rep_names])
        h_out = _split_start(gw_out.reshape(N_DEV, 1, D // N_DEV, D), _peers_all, True, "scatter_w_out_start")
        h_rep = _split_start(rep_block, _peers_all, False, "gather_rep_start")
        gw_in = _in_proj_bwd_w(hs_bf[l], dproj, h_out[4] + h_rep[4])
        h_in = _split_start(gw_in.reshape(N_DEV, UNITS_PER_DEV, D, UNIT), _peers_all, True, "scatter_w_in_start")
        started = (l, h_in, h_out, h_rep)
        dh = _in_proj_bwd_h(dproj, wg[l], h_in[4])
        dx, dshift, dscale, g_gain = _norm_mod_bwd(dh, xs[l], dx, norm_gain[l][None, :], scale[l])
        g_gains[l] = g_gain
        dmods[l] = jnp.concatenate([dshift, dscale, dgate], axis=2)[:, 0, :]
        if pending is not None:
            finish_exchange(pending, dx)
        pending = started

    conv_parts = jnp.concatenate(
        [jnp.stack([g_conv[l][0] for l in range(nl)]).reshape(nl * 3, N_DEV, LANES),
         jnp.stack([g_conv[l][1] for l in range(nl)]).reshape(nl * 4, N_DEV, LANES)], axis=0)
    conv_parts = jnp.transpose(conv_parts, (1, 0, 2))[:, None]
    conv_recv = _all_to_all(conv_parts, "scatter_conv")

    dmod_loc = jnp.stack(dmods).reshape(nl * nb, 3 * D)
    gain_rows = jnp.pad(jnp.concatenate(g_gains + [g_final], axis=0), ((0, (-(nl + 1)) % 8), (0, 2 * D)))
    tail_g, _ = _all_gather(jnp.concatenate([dmod_loc, gain_rows], axis=0), "gather_dmod",
                            dep=res_big["w_in"][3])
    dmod_g = tail_g[:, :nl * nb]
    gain_parts = tail_g[:, nl * nb:nl * nb + nl + 1, :D][:, None]
    gain_cat = lambda a, b: jnp.concatenate([a, b[None, :]], axis=0)
    res_gain = _adamw(gain_parts, gain_cat(norm_gain, final_gain), gain_cat(m_norm_gain, m_final_gain),
                      gain_cat(v_norm_gain, v_final_gain), "adamw_gain")
    dmod_all = jnp.transpose(dmod_g.reshape(N_DEV, nl, nb, 3 * D), (1, 0, 2, 3)).reshape(nl, N_DEV * nb, 3 * D)
    dmod_mine = lax.dynamic_slice_in_dim(dmod_all, me * mod_cols, mod_cols, axis=2)
    gw_mod, gb_mod = _mod_grad(c_all, dmod_all, dmod_mine)
    res_w_mod = _adamw(gw_mod.reshape(1, 1, nl * D, mod_cols), w_mod.reshape(nl * D, mod_cols),
                       m_w_mod.reshape(nl * D, mod_cols), v_w_mod.reshape(nl * D, mod_cols), "adamw_w_mod")
    res_w_mod = [a.reshape(nl, D, mod_cols) for a in res_w_mod]
    res_b_mod = _adamw(gb_mod.reshape(1, 1, nl, 3 * D), b_mod, m_b_mod, v_b_mod, "adamw_b_mod")
    finish_exchange(pending, res_b_mod[1])

    cat = lambda a, b: jnp.concatenate([a.reshape(nl * 3, LANES), b.reshape(nl * 4, LANES)], axis=0)
    res_conv = _adamw(conv_recv, cat(conv_a_w, lru_conv_w), cat(m_conv_a_w, m_lru_conv_w),
                      cat(v_conv_a_w, v_lru_conv_w), "adamw_conv")
    res_conv_a = [a[:nl * 3].reshape(nl, 3, LANES) for a in res_conv]
    res_lru_conv = [a[nl * 3:].reshape(nl, 4, LANES) for a in res_conv]

    res_rep = []
    for k in range(4):
        off, d = 0, {}
        for n in rep_names:
            n_rows = rep_w[n][0].size // LANES
            d[n] = res_big["rep"][k][:, off:off + n_rows].reshape(rep_w[n].shape)
            off += n_rows
        res_rep.append(d)

    def leaf(k, name):
        if name == "norm_gain":
            return res_gain[k][:nl]
        if name == "final_gain":
            return res_gain[k][nl]
        if name == "w_mod":
            return res_w_mod[k]
        if name == "b_mod":
            return res_b_mod[k]
        if name in ("w_in", "w_out"):
            return res_big[name][k]
        if name == "conv_a_w":
            return res_conv_a[k]
        if name == "lru_conv_w":
            return res_lru_conv[k]
        return res_rep[k][name]

    order = ["norm_gain", "w_mod", "b_mod", "w_in", "w_out", "conv_a_w", "sgu_w", "sgu_b", "lru_conv_w",
             "lru_conv_b", "lru_wa", "lru_ba", "lru_wx", "lru_bx", "lru_lambda", "final_gain"]
    outs = [loss, dx]
    for k in range(4):
        outs += [leaf(k, n) for n in order]
    return tuple(outs)
```

```python
import functools

import jax
import jax.numpy as jnp
from jax import lax
from jax.experimental import pallas as pl
from jax.experimental.pallas import tpu as pltpu

F32 = jnp.float32
BF16 = jnp.bfloat16

D = 1024
N_DEV = 8
N_SEG = 12
LANES = 128
SUBLANES = 8
CHUNK = 128
HALO = 16
UNIT = 512
UNITS_PER_DEV = 3
EPS = 1e-6
LRU_C = 8.0
ADAM_LR, ADAM_B1, ADAM_B2, ADAM_EPS, ADAM_WD, ADAM_STEP = 0.001, 0.9, 0.999, 1e-08, 0.01, 10
VMEM_LIMIT = 56 * 1024 * 1024

AX, AB, AC, AZ, SU, SV, SZ, RX, RZ, GA, GS, GR = range(N_SEG)
MESH = pl.DeviceIdType.MESH


def _params(sem=None):
    return pltpu.CompilerParams(dimension_semantics=sem, vmem_limit_bytes=VMEM_LIMIT)


def _my_index():
    return 4 * lax.axis_index("x") + 2 * lax.axis_index("y") + lax.axis_index("c")


def _all_gather(block, name, dep=None):
    deps = [] if dep is None else list(dep) if isinstance(dep, (list, tuple)) else [dep]

    def body(x_ref, *refs):
        out_ref, token, send_sems, recv_sems, local_sem = refs[-5:]
        x, y, c = lax.axis_index("x"), lax.axis_index("y"), lax.axis_index("c")
        me, sibling = (x, y, c), (x, y, 1 - c)
        chips = [(1 - x, y), (x, 1 - y), (1 - x, 1 - y)]
        token[...] = jnp.zeros_like(token)

        def rows(px, py, pc):
            return out_ref.at[4 * px + 2 * py + pc]

        def copy(k, blk, to, src=None):
            return pltpu.make_async_remote_copy(
                src_ref=rows(*blk) if src is None else src, dst_ref=rows(*blk),
                send_sem=send_sems.at[k], recv_sem=recv_sems.at[k], device_id=to, device_id_type=MESH)

        mine = pltpu.make_async_copy(x_ref, rows(*me), local_sem)
        mine.start()
        first = [copy(0, me, sibling, src=x_ref)]
        first += [copy(1 + j, me, (*chip, c), src=x_ref) for j, chip in enumerate(chips)]
        for cp in first:
            cp.start()
        passed = [copy(4 + j, (*chip, c), sibling) for j, chip in enumerate(chips)]
        for j, chip in enumerate(chips):
            copy(1 + j, (*chip, c), me).wait_recv()
            passed[j].start()
        copy(0, sibling, me).wait_recv()
        for j, chip in enumerate(chips):
            copy(4 + j, (*chip, 1 - c), me).wait_recv()
        for cp in first + passed:
            cp.wait_send()
        mine.wait()

    return pl.pallas_call(
        body, name=name,
        out_shape=[jax.ShapeDtypeStruct((N_DEV,) + block.shape, block.dtype), jax.ShapeDtypeStruct((8, LANES), F32)],
        in_specs=[pl.BlockSpec(memory_space=pltpu.VMEM)]
        + [pl.BlockSpec(memory_space=pl.ANY)] * len(deps),
        out_specs=[pl.BlockSpec(memory_space=pl.ANY), pl.BlockSpec(memory_space=pltpu.VMEM)],
        scratch_shapes=[pltpu.SemaphoreType.DMA((7,)), pltpu.SemaphoreType.DMA((7,)), pltpu.SemaphoreType.DMA],
    )(block, *deps)


def _all_to_all(blocks, name):
    def body(x_ref, out_ref, send_sems, recv_sems, local_sem):
        x, y, c = lax.axis_index("x"), lax.axis_index("y"), lax.axis_index("c")
        my = 4 * x + 2 * y + c
        mine = pltpu.make_async_copy(x_ref.at[my], out_ref.at[my], local_sem)
        mine.start()
        peers = []
        for r in range(1, N_DEV):
            px = 1 - x if r & 4 else x
            py = 1 - y if r & 2 else y
            pc = 1 - c if r & 1 else c
            peers.append((r - 1, 4 * px + 2 * py + pc, (px, py, pc)))

        def copy(k, src_slot, dst_slot, to):
            return pltpu.make_async_remote_copy(
                src_ref=x_ref.at[src_slot], dst_ref=out_ref.at[dst_slot],
                send_sem=send_sems.at[k], recv_sem=recv_sems.at[k], device_id=to, device_id_type=MESH)

        sends = [copy(k, pid, my, to) for k, pid, to in peers]
        for cp in sends:
            cp.start()
        for k, pid, to in peers:
            copy(k, pid, pid, to).wait_recv()
        for cp in sends:
            cp.wait_send()
        mine.wait()

    return pl.pallas_call(
        body, name=name,
        out_shape=jax.ShapeDtypeStruct(blocks.shape, blocks.dtype),
        in_specs=[pl.BlockSpec(memory_space=pltpu.VMEM)],
        out_specs=pl.BlockSpec(memory_space=pl.ANY),
        scratch_shapes=[pltpu.SemaphoreType.DMA((7,)), pltpu.SemaphoreType.DMA((7,)), pltpu.SemaphoreType.DMA],
    )(blocks)


_HBM = pl.BlockSpec(memory_space=pltpu.HBM)
_SEM = pl.BlockSpec(memory_space=pltpu.SEMAPHORE)
_EFFECT = pltpu.SideEffectType.DATAFLOW_SIDE_EFFECTING


def _peers_all(x, y, c):
    out = []
    for r in range(1, N_DEV):
        px = 1 - x if r & 4 else x
        py = 1 - y if r & 2 else y
        pc = 1 - c if r & 1 else c
        out.append((r - 1, 4 * px + 2 * py + pc, (px, py, pc)))
    return out


def _peers_same_core(x, y, c):
    return [(k, 4 * px + 2 * py + c, (px, py, c))
            for k, (px, py) in enumerate([(1 - x, y), (x, 1 - y), (1 - x, 1 - y)])]


def _split_start(src, peers_fn, scatter, name, dep=None):
    blk = src.shape[1:] if scatter else src.shape
    land_shape = (N_DEV,) + tuple(blk)
    n = len(peers_fn(0, 0, 0))
    deps = [] if dep is None else [dep]

    def body(x_ref, land_ref, *rest):
        send_sems, recv_sems, x_thru, land_thru, token = rest[len(deps):]
        x, y, c = lax.axis_index("x"), lax.axis_index("y"), lax.axis_index("c")
        my = 4 * x + 2 * y + c
        for k, pid, to in peers_fn(x, y, c):
            pltpu.make_async_remote_copy(
                src_ref=x_ref.at[pid] if scatter else x_ref, dst_ref=land_ref.at[my],
                send_sem=send_sems.at[k], recv_sem=recv_sems.at[k], device_id=to, device_id_type=MESH).start()
        token[...] = jnp.zeros_like(token)

    return pl.pallas_call(
        body, name=name,
        out_shape=(pltpu.SemaphoreType.DMA((n,)), pltpu.SemaphoreType.DMA((n,)),
                   pltpu.HBM(src.shape, src.dtype), pltpu.HBM(land_shape, src.dtype),
                   jax.ShapeDtypeStruct((8, LANES), F32)),
        in_specs=(_HBM, _HBM) + (pl.BlockSpec(memory_space=pl.ANY),) * len(deps),
        out_specs=(_SEM, _SEM, _HBM, _HBM, pl.BlockSpec(memory_space=pltpu.VMEM)),
        input_output_aliases={0: 2, 1: 3},
        compiler_params=pltpu.CompilerParams(has_side_effects=_EFFECT),
    )(pltpu.with_memory_space_constraint(src, pltpu.HBM),
      pltpu.with_memory_space_constraint(lax.empty(land_shape, src.dtype), pltpu.HBM), *deps)


def _split_wait(handles, after, peers_fn, scatter, own, name):
    send_sems, recv_sems, src_thru, land_thru, _ = handles
    blk = land_thru.shape[1:]
    after = list(after) if isinstance(after, (list, tuple)) else [after]

    def body(x_ref, land_ref, send_sems, recv_sems, *rest):
        stage = rest[len(after) + 2:]
        x, y, c = lax.axis_index("x"), lax.axis_index("y"), lax.axis_index("c")
        if own:
            my = 4 * x + 2 * y + c
            mine = _staged_copy(x_ref.at[my] if scatter else x_ref, land_ref.at[my], *stage)
        for k, pid, to in peers_fn(x, y, c):
            cp = pltpu.make_async_remote_copy(
                src_ref=x_ref.at[pid] if scatter else x_ref, dst_ref=land_ref.at[pid],
                send_sem=send_sems.at[k], recv_sem=recv_sems.at[k], device_id=to, device_id_type=MESH)
            cp.wait_send()
            cp.wait_recv()
        if own:
            mine.wait()

    return pl.pallas_call(
        body, name=name,
        out_shape=(pltpu.HBM(src_thru.shape, src_thru.dtype), pltpu.HBM(land_thru.shape, land_thru.dtype)),
        in_specs=(_HBM, _HBM, _SEM, _SEM) + (pl.BlockSpec(memory_space=pl.ANY),) * len(after),
        out_specs=(_HBM, _HBM),
        input_output_aliases={0: 0, 1: 1},
        scratch_shapes=[pltpu.VMEM(blk, land_thru.dtype), pltpu.SemaphoreType.DMA((2,))] if own else [],
        compiler_params=pltpu.CompilerParams(has_side_effects=_EFFECT, vmem_limit_bytes=VMEM_LIMIT),
    )(src_thru, land_thru, send_sems, recv_sems, *after)


def _staged_copy(src_ref, dst_ref, buf, sems):
    leg = pltpu.make_async_copy(src_ref, buf, sems.at[0])
    leg.start()
    leg.wait()
    leg = pltpu.make_async_copy(buf, dst_ref, sems.at[1])
    leg.start()
    return leg


def _gather_finish(block, land, name):
    def body(x_ref, land_ref, out_ref, token, send_sems, recv_sems, buf, local_sems):
        x, y, c = lax.axis_index("x"), lax.axis_index("y"), lax.axis_index("c")
        my, sib_id, sibling = 4 * x + 2 * y + c, 4 * x + 2 * y + 1 - c, (x, y, 1 - c)
        token[...] = jnp.zeros_like(token)

        def copy(k, slot, src=None):
            return pltpu.make_async_remote_copy(
                src_ref=land_ref.at[slot] if src is None else src, dst_ref=out_ref.at[slot],
                send_sem=send_sems.at[k], recv_sem=recv_sems.at[k], device_id=sibling, device_id_type=MESH)

        chips = _peers_same_core(x, y, c)
        sends = [copy(0, my, src=x_ref)] + [copy(1 + k, pid) for k, pid, _ in chips]
        for cp in sends:
            cp.start()
        mine = _staged_copy(x_ref, out_ref.at[my], buf, local_sems)
        copy(0, sib_id).wait_recv()
        for k, pid, _ in chips:
            copy(1 + k, pid + 1 - 2 * c).wait_recv()
        for cp in sends:
            cp.wait_send()
        mine.wait()

    return pl.pallas_call(
        body, name=name,
        out_shape=[jax.ShapeDtypeStruct(land.shape, land.dtype), jax.ShapeDtypeStruct((8, LANES), F32)],
        in_specs=[pl.BlockSpec(memory_space=pl.ANY), pl.BlockSpec(memory_space=pl.ANY)],
        out_specs=[pl.BlockSpec(memory_space=pl.ANY), pl.BlockSpec(memory_space=pltpu.VMEM)],
        input_output_aliases={1: 0},
        scratch_shapes=[pltpu.SemaphoreType.DMA((4,)), pltpu.SemaphoreType.DMA((4,)),
                        pltpu.VMEM(block.shape, block.dtype), pltpu.SemaphoreType.DMA((2,))],
        compiler_params=pltpu.CompilerParams(vmem_limit_bytes=VMEM_LIMIT),
    )(block, land)


def _forward_start(block, land, name):
    def body(x_ref, land_ref, send_sems, recv_sems, x_thru, land_thru, token):
        x, y, c = lax.axis_index("x"), lax.axis_index("y"), lax.axis_index("c")
        my, sibling = 4 * x + 2 * y + c, (x, y, 1 - c)
        slots = [(0, my, x_ref)] + [(1 + k, pid, land_ref.at[pid]) for k, pid, _ in _peers_same_core(x, y, c)]
        for k, slot, src in slots:
            pltpu.make_async_remote_copy(
                src_ref=src, dst_ref=land_ref.at[slot], send_sem=send_sems.at[k], recv_sem=recv_sems.at[k],
                device_id=sibling, device_id_type=MESH).start()
        token[...] = jnp.zeros_like(token)

    return pl.pallas_call(
        body, name=name,
        out_shape=(pltpu.SemaphoreType.DMA((4,)), pltpu.SemaphoreType.DMA((4,)),
                   pltpu.HBM(block.shape, block.dtype), pltpu.HBM(land.shape, land.dtype),
                   jax.ShapeDtypeStruct((8, LANES), F32)),
        in_specs=(_HBM, _HBM),
        out_specs=(_SEM, _SEM, _HBM, _HBM, pl.BlockSpec(memory_space=pltpu.VMEM)),
        input_output_aliases={0: 2, 1: 3},
        compiler_params=pltpu.CompilerParams(has_side_effects=_EFFECT),
    )(block, land)


def _forward_wait(handles, after, name):
    send_sems, recv_sems, block_thru, land_thru, _ = handles

    def body(x_ref, land_ref, send_sems, recv_sems, after_ref, x_dead, got_ref, buf, local_sems):
        x, y, c = lax.axis_index("x"), lax.axis_index("y"), lax.axis_index("c")
        my, sib_id, sibling = 4 * x + 2 * y + c, 4 * x + 2 * y + 1 - c, (x, y, 1 - c)
        mine = _staged_copy(x_ref, land_ref.at[my], buf, local_sems)
        slots = [(0, my, sib_id)] + [(1 + k, pid, pid + 1 - 2 * c) for k, pid, _ in _peers_same_core(x, y, c)]
        for k, sent, got in slots:
            cp = pltpu.make_async_remote_copy(
                src_ref=land_ref.at[sent], dst_ref=land_ref.at[got], send_sem=send_sems.at[k],
                recv_sem=recv_sems.at[k], device_id=sibling, device_id_type=MESH)
            cp.wait_send()
            cp.wait_recv()
        mine.wait()

    return pl.pallas_call(
        body, name=name,
        out_shape=(pltpu.HBM(block_thru.shape, block_thru.dtype), pltpu.HBM(land_thru.shape, land_thru.dtype)),
        in_specs=(_HBM, _HBM, _SEM, _SEM, pl.BlockSpec(memory_space=pl.ANY)),
        out_specs=(_HBM, _HBM),
        input_output_aliases={0: 0, 1: 1},
        scratch_shapes=[pltpu.VMEM(block_thru.shape, block_thru.dtype), pltpu.SemaphoreType.DMA((2,))],
        compiler_params=pltpu.CompilerParams(has_side_effects=_EFFECT, vmem_limit_bytes=VMEM_LIMIT),
    )(block_thru, land_thru, send_sems, recv_sems, after)[1]


def _after(v, token):
    return v + token[0, 0].astype(v.dtype)


def _dsilu(x, s):
    return s * (1.0 + x * (1.0 - s))


def _log1p(x):
    u = 1.0 + x
    d = u - 1.0
    return jnp.where(d == 0.0, x, jnp.log(u) * (x / jnp.where(d == 0.0, 1.0, d)))


def _softplus_neg(lam):
    return jnp.maximum(-lam, 0.0) + _log1p(jnp.exp(-jnp.abs(lam)))


def _neg_expm1(y, exp_y):
    poly = -y * (1.0 + y * (0.5 + y * (1.0 / 6.0 + y * (1.0 / 24.0))))
    return jnp.where(y > -0.05, poly, 1.0 - exp_y)


def _sigmoid(x):
    return 0.5 * jnp.tanh(0.5 * x) + 0.5


def _shift_dn(cur, prev, k):
    ext = jnp.concatenate([prev, cur], axis=0)
    return pltpu.roll(ext, k, 0)[HALO:, :]


def _shift_up(cur, nxt, k):
    n = cur.shape[0]
    ext = jnp.concatenate([cur, nxt], axis=0)
    return pltpu.roll(ext, n + HALO - k, 0)[:n, :]


def _scan_fwd(a, b, h_prev):
    groups = a.shape[0] // SUBLANES
    a3 = a.reshape(groups, SUBLANES, LANES)
    b3 = b.reshape(groups, SUBLANES, LANES)
    row = lax.broadcasted_iota(jnp.int32, a3.shape, 1)
    k = 1
    while k < SUBLANES:
        a_sh = jnp.where(row >= k, pltpu.roll(a3, k, 1), 1.0)
        b_sh = jnp.where(row >= k, pltpu.roll(b3, k, 1), 0.0)
        b3 = a3 * b_sh + b3
        a3 = a3 * a_sh
        k *= 2
    carry = h_prev[HALO - 1:HALO, :]
    out = []
    for i in range(groups):
        hg = b3[i] + a3[i] * carry
        out.append(hg)
        carry = hg[SUBLANES - 1:SUBLANES, :]
    return jnp.concatenate(out, axis=0)


def _scan_rev(a_next, g, lam_next):
    groups = g.shape[0] // SUBLANES
    a3 = a_next.reshape(groups, SUBLANES, LANES)
    g3 = g.reshape(groups, SUBLANES, LANES)
    row = lax.broadcasted_iota(jnp.int32, a3.shape, 1)
    k = 1
    while k < SUBLANES:
        ok = row < SUBLANES - k
        a_sh = jnp.where(ok, pltpu.roll(a3, SUBLANES - k, 1), 1.0)
        g_sh = jnp.where(ok, pltpu.roll(g3, SUBLANES - k, 1), 0.0)
        g3 = g3 + a3 * g_sh
        a3 = a3 * a_sh
        k *= 2
    carry = lam_next[0:1, :]
    out = [None] * groups
    for i in reversed(range(groups)):
        lg = g3[i] + a3[i] * carry
        out[i] = lg
        carry = lg[0:1, :]
    return jnp.concatenate(out, axis=0)


def _rowsum(v):
    return jnp.sum(v, axis=0, keepdims=True)


def _dot(a, b):
    return jnp.dot(a, b, preferred_element_type=F32)


def _dot_nt(a, b):
    return lax.dot_general(a, b, (((1,), (1,)), ((), ())), preferred_element_type=F32)


def _dot_tn(a, b):
    return lax.dot_general(a, b, (((0,), (0,)), ((), ())), preferred_element_type=F32)


class _MixerWeights:
    def __init__(self, caw_ref, sw_ref, sb_ref, lcw_ref, lcb_ref, wa_ref, wx_ref, ba_ref, bx_ref, lam_ref):
        self.caw = [caw_ref[j:j + 1, :] for j in range(3)]
        self.lcw = [lcw_ref[j:j + 1, :] for j in range(4)]
        self.lcb = lcb_ref[...]
        row = lax.broadcasted_iota(jnp.int32, (CHUNK, CHUNK), 0)
        col = lax.broadcasted_iota(jnp.int32, (CHUNK, CHUNK), 1)
        self.tril = col <= row
        self.sw = jnp.where(self.tril, sw_ref[...], 0.0).astype(BF16)
        self.sb = sb_ref[...]
        self.wa = wa_ref[...]
        self.wx = wx_ref[...]
        self.ba = ba_ref[...]
        self.bx = bx_ref[...]
        lam = lam_ref[...]
        self.neg_c_sp = -LRU_C * _softplus_neg(lam)
        self.dsp_dlam = -_sigmoid(-lam)


def _mixer_a(ld, ldp, w):
    t = {}
    a_x, a_c = ld(AX), ld(AC)
    t["a_x"], t["a_c"], t["a_b"], t["a_z"] = a_x, a_c, ld(AB), ld(AZ)
    ca = a_c * a_x
    ca_p = ldp(AC) * ldp(AX)
    t["ca"], t["ca1"], t["ca2"] = ca, _shift_dn(ca, ca_p, 1), _shift_dn(ca, ca_p, 2)
    t["cv"] = w.caw[2] * ca + w.caw[1] * t["ca1"] + w.caw[0] * t["ca2"]
    t["sa"] = _sigmoid(t["a_z"])
    t["silu_az"] = t["a_z"] * t["sa"]
    t["y_a"] = t["silu_az"] * t["a_b"] * t["cv"]
    return t


def _mixer_b(ld, w):
    t = {}
    v = ld(SV)
    vc = v - jnp.mean(v, axis=1, keepdims=True)
    t["rstd"] = lax.rsqrt(jnp.mean(vc * vc, axis=1, keepdims=True) + EPS)
    t["vn"] = vc * t["rstd"]
    t["z"] = _dot(w.sw, t["vn"].astype(BF16)) + w.sb
    t["s_u"], t["s_z"] = ld(SU), ld(SZ)
    t["ss"] = _sigmoid(t["s_z"])
    t["silu_sz"] = t["s_z"] * t["ss"]
    t["y_s"] = t["silu_sz"] * t["s_u"] * t["z"]
    return t


def _mixer_c(ld, ldp, w, backward):
    t = {}
    r_x, r_xp = ld(RX), ldp(RX)
    t["rx"] = [_shift_dn(r_x, r_xp, 3), _shift_dn(r_x, r_xp, 2), _shift_dn(r_x, r_xp, 1), r_x]
    xc = w.lcb + w.lcw[0] * t["rx"][0] + w.lcw[1] * t["rx"][1] + w.lcw[2] * t["rx"][2] + w.lcw[3] * r_x
    t["xc"] = xc
    xcb = xc.astype(BF16)
    t["r"] = _sigmoid(_dot(xcb, w.wa) + w.ba)
    t["i"] = _sigmoid(_dot(xcb, w.wx) + w.bx)
    la = t["r"] * w.neg_c_sp
    t["a"] = jnp.exp(la)
    t["em"] = _neg_expm1(2.0 * la, t["a"] * t["a"])
    if backward:
        t["inv_mult"] = lax.rsqrt(t["em"])
        t["mult"] = t["em"] * t["inv_mult"]
    else:
        t["mult"] = jnp.sqrt(t["em"])
    t["b"] = t["mult"] * (t["i"] * xc)
    t["r_z"] = ld(RZ)
    t["sr"] = _sigmoid(t["r_z"])
    t["silu_rz"] = t["r_z"] * t["sr"]
    return t


def _mixer_pre_scan(ld, ldp, w, backward):
    t = {**_mixer_a(ld, ldp, w), **_mixer_b(ld, w), **_mixer_c(ld, ldp, w, backward)}
    t["ga"], t["gs"], t["gr"] = _sigmoid(ld(GA)), _sigmoid(ld(GS)), _sigmoid(ld(GR))
    return t


def _weight_specs(n_cb_axis):
    def at(fn):
        return lambda *g: fn(g[n_cb_axis])
    return [
        pl.BlockSpec((3, LANES), at(lambda cb: (0, cb))),
        pl.BlockSpec((None, CHUNK, CHUNK), at(lambda cb: (cb, 0, 0))),
        pl.BlockSpec((None, CHUNK, LANES), at(lambda cb: (cb, 0, 0))),
        pl.BlockSpec((4, LANES), at(lambda cb: (0, cb))),
        pl.BlockSpec((1, LANES), at(lambda cb: (0, cb))),
        pl.BlockSpec((None, LANES, LANES), at(lambda cb: (cb, 0, 0))),
        pl.BlockSpec((None, LANES, LANES), at(lambda cb: (cb, 0, 0))),
        pl.BlockSpec((1, LANES), at(lambda cb: (0, cb))),
        pl.BlockSpec((1, LANES), at(lambda cb: (0, cb))),
        pl.BlockSpec((1, LANES), at(lambda cb: (0, cb))),
    ]


def _chunk_loaders(p_ref, c):
    r0 = pl.multiple_of(c * CHUNK, CHUNK)
    rp = pl.multiple_of(jnp.maximum(c * CHUNK - HALO, 0), HALO)

    def ld(j):
        return p_ref[j, pl.ds(r0, CHUNK), :].astype(F32)

    def ldp(j):
        return jnp.where(c > 0, p_ref[j, pl.ds(rp, HALO), :].astype(F32), 0.0)

    return r0, rp, ld, ldp


def _mixer_fwd(proj, mw):
    _, nb, s, _ = proj.shape
    n_chunks = s // CHUNK

    def body(p_ref, *refs):
        w = _MixerWeights(*refs[:10])
        merged_ref, hs_ref = refs[10:]

        def chunk(c, h_prev):
            r0, _, ld, ldp = _chunk_loaders(p_ref, c)
            t = _mixer_pre_scan(ld, ldp, w, False)
            h = _scan_fwd(t["a"], t["b"], h_prev)
            y_r = t["silu_rz"] * h
            merged = t["ga"] * t["y_a"] + t["gs"] * t["y_s"] + t["gr"] * y_r
            merged_ref[pl.ds(r0, CHUNK), :] = merged.astype(BF16)
            hs_ref[pl.ds(r0, CHUNK), :] = h
            return h[CHUNK - HALO:, :]

        lax.fori_loop(0, n_chunks, chunk, jnp.zeros((HALO, LANES), F32))

    slab = pl.BlockSpec((None, s, LANES), lambda cb, b: (b, 0, cb))
    return pl.pallas_call(
        body, name="mixer_fwd", grid=(D // LANES, nb),
        in_specs=[pl.BlockSpec((N_SEG, None, s, LANES), lambda cb, b: (0, b, 0, cb))] + _weight_specs(0),
        out_specs=[slab, slab],
        out_shape=[jax.ShapeDtypeStruct((nb, s, D), BF16), jax.ShapeDtypeStruct((nb, s, D), F32)],
        compiler_params=_params(("arbitrary", "arbitrary")),
    )(proj, *mw)


def _mixer_bwd(proj, dmerged, hs, mw):
    _, nb, s, _ = proj.shape
    n_chunks = s // CHUNK

    def body(p_ref, dm_ref, hs_ref, *refs):
        w = _MixerWeights(*refs[:10])
        dp_ref, g_caw, g_sw, g_sb, g_lcw, g_vec, g_wa, g_wx = refs[10:]

        @pl.when(pl.program_id(1) == 0)
        def _():
            for ref in (g_caw, g_sw, g_sb, g_lcw, g_vec, g_wa, g_wx):
                ref[...] = jnp.zeros_like(ref)

        def chunk(i, carry):
            dcv_n, dxc_n, lam_n, a_n = carry
            c = n_chunks - 1 - i
            r0, rp, ld, ldp = _chunk_loaders(p_ref, c)
            t = _mixer_pre_scan(ld, ldp, w, True)
            h = hs_ref[pl.ds(r0, CHUNK), :]
            h_p = jnp.where(c > 0, hs_ref[pl.ds(rp, HALO), :], 0.0)
            h_prev = _shift_dn(h, h_p, 1)
            dm = dm_ref[pl.ds(r0, CHUNK), :].astype(F32)
            y_r = t["silu_rz"] * h

            def out(j, val):
                dp_ref[j, pl.ds(r0, CHUNK), :] = val.astype(BF16)

            ga, gs, gr = t["ga"], t["gs"], t["gr"]
            out(GA, dm * t["y_a"] * ga * (1.0 - ga))
            out(GS, dm * t["y_s"] * gs * (1.0 - gs))
            out(GR, dm * y_r * gr * (1.0 - gr))

            dy_a = dm * ga
            out(AZ, dy_a * t["a_b"] * t["cv"] * _dsilu(t["a_z"], t["sa"]))
            out(AB, dy_a * t["silu_az"] * t["cv"])
            dcv = dy_a * t["silu_az"] * t["a_b"]
            dca = w.caw[2] * dcv + w.caw[1] * _shift_up(dcv, dcv_n, 1) + w.caw[0] * _shift_up(dcv, dcv_n, 2)
            out(AC, dca * t["a_x"])
            out(AX, dca * t["a_c"])
            g_caw[2:3, :] += _rowsum(dcv * t["ca"])
            g_caw[1:2, :] += _rowsum(dcv * t["ca1"])
            g_caw[0:1, :] += _rowsum(dcv * t["ca2"])

            dy_s = dm * gs
            out(SZ, dy_s * t["s_u"] * t["z"] * _dsilu(t["s_z"], t["ss"]))
            out(SU, dy_s * t["silu_sz"] * t["z"])
            dz = dy_s * t["silu_sz"] * t["s_u"]
            dzb = dz.astype(BF16)
            g_sb[...] += jnp.broadcast_to(jnp.sum(dz, axis=1, keepdims=True), (CHUNK, LANES))
            g_sw[...] += _dot_nt(dzb, t["vn"].astype(BF16))
            dvn = _dot_tn(w.sw, dzb)
            vn = t["vn"]
            out(SV, t["rstd"] * (dvn - jnp.mean(dvn, axis=1, keepdims=True)
                                 - vn * jnp.mean(dvn * vn, axis=1, keepdims=True)))

            dy_r = dm * gr
            out(RZ, dy_r * h * _dsilu(t["r_z"], t["sr"]))
            lam = _scan_rev(_shift_up(t["a"], a_n, 1), dy_r * t["silu_rz"], lam_n)
            a, r, ig, xc, mult = t["a"], t["r"], t["i"], t["xc"], t["mult"]
            d_i = lam * mult * xc
            d_mult = lam * ig * xc
            dxc = lam * mult * ig
            dla = lam * h_prev * a - d_mult * ((1.0 - t["em"]) * t["inv_mult"])
            g_vec[3:4, :] += _rowsum(dla * r) * (-LRU_C * w.dsp_dlam)
            dpr = (dla * w.neg_c_sp) * r * (1.0 - r)
            dpi = d_i * ig * (1.0 - ig)
            dprb, dpib, xcb = dpr.astype(BF16), dpi.astype(BF16), xc.astype(BF16)
            g_wa[...] += _dot_tn(xcb, dprb)
            g_wx[...] += _dot_tn(xcb, dpib)
            g_vec[1:2, :] += _rowsum(dpr)
            g_vec[2:3, :] += _rowsum(dpi)
            dxc = dxc + _dot_nt(dprb, w.wa) + _dot_nt(dpib, w.wx)
            g_vec[0:1, :] += _rowsum(dxc)
            out(RX, w.lcw[3] * dxc + w.lcw[2] * _shift_up(dxc, dxc_n, 1)
                + w.lcw[1] * _shift_up(dxc, dxc_n, 2) + w.lcw[0] * _shift_up(dxc, dxc_n, 3))
            for j in range(4):
                g_lcw[j:j + 1, :] += _rowsum(dxc * t["rx"][j])
            return dcv[:HALO, :], dxc[:HALO, :], lam[:HALO, :], a[:HALO, :]

        zero = jnp.zeros((HALO, LANES), F32)
        lax.fori_loop(0, n_chunks, chunk, (zero, zero, zero, zero))

        @pl.when(pl.program_id(1) == nb - 1)
        def _():
            g_sw[...] = jnp.where(w.tril, g_sw[...], 0.0)

    slab = lambda dt: pl.BlockSpec((None, s, LANES), lambda cb, b: (b, 0, cb))
    seg = pl.BlockSpec((N_SEG, None, s, LANES), lambda cb, b: (0, b, 0, cb))
    rows = lambda n: pl.BlockSpec((n, LANES), lambda cb, b: (0, cb))
    sq = pl.BlockSpec((None, LANES, LANES), lambda cb, b: (cb, 0, 0))
    n_cb = D // LANES
    return pl.pallas_call(
        body, name="mixer_bwd", grid=(n_cb, nb),
        in_specs=[seg, slab(BF16), slab(F32)] + _weight_specs(0),
        out_specs=[seg, rows(3), sq, sq, rows(4), rows(8), sq, sq],
        out_shape=[
            jax.ShapeDtypeStruct(proj.shape, BF16),
            jax.ShapeDtypeStruct((3, D), F32),
            jax.ShapeDtypeStruct((n_cb, CHUNK, CHUNK), F32),
            jax.ShapeDtypeStruct((n_cb, CHUNK, LANES), F32),
            jax.ShapeDtypeStruct((4, D), F32),
            jax.ShapeDtypeStruct((8, D), F32),
            jax.ShapeDtypeStruct((n_cb, LANES, LANES), F32),
            jax.ShapeDtypeStruct((n_cb, LANES, LANES), F32),
        ],
        compiler_params=_params(("arbitrary", "arbitrary")),
    )(proj, dmerged, hs, *mw)


def _row_tile(s, want):
    return want if s % want == 0 else s


def _norm_mod(x, gain, shift, scale):
    nb, s, _ = x.shape
    tm = _row_tile(s, 512)

    def body(x_ref, g_ref, sh_ref, sc_ref, h_ref):
        xv = x_ref[...]
        r = lax.rsqrt(jnp.mean(xv * xv, axis=1, keepdims=True) + EPS)
        h_ref[...] = ((xv * r) * g_ref[...] * (1.0 + sc_ref[...]) + sh_ref[...]).astype(BF16)

    tile = pl.BlockSpec((None, tm, D), lambda b, m: (b, m, 0))
    vec = pl.BlockSpec((None, 1, D), lambda b, m: (b, 0, 0))
    return pl.pallas_call(
        body, name="norm_mod", grid=(nb, s // tm),
        in_specs=[tile, pl.BlockSpec((1, D), lambda b, m: (0, 0)), vec, vec],
        out_specs=tile, out_shape=jax.ShapeDtypeStruct(x.shape, BF16),
        compiler_params=_params(("arbitrary", "arbitrary")),
    )(x, gain, shift, scale)


def _in_proj(h, wg, dep):
    nb, s, _ = h.shape

    def body(h_ref, w_ref, dep_ref, o_ref):
        o_ref[...] = _dot(h_ref[...], w_ref[...]).astype(BF16)

    return pl.pallas_call(
        body, name="in_proj", grid=(nb, N_DEV * UNITS_PER_DEV),
        in_specs=[pl.BlockSpec((None, s, D), lambda b, u: (b, 0, 0)),
                  pl.BlockSpec((None, D, UNIT), lambda b, u: (u // UNITS_PER_DEV, 0, u % UNITS_PER_DEV)),
                  pl.BlockSpec((8, LANES), lambda b, u: (0, 0))],
        out_specs=pl.BlockSpec((None, None, s, UNIT), lambda b, u: (u // 2, b, 0, u % 2)),
        out_shape=jax.ShapeDtypeStruct((N_SEG, nb, s, D), BF16),
        compiler_params=_params(("arbitrary", "arbitrary")),
    )(h, wg, dep)


def _out_proj(x, merged, wout, gate):
    nb, s, _ = x.shape
    tm = _row_tile(s, 512)

    def body(x_ref, m_ref, w_ref, g_ref, o_ref):
        o_ref[...] = x_ref[...] + g_ref[...] * _dot(m_ref[...], w_ref[...])

    tile = pl.BlockSpec((None, tm, D), lambda b, m: (b, m, 0))
    return pl.pallas_call(
        body, name="out_proj", grid=(nb, s // tm),
        in_specs=[tile, tile, pl.BlockSpec((D, D), lambda b, m: (0, 0)),
                  pl.BlockSpec((None, 1, D), lambda b, m: (b, 0, 0))],
        out_specs=tile, out_shape=jax.ShapeDtypeStruct(x.shape, F32),
        compiler_params=_params(("arbitrary", "arbitrary")),
    )(x, merged, wout, gate)


def _loss_head(x, gain, target):
    nb, s, _ = x.shape
    tm = _row_tile(s, 512)

    def body(x_ref, g_ref, t_ref, loss_ref, dx_ref, dg_ref):
        first = (pl.program_id(0) == 0) & (pl.program_id(1) == 0)
        last = (pl.program_id(0) == nb - 1) & (pl.program_id(1) == s // tm - 1)

        @pl.when(first)
        def _():
            loss_ref[...] = jnp.zeros_like(loss_ref)
            dg_ref[...] = jnp.zeros_like(dg_ref)

        xv = x_ref[...]
        r = lax.rsqrt(jnp.mean(xv * xv, axis=1, keepdims=True) + EPS)
        xn = xv * r
        g = g_ref[...]
        e = xn * g - t_ref[...]
        loss_ref[...] += _rowsum(e * e) * (0.5 / D)
        dy = e * (1.0 / D)
        dg_ref[...] += _rowsum(dy * xn)
        dxn = dy * g
        dx_ref[...] = r * (dxn - xn * jnp.mean(dxn * xn, axis=1, keepdims=True))

        @pl.when(last)
        def _():
            loss_ref[...] = jnp.broadcast_to(jnp.sum(loss_ref[...], axis=1, keepdims=True), (1, D))

    tile = pl.BlockSpec((None, tm, D), lambda b, m: (b, m, 0))
    vec = pl.BlockSpec((1, D), lambda b, m: (0, 0))
    return pl.pallas_call(
        body, name="loss_head", grid=(nb, s // tm),
        in_specs=[tile, vec, tile], out_specs=[vec, tile, vec],
        out_shape=[jax.ShapeDtypeStruct((1, D), F32), jax.ShapeDtypeStruct(x.shape, F32),
                   jax.ShapeDtypeStruct((1, D), F32)],
        compiler_params=_params(("arbitrary", "arbitrary")),
    )(x, gain, target)


def _out_proj_bwd(dxo, merged, wout, gate):
    nb, s, _ = dxo.shape
    tm = _row_tile(s, 512)

    def body(d_ref, m_ref, w_ref, g_ref, dm_ref, gw_ref, dg_ref):
        @pl.when((pl.program_id(0) == 0) & (pl.program_id(1) == 0))
        def _():
            gw_ref[...] = jnp.zeros_like(gw_ref)

        @pl.when(pl.program_id(1) == 0)
        def _():
            dg_ref[...] = jnp.zeros_like(dg_ref)

        d = d_ref[...]
        m = m_ref[...]
        wv = w_ref[...]
        dg_ref[...] += _rowsum(d * _dot(m, wv))
        dout = (d * g_ref[...]).astype(BF16)
        dm_ref[...] = _dot_nt(dout, wv).astype(BF16)
        gw_ref[...] += _dot_tn(m, dout)

    tile = pl.BlockSpec((None, tm, D), lambda b, m: (b, m, 0))
    vec = pl.BlockSpec((None, 1, D), lambda b, m: (b, 0, 0))
    full = pl.BlockSpec((D, D), lambda b, m: (0, 0))
    return pl.pallas_call(
        body, name="out_proj_bwd", grid=(nb, s // tm),
        in_specs=[tile, tile, full, vec], out_specs=[tile, full, vec],
        out_shape=[jax.ShapeDtypeStruct(dxo.shape, BF16), jax.ShapeDtypeStruct((D, D), F32),
                   jax.ShapeDtypeStruct((nb, 1, D), F32)],
        compiler_params=_params(("arbitrary", "arbitrary")),
    )(dxo, merged, wout, gate)


def _in_proj_bwd_h(dproj, wg, dep):
    _, nb, s, _ = dproj.shape
    tm = _row_tile(s, 1024)

    def body(dp0_ref, dp1_ref, w0_ref, w1_ref, w2_ref, w3_ref, dep_ref, dh_ref):
        j = pl.program_id(2)
        part = (_dot_nt(dp0_ref[...], jnp.concatenate([w0_ref[...], w1_ref[...]], axis=1))
                + _dot_nt(dp1_ref[...], jnp.concatenate([w2_ref[...], w3_ref[...]], axis=1)))

        @pl.when(j == 0)
        def _():
            dh_ref[...] = part

        @pl.when(j > 0)
        def _():
            dh_ref[...] += part

    def seg(k):
        return pl.BlockSpec((None, None, tm, D), lambda b, m, j: (2 * j + k, b, m, 0))

    def unit(k):
        return pl.BlockSpec((None, D, UNIT),
                            lambda b, m, j: ((4 * j + k) // UNITS_PER_DEV, 0, (4 * j + k) % UNITS_PER_DEV))

    return pl.pallas_call(
        body, name="in_proj_bwd_h", grid=(nb, s // tm, N_SEG // 2),
        in_specs=[seg(0), seg(1), unit(0), unit(1), unit(2), unit(3),
                  pl.BlockSpec((8, LANES), lambda b, m, j: (0, 0))],
        out_specs=pl.BlockSpec((None, tm, D), lambda b, m, j: (b, m, 0)),
        out_shape=jax.ShapeDtypeStruct((nb, s, D), F32),
        compiler_params=_params(("arbitrary", "arbitrary", "arbitrary")),
    )(dproj, dproj, wg, wg, wg, wg, dep)


def _norm_mod_bwd(dh, x, dxo, gain, scale):
    nb, s, _ = x.shape
    tm = _row_tile(s, 512)

    def body(dh_ref, x_ref, dxo_ref, g_ref, sc_ref, dx_ref, dsh_ref, dsc_ref, dg_ref):
        b, m = pl.program_id(0), pl.program_id(1)

        @pl.when((b == 0) & (m == 0))
        def _():
            dg_ref[...] = jnp.zeros_like(dg_ref)

        @pl.when(m == 0)
        def _():
            dsh_ref[...] = jnp.zeros_like(dsh_ref)
            dsc_ref[...] = jnp.zeros_like(dsc_ref)

        dh = dh_ref[...]
        xv = x_ref[...]
        r = lax.rsqrt(jnp.mean(xv * xv, axis=1, keepdims=True) + EPS)
        xn = xv * r
        g = g_ref[...]
        one_sc = 1.0 + sc_ref[...]
        dsh_ref[...] += _rowsum(dh)
        dsc_ref[...] += _rowsum(dh * (xn * g))
        dg_ref[...] += _rowsum(dh * one_sc * xn)
        dxn = dh * (g * one_sc)
        dx_ref[...] = dxo_ref[...] + r * (dxn - xn * jnp.mean(dxn * xn, axis=1, keepdims=True))

    tile = pl.BlockSpec((None, tm, D), lambda b, m: (b, m, 0))
    vec = pl.BlockSpec((None, 1, D), lambda b, m: (b, 0, 0))
    one = pl.BlockSpec((1, D), lambda b, m: (0, 0))
    return pl.pallas_call(
        body, name="norm_mod_bwd", grid=(nb, s // tm),
        in_specs=[tile, tile, tile, one, vec],
        out_specs=[tile, vec, vec, one],
        out_shape=[jax.ShapeDtypeStruct(x.shape, F32), jax.ShapeDtypeStruct((nb, 1, D), F32),
                   jax.ShapeDtypeStruct((nb, 1, D), F32), jax.ShapeDtypeStruct((1, D), F32)],
        compiler_params=_params(("arbitrary", "arbitrary")),
    )(dh, x, dxo, gain, scale)


def _in_proj_bwd_w(h, dproj, dep):
    nb, s, _ = h.shape
    tm = _row_tile(s, 2048)
    n_m = s // tm

    def body(h_ref, dp_ref, dep_ref, o_ref, acc_ref):
        b, m = pl.program_id(1), pl.program_id(2)

        @pl.when((b == 0) & (m == 0))
        def _():
            acc_ref[...] = jnp.zeros_like(acc_ref)

        acc_ref[...] += _dot_tn(h_ref[...], dp_ref[...])

        @pl.when((b == nb - 1) & (m == n_m - 1))
        def _():
            o_ref[0] = acc_ref[:, :UNIT].astype(BF16)
            o_ref[1] = acc_ref[:, UNIT:].astype(BF16)

    return pl.pallas_call(
        body, name="in_proj_bwd_w", grid=(N_SEG, nb, n_m),
        in_specs=[pl.BlockSpec((None, tm, D), lambda j, b, m: (b, m, 0)),
                  pl.BlockSpec((None, None, tm, D), lambda j, b, m: (j, b, m, 0)),
                  pl.BlockSpec((8, LANES), lambda j, b, m: (0, 0))],
        out_specs=pl.BlockSpec((2, D, UNIT), lambda j, b, m: (j, 0, 0)),
        out_shape=jax.ShapeDtypeStruct((2 * N_SEG, D, UNIT), BF16),
        scratch_shapes=[pltpu.VMEM((D, D), F32)],
        compiler_params=_params(("arbitrary", "arbitrary", "arbitrary")),
    )(h, dproj, dep)


def _mod_proj(c_all, w_mod, b_mod_mine):
    nl, _, ncol = w_mod.shape
    nbg = c_all.shape[0]

    def body(c_ref, w_ref, b_ref, o_ref):
        cv = c_ref[...]
        o_ref[...] = jnp.dot(cv * jax.nn.sigmoid(cv), w_ref[...], preferred_element_type=F32,
                             precision=lax.Precision.HIGHEST) + b_ref[...]

    return pl.pallas_call(
        body, name="mod_proj", grid=(nl,),
        in_specs=[pl.BlockSpec((nbg, D), lambda l: (0, 0)), pl.BlockSpec((None, D, ncol), lambda l: (l, 0, 0)),
                  pl.BlockSpec((None, 1, ncol), lambda l: (l, 0, 0))],
        out_specs=pl.BlockSpec((None, nbg, ncol), lambda l: (l, 0, 0)),
        out_shape=jax.ShapeDtypeStruct((nl, nbg, ncol), F32),
        compiler_params=_params(("arbitrary",)),
    )(c_all, w_mod, b_mod_mine)


def _mod_grad(c_all, dmod_all, dmod_mine):
    nl, nbg, ncol = dmod_mine.shape

    def body(c_ref, da_ref, dm_ref, gw_ref, gb_ref):
        cv = c_ref[...]
        gw_ref[...] = lax.dot_general(cv * jax.nn.sigmoid(cv), dm_ref[...], (((0,), (0,)), ((), ())),
                                      preferred_element_type=F32, precision=lax.Precision.HIGHEST)
        gb_ref[...] = _rowsum(da_ref[...])

    return pl.pallas_call(
        body, name="mod_grad", grid=(nl,),
        in_specs=[pl.BlockSpec((nbg, D), lambda l: (0, 0)), pl.BlockSpec((None, nbg, 3 * D), lambda l: (l, 0, 0)),
                  pl.BlockSpec((None, nbg, ncol), lambda l: (l, 0, 0))],
        out_specs=[pl.BlockSpec((None, D, ncol), lambda l: (l, 0, 0)),
                   pl.BlockSpec((None, 1, 3 * D), lambda l: (l, 0, 0))],
        out_shape=[jax.ShapeDtypeStruct((nl, D, ncol), F32), jax.ShapeDtypeStruct((nl, 1, 3 * D), F32)],
        compiler_params=_params(("arbitrary",)),
    )(c_all, dmod_all, dmod_mine)


def _adamw(parts, w, m, v, name, layer=None, prev=None):
    n_parts, n_u, n_r, cu = parts.shape
    assert w.shape[-2:] == (n_r, n_u * cu), (parts.shape, w.shape)
    tr = n_r
    for cand in (512, 256, 128):
        if n_r > cand and n_r % cand == 0:
            tr = cand
            break
    n_prev = 0 if prev is None else 4

    def body(p_ref, w_ref, m_ref, v_ref, *rest):
        g_ref, d_ref, nm_ref, nv_ref = rest[n_prev:]
        g = p_ref[0].astype(F32)
        for k in range(1, n_parts):
            g = g + p_ref[k].astype(F32)
        m2 = ADAM_B1 * m_ref[...] + (1.0 - ADAM_B1) * g
        v2 = ADAM_B2 * v_ref[...] + (1.0 - ADAM_B2) * (g * g)
        m_hat = m2 / (1.0 - ADAM_B1 ** ADAM_STEP)
        v_hat = v2 / (1.0 - ADAM_B2 ** ADAM_STEP)
        g_ref[...] = g
        d_ref[...] = -ADAM_LR * (m_hat / (jnp.sqrt(v_hat) + ADAM_EPS) + ADAM_WD * w_ref[...])
        nm_ref[...] = m2
        nv_ref[...] = v2

    if layer is None:
        tile = pl.BlockSpec((tr, cu), lambda u, i: (i, u))
    else:
        tile = pl.BlockSpec((None, tr, cu), lambda u, i: (layer, i, u))
    shp = jax.ShapeDtypeStruct(w.shape, F32)
    return pl.pallas_call(
        body, name=name, grid=(n_u, n_r // tr),
        in_specs=[pl.BlockSpec((n_parts, None, tr, cu), lambda u, i: (0, u, i, 0)), tile, tile, tile]
        + [pl.BlockSpec(memory_space=pl.ANY)] * n_prev,
        out_specs=[tile, tile, tile, tile], out_shape=[shp, shp, shp, shp],
        input_output_aliases={4 + k: k for k in range(n_prev)},
        compiler_params=_params(("arbitrary", "arbitrary")),
    )(parts, w, m, v, *(prev or ()))


def _gathered_cols(g, inner):
    k = len(inner)
    perm = tuple(range(1, k + 1)) + (0, k + 1)
    t = jnp.transpose(g, perm)
    return t.reshape(tuple(inner) + (g.shape[0] * g.shape[-1],))


def _pair_blocks(wh):
    z = jnp.zeros((8, 64, 64), wh.dtype)
    w2 = wh.reshape(8, 2, 64, 64)
    top = jnp.concatenate([w2[:, 0], z], axis=2)
    bot = jnp.concatenate([z, w2[:, 1]], axis=2)
    return jnp.concatenate([top, bot], axis=1).astype(BF16)


def _unpair_blocks(g):
    return jnp.stack([g[:, :64, :64], g[:, 64:, 64:]], axis=1).reshape(16, 64, 64)


FLAT_ROWS = 512


def _pack_rows(arrays, lead=0):
    parts = [a.reshape(a.shape[:lead] + (-1, LANES)) for a in arrays]
    rows = jnp.concatenate(parts, axis=lead)
    pad = [(0, 0)] * rows.ndim
    pad[lead] = (0, (-rows.shape[lead]) % FLAT_ROWS)
    return jnp.pad(rows, pad)


def kernel(x, c, norm_gain, w_mod, b_mod, w_in, w_out, conv_a_w, sgu_w, sgu_b, lru_conv_w, lru_conv_b, lru_wa, lru_ba, lru_wx, lru_bx, lru_lambda, final_gain, loss_target, m_norm_gain, m_w_mod, m_b_mod, m_w_in, m_w_out, m_conv_a_w, m_sgu_w, m_sgu_b, m_lru_conv_w, m_lru_conv_b, m_lru_wa, m_lru_ba, m_lru_wx, m_lru_bx, m_lru_lambda, m_final_gain, v_norm_gain, v_w_mod, v_b_mod, v_w_in, v_w_out, v_conv_a_w, v_sgu_w, v_sgu_b, v_lru_conv_w, v_lru_conv_b, v_lru_wa, v_lru_ba, v_lru_wx, v_lru_bx, v_lru_lambda, v_final_gain):
    nl = w_in.shape[0]
    nb, s, _ = x.shape
    me = _my_index()
    mod_cols = w_mod.shape[2]


    small = jnp.concatenate([c.reshape(-1, LANES), conv_a_w.reshape(-1, LANES), lru_conv_w.reshape(-1, LANES)])
    n_c, n_ca = nb * D // LANES, nl * 3
    n_small = small.shape[0]
    small = jnp.pad(small, ((0, (-n_small) % 8), (0, 0)))
    small_all, _ = _all_gather(small, "gather_small")
    c_all = small_all[:, :n_c].reshape(N_DEV * nb, D)

    w_in_b = [w_in[0].astype(BF16)] + list(w_in[1:].astype(BF16))

    def start_w_in(l, dep):
        return _split_start(w_in_b[l], _peers_same_core, False, "gather_w_in_start", dep)

    conv_a_full = _gathered_cols(small_all[:, n_c:n_c + n_ca].reshape(N_DEV, nl, 3, LANES), (nl, 3))
    lru_conv_full = _gathered_cols(small_all[:, n_c + n_ca:n_small].reshape(N_DEV, nl, 4, LANES), (nl, 4))

    def gathered_w_in(started, after):
        block, land = _split_wait(started, after, _peers_same_core, False, False, "gather_w_in_wait")
        return _gather_finish(block, land, "gather_w_in_finish")

    sgu_b_lanes = jnp.broadcast_to(sgu_b[..., None], sgu_b.shape + (LANES,))
    mws = []
    for l in range(nl):
        mws.append((conv_a_full[l], sgu_w[l], sgu_b_lanes[l], lru_conv_full[l], lru_conv_b[l][None, :],
                    _pair_blocks(lru_wa[l]), _pair_blocks(lru_wx[l]), lru_ba[l].reshape(1, D),
                    lru_bx[l].reshape(1, D), lru_lambda[l][None, :]))

    rep_names = ["sgu_w", "sgu_b", "lru_conv_b", "lru_wa", "lru_ba", "lru_wx", "lru_bx", "lru_lambda"]
    rep_w = dict(sgu_w=sgu_w, sgu_b=sgu_b, lru_conv_b=lru_conv_b, lru_wa=lru_wa, lru_ba=lru_ba,
                 lru_wx=lru_wx, lru_bx=lru_bx, lru_lambda=lru_lambda)
    rep_m = dict(sgu_w=m_sgu_w, sgu_b=m_sgu_b, lru_conv_b=m_lru_conv_b, lru_wa=m_lru_wa,
                 lru_ba=m_lru_ba, lru_wx=m_lru_wx, lru_bx=m_lru_bx, lru_lambda=m_lru_lambda)
    rep_v = dict(sgu_w=v_sgu_w, sgu_b=v_sgu_b, lru_conv_b=v_lru_conv_b, lru_wa=v_lru_wa,
                 lru_ba=v_lru_ba, lru_wx=v_lru_wx, lru_bx=v_lru_bx, lru_lambda=v_lru_lambda)

    rep_w_all, rep_m_all, rep_v_all = [_pack_rows([src[n] for n in rep_names], lead=1)
                                       for src in (rep_w, rep_m, rep_v)]
    early = [rep_w_all, rep_m_all, rep_v_all] + w_in_b[1:] + [a for mw in mws for a in mw]

    b_mod_mine = lax.dynamic_slice_in_dim(b_mod, me * mod_cols, mod_cols, axis=1)[:, None, :]
    mod_mine = _mod_proj(c_all, w_mod, b_mod_mine)
    mod_all, mod_token = _all_gather(mod_mine.reshape(nl * N_DEV * nb, mod_cols), "gather_mod", dep=w_in_b[0])
    ici = {0: start_w_in(0, mod_token)}
    mod_full = _gathered_cols(mod_all.reshape(N_DEV, nl, N_DEV * nb, mod_cols), (nl, N_DEV * nb))
    mod_loc = lax.dynamic_slice_in_dim(mod_full, me * nb, nb, axis=1)
    shift, scale, gate = [mod_loc[:, :, j * D:(j + 1) * D][:, :, None, :] for j in range(3)]

    xs, hs_bf, projs, mergeds, states, wg = [], [], [], [], [], []
    xl = x
    d2d = {}
    wo_started = _split_start(w_out.astype(BF16).reshape(nl * (D // N_DEV), D), _peers_all, False,
                              "gather_w_out_start", ici[0][4])
    wo = None
    for l in range(nl):
        h = _norm_mod(xl, _after(norm_gain[l][None, :], wo_started[4]), shift[l], scale[l])
        if l == 0:
            wg_l, token = gathered_w_in(ici[0], [h] + early)
            ici[1] = start_w_in(1, token)
            dep = ici[1][4]
        else:
            wg_l = _forward_wait(d2d[l], h, "gather_w_in_d2d_wait")
            dep = d2d[l][4]
        wg.append(wg_l)
        proj = _in_proj(h, wg_l, dep)
        merged, st = _mixer_fwd(proj, mws[l])
        xs.append(xl), hs_bf.append(h), projs.append(proj), mergeds.append(merged), states.append(st)
        gate_l = gate[l]
        if l + 1 < nl:
            block, land = _split_wait(ici[l + 1], merged, _peers_same_core, False, False, "gather_w_in_wait")
            d2d[l + 1] = _forward_start(block, land, "gather_w_in_d2d_start")
            gate_l = _after(gate_l, d2d[l + 1][4])
            if l + 2 < nl:
                ici[l + 2] = start_w_in(l + 2, d2d[l + 1][4])
                gate_l = _after(gate_l, ici[l + 2][4])
        if wo is None:
            _, wo_all = _split_wait(wo_started, merged, _peers_all, False, True, "gather_w_out_wait")
            wo = jnp.transpose(wo_all.reshape(N_DEV, nl, D // N_DEV, D), (1, 0, 2, 3)).reshape(nl, D, D)
        xl = _out_proj(xl, merged, wo[l], gate_l)

    loss_row, dx, g_final = _loss_head(xl, final_gain[None, :], loss_target)
    loss = lax.psum(loss_row[0, 0], ("x", "y", "c"))

    res_big, g_conv = {}, [None] * nl
    dmods = [None] * nl

    def finish_exchange(pending, after):
        l, h_in, h_out, h_rep = pending
        _, recv = _split_wait(h_in, after, _peers_all, True, True, "scatter_w_in_wait")
        res_big["w_in"] = _adamw(recv, w_in, m_w_in, v_w_in, "adamw_w_in", l, res_big.get("w_in"))
        _, recv = _split_wait(h_out, after, _peers_all, True, True, "scatter_w_out_wait")
        res_big["w_out"] = _adamw(recv, w_out, m_w_out, v_w_out, "adamw_w_out", l, res_big.get("w_out"))
        _, recv = _split_wait(h_rep, after, _peers_all, False, True, "gather_rep_wait")
        res_big["rep"] = _adamw(recv[:, None], rep_w_all, rep_m_all, rep_v_all, "adamw_rep", l, res_big.get("rep"))

    pending = None
    g_gains = [None] * nl
    for l in reversed(range(nl)):
        dmerged, gw_out, dgate = _out_proj_bwd(dx, mergeds[l], wo[l], gate[l])
        dproj, g_caw, g_sw, g_sb, g_lcw, g_vec, g_wa, g_wx = _mixer_bwd(projs[l], dmerged, states[l], mws[l])
        g_conv[l] = (g_caw, g_lcw)
        rep_g = dict(
            sgu_w=g_sw, sgu_b=g_sb[:, :, 0], lru_conv_b=g_vec[0],
            lru_wa=_unpair_blocks(g_wa), lru_ba=g_vec[1].reshape(16, 64), lru_wx=_unpair_blocks(g_wx),
            lru_bx=g_vec[2].reshape(16, 64), lru_lambda=g_vec[3])
        rep_block = _pack_rows([rep_g[n] for n in rep_names])
        h_out = _split_start(gw_out.reshape(N_DEV, 1, D // N_DEV, D), _peers_all, True, "scatter_w_out_start")
        h_rep = _split_start(rep_block, _peers_all, False, "gather_rep_start")
        gw_in = _in_proj_bwd_w(hs_bf[l], dproj, h_out[4] + h_rep[4])
        h_in = _split_start(gw_in.reshape(N_DEV, UNITS_PER_DEV, D, UNIT), _peers_all, True, "scatter_w_in_start")
        started = (l, h_in, h_out, h_rep)
        dh = _in_proj_bwd_h(dproj, wg[l], h_in[4])
        dx, dshift, dscale, g_gain = _norm_mod_bwd(dh, xs[l], dx, norm_gain[l][None, :], scale[l])
        g_gains[l] = g_gain
        dmods[l] = jnp.concatenate([dshift, dscale, dgate], axis=2)[:, 0, :]
        if pending is not None:
            finish_exchange(pending, dx)
        pending = started

    conv_parts = jnp.concatenate(
        [jnp.stack([g_conv[l][0] for l in range(nl)]).reshape(nl * 3, N_DEV, LANES),
         jnp.stack([g_conv[l][1] for l in range(nl)]).reshape(nl * 4, N_DEV, LANES)], axis=0)
    conv_parts = jnp.transpose(conv_parts, (1, 0, 2))[:, None]
    conv_recv = _all_to_all(conv_parts, "scatter_conv")

    dmod_loc = jnp.stack(dmods).reshape(nl * nb, 3 * D)
    gain_rows = jnp.pad(jnp.concatenate(g_gains + [g_final], axis=0), ((0, (-(nl + 1)) % 8), (0, 2 * D)))
    tail_g, _ = _all_gather(jnp.concatenate([dmod_loc, gain_rows], axis=0), "gather_dmod",
                            dep=res_big["w_in"][3])
    dmod_g = tail_g[:, :nl * nb]
    gain_parts = tail_g[:, nl * nb:nl * nb + nl + 1, :D][:, None]
    gain_cat = lambda a, b: jnp.concatenate([a, b[None, :]], axis=0)
    res_gain = _adamw(gain_parts, gain_cat(norm_gain, final_gain), gain_cat(m_norm_gain, m_final_gain),
                      gain_cat(v_norm_gain, v_final_gain), "adamw_gain")
    dmod_all = jnp.transpose(dmod_g.reshape(N_DEV, nl, nb, 3 * D), (1, 0, 2, 3)).reshape(nl, N_DEV * nb, 3 * D)
    dmod_mine = lax.dynamic_slice_in_dim(dmod_all, me * mod_cols, mod_cols, axis=2)
    gw_mod, gb_mod = _mod_grad(c_all, dmod_all, dmod_mine)
    res_w_mod = _adamw(gw_mod.reshape(1, 1, nl * D, mod_cols), w_mod.reshape(nl * D, mod_cols),
                       m_w_mod.reshape(nl * D, mod_cols), v_w_mod.reshape(nl * D, mod_cols), "adamw_w_mod")
    res_w_mod = [a.reshape(nl, D, mod_cols) for a in res_w_mod]
    res_b_mod = _adamw(gb_mod.reshape(1, 1, nl, 3 * D), b_mod, m_b_mod, v_b_mod, "adamw_b_mod")
    finish_exchange(pending, res_b_mod[1])

    cat = lambda a, b: jnp.concatenate([a.reshape(nl * 3, LANES), b.reshape(nl * 4, LANES)], axis=0)
    res_conv = _adamw(conv_recv, cat(conv_a_w, lru_conv_w), cat(m_conv_a_w, m_lru_conv_w),
                      cat(v_conv_a_w, v_lru_conv_w), "adamw_conv")
    res_conv_a = [a[:nl * 3].reshape(nl, 3, LANES) for a in res_conv]
    res_lru_conv = [a[nl * 3:].reshape(nl, 4, LANES) for a in res_conv]

    res_rep = []
    for k in range(4):
        off, d = 0, {}
        for n in rep_names:
            n_rows = rep_w[n][0].size // LANES
            d[n] = res_big["rep"][k][:, off:off + n_rows].reshape(rep_w[n].shape)
            off += n_rows
        res_rep.append(d)

    def leaf(k, name):
        if name == "norm_gain":
            return res_gain[k][:nl]
        if name == "final_gain":
            return res_gain[k][nl]
        if name == "w_mod":
            return res_w_mod[k]
        if name == "b_mod":
            return res_b_mod[k]
        if name in ("w_in", "w_out"):
            return res_big[name][k]
        if name == "conv_a_w":
            return res_conv_a[k]
        if name == "lru_conv_w":
            return res_lru_conv[k]
        return res_rep[k][name]

    order = ["norm_gain", "w_mod", "b_mod", "w_in", "w_out", "conv_a_w", "sgu_w", "sgu_b", "lru_conv_w",
             "lru_conv_b", "lru_wa", "lru_ba", "lru_wx", "lru_bx", "lru_lambda", "final_gain"]
    outs = [loss, dx]
    for k in range(4):
        outs += [leaf(k, n) for n in order]
    return tuple(outs)
```

```python
import functools

import jax
import jax.numpy as jnp
from jax import lax
from jax.experimental import pallas as pl
from jax.experimental.pallas import tpu as pltpu

F32 = jnp.float32
BF16 = jnp.bfloat16

D = 1024
N_DEV = 8
N_SEG = 12
LANES = 128
SUBLANES = 8
CHUNK = 128
HALO = 16
UNIT = 512
UNITS_PER_DEV = 3
EPS = 1e-6
LRU_C = 8.0
ADAM_LR, ADAM_B1, ADAM_B2, ADAM_EPS, ADAM_WD, ADAM_STEP = 0.001, 0.9, 0.999, 1e-08, 0.01, 10
VMEM_LIMIT = 56 * 1024 * 1024

AX, AB, AC, AZ, SU, SV, SZ, RX, RZ, GA, GS, GR = range(N_SEG)
MESH = pl.DeviceIdType.MESH


def _params(sem=None):
    return pltpu.CompilerParams(dimension_semantics=sem, vmem_limit_bytes=VMEM_LIMIT)


def _my_index():
    return 4 * lax.axis_index("x") + 2 * lax.axis_index("y") + lax.axis_index("c")


def _all_gather(block, name, dep=None):
    deps = [] if dep is None else list(dep) if isinstance(dep, (list, tuple)) else [dep]

    def body(x_ref, *refs):
        out_ref, token, send_sems, recv_sems, local_sem = refs[-5:]
        x, y, c = lax.axis_index("x"), lax.axis_index("y"), lax.axis_index("c")
        me, sibling = (x, y, c), (x, y, 1 - c)
        chips = [(1 - x, y), (x, 1 - y), (1 - x, 1 - y)]
        token[...] = jnp.zeros_like(token)

        def rows(px, py, pc):
            return out_ref.at[4 * px + 2 * py + pc]

        def copy(k, blk, to, src=None):
            return pltpu.make_async_remote_copy(
                src_ref=rows(*blk) if src is None else src, dst_ref=rows(*blk),
                send_sem=send_sems.at[k], recv_sem=recv_sems.at[k], device_id=to, device_id_type=MESH)

        mine = pltpu.make_async_copy(x_ref, rows(*me), local_sem)
        mine.start()
        first = [copy(0, me, sibling, src=x_ref)]
        first += [copy(1 + j, me, (*chip, c), src=x_ref) for j, chip in enumerate(chips)]
        for cp in first:
            cp.start()
        passed = [copy(4 + j, (*chip, c), sibling) for j, chip in enumerate(chips)]
        for j, chip in enumerate(chips):
            copy(1 + j, (*chip, c), me).wait_recv()
            passed[j].start()
        copy(0, sibling, me).wait_recv()
        for j, chip in enumerate(chips):
            copy(4 + j, (*chip, 1 - c), me).wait_recv()
        for cp in first + passed:
            cp.wait_send()
        mine.wait()

    return pl.pallas_call(
        body, name=name,
        out_shape=[jax.ShapeDtypeStruct((N_DEV,) + block.shape, block.dtype), jax.ShapeDtypeStruct((8, LANES), F32)],
        in_specs=[pl.BlockSpec(memory_space=pltpu.VMEM)]
        + [pl.BlockSpec(memory_space=pl.ANY)] * len(deps),
        out_specs=[pl.BlockSpec(memory_space=pl.ANY), pl.BlockSpec(memory_space=pltpu.VMEM)],
        scratch_shapes=[pltpu.SemaphoreType.DMA((7,)), pltpu.SemaphoreType.DMA((7,)), pltpu.SemaphoreType.DMA],
    )(block, *deps)


def _all_to_all(blocks, name):
    def body(x_ref, out_ref, send_sems, recv_sems, local_sem):
        x, y, c = lax.axis_index("x"), lax.axis_index("y"), lax.axis_index("c")
        my = 4 * x + 2 * y + c
        mine = pltpu.make_async_copy(x_ref.at[my], out_ref.at[my], local_sem)
        mine.start()
        peers = []
        for r in range(1, N_DEV):
            px = 1 - x if r & 4 else x
            py = 1 - y if r & 2 else y
            pc = 1 - c if r & 1 else c
            peers.append((r - 1, 4 * px + 2 * py + pc, (px, py, pc)))

        def copy(k, src_slot, dst_slot, to):
            return pltpu.make_async_remote_copy(
                src_ref=x_ref.at[src_slot], dst_ref=out_ref.at[dst_slot],
                send_sem=send_sems.at[k], recv_sem=recv_sems.at[k], device_id=to, device_id_type=MESH)

        sends = [copy(k, pid, my, to) for k, pid, to in peers]
        for cp in sends:
            cp.start()
        for k, pid, to in peers:
            copy(k, pid, pid, to).wait_recv()
        for cp in sends:
            cp.wait_send()
        mine.wait()

    return pl.pallas_call(
        body, name=name,
        out_shape=jax.ShapeDtypeStruct(blocks.shape, blocks.dtype),
        in_specs=[pl.BlockSpec(memory_space=pltpu.VMEM)],
        out_specs=pl.BlockSpec(memory_space=pl.ANY),
        scratch_shapes=[pltpu.SemaphoreType.DMA((7,)), pltpu.SemaphoreType.DMA((7,)), pltpu.SemaphoreType.DMA],
    )(blocks)


_HBM = pl.BlockSpec(memory_space=pltpu.HBM)
_SEM = pl.BlockSpec(memory_space=pltpu.SEMAPHORE)
_EFFECT = pltpu.SideEffectType.DATAFLOW_SIDE_EFFECTING


def _peers_all(x, y, c):
    out = []
    for r in range(1, N_DEV):
        px = 1 - x if r & 4 else x
        py = 1 - y if r & 2 else y
        pc = 1 - c if r & 1 else c
        out.append((r - 1, 4 * px + 2 * py + pc, (px, py, pc)))
    return out


def _peers_same_core(x, y, c):
    return [(k, 4 * px + 2 * py + c, (px, py, c))
            for k, (px, py) in enumerate([(1 - x, y), (x, 1 - y), (1 - x, 1 - y)])]


def _split_start(src, peers_fn, scatter, name, dep=None):
    blk = src.shape[1:] if scatter else src.shape
    land_shape = (N_DEV,) + tuple(blk)
    n = len(peers_fn(0, 0, 0))
    deps = [] if dep is None else [dep]

    def body(x_ref, land_ref, *rest):
        send_sems, recv_sems, x_thru, land_thru, token = rest[len(deps):]
        x, y, c = lax.axis_index("x"), lax.axis_index("y"), lax.axis_index("c")
        my = 4 * x + 2 * y + c
        for k, pid, to in peers_fn(x, y, c):
            pltpu.make_async_remote_copy(
                src_ref=x_ref.at[pid] if scatter else x_ref, dst_ref=land_ref.at[my],
                send_sem=send_sems.at[k], recv_sem=recv_sems.at[k], device_id=to, device_id_type=MESH).start()
        token[...] = jnp.zeros_like(token)

    return pl.pallas_call(
        body, name=name,
        out_shape=(pltpu.SemaphoreType.DMA((n,)), pltpu.SemaphoreType.DMA((n,)),
                   pltpu.HBM(src.shape, src.dtype), pltpu.HBM(land_shape, src.dtype),
                   jax.ShapeDtypeStruct((8, LANES), F32)),
        in_specs=(_HBM, _HBM) + (pl.BlockSpec(memory_space=pl.ANY),) * len(deps),
        out_specs=(_SEM, _SEM, _HBM, _HBM, pl.BlockSpec(memory_space=pltpu.VMEM)),
        input_output_aliases={0: 2, 1: 3},
        compiler_params=pltpu.CompilerParams(has_side_effects=_EFFECT),
    )(pltpu.with_memory_space_constraint(src, pltpu.HBM),
      pltpu.with_memory_space_constraint(lax.empty(land_shape, src.dtype), pltpu.HBM), *deps)


def _split_wait(handles, after, peers_fn, scatter, own, name):
    send_sems, recv_sems, src_thru, land_thru, _ = handles
    blk = land_thru.shape[1:]
    after = list(after) if isinstance(after, (list, tuple)) else [after]

    def body(x_ref, land_ref, send_sems, recv_sems, *rest):
        stage = rest[len(after) + 2:]
        x, y, c = lax.axis_index("x"), lax.axis_index("y"), lax.axis_index("c")
        if own:
            my = 4 * x + 2 * y + c
            mine = _staged_copy(x_ref.at[my] if scatter else x_ref, land_ref.at[my], *stage)
        for k, pid, to in peers_fn(x, y, c):
            cp = pltpu.make_async_remote_copy(
                src_ref=x_ref.at[pid] if scatter else x_ref, dst_ref=land_ref.at[pid],
                send_sem=send_sems.at[k], recv_sem=recv_sems.at[k], device_id=to, device_id_type=MESH)
            cp.wait_send()
            cp.wait_recv()
        if own:
            mine.wait()

    return pl.pallas_call(
        body, name=name,
        out_shape=(pltpu.HBM(src_thru.shape, src_thru.dtype), pltpu.HBM(land_thru.shape, land_thru.dtype)),
        in_specs=(_HBM, _HBM, _SEM, _SEM) + (pl.BlockSpec(memory_space=pl.ANY),) * len(after),
        out_specs=(_HBM, _HBM),
        input_output_aliases={0: 0, 1: 1},
        scratch_shapes=[pltpu.VMEM(blk, land_thru.dtype), pltpu.SemaphoreType.DMA((2,))] if own else [],
        compiler_params=pltpu.CompilerParams(has_side_effects=_EFFECT, vmem_limit_bytes=VMEM_LIMIT),
    )(src_thru, land_thru, send_sems, recv_sems, *after)


def _staged_copy(src_ref, dst_ref, buf, sems):
    leg = pltpu.make_async_copy(src_ref, buf, sems.at[0])
    leg.start()
    leg.wait()
    leg = pltpu.make_async_copy(buf, dst_ref, sems.at[1])
    leg.start()
    return leg


def _gather_finish(block, land, name):
    def body(x_ref, land_ref, out_ref, token, send_sems, recv_sems, buf, local_sems):
        x, y, c = lax.axis_index("x"), lax.axis_index("y"), lax.axis_index("c")
        my, sib_id, sibling = 4 * x + 2 * y + c, 4 * x + 2 * y + 1 - c, (x, y, 1 - c)
        token[...] = jnp.zeros_like(token)

        def copy(k, slot, src=None):
            return pltpu.make_async_remote_copy(
                src_ref=land_ref.at[slot] if src is None else src, dst_ref=out_ref.at[slot],
                send_sem=send_sems.at[k], recv_sem=recv_sems.at[k], device_id=sibling, device_id_type=MESH)

        chips = _peers_same_core(x, y, c)
        sends = [copy(0, my, src=x_ref)] + [copy(1 + k, pid) for k, pid, _ in chips]
        for cp in sends:
            cp.start()
        mine = _staged_copy(x_ref, out_ref.at[my], buf, local_sems)
        copy(0, sib_id).wait_recv()
        for k, pid, _ in chips:
            copy(1 + k, pid + 1 - 2 * c).wait_recv()
        for cp in sends:
            cp.wait_send()
        mine.wait()

    return pl.pallas_call(
        body, name=name,
        out_shape=[jax.ShapeDtypeStruct(land.shape, land.dtype), jax.ShapeDtypeStruct((8, LANES), F32)],
        in_specs=[pl.BlockSpec(memory_space=pl.ANY), pl.BlockSpec(memory_space=pl.ANY)],
        out_specs=[pl.BlockSpec(memory_space=pl.ANY), pl.BlockSpec(memory_space=pltpu.VMEM)],
        input_output_aliases={1: 0},
        scratch_shapes=[pltpu.SemaphoreType.DMA((4,)), pltpu.SemaphoreType.DMA((4,)),
                        pltpu.VMEM(block.shape, block.dtype), pltpu.SemaphoreType.DMA((2,))],
        compiler_params=pltpu.CompilerParams(vmem_limit_bytes=VMEM_LIMIT),
    )(block, land)


def _forward_start(block, land, name):
    def body(x_ref, land_ref, send_sems, recv_sems, x_thru, land_thru, token):
        x, y, c = lax.axis_index("x"), lax.axis_index("y"), lax.axis_index("c")
        my, sibling = 4 * x + 2 * y + c, (x, y, 1 - c)
        slots = [(0, my, x_ref)] + [(1 + k, pid, land_ref.at[pid]) for k, pid, _ in _peers_same_core(x, y, c)]
        for k, slot, src in slots:
            pltpu.make_async_remote_copy(
                src_ref=src, dst_ref=land_ref.at[slot], send_sem=send_sems.at[k], recv_sem=recv_sems.at[k],
                device_id=sibling, device_id_type=MESH).start()
        token[...] = jnp.zeros_like(token)

    return pl.pallas_call(
        body, name=name,
        out_shape=(pltpu.SemaphoreType.DMA((4,)), pltpu.SemaphoreType.DMA((4,)),
                   pltpu.HBM(block.shape, block.dtype), pltpu.HBM(land.shape, land.dtype),
                   jax.ShapeDtypeStruct((8, LANES), F32)),
        in_specs=(_HBM, _HBM),
        out_specs=(_SEM, _SEM, _HBM, _HBM, pl.BlockSpec(memory_space=pltpu.VMEM)),
        input_output_aliases={0: 2, 1: 3},
        compiler_params=pltpu.CompilerParams(has_side_effects=_EFFECT),
    )(block, land)


def _forward_wait(handles, after, name):
    send_sems, recv_sems, block_thru, land_thru, _ = handles

    def body(x_ref, land_ref, send_sems, recv_sems, after_ref, x_dead, got_ref, buf, local_sems):
        x, y, c = lax.axis_index("x"), lax.axis_index("y"), lax.axis_index("c")
        my, sib_id, sibling = 4 * x + 2 * y + c, 4 * x + 2 * y + 1 - c, (x, y, 1 - c)
        mine = _staged_copy(x_ref, land_ref.at[my], buf, local_sems)
        slots = [(0, my, sib_id)] + [(1 + k, pid, pid + 1 - 2 * c) for k, pid, _ in _peers_same_core(x, y, c)]
        for k, sent, got in slots:
            cp = pltpu.make_async_remote_copy(
                src_ref=land_ref.at[sent], dst_ref=land_ref.at[got], send_sem=send_sems.at[k],
                recv_sem=recv_sems.at[k], device_id=sibling, device_id_type=MESH)
            cp.wait_send()
            cp.wait_recv()
        mine.wait()

    return pl.pallas_call(
        body, name=name,
        out_shape=(pltpu.HBM(block_thru.shape, block_thru.dtype), pltpu.HBM(land_thru.shape, land_thru.dtype)),
        in_specs=(_HBM, _HBM, _SEM, _SEM, pl.BlockSpec(memory_space=pl.ANY)),
        out_specs=(_HBM, _HBM),
        input_output_aliases={0: 0, 1: 1},
        scratch_shapes=[pltpu.VMEM(block_thru.shape, block_thru.dtype), pltpu.SemaphoreType.DMA((2,))],
        compiler_params=pltpu.CompilerParams(has_side_effects=_EFFECT, vmem_limit_bytes=VMEM_LIMIT),
    )(block_thru, land_thru, send_sems, recv_sems, after)[1]


def _after(v, token):
    return v + token[0, 0].astype(v.dtype)


def _dsilu(x, s):
    return s * (1.0 + x * (1.0 - s))


def _log1p(x):
    u = 1.0 + x
    d = u - 1.0
    return jnp.where(d == 0.0, x, jnp.log(u) * (x / jnp.where(d == 0.0, 1.0, d)))


def _softplus_neg(lam):
    return jnp.maximum(-lam, 0.0) + _log1p(jnp.exp(-jnp.abs(lam)))


def _neg_expm1(y, exp_y):
    poly = -y * (1.0 + y * (0.5 + y * (1.0 / 6.0 + y * (1.0 / 24.0))))
    return jnp.where(y > -0.05, poly, 1.0 - exp_y)


def _sigmoid(x):
    return 0.5 * jnp.tanh(0.5 * x) + 0.5


def _shift_dn(cur, prev, k):
    ext = jnp.concatenate([prev, cur], axis=0)
    return pltpu.roll(ext, k, 0)[HALO:, :]


def _shift_up(cur, nxt, k):
    n = cur.shape[0]
    ext = jnp.concatenate([cur, nxt], axis=0)
    return pltpu.roll(ext, n + HALO - k, 0)[:n, :]


def _scan_fwd(a, b, h_prev):
    groups = a.shape[0] // SUBLANES
    a3 = a.reshape(groups, SUBLANES, LANES)
    b3 = b.reshape(groups, SUBLANES, LANES)
    row = lax.broadcasted_iota(jnp.int32, a3.shape, 1)
    k = 1
    while k < SUBLANES:
        a_sh = jnp.where(row >= k, pltpu.roll(a3, k, 1), 1.0)
        b_sh = jnp.where(row >= k, pltpu.roll(b3, k, 1), 0.0)
        b3 = a3 * b_sh + b3
        a3 = a3 * a_sh
        k *= 2
    carry = h_prev[HALO - 1:HALO, :]
    out = []
    for i in range(groups):
        hg = b3[i] + a3[i] * carry
        out.append(hg)
        carry = hg[SUBLANES - 1:SUBLANES, :]
    return jnp.concatenate(out, axis=0)


def _scan_rev(a_next, g, lam_next):
    groups = g.shape[0] // SUBLANES
    a3 = a_next.reshape(groups, SUBLANES, LANES)
    g3 = g.reshape(groups, SUBLANES, LANES)
    row = lax.broadcasted_iota(jnp.int32, a3.shape, 1)
    k = 1
    while k < SUBLANES:
        ok = row < SUBLANES - k
        a_sh = jnp.where(ok, pltpu.roll(a3, SUBLANES - k, 1), 1.0)
        g_sh = jnp.where(ok, pltpu.roll(g3, SUBLANES - k, 1), 0.0)
        g3 = g3 + a3 * g_sh
        a3 = a3 * a_sh
        k *= 2
    carry = lam_next[0:1, :]
    out = [None] * groups
    for i in reversed(range(groups)):
        lg = g3[i] + a3[i] * carry
        out[i] = lg
        carry = lg[0:1, :]
    return jnp.concatenate(out, axis=0)


def _rowsum(v):
    return jnp.sum(v, axis=0, keepdims=True)


def _dot(a, b):
    return jnp.dot(a, b, preferred_element_type=F32)


def _dot_nt(a, b):
    return lax.dot_general(a, b, (((1,), (1,)), ((), ())), preferred_element_type=F32)


def _dot_tn(a, b):
    return lax.dot_general(a, b, (((0,), (0,)), ((), ())), preferred_element_type=F32)


class _MixerWeights:
    def __init__(self, caw_ref, sw_ref, sb_ref, lcw_ref, lcb_ref, wa_ref, wx_ref, ba_ref, bx_ref, lam_ref):
        self.caw = [caw_ref[j:j + 1, :] for j in range(3)]
        self.lcw = [lcw_ref[j:j + 1, :] for j in range(4)]
        self.lcb = lcb_ref[...]
        row = lax.broadcasted_iota(jnp.int32, (CHUNK, CHUNK), 0)
        col = lax.broadcasted_iota(jnp.int32, (CHUNK, CHUNK), 1)
        self.tril = col <= row
        self.sw = jnp.where(self.tril, sw_ref[...], 0.0).astype(BF16)
        self.sb = sb_ref[...]
        self.wa = wa_ref[...]
        self.wx = wx_ref[...]
        self.ba = ba_ref[...]
        self.bx = bx_ref[...]
        lam = lam_ref[...]
        self.neg_c_sp = -LRU_C * _softplus_neg(lam)
        self.dsp_dlam = -_sigmoid(-lam)


def _mixer_a(ld, ldp, w):
    t = {}
    a_x, a_c = ld(AX), ld(AC)
    t["a_x"], t["a_c"], t["a_b"], t["a_z"] = a_x, a_c, ld(AB), ld(AZ)
    ca = a_c * a_x
    ca_p = ldp(AC) * ldp(AX)
    t["ca"], t["ca1"], t["ca2"] = ca, _shift_dn(ca, ca_p, 1), _shift_dn(ca, ca_p, 2)
    t["cv"] = w.caw[2] * ca + w.caw[1] * t["ca1"] + w.caw[0] * t["ca2"]
    t["sa"] = _sigmoid(t["a_z"])
    t["silu_az"] = t["a_z"] * t["sa"]
    t["y_a"] = t["silu_az"] * t["a_b"] * t["cv"]
    return t


def _mixer_b(ld, w):
    t = {}
    v = ld(SV)
    vc = v - jnp.mean(v, axis=1, keepdims=True)
    t["rstd"] = lax.rsqrt(jnp.mean(vc * vc, axis=1, keepdims=True) + EPS)
    t["vn"] = vc * t["rstd"]
    t["z"] = _dot(w.sw, t["vn"].astype(BF16)) + w.sb
    t["s_u"], t["s_z"] = ld(SU), ld(SZ)
    t["ss"] = _sigmoid(t["s_z"])
    t["silu_sz"] = t["s_z"] * t["ss"]
    t["y_s"] = t["silu_sz"] * t["s_u"] * t["z"]
    return t


def _mixer_c(ld, ldp, w, backward):
    t = {}
    r_x, r_xp = ld(RX), ldp(RX)
    t["rx"] = [_shift_dn(r_x, r_xp, 3), _shift_dn(r_x, r_xp, 2), _shift_dn(r_x, r_xp, 1), r_x]
    xc = w.lcb + w.lcw[0] * t["rx"][0] + w.lcw[1] * t["rx"][1] + w.lcw[2] * t["rx"][2] + w.lcw[3] * r_x
    t["xc"] = xc
    xcb = xc.astype(BF16)
    t["r"] = _sigmoid(_dot(xcb, w.wa) + w.ba)
    t["i"] = _sigmoid(_dot(xcb, w.wx) + w.bx)
    la = t["r"] * w.neg_c_sp
    t["a"] = jnp.exp(la)
    t["em"] = _neg_expm1(2.0 * la, t["a"] * t["a"])
    if backward:
        t["inv_mult"] = lax.rsqrt(t["em"])
        t["mult"] = t["em"] * t["inv_mult"]
    else:
        t["mult"] = jnp.sqrt(t["em"])
    t["b"] = t["mult"] * (t["i"] * xc)
    t["r_z"] = ld(RZ)
    t["sr"] = _sigmoid(t["r_z"])
    t["silu_rz"] = t["r_z"] * t["sr"]
    return t


def _mixer_pre_scan(ld, ldp, w, backward):
    t = {**_mixer_a(ld, ldp, w), **_mixer_b(ld, w), **_mixer_c(ld, ldp, w, backward)}
    t["ga"], t["gs"], t["gr"] = _sigmoid(ld(GA)), _sigmoid(ld(GS)), _sigmoid(ld(GR))
    return t


def _weight_specs(n_cb_axis):
    def at(fn):
        return lambda *g: fn(g[n_cb_axis])
    return [
        pl.BlockSpec((3, LANES), at(lambda cb: (0, cb))),
        pl.BlockSpec((None, CHUNK, CHUNK), at(lambda cb: (cb, 0, 0))),
        pl.BlockSpec((None, CHUNK, LANES), at(lambda cb: (cb, 0, 0))),
        pl.BlockSpec((4, LANES), at(lambda cb: (0, cb))),
        pl.BlockSpec((1, LANES), at(lambda cb: (0, cb))),
        pl.BlockSpec((None, LANES, LANES), at(lambda cb: (cb, 0, 0))),
        pl.BlockSpec((None, LANES, LANES), at(lambda cb: (cb, 0, 0))),
        pl.BlockSpec((1, LANES), at(lambda cb: (0, cb))),
        pl.BlockSpec((1, LANES), at(lambda cb: (0, cb))),
        pl.BlockSpec((1, LANES), at(lambda cb: (0, cb))),
    ]


def _chunk_loaders(p_ref, c):
    r0 = pl.multiple_of(c * CHUNK, CHUNK)
    rp = pl.multiple_of(jnp.maximum(c * CHUNK - HALO, 0), HALO)

    def ld(j):
        return p_ref[j, pl.ds(r0, CHUNK), :].astype(F32)

    def ldp(j):
        return jnp.where(c > 0, p_ref[j, pl.ds(rp, HALO), :].astype(F32), 0.0)

    return r0, rp, ld, ldp


def _mixer_fwd(proj, mw):
    _, nb, s, _ = proj.shape
    n_chunks = s // CHUNK

    def body(p_ref, *refs):
        w = _MixerWeights(*refs[:10])
        merged_ref, hs_ref = refs[10:]

        def chunk(c, h_prev):
            r0, _, ld, ldp = _chunk_loaders(p_ref, c)
            t = _mixer_pre_scan(ld, ldp, w, False)
            h = _scan_fwd(t["a"], t["b"], h_prev)
            y_r = t["silu_rz"] * h
            merged = t["ga"] * t["y_a"] + t["gs"] * t["y_s"] + t["gr"] * y_r
            merged_ref[pl.ds(r0, CHUNK), :] = merged.astype(BF16)
            hs_ref[pl.ds(r0, CHUNK), :] = h
            return h[CHUNK - HALO:, :]

        lax.fori_loop(0, n_chunks, chunk, jnp.zeros((HALO, LANES), F32))

    slab = pl.BlockSpec((None, s, LANES), lambda cb, b: (b, 0, cb))
    return pl.pallas_call(
        body, name="mixer_fwd", grid=(D // LANES, nb),
        in_specs=[pl.BlockSpec((N_SEG, None, s, LANES), lambda cb, b: (0, b, 0, cb))] + _weight_specs(0),
        out_specs=[slab, slab],
        out_shape=[jax.ShapeDtypeStruct((nb, s, D), BF16), jax.ShapeDtypeStruct((nb, s, D), F32)],
        compiler_params=_params(("arbitrary", "arbitrary")),
    )(proj, *mw)


def _mixer_bwd(proj, dmerged, hs, mw):
    _, nb, s, _ = proj.shape
    n_chunks = s // CHUNK

    def body(p_ref, dm_ref, hs_ref, *refs):
        w = _MixerWeights(*refs[:10])
        dp_ref, g_caw, g_sw, g_sb, g_lcw, g_vec, g_wa, g_wx = refs[10:]

        @pl.when(pl.program_id(1) == 0)
        def _():
            for ref in (g_caw, g_sw, g_sb, g_lcw, g_vec, g_wa, g_wx):
                ref[...] = jnp.zeros_like(ref)

        def chunk(i, carry):
            dcv_n, dxc_n, lam_n, a_n = carry
            c = n_chunks - 1 - i
            r0, rp, ld, ldp = _chunk_loaders(p_ref, c)
            t = _mixer_pre_scan(ld, ldp, w, True)
            h = hs_ref[pl.ds(r0, CHUNK), :]
            h_p = jnp.where(c > 0, hs_ref[pl.ds(rp, HALO), :], 0.0)
            h_prev = _shift_dn(h, h_p, 1)
            dm = dm_ref[pl.ds(r0, CHUNK), :].astype(F32)
            y_r = t["silu_rz"] * h

            def out(j, val):
                dp_ref[j, pl.ds(r0, CHUNK), :] = val.astype(BF16)

            ga, gs, gr = t["ga"], t["gs"], t["gr"]
            out(GA, dm * t["y_a"] * ga * (1.0 - ga))
            out(GS, dm * t["y_s"] * gs * (1.0 - gs))
            out(GR, dm * y_r * gr * (1.0 - gr))

            dy_a = dm * ga
            out(AZ, dy_a * t["a_b"] * t["cv"] * _dsilu(t["a_z"], t["sa"]))
            out(AB, dy_a * t["silu_az"] * t["cv"])
            dcv = dy_a * t["silu_az"] * t["a_b"]
            dca = w.caw[2] * dcv + w.caw[1] * _shift_up(dcv, dcv_n, 1) + w.caw[0] * _shift_up(dcv, dcv_n, 2)
            out(AC, dca * t["a_x"])
            out(AX, dca * t["a_c"])
            g_caw[2:3, :] += _rowsum(dcv * t["ca"])
            g_caw[1:2, :] += _rowsum(dcv * t["ca1"])
            g_caw[0:1, :] += _rowsum(dcv * t["ca2"])

            dy_s = dm * gs
            out(SZ, dy_s * t["s_u"] * t["z"] * _dsilu(t["s_z"], t["ss"]))
            out(SU, dy_s * t["silu_sz"] * t["z"])
            dz = dy_s * t["silu_sz"] * t["s_u"]
            dzb = dz.astype(BF16)
            g_sb[...] += jnp.broadcast_to(jnp.sum(dz, axis=1, keepdims=True), (CHUNK, LANES))
            g_sw[...] += _dot_nt(dzb, t["vn"].astype(BF16))
            dvn = _dot_tn(w.sw, dzb)
            vn = t["vn"]
            out(SV, t["rstd"] * (dvn - jnp.mean(dvn, axis=1, keepdims=True)
                                 - vn * jnp.mean(dvn * vn, axis=1, keepdims=True)))

            dy_r = dm * gr
            out(RZ, dy_r * h * _dsilu(t["r_z"], t["sr"]))
            lam = _scan_rev(_shift_up(t["a"], a_n, 1), dy_r * t["silu_rz"], lam_n)
            a, r, ig, xc, mult = t["a"], t["r"], t["i"], t["xc"], t["mult"]
            d_i = lam * mult * xc
            d_mult = lam * ig * xc
            dxc = lam * mult * ig
            dla = lam * h_prev * a - d_mult * ((1.0 - t["em"]) * t["inv_mult"])
            g_vec[3:4, :] += _rowsum(dla * r) * (-LRU_C * w.dsp_dlam)
            dpr = (dla * w.neg_c_sp) * r * (1.0 - r)
            dpi = d_i * ig * (1.0 - ig)
            dprb, dpib, xcb = dpr.astype(BF16), dpi.astype(BF16), xc.astype(BF16)
            g_wa[...] += _dot_tn(xcb, dprb)
            g_wx[...] += _dot_tn(xcb, dpib)
            g_vec[1:2, :] += _rowsum(dpr)
            g_vec[2:3, :] += _rowsum(dpi)
            dxc = dxc + _dot_nt(dprb, w.wa) + _dot_nt(dpib, w.wx)
            g_vec[0:1, :] += _rowsum(dxc)
            out(RX, w.lcw[3] * dxc + w.lcw[2] * _shift_up(dxc, dxc_n, 1)
                + w.lcw[1] * _shift_up(dxc, dxc_n, 2) + w.lcw[0] * _shift_up(dxc, dxc_n, 3))
            for j in range(4):
                g_lcw[j:j + 1, :] += _rowsum(dxc * t["rx"][j])
            return dcv[:HALO, :], dxc[:HALO, :], lam[:HALO, :], a[:HALO, :]

        zero = jnp.zeros((HALO, LANES), F32)
        lax.fori_loop(0, n_chunks, chunk, (zero, zero, zero, zero))

        @pl.when(pl.program_id(1) == nb - 1)
        def _():
            g_sw[...] = jnp.where(w.tril, g_sw[...], 0.0)

    slab = lambda dt: pl.BlockSpec((None, s, LANES), lambda cb, b: (b, 0, cb))
    seg = pl.BlockSpec((N_SEG, None, s, LANES), lambda cb, b: (0, b, 0, cb))
    rows = lambda n: pl.BlockSpec((n, LANES), lambda cb, b: (0, cb))
    sq = pl.BlockSpec((None, LANES, LANES), lambda cb, b: (cb, 0, 0))
    n_cb = D // LANES
    return pl.pallas_call(
        body, name="mixer_bwd", grid=(n_cb, nb),
        in_specs=[seg, slab(BF16), slab(F32)] + _weight_specs(0),
        out_specs=[seg, rows(3), sq, sq, rows(4), rows(8), sq, sq],
        out_shape=[
            jax.ShapeDtypeStruct(proj.shape, BF16),
            jax.ShapeDtypeStruct((3, D), F32),
            jax.ShapeDtypeStruct((n_cb, CHUNK, CHUNK), F32),
            jax.ShapeDtypeStruct((n_cb, CHUNK, LANES), F32),
            jax.ShapeDtypeStruct((4, D), F32),
            jax.ShapeDtypeStruct((8, D), F32),
            jax.ShapeDtypeStruct((n_cb, LANES, LANES), F32),
            jax.ShapeDtypeStruct((n_cb, LANES, LANES), F32),
        ],
        compiler_params=_params(("arbitrary", "arbitrary")),
    )(proj, dmerged, hs, *mw)


def _row_tile(s, want):
    return want if s % want == 0 else s


def _norm_mod(x, gain, shift, scale):
    nb, s, _ = x.shape
    tm = _row_tile(s, 512)

    def body(x_ref, g_ref, sh_ref, sc_ref, h_ref):
        xv = x_ref[...]
        r = lax.rsqrt(jnp.mean(xv * xv, axis=1, keepdims=True) + EPS)
        h_ref[...] = ((xv * r) * g_ref[...] * (1.0 + sc_ref[...]) + sh_ref[...]).astype(BF16)

    tile = pl.BlockSpec((None, tm, D), lambda b, m: (b, m, 0))
    vec = pl.BlockSpec((None, 1, D), lambda b, m: (b, 0, 0))
    return pl.pallas_call(
        body, name="norm_mod", grid=(nb, s // tm),
        in_specs=[tile, pl.BlockSpec((1, D), lambda b, m: (0, 0)), vec, vec],
        out_specs=tile, out_shape=jax.ShapeDtypeStruct(x.shape, BF16),
        compiler_params=_params(("arbitrary", "arbitrary")),
    )(x, gain, shift, scale)


def _in_proj(h, wg, dep):
    nb, s, _ = h.shape

    def body(h_ref, w_ref, dep_ref, o_ref):
        o_ref[...] = _dot(h_ref[...], w_ref[...]).astype(BF16)

    return pl.pallas_call(
        body, name="in_proj", grid=(nb, N_DEV * UNITS_PER_DEV),
        in_specs=[pl.BlockSpec((None, s, D), lambda b, u: (b, 0, 0)),
                  pl.BlockSpec((None, D, UNIT), lambda b, u: (u // UNITS_PER_DEV, 0, u % UNITS_PER_DEV)),
                  pl.BlockSpec((8, LANES), lambda b, u: (0, 0))],
        out_specs=pl.BlockSpec((None, None, s, UNIT), lambda b, u: (u // 2, b, 0, u % 2)),
        out_shape=jax.ShapeDtypeStruct((N_SEG, nb, s, D), BF16),
        compiler_params=_params(("arbitrary", "arbitrary")),
    )(h, wg, dep)


def _out_proj(x, merged, wout, gate):
    nb, s, _ = x.shape
    tm = _row_tile(s, 512)

    def body(x_ref, m_ref, w_ref, g_ref, o_ref):
        o_ref[...] = x_ref[...] + g_ref[...] * _dot(m_ref[...], w_ref[...])

    tile = pl.BlockSpec((None, tm, D), lambda b, m: (b, m, 0))
    return pl.pallas_call(
        body, name="out_proj", grid=(nb, s // tm),
        in_specs=[tile, tile, pl.BlockSpec((D, D), lambda b, m: (0, 0)),
                  pl.BlockSpec((None, 1, D), lambda b, m: (b, 0, 0))],
        out_specs=tile, out_shape=jax.ShapeDtypeStruct(x.shape, F32),
        compiler_params=_params(("arbitrary", "arbitrary")),
    )(x, merged, wout, gate)


def _loss_head(x, gain, target):
    nb, s, _ = x.shape
    tm = _row_tile(s, 512)

    def body(x_ref, g_ref, t_ref, loss_ref, dx_ref, dg_ref):
        first = (pl.program_id(0) == 0) & (pl.program_id(1) == 0)
        last = (pl.program_id(0) == nb - 1) & (pl.program_id(1) == s // tm - 1)

        @pl.when(first)
        def _():
            loss_ref[...] = jnp.zeros_like(loss_ref)
            dg_ref[...] = jnp.zeros_like(dg_ref)

        xv = x_ref[...]
        r = lax.rsqrt(jnp.mean(xv * xv, axis=1, keepdims=True) + EPS)
        xn = xv * r
        g = g_ref[...]
        e = xn * g - t_ref[...]
        loss_ref[...] += _rowsum(e * e) * (0.5 / D)
        dy = e * (1.0 / D)
        dg_ref[...] += _rowsum(dy * xn)
        dxn = dy * g
        dx_ref[...] = r * (dxn - xn * jnp.mean(dxn * xn, axis=1, keepdims=True))

        @pl.when(last)
        def _():
            loss_ref[...] = jnp.broadcast_to(jnp.sum(loss_ref[...], axis=1, keepdims=True), (1, D))

    tile = pl.BlockSpec((None, tm, D), lambda b, m: (b, m, 0))
    vec = pl.BlockSpec((1, D), lambda b, m: (0, 0))
    return pl.pallas_call(
        body, name="loss_head", grid=(nb, s // tm),
        in_specs=[tile, vec, tile], out_specs=[vec, tile, vec],
        out_shape=[jax.ShapeDtypeStruct((1, D), F32), jax.ShapeDtypeStruct(x.shape, F32),
                   jax.ShapeDtypeStruct((1, D), F32)],
        compiler_params=_params(("arbitrary", "arbitrary")),
    )(x, gain, target)


def _out_proj_bwd(dxo, merged, wout, gate):
    nb, s, _ = dxo.shape
    tm = _row_tile(s, 512)

    def body(d_ref, m_ref, w_ref, g_ref, dm_ref, gw_ref, dg_ref):
        @pl.when((pl.program_id(0) == 0) & (pl.program_id(1) == 0))
        def _():
            gw_ref[...] = jnp.zeros_like(gw_ref)

        @pl.when(pl.program_id(1) == 0)
        def _():
            dg_ref[...] = jnp.zeros_like(dg_ref)

        d = d_ref[...]
        m = m_ref[...]
        wv = w_ref[...]
        dg_ref[...] += _rowsum(d * _dot(m, wv))
        dout = (d * g_ref[...]).astype(BF16)
        dm_ref[...] = _dot_nt(dout, wv).astype(BF16)
        gw_ref[...] += _dot_tn(m, dout)

    tile = pl.BlockSpec((None, tm, D), lambda b, m: (b, m, 0))
    vec = pl.BlockSpec((None, 1, D), lambda b, m: (b, 0, 0))
    full = pl.BlockSpec((D, D), lambda b, m: (0, 0))
    return pl.pallas_call(
        body, name="out_proj_bwd", grid=(nb, s // tm),
        in_specs=[tile, tile, full, vec], out_specs=[tile, full, vec],
        out_shape=[jax.ShapeDtypeStruct(dxo.shape, BF16), jax.ShapeDtypeStruct((D, D), F32),
                   jax.ShapeDtypeStruct((nb, 1, D), F32)],
        compiler_params=_params(("arbitrary", "arbitrary")),
    )(dxo, merged, wout, gate)


def _in_proj_bwd_h(dproj, wg, dep):
    _, nb, s, _ = dproj.shape
    tm = _row_tile(s, 1024)

    def body(dp0_ref, dp1_ref, w0_ref, w1_ref, w2_ref, w3_ref, dep_ref, dh_ref):
        j = pl.program_id(2)
        part = (_dot_nt(dp0_ref[...], jnp.concatenate([w0_ref[...], w1_ref[...]], axis=1))
                + _dot_nt(dp1_ref[...], jnp.concatenate([w2_ref[...], w3_ref[...]], axis=1)))

        @pl.when(j == 0)
        def _():
            dh_ref[...] = part

        @pl.when(j > 0)
        def _():
            dh_ref[...] += part

    def seg(k):
        return pl.BlockSpec((None, None, tm, D), lambda b, m, j: (2 * j + k, b, m, 0))

    def unit(k):
        return pl.BlockSpec((None, D, UNIT),
                            lambda b, m, j: ((4 * j + k) // UNITS_PER_DEV, 0, (4 * j + k) % UNITS_PER_DEV))

    return pl.pallas_call(
        body, name="in_proj_bwd_h", grid=(nb, s // tm, N_SEG // 2),
        in_specs=[seg(0), seg(1), unit(0), unit(1), unit(2), unit(3),
                  pl.BlockSpec((8, LANES), lambda b, m, j: (0, 0))],
        out_specs=pl.BlockSpec((None, tm, D), lambda b, m, j: (b, m, 0)),
        out_shape=jax.ShapeDtypeStruct((nb, s, D), F32),
        compiler_params=_params(("arbitrary", "arbitrary", "arbitrary")),
    )(dproj, dproj, wg, wg, wg, wg, dep)


def _norm_mod_bwd(dh, x, dxo, gain, scale):
    nb, s, _ = x.shape
    tm = _row_tile(s, 512)

    def body(dh_ref, x_ref, dxo_ref, g_ref, sc_ref, dx_ref, dsh_ref, dsc_ref, dg_ref):
        b, m = pl.program_id(0), pl.program_id(1)

        @pl.when((b == 0) & (m == 0))
        def _():
            dg_ref[...] = jnp.zeros_like(dg_ref)

        @pl.when(m == 0)
        def _():
            dsh_ref[...] = jnp.zeros_like(dsh_ref)
            dsc_ref[...] = jnp.zeros_like(dsc_ref)

        dh = dh_ref[...]
        xv = x_ref[...]
        r = lax.rsqrt(jnp.mean(xv * xv, axis=1, keepdims=True) + EPS)
        xn = xv * r
        g = g_ref[...]
        one_sc = 1.0 + sc_ref[...]
        dsh_ref[...] += _rowsum(dh)
        dsc_ref[...] += _rowsum(dh * (xn * g))
        dg_ref[...] += _rowsum(dh * one_sc * xn)
        dxn = dh * (g * one_sc)
        dx_ref[...] = dxo_ref[...] + r * (dxn - xn * jnp.mean(dxn * xn, axis=1, keepdims=True))

    tile = pl.BlockSpec((None, tm, D), lambda b, m: (b, m, 0))
    vec = pl.BlockSpec((None, 1, D), lambda b, m: (b, 0, 0))
    one = pl.BlockSpec((1, D), lambda b, m: (0, 0))
    return pl.pallas_call(
        body, name="norm_mod_bwd", grid=(nb, s // tm),
        in_specs=[tile, tile, tile, one, vec],
        out_specs=[tile, vec, vec, one],
        out_shape=[jax.ShapeDtypeStruct(x.shape, F32), jax.ShapeDtypeStruct((nb, 1, D), F32),
                   jax.ShapeDtypeStruct((nb, 1, D), F32), jax.ShapeDtypeStruct((1, D), F32)],
        compiler_params=_params(("arbitrary", "arbitrary")),
    )(dh, x, dxo, gain, scale)


def _in_proj_bwd_w(h, dproj, dep):
    nb, s, _ = h.shape
    tm = _row_tile(s, 2048)
    n_m = s // tm

    def body(h_ref, dp_ref, dep_ref, o_ref, acc_ref):
        b, m = pl.program_id(1), pl.program_id(2)

        @pl.when((b == 0) & (m == 0))
        def _():
            acc_ref[...] = jnp.zeros_like(acc_ref)

        acc_ref[...] += _dot_tn(h_ref[...], dp_ref[...])

        @pl.when((b == nb - 1) & (m == n_m - 1))
        def _():
            o_ref[0] = acc_ref[:, :UNIT].astype(BF16)
            o_ref[1] = acc_ref[:, UNIT:].astype(BF16)

    return pl.pallas_call(
        body, name="in_proj_bwd_w", grid=(N_SEG, nb, n_m),
        in_specs=[pl.BlockSpec((None, tm, D), lambda j, b, m: (b, m, 0)),
                  pl.BlockSpec((None, None, tm, D), lambda j, b, m: (j, b, m, 0)),
                  pl.BlockSpec((8, LANES), lambda j, b, m: (0, 0))],
        out_specs=pl.BlockSpec((2, D, UNIT), lambda j, b, m: (j, 0, 0)),
        out_shape=jax.ShapeDtypeStruct((2 * N_SEG, D, UNIT), BF16),
        scratch_shapes=[pltpu.VMEM((D, D), F32)],
        compiler_params=_params(("arbitrary", "arbitrary", "arbitrary")),
    )(h, dproj, dep)


def _mod_proj(c_all, w_mod, b_mod_mine):
    nl, _, ncol = w_mod.shape
    nbg = c_all.shape[0]

    def body(c_ref, w_ref, b_ref, o_ref):
        cv = c_ref[...]
        o_ref[...] = jnp.dot(cv * jax.nn.sigmoid(cv), w_ref[...], preferred_element_type=F32,
                             precision=lax.Precision.HIGHEST) + b_ref[...]

    return pl.pallas_call(
        body, name="mod_proj", grid=(nl,),
        in_specs=[pl.BlockSpec((nbg, D), lambda l: (0, 0)), pl.BlockSpec((None, D, ncol), lambda l: (l, 0, 0)),
                  pl.BlockSpec((None, 1, ncol), lambda l: (l, 0, 0))],
        out_specs=pl.BlockSpec((None, nbg, ncol), lambda l: (l, 0, 0)),
        out_shape=jax.ShapeDtypeStruct((nl, nbg, ncol), F32),
        compiler_params=_params(("arbitrary",)),
    )(c_all, w_mod, b_mod_mine)


def _mod_grad(c_all, dmod_all, dmod_mine):
    nl, nbg, ncol = dmod_mine.shape

    def body(c_ref, da_ref, dm_ref, gw_ref, gb_ref):
        cv = c_ref[...]
        gw_ref[...] = lax.dot_general(cv * jax.nn.sigmoid(cv), dm_ref[...], (((0,), (0,)), ((), ())),
                                      preferred_element_type=F32, precision=lax.Precision.HIGHEST)
        gb_ref[...] = _rowsum(da_ref[...])

    return pl.pallas_call(
        body, name="mod_grad", grid=(nl,),
        in_specs=[pl.BlockSpec((nbg, D), lambda l: (0, 0)), pl.BlockSpec((None, nbg, 3 * D), lambda l: (l, 0, 0)),
                  pl.BlockSpec((None, nbg, ncol), lambda l: (l, 0, 0))],
        out_specs=[pl.BlockSpec((None, D, ncol), lambda l: (l, 0, 0)),
                   pl.BlockSpec((None, 1, 3 * D), lambda l: (l, 0, 0))],
        out_shape=[jax.ShapeDtypeStruct((nl, D, ncol), F32), jax.ShapeDtypeStruct((nl, 1, 3 * D), F32)],
        compiler_params=_params(("arbitrary",)),
    )(c_all, dmod_all, dmod_mine)


def _adamw(parts, w, m, v, name, layer=None, prev=None):
    n_parts, n_u, n_r, cu = parts.shape
    assert w.shape[-2:] == (n_r, n_u * cu), (parts.shape, w.shape)
    tr = n_r
    for cand in (512, 256, 128):
        if n_r > cand and n_r % cand == 0:
            tr = cand
            break
    n_prev = 0 if prev is None else 4

    def body(p_ref, w_ref, m_ref, v_ref, *rest):
        g_ref, d_ref, nm_ref, nv_ref = rest[n_prev:]
        g = p_ref[0].astype(F32)
        for k in range(1, n_parts):
            g = g + p_ref[k].astype(F32)
        m2 = ADAM_B1 * m_ref[...] + (1.0 - ADAM_B1) * g
        v2 = ADAM_B2 * v_ref[...] + (1.0 - ADAM_B2) * (g * g)
        m_hat = m2 / (1.0 - ADAM_B1 ** ADAM_STEP)
        v_hat = v2 / (1.0 - ADAM_B2 ** ADAM_STEP)
        g_ref[...] = g
        d_ref[...] = -ADAM_LR * (m_hat / (jnp.sqrt(v_hat) + ADAM_EPS) + ADAM_WD * w_ref[...])
        nm_ref[...] = m2
        nv_ref[...] = v2

    if layer is None:
        tile = pl.BlockSpec((tr, cu), lambda u, i: (i, u))
    else:
        tile = pl.BlockSpec((None, tr, cu), lambda u, i: (layer, i, u))
    shp = jax.ShapeDtypeStruct(w.shape, F32)
    return pl.pallas_call(
        body, name=name, grid=(n_u, n_r // tr),
        in_specs=[pl.BlockSpec((n_parts, None, tr, cu), lambda u, i: (0, u, i, 0)), tile, tile, tile]
        + [pl.BlockSpec(memory_space=pl.ANY)] * n_prev,
        out_specs=[tile, tile, tile, tile], out_shape=[shp, shp, shp, shp],
        input_output_aliases={4 + k: k for k in range(n_prev)},
        compiler_params=_params(("arbitrary", "arbitrary")),
    )(parts, w, m, v, *(prev or ()))


def _gathered_cols(g, inner):
    k = len(inner)
    perm = tuple(range(1, k + 1)) + (0, k + 1)
    t = jnp.transpose(g, perm)
    return t.reshape(tuple(inner) + (g.shape[0] * g.shape[-1],))


def _pair_blocks(wh):
    z = jnp.zeros((8, 64, 64), wh.dtype)
    w2 = wh.reshape(8, 2, 64, 64)
    top = jnp.concatenate([w2[:, 0], z], axis=2)
    bot = jnp.concatenate([z, w2[:, 1]], axis=2)
    return jnp.concatenate([top, bot], axis=1).astype(BF16)


def _unpair_blocks(g):
    return jnp.stack([g[:, :64, :64], g[:, 64:, 64:]], axis=1).reshape(16, 64, 64)


FLAT_ROWS = 512


def _pack_rows(arrays, lead=0):
    parts = [a.reshape(a.shape[:lead] + (-1, LANES)) for a in arrays]
    rows = jnp.concatenate(parts, axis=lead)
    pad = [(0, 0)] * rows.ndim
    pad[lead] = (0, (-rows.shape[lead]) % FLAT_ROWS)
    return jnp.pad(rows, pad)


def kernel(x, c, norm_gain, w_mod, b_mod, w_in, w_out, conv_a_w, sgu_w, sgu_b, lru_conv_w, lru_conv_b, lru_wa, lru_ba, lru_wx, lru_bx, lru_lambda, final_gain, loss_target, m_norm_gain, m_w_mod, m_b_mod, m_w_in, m_w_out, m_conv_a_w, m_sgu_w, m_sgu_b, m_lru_conv_w, m_lru_conv_b, m_lru_wa, m_lru_ba, m_lru_wx, m_lru_bx, m_lru_lambda, m_final_gain, v_norm_gain, v_w_mod, v_b_mod, v_w_in, v_w_out, v_conv_a_w, v_sgu_w, v_sgu_b, v_lru_conv_w, v_lru_conv_b, v_lru_wa, v_lru_ba, v_lru_wx, v_lru_bx, v_lru_lambda, v_final_gain):
    nl = w_in.shape[0]
    nb, s, _ = x.shape
    me = _my_index()
    mod_cols = w_mod.shape[2]


    small = jnp.concatenate([c.reshape(-1, LANES), conv_a_w.reshape(-1, LANES), lru_conv_w.reshape(-1, LANES)])
    n_c, n_ca = nb * D // LANES, nl * 3
    n_small = small.shape[0]
    small = jnp.pad(small, ((0, (-n_small) % 8), (0, 0)))
    small_all, _ = _all_gather(small, "gather_small")
    c_all = small_all[:, :n_c].reshape(N_DEV * nb, D)

    w_in_b = [w_in[0].astype(BF16)] + list(w_in[1:].astype(BF16))

    def start_w_in(l, dep):
        return _split_start(w_in_b[l], _peers_same_core, False, "gather_w_in_start", dep)

    conv_a_full = _gathered_cols(small_all[:, n_c:n_c + n_ca].reshape(N_DEV, nl, 3, LANES), (nl, 3))
    lru_conv_full = _gathered_cols(small_all[:, n_c + n_ca:n_small].reshape(N_DEV, nl, 4, LANES), (nl, 4))

    def gathered_w_in(started, after):
        block, land = _split_wait(started, after, _peers_same_core, False, False, "gather_w_in_wait")
        return _gather_finish(block, land, "gather_w_in_finish")

    sgu_b_lanes = jnp.broadcast_to(sgu_b[..., None], sgu_b.shape + (LANES,))
    mws = []
    for l in range(nl):
        mws.append((conv_a_full[l], sgu_w[l], sgu_b_lanes[l], lru_conv_full[l], lru_conv_b[l][None, :],
                    _pair_blocks(lru_wa[l]), _pair_blocks(lru_wx[l]), lru_ba[l].reshape(1, D),
                    lru_bx[l].reshape(1, D), lru_lambda[l][None, :]))

    rep_names = ["sgu_w", "sgu_b", "lru_conv_b", "lru_wa", "lru_ba", "lru_wx", "lru_bx", "lru_lambda"]
    rep_w = dict(sgu_w=sgu_w, sgu_b=sgu_b, lru_conv_b=lru_conv_b, lru_wa=lru_wa, lru_ba=lru_ba,
                 lru_wx=lru_wx, lru_bx=lru_bx, lru_lambda=lru_lambda)
    rep_m = dict(sgu_w=m_sgu_w, sgu_b=m_sgu_b, lru_conv_b=m_lru_conv_b, lru_wa=m_lru_wa,
                 lru_ba=m_lru_ba, lru_wx=m_lru_wx, lru_bx=m_lru_bx, lru_lambda=m_lru_lambda)
    rep_v = dict(sgu_w=v_sgu_w, sgu_b=v_sgu_b, lru_conv_b=v_lru_conv_b, lru_wa=v_lru_wa,
                 lru_ba=v_lru_ba, lru_wx=v_lru_wx, lru_bx=v_lru_bx, lru_lambda=v_lru_lambda)

    rep_w_all, rep_m_all, rep_v_all = [_pack_rows([src[n] for n in rep_names], lead=1)
                                       for src in (rep_w, rep_m, rep_v)]
    early = [rep_w_all, rep_m_all, rep_v_all] + w_in_b[1:] + [a for mw in mws for a in mw]

    b_mod_mine = lax.dynamic_slice_in_dim(b_mod, me * mod_cols, mod_cols, axis=1)[:, None, :]
    mod_mine = _mod_proj(c_all, w_mod, b_mod_mine)
    mod_all, mod_token = _all_gather(mod_mine.reshape(nl * N_DEV * nb, mod_cols), "gather_mod", dep=w_in_b[0])
    ici = {0: start_w_in(0, mod_token)}
    mod_full = _gathered_cols(mod_all.reshape(N_DEV, nl, N_DEV * nb, mod_cols), (nl, N_DEV * nb))
    mod_loc = lax.dynamic_slice_in_dim(mod_full, me * nb, nb, axis=1)
    shift, scale, gate = [mod_loc[:, :, j * D:(j + 1) * D][:, :, None, :] for j in range(3)]

    xs, hs_bf, projs, mergeds, states, wg = [], [], [], [], [], []
    xl = x
    d2d = {}
    wo_started = _split_start(w_out.astype(BF16).reshape(nl * (D // N_DEV), D), _peers_all, False,
                              "gather_w_out_start", ici[0][4])
    wo = None
    for l in range(nl):
        h = _norm_mod(xl, _after(norm_gain[l][None, :], wo_started[4]), shift[l], scale[l])
        if l == 0:
            wg_l, token = gathered_w_in(ici[0], [h] + early)
            ici[1] = start_w_in(1, token)
            dep = ici[1][4]
        else:
            wg_l = _forward_wait(d2d[l], h, "gather_w_in_d2d_wait")
            dep = d2d[l][4]
        wg.append(wg_l)
        proj = _in_proj(h, wg_l, dep)
        merged, st = _mixer_fwd(proj, mws[l])
        xs.append(xl), hs_bf.append(h), projs.append(proj), mergeds.append(merged), states.append(st)
        gate_l = gate[l]
        if l + 1 < nl:
            block, land = _split_wait(ici[l + 1], merged, _peers_same_core, False, False, "gather_w_in_wait")
            d2d[l + 1] = _forward_start(block, land, "gather_w_in_d2d_start")
            gate_l = _after(gate_l, d2d[l + 1][4])
            if l + 2 < nl:
                ici[l + 2] = start_w_in(l + 2, d2d[l + 1][4])
                gate_l = _after(gate_l, ici[l + 2][4])
        if wo is None:
            _, wo_all = _split_wait(wo_started, merged, _peers_all, False, True, "gather_w_out_wait")
            wo = jnp.transpose(wo_all.reshape(N_DEV, nl, D // N_DEV, D), (1, 0, 2, 3)).reshape(nl, D, D)
        xl = _out_proj(xl, merged, wo[l], gate_l)

    loss_row, dx, g_final = _loss_head(xl, final_gain[None, :], loss_target)

    res_big, g_conv = {}, [None] * nl
    dmods = [None] * nl

    def finish_exchange(pending, after):
        l, h_in, h_out, h_rep = pending
        _, recv = _split_wait(h_in, after, _peers_all, True, True, "scatter_w_in_wait")
        res_big["w_in"] = _adamw(recv, w_in, m_w_in, v_w_in, "adamw_w_in", l, res_big.get("w_in"))
        _, recv = _split_wait(h_out, after, _peers_all, True, True, "scatter_w_out_wait")
        res_big["w_out"] = _adamw(recv, w_out, m_w_out, v_w_out, "adamw_w_out", l, res_big.get("w_out"))
        _, recv = _split_wait(h_rep, after, _peers_all, False, True, "gather_rep_wait")
        res_big["rep"] = _adamw(recv[:, None], rep_w_all, rep_m_all, rep_v_all, "adamw_rep", l, res_big.get("rep"))

    pending = None
    g_gains = [None] * nl
    for l in reversed(range(nl)):
        dmerged, gw_out, dgate = _out_proj_bwd(dx, mergeds[l], wo[l], gate[l])
        dproj, g_caw, g_sw, g_sb, g_lcw, g_vec, g_wa, g_wx = _mixer_bwd(projs[l], dmerged, states[l], mws[l])
        g_conv[l] = (g_caw, g_lcw)
        rep_g = dict(
            sgu_w=g_sw, sgu_b=g_sb[:, :, 0], lru_conv_b=g_vec[0],
            lru_wa=_unpair_blocks(g_wa), lru_ba=g_vec[1].reshape(16, 64), lru_wx=_unpair_blocks(g_wx),
            lru_bx=g_vec[2].reshape(16, 64), lru_lambda=g_vec[3])
        rep_block = _pack_rows([rep_g[n] for n in rep_names])
        h_out = _split_start(gw_out.reshape(N_DEV, 1, D // N_DEV, D), _peers_all, True, "scatter_w_out_start")
        h_rep = _split_start(rep_block, _peers_all, False, "gather_rep_start")
        gw_in = _in_proj_bwd_w(hs_bf[l], dproj, h_out[4] + h_rep[4])
        h_in = _split_start(gw_in.reshape(N_DEV, UNITS_PER_DEV, D, UNIT), _peers_all, True, "scatter_w_in_start")
        started = (l, h_in, h_out, h_rep)
        dh = _in_proj_bwd_h(dproj, wg[l], h_in[4])
        dx, dshift, dscale, g_gain = _norm_mod_bwd(dh, xs[l], dx, norm_gain[l][None, :], scale[l])
        g_gains[l] = g_gain
        dmods[l] = jnp.concatenate([dshift, dscale, dgate], axis=2)[:, 0, :]
        if pending is not None:
            finish_exchange(pending, dx)
        pending = started

    conv_parts = jnp.concatenate(
        [jnp.stack([g_conv[l][0] for l in range(nl)]).reshape(nl * 3, N_DEV, LANES),
         jnp.stack([g_conv[l][1] for l in range(nl)]).reshape(nl * 4, N_DEV, LANES)], axis=0)
    conv_parts = jnp.transpose(conv_parts, (1, 0, 2))[:, None]
    conv_recv = _all_to_all(conv_parts, "scatter_conv")

    dmod_loc = jnp.stack(dmods).reshape(nl * nb, 3 * D)
    gain_rows = jnp.pad(jnp.concatenate(g_gains + [g_final, loss_row], axis=0),
                        ((0, (-(nl + 2)) % 8), (0, 2 * D)))
    tail_g, _ = _all_gather(jnp.concatenate([dmod_loc, gain_rows], axis=0), "gather_dmod",
                            dep=[res_big[k][3] for k in ("w_in", "w_out", "rep")])
    loss = jnp.sum(tail_g[:, nl * nb + nl + 1, 0])
    dmod_g = tail_g[:, :nl * nb]
    gain_parts = tail_g[:, nl * nb:nl * nb + nl + 1, :D][:, None]
    gain_cat = lambda a, b: jnp.concatenate([a, b[None, :]], axis=0)
    res_gain = _adamw(gain_parts, gain_cat(norm_gain, final_gain), gain_cat(m_norm_gain, m_final_gain),
                      gain_cat(v_norm_gain, v_final_gain), "adamw_gain")
    dmod_all = jnp.transpose(dmod_g.reshape(N_DEV, nl, nb, 3 * D), (1, 0, 2, 3)).reshape(nl, N_DEV * nb, 3 * D)
    dmod_mine = lax.dynamic_slice_in_dim(dmod_all, me * mod_cols, mod_cols, axis=2)
    gw_mod, gb_mod = _mod_grad(c_all, dmod_all, dmod_mine)
    res_w_mod = _adamw(gw_mod.reshape(1, 1, nl * D, mod_cols), w_mod.reshape(nl * D, mod_cols),
                       m_w_mod.reshape(nl * D, mod_cols), v_w_mod.reshape(nl * D, mod_cols), "adamw_w_mod")
    res_w_mod = [a.reshape(nl, D, mod_cols) for a in res_w_mod]
    res_b_mod = _adamw(gb_mod.reshape(1, 1, nl, 3 * D), b_mod, m_b_mod, v_b_mod, "adamw_b_mod")
    finish_exchange(pending, res_b_mod[1])

    cat = lambda a, b: jnp.concatenate([a.reshape(nl * 3, LANES), b.reshape(nl * 4, LANES)], axis=0)
    res_conv = _adamw(conv_recv, cat(conv_a_w, lru_conv_w), cat(m_conv_a_w, m_lru_conv_w),
                      cat(v_conv_a_w, v_lru_conv_w), "adamw_conv")
    res_conv_a = [a[:nl * 3].reshape(nl, 3, LANES) for a in res_conv]
    res_lru_conv = [a[nl * 3:].reshape(nl, 4, LANES) for a in res_conv]

    res_rep = []
    for k in range(4):
        off, d = 0, {}
        for n in rep_names:
            n_rows = rep_w[n][0].size // LANES
            d[n] = res_big["rep"][k][:, off:off + n_rows].reshape(rep_w[n].shape)
            off += n_rows
        res_rep.append(d)

    def leaf(k, name):
        if name == "norm_gain":
            return res_gain[k][:nl]
        if name == "final_gain":
            return res_gain[k][nl]
        if name == "w_mod":
            return res_w_mod[k]
        if name == "b_mod":
            return res_b_mod[k]
        if name in ("w_in", "w_out"):
            return res_big[name][k]
        if name == "conv_a_w":
            return res_conv_a[k]
        if name == "lru_conv_w":
            return res_lru_conv[k]
        return res_rep[k][name]

    order = ["norm_gain", "w_mod", "b_mod", "w_in", "w_out", "conv_a_w", "sgu_w", "sgu_b", "lru_conv_w",
             "lru_conv_b", "lru_wa", "lru_ba", "lru_wx", "lru_bx", "lru_lambda", "final_gain"]
    outs = [loss, dx]
    for k in range(4):
        outs += [leaf(k, n) for n in order]
    return tuple(outs)
```

```python
import functools

import jax
import jax.numpy as jnp
from jax import lax
from jax.experimental import pallas as pl
from jax.experimental.pallas import tpu as pltpu

F32 = jnp.float32
BF16 = jnp.bfloat16

D = 1024
N_DEV = 8
N_SEG = 12
LANES = 128
SUBLANES = 8
CHUNK = 128
HALO = 16
UNIT = 512
UNITS_PER_DEV = 3
EPS = 1e-6
LRU_C = 8.0
ADAM_LR, ADAM_B1, ADAM_B2, ADAM_EPS, ADAM_WD, ADAM_STEP = 0.001, 0.9, 0.999, 1e-08, 0.01, 10
VMEM_LIMIT = 56 * 1024 * 1024

AX, AB, AC, AZ, SU, SV, SZ, RX, RZ, GA, GS, GR = range(N_SEG)
MESH = pl.DeviceIdType.MESH


def _params(sem=None):
    return pltpu.CompilerParams(dimension_semantics=sem, vmem_limit_bytes=VMEM_LIMIT)


def _my_index():
    return 4 * lax.axis_index("x") + 2 * lax.axis_index("y") + lax.axis_index("c")


def _all_gather(block, name, dep=None):
    deps = [] if dep is None else list(dep) if isinstance(dep, (list, tuple)) else [dep]

    def body(x_ref, *refs):
        out_ref, token, send_sems, recv_sems, local_sem = refs[-5:]
        x, y, c = lax.axis_index("x"), lax.axis_index("y"), lax.axis_index("c")
        me, sibling = (x, y, c), (x, y, 1 - c)
        chips = [(1 - x, y), (x, 1 - y), (1 - x, 1 - y)]
        token[...] = jnp.zeros_like(token)

        def rows(px, py, pc):
            return out_ref.at[4 * px + 2 * py + pc]

        def copy(k, blk, to, src=None):
            return pltpu.make_async_remote_copy(
                src_ref=rows(*blk) if src is None else src, dst_ref=rows(*blk),
                send_sem=send_sems.at[k], recv_sem=recv_sems.at[k], device_id=to, device_id_type=MESH)

        mine = pltpu.make_async_copy(x_ref, rows(*me), local_sem)
        mine.start()
        first = [copy(0, me, sibling, src=x_ref)]
        first += [copy(1 + j, me, (*chip, c), src=x_ref) for j, chip in enumerate(chips)]
        for cp in first:
            cp.start()
        passed = [copy(4 + j, (*chip, c), sibling) for j, chip in enumerate(chips)]
        for j, chip in enumerate(chips):
            copy(1 + j, (*chip, c), me).wait_recv()
            passed[j].start()
        copy(0, sibling, me).wait_recv()
        for j, chip in enumerate(chips):
            copy(4 + j, (*chip, 1 - c), me).wait_recv()
        for cp in first + passed:
            cp.wait_send()
        mine.wait()

    return pl.pallas_call(
        body, name=name,
        out_shape=[jax.ShapeDtypeStruct((N_DEV,) + block.shape, block.dtype), jax.ShapeDtypeStruct((8, LANES), F32)],
        in_specs=[pl.BlockSpec(memory_space=pltpu.VMEM)]
        + [pl.BlockSpec(memory_space=pl.ANY)] * len(deps),
        out_specs=[pl.BlockSpec(memory_space=pl.ANY), pl.BlockSpec(memory_space=pltpu.VMEM)],
        scratch_shapes=[pltpu.SemaphoreType.DMA((7,)), pltpu.SemaphoreType.DMA((7,)), pltpu.SemaphoreType.DMA],
    )(block, *deps)


def _all_to_all(blocks, name):
    def body(x_ref, out_ref, send_sems, recv_sems, local_sem):
        x, y, c = lax.axis_index("x"), lax.axis_index("y"), lax.axis_index("c")
        my = 4 * x + 2 * y + c
        mine = pltpu.make_async_copy(x_ref.at[my], out_ref.at[my], local_sem)
        mine.start()
        peers = []
        for r in range(1, N_DEV):
            px = 1 - x if r & 4 else x
            py = 1 - y if r & 2 else y
            pc = 1 - c if r & 1 else c
            peers.append((r - 1, 4 * px + 2 * py + pc, (px, py, pc)))

        def copy(k, src_slot, dst_slot, to):
            return pltpu.make_async_remote_copy(
                src_ref=x_ref.at[src_slot], dst_ref=out_ref.at[dst_slot],
                send_sem=send_sems.at[k], recv_sem=recv_sems.at[k], device_id=to, device_id_type=MESH)

        sends = [copy(k, pid, my, to) for k, pid, to in peers]
        for cp in sends:
            cp.start()
        for k, pid, to in peers:
            copy(k, pid, pid, to).wait_recv()
        for cp in sends:
            cp.wait_send()
        mine.wait()

    return pl.pallas_call(
        body, name=name,
        out_shape=jax.ShapeDtypeStruct(blocks.shape, blocks.dtype),
        in_specs=[pl.BlockSpec(memory_space=pltpu.VMEM)],
        out_specs=pl.BlockSpec(memory_space=pl.ANY),
        scratch_shapes=[pltpu.SemaphoreType.DMA((7,)), pltpu.SemaphoreType.DMA((7,)), pltpu.SemaphoreType.DMA],
    )(blocks)


_HBM = pl.BlockSpec(memory_space=pltpu.HBM)
_SEM = pl.BlockSpec(memory_space=pltpu.SEMAPHORE)
_EFFECT = pltpu.SideEffectType.DATAFLOW_SIDE_EFFECTING


def _peers_all(x, y, c):
    out = []
    for r in range(1, N_DEV):
        px = 1 - x if r & 4 else x
        py = 1 - y if r & 2 else y
        pc = 1 - c if r & 1 else c
        out.append((r - 1, 4 * px + 2 * py + pc, (px, py, pc)))
    return out


def _peers_same_core(x, y, c):
    return [(k, 4 * px + 2 * py + c, (px, py, c))
            for k, (px, py) in enumerate([(1 - x, y), (x, 1 - y), (1 - x, 1 - y)])]


def _split_start(src, peers_fn, scatter, name, dep=None):
    blk = src.shape[1:] if scatter else src.shape
    land_shape = (N_DEV,) + tuple(blk)
    n = len(peers_fn(0, 0, 0))
    deps = [] if dep is None else [dep]

    def body(x_ref, land_ref, *rest):
        send_sems, recv_sems, x_thru, land_thru, token = rest[len(deps):]
        x, y, c = lax.axis_index("x"), lax.axis_index("y"), lax.axis_index("c")
        my = 4 * x + 2 * y + c
        for k, pid, to in peers_fn(x, y, c):
            pltpu.make_async_remote_copy(
                src_ref=x_ref.at[pid] if scatter else x_ref, dst_ref=land_ref.at[my],
                send_sem=send_sems.at[k], recv_sem=recv_sems.at[k], device_id=to, device_id_type=MESH).start()
        token[...] = jnp.zeros_like(token)

    return pl.pallas_call(
        body, name=name,
        out_shape=(pltpu.SemaphoreType.DMA((n,)), pltpu.SemaphoreType.DMA((n,)),
                   pltpu.HBM(src.shape, src.dtype), pltpu.HBM(land_shape, src.dtype),
                   jax.ShapeDtypeStruct((8, LANES), F32)),
        in_specs=(_HBM, _HBM) + (pl.BlockSpec(memory_space=pl.ANY),) * len(deps),
        out_specs=(_SEM, _SEM, _HBM, _HBM, pl.BlockSpec(memory_space=pltpu.VMEM)),
        input_output_aliases={0: 2, 1: 3},
        compiler_params=pltpu.CompilerParams(has_side_effects=_EFFECT),
    )(pltpu.with_memory_space_constraint(src, pltpu.HBM),
      pltpu.with_memory_space_constraint(lax.empty(land_shape, src.dtype), pltpu.HBM), *deps)


def _split_wait(handles, after, peers_fn, scatter, own, name):
    send_sems, recv_sems, src_thru, land_thru, _ = handles
    blk = land_thru.shape[1:]
    after = list(after) if isinstance(after, (list, tuple)) else [after]

    def body(x_ref, land_ref, send_sems, recv_sems, *rest):
        stage = rest[len(after) + 2:]
        x, y, c = lax.axis_index("x"), lax.axis_index("y"), lax.axis_index("c")
        if own:
            my = 4 * x + 2 * y + c
            mine = _staged_copy(x_ref.at[my] if scatter else x_ref, land_ref.at[my], *stage)
        for k, pid, to in peers_fn(x, y, c):
            cp = pltpu.make_async_remote_copy(
                src_ref=x_ref.at[pid] if scatter else x_ref, dst_ref=land_ref.at[pid],
                send_sem=send_sems.at[k], recv_sem=recv_sems.at[k], device_id=to, device_id_type=MESH)
            cp.wait_send()
            cp.wait_recv()
        if own:
            mine.wait()

    return pl.pallas_call(
        body, name=name,
        out_shape=(pltpu.HBM(src_thru.shape, src_thru.dtype), pltpu.HBM(land_thru.shape, land_thru.dtype)),
        in_specs=(_HBM, _HBM, _SEM, _SEM) + (pl.BlockSpec(memory_space=pl.ANY),) * len(after),
        out_specs=(_HBM, _HBM),
        input_output_aliases={0: 0, 1: 1},
        scratch_shapes=[pltpu.VMEM(blk, land_thru.dtype), pltpu.SemaphoreType.DMA((2,))] if own else [],
        compiler_params=pltpu.CompilerParams(has_side_effects=_EFFECT, vmem_limit_bytes=VMEM_LIMIT),
    )(src_thru, land_thru, send_sems, recv_sems, *after)


def _staged_copy(src_ref, dst_ref, buf, sems):
    leg = pltpu.make_async_copy(src_ref, buf, sems.at[0])
    leg.start()
    leg.wait()
    leg = pltpu.make_async_copy(buf, dst_ref, sems.at[1])
    leg.start()
    return leg


def _multi_start(srcs, name, dep=None):
    n_src = len(srcs)
    lands = [(N_DEV,) + tuple(a.shape[1:] if sc else a.shape) for a, sc in srcs]
    deps = [] if dep is None else [dep]

    def body(*refs):
        ins, outs = refs[:2 * n_src], refs[2 * n_src + len(deps):]
        x, y, c = lax.axis_index("x"), lax.axis_index("y"), lax.axis_index("c")
        my = 4 * x + 2 * y + c
        for i, (_, scatter) in enumerate(srcs):
            x_ref, land_ref, send_sems, recv_sems = ins[2 * i], ins[2 * i + 1], outs[4 * i], outs[4 * i + 1]
            for k, pid, to in _peers_all(x, y, c):
                pltpu.make_async_remote_copy(
                    src_ref=x_ref.at[pid] if scatter else x_ref, dst_ref=land_ref.at[my],
                    send_sem=send_sems.at[k], recv_sem=recv_sems.at[k], device_id=to, device_id_type=MESH).start()
        outs[-1][...] = jnp.zeros_like(outs[-1])

    out_shape, out_specs, operands = [], [], []
    for (a, _), land in zip(srcs, lands):
        out_shape += [pltpu.SemaphoreType.DMA((N_DEV - 1,)), pltpu.SemaphoreType.DMA((N_DEV - 1,)),
                      pltpu.HBM(a.shape, a.dtype), pltpu.HBM(land, a.dtype)]
        out_specs += [_SEM, _SEM, _HBM, _HBM]
        operands += [pltpu.with_memory_space_constraint(a, pltpu.HBM),
                     pltpu.with_memory_space_constraint(lax.empty(land, a.dtype), pltpu.HBM)]
    res = pl.pallas_call(
        body, name=name,
        out_shape=tuple(out_shape) + (jax.ShapeDtypeStruct((8, LANES), F32),),
        in_specs=(_HBM,) * (2 * n_src) + (pl.BlockSpec(memory_space=pl.ANY),) * len(deps),
        out_specs=tuple(out_specs) + (pl.BlockSpec(memory_space=pltpu.VMEM),),
        input_output_aliases={2 * i + j: 4 * i + 2 + j for i in range(n_src) for j in range(2)},
        compiler_params=pltpu.CompilerParams(has_side_effects=_EFFECT),
    )(*operands, *deps)
    return [tuple(res[4 * i:4 * i + 4]) + (srcs[i][1],) for i in range(n_src)], res[-1]


def _multi_wait(started, after, name):
    n_src = len(started)
    after = list(after) if isinstance(after, (list, tuple)) else [after]

    def body(*refs):
        ins = refs[:4 * n_src]
        stage = refs[4 * n_src + len(after) + 2 * n_src:]
        x, y, c = lax.axis_index("x"), lax.axis_index("y"), lax.axis_index("c")
        my = 4 * x + 2 * y + c
        pending = []
        for i, h in enumerate(started):
            x_ref, land_ref, send_sems, recv_sems = ins[4 * i:4 * i + 4]
            scatter = h[4]
            pending.append(_staged_copy(x_ref.at[my] if scatter else x_ref, land_ref.at[my],
                                        stage[2 * i], stage[2 * i + 1]))
            for k, pid, to in _peers_all(x, y, c):
                cp = pltpu.make_async_remote_copy(
                    src_ref=x_ref.at[pid] if scatter else x_ref, dst_ref=land_ref.at[pid],
                    send_sem=send_sems.at[k], recv_sem=recv_sems.at[k], device_id=to, device_id_type=MESH)
                cp.wait_send()
                cp.wait_recv()
        for leg in pending:
            leg.wait()

    operands, out_shape, scratch = [], [], []
    for send, recv, src_thru, land_thru, _ in started:
        operands += [src_thru, land_thru, send, recv]
        out_shape += [pltpu.HBM(src_thru.shape, src_thru.dtype), pltpu.HBM(land_thru.shape, land_thru.dtype)]
        scratch += [pltpu.VMEM(land_thru.shape[1:], land_thru.dtype), pltpu.SemaphoreType.DMA((2,))]
    res = pl.pallas_call(
        body, name=name,
        out_shape=tuple(out_shape),
        in_specs=(_HBM, _HBM, _SEM, _SEM) * n_src + (pl.BlockSpec(memory_space=pl.ANY),) * len(after),
        out_specs=(_HBM,) * (2 * n_src),
        input_output_aliases={4 * i + j: 2 * i + j for i in range(n_src) for j in range(2)},
        scratch_shapes=scratch,
        compiler_params=pltpu.CompilerParams(has_side_effects=_EFFECT, vmem_limit_bytes=VMEM_LIMIT),
    )(*operands, *after)
    return [res[2 * i + 1] for i in range(n_src)]


def _gather_finish(block, land, name):
    def body(x_ref, land_ref, out_ref, token, send_sems, recv_sems, buf, local_sems):
        x, y, c = lax.axis_index("x"), lax.axis_index("y"), lax.axis_index("c")
        my, sib_id, sibling = 4 * x + 2 * y + c, 4 * x + 2 * y + 1 - c, (x, y, 1 - c)
        token[...] = jnp.zeros_like(token)

        def copy(k, slot, src=None):
            return pltpu.make_async_remote_copy(
                src_ref=land_ref.at[slot] if src is None else src, dst_ref=out_ref.at[slot],
                send_sem=send_sems.at[k], recv_sem=recv_sems.at[k], device_id=sibling, device_id_type=MESH)

        chips = _peers_same_core(x, y, c)
        sends = [copy(0, my, src=x_ref)] + [copy(1 + k, pid) for k, pid, _ in chips]
        for cp in sends:
            cp.start()
        mine = _staged_copy(x_ref, out_ref.at[my], buf, local_sems)
        copy(0, sib_id).wait_recv()
        for k, pid, _ in chips:
            copy(1 + k, pid + 1 - 2 * c).wait_recv()
        for cp in sends:
            cp.wait_send()
        mine.wait()

    return pl.pallas_call(
        body, name=name,
        out_shape=[jax.ShapeDtypeStruct(land.shape, land.dtype), jax.ShapeDtypeStruct((8, LANES), F32)],
        in_specs=[pl.BlockSpec(memory_space=pl.ANY), pl.BlockSpec(memory_space=pl.ANY)],
        out_specs=[pl.BlockSpec(memory_space=pl.ANY), pl.BlockSpec(memory_space=pltpu.VMEM)],
        input_output_aliases={1: 0},
        scratch_shapes=[pltpu.SemaphoreType.DMA((4,)), pltpu.SemaphoreType.DMA((4,)),
                        pltpu.VMEM(block.shape, block.dtype), pltpu.SemaphoreType.DMA((2,))],
        compiler_params=pltpu.CompilerParams(vmem_limit_bytes=VMEM_LIMIT),
    )(block, land)


def _forward_start(block, land, name):
    def body(x_ref, land_ref, send_sems, recv_sems, x_thru, land_thru, token):
        x, y, c = lax.axis_index("x"), lax.axis_index("y"), lax.axis_index("c")
        my, sibling = 4 * x + 2 * y + c, (x, y, 1 - c)
        slots = [(0, my, x_ref)] + [(1 + k, pid, land_ref.at[pid]) for k, pid, _ in _peers_same_core(x, y, c)]
        for k, slot, src in slots:
            pltpu.make_async_remote_copy(
                src_ref=src, dst_ref=land_ref.at[slot], send_sem=send_sems.at[k], recv_sem=recv_sems.at[k],
                device_id=sibling, device_id_type=MESH).start()
        token[...] = jnp.zeros_like(token)

    return pl.pallas_call(
        body, name=name,
        out_shape=(pltpu.SemaphoreType.DMA((4,)), pltpu.SemaphoreType.DMA((4,)),
                   pltpu.HBM(block.shape, block.dtype), pltpu.HBM(land.shape, land.dtype),
                   jax.ShapeDtypeStruct((8, LANES), F32)),
        in_specs=(_HBM, _HBM),
        out_specs=(_SEM, _SEM, _HBM, _HBM, pl.BlockSpec(memory_space=pltpu.VMEM)),
        input_output_aliases={0: 2, 1: 3},
        compiler_params=pltpu.CompilerParams(has_side_effects=_EFFECT),
    )(block, land)


def _forward_wait(handles, after, name):
    send_sems, recv_sems, block_thru, land_thru, _ = handles

    def body(x_ref, land_ref, send_sems, recv_sems, after_ref, x_dead, got_ref, buf, local_sems):
        x, y, c = lax.axis_index("x"), lax.axis_index("y"), lax.axis_index("c")
        my, sib_id, sibling = 4 * x + 2 * y + c, 4 * x + 2 * y + 1 - c, (x, y, 1 - c)
        mine = _staged_copy(x_ref, land_ref.at[my], buf, local_sems)
        slots = [(0, my, sib_id)] + [(1 + k, pid, pid + 1 - 2 * c) for k, pid, _ in _peers_same_core(x, y, c)]
        for k, sent, got in slots:
            cp = pltpu.make_async_remote_copy(
                src_ref=land_ref.at[sent], dst_ref=land_ref.at[got], send_sem=send_sems.at[k],
                recv_sem=recv_sems.at[k], device_id=sibling, device_id_type=MESH)
            cp.wait_send()
            cp.wait_recv()
        mine.wait()

    return pl.pallas_call(
        body, name=name,
        out_shape=(pltpu.HBM(block_thru.shape, block_thru.dtype), pltpu.HBM(land_thru.shape, land_thru.dtype)),
        in_specs=(_HBM, _HBM, _SEM, _SEM, pl.BlockSpec(memory_space=pl.ANY)),
        out_specs=(_HBM, _HBM),
        input_output_aliases={0: 0, 1: 1},
        scratch_shapes=[pltpu.VMEM(block_thru.shape, block_thru.dtype), pltpu.SemaphoreType.DMA((2,))],
        compiler_params=pltpu.CompilerParams(has_side_effects=_EFFECT, vmem_limit_bytes=VMEM_LIMIT),
    )(block_thru, land_thru, send_sems, recv_sems, after)[1]


def _after(v, token):
    return v + token[0, 0].astype(v.dtype)


def _dsilu(x, s):
    return s * (1.0 + x * (1.0 - s))


def _log1p(x):
    u = 1.0 + x
    d = u - 1.0
    return jnp.where(d == 0.0, x, jnp.log(u) * (x / jnp.where(d == 0.0, 1.0, d)))


def _softplus_neg(lam):
    return jnp.maximum(-lam, 0.0) + _log1p(jnp.exp(-jnp.abs(lam)))


def _neg_expm1(y, exp_y):
    poly = -y * (1.0 + y * (0.5 + y * (1.0 / 6.0 + y * (1.0 / 24.0))))
    return jnp.where(y > -0.05, poly, 1.0 - exp_y)


def _sigmoid(x):
    return 0.5 * jnp.tanh(0.5 * x) + 0.5


def _shift_dn(cur, prev, k):
    ext = jnp.concatenate([prev, cur], axis=0)
    return pltpu.roll(ext, k, 0)[HALO:, :]


def _shift_up(cur, nxt, k):
    n = cur.shape[0]
    ext = jnp.concatenate([cur, nxt], axis=0)
    return pltpu.roll(ext, n + HALO - k, 0)[:n, :]


def _scan_fwd(a, b, h_prev):
    groups = a.shape[0] // SUBLANES
    a3 = a.reshape(groups, SUBLANES, LANES)
    b3 = b.reshape(groups, SUBLANES, LANES)
    row = lax.broadcasted_iota(jnp.int32, a3.shape, 1)
    k = 1
    while k < SUBLANES:
        a_sh = jnp.where(row >= k, pltpu.roll(a3, k, 1), 1.0)
        b_sh = jnp.where(row >= k, pltpu.roll(b3, k, 1), 0.0)
        b3 = a3 * b_sh + b3
        a3 = a3 * a_sh
        k *= 2
    carry = h_prev[HALO - 1:HALO, :]
    out = []
    for i in range(groups):
        hg = b3[i] + a3[i] * carry
        out.append(hg)
        carry = hg[SUBLANES - 1:SUBLANES, :]
    return jnp.concatenate(out, axis=0)


def _scan_rev(a_next, g, lam_next):
    groups = g.shape[0] // SUBLANES
    a3 = a_next.reshape(groups, SUBLANES, LANES)
    g3 = g.reshape(groups, SUBLANES, LANES)
    row = lax.broadcasted_iota(jnp.int32, a3.shape, 1)
    k = 1
    while k < SUBLANES:
        ok = row < SUBLANES - k
        a_sh = jnp.where(ok, pltpu.roll(a3, SUBLANES - k, 1), 1.0)
        g_sh = jnp.where(ok, pltpu.roll(g3, SUBLANES - k, 1), 0.0)
        g3 = g3 + a3 * g_sh
        a3 = a3 * a_sh
        k *= 2
    carry = lam_next[0:1, :]
    out = [None] * groups
    for i in reversed(range(groups)):
        lg = g3[i] + a3[i] * carry
        out[i] = lg
        carry = lg[0:1, :]
    return jnp.concatenate(out, axis=0)


def _rowsum(v):
    return jnp.sum(v, axis=0, keepdims=True)


def _dot(a, b):
    return jnp.dot(a, b, preferred_element_type=F32)


def _dot_nt(a, b):
    return lax.dot_general(a, b, (((1,), (1,)), ((), ())), preferred_element_type=F32)


def _dot_tn(a, b):
    return lax.dot_general(a, b, (((0,), (0,)), ((), ())), preferred_element_type=F32)


class _MixerWeights:
    def __init__(self, caw_ref, sw_ref, sb_ref, lcw_ref, lcb_ref, wa_ref, wx_ref, ba_ref, bx_ref, lam_ref):
        self.caw = [caw_ref[j:j + 1, :] for j in range(3)]
        self.lcw = [lcw_ref[j:j + 1, :] for j in range(4)]
        self.lcb = lcb_ref[...]
        row = lax.broadcasted_iota(jnp.int32, (CHUNK, CHUNK), 0)
        col = lax.broadcasted_iota(jnp.int32, (CHUNK, CHUNK), 1)
        self.tril = col <= row
        self.sw = jnp.where(self.tril, sw_ref[...], 0.0).astype(BF16)
        self.sb = sb_ref[...]
        self.wa = wa_ref[...]
        self.wx = wx_ref[...]
        self.ba = ba_ref[...]
        self.bx = bx_ref[...]
        lam = lam_ref[...]
        self.neg_c_sp = -LRU_C * _softplus_neg(lam)
        self.dsp_dlam = -_sigmoid(-lam)


def _mixer_a(ld, ldp, w):
    t = {}
    a_x, a_c = ld(AX), ld(AC)
    t["a_x"], t["a_c"], t["a_b"], t["a_z"] = a_x, a_c, ld(AB), ld(AZ)
    ca = a_c * a_x
    ca_p = ldp(AC) * ldp(AX)
    t["ca"], t["ca1"], t["ca2"] = ca, _shift_dn(ca, ca_p, 1), _shift_dn(ca, ca_p, 2)
    t["cv"] = w.caw[2] * ca + w.caw[1] * t["ca1"] + w.caw[0] * t["ca2"]
    t["sa"] = _sigmoid(t["a_z"])
    t["silu_az"] = t["a_z"] * t["sa"]
    t["y_a"] = t["silu_az"] * t["a_b"] * t["cv"]
    return t


def _mixer_b(ld, w):
    t = {}
    v = ld(SV)
    vc = v - jnp.mean(v, axis=1, keepdims=True)
    t["rstd"] = lax.rsqrt(jnp.mean(vc * vc, axis=1, keepdims=True) + EPS)
    t["vn"] = vc * t["rstd"]
    t["z"] = _dot(w.sw, t["vn"].astype(BF16)) + w.sb
    t["s_u"], t["s_z"] = ld(SU), ld(SZ)
    t["ss"] = _sigmoid(t["s_z"])
    t["silu_sz"] = t["s_z"] * t["ss"]
    t["y_s"] = t["silu_sz"] * t["s_u"] * t["z"]
    return t


def _mixer_c(ld, ldp, w, backward):
    t = {}
    r_x, r_xp = ld(RX), ldp(RX)
    t["rx"] = [_shift_dn(r_x, r_xp, 3), _shift_dn(r_x, r_xp, 2), _shift_dn(r_x, r_xp, 1), r_x]
    xc = w.lcb + w.lcw[0] * t["rx"][0] + w.lcw[1] * t["rx"][1] + w.lcw[2] * t["rx"][2] + w.lcw[3] * r_x
    t["xc"] = xc
    xcb = xc.astype(BF16)
    t["r"] = _sigmoid(_dot(xcb, w.wa) + w.ba)
    t["i"] = _sigmoid(_dot(xcb, w.wx) + w.bx)
    la = t["r"] * w.neg_c_sp
    t["a"] = jnp.exp(la)
    t["em"] = _neg_expm1(2.0 * la, t["a"] * t["a"])
    if backward:
        t["inv_mult"] = lax.rsqrt(t["em"])
        t["mult"] = t["em"] * t["inv_mult"]
    else:
        t["mult"] = jnp.sqrt(t["em"])
    t["b"] = t["mult"] * (t["i"] * xc)
    t["r_z"] = ld(RZ)
    t["sr"] = _sigmoid(t["r_z"])
    t["silu_rz"] = t["r_z"] * t["sr"]
    return t


def _mixer_pre_scan(ld, ldp, w, backward):
    t = {**_mixer_a(ld, ldp, w), **_mixer_b(ld, w), **_mixer_c(ld, ldp, w, backward)}
    t["ga"], t["gs"], t["gr"] = _sigmoid(ld(GA)), _sigmoid(ld(GS)), _sigmoid(ld(GR))
    return t


def _weight_specs(n_cb_axis):
    def at(fn):
        return lambda *g: fn(g[n_cb_axis])
    return [
        pl.BlockSpec((3, LANES), at(lambda cb: (0, cb))),
        pl.BlockSpec((None, CHUNK, CHUNK), at(lambda cb: (cb, 0, 0))),
        pl.BlockSpec((None, CHUNK, LANES), at(lambda cb: (cb, 0, 0))),
        pl.BlockSpec((4, LANES), at(lambda cb: (0, cb))),
        pl.BlockSpec((1, LANES), at(lambda cb: (0, cb))),
        pl.BlockSpec((None, LANES, LANES), at(lambda cb: (cb, 0, 0))),
        pl.BlockSpec((None, LANES, LANES), at(lambda cb: (cb, 0, 0))),
        pl.BlockSpec((1, LANES), at(lambda cb: (0, cb))),
        pl.BlockSpec((1, LANES), at(lambda cb: (0, cb))),
        pl.BlockSpec((1, LANES), at(lambda cb: (0, cb))),
    ]


def _chunk_loaders(p_ref, c):
    r0 = pl.multiple_of(c * CHUNK, CHUNK)
    rp = pl.multiple_of(jnp.maximum(c * CHUNK - HALO, 0), HALO)

    def ld(j):
        return p_ref[j, pl.ds(r0, CHUNK), :].astype(F32)

    def ldp(j):
        return jnp.where(c > 0, p_ref[j, pl.ds(rp, HALO), :].astype(F32), 0.0)

    return r0, rp, ld, ldp


def _mixer_fwd(proj, mw):
    _, nb, s, _ = proj.shape
    n_chunks = s // CHUNK

    def body(p_ref, *refs):
        w = _MixerWeights(*refs[:10])
        merged_ref, hs_ref = refs[10:]

        def chunk(c, h_prev):
            r0, _, ld, ldp = _chunk_loaders(p_ref, c)
            t = _mixer_pre_scan(ld, ldp, w, False)
            h = _scan_fwd(t["a"], t["b"], h_prev)
            y_r = t["silu_rz"] * h
            merged = t["ga"] * t["y_a"] + t["gs"] * t["y_s"] + t["gr"] * y_r
            merged_ref[pl.ds(r0, CHUNK), :] = merged.astype(BF16)
            hs_ref[pl.ds(r0, CHUNK), :] = h
            return h[CHUNK - HALO:, :]

        lax.fori_loop(0, n_chunks, chunk, jnp.zeros((HALO, LANES), F32))

    slab = pl.BlockSpec((None, s, LANES), lambda cb, b: (b, 0, cb))
    return pl.pallas_call(
        body, name="mixer_fwd", grid=(D // LANES, nb),
        in_specs=[pl.BlockSpec((N_SEG, None, s, LANES), lambda cb, b: (0, b, 0, cb))] + _weight_specs(0),
        out_specs=[slab, slab],
        out_shape=[jax.ShapeDtypeStruct((nb, s, D), BF16), jax.ShapeDtypeStruct((nb, s, D), F32)],
        compiler_params=_params(("arbitrary", "arbitrary")),
    )(proj, *mw)


def _mixer_bwd(proj, dmerged, hs, mw):
    _, nb, s, _ = proj.shape
    n_chunks = s // CHUNK

    def body(p_ref, dm_ref, hs_ref, *refs):
        w = _MixerWeights(*refs[:10])
        dp_ref, g_caw, g_sw, g_sb, g_lcw, g_vec, g_wa, g_wx = refs[10:]

        @pl.when(pl.program_id(1) == 0)
        def _():
            for ref in (g_caw, g_sw, g_sb, g_lcw, g_vec, g_wa, g_wx):
                ref[...] = jnp.zeros_like(ref)

        def chunk(i, carry):
            dcv_n, dxc_n, lam_n, a_n = carry
            c = n_chunks - 1 - i
            r0, rp, ld, ldp = _chunk_loaders(p_ref, c)
            t = _mixer_pre_scan(ld, ldp, w, True)
            h = hs_ref[pl.ds(r0, CHUNK), :]
            h_p = jnp.where(c > 0, hs_ref[pl.ds(rp, HALO), :], 0.0)
            h_prev = _shift_dn(h, h_p, 1)
            dm = dm_ref[pl.ds(r0, CHUNK), :].astype(F32)
            y_r = t["silu_rz"] * h

            def out(j, val):
                dp_ref[j, pl.ds(r0, CHUNK), :] = val.astype(BF16)

            ga, gs, gr = t["ga"], t["gs"], t["gr"]
            out(GA, dm * t["y_a"] * ga * (1.0 - ga))
            out(GS, dm * t["y_s"] * gs * (1.0 - gs))
            out(GR, dm * y_r * gr * (1.0 - gr))

            dy_a = dm * ga
            out(AZ, dy_a * t["a_b"] * t["cv"] * _dsilu(t["a_z"], t["sa"]))
            out(AB, dy_a * t["silu_az"] * t["cv"])
            dcv = dy_a * t["silu_az"] * t["a_b"]
            dca = w.caw[2] * dcv + w.caw[1] * _shift_up(dcv, dcv_n, 1) + w.caw[0] * _shift_up(dcv, dcv_n, 2)
            out(AC, dca * t["a_x"])
            out(AX, dca * t["a_c"])
            g_caw[2:3, :] += _rowsum(dcv * t["ca"])
            g_caw[1:2, :] += _rowsum(dcv * t["ca1"])
            g_caw[0:1, :] += _rowsum(dcv * t["ca2"])

            dy_s = dm * gs
            out(SZ, dy_s * t["s_u"] * t["z"] * _dsilu(t["s_z"], t["ss"]))
            out(SU, dy_s * t["silu_sz"] * t["z"])
            dz = dy_s * t["silu_sz"] * t["s_u"]
            dzb = dz.astype(BF16)
            g_sb[...] += jnp.broadcast_to(jnp.sum(dz, axis=1, keepdims=True), (CHUNK, LANES))
            g_sw[...] += _dot_nt(dzb, t["vn"].astype(BF16))
            dvn = _dot_tn(w.sw, dzb)
            vn = t["vn"]
            out(SV, t["rstd"] * (dvn - jnp.mean(dvn, axis=1, keepdims=True)
                                 - vn * jnp.mean(dvn * vn, axis=1, keepdims=True)))

            dy_r = dm * gr
            out(RZ, dy_r * h * _dsilu(t["r_z"], t["sr"]))
            lam = _scan_rev(_shift_up(t["a"], a_n, 1), dy_r * t["silu_rz"], lam_n)
            a, r, ig, xc, mult = t["a"], t["r"], t["i"], t["xc"], t["mult"]
            d_i = lam * mult * xc
            d_mult = lam * ig * xc
            dxc = lam * mult * ig
            dla = lam * h_prev * a - d_mult * ((1.0 - t["em"]) * t["inv_mult"])
            g_vec[3:4, :] += _rowsum(dla * r) * (-LRU_C * w.dsp_dlam)
            dpr = (dla * w.neg_c_sp) * r * (1.0 - r)
            dpi = d_i * ig * (1.0 - ig)
            dprb, dpib, xcb = dpr.astype(BF16), dpi.astype(BF16), xc.astype(BF16)
            g_wa[...] += _dot_tn(xcb, dprb)
            g_wx[...] += _dot_tn(xcb, dpib)
            g_vec[1:2, :] += _rowsum(dpr)
            g_vec[2:3, :] += _rowsum(dpi)
            dxc = dxc + _dot_nt(dprb, w.wa) + _dot_nt(dpib, w.wx)
            g_vec[0:1, :] += _rowsum(dxc)
            out(RX, w.lcw[3] * dxc + w.lcw[2] * _shift_up(dxc, dxc_n, 1)
                + w.lcw[1] * _shift_up(dxc, dxc_n, 2) + w.lcw[0] * _shift_up(dxc, dxc_n, 3))
            for j in range(4):
                g_lcw[j:j + 1, :] += _rowsum(dxc * t["rx"][j])
            return dcv[:HALO, :], dxc[:HALO, :], lam[:HALO, :], a[:HALO, :]

        zero = jnp.zeros((HALO, LANES), F32)
        lax.fori_loop(0, n_chunks, chunk, (zero, zero, zero, zero))

        @pl.when(pl.program_id(1) == nb - 1)
        def _():
            g_sw[...] = jnp.where(w.tril, g_sw[...], 0.0)

    slab = lambda dt: pl.BlockSpec((None, s, LANES), lambda cb, b: (b, 0, cb))
    seg = pl.BlockSpec((N_SEG, None, s, LANES), lambda cb, b: (0, b, 0, cb))
    rows = lambda n: pl.BlockSpec((n, LANES), lambda cb, b: (0, cb))
    sq = pl.BlockSpec((None, LANES, LANES), lambda cb, b: (cb, 0, 0))
    n_cb = D // LANES
    return pl.pallas_call(
        body, name="mixer_bwd", grid=(n_cb, nb),
        in_specs=[seg, slab(BF16), slab(F32)] + _weight_specs(0),
        out_specs=[seg, rows(3), sq, sq, rows(4), rows(8), sq, sq],
        out_shape=[
            jax.ShapeDtypeStruct(proj.shape, BF16),
            jax.ShapeDtypeStruct((3, D), F32),
            jax.ShapeDtypeStruct((n_cb, CHUNK, CHUNK), F32),
            jax.ShapeDtypeStruct((n_cb, CHUNK, LANES), F32),
            jax.ShapeDtypeStruct((4, D), F32),
            jax.ShapeDtypeStruct((8, D), F32),
            jax.ShapeDtypeStruct((n_cb, LANES, LANES), F32),
            jax.ShapeDtypeStruct((n_cb, LANES, LANES), F32),
        ],
        compiler_params=_params(("arbitrary", "arbitrary")),
    )(proj, dmerged, hs, *mw)


def _row_tile(s, want):
    return want if s % want == 0 else s


def _norm_mod(x, gain, shift, scale):
    nb, s, _ = x.shape
    tm = _row_tile(s, 512)

    def body(x_ref, g_ref, sh_ref, sc_ref, h_ref):
        xv = x_ref[...]
        r = lax.rsqrt(jnp.mean(xv * xv, axis=1, keepdims=True) + EPS)
        h_ref[...] = ((xv * r) * g_ref[...] * (1.0 + sc_ref[...]) + sh_ref[...]).astype(BF16)

    tile = pl.BlockSpec((None, tm, D), lambda b, m: (b, m, 0))
    vec = pl.BlockSpec((None, 1, D), lambda b, m: (b, 0, 0))
    return pl.pallas_call(
        body, name="norm_mod", grid=(nb, s // tm),
        in_specs=[tile, pl.BlockSpec((1, D), lambda b, m: (0, 0)), vec, vec],
        out_specs=tile, out_shape=jax.ShapeDtypeStruct(x.shape, BF16),
        compiler_params=_params(("arbitrary", "arbitrary")),
    )(x, gain, shift, scale)


def _in_proj(h, wg, dep):
    nb, s, _ = h.shape

    def body(h_ref, w_ref, dep_ref, o_ref):
        o_ref[...] = _dot(h_ref[...], w_ref[...]).astype(BF16)

    return pl.pallas_call(
        body, name="in_proj", grid=(nb, N_DEV * UNITS_PER_DEV),
        in_specs=[pl.BlockSpec((None, s, D), lambda b, u: (b, 0, 0)),
                  pl.BlockSpec((None, D, UNIT), lambda b, u: (u // UNITS_PER_DEV, 0, u % UNITS_PER_DEV)),
                  pl.BlockSpec((8, LANES), lambda b, u: (0, 0))],
        out_specs=pl.BlockSpec((None, None, s, UNIT), lambda b, u: (u // 2, b, 0, u % 2)),
        out_shape=jax.ShapeDtypeStruct((N_SEG, nb, s, D), BF16),
        compiler_params=_params(("arbitrary", "arbitrary")),
    )(h, wg, dep)


def _out_proj(x, merged, wout, gate):
    nb, s, _ = x.shape
    tm = _row_tile(s, 512)

    def body(x_ref, m_ref, w_ref, g_ref, o_ref):
        o_ref[...] = x_ref[...] + g_ref[...] * _dot(m_ref[...], w_ref[...])

    tile = pl.BlockSpec((None, tm, D), lambda b, m: (b, m, 0))
    return pl.pallas_call(
        body, name="out_proj", grid=(nb, s // tm),
        in_specs=[tile, tile, pl.BlockSpec((D, D), lambda b, m: (0, 0)),
                  pl.BlockSpec((None, 1, D), lambda b, m: (b, 0, 0))],
        out_specs=tile, out_shape=jax.ShapeDtypeStruct(x.shape, F32),
        compiler_params=_params(("arbitrary", "arbitrary")),
    )(x, merged, wout, gate)


def _out_proj_norm(x, merged, wout, gate, gain, shift, scale):
    nb, s, _ = x.shape
    tm = _row_tile(s, 512)

    def body(x_ref, m_ref, w_ref, g_ref, gn_ref, sh_ref, sc_ref, o_ref, h_ref):
        xv = x_ref[...] + g_ref[...] * _dot(m_ref[...], w_ref[...])
        o_ref[...] = xv
        r = lax.rsqrt(jnp.mean(xv * xv, axis=1, keepdims=True) + EPS)
        h_ref[...] = ((xv * r) * gn_ref[...] * (1.0 + sc_ref[...]) + sh_ref[...]).astype(BF16)

    tile = pl.BlockSpec((None, tm, D), lambda b, m: (b, m, 0))
    vec = pl.BlockSpec((None, 1, D), lambda b, m: (b, 0, 0))
    return pl.pallas_call(
        body, name="out_proj_norm", grid=(nb, s // tm),
        in_specs=[tile, tile, pl.BlockSpec((D, D), lambda b, m: (0, 0)), vec,
                  pl.BlockSpec((1, D), lambda b, m: (0, 0)), vec, vec],
        out_specs=[tile, tile],
        out_shape=[jax.ShapeDtypeStruct(x.shape, F32), jax.ShapeDtypeStruct(x.shape, BF16)],
        compiler_params=_params(("arbitrary", "arbitrary")),
    )(x, merged, wout, gate, gain, shift, scale)


def _loss_head(x, gain, target):
    nb, s, _ = x.shape
    tm = _row_tile(s, 512)

    def body(x_ref, g_ref, t_ref, loss_ref, dx_ref, dg_ref):
        first = (pl.program_id(0) == 0) & (pl.program_id(1) == 0)
        last = (pl.program_id(0) == nb - 1) & (pl.program_id(1) == s // tm - 1)

        @pl.when(first)
        def _():
            loss_ref[...] = jnp.zeros_like(loss_ref)
            dg_ref[...] = jnp.zeros_like(dg_ref)

        xv = x_ref[...]
        r = lax.rsqrt(jnp.mean(xv * xv, axis=1, keepdims=True) + EPS)
        xn = xv * r
        g = g_ref[...]
        e = xn * g - t_ref[...]
        loss_ref[...] += _rowsum(e * e) * (0.5 / D)
        dy = e * (1.0 / D)
        dg_ref[...] += _rowsum(dy * xn)
        dxn = dy * g
        dx_ref[...] = r * (dxn - xn * jnp.mean(dxn * xn, axis=1, keepdims=True))

        @pl.when(last)
        def _():
            loss_ref[...] = jnp.broadcast_to(jnp.sum(loss_ref[...], axis=1, keepdims=True), (1, D))

    tile = pl.BlockSpec((None, tm, D), lambda b, m: (b, m, 0))
    vec = pl.BlockSpec((1, D), lambda b, m: (0, 0))
    return pl.pallas_call(
        body, name="loss_head", grid=(nb, s // tm),
        in_specs=[tile, vec, tile], out_specs=[vec, tile, vec],
        out_shape=[jax.ShapeDtypeStruct((1, D), F32), jax.ShapeDtypeStruct(x.shape, F32),
                   jax.ShapeDtypeStruct((1, D), F32)],
        compiler_params=_params(("arbitrary", "arbitrary")),
    )(x, gain, target)


def _out_proj_bwd(dxo, merged, wout, gate):
    nb, s, _ = dxo.shape
    tm = _row_tile(s, 512)

    def body(d_ref, m_ref, w_ref, g_ref, dm_ref, gw_ref, dg_ref):
        @pl.when((pl.program_id(0) == 0) & (pl.program_id(1) == 0))
        def _():
            gw_ref[...] = jnp.zeros_like(gw_ref)

        @pl.when(pl.program_id(1) == 0)
        def _():
            dg_ref[...] = jnp.zeros_like(dg_ref)

        d = d_ref[...]
        m = m_ref[...]
        wv = w_ref[...]
        dg_ref[...] += _rowsum(d * _dot(m, wv))
        dout = (d * g_ref[...]).astype(BF16)
        dm_ref[...] = _dot_nt(dout, wv).astype(BF16)
        gw_ref[...] += _dot_tn(m, dout)

    tile = pl.BlockSpec((None, tm, D), lambda b, m: (b, m, 0))
    vec = pl.BlockSpec((None, 1, D), lambda b, m: (b, 0, 0))
    full = pl.BlockSpec((D, D), lambda b, m: (0, 0))
    return pl.pallas_call(
        body, name="out_proj_bwd", grid=(nb, s // tm),
        in_specs=[tile, tile, full, vec], out_specs=[tile, full, vec],
        out_shape=[jax.ShapeDtypeStruct(dxo.shape, BF16), jax.ShapeDtypeStruct((D, D), F32),
                   jax.ShapeDtypeStruct((nb, 1, D), F32)],
        compiler_params=_params(("arbitrary", "arbitrary")),
    )(dxo, merged, wout, gate)


def _in_proj_bwd_h(dproj, wg, dep):
    _, nb, s, _ = dproj.shape
    tm = _row_tile(s, 1024)

    def body(dp0_ref, dp1_ref, w0_ref, w1_ref, w2_ref, w3_ref, dep_ref, dh_ref):
        j = pl.program_id(2)
        part = (_dot_nt(dp0_ref[...], jnp.concatenate([w0_ref[...], w1_ref[...]], axis=1))
                + _dot_nt(dp1_ref[...], jnp.concatenate([w2_ref[...], w3_ref[...]], axis=1)))

        @pl.when(j == 0)
        def _():
            dh_ref[...] = part

        @pl.when(j > 0)
        def _():
            dh_ref[...] += part

    def seg(k):
        return pl.BlockSpec((None, None, tm, D), lambda b, m, j: (2 * j + k, b, m, 0))

    def unit(k):
        return pl.BlockSpec((None, D, UNIT),
                            lambda b, m, j: ((4 * j + k) // UNITS_PER_DEV, 0, (4 * j + k) % UNITS_PER_DEV))

    return pl.pallas_call(
        body, name="in_proj_bwd_h", grid=(nb, s // tm, N_SEG // 2),
        in_specs=[seg(0), seg(1), unit(0), unit(1), unit(2), unit(3),
                  pl.BlockSpec((8, LANES), lambda b, m, j: (0, 0))],
        out_specs=pl.BlockSpec((None, tm, D), lambda b, m, j: (b, m, 0)),
        out_shape=jax.ShapeDtypeStruct((nb, s, D), F32),
        compiler_params=_params(("arbitrary", "arbitrary", "arbitrary")),
    )(dproj, dproj, wg, wg, wg, wg, dep)


def _norm_mod_bwd(dh, x, dxo, gain, scale):
    nb, s, _ = x.shape
    tm = _row_tile(s, 512)

    def body(dh_ref, x_ref, dxo_ref, g_ref, sc_ref, dx_ref, dsh_ref, dsc_ref, dg_ref):
        b, m = pl.program_id(0), pl.program_id(1)

        @pl.when((b == 0) & (m == 0))
        def _():
            dg_ref[...] = jnp.zeros_like(dg_ref)

        @pl.when(m == 0)
        def _():
            dsh_ref[...] = jnp.zeros_like(dsh_ref)
            dsc_ref[...] = jnp.zeros_like(dsc_ref)

        dh = dh_ref[...]
        xv = x_ref[...]
        r = lax.rsqrt(jnp.mean(xv * xv, axis=1, keepdims=True) + EPS)
        xn = xv * r
        g = g_ref[...]
        one_sc = 1.0 + sc_ref[...]
        dsh_ref[...] += _rowsum(dh)
        dsc_ref[...] += _rowsum(dh * (xn * g))
        dg_ref[...] += _rowsum(dh * one_sc * xn)
        dxn = dh * (g * one_sc)
        dx_ref[...] = dxo_ref[...] + r * (dxn - xn * jnp.mean(dxn * xn, axis=1, keepdims=True))

    tile = pl.BlockSpec((None, tm, D), lambda b, m: (b, m, 0))
    vec = pl.BlockSpec((None, 1, D), lambda b, m: (b, 0, 0))
    one = pl.BlockSpec((1, D), lambda b, m: (0, 0))
    return pl.pallas_call(
        body, name="norm_mod_bwd", grid=(nb, s // tm),
        in_specs=[tile, tile, tile, one, vec],
        out_specs=[tile, vec, vec, one],
        out_shape=[jax.ShapeDtypeStruct(x.shape, F32), jax.ShapeDtypeStruct((nb, 1, D), F32),
                   jax.ShapeDtypeStruct((nb, 1, D), F32), jax.ShapeDtypeStruct((1, D), F32)],
        compiler_params=_params(("arbitrary", "arbitrary")),
    )(dh, x, dxo, gain, scale)


def _in_proj_bwd_w(h, dproj, dep):
    nb, s, _ = h.shape
    tm = _row_tile(s, 2048)
    n_m = s // tm

    def body(h_ref, dp_ref, dep_ref, o_ref, acc_ref):
        b, m = pl.program_id(1), pl.program_id(2)

        @pl.when((b == 0) & (m == 0))
        def _():
            acc_ref[...] = jnp.zeros_like(acc_ref)

        acc_ref[...] += _dot_tn(h_ref[...], dp_ref[...])

        @pl.when((b == nb - 1) & (m == n_m - 1))
        def _():
            o_ref[0] = acc_ref[:, :UNIT].astype(BF16)
            o_ref[1] = acc_ref[:, UNIT:].astype(BF16)

    return pl.pallas_call(
        body, name="in_proj_bwd_w", grid=(N_SEG, nb, n_m),
        in_specs=[pl.BlockSpec((None, tm, D), lambda j, b, m: (b, m, 0)),
                  pl.BlockSpec((None, None, tm, D), lambda j, b, m: (j, b, m, 0)),
                  pl.BlockSpec((8, LANES), lambda j, b, m: (0, 0))],
        out_specs=pl.BlockSpec((2, D, UNIT), lambda j, b, m: (j, 0, 0)),
        out_shape=jax.ShapeDtypeStruct((2 * N_SEG, D, UNIT), BF16),
        scratch_shapes=[pltpu.VMEM((D, D), F32)],
        compiler_params=_params(("arbitrary", "arbitrary", "arbitrary")),
    )(h, dproj, dep)


def _mod_proj(c_all, w_mod, b_mod_mine):
    nl, _, ncol = w_mod.shape
    nbg = c_all.shape[0]

    def body(c_ref, w_ref, b_ref, o_ref):
        cv = c_ref[...]
        o_ref[...] = jnp.dot(cv * jax.nn.sigmoid(cv), w_ref[...], preferred_element_type=F32,
                             precision=lax.Precision.HIGHEST) + b_ref[...]

    return pl.pallas_call(
        body, name="mod_proj", grid=(nl,),
        in_specs=[pl.BlockSpec((nbg, D), lambda l: (0, 0)), pl.BlockSpec((None, D, ncol), lambda l: (l, 0, 0)),
                  pl.BlockSpec((None, 1, ncol), lambda l: (l, 0, 0))],
        out_specs=pl.BlockSpec((None, nbg, ncol), lambda l: (l, 0, 0)),
        out_shape=jax.ShapeDtypeStruct((nl, nbg, ncol), F32),
        compiler_params=_params(("arbitrary",)),
    )(c_all, w_mod, b_mod_mine)


def _mod_grad(c_all, dmod_all, dmod_mine):
    nl, nbg, ncol = dmod_mine.shape

    def body(c_ref, da_ref, dm_ref, gw_ref, gb_ref):
        cv = c_ref[...]
        gw_ref[...] = lax.dot_general(cv * jax.nn.sigmoid(cv), dm_ref[...], (((0,), (0,)), ((), ())),
                                      preferred_element_type=F32, precision=lax.Precision.HIGHEST)
        gb_ref[...] = _rowsum(da_ref[...])

    return pl.pallas_call(
        body, name="mod_grad", grid=(nl,),
        in_specs=[pl.BlockSpec((nbg, D), lambda l: (0, 0)), pl.BlockSpec((None, nbg, 3 * D), lambda l: (l, 0, 0)),
                  pl.BlockSpec((None, nbg, ncol), lambda l: (l, 0, 0))],
        out_specs=[pl.BlockSpec((None, D, ncol), lambda l: (l, 0, 0)),
                   pl.BlockSpec((None, 1, 3 * D), lambda l: (l, 0, 0))],
        out_shape=[jax.ShapeDtypeStruct((nl, D, ncol), F32), jax.ShapeDtypeStruct((nl, 1, 3 * D), F32)],
        compiler_params=_params(("arbitrary",)),
    )(c_all, dmod_all, dmod_mine)


def _adamw(parts, w, m, v, name, layer=None, prev=None):
    n_parts, n_u, n_r, cu = parts.shape
    assert w.shape[-2:] == (n_r, n_u * cu), (parts.shape, w.shape)
    tr = n_r
    for cand in (512, 256, 128):
        if n_r > cand and n_r % cand == 0:
            tr = cand
            break
    n_prev = 0 if prev is None else 4

    def body(p_ref, w_ref, m_ref, v_ref, *rest):
        g_ref, d_ref, nm_ref, nv_ref = rest[n_prev:]
        g = p_ref[0].astype(F32)
        for k in range(1, n_parts):
            g = g + p_ref[k].astype(F32)
        m2 = ADAM_B1 * m_ref[...] + (1.0 - ADAM_B1) * g
        v2 = ADAM_B2 * v_ref[...] + (1.0 - ADAM_B2) * (g * g)
        m_hat = m2 / (1.0 - ADAM_B1 ** ADAM_STEP)
        v_hat = v2 / (1.0 - ADAM_B2 ** ADAM_STEP)
        g_ref[...] = g
        d_ref[...] = -ADAM_LR * (m_hat / (jnp.sqrt(v_hat) + ADAM_EPS) + ADAM_WD * w_ref[...])
        nm_ref[...] = m2
        nv_ref[...] = v2

    if layer is None:
        tile = pl.BlockSpec((tr, cu), lambda u, i: (i, u))
    else:
        tile = pl.BlockSpec((None, tr, cu), lambda u, i: (layer, i, u))
    shp = jax.ShapeDtypeStruct(w.shape, F32)
    return pl.pallas_call(
        body, name=name, grid=(n_u, n_r // tr),
        in_specs=[pl.BlockSpec((n_parts, None, tr, cu), lambda u, i: (0, u, i, 0)), tile, tile, tile]
        + [pl.BlockSpec(memory_space=pl.ANY)] * n_prev,
        out_specs=[tile, tile, tile, tile], out_shape=[shp, shp, shp, shp],
        input_output_aliases={4 + k: k for k in range(n_prev)},
        compiler_params=_params(("arbitrary", "arbitrary")),
    )(parts, w, m, v, *(prev or ()))


def _gathered_cols(g, inner):
    k = len(inner)
    perm = tuple(range(1, k + 1)) + (0, k + 1)
    t = jnp.transpose(g, perm)
    return t.reshape(tuple(inner) + (g.shape[0] * g.shape[-1],))


def _pair_blocks(wh):
    z = jnp.zeros((8, 64, 64), wh.dtype)
    w2 = wh.reshape(8, 2, 64, 64)
    top = jnp.concatenate([w2[:, 0], z], axis=2)
    bot = jnp.concatenate([z, w2[:, 1]], axis=2)
    return jnp.concatenate([top, bot], axis=1).astype(BF16)


def _unpair_blocks(g):
    return jnp.stack([g[:, :64, :64], g[:, 64:, 64:]], axis=1).reshape(16, 64, 64)


FLAT_ROWS = 512


def _pack_rows(arrays, lead=0):
    parts = [a.reshape(a.shape[:lead] + (-1, LANES)) for a in arrays]
    rows = jnp.concatenate(parts, axis=lead)
    pad = [(0, 0)] * rows.ndim
    pad[lead] = (0, (-rows.shape[lead]) % FLAT_ROWS)
    return jnp.pad(rows, pad)


def kernel(x, c, norm_gain, w_mod, b_mod, w_in, w_out, conv_a_w, sgu_w, sgu_b, lru_conv_w, lru_conv_b, lru_wa, lru_ba, lru_wx, lru_bx, lru_lambda, final_gain, loss_target, m_norm_gain, m_w_mod, m_b_mod, m_w_in, m_w_out, m_conv_a_w, m_sgu_w, m_sgu_b, m_lru_conv_w, m_lru_conv_b, m_lru_wa, m_lru_ba, m_lru_wx, m_lru_bx, m_lru_lambda, m_final_gain, v_norm_gain, v_w_mod, v_b_mod, v_w_in, v_w_out, v_conv_a_w, v_sgu_w, v_sgu_b, v_lru_conv_w, v_lru_conv_b, v_lru_wa, v_lru_ba, v_lru_wx, v_lru_bx, v_lru_lambda, v_final_gain):
    nl = w_in.shape[0]
    nb, s, _ = x.shape
    me = _my_index()
    mod_cols = w_mod.shape[2]


    small = jnp.concatenate([c.reshape(-1, LANES), conv_a_w.reshape(-1, LANES), lru_conv_w.reshape(-1, LANES)])
    n_c, n_ca = nb * D // LANES, nl * 3
    n_small = small.shape[0]
    small = jnp.pad(small, ((0, (-n_small) % 8), (0, 0)))
    small_all, _ = _all_gather(small, "gather_small")
    c_all = small_all[:, :n_c].reshape(N_DEV * nb, D)

    w_in_b = [w_in[0].astype(BF16)] + list(w_in[1:].astype(BF16))

    def start_w_in(l, dep):
        return _split_start(w_in_b[l], _peers_same_core, False, "gather_w_in_start", dep)

    conv_a_full = _gathered_cols(small_all[:, n_c:n_c + n_ca].reshape(N_DEV, nl, 3, LANES), (nl, 3))
    lru_conv_full = _gathered_cols(small_all[:, n_c + n_ca:n_small].reshape(N_DEV, nl, 4, LANES), (nl, 4))

    def gathered_w_in(started, after):
        block, land = _split_wait(started, after, _peers_same_core, False, False, "gather_w_in_wait")
        return _gather_finish(block, land, "gather_w_in_finish")

    sgu_b_lanes = jnp.broadcast_to(sgu_b[..., None], sgu_b.shape + (LANES,))
    mws = []
    for l in range(nl):
        mws.append((conv_a_full[l], sgu_w[l], sgu_b_lanes[l], lru_conv_full[l], lru_conv_b[l][None, :],
                    _pair_blocks(lru_wa[l]), _pair_blocks(lru_wx[l]), lru_ba[l].reshape(1, D),
                    lru_bx[l].reshape(1, D), lru_lambda[l][None, :]))

    rep_names = ["sgu_w", "sgu_b", "lru_conv_b", "lru_wa", "lru_ba", "lru_wx", "lru_bx", "lru_lambda"]
    rep_w = dict(sgu_w=sgu_w, sgu_b=sgu_b, lru_conv_b=lru_conv_b, lru_wa=lru_wa, lru_ba=lru_ba,
                 lru_wx=lru_wx, lru_bx=lru_bx, lru_lambda=lru_lambda)
    rep_m = dict(sgu_w=m_sgu_w, sgu_b=m_sgu_b, lru_conv_b=m_lru_conv_b, lru_wa=m_lru_wa,
                 lru_ba=m_lru_ba, lru_wx=m_lru_wx, lru_bx=m_lru_bx, lru_lambda=m_lru_lambda)
    rep_v = dict(sgu_w=v_sgu_w, sgu_b=v_sgu_b, lru_conv_b=v_lru_conv_b, lru_wa=v_lru_wa,
                 lru_ba=v_lru_ba, lru_wx=v_lru_wx, lru_bx=v_lru_bx, lru_lambda=v_lru_lambda)

    rep_w_all, rep_m_all, rep_v_all = [_pack_rows([src[n] for n in rep_names], lead=1)
                                       for src in (rep_w, rep_m, rep_v)]
    early = [rep_w_all, rep_m_all, rep_v_all] + w_in_b[1:] + [a for mw in mws for a in mw]

    b_mod_mine = lax.dynamic_slice_in_dim(b_mod, me * mod_cols, mod_cols, axis=1)[:, None, :]
    mod_mine = _mod_proj(c_all, w_mod, b_mod_mine)
    mod_all, mod_token = _all_gather(mod_mine.reshape(nl * N_DEV * nb, mod_cols), "gather_mod", dep=w_in_b[0])
    ici = {0: start_w_in(0, mod_token)}
    mod_full = _gathered_cols(mod_all.reshape(N_DEV, nl, N_DEV * nb, mod_cols), (nl, N_DEV * nb))
    mod_loc = lax.dynamic_slice_in_dim(mod_full, me * nb, nb, axis=1)
    shift, scale, gate = [mod_loc[:, :, j * D:(j + 1) * D][:, :, None, :] for j in range(3)]

    xs, hs_bf, projs, mergeds, states, wg = [], [], [], [], [], []
    xl = x
    d2d = {}
    wo_started = _split_start(w_out.astype(BF16).reshape(nl * (D // N_DEV), D), _peers_all, False,
                              "gather_w_out_start", ici[0][4])
    wo = None
    h = _norm_mod(xl, _after(norm_gain[0][None, :], wo_started[4]), shift[0], scale[0])
    for l in range(nl):
        if l == 0:
            wg_l, token = gathered_w_in(ici[0], [h] + early)
            ici[1] = start_w_in(1, token)
            dep = ici[1][4]
        else:
            wg_l = _forward_wait(d2d[l], h, "gather_w_in_d2d_wait")
            dep = d2d[l][4]
        wg.append(wg_l)
        proj = _in_proj(h, wg_l, dep)
        merged, st = _mixer_fwd(proj, mws[l])
        xs.append(xl), hs_bf.append(h), projs.append(proj), mergeds.append(merged), states.append(st)
        gate_l = gate[l]
        if l + 1 < nl:
            block, land = _split_wait(ici[l + 1], merged, _peers_same_core, False, False, "gather_w_in_wait")
            d2d[l + 1] = _forward_start(block, land, "gather_w_in_d2d_start")
            gate_l = _after(gate_l, d2d[l + 1][4])
            if l + 2 < nl:
                ici[l + 2] = start_w_in(l + 2, d2d[l + 1][4])
                gate_l = _after(gate_l, ici[l + 2][4])
        if wo is None:
            _, wo_all = _split_wait(wo_started, merged, _peers_all, False, True, "gather_w_out_wait")
            wo = jnp.transpose(wo_all.reshape(N_DEV, nl, D // N_DEV, D), (1, 0, 2, 3)).reshape(nl, D, D)
        if l + 1 < nl:
            xl, h = _out_proj_norm(xl, merged, wo[l], gate_l, norm_gain[l + 1][None, :], shift[l + 1], scale[l + 1])
        else:
            xl = _out_proj(xl, merged, wo[l], gate_l)

    loss_row, dx, g_final = _loss_head(xl, final_gain[None, :], loss_target)

    res_big, g_conv = {}, [None] * nl
    dmods = [None] * nl

    def finish_exchange(pending, after):
        l, h_in, h_out, h_rep = pending
        r_in, r_out, r_rep = _multi_wait([h_in, h_out, h_rep], after, "scatter_wait")
        res_big["w_in"] = _adamw(r_in, w_in, m_w_in, v_w_in, "adamw_w_in", l, res_big.get("w_in"))
        res_big["w_out"] = _adamw(r_out, w_out, m_w_out, v_w_out, "adamw_w_out", l, res_big.get("w_out"))
        res_big["rep"] = _adamw(r_rep[:, None], rep_w_all, rep_m_all, rep_v_all, "adamw_rep", l, res_big.get("rep"))

    pending = None
    g_gains = [None] * nl
    for l in reversed(range(nl)):
        dmerged, gw_out, dgate = _out_proj_bwd(dx, mergeds[l], wo[l], gate[l])
        dproj, g_caw, g_sw, g_sb, g_lcw, g_vec, g_wa, g_wx = _mixer_bwd(projs[l], dmerged, states[l], mws[l])
        g_conv[l] = (g_caw, g_lcw)
        rep_g = dict(
            sgu_w=g_sw, sgu_b=g_sb[:, :, 0], lru_conv_b=g_vec[0],
            lru_wa=_unpair_blocks(g_wa), lru_ba=g_vec[1].reshape(16, 64), lru_wx=_unpair_blocks(g_wx),
            lru_bx=g_vec[2].reshape(16, 64), lru_lambda=g_vec[3])
        rep_block = _pack_rows([rep_g[n] for n in rep_names])
        (h_out, h_rep), token = _multi_start([(gw_out.reshape(N_DEV, 1, D // N_DEV, D), True), (rep_block, False)],
                                             "scatter_small_start")
        gw_in = _in_proj_bwd_w(hs_bf[l], dproj, token)
        (h_in,), token = _multi_start([(gw_in.reshape(N_DEV, UNITS_PER_DEV, D, UNIT), True)], "scatter_w_in_start")
        started = (l, h_in, h_out, h_rep)
        dh = _in_proj_bwd_h(dproj, wg[l], token)
        dx, dshift, dscale, g_gain = _norm_mod_bwd(dh, xs[l], dx, norm_gain[l][None, :], scale[l])
        g_gains[l] = g_gain
        dmods[l] = jnp.concatenate([dshift, dscale, dgate], axis=2)[:, 0, :]
        if pending is not None:
            finish_exchange(pending, dx)
        pending = started

    conv_parts = jnp.concatenate(
        [jnp.stack([g_conv[l][0] for l in range(nl)]).reshape(nl * 3, N_DEV, LANES),
         jnp.stack([g_conv[l][1] for l in range(nl)]).reshape(nl * 4, N_DEV, LANES)], axis=0)
    conv_parts = jnp.transpose(conv_parts, (1, 0, 2))[:, None]
    conv_recv = _all_to_all(conv_parts, "scatter_conv")

    dmod_loc = jnp.stack(dmods).reshape(nl * nb, 3 * D)
    gain_rows = jnp.pad(jnp.concatenate(g_gains + [g_final, loss_row], axis=0),
                        ((0, (-(nl + 2)) % 8), (0, 2 * D)))
    tail_g, _ = _all_gather(jnp.concatenate([dmod_loc, gain_rows], axis=0), "gather_dmod",
                            dep=[res_big[k][3] for k in ("w_in", "w_out", "rep")])
    loss = jnp.sum(tail_g[:, nl * nb + nl + 1, 0])
    dmod_g = tail_g[:, :nl * nb]
    gain_parts = tail_g[:, nl * nb:nl * nb + nl + 1, :D][:, None]
    gain_cat = lambda a, b: jnp.concatenate([a, b[None, :]], axis=0)
    res_gain = _adamw(gain_parts, gain_cat(norm_gain, final_gain), gain_cat(m_norm_gain, m_final_gain),
                      gain_cat(v_norm_gain, v_final_gain), "adamw_gain")
    dmod_all = jnp.transpose(dmod_g.reshape(N_DEV, nl, nb, 3 * D), (1, 0, 2, 3)).reshape(nl, N_DEV * nb, 3 * D)
    dmod_mine = lax.dynamic_slice_in_dim(dmod_all, me * mod_cols, mod_cols, axis=2)
    gw_mod, gb_mod = _mod_grad(c_all, dmod_all, dmod_mine)
    res_w_mod = _adamw(gw_mod.reshape(1, 1, nl * D, mod_cols), w_mod.reshape(nl * D, mod_cols),
                       m_w_mod.reshape(nl * D, mod_cols), v_w_mod.reshape(nl * D, mod_cols), "adamw_w_mod")
    res_w_mod = [a.reshape(nl, D, mod_cols) for a in res_w_mod]
    res_b_mod = _adamw(gb_mod.reshape(1, 1, nl, 3 * D), b_mod, m_b_mod, v_b_mod, "adamw_b_mod")
    finish_exchange(pending, res_b_mod[1])

    cat = lambda a, b: jnp.concatenate([a.reshape(nl * 3, LANES), b.reshape(nl * 4, LANES)], axis=0)
    res_conv = _adamw(conv_recv, cat(conv_a_w, lru_conv_w), cat(m_conv_a_w, m_lru_conv_w),
                      cat(v_conv_a_w, v_lru_conv_w), "adamw_conv")
    res_conv_a = [a[:nl * 3].reshape(nl, 3, LANES) for a in res_conv]
    res_lru_conv = [a[nl * 3:].reshape(nl, 4, LANES) for a in res_conv]

    res_rep = []
    for k in range(4):
        off, d = 0, {}
        for n in rep_names:
            n_rows = rep_w[n][0].size // LANES
            d[n] = res_big["rep"][k][:, off:off + n_rows].reshape(rep_w[n].shape)
            off += n_rows
        res_rep.append(d)

    def leaf(k, name):
        if name == "norm_gain":
            return res_gain[k][:nl]
        if name == "final_gain":
            return res_gain[k][nl]
        if name == "w_mod":
            return res_w_mod[k]
        if name == "b_mod":
            return res_b_mod[k]
        if name in ("w_in", "w_out"):
            return res_big[name][k]
        if name == "conv_a_w":
            return res_conv_a[k]
        if name == "lru_conv_w":
            return res_lru_conv[k]
        return res_rep[k][name]

    order = ["norm_gain", "w_mod", "b_mod", "w_in", "w_out", "conv_a_w", "sgu_w", "sgu_b", "lru_conv_w",
             "lru_conv_b", "lru_wa", "lru_ba", "lru_wx", "lru_bx", "lru_lambda", "final_gain"]
    outs = [loss, dx]
    for k in range(4):
        outs += [leaf(k, n) for n in order]
    return tuple(outs)
```

```python
import functools

import jax
import jax.numpy as jnp
from jax import lax
from jax.experimental import pallas as pl
from jax.experimental.pallas import tpu as pltpu

F32 = jnp.float32
BF16 = jnp.bfloat16

D = 1024
N_DEV = 8
N_SEG = 12
LANES = 128
SUBLANES = 8
CHUNK = 128
HALO = 16
FWD_CHUNKS_PER_TRIP = 4
BWD_CHUNKS_PER_TRIP = 4
UNIT = 512
UNITS_PER_DEV = 3
EPS = 1e-6
LRU_C = 8.0
ADAM_LR, ADAM_B1, ADAM_B2, ADAM_EPS, ADAM_WD, ADAM_STEP = 0.001, 0.9, 0.999, 1e-08, 0.01, 10
VMEM_LIMIT = 56 * 1024 * 1024

AX, AB, AC, AZ, SU, SV, SZ, RX, RZ, GA, GS, GR = range(N_SEG)
MESH = pl.DeviceIdType.MESH


def _params(sem=None):
    return pltpu.CompilerParams(dimension_semantics=sem, vmem_limit_bytes=VMEM_LIMIT)


def _my_index():
    return 4 * lax.axis_index("x") + 2 * lax.axis_index("y") + lax.axis_index("c")


def _all_gather(block, name, dep=None):
    deps = [] if dep is None else list(dep) if isinstance(dep, (list, tuple)) else [dep]

    def body(x_ref, *refs):
        out_ref, token, send_sems, recv_sems, local_sem = refs[-5:]
        x, y, c = lax.axis_index("x"), lax.axis_index("y"), lax.axis_index("c")
        me, sibling = (x, y, c), (x, y, 1 - c)
        chips = [(1 - x, y), (x, 1 - y), (1 - x, 1 - y)]
        token[...] = jnp.zeros_like(token)

        def rows(px, py, pc):
            return out_ref.at[4 * px + 2 * py + pc]

        def copy(k, blk, to, src=None):
            return pltpu.make_async_remote_copy(
                src_ref=rows(*blk) if src is None else src, dst_ref=rows(*blk),
                send_sem=send_sems.at[k], recv_sem=recv_sems.at[k], device_id=to, device_id_type=MESH)

        mine = pltpu.make_async_copy(x_ref, rows(*me), local_sem)
        mine.start()
        first = [copy(0, me, sibling, src=x_ref)]
        first += [copy(1 + j, me, (*chip, c), src=x_ref) for j, chip in enumerate(chips)]
        for cp in first:
            cp.start()
        passed = [copy(4 + j, (*chip, c), sibling) for j, chip in enumerate(chips)]
        for j, chip in enumerate(chips):
            copy(1 + j, (*chip, c), me).wait_recv()
            passed[j].start()
        copy(0, sibling, me).wait_recv()
        for j, chip in enumerate(chips):
            copy(4 + j, (*chip, 1 - c), me).wait_recv()
        for cp in first + passed:
            cp.wait_send()
        mine.wait()

    return pl.pallas_call(
        body, name=name,
        out_shape=[jax.ShapeDtypeStruct((N_DEV,) + block.shape, block.dtype), jax.ShapeDtypeStruct((8, LANES), F32)],
        in_specs=[pl.BlockSpec(memory_space=pltpu.VMEM)]
        + [pl.BlockSpec(memory_space=pl.ANY)] * len(deps),
        out_specs=[pl.BlockSpec(memory_space=pl.ANY), pl.BlockSpec(memory_space=pltpu.VMEM)],
        scratch_shapes=[pltpu.SemaphoreType.DMA((7,)), pltpu.SemaphoreType.DMA((7,)), pltpu.SemaphoreType.DMA],
    )(block, *deps)


def _all_to_all(blocks, name):
    def body(x_ref, out_ref, send_sems, recv_sems, local_sem):
        x, y, c = lax.axis_index("x"), lax.axis_index("y"), lax.axis_index("c")
        my = 4 * x + 2 * y + c
        mine = pltpu.make_async_copy(x_ref.at[my], out_ref.at[my], local_sem)
        mine.start()
        peers = []
        for r in range(1, N_DEV):
            px = 1 - x if r & 4 else x
            py = 1 - y if r & 2 else y
            pc = 1 - c if r & 1 else c
            peers.append((r - 1, 4 * px + 2 * py + pc, (px, py, pc)))

        def copy(k, src_slot, dst_slot, to):
            return pltpu.make_async_remote_copy(
                src_ref=x_ref.at[src_slot], dst_ref=out_ref.at[dst_slot],
                send_sem=send_sems.at[k], recv_sem=recv_sems.at[k], device_id=to, device_id_type=MESH)

        sends = [copy(k, pid, my, to) for k, pid, to in peers]
        for cp in sends:
            cp.start()
        for k, pid, to in peers:
            copy(k, pid, pid, to).wait_recv()
        for cp in sends:
            cp.wait_send()
        mine.wait()

    return pl.pallas_call(
        body, name=name,
        out_shape=jax.ShapeDtypeStruct(blocks.shape, blocks.dtype),
        in_specs=[pl.BlockSpec(memory_space=pltpu.VMEM)],
        out_specs=pl.BlockSpec(memory_space=pl.ANY),
        scratch_shapes=[pltpu.SemaphoreType.DMA((7,)), pltpu.SemaphoreType.DMA((7,)), pltpu.SemaphoreType.DMA],
    )(blocks)


_HBM = pl.BlockSpec(memory_space=pltpu.HBM)
_SEM = pl.BlockSpec(memory_space=pltpu.SEMAPHORE)
_EFFECT = pltpu.SideEffectType.DATAFLOW_SIDE_EFFECTING


def _peers_all(x, y, c):
    out = []
    for r in range(1, N_DEV):
        px = 1 - x if r & 4 else x
        py = 1 - y if r & 2 else y
        pc = 1 - c if r & 1 else c
        out.append((r - 1, 4 * px + 2 * py + pc, (px, py, pc)))
    return out


def _peers_same_core(x, y, c):
    return [(k, 4 * px + 2 * py + c, (px, py, c))
            for k, (px, py) in enumerate([(1 - x, y), (x, 1 - y), (1 - x, 1 - y)])]


def _split_start(src, peers_fn, scatter, name, dep=None):
    blk = src.shape[1:] if scatter else src.shape
    land_shape = (N_DEV,) + tuple(blk)
    n = len(peers_fn(0, 0, 0))
    deps = [] if dep is None else [dep]

    def body(x_ref, land_ref, *rest):
        send_sems, recv_sems, x_thru, land_thru, token = rest[len(deps):]
        x, y, c = lax.axis_index("x"), lax.axis_index("y"), lax.axis_index("c")
        my = 4 * x + 2 * y + c
        for k, pid, to in peers_fn(x, y, c):
            pltpu.make_async_remote_copy(
                src_ref=x_ref.at[pid] if scatter else x_ref, dst_ref=land_ref.at[my],
                send_sem=send_sems.at[k], recv_sem=recv_sems.at[k], device_id=to, device_id_type=MESH).start()
        token[...] = jnp.zeros_like(token)

    return pl.pallas_call(
        body, name=name,
        out_shape=(pltpu.SemaphoreType.DMA((n,)), pltpu.SemaphoreType.DMA((n,)),
                   pltpu.HBM(src.shape, src.dtype), pltpu.HBM(land_shape, src.dtype),
                   jax.ShapeDtypeStruct((8, LANES), F32)),
        in_specs=(_HBM, _HBM) + (pl.BlockSpec(memory_space=pl.ANY),) * len(deps),
        out_specs=(_SEM, _SEM, _HBM, _HBM, pl.BlockSpec(memory_space=pltpu.VMEM)),
        input_output_aliases={0: 2, 1: 3},
        compiler_params=pltpu.CompilerParams(has_side_effects=_EFFECT),
    )(pltpu.with_memory_space_constraint(src, pltpu.HBM),
      pltpu.with_memory_space_constraint(lax.empty(land_shape, src.dtype), pltpu.HBM), *deps)


def _split_wait(handles, after, peers_fn, scatter, own, name):
    send_sems, recv_sems, src_thru, land_thru, _ = handles
    blk = land_thru.shape[1:]
    after = list(after) if isinstance(after, (list, tuple)) else [after]

    def body(x_ref, land_ref, send_sems, recv_sems, *rest):
        stage = rest[len(after) + 2:]
        x, y, c = lax.axis_index("x"), lax.axis_index("y"), lax.axis_index("c")
        if own:
            my = 4 * x + 2 * y + c
            mine = _staged_copy(x_ref.at[my] if scatter else x_ref, land_ref.at[my], *stage)
        for k, pid, to in peers_fn(x, y, c):
            cp = pltpu.make_async_remote_copy(
                src_ref=x_ref.at[pid] if scatter else x_ref, dst_ref=land_ref.at[pid],
                send_sem=send_sems.at[k], recv_sem=recv_sems.at[k], device_id=to, device_id_type=MESH)
            cp.wait_send()
            cp.wait_recv()
        if own:
            mine.wait()

    return pl.pallas_call(
        body, name=name,
        out_shape=(pltpu.HBM(src_thru.shape, src_thru.dtype), pltpu.HBM(land_thru.shape, land_thru.dtype)),
        in_specs=(_HBM, _HBM, _SEM, _SEM) + (pl.BlockSpec(memory_space=pl.ANY),) * len(after),
        out_specs=(_HBM, _HBM),
        input_output_aliases={0: 0, 1: 1},
        scratch_shapes=[pltpu.VMEM(blk, land_thru.dtype), pltpu.SemaphoreType.DMA((2,))] if own else [],
        compiler_params=pltpu.CompilerParams(has_side_effects=_EFFECT, vmem_limit_bytes=VMEM_LIMIT),
    )(src_thru, land_thru, send_sems, recv_sems, *after)


def _staged_copy(src_ref, dst_ref, buf, sems):
    leg = pltpu.make_async_copy(src_ref, buf, sems.at[0])
    leg.start()
    leg.wait()
    leg = pltpu.make_async_copy(buf, dst_ref, sems.at[1])
    leg.start()
    return leg


def _multi_start(srcs, name, dep=None):
    n_src = len(srcs)
    lands = [(N_DEV,) + tuple(a.shape[1:] if sc else a.shape) for a, sc in srcs]
    deps = [] if dep is None else [dep]

    def body(*refs):
        ins, outs = refs[:2 * n_src], refs[2 * n_src + len(deps):]
        x, y, c = lax.axis_index("x"), lax.axis_index("y"), lax.axis_index("c")
        my = 4 * x + 2 * y + c
        for i, (_, scatter) in enumerate(srcs):
            x_ref, land_ref, send_sems, recv_sems = ins[2 * i], ins[2 * i + 1], outs[4 * i], outs[4 * i + 1]
            for k, pid, to in _peers_all(x, y, c):
                pltpu.make_async_remote_copy(
                    src_ref=x_ref.at[pid] if scatter else x_ref, dst_ref=land_ref.at[my],
                    send_sem=send_sems.at[k], recv_sem=recv_sems.at[k], device_id=to, device_id_type=MESH).start()
        outs[-1][...] = jnp.zeros_like(outs[-1])

    out_shape, out_specs, operands = [], [], []
    for (a, _), land in zip(srcs, lands):
        out_shape += [pltpu.SemaphoreType.DMA((N_DEV - 1,)), pltpu.SemaphoreType.DMA((N_DEV - 1,)),
                      pltpu.HBM(a.shape, a.dtype), pltpu.HBM(land, a.dtype)]
        out_specs += [_SEM, _SEM, _HBM, _HBM]
        operands += [pltpu.with_memory_space_constraint(a, pltpu.HBM),
                     pltpu.with_memory_space_constraint(lax.empty(land, a.dtype), pltpu.HBM)]
    res = pl.pallas_call(
        body, name=name,
        out_shape=tuple(out_shape) + (jax.ShapeDtypeStruct((8, LANES), F32),),
        in_specs=(_HBM,) * (2 * n_src) + (pl.BlockSpec(memory_space=pl.ANY),) * len(deps),
        out_specs=tuple(out_specs) + (pl.BlockSpec(memory_space=pltpu.VMEM),),
        input_output_aliases={2 * i + j: 4 * i + 2 + j for i in range(n_src) for j in range(2)},
        compiler_params=pltpu.CompilerParams(has_side_effects=_EFFECT),
    )(*operands, *deps)
    return [tuple(res[4 * i:4 * i + 4]) + (srcs[i][1],) for i in range(n_src)], res[-1]


def _multi_wait(started, after, name):
    n_src = len(started)
    after = list(after) if isinstance(after, (list, tuple)) else [after]

    def body(*refs):
        ins = refs[:4 * n_src]
        stage = refs[4 * n_src + len(after) + 2 * n_src:]
        x, y, c = lax.axis_index("x"), lax.axis_index("y"), lax.axis_index("c")
        my = 4 * x + 2 * y + c
        pending = []
        for i, h in enumerate(started):
            x_ref, land_ref, send_sems, recv_sems = ins[4 * i:4 * i + 4]
            scatter = h[4]
            pending.append(_staged_copy(x_ref.at[my] if scatter else x_ref, land_ref.at[my],
                                        stage[2 * i], stage[2 * i + 1]))
            for k, pid, to in _peers_all(x, y, c):
                cp = pltpu.make_async_remote_copy(
                    src_ref=x_ref.at[pid] if scatter else x_ref, dst_ref=land_ref.at[pid],
                    send_sem=send_sems.at[k], recv_sem=recv_sems.at[k], device_id=to, device_id_type=MESH)
                cp.wait_send()
                cp.wait_recv()
        for leg in pending:
            leg.wait()

    operands, out_shape, scratch = [], [], []
    for send, recv, src_thru, land_thru, _ in started:
        operands += [src_thru, land_thru, send, recv]
        out_shape += [pltpu.HBM(src_thru.shape, src_thru.dtype), pltpu.HBM(land_thru.shape, land_thru.dtype)]
        scratch += [pltpu.VMEM(land_thru.shape[1:], land_thru.dtype), pltpu.SemaphoreType.DMA((2,))]
    res = pl.pallas_call(
        body, name=name,
        out_shape=tuple(out_shape),
        in_specs=(_HBM, _HBM, _SEM, _SEM) * n_src + (pl.BlockSpec(memory_space=pl.ANY),) * len(after),
        out_specs=(_HBM,) * (2 * n_src),
        input_output_aliases={4 * i + j: 2 * i + j for i in range(n_src) for j in range(2)},
        scratch_shapes=scratch,
        compiler_params=pltpu.CompilerParams(has_side_effects=_EFFECT, vmem_limit_bytes=VMEM_LIMIT),
    )(*operands, *after)
    return [res[2 * i + 1] for i in range(n_src)]


def _gather_finish(block, land, name):
    def body(x_ref, land_ref, out_ref, token, send_sems, recv_sems, buf, local_sems):
        x, y, c = lax.axis_index("x"), lax.axis_index("y"), lax.axis_index("c")
        my, sib_id, sibling = 4 * x + 2 * y + c, 4 * x + 2 * y + 1 - c, (x, y, 1 - c)
        token[...] = jnp.zeros_like(token)

        def copy(k, slot, src=None):
            return pltpu.make_async_remote_copy(
                src_ref=land_ref.at[slot] if src is None else src, dst_ref=out_ref.at[slot],
                send_sem=send_sems.at[k], recv_sem=recv_sems.at[k], device_id=sibling, device_id_type=MESH)

        chips = _peers_same_core(x, y, c)
        sends = [copy(0, my, src=x_ref)] + [copy(1 + k, pid) for k, pid, _ in chips]
        for cp in sends:
            cp.start()
        mine = _staged_copy(x_ref, out_ref.at[my], buf, local_sems)
        copy(0, sib_id).wait_recv()
        for k, pid, _ in chips:
            copy(1 + k, pid + 1 - 2 * c).wait_recv()
        for cp in sends:
            cp.wait_send()
        mine.wait()

    return pl.pallas_call(
        body, name=name,
        out_shape=[jax.ShapeDtypeStruct(land.shape, land.dtype), jax.ShapeDtypeStruct((8, LANES), F32)],
        in_specs=[pl.BlockSpec(memory_space=pl.ANY), pl.BlockSpec(memory_space=pl.ANY)],
        out_specs=[pl.BlockSpec(memory_space=pl.ANY), pl.BlockSpec(memory_space=pltpu.VMEM)],
        input_output_aliases={1: 0},
        scratch_shapes=[pltpu.SemaphoreType.DMA((4,)), pltpu.SemaphoreType.DMA((4,)),
                        pltpu.VMEM(block.shape, block.dtype), pltpu.SemaphoreType.DMA((2,))],
        compiler_params=pltpu.CompilerParams(vmem_limit_bytes=VMEM_LIMIT),
    )(block, land)


def _forward_start(block, land, name):
    def body(x_ref, land_ref, send_sems, recv_sems, x_thru, land_thru, token):
        x, y, c = lax.axis_index("x"), lax.axis_index("y"), lax.axis_index("c")
        my, sibling = 4 * x + 2 * y + c, (x, y, 1 - c)
        slots = [(0, my, x_ref)] + [(1 + k, pid, land_ref.at[pid]) for k, pid, _ in _peers_same_core(x, y, c)]
        for k, slot, src in slots:
            pltpu.make_async_remote_copy(
                src_ref=src, dst_ref=land_ref.at[slot], send_sem=send_sems.at[k], recv_sem=recv_sems.at[k],
                device_id=sibling, device_id_type=MESH).start()
        token[...] = jnp.zeros_like(token)

    return pl.pallas_call(
        body, name=name,
        out_shape=(pltpu.SemaphoreType.DMA((4,)), pltpu.SemaphoreType.DMA((4,)),
                   pltpu.HBM(block.shape, block.dtype), pltpu.HBM(land.shape, land.dtype),
                   jax.ShapeDtypeStruct((8, LANES), F32)),
        in_specs=(_HBM, _HBM),
        out_specs=(_SEM, _SEM, _HBM, _HBM, pl.BlockSpec(memory_space=pltpu.VMEM)),
        input_output_aliases={0: 2, 1: 3},
        compiler_params=pltpu.CompilerParams(has_side_effects=_EFFECT),
    )(block, land)


def _forward_wait(handles, after, name):
    send_sems, recv_sems, block_thru, land_thru, _ = handles

    def body(x_ref, land_ref, send_sems, recv_sems, after_ref, x_dead, got_ref, buf, local_sems):
        x, y, c = lax.axis_index("x"), lax.axis_index("y"), lax.axis_index("c")
        my, sib_id, sibling = 4 * x + 2 * y + c, 4 * x + 2 * y + 1 - c, (x, y, 1 - c)
        mine = _staged_copy(x_ref, land_ref.at[my], buf, local_sems)
        slots = [(0, my, sib_id)] + [(1 + k, pid, pid + 1 - 2 * c) for k, pid, _ in _peers_same_core(x, y, c)]
        for k, sent, got in slots:
            cp = pltpu.make_async_remote_copy(
                src_ref=land_ref.at[sent], dst_ref=land_ref.at[got], send_sem=send_sems.at[k],
                recv_sem=recv_sems.at[k], device_id=sibling, device_id_type=MESH)
            cp.wait_send()
            cp.wait_recv()
        mine.wait()

    return pl.pallas_call(
        body, name=name,
        out_shape=(pltpu.HBM(block_thru.shape, block_thru.dtype), pltpu.HBM(land_thru.shape, land_thru.dtype)),
        in_specs=(_HBM, _HBM, _SEM, _SEM, pl.BlockSpec(memory_space=pl.ANY)),
        out_specs=(_HBM, _HBM),
        input_output_aliases={0: 0, 1: 1},
        scratch_shapes=[pltpu.VMEM(block_thru.shape, block_thru.dtype), pltpu.SemaphoreType.DMA((2,))],
        compiler_params=pltpu.CompilerParams(has_side_effects=_EFFECT, vmem_limit_bytes=VMEM_LIMIT),
    )(block_thru, land_thru, send_sems, recv_sems, after)[1]


def _after(v, token):
    return v + token[0, 0].astype(v.dtype)


def _dsilu(x, s):
    return s * (1.0 + x * (1.0 - s))


def _log1p(x):
    u = 1.0 + x
    d = u - 1.0
    return jnp.where(d == 0.0, x, jnp.log(u) * (x / jnp.where(d == 0.0, 1.0, d)))


def _softplus_neg(lam):
    return jnp.maximum(-lam, 0.0) + _log1p(jnp.exp(-jnp.abs(lam)))


def _neg_expm1(y, exp_y):
    poly = -y * (1.0 + y * (0.5 + y * (1.0 / 6.0 + y * (1.0 / 24.0))))
    return jnp.where(y > -0.05, poly, 1.0 - exp_y)


def _sigmoid(x):
    return 0.5 * jnp.tanh(0.5 * x) + 0.5


def _shift_dn(cur, prev, k):
    ext = jnp.concatenate([prev, cur], axis=0)
    return pltpu.roll(ext, k, 0)[HALO:, :]


def _shift_up(cur, nxt, k):
    n = cur.shape[0]
    ext = jnp.concatenate([cur, nxt], axis=0)
    return pltpu.roll(ext, n + HALO - k, 0)[:n, :]


def _scan_fwd(a, b, h_prev):
    groups = a.shape[0] // SUBLANES
    a3 = a.reshape(groups, SUBLANES, LANES)
    b3 = b.reshape(groups, SUBLANES, LANES)
    row = lax.broadcasted_iota(jnp.int32, a3.shape, 1)
    k = 1
    while k < SUBLANES:
        a_sh = jnp.where(row >= k, pltpu.roll(a3, k, 1), 1.0)
        b_sh = jnp.where(row >= k, pltpu.roll(b3, k, 1), 0.0)
        b3 = a3 * b_sh + b3
        a3 = a3 * a_sh
        k *= 2
    carry = h_prev[HALO - 1:HALO, :]
    out = []
    for i in range(groups):
        hg = b3[i] + a3[i] * carry
        out.append(hg)
        carry = hg[SUBLANES - 1:SUBLANES, :]
    return jnp.concatenate(out, axis=0)


def _scan_rev(a_next, g, lam_next):
    groups = g.shape[0] // SUBLANES
    a3 = a_next.reshape(groups, SUBLANES, LANES)
    g3 = g.reshape(groups, SUBLANES, LANES)
    row = lax.broadcasted_iota(jnp.int32, a3.shape, 1)
    k = 1
    while k < SUBLANES:
        ok = row < SUBLANES - k
        a_sh = jnp.where(ok, pltpu.roll(a3, SUBLANES - k, 1), 1.0)
        g_sh = jnp.where(ok, pltpu.roll(g3, SUBLANES - k, 1), 0.0)
        g3 = g3 + a3 * g_sh
        a3 = a3 * a_sh
        k *= 2
    carry = lam_next[0:1, :]
    out = [None] * groups
    for i in reversed(range(groups)):
        lg = g3[i] + a3[i] * carry
        out[i] = lg
        carry = lg[0:1, :]
    return jnp.concatenate(out, axis=0)


def _rowsum(v):
    return jnp.sum(v, axis=0, keepdims=True)


def _dot(a, b):
    return jnp.dot(a, b, preferred_element_type=F32)


def _dot_nt(a, b):
    return lax.dot_general(a, b, (((1,), (1,)), ((), ())), preferred_element_type=F32)


def _dot_tn(a, b):
    return lax.dot_general(a, b, (((0,), (0,)), ((), ())), preferred_element_type=F32)


class _MixerWeights:
    def __init__(self, caw_ref, sw_ref, sb_ref, lcw_ref, lcb_ref, wa_ref, wx_ref, ba_ref, bx_ref, lam_ref):
        self.caw = [caw_ref[j:j + 1, :] for j in range(3)]
        self.lcw = [lcw_ref[j:j + 1, :] for j in range(4)]
        self.lcb = lcb_ref[...]
        row = lax.broadcasted_iota(jnp.int32, (CHUNK, CHUNK), 0)
        col = lax.broadcasted_iota(jnp.int32, (CHUNK, CHUNK), 1)
        self.tril = col <= row
        self.sw = jnp.where(self.tril, sw_ref[...], 0.0).astype(BF16)
        self.sb = sb_ref[...]
        self.wa = wa_ref[...]
        self.wx = wx_ref[...]
        self.ba = ba_ref[...]
        self.bx = bx_ref[...]
        lam = lam_ref[...]
        self.neg_c_sp = -LRU_C * _softplus_neg(lam)
        self.dsp_dlam = -_sigmoid(-lam)


def _mixer_a(ld, ldp, w, cv=None):
    t = {}
    a_x, a_c = ld(AX), ld(AC)
    t["a_x"], t["a_c"], t["a_b"], t["a_z"] = a_x, a_c, ld(AB), ld(AZ)
    t["ca"] = ca = a_c * a_x
    if cv is None:
        ca_p = ldp(AC) * ldp(AX)
        cv = w.caw[2] * ca + w.caw[1] * _shift_dn(ca, ca_p, 1) + w.caw[0] * _shift_dn(ca, ca_p, 2)
    t["cv"] = cv
    t["sa"] = _sigmoid(t["a_z"])
    t["silu_az"] = t["a_z"] * t["sa"]
    t["y_a"] = t["silu_az"] * t["a_b"] * t["cv"]
    return t


def _mixer_b(ld, w):
    t = {}
    v = ld(SV)
    vc = v - jnp.mean(v, axis=1, keepdims=True)
    t["rstd"] = lax.rsqrt(jnp.mean(vc * vc, axis=1, keepdims=True) + EPS)
    t["vn"] = vc * t["rstd"]
    t["z"] = _dot(w.sw, t["vn"].astype(BF16)) + w.sb
    t["s_u"], t["s_z"] = ld(SU), ld(SZ)
    t["ss"] = _sigmoid(t["s_z"])
    t["silu_sz"] = t["s_z"] * t["ss"]
    t["y_s"] = t["silu_sz"] * t["s_u"] * t["z"]
    return t


def _mixer_c(ld, ldp, w, backward, xc=None):
    t = {}
    t["r_x"] = r_x = ld(RX)
    if xc is None:
        r_xp = ldp(RX)
        xc = (w.lcb + w.lcw[0] * _shift_dn(r_x, r_xp, 3) + w.lcw[1] * _shift_dn(r_x, r_xp, 2)
              + w.lcw[2] * _shift_dn(r_x, r_xp, 1) + w.lcw[3] * r_x)
    t["xc"] = xc
    xcb = xc.astype(BF16)
    t["r"] = _sigmoid(_dot(xcb, w.wa) + w.ba)
    t["i"] = _sigmoid(_dot(xcb, w.wx) + w.bx)
    la = t["r"] * w.neg_c_sp
    t["a"] = jnp.exp(la)
    t["em"] = _neg_expm1(2.0 * la, t["a"] * t["a"])
    if backward:
        t["inv_mult"] = lax.rsqrt(t["em"])
        t["mult"] = t["em"] * t["inv_mult"]
    else:
        t["mult"] = jnp.sqrt(t["em"])
    t["b"] = t["mult"] * (t["i"] * xc)
    t["r_z"] = ld(RZ)
    t["sr"] = _sigmoid(t["r_z"])
    t["silu_rz"] = t["r_z"] * t["sr"]
    return t


def _mixer_pre_scan(ld, ldp, w, backward, cv=None, xc=None):
    t = {**_mixer_a(ld, ldp, w, cv), **_mixer_b(ld, w), **_mixer_c(ld, ldp, w, backward, xc)}
    t["ga"], t["gs"], t["gr"] = _sigmoid(ld(GA)), _sigmoid(ld(GS)), _sigmoid(ld(GR))
    return t


def _weight_specs(n_cb_axis):
    def at(fn):
        return lambda *g: fn(g[n_cb_axis])
    return [
        pl.BlockSpec((3, LANES), at(lambda cb: (0, cb))),
        pl.BlockSpec((None, CHUNK, CHUNK), at(lambda cb: (cb, 0, 0))),
        pl.BlockSpec((None, CHUNK, LANES), at(lambda cb: (cb, 0, 0))),
        pl.BlockSpec((4, LANES), at(lambda cb: (0, cb))),
        pl.BlockSpec((1, LANES), at(lambda cb: (0, cb))),
        pl.BlockSpec((None, LANES, LANES), at(lambda cb: (cb, 0, 0))),
        pl.BlockSpec((None, LANES, LANES), at(lambda cb: (cb, 0, 0))),
        pl.BlockSpec((1, LANES), at(lambda cb: (0, cb))),
        pl.BlockSpec((1, LANES), at(lambda cb: (0, cb))),
        pl.BlockSpec((1, LANES), at(lambda cb: (0, cb))),
    ]


def _chunk_loaders(p_ref, c):
    r0 = pl.multiple_of(c * CHUNK, CHUNK)
    rp = pl.multiple_of(jnp.maximum(c * CHUNK - HALO, 0), HALO)

    def ld(j):
        return p_ref[j, pl.ds(r0, CHUNK), :].astype(F32)

    def ldp(j):
        return jnp.where(c > 0, p_ref[j, pl.ds(rp, HALO), :].astype(F32), 0.0)

    return r0, rp, ld, ldp


def _mixer_fwd(proj, mw):
    _, nb, s, _ = proj.shape
    n_chunks = s // CHUNK

    def body(p_ref, *refs):
        w = _MixerWeights(*refs[:10])
        merged_ref, hs_ref, cv_ref, xc_ref = refs[10:]

        def chunk(c, h_prev):
            r0, _, ld, ldp = _chunk_loaders(p_ref, c)
            t = _mixer_pre_scan(ld, ldp, w, False)
            h = _scan_fwd(t["a"], t["b"], h_prev)
            y_r = t["silu_rz"] * h
            merged = t["ga"] * t["y_a"] + t["gs"] * t["y_s"] + t["gr"] * y_r
            merged_ref[pl.ds(r0, CHUNK), :] = merged.astype(BF16)
            hs_ref[pl.ds(r0, CHUNK), :] = h
            cv_ref[pl.ds(r0, CHUNK), :] = t["cv"].astype(BF16)
            xc_ref[pl.ds(r0, CHUNK), :] = t["xc"].astype(BF16)
            return h[CHUNK - HALO:, :]

        def group(i, carry):
            for k in range(FWD_CHUNKS_PER_TRIP):
                carry = chunk(FWD_CHUNKS_PER_TRIP * i + k, carry)
            return carry

        assert n_chunks % FWD_CHUNKS_PER_TRIP == 0
        lax.fori_loop(0, n_chunks // FWD_CHUNKS_PER_TRIP, group, jnp.zeros((HALO, LANES), F32))

    slab = pl.BlockSpec((None, s, LANES), lambda cb, b: (b, 0, cb))
    half = jax.ShapeDtypeStruct((nb, s, D), BF16)
    return pl.pallas_call(
        body, name="mixer_fwd", grid=(D // LANES, nb),
        in_specs=[pl.BlockSpec((N_SEG, None, s, LANES), lambda cb, b: (0, b, 0, cb))] + _weight_specs(0),
        out_specs=[slab, slab, slab, slab],
        out_shape=[half, jax.ShapeDtypeStruct((nb, s, D), F32), half, half],
        compiler_params=_params(("arbitrary", "arbitrary")),
    )(proj, *mw)


def _mixer_bwd(proj, dmerged, hs, cv, xc, mw):
    _, nb, s, _ = proj.shape
    n_chunks = s // CHUNK

    def body(p_ref, dm_ref, hs_ref, cv_ref, xc_ref, *refs):
        w = _MixerWeights(*refs[:10])
        dp_ref, g_caw, g_sw, g_sb, g_lcw, g_vec, g_wa, g_wx = refs[10:]

        @pl.when(pl.program_id(1) == 0)
        def _():
            for ref in (g_caw, g_sw, g_sb, g_lcw, g_vec, g_wa, g_wx):
                ref[...] = jnp.zeros_like(ref)

        def chunk(i, carry):
            dcv_n, dxc_n, lam_n, a_n = carry
            c = n_chunks - 1 - i
            r0, rp, ld, ldp = _chunk_loaders(p_ref, c)
            t = _mixer_pre_scan(ld, ldp, w, True, cv_ref[pl.ds(r0, CHUNK), :].astype(F32),
                                xc_ref[pl.ds(r0, CHUNK), :].astype(F32))
            h = hs_ref[pl.ds(r0, CHUNK), :]
            h_p = jnp.where(c > 0, hs_ref[pl.ds(rp, HALO), :], 0.0)
            h_prev = _shift_dn(h, h_p, 1)
            dm = dm_ref[pl.ds(r0, CHUNK), :].astype(F32)
            y_r = t["silu_rz"] * h

            def out(j, val):
                dp_ref[j, pl.ds(r0, CHUNK), :] = val.astype(BF16)

            ga, gs, gr = t["ga"], t["gs"], t["gr"]
            out(GA, dm * t["y_a"] * ga * (1.0 - ga))
            out(GS, dm * t["y_s"] * gs * (1.0 - gs))
            out(GR, dm * y_r * gr * (1.0 - gr))

            dy_a = dm * ga
            out(AZ, dy_a * t["a_b"] * t["cv"] * _dsilu(t["a_z"], t["sa"]))
            out(AB, dy_a * t["silu_az"] * t["cv"])
            dcv = dy_a * t["silu_az"] * t["a_b"]
            dcv1, dcv2 = _shift_up(dcv, dcv_n, 1), _shift_up(dcv, dcv_n, 2)
            dca = w.caw[2] * dcv + w.caw[1] * dcv1 + w.caw[0] * dcv2
            out(AC, dca * t["a_x"])
            out(AX, dca * t["a_c"])
            g_caw[2:3, :] += _rowsum(dcv * t["ca"])
            g_caw[1:2, :] += _rowsum(dcv1 * t["ca"])
            g_caw[0:1, :] += _rowsum(dcv2 * t["ca"])

            dy_s = dm * gs
            out(SZ, dy_s * t["s_u"] * t["z"] * _dsilu(t["s_z"], t["ss"]))
            out(SU, dy_s * t["silu_sz"] * t["z"])
            dz = dy_s * t["silu_sz"] * t["s_u"]
            dzb = dz.astype(BF16)
            g_sb[...] += jnp.broadcast_to(jnp.sum(dz, axis=1, keepdims=True), (CHUNK, LANES))
            g_sw[...] += _dot_nt(dzb, t["vn"].astype(BF16))
            dvn = _dot_tn(w.sw, dzb)
            vn = t["vn"]
            out(SV, t["rstd"] * (dvn - jnp.mean(dvn, axis=1, keepdims=True)
                                 - vn * jnp.mean(dvn * vn, axis=1, keepdims=True)))

            dy_r = dm * gr
            out(RZ, dy_r * h * _dsilu(t["r_z"], t["sr"]))
            lam = _scan_rev(_shift_up(t["a"], a_n, 1), dy_r * t["silu_rz"], lam_n)
            a, r, ig, xc, mult = t["a"], t["r"], t["i"], t["xc"], t["mult"]
            d_i = lam * mult * xc
            d_mult = lam * ig * xc
            dxc = lam * mult * ig
            dla = lam * h_prev * a - d_mult * ((1.0 - t["em"]) * t["inv_mult"])
            g_vec[3:4, :] += _rowsum(dla * r) * (-LRU_C * w.dsp_dlam)
            dpr = (dla * w.neg_c_sp) * r * (1.0 - r)
            dpi = d_i * ig * (1.0 - ig)
            dprb, dpib, xcb = dpr.astype(BF16), dpi.astype(BF16), xc.astype(BF16)
            g_wa[...] += _dot_tn(xcb, dprb)
            g_wx[...] += _dot_tn(xcb, dpib)
            g_vec[1:2, :] += _rowsum(dpr)
            g_vec[2:3, :] += _rowsum(dpi)
            dxc = dxc + _dot_nt(dprb, w.wa) + _dot_nt(dpib, w.wx)
            g_vec[0:1, :] += _rowsum(dxc)
            dxcs = [_shift_up(dxc, dxc_n, 3), _shift_up(dxc, dxc_n, 2), _shift_up(dxc, dxc_n, 1), dxc]
            out(RX, w.lcw[3] * dxcs[3] + w.lcw[2] * dxcs[2] + w.lcw[1] * dxcs[1] + w.lcw[0] * dxcs[0])
            for j in range(4):
                g_lcw[j:j + 1, :] += _rowsum(dxcs[j] * t["r_x"])
            return dcv[:HALO, :], dxc[:HALO, :], lam[:HALO, :], a[:HALO, :]

        zero = jnp.zeros((HALO, LANES), F32)
        def group(i, carry):
            for k in range(BWD_CHUNKS_PER_TRIP):
                carry = chunk(BWD_CHUNKS_PER_TRIP * i + k, carry)
            return carry

        assert n_chunks % BWD_CHUNKS_PER_TRIP == 0
        lax.fori_loop(0, n_chunks // BWD_CHUNKS_PER_TRIP, group, (zero, zero, zero, zero))

        @pl.when(pl.program_id(1) == nb - 1)
        def _():
            g_sw[...] = jnp.where(w.tril, g_sw[...], 0.0)

    slab = lambda dt: pl.BlockSpec((None, s, LANES), lambda cb, b: (b, 0, cb))
    seg = pl.BlockSpec((N_SEG, None, s, LANES), lambda cb, b: (0, b, 0, cb))
    rows = lambda n: pl.BlockSpec((n, LANES), lambda cb, b: (0, cb))
    sq = pl.BlockSpec((None, LANES, LANES), lambda cb, b: (cb, 0, 0))
    n_cb = D // LANES
    return pl.pallas_call(
        body, name="mixer_bwd", grid=(n_cb, nb),
        in_specs=[seg, slab(BF16), slab(F32), slab(BF16), slab(BF16)] + _weight_specs(0),
        out_specs=[seg, rows(3), sq, sq, rows(4), rows(8), sq, sq],
        out_shape=[
            jax.ShapeDtypeStruct(proj.shape, BF16),
            jax.ShapeDtypeStruct((3, D), F32),
            jax.ShapeDtypeStruct((n_cb, CHUNK, CHUNK), F32),
            jax.ShapeDtypeStruct((n_cb, CHUNK, LANES), F32),
            jax.ShapeDtypeStruct((4, D), F32),
            jax.ShapeDtypeStruct((8, D), F32),
            jax.ShapeDtypeStruct((n_cb, LANES, LANES), F32),
            jax.ShapeDtypeStruct((n_cb, LANES, LANES), F32),
        ],
        compiler_params=_params(("arbitrary", "arbitrary")),
    )(proj, dmerged, hs, cv, xc, *mw)


def _row_tile(s, want):
    return want if s % want == 0 else s


def _norm_mod(x, gain, shift, scale):
    nb, s, _ = x.shape
    tm = _row_tile(s, 512)

    def body(x_ref, g_ref, sh_ref, sc_ref, h_ref):
        xv = x_ref[...]
        r = lax.rsqrt(jnp.mean(xv * xv, axis=1, keepdims=True) + EPS)
        h_ref[...] = ((xv * r) * g_ref[...] * (1.0 + sc_ref[...]) + sh_ref[...]).astype(BF16)

    tile = pl.BlockSpec((None, tm, D), lambda b, m: (b, m, 0))
    vec = pl.BlockSpec((None, 1, D), lambda b, m: (b, 0, 0))
    return pl.pallas_call(
        body, name="norm_mod", grid=(nb, s // tm),
        in_specs=[tile, pl.BlockSpec((1, D), lambda b, m: (0, 0)), vec, vec],
        out_specs=tile, out_shape=jax.ShapeDtypeStruct(x.shape, BF16),
        compiler_params=_params(("arbitrary", "arbitrary")),
    )(x, gain, shift, scale)


def _in_proj(h, wg, dep):
    nb, s, _ = h.shape

    def body(h_ref, w_ref, dep_ref, o_ref):
        o_ref[...] = _dot(h_ref[...], w_ref[...]).astype(BF16)

    return pl.pallas_call(
        body, name="in_proj", grid=(nb, N_DEV * UNITS_PER_DEV),
        in_specs=[pl.BlockSpec((None, s, D), lambda b, u: (b, 0, 0)),
                  pl.BlockSpec((None, D, UNIT), lambda b, u: (u // UNITS_PER_DEV, 0, u % UNITS_PER_DEV)),
                  pl.BlockSpec((8, LANES), lambda b, u: (0, 0))],
        out_specs=pl.BlockSpec((None, None, s, UNIT), lambda b, u: (u // 2, b, 0, u % 2)),
        out_shape=jax.ShapeDtypeStruct((N_SEG, nb, s, D), BF16),
        compiler_params=_params(("arbitrary", "arbitrary")),
    )(h, wg, dep)


def _out_proj(x, merged, wout, gate):
    nb, s, _ = x.shape
    tm = _row_tile(s, 512)

    def body(x_ref, m_ref, w_ref, g_ref, o_ref):
        o_ref[...] = x_ref[...] + g_ref[...] * _dot(m_ref[...], w_ref[...])

    tile = pl.BlockSpec((None, tm, D), lambda b, m: (b, m, 0))
    return pl.pallas_call(
        body, name="out_proj", grid=(nb, s // tm),
        in_specs=[tile, tile, pl.BlockSpec((D, D), lambda b, m: (0, 0)),
                  pl.BlockSpec((None, 1, D), lambda b, m: (b, 0, 0))],
        out_specs=tile, out_shape=jax.ShapeDtypeStruct(x.shape, F32),
        compiler_params=_params(("arbitrary", "arbitrary")),
    )(x, merged, wout, gate)


def _out_proj_norm(x, merged, wout, gate, gain, shift, scale):
    nb, s, _ = x.shape
    tm = _row_tile(s, 512)

    def body(x_ref, m_ref, w_ref, g_ref, gn_ref, sh_ref, sc_ref, o_ref, h_ref):
        xv = x_ref[...] + g_ref[...] * _dot(m_ref[...], w_ref[...])
        o_ref[...] = xv
        r = lax.rsqrt(jnp.mean(xv * xv, axis=1, keepdims=True) + EPS)
        h_ref[...] = ((xv * r) * gn_ref[...] * (1.0 + sc_ref[...]) + sh_ref[...]).astype(BF16)

    tile = pl.BlockSpec((None, tm, D), lambda b, m: (b, m, 0))
    vec = pl.BlockSpec((None, 1, D), lambda b, m: (b, 0, 0))
    return pl.pallas_call(
        body, name="out_proj_norm", grid=(nb, s // tm),
        in_specs=[tile, tile, pl.BlockSpec((D, D), lambda b, m: (0, 0)), vec,
                  pl.BlockSpec((1, D), lambda b, m: (0, 0)), vec, vec],
        out_specs=[tile, tile],
        out_shape=[jax.ShapeDtypeStruct(x.shape, F32), jax.ShapeDtypeStruct(x.shape, BF16)],
        compiler_params=_params(("arbitrary", "arbitrary")),
    )(x, merged, wout, gate, gain, shift, scale)


def _loss_head(x, gain, target):
    nb, s, _ = x.shape
    tm = _row_tile(s, 512)

    def body(x_ref, g_ref, t_ref, loss_ref, dx_ref, dg_ref):
        first = (pl.program_id(0) == 0) & (pl.program_id(1) == 0)
        last = (pl.program_id(0) == nb - 1) & (pl.program_id(1) == s // tm - 1)

        @pl.when(first)
        def _():
            loss_ref[...] = jnp.zeros_like(loss_ref)
            dg_ref[...] = jnp.zeros_like(dg_ref)

        xv = x_ref[...]
        r = lax.rsqrt(jnp.mean(xv * xv, axis=1, keepdims=True) + EPS)
        xn = xv * r
        g = g_ref[...]
        e = xn * g - t_ref[...]
        loss_ref[...] += _rowsum(e * e) * (0.5 / D)
        dy = e * (1.0 / D)
        dg_ref[...] += _rowsum(dy * xn)
        dxn = dy * g
        dx_ref[...] = r * (dxn - xn * jnp.mean(dxn * xn, axis=1, keepdims=True))

        @pl.when(last)
        def _():
            loss_ref[...] = jnp.broadcast_to(jnp.sum(loss_ref[...], axis=1, keepdims=True), (1, D))

    tile = pl.BlockSpec((None, tm, D), lambda b, m: (b, m, 0))
    vec = pl.BlockSpec((1, D), lambda b, m: (0, 0))
    return pl.pallas_call(
        body, name="loss_head", grid=(nb, s // tm),
        in_specs=[tile, vec, tile], out_specs=[vec, tile, vec],
        out_shape=[jax.ShapeDtypeStruct((1, D), F32), jax.ShapeDtypeStruct(x.shape, F32),
                   jax.ShapeDtypeStruct((1, D), F32)],
        compiler_params=_params(("arbitrary", "arbitrary")),
    )(x, gain, target)


def _out_proj_bwd(dxo, merged, wout, gate):
    nb, s, _ = dxo.shape
    tm = _row_tile(s, 512)

    def body(d_ref, m_ref, w_ref, g_ref, dm_ref, gw_ref, dg_ref):
        @pl.when((pl.program_id(0) == 0) & (pl.program_id(1) == 0))
        def _():
            gw_ref[...] = jnp.zeros_like(gw_ref)

        @pl.when(pl.program_id(1) == 0)
        def _():
            dg_ref[...] = jnp.zeros_like(dg_ref)

        d = d_ref[...]
        m = m_ref[...]
        wv = w_ref[...]
        dg_ref[...] += _rowsum(d * _dot(m, wv))
        dout = (d * g_ref[...]).astype(BF16)
        dm_ref[...] = _dot_nt(dout, wv).astype(BF16)
        gw_ref[...] += _dot_tn(m, dout)

    tile = pl.BlockSpec((None, tm, D), lambda b, m: (b, m, 0))
    vec = pl.BlockSpec((None, 1, D), lambda b, m: (b, 0, 0))
    full = pl.BlockSpec((D, D), lambda b, m: (0, 0))
    return pl.pallas_call(
        body, name="out_proj_bwd", grid=(nb, s // tm),
        in_specs=[tile, tile, full, vec], out_specs=[tile, full, vec],
        out_shape=[jax.ShapeDtypeStruct(dxo.shape, BF16), jax.ShapeDtypeStruct((D, D), F32),
                   jax.ShapeDtypeStruct((nb, 1, D), F32)],
        compiler_params=_params(("arbitrary", "arbitrary")),
    )(dxo, merged, wout, gate)


def _in_proj_bwd_h(dproj, wg, dep):
    _, nb, s, _ = dproj.shape
    tm = _row_tile(s, 1024)

    def body(dp0_ref, dp1_ref, w0_ref, w1_ref, w2_ref, w3_ref, dep_ref, dh_ref):
        j = pl.program_id(2)
        part = (_dot_nt(dp0_ref[...], jnp.concatenate([w0_ref[...], w1_ref[...]], axis=1))
                + _dot_nt(dp1_ref[...], jnp.concatenate([w2_ref[...], w3_ref[...]], axis=1)))

        @pl.when(j == 0)
        def _():
            dh_ref[...] = part

        @pl.when(j > 0)
        def _():
            dh_ref[...] += part

    def seg(k):
        return pl.BlockSpec((None, None, tm, D), lambda b, m, j: (2 * j + k, b, m, 0))

    def unit(k):
        return pl.BlockSpec((None, D, UNIT),
                            lambda b, m, j: ((4 * j + k) // UNITS_PER_DEV, 0, (4 * j + k) % UNITS_PER_DEV))

    return pl.pallas_call(
        body, name="in_proj_bwd_h", grid=(nb, s // tm, N_SEG // 2),
        in_specs=[seg(0), seg(1), unit(0), unit(1), unit(2), unit(3),
                  pl.BlockSpec((8, LANES), lambda b, m, j: (0, 0))],
        out_specs=pl.BlockSpec((None, tm, D), lambda b, m, j: (b, m, 0)),
        out_shape=jax.ShapeDtypeStruct((nb, s, D), F32),
        compiler_params=_params(("arbitrary", "arbitrary", "arbitrary")),
    )(dproj, dproj, wg, wg, wg, wg, dep)


def _norm_mod_bwd(dh, x, dxo, gain, scale):
    nb, s, _ = x.shape
    tm = _row_tile(s, 512)

    def body(dh_ref, x_ref, dxo_ref, g_ref, sc_ref, dx_ref, dsh_ref, dsc_ref, dg_ref):
        b, m = pl.program_id(0), pl.program_id(1)

        @pl.when((b == 0) & (m == 0))
        def _():
            dg_ref[...] = jnp.zeros_like(dg_ref)

        @pl.when(m == 0)
        def _():
            dsh_ref[...] = jnp.zeros_like(dsh_ref)
            dsc_ref[...] = jnp.zeros_like(dsc_ref)

        dh = dh_ref[...]
        xv = x_ref[...]
        r = lax.rsqrt(jnp.mean(xv * xv, axis=1, keepdims=True) + EPS)
        xn = xv * r
        g = g_ref[...]
        one_sc = 1.0 + sc_ref[...]
        dsh_ref[...] += _rowsum(dh)
        dsc_ref[...] += _rowsum(dh * (xn * g))
        dg_ref[...] += _rowsum(dh * one_sc * xn)
        dxn = dh * (g * one_sc)
        dx_ref[...] = dxo_ref[...] + r * (dxn - xn * jnp.mean(dxn * xn, axis=1, keepdims=True))

    tile = pl.BlockSpec((None, tm, D), lambda b, m: (b, m, 0))
    vec = pl.BlockSpec((None, 1, D), lambda b, m: (b, 0, 0))
    one = pl.BlockSpec((1, D), lambda b, m: (0, 0))
    return pl.pallas_call(
        body, name="norm_mod_bwd", grid=(nb, s // tm),
        in_specs=[tile, tile, tile, one, vec],
        out_specs=[tile, vec, vec, one],
        out_shape=[jax.ShapeDtypeStruct(x.shape, F32), jax.ShapeDtypeStruct((nb, 1, D), F32),
                   jax.ShapeDtypeStruct((nb, 1, D), F32), jax.ShapeDtypeStruct((1, D), F32)],
        compiler_params=_params(("arbitrary", "arbitrary")),
    )(dh, x, dxo, gain, scale)


def _in_proj_bwd_w(h, dproj, dep):
    nb, s, _ = h.shape
    tm = _row_tile(s, 2048)
    n_m = s // tm

    def body(h_ref, dp_ref, dep_ref, o_ref, acc_ref):
        b, m = pl.program_id(1), pl.program_id(2)

        @pl.when((b == 0) & (m == 0))
        def _():
            acc_ref[...] = jnp.zeros_like(acc_ref)

        acc_ref[...] += _dot_tn(h_ref[...], dp_ref[...])

        @pl.when((b == nb - 1) & (m == n_m - 1))
        def _():
            o_ref[0] = acc_ref[:, :UNIT].astype(BF16)
            o_ref[1] = acc_ref[:, UNIT:].astype(BF16)

    return pl.pallas_call(
        body, name="in_proj_bwd_w", grid=(N_SEG, nb, n_m),
        in_specs=[pl.BlockSpec((None, tm, D), lambda j, b, m: (b, m, 0)),
                  pl.BlockSpec((None, None, tm, D), lambda j, b, m: (j, b, m, 0)),
                  pl.BlockSpec((8, LANES), lambda j, b, m: (0, 0))],
        out_specs=pl.BlockSpec((2, D, UNIT), lambda j, b, m: (j, 0, 0)),
        out_shape=jax.ShapeDtypeStruct((2 * N_SEG, D, UNIT), BF16),
        scratch_shapes=[pltpu.VMEM((D, D), F32)],
        compiler_params=_params(("arbitrary", "arbitrary", "arbitrary")),
    )(h, dproj, dep)


def _mod_proj(c_all, w_mod, b_mod_mine):
    nl, _, ncol = w_mod.shape
    nbg = c_all.shape[0]

    def body(c_ref, w_ref, b_ref, o_ref):
        cv = c_ref[...]
        o_ref[...] = jnp.dot(cv * jax.nn.sigmoid(cv), w_ref[...], preferred_element_type=F32,
                             precision=lax.Precision.HIGHEST) + b_ref[...]

    return pl.pallas_call(
        body, name="mod_proj", grid=(nl,),
        in_specs=[pl.BlockSpec((nbg, D), lambda l: (0, 0)), pl.BlockSpec((None, D, ncol), lambda l: (l, 0, 0)),
                  pl.BlockSpec((None, 1, ncol), lambda l: (l, 0, 0))],
        out_specs=pl.BlockSpec((None, nbg, ncol), lambda l: (l, 0, 0)),
        out_shape=jax.ShapeDtypeStruct((nl, nbg, ncol), F32),
        compiler_params=_params(("arbitrary",)),
    )(c_all, w_mod, b_mod_mine)


def _mod_grad(c_all, dmod_all, dmod_mine):
    nl, nbg, ncol = dmod_mine.shape

    def body(c_ref, da_ref, dm_ref, gw_ref, gb_ref):
        cv = c_ref[...]
        gw_ref[...] = lax.dot_general(cv * jax.nn.sigmoid(cv), dm_ref[...], (((0,), (0,)), ((), ())),
                                      preferred_element_type=F32, precision=lax.Precision.HIGHEST)
        gb_ref[...] = _rowsum(da_ref[...])

    return pl.pallas_call(
        body, name="mod_grad", grid=(nl,),
        in_specs=[pl.BlockSpec((nbg, D), lambda l: (0, 0)), pl.BlockSpec((None, nbg, 3 * D), lambda l: (l, 0, 0)),
                  pl.BlockSpec((None, nbg, ncol), lambda l: (l, 0, 0))],
        out_specs=[pl.BlockSpec((None, D, ncol), lambda l: (l, 0, 0)),
                   pl.BlockSpec((None, 1, 3 * D), lambda l: (l, 0, 0))],
        out_shape=[jax.ShapeDtypeStruct((nl, D, ncol), F32), jax.ShapeDtypeStruct((nl, 1, 3 * D), F32)],
        compiler_params=_params(("arbitrary",)),
    )(c_all, dmod_all, dmod_mine)


def _adamw(parts, w, m, v, name, layer=None, prev=None):
    n_parts, n_u, n_r, cu = parts.shape
    assert w.shape[-2:] == (n_r, n_u * cu), (parts.shape, w.shape)
    tr = n_r
    for cand in (512, 256, 128):
        if n_r > cand and n_r % cand == 0:
            tr = cand
            break
    n_prev = 0 if prev is None else 4

    def body(p_ref, w_ref, m_ref, v_ref, *rest):
        g_ref, d_ref, nm_ref, nv_ref = rest[n_prev:]
        g = p_ref[0].astype(F32)
        for k in range(1, n_parts):
            g = g + p_ref[k].astype(F32)
        m2 = ADAM_B1 * m_ref[...] + (1.0 - ADAM_B1) * g
        v2 = ADAM_B2 * v_ref[...] + (1.0 - ADAM_B2) * (g * g)
        m_hat = m2 / (1.0 - ADAM_B1 ** ADAM_STEP)
        v_hat = v2 / (1.0 - ADAM_B2 ** ADAM_STEP)
        g_ref[...] = g
        d_ref[...] = -ADAM_LR * (m_hat / (jnp.sqrt(v_hat) + ADAM_EPS) + ADAM_WD * w_ref[...])
        nm_ref[...] = m2
        nv_ref[...] = v2

    if layer is None:
        tile = pl.BlockSpec((tr, cu), lambda u, i: (i, u))
    else:
        tile = pl.BlockSpec((None, tr, cu), lambda u, i: (layer, i, u))
    shp = jax.ShapeDtypeStruct(w.shape, F32)
    return pl.pallas_call(
        body, name=name, grid=(n_u, n_r // tr),
        in_specs=[pl.BlockSpec((n_parts, None, tr, cu), lambda u, i: (0, u, i, 0)), tile, tile, tile]
        + [pl.BlockSpec(memory_space=pl.ANY)] * n_prev,
        out_specs=[tile, tile, tile, tile], out_shape=[shp, shp, shp, shp],
        input_output_aliases={4 + k: k for k in range(n_prev)},
        compiler_params=_params(("arbitrary", "arbitrary")),
    )(parts, w, m, v, *(prev or ()))


def _gathered_cols(g, inner):
    k = len(inner)
    perm = tuple(range(1, k + 1)) + (0, k + 1)
    t = jnp.transpose(g, perm)
    return t.reshape(tuple(inner) + (g.shape[0] * g.shape[-1],))


def _pair_blocks(wh):
    z = jnp.zeros((8, 64, 64), wh.dtype)
    w2 = wh.reshape(8, 2, 64, 64)
    top = jnp.concatenate([w2[:, 0], z], axis=2)
    bot = jnp.concatenate([z, w2[:, 1]], axis=2)
    return jnp.concatenate([top, bot], axis=1).astype(BF16)


def _unpair_blocks(g):
    return jnp.stack([g[:, :64, :64], g[:, 64:, 64:]], axis=1).reshape(16, 64, 64)


FLAT_ROWS = 512


def _pack_rows(arrays, lead=0):
    parts = [a.reshape(a.shape[:lead] + (-1, LANES)) for a in arrays]
    rows = jnp.concatenate(parts, axis=lead)
    pad = [(0, 0)] * rows.ndim
    pad[lead] = (0, (-rows.shape[lead]) % FLAT_ROWS)
    return jnp.pad(rows, pad)


def kernel(x, c, norm_gain, w_mod, b_mod, w_in, w_out, conv_a_w, sgu_w, sgu_b, lru_conv_w, lru_conv_b, lru_wa, lru_ba, lru_wx, lru_bx, lru_lambda, final_gain, loss_target, m_norm_gain, m_w_mod, m_b_mod, m_w_in, m_w_out, m_conv_a_w, m_sgu_w, m_sgu_b, m_lru_conv_w, m_lru_conv_b, m_lru_wa, m_lru_ba, m_lru_wx, m_lru_bx, m_lru_lambda, m_final_gain, v_norm_gain, v_w_mod, v_b_mod, v_w_in, v_w_out, v_conv_a_w, v_sgu_w, v_sgu_b, v_lru_conv_w, v_lru_conv_b, v_lru_wa, v_lru_ba, v_lru_wx, v_lru_bx, v_lru_lambda, v_final_gain):
    nl = w_in.shape[0]
    nb, s, _ = x.shape
    me = _my_index()
    mod_cols = w_mod.shape[2]


    small = jnp.concatenate([c.reshape(-1, LANES), conv_a_w.reshape(-1, LANES), lru_conv_w.reshape(-1, LANES)])
    n_c, n_ca = nb * D // LANES, nl * 3
    n_small = small.shape[0]
    small = jnp.pad(small, ((0, (-n_small) % 8), (0, 0)))
    small_all, _ = _all_gather(small, "gather_small")
    c_all = small_all[:, :n_c].reshape(N_DEV * nb, D)

    w_in_b = [w_in[0].astype(BF16)] + list(w_in[1:].astype(BF16))

    def start_w_in(l, dep):
        return _split_start(w_in_b[l], _peers_same_core, False, "gather_w_in_start", dep)

    conv_a_full = _gathered_cols(small_all[:, n_c:n_c + n_ca].reshape(N_DEV, nl, 3, LANES), (nl, 3))
    lru_conv_full = _gathered_cols(small_all[:, n_c + n_ca:n_small].reshape(N_DEV, nl, 4, LANES), (nl, 4))

    def gathered_w_in(started, after):
        block, land = _split_wait(started, after, _peers_same_core, False, False, "gather_w_in_wait")
        return _gather_finish(block, land, "gather_w_in_finish")

    sgu_b_lanes = jnp.broadcast_to(sgu_b[..., None], sgu_b.shape + (LANES,))
    mws = []
    for l in range(nl):
        mws.append((conv_a_full[l], sgu_w[l], sgu_b_lanes[l], lru_conv_full[l], lru_conv_b[l][None, :],
                    _pair_blocks(lru_wa[l]), _pair_blocks(lru_wx[l]), lru_ba[l].reshape(1, D),
                    lru_bx[l].reshape(1, D), lru_lambda[l][None, :]))

    rep_names = ["sgu_w", "sgu_b", "lru_conv_b", "lru_wa", "lru_ba", "lru_wx", "lru_bx", "lru_lambda"]
    rep_w = dict(sgu_w=sgu_w, sgu_b=sgu_b, lru_conv_b=lru_conv_b, lru_wa=lru_wa, lru_ba=lru_ba,
                 lru_wx=lru_wx, lru_bx=lru_bx, lru_lambda=lru_lambda)
    rep_m = dict(sgu_w=m_sgu_w, sgu_b=m_sgu_b, lru_conv_b=m_lru_conv_b, lru_wa=m_lru_wa,
                 lru_ba=m_lru_ba, lru_wx=m_lru_wx, lru_bx=m_lru_bx, lru_lambda=m_lru_lambda)
    rep_v = dict(sgu_w=v_sgu_w, sgu_b=v_sgu_b, lru_conv_b=v_lru_conv_b, lru_wa=v_lru_wa,
                 lru_ba=v_lru_ba, lru_wx=v_lru_wx, lru_bx=v_lru_bx, lru_lambda=v_lru_lambda)

    rep_w_all, rep_m_all, rep_v_all = [_pack_rows([src[n] for n in rep_names], lead=1)
                                       for src in (rep_w, rep_m, rep_v)]
    early = [rep_w_all, rep_m_all, rep_v_all] + w_in_b[1:] + [a for mw in mws for a in mw]

    b_mod_mine = lax.dynamic_slice_in_dim(b_mod, me * mod_cols, mod_cols, axis=1)[:, None, :]
    mod_mine = _mod_proj(c_all, w_mod, b_mod_mine)
    mod_all, mod_token = _all_gather(mod_mine.reshape(nl * N_DEV * nb, mod_cols), "gather_mod", dep=w_in_b[0])
    ici = {0: start_w_in(0, mod_token)}
    mod_full = _gathered_cols(mod_all.reshape(N_DEV, nl, N_DEV * nb, mod_cols), (nl, N_DEV * nb))
    mod_loc = lax.dynamic_slice_in_dim(mod_full, me * nb, nb, axis=1)
    shift, scale, gate = [mod_loc[:, :, j * D:(j + 1) * D][:, :, None, :] for j in range(3)]

    xs, hs_bf, projs, mergeds, states, wg = [], [], [], [], [], []
    xl = x
    d2d = {}
    wo_started = _split_start(w_out.astype(BF16).reshape(nl * (D // N_DEV), D), _peers_all, False,
                              "gather_w_out_start", ici[0][4])
    wo = None
    h = _norm_mod(xl, _after(norm_gain[0][None, :], wo_started[4]), shift[0], scale[0])
    for l in range(nl):
        if l == 0:
            wg_l, token = gathered_w_in(ici[0], [h] + early)
            ici[1] = start_w_in(1, token)
            dep = ici[1][4]
        else:
            wg_l = _forward_wait(d2d[l], h, "gather_w_in_d2d_wait")
            dep = d2d[l][4]
        wg.append(wg_l)
        proj = _in_proj(h, wg_l, dep)
        merged, *st = _mixer_fwd(proj, mws[l])
        xs.append(xl), hs_bf.append(h), projs.append(proj), mergeds.append(merged), states.append(st)
        gate_l = gate[l]
        if l + 1 < nl:
            block, land = _split_wait(ici[l + 1], merged, _peers_same_core, False, False, "gather_w_in_wait")
            d2d[l + 1] = _forward_start(block, land, "gather_w_in_d2d_start")
            gate_l = _after(gate_l, d2d[l + 1][4])
            if l + 2 < nl:
                ici[l + 2] = start_w_in(l + 2, d2d[l + 1][4])
                gate_l = _after(gate_l, ici[l + 2][4])
        if wo is None:
            _, wo_all = _split_wait(wo_started, merged, _peers_all, False, True, "gather_w_out_wait")
            wo = jnp.transpose(wo_all.reshape(N_DEV, nl, D // N_DEV, D), (1, 0, 2, 3)).reshape(nl, D, D)
        if l + 1 < nl:
            xl, h = _out_proj_norm(xl, merged, wo[l], gate_l, norm_gain[l + 1][None, :], shift[l + 1], scale[l + 1])
        else:
            xl = _out_proj(xl, merged, wo[l], gate_l)

    loss_row, dx, g_final = _loss_head(xl, final_gain[None, :], loss_target)

    res_big, g_conv = {}, [None] * nl
    dmods = [None] * nl

    def finish_exchange(pending, after):
        l, h_in, h_out, h_rep = pending
        r_in, r_out, r_rep = _multi_wait([h_in, h_out, h_rep], after, "scatter_wait")
        res_big["w_in"] = _adamw(r_in, w_in, m_w_in, v_w_in, "adamw_w_in", l, res_big.get("w_in"))
        res_big["w_out"] = _adamw(r_out, w_out, m_w_out, v_w_out, "adamw_w_out", l, res_big.get("w_out"))
        res_big["rep"] = _adamw(r_rep[:, None], rep_w_all, rep_m_all, rep_v_all, "adamw_rep", l, res_big.get("rep"))

    pending = None
    g_gains = [None] * nl
    for l in reversed(range(nl)):
        dmerged, gw_out, dgate = _out_proj_bwd(dx, mergeds[l], wo[l], gate[l])
        dproj, g_caw, g_sw, g_sb, g_lcw, g_vec, g_wa, g_wx = _mixer_bwd(projs[l], dmerged, *states[l], mws[l])
        g_conv[l] = (g_caw, g_lcw)
        rep_g = dict(
            sgu_w=g_sw, sgu_b=g_sb[:, :, 0], lru_conv_b=g_vec[0],
            lru_wa=_unpair_blocks(g_wa), lru_ba=g_vec[1].reshape(16, 64), lru_wx=_unpair_blocks(g_wx),
            lru_bx=g_vec[2].reshape(16, 64), lru_lambda=g_vec[3])
        rep_block = _pack_rows([rep_g[n] for n in rep_names])
        (h_out, h_rep), token = _multi_start([(gw_out.reshape(N_DEV, 1, D // N_DEV, D), True), (rep_block, False)],
                                             "scatter_small_start")
        gw_in = _in_proj_bwd_w(hs_bf[l], dproj, token)
        (h_in,), token = _multi_start([(gw_in.reshape(N_DEV, UNITS_PER_DEV, D, UNIT), True)], "scatter_w_in_start")
        started = (l, h_in, h_out, h_rep)
        dh = _in_proj_bwd_h(dproj, wg[l], token)
        dx, dshift, dscale, g_gain = _norm_mod_bwd(dh, xs[l], dx, norm_gain[l][None, :], scale[l])
        g_gains[l] = g_gain
        dmods[l] = jnp.concatenate([dshift, dscale, dgate], axis=2)[:, 0, :]
        if pending is not None:
            finish_exchange(pending, dx)
        pending = started

    conv_parts = jnp.concatenate(
        [jnp.stack([g_conv[l][0] for l in range(nl)]).reshape(nl * 3, N_DEV, LANES),
         jnp.stack([g_conv[l][1] for l in range(nl)]).reshape(nl * 4, N_DEV, LANES)], axis=0)
    conv_parts = jnp.transpose(conv_parts, (1, 0, 2))[:, None]
    conv_recv = _all_to_all(conv_parts, "scatter_conv")

    dmod_loc = jnp.stack(dmods).reshape(nl * nb, 3 * D)
    gain_rows = jnp.pad(jnp.concatenate(g_gains + [g_final, loss_row], axis=0),
                        ((0, (-(nl + 2)) % 8), (0, 2 * D)))
    tail_g, _ = _all_gather(jnp.concatenate([dmod_loc, gain_rows], axis=0), "gather_dmod",
                            dep=[res_big[k][3] for k in ("w_in", "w_out", "rep")])
    loss = jnp.sum(tail_g[:, nl * nb + nl + 1, 0])
    dmod_g = tail_g[:, :nl * nb]
    gain_parts = tail_g[:, nl * nb:nl * nb + nl + 1, :D][:, None]
    gain_cat = lambda a, b: jnp.concatenate([a, b[None, :]], axis=0)
    res_gain = _adamw(gain_parts, gain_cat(norm_gain, final_gain), gain_cat(m_norm_gain, m_final_gain),
                      gain_cat(v_norm_gain, v_final_gain), "adamw_gain")
    dmod_all = jnp.transpose(dmod_g.reshape(N_DEV, nl, nb, 3 * D), (1, 0, 2, 3)).reshape(nl, N_DEV * nb, 3 * D)
    dmod_mine = lax.dynamic_slice_in_dim(dmod_all, me * mod_cols, mod_cols, axis=2)
    gw_mod, gb_mod = _mod_grad(c_all, dmod_all, dmod_mine)
    res_w_mod = _adamw(gw_mod.reshape(1, 1, nl * D, mod_cols), w_mod.reshape(nl * D, mod_cols),
                       m_w_mod.reshape(nl * D, mod_cols), v_w_mod.reshape(nl * D, mod_cols), "adamw_w_mod")
    res_w_mod = [a.reshape(nl, D, mod_cols) for a in res_w_mod]
    res_b_mod = _adamw(gb_mod.reshape(1, 1, nl, 3 * D), b_mod, m_b_mod, v_b_mod, "adamw_b_mod")
    finish_exchange(pending, res_b_mod[1])

    cat = lambda a, b: jnp.concatenate([a.reshape(nl * 3, LANES), b.reshape(nl * 4, LANES)], axis=0)
    res_conv = _adamw(conv_recv, cat(conv_a_w, lru_conv_w), cat(m_conv_a_w, m_lru_conv_w),
                      cat(v_conv_a_w, v_lru_conv_w), "adamw_conv")
    res_conv_a = [a[:nl * 3].reshape(nl, 3, LANES) for a in res_conv]
    res_lru_conv = [a[nl * 3:].reshape(nl, 4, LANES) for a in res_conv]

    res_rep = []
    for k in range(4):
        off, d = 0, {}
        for n in rep_names:
            n_rows = rep_w[n][0].size // LANES
            d[n] = res_big["rep"][k][:, off:off + n_rows].reshape(rep_w[n].shape)
            off += n_rows
        res_rep.append(d)

    def leaf(k, name):
        if name == "norm_gain":
            return res_gain[k][:nl]
        if name == "final_gain":
            return res_gain[k][nl]
        if name == "w_mod":
            return res_w_mod[k]
        if name == "b_mod":
            return res_b_mod[k]
        if name in ("w_in", "w_out"):
            return res_big[name][k]
        if name == "conv_a_w":
            return res_conv_a[k]
        if name == "lru_conv_w":
            return res_lru_conv[k]
        return res_rep[k][name]

    order = ["norm_gain", "w_mod", "b_mod", "w_in", "w_out", "conv_a_w", "sgu_w", "sgu_b", "lru_conv_w",
             "lru_conv_b", "lru_wa", "lru_ba", "lru_wx", "lru_bx", "lru_lambda", "final_gain"]
    outs = [loss, dx]
    for k in range(4):
        outs += [leaf(k, n) for n in order]
    return tuple(outs)
```

```python
import functools

import jax
import jax.numpy as jnp
from jax import lax
from jax.experimental import pallas as pl
from jax.experimental.pallas import tpu as pltpu

F32 = jnp.float32
BF16 = jnp.bfloat16

D = 1024
N_DEV = 8
N_SEG = 12
LANES = 128
SUBLANES = 8
CHUNK = 128
HALO = 16
FWD_CHUNKS_PER_TRIP = 4
BWD_CHUNKS_PER_TRIP = 8
UNIT = 512
UNITS_PER_DEV = 3
EPS = 1e-6
LRU_C = 8.0
ADAM_LR, ADAM_B1, ADAM_B2, ADAM_EPS, ADAM_WD, ADAM_STEP = 0.001, 0.9, 0.999, 1e-08, 0.01, 10
VMEM_LIMIT = 56 * 1024 * 1024

AX, AB, AC, AZ, SU, SV, SZ, RX, RZ, GA, GS, GR = range(N_SEG)
MESH = pl.DeviceIdType.MESH


def _params(sem=None):
    return pltpu.CompilerParams(dimension_semantics=sem, vmem_limit_bytes=VMEM_LIMIT)


def _my_index():
    return 4 * lax.axis_index("x") + 2 * lax.axis_index("y") + lax.axis_index("c")


def _all_gather(block, name, dep=None):
    deps = [] if dep is None else list(dep) if isinstance(dep, (list, tuple)) else [dep]

    def body(x_ref, *refs):
        out_ref, token, send_sems, recv_sems, local_sem = refs[-5:]
        x, y, c = lax.axis_index("x"), lax.axis_index("y"), lax.axis_index("c")
        me, sibling = (x, y, c), (x, y, 1 - c)
        chips = [(1 - x, y), (x, 1 - y), (1 - x, 1 - y)]
        token[...] = jnp.zeros_like(token)

        def rows(px, py, pc):
            return out_ref.at[4 * px + 2 * py + pc]

        def copy(k, blk, to, src=None):
            return pltpu.make_async_remote_copy(
                src_ref=rows(*blk) if src is None else src, dst_ref=rows(*blk),
                send_sem=send_sems.at[k], recv_sem=recv_sems.at[k], device_id=to, device_id_type=MESH)

        mine = pltpu.make_async_copy(x_ref, rows(*me), local_sem)
        mine.start()
        first = [copy(0, me, sibling, src=x_ref)]
        first += [copy(1 + j, me, (*chip, c), src=x_ref) for j, chip in enumerate(chips)]
        for cp in first:
            cp.start()
        passed = [copy(4 + j, (*chip, c), sibling) for j, chip in enumerate(chips)]
        for j, chip in enumerate(chips):
            copy(1 + j, (*chip, c), me).wait_recv()
            passed[j].start()
        copy(0, sibling, me).wait_recv()
        for j, chip in enumerate(chips):
            copy(4 + j, (*chip, 1 - c), me).wait_recv()
        for cp in first + passed:
            cp.wait_send()
        mine.wait()

    return pl.pallas_call(
        body, name=name,
        out_shape=[jax.ShapeDtypeStruct((N_DEV,) + block.shape, block.dtype), jax.ShapeDtypeStruct((8, LANES), F32)],
        in_specs=[pl.BlockSpec(memory_space=pltpu.VMEM)]
        + [pl.BlockSpec(memory_space=pl.ANY)] * len(deps),
        out_specs=[pl.BlockSpec(memory_space=pl.ANY), pl.BlockSpec(memory_space=pltpu.VMEM)],
        scratch_shapes=[pltpu.SemaphoreType.DMA((7,)), pltpu.SemaphoreType.DMA((7,)), pltpu.SemaphoreType.DMA],
    )(block, *deps)


def _all_to_all(blocks, name):
    def body(x_ref, out_ref, send_sems, recv_sems, local_sem):
        x, y, c = lax.axis_index("x"), lax.axis_index("y"), lax.axis_index("c")
        my = 4 * x + 2 * y + c
        mine = pltpu.make_async_copy(x_ref.at[my], out_ref.at[my], local_sem)
        mine.start()
        peers = []
        for r in range(1, N_DEV):
            px = 1 - x if r & 4 else x
            py = 1 - y if r & 2 else y
            pc = 1 - c if r & 1 else c
            peers.append((r - 1, 4 * px + 2 * py + pc, (px, py, pc)))

        def copy(k, src_slot, dst_slot, to):
            return pltpu.make_async_remote_copy(
                src_ref=x_ref.at[src_slot], dst_ref=out_ref.at[dst_slot],
                send_sem=send_sems.at[k], recv_sem=recv_sems.at[k], device_id=to, device_id_type=MESH)

        sends = [copy(k, pid, my, to) for k, pid, to in peers]
        for cp in sends:
            cp.start()
        for k, pid, to in peers:
            copy(k, pid, pid, to).wait_recv()
        for cp in sends:
            cp.wait_send()
        mine.wait()

    return pl.pallas_call(
        body, name=name,
        out_shape=jax.ShapeDtypeStruct(blocks.shape, blocks.dtype),
        in_specs=[pl.BlockSpec(memory_space=pltpu.VMEM)],
        out_specs=pl.BlockSpec(memory_space=pl.ANY),
        scratch_shapes=[pltpu.SemaphoreType.DMA((7,)), pltpu.SemaphoreType.DMA((7,)), pltpu.SemaphoreType.DMA],
    )(blocks)


_HBM = pl.BlockSpec(memory_space=pltpu.HBM)
_SEM = pl.BlockSpec(memory_space=pltpu.SEMAPHORE)
_EFFECT = pltpu.SideEffectType.DATAFLOW_SIDE_EFFECTING


def _peers_all(x, y, c):
    out = []
    for r in range(1, N_DEV):
        px = 1 - x if r & 4 else x
        py = 1 - y if r & 2 else y
        pc = 1 - c if r & 1 else c
        out.append((r - 1, 4 * px + 2 * py + pc, (px, py, pc)))
    return out


def _peers_same_core(x, y, c):
    return [(k, 4 * px + 2 * py + c, (px, py, c))
            for k, (px, py) in enumerate([(1 - x, y), (x, 1 - y), (1 - x, 1 - y)])]


def _split_start(src, peers_fn, scatter, name, dep=None):
    blk = src.shape[1:] if scatter else src.shape
    land_shape = (N_DEV,) + tuple(blk)
    n = len(peers_fn(0, 0, 0))
    deps = [] if dep is None else [dep]

    def body(x_ref, land_ref, *rest):
        send_sems, recv_sems, x_thru, land_thru, token = rest[len(deps):]
        x, y, c = lax.axis_index("x"), lax.axis_index("y"), lax.axis_index("c")
        my = 4 * x + 2 * y + c
        for k, pid, to in peers_fn(x, y, c):
            pltpu.make_async_remote_copy(
                src_ref=x_ref.at[pid] if scatter else x_ref, dst_ref=land_ref.at[my],
                send_sem=send_sems.at[k], recv_sem=recv_sems.at[k], device_id=to, device_id_type=MESH).start()
        token[...] = jnp.zeros_like(token)

    return pl.pallas_call(
        body, name=name,
        out_shape=(pltpu.SemaphoreType.DMA((n,)), pltpu.SemaphoreType.DMA((n,)),
                   pltpu.HBM(src.shape, src.dtype), pltpu.HBM(land_shape, src.dtype),
                   jax.ShapeDtypeStruct((8, LANES), F32)),
        in_specs=(_HBM, _HBM) + (pl.BlockSpec(memory_space=pl.ANY),) * len(deps),
        out_specs=(_SEM, _SEM, _HBM, _HBM, pl.BlockSpec(memory_space=pltpu.VMEM)),
        input_output_aliases={0: 2, 1: 3},
        compiler_params=pltpu.CompilerParams(has_side_effects=_EFFECT),
    )(pltpu.with_memory_space_constraint(src, pltpu.HBM),
      pltpu.with_memory_space_constraint(lax.empty(land_shape, src.dtype), pltpu.HBM), *deps)


def _split_wait(handles, after, peers_fn, scatter, own, name):
    send_sems, recv_sems, src_thru, land_thru, _ = handles
    blk = land_thru.shape[1:]
    after = list(after) if isinstance(after, (list, tuple)) else [after]

    def body(x_ref, land_ref, send_sems, recv_sems, *rest):
        stage = rest[len(after) + 2:]
        x, y, c = lax.axis_index("x"), lax.axis_index("y"), lax.axis_index("c")
        if own:
            my = 4 * x + 2 * y + c
            mine = _staged_copy(x_ref.at[my] if scatter else x_ref, land_ref.at[my], *stage)
        for k, pid, to in peers_fn(x, y, c):
            cp = pltpu.make_async_remote_copy(
                src_ref=x_ref.at[pid] if scatter else x_ref, dst_ref=land_ref.at[pid],
                send_sem=send_sems.at[k], recv_sem=recv_sems.at[k], device_id=to, device_id_type=MESH)
            cp.wait_send()
            cp.wait_recv()
        if own:
            mine.wait()

    return pl.pallas_call(
        body, name=name,
        out_shape=(pltpu.HBM(src_thru.shape, src_thru.dtype), pltpu.HBM(land_thru.shape, land_thru.dtype)),
        in_specs=(_HBM, _HBM, _SEM, _SEM) + (pl.BlockSpec(memory_space=pl.ANY),) * len(after),
        out_specs=(_HBM, _HBM),
        input_output_aliases={0: 0, 1: 1},
        scratch_shapes=[pltpu.VMEM(blk, land_thru.dtype), pltpu.SemaphoreType.DMA((2,))] if own else [],
        compiler_params=pltpu.CompilerParams(has_side_effects=_EFFECT, vmem_limit_bytes=VMEM_LIMIT),
    )(src_thru, land_thru, send_sems, recv_sems, *after)


def _staged_copy(src_ref, dst_ref, buf, sems):
    leg = pltpu.make_async_copy(src_ref, buf, sems.at[0])
    leg.start()
    leg.wait()
    leg = pltpu.make_async_copy(buf, dst_ref, sems.at[1])
    leg.start()
    return leg


def _multi_start(srcs, name, dep=None):
    n_src = len(srcs)
    lands = [(N_DEV,) + tuple(a.shape[1:] if sc else a.shape) for a, sc in srcs]
    deps = [] if dep is None else [dep]

    def body(*refs):
        ins, outs = refs[:2 * n_src], refs[2 * n_src + len(deps):]
        x, y, c = lax.axis_index("x"), lax.axis_index("y"), lax.axis_index("c")
        my = 4 * x + 2 * y + c
        for i, (_, scatter) in enumerate(srcs):
            x_ref, land_ref, send_sems, recv_sems = ins[2 * i], ins[2 * i + 1], outs[4 * i], outs[4 * i + 1]
            for k, pid, to in _peers_all(x, y, c):
                pltpu.make_async_remote_copy(
                    src_ref=x_ref.at[pid] if scatter else x_ref, dst_ref=land_ref.at[my],
                    send_sem=send_sems.at[k], recv_sem=recv_sems.at[k], device_id=to, device_id_type=MESH).start()
        outs[-1][...] = jnp.zeros_like(outs[-1])

    out_shape, out_specs, operands = [], [], []
    for (a, _), land in zip(srcs, lands):
        out_shape += [pltpu.SemaphoreType.DMA((N_DEV - 1,)), pltpu.SemaphoreType.DMA((N_DEV - 1,)),
                      pltpu.HBM(a.shape, a.dtype), pltpu.HBM(land, a.dtype)]
        out_specs += [_SEM, _SEM, _HBM, _HBM]
        operands += [pltpu.with_memory_space_constraint(a, pltpu.HBM),
                     pltpu.with_memory_space_constraint(lax.empty(land, a.dtype), pltpu.HBM)]
    res = pl.pallas_call(
        body, name=name,
        out_shape=tuple(out_shape) + (jax.ShapeDtypeStruct((8, LANES), F32),),
        in_specs=(_HBM,) * (2 * n_src) + (pl.BlockSpec(memory_space=pl.ANY),) * len(deps),
        out_specs=tuple(out_specs) + (pl.BlockSpec(memory_space=pltpu.VMEM),),
        input_output_aliases={2 * i + j: 4 * i + 2 + j for i in range(n_src) for j in range(2)},
        compiler_params=pltpu.CompilerParams(has_side_effects=_EFFECT),
    )(*operands, *deps)
    return [tuple(res[4 * i:4 * i + 4]) + (srcs[i][1],) for i in range(n_src)], res[-1]


def _multi_wait(started, after, name):
    n_src = len(started)
    after = list(after) if isinstance(after, (list, tuple)) else [after]

    def body(*refs):
        ins = refs[:4 * n_src]
        stage = refs[4 * n_src + len(after) + 2 * n_src:]
        x, y, c = lax.axis_index("x"), lax.axis_index("y"), lax.axis_index("c")
        my = 4 * x + 2 * y + c
        pending = []
        for i, h in enumerate(started):
            x_ref, land_ref, send_sems, recv_sems = ins[4 * i:4 * i + 4]
            scatter = h[4]
            pending.append(_staged_copy(x_ref.at[my] if scatter else x_ref, land_ref.at[my],
                                        stage[2 * i], stage[2 * i + 1]))
            for k, pid, to in _peers_all(x, y, c):
                cp = pltpu.make_async_remote_copy(
                    src_ref=x_ref.at[pid] if scatter else x_ref, dst_ref=land_ref.at[pid],
                    send_sem=send_sems.at[k], recv_sem=recv_sems.at[k], device_id=to, device_id_type=MESH)
                cp.wait_send()
                cp.wait_recv()
        for leg in pending:
            leg.wait()

    operands, out_shape, scratch = [], [], []
    for send, recv, src_thru, land_thru, _ in started:
        operands += [src_thru, land_thru, send, recv]
        out_shape += [pltpu.HBM(src_thru.shape, src_thru.dtype), pltpu.HBM(land_thru.shape, land_thru.dtype)]
        scratch += [pltpu.VMEM(land_thru.shape[1:], land_thru.dtype), pltpu.SemaphoreType.DMA((2,))]
    res = pl.pallas_call(
        body, name=name,
        out_shape=tuple(out_shape),
        in_specs=(_HBM, _HBM, _SEM, _SEM) * n_src + (pl.BlockSpec(memory_space=pl.ANY),) * len(after),
        out_specs=(_HBM,) * (2 * n_src),
        input_output_aliases={4 * i + j: 2 * i + j for i in range(n_src) for j in range(2)},
        scratch_shapes=scratch,
        compiler_params=pltpu.CompilerParams(has_side_effects=_EFFECT, vmem_limit_bytes=VMEM_LIMIT),
    )(*operands, *after)
    return [res[2 * i + 1] for i in range(n_src)]


def _gather_finish(block, land, name):
    def body(x_ref, land_ref, out_ref, token, send_sems, recv_sems, buf, local_sems):
        x, y, c = lax.axis_index("x"), lax.axis_index("y"), lax.axis_index("c")
        my, sib_id, sibling = 4 * x + 2 * y + c, 4 * x + 2 * y + 1 - c, (x, y, 1 - c)
        token[...] = jnp.zeros_like(token)

        def copy(k, slot, src=None):
            return pltpu.make_async_remote_copy(
                src_ref=land_ref.at[slot] if src is None else src, dst_ref=out_ref.at[slot],
                send_sem=send_sems.at[k], recv_sem=recv_sems.at[k], device_id=sibling, device_id_type=MESH)

        chips = _peers_same_core(x, y, c)
        sends = [copy(0, my, src=x_ref)] + [copy(1 + k, pid) for k, pid, _ in chips]
        for cp in sends:
            cp.start()
        mine = _staged_copy(x_ref, out_ref.at[my], buf, local_sems)
        copy(0, sib_id).wait_recv()
        for k, pid, _ in chips:
            copy(1 + k, pid + 1 - 2 * c).wait_recv()
        for cp in sends:
            cp.wait_send()
        mine.wait()

    return pl.pallas_call(
        body, name=name,
        out_shape=[jax.ShapeDtypeStruct(land.shape, land.dtype), jax.ShapeDtypeStruct((8, LANES), F32)],
        in_specs=[pl.BlockSpec(memory_space=pl.ANY), pl.BlockSpec(memory_space=pl.ANY)],
        out_specs=[pl.BlockSpec(memory_space=pl.ANY), pl.BlockSpec(memory_space=pltpu.VMEM)],
        input_output_aliases={1: 0},
        scratch_shapes=[pltpu.SemaphoreType.DMA((4,)), pltpu.SemaphoreType.DMA((4,)),
                        pltpu.VMEM(block.shape, block.dtype), pltpu.SemaphoreType.DMA((2,))],
        compiler_params=pltpu.CompilerParams(vmem_limit_bytes=VMEM_LIMIT),
    )(block, land)


def _forward_start(block, land, name):
    def body(x_ref, land_ref, send_sems, recv_sems, x_thru, land_thru, token):
        x, y, c = lax.axis_index("x"), lax.axis_index("y"), lax.axis_index("c")
        my, sibling = 4 * x + 2 * y + c, (x, y, 1 - c)
        slots = [(0, my, x_ref)] + [(1 + k, pid, land_ref.at[pid]) for k, pid, _ in _peers_same_core(x, y, c)]
        for k, slot, src in slots:
            pltpu.make_async_remote_copy(
                src_ref=src, dst_ref=land_ref.at[slot], send_sem=send_sems.at[k], recv_sem=recv_sems.at[k],
                device_id=sibling, device_id_type=MESH).start()
        token[...] = jnp.zeros_like(token)

    return pl.pallas_call(
        body, name=name,
        out_shape=(pltpu.SemaphoreType.DMA((4,)), pltpu.SemaphoreType.DMA((4,)),
                   pltpu.HBM(block.shape, block.dtype), pltpu.HBM(land.shape, land.dtype),
                   jax.ShapeDtypeStruct((8, LANES), F32)),
        in_specs=(_HBM, _HBM),
        out_specs=(_SEM, _SEM, _HBM, _HBM, pl.BlockSpec(memory_space=pltpu.VMEM)),
        input_output_aliases={0: 2, 1: 3},
        compiler_params=pltpu.CompilerParams(has_side_effects=_EFFECT),
    )(block, land)


def _forward_wait(handles, after, name):
    send_sems, recv_sems, block_thru, land_thru, _ = handles

    def body(x_ref, land_ref, send_sems, recv_sems, after_ref, x_dead, got_ref, buf, local_sems):
        x, y, c = lax.axis_index("x"), lax.axis_index("y"), lax.axis_index("c")
        my, sib_id, sibling = 4 * x + 2 * y + c, 4 * x + 2 * y + 1 - c, (x, y, 1 - c)
        mine = _staged_copy(x_ref, land_ref.at[my], buf, local_sems)
        slots = [(0, my, sib_id)] + [(1 + k, pid, pid + 1 - 2 * c) for k, pid, _ in _peers_same_core(x, y, c)]
        for k, sent, got in slots:
            cp = pltpu.make_async_remote_copy(
                src_ref=land_ref.at[sent], dst_ref=land_ref.at[got], send_sem=send_sems.at[k],
                recv_sem=recv_sems.at[k], device_id=sibling, device_id_type=MESH)
            cp.wait_send()
            cp.wait_recv()
        mine.wait()

    return pl.pallas_call(
        body, name=name,
        out_shape=(pltpu.HBM(block_thru.shape, block_thru.dtype), pltpu.HBM(land_thru.shape, land_thru.dtype)),
        in_specs=(_HBM, _HBM, _SEM, _SEM, pl.BlockSpec(memory_space=pl.ANY)),
        out_specs=(_HBM, _HBM),
        input_output_aliases={0: 0, 1: 1},
        scratch_shapes=[pltpu.VMEM(block_thru.shape, block_thru.dtype), pltpu.SemaphoreType.DMA((2,))],
        compiler_params=pltpu.CompilerParams(has_side_effects=_EFFECT, vmem_limit_bytes=VMEM_LIMIT),
    )(block_thru, land_thru, send_sems, recv_sems, after)[1]


def _after(v, token):
    return v + token[0, 0].astype(v.dtype)


def _dsilu(x, s):
    return s * (1.0 + x * (1.0 - s))


def _log1p(x):
    u = 1.0 + x
    d = u - 1.0
    return jnp.where(d == 0.0, x, jnp.log(u) * (x / jnp.where(d == 0.0, 1.0, d)))


def _softplus_neg(lam):
    return jnp.maximum(-lam, 0.0) + _log1p(jnp.exp(-jnp.abs(lam)))


def _neg_expm1(y, exp_y):
    poly = -y * (1.0 + y * (0.5 + y * (1.0 / 6.0 + y * (1.0 / 24.0))))
    return jnp.where(y > -0.05, poly, 1.0 - exp_y)


def _sigmoid(x):
    return 0.5 * jnp.tanh(0.5 * x) + 0.5


def _shift_dn(cur, prev, k):
    ext = jnp.concatenate([prev, cur], axis=0)
    return pltpu.roll(ext, k, 0)[HALO:, :]


def _shift_up(cur, nxt, k):
    n = cur.shape[0]
    ext = jnp.concatenate([cur, nxt], axis=0)
    return pltpu.roll(ext, n + HALO - k, 0)[:n, :]


def _scan_fwd(a, b, h_prev):
    groups = a.shape[0] // SUBLANES
    a3 = a.reshape(groups, SUBLANES, LANES)
    b3 = b.reshape(groups, SUBLANES, LANES)
    row = lax.broadcasted_iota(jnp.int32, a3.shape, 1)
    k = 1
    while k < SUBLANES:
        a_sh = jnp.where(row >= k, pltpu.roll(a3, k, 1), 1.0)
        b_sh = jnp.where(row >= k, pltpu.roll(b3, k, 1), 0.0)
        b3 = a3 * b_sh + b3
        a3 = a3 * a_sh
        k *= 2
    carry = h_prev[HALO - 1:HALO, :]
    out = []
    for i in range(groups):
        hg = b3[i] + a3[i] * carry
        out.append(hg)
        carry = hg[SUBLANES - 1:SUBLANES, :]
    return jnp.concatenate(out, axis=0)


def _scan_rev(a_next, g, lam_next):
    groups = g.shape[0] // SUBLANES
    a3 = a_next.reshape(groups, SUBLANES, LANES)
    g3 = g.reshape(groups, SUBLANES, LANES)
    row = lax.broadcasted_iota(jnp.int32, a3.shape, 1)
    k = 1
    while k < SUBLANES:
        ok = row < SUBLANES - k
        a_sh = jnp.where(ok, pltpu.roll(a3, SUBLANES - k, 1), 1.0)
        g_sh = jnp.where(ok, pltpu.roll(g3, SUBLANES - k, 1), 0.0)
        g3 = g3 + a3 * g_sh
        a3 = a3 * a_sh
        k *= 2
    carry = lam_next[0:1, :]
    out = [None] * groups
    for i in reversed(range(groups)):
        lg = g3[i] + a3[i] * carry
        out[i] = lg
        carry = lg[0:1, :]
    return jnp.concatenate(out, axis=0)


def _rowsum(v):
    return jnp.sum(v, axis=0, keepdims=True)


def _dot(a, b):
    return jnp.dot(a, b, preferred_element_type=F32)


def _dot_nt(a, b):
    return lax.dot_general(a, b, (((1,), (1,)), ((), ())), preferred_element_type=F32)


def _dot_tn(a, b):
    return lax.dot_general(a, b, (((0,), (0,)), ((), ())), preferred_element_type=F32)


class _MixerWeights:
    def __init__(self, caw_ref, sw_ref, sb_ref, lcw_ref, lcb_ref, wa_ref, wx_ref, ba_ref, bx_ref, lam_ref):
        self.caw = [caw_ref[j:j + 1, :] for j in range(3)]
        self.lcw = [lcw_ref[j:j + 1, :] for j in range(4)]
        self.lcb = lcb_ref[...]
        row = lax.broadcasted_iota(jnp.int32, (CHUNK, CHUNK), 0)
        col = lax.broadcasted_iota(jnp.int32, (CHUNK, CHUNK), 1)
        self.tril = col <= row
        self.sw = jnp.where(self.tril, sw_ref[...], 0.0).astype(BF16)
        self.sb = sb_ref[...]
        self.wa = wa_ref[...]
        self.wx = wx_ref[...]
        self.ba = ba_ref[...]
        self.bx = bx_ref[...]
        lam = lam_ref[...]
        self.neg_c_sp = -LRU_C * _softplus_neg(lam)
        self.dsp_dlam = -_sigmoid(-lam)


def _mixer_a(ld, ldp, w, cv=None):
    t = {}
    a_x, a_c = ld(AX), ld(AC)
    t["a_x"], t["a_c"], t["a_b"], t["a_z"] = a_x, a_c, ld(AB), ld(AZ)
    t["ca"] = ca = a_c * a_x
    if cv is None:
        ca_p = ldp(AC) * ldp(AX)
        cv = w.caw[2] * ca + w.caw[1] * _shift_dn(ca, ca_p, 1) + w.caw[0] * _shift_dn(ca, ca_p, 2)
    t["cv"] = cv
    t["sa"] = _sigmoid(t["a_z"])
    t["silu_az"] = t["a_z"] * t["sa"]
    t["y_a"] = t["silu_az"] * t["a_b"] * t["cv"]
    return t


def _mixer_b(ld, w):
    t = {}
    v = ld(SV)
    vc = v - jnp.mean(v, axis=1, keepdims=True)
    t["rstd"] = lax.rsqrt(jnp.mean(vc * vc, axis=1, keepdims=True) + EPS)
    t["vn"] = vc * t["rstd"]
    t["z"] = _dot(w.sw, t["vn"].astype(BF16)) + w.sb
    t["s_u"], t["s_z"] = ld(SU), ld(SZ)
    t["ss"] = _sigmoid(t["s_z"])
    t["silu_sz"] = t["s_z"] * t["ss"]
    t["y_s"] = t["silu_sz"] * t["s_u"] * t["z"]
    return t


def _mixer_c(ld, ldp, w, backward, xc=None):
    t = {}
    t["r_x"] = r_x = ld(RX)
    if xc is None:
        r_xp = ldp(RX)
        xc = (w.lcb + w.lcw[0] * _shift_dn(r_x, r_xp, 3) + w.lcw[1] * _shift_dn(r_x, r_xp, 2)
              + w.lcw[2] * _shift_dn(r_x, r_xp, 1) + w.lcw[3] * r_x)
    t["xc"] = xc
    xcb = xc.astype(BF16)
    t["r"] = _sigmoid(_dot(xcb, w.wa) + w.ba)
    t["i"] = _sigmoid(_dot(xcb, w.wx) + w.bx)
    la = t["r"] * w.neg_c_sp
    t["a"] = jnp.exp(la)
    t["em"] = _neg_expm1(2.0 * la, t["a"] * t["a"])
    if backward:
        t["inv_mult"] = lax.rsqrt(t["em"])
        t["mult"] = t["em"] * t["inv_mult"]
    else:
        t["mult"] = jnp.sqrt(t["em"])
    t["b"] = t["mult"] * (t["i"] * xc)
    t["r_z"] = ld(RZ)
    t["sr"] = _sigmoid(t["r_z"])
    t["silu_rz"] = t["r_z"] * t["sr"]
    return t


def _mixer_pre_scan(ld, ldp, w, backward, cv=None, xc=None):
    t = {**_mixer_a(ld, ldp, w, cv), **_mixer_b(ld, w), **_mixer_c(ld, ldp, w, backward, xc)}
    t["ga"], t["gs"], t["gr"] = _sigmoid(ld(GA)), _sigmoid(ld(GS)), _sigmoid(ld(GR))
    return t


def _weight_specs(n_cb_axis):
    def at(fn):
        return lambda *g: fn(g[n_cb_axis])
    return [
        pl.BlockSpec((3, LANES), at(lambda cb: (0, cb))),
        pl.BlockSpec((None, CHUNK, CHUNK), at(lambda cb: (cb, 0, 0))),
        pl.BlockSpec((None, CHUNK, LANES), at(lambda cb: (cb, 0, 0))),
        pl.BlockSpec((4, LANES), at(lambda cb: (0, cb))),
        pl.BlockSpec((1, LANES), at(lambda cb: (0, cb))),
        pl.BlockSpec((None, LANES, LANES), at(lambda cb: (cb, 0, 0))),
        pl.BlockSpec((None, LANES, LANES), at(lambda cb: (cb, 0, 0))),
        pl.BlockSpec((1, LANES), at(lambda cb: (0, cb))),
        pl.BlockSpec((1, LANES), at(lambda cb: (0, cb))),
        pl.BlockSpec((1, LANES), at(lambda cb: (0, cb))),
    ]


def _chunk_loaders(p_ref, c):
    r0 = pl.multiple_of(c * CHUNK, CHUNK)
    rp = pl.multiple_of(jnp.maximum(c * CHUNK - HALO, 0), HALO)

    def ld(j):
        return p_ref[j, pl.ds(r0, CHUNK), :].astype(F32)

    def ldp(j):
        return jnp.where(c > 0, p_ref[j, pl.ds(rp, HALO), :].astype(F32), 0.0)

    return r0, rp, ld, ldp


def _mixer_fwd(proj, mw):
    _, nb, s, _ = proj.shape
    n_chunks = s // CHUNK

    def body(p_ref, *refs):
        w = _MixerWeights(*refs[:10])
        merged_ref, hs_ref, cv_ref, xc_ref = refs[10:]

        def chunk(c, h_prev):
            r0, _, ld, ldp = _chunk_loaders(p_ref, c)
            t = _mixer_pre_scan(ld, ldp, w, False)
            h = _scan_fwd(t["a"], t["b"], h_prev)
            y_r = t["silu_rz"] * h
            merged = t["ga"] * t["y_a"] + t["gs"] * t["y_s"] + t["gr"] * y_r
            merged_ref[pl.ds(r0, CHUNK), :] = merged.astype(BF16)
            hs_ref[pl.ds(r0, CHUNK), :] = h
            cv_ref[pl.ds(r0, CHUNK), :] = t["cv"].astype(BF16)
            xc_ref[pl.ds(r0, CHUNK), :] = t["xc"].astype(BF16)
            return h[CHUNK - HALO:, :]

        def group(i, carry):
            for k in range(FWD_CHUNKS_PER_TRIP):
                carry = chunk(FWD_CHUNKS_PER_TRIP * i + k, carry)
            return carry

        assert n_chunks % FWD_CHUNKS_PER_TRIP == 0
        lax.fori_loop(0, n_chunks // FWD_CHUNKS_PER_TRIP, group, jnp.zeros((HALO, LANES), F32))

    slab = pl.BlockSpec((None, s, LANES), lambda cb, b: (b, 0, cb))
    half = jax.ShapeDtypeStruct((nb, s, D), BF16)
    return pl.pallas_call(
        body, name="mixer_fwd", grid=(D // LANES, nb),
        in_specs=[pl.BlockSpec((N_SEG, None, s, LANES), lambda cb, b: (0, b, 0, cb))] + _weight_specs(0),
        out_specs=[slab, slab, slab, slab],
        out_shape=[half, jax.ShapeDtypeStruct((nb, s, D), F32), half, half],
        compiler_params=_params(("arbitrary", "arbitrary")),
    )(proj, *mw)


def _mixer_bwd(proj, dmerged, hs, cv, xc, mw):
    _, nb, s, _ = proj.shape
    n_chunks = s // CHUNK

    def body(p_ref, dm_ref, hs_ref, cv_ref, xc_ref, *refs):
        w = _MixerWeights(*refs[:10])
        dp_ref, g_caw, g_sw, g_sb, g_lcw, g_vec, g_wa, g_wx = refs[10:]

        @pl.when(pl.program_id(1) == 0)
        def _():
            for ref in (g_caw, g_sw, g_sb, g_lcw, g_vec, g_wa, g_wx):
                ref[...] = jnp.zeros_like(ref)

        def chunk(i, carry):
            dcv_n, dxc_n, lam_n, a_n = carry
            c = n_chunks - 1 - i
            r0, rp, ld, ldp = _chunk_loaders(p_ref, c)
            t = _mixer_pre_scan(ld, ldp, w, True, cv_ref[pl.ds(r0, CHUNK), :].astype(F32),
                                xc_ref[pl.ds(r0, CHUNK), :].astype(F32))
            h = hs_ref[pl.ds(r0, CHUNK), :]
            h_p = jnp.where(c > 0, hs_ref[pl.ds(rp, HALO), :], 0.0)
            h_prev = _shift_dn(h, h_p, 1)
            dm = dm_ref[pl.ds(r0, CHUNK), :].astype(F32)
            y_r = t["silu_rz"] * h

            def out(j, val):
                dp_ref[j, pl.ds(r0, CHUNK), :] = val.astype(BF16)

            ga, gs, gr = t["ga"], t["gs"], t["gr"]
            out(GA, dm * t["y_a"] * ga * (1.0 - ga))
            out(GS, dm * t["y_s"] * gs * (1.0 - gs))
            out(GR, dm * y_r * gr * (1.0 - gr))

            dy_a = dm * ga
            out(AZ, dy_a * t["a_b"] * t["cv"] * _dsilu(t["a_z"], t["sa"]))
            out(AB, dy_a * t["silu_az"] * t["cv"])
            dcv = dy_a * t["silu_az"] * t["a_b"]
            dcv1, dcv2 = _shift_up(dcv, dcv_n, 1), _shift_up(dcv, dcv_n, 2)
            dca = w.caw[2] * dcv + w.caw[1] * dcv1 + w.caw[0] * dcv2
            out(AC, dca * t["a_x"])
            out(AX, dca * t["a_c"])
            g_caw[2:3, :] += _rowsum(dcv * t["ca"])
            g_caw[1:2, :] += _rowsum(dcv1 * t["ca"])
            g_caw[0:1, :] += _rowsum(dcv2 * t["ca"])

            dy_s = dm * gs
            out(SZ, dy_s * t["s_u"] * t["z"] * _dsilu(t["s_z"], t["ss"]))
            out(SU, dy_s * t["silu_sz"] * t["z"])
            dz = dy_s * t["silu_sz"] * t["s_u"]
            dzb = dz.astype(BF16)
            g_sb[...] += jnp.broadcast_to(jnp.sum(dz, axis=1, keepdims=True), (CHUNK, LANES))
            g_sw[...] += _dot_nt(dzb, t["vn"].astype(BF16))
            dvn = _dot_tn(w.sw, dzb)
            vn = t["vn"]
            out(SV, t["rstd"] * (dvn - jnp.mean(dvn, axis=1, keepdims=True)
                                 - vn * jnp.mean(dvn * vn, axis=1, keepdims=True)))

            dy_r = dm * gr
            out(RZ, dy_r * h * _dsilu(t["r_z"], t["sr"]))
            lam = _scan_rev(_shift_up(t["a"], a_n, 1), dy_r * t["silu_rz"], lam_n)
            a, r, ig, xc, mult = t["a"], t["r"], t["i"], t["xc"], t["mult"]
            d_i = lam * mult * xc
            d_mult = lam * ig * xc
            dxc = lam * mult * ig
            dla = lam * h_prev * a - d_mult * ((1.0 - t["em"]) * t["inv_mult"])
            g_vec[3:4, :] += _rowsum(dla * r) * (-LRU_C * w.dsp_dlam)
            dpr = (dla * w.neg_c_sp) * r * (1.0 - r)
            dpi = d_i * ig * (1.0 - ig)
            dprb, dpib, xcb = dpr.astype(BF16), dpi.astype(BF16), xc.astype(BF16)
            g_wa[...] += _dot_tn(xcb, dprb)
            g_wx[...] += _dot_tn(xcb, dpib)
            g_vec[1:2, :] += _rowsum(dpr)
            g_vec[2:3, :] += _rowsum(dpi)
            dxc = dxc + _dot_nt(dprb, w.wa) + _dot_nt(dpib, w.wx)
            g_vec[0:1, :] += _rowsum(dxc)
            dxcs = [_shift_up(dxc, dxc_n, 3), _shift_up(dxc, dxc_n, 2), _shift_up(dxc, dxc_n, 1), dxc]
            out(RX, w.lcw[3] * dxcs[3] + w.lcw[2] * dxcs[2] + w.lcw[1] * dxcs[1] + w.lcw[0] * dxcs[0])
            for j in range(4):
                g_lcw[j:j + 1, :] += _rowsum(dxcs[j] * t["r_x"])
            return dcv[:HALO, :], dxc[:HALO, :], lam[:HALO, :], a[:HALO, :]

        zero = jnp.zeros((HALO, LANES), F32)
        def group(i, carry):
            for k in range(BWD_CHUNKS_PER_TRIP):
                carry = chunk(BWD_CHUNKS_PER_TRIP * i + k, carry)
            return carry

        assert n_chunks % BWD_CHUNKS_PER_TRIP == 0
        lax.fori_loop(0, n_chunks // BWD_CHUNKS_PER_TRIP, group, (zero, zero, zero, zero))

        @pl.when(pl.program_id(1) == nb - 1)
        def _():
            g_sw[...] = jnp.where(w.tril, g_sw[...], 0.0)

    slab = lambda dt: pl.BlockSpec((None, s, LANES), lambda cb, b: (b, 0, cb))
    seg = pl.BlockSpec((N_SEG, None, s, LANES), lambda cb, b: (0, b, 0, cb))
    rows = lambda n: pl.BlockSpec((n, LANES), lambda cb, b: (0, cb))
    sq = pl.BlockSpec((None, LANES, LANES), lambda cb, b: (cb, 0, 0))
    n_cb = D // LANES
    return pl.pallas_call(
        body, name="mixer_bwd", grid=(n_cb, nb),
        in_specs=[seg, slab(BF16), slab(F32), slab(BF16), slab(BF16)] + _weight_specs(0),
        out_specs=[seg, rows(3), sq, sq, rows(4), rows(8), sq, sq],
        out_shape=[
            jax.ShapeDtypeStruct(proj.shape, BF16),
            jax.ShapeDtypeStruct((3, D), F32),
            jax.ShapeDtypeStruct((n_cb, CHUNK, CHUNK), F32),
            jax.ShapeDtypeStruct((n_cb, CHUNK, LANES), F32),
            jax.ShapeDtypeStruct((4, D), F32),
            jax.ShapeDtypeStruct((8, D), F32),
            jax.ShapeDtypeStruct((n_cb, LANES, LANES), F32),
            jax.ShapeDtypeStruct((n_cb, LANES, LANES), F32),
        ],
        compiler_params=_params(("arbitrary", "arbitrary")),
    )(proj, dmerged, hs, cv, xc, *mw)


def _row_tile(s, want):
    return want if s % want == 0 else s


def _norm_mod(x, gain, shift, scale):
    nb, s, _ = x.shape
    tm = _row_tile(s, 512)

    def body(x_ref, g_ref, sh_ref, sc_ref, h_ref):
        xv = x_ref[...]
        r = lax.rsqrt(jnp.mean(xv * xv, axis=1, keepdims=True) + EPS)
        h_ref[...] = ((xv * r) * g_ref[...] * (1.0 + sc_ref[...]) + sh_ref[...]).astype(BF16)

    tile = pl.BlockSpec((None, tm, D), lambda b, m: (b, m, 0))
    vec = pl.BlockSpec((None, 1, D), lambda b, m: (b, 0, 0))
    return pl.pallas_call(
        body, name="norm_mod", grid=(nb, s // tm),
        in_specs=[tile, pl.BlockSpec((1, D), lambda b, m: (0, 0)), vec, vec],
        out_specs=tile, out_shape=jax.ShapeDtypeStruct(x.shape, BF16),
        compiler_params=_params(("arbitrary", "arbitrary")),
    )(x, gain, shift, scale)


def _in_proj(h, wg, dep):
    nb, s, _ = h.shape

    def body(h_ref, w0_ref, w1_ref, dep_ref, o_ref):
        hv = h_ref[...]
        o_ref[:, :UNIT] = _dot(hv, w0_ref[...]).astype(BF16)
        o_ref[:, UNIT:] = _dot(hv, w1_ref[...]).astype(BF16)

    def unit(k):
        return pl.BlockSpec((None, D, UNIT),
                            lambda b, j: ((2 * j + k) // UNITS_PER_DEV, 0, (2 * j + k) % UNITS_PER_DEV))

    return pl.pallas_call(
        body, name="in_proj", grid=(nb, N_SEG),
        in_specs=[pl.BlockSpec((None, s, D), lambda b, j: (b, 0, 0)), unit(0), unit(1),
                  pl.BlockSpec((8, LANES), lambda b, j: (0, 0))],
        out_specs=pl.BlockSpec((None, None, s, D), lambda b, j: (j, b, 0, 0)),
        out_shape=jax.ShapeDtypeStruct((N_SEG, nb, s, D), BF16),
        compiler_params=_params(("arbitrary", "arbitrary")),
    )(h, wg, wg, dep)


def _out_proj(x, merged, wout, gate):
    nb, s, _ = x.shape
    tm = _row_tile(s, 512)

    def body(x_ref, m_ref, w_ref, g_ref, o_ref):
        o_ref[...] = x_ref[...] + g_ref[...] * _dot(m_ref[...], w_ref[...])

    tile = pl.BlockSpec((None, tm, D), lambda b, m: (b, m, 0))
    return pl.pallas_call(
        body, name="out_proj", grid=(nb, s // tm),
        in_specs=[tile, tile, pl.BlockSpec((D, D), lambda b, m: (0, 0)),
                  pl.BlockSpec((None, 1, D), lambda b, m: (b, 0, 0))],
        out_specs=tile, out_shape=jax.ShapeDtypeStruct(x.shape, F32),
        compiler_params=_params(("arbitrary", "arbitrary")),
    )(x, merged, wout, gate)


def _out_proj_norm(x, merged, wout, gate, gain, shift, scale):
    nb, s, _ = x.shape
    tm = _row_tile(s, 512)

    def body(x_ref, m_ref, w_ref, g_ref, gn_ref, sh_ref, sc_ref, o_ref, h_ref):
        xv = x_ref[...] + g_ref[...] * _dot(m_ref[...], w_ref[...])
        o_ref[...] = xv
        r = lax.rsqrt(jnp.mean(xv * xv, axis=1, keepdims=True) + EPS)
        h_ref[...] = ((xv * r) * gn_ref[...] * (1.0 + sc_ref[...]) + sh_ref[...]).astype(BF16)

    tile = pl.BlockSpec((None, tm, D), lambda b, m: (b, m, 0))
    vec = pl.BlockSpec((None, 1, D), lambda b, m: (b, 0, 0))
    return pl.pallas_call(
        body, name="out_proj_norm", grid=(nb, s // tm),
        in_specs=[tile, tile, pl.BlockSpec((D, D), lambda b, m: (0, 0)), vec,
                  pl.BlockSpec((1, D), lambda b, m: (0, 0)), vec, vec],
        out_specs=[tile, tile],
        out_shape=[jax.ShapeDtypeStruct(x.shape, F32), jax.ShapeDtypeStruct(x.shape, BF16)],
        compiler_params=_params(("arbitrary", "arbitrary")),
    )(x, merged, wout, gate, gain, shift, scale)


def _loss_head(x, gain, target):
    nb, s, _ = x.shape
    tm = _row_tile(s, 512)

    def body(x_ref, g_ref, t_ref, loss_ref, dx_ref, dg_ref):
        first = (pl.program_id(0) == 0) & (pl.program_id(1) == 0)
        last = (pl.program_id(0) == nb - 1) & (pl.program_id(1) == s // tm - 1)

        @pl.when(first)
        def _():
            loss_ref[...] = jnp.zeros_like(loss_ref)
            dg_ref[...] = jnp.zeros_like(dg_ref)

        xv = x_ref[...]
        r = lax.rsqrt(jnp.mean(xv * xv, axis=1, keepdims=True) + EPS)
        xn = xv * r
        g = g_ref[...]
        e = xn * g - t_ref[...]
        loss_ref[...] += _rowsum(e * e) * (0.5 / D)
        dy = e * (1.0 / D)
        dg_ref[...] += _rowsum(dy * xn)
        dxn = dy * g
        dx_ref[...] = r * (dxn - xn * jnp.mean(dxn * xn, axis=1, keepdims=True))

        @pl.when(last)
        def _():
            loss_ref[...] = jnp.broadcast_to(jnp.sum(loss_ref[...], axis=1, keepdims=True), (1, D))

    tile = pl.BlockSpec((None, tm, D), lambda b, m: (b, m, 0))
    vec = pl.BlockSpec((1, D), lambda b, m: (0, 0))
    return pl.pallas_call(
        body, name="loss_head", grid=(nb, s // tm),
        in_specs=[tile, vec, tile], out_specs=[vec, tile, vec],
        out_shape=[jax.ShapeDtypeStruct((1, D), F32), jax.ShapeDtypeStruct(x.shape, F32),
                   jax.ShapeDtypeStruct((1, D), F32)],
        compiler_params=_params(("arbitrary", "arbitrary")),
    )(x, gain, target)


def _out_proj_bwd(dxo, merged, wout, gate):
    nb, s, _ = dxo.shape
    tm = _row_tile(s, 512)

    def body(d_ref, m_ref, w_ref, g_ref, dm_ref, gw_ref, dg_ref):
        @pl.when((pl.program_id(0) == 0) & (pl.program_id(1) == 0))
        def _():
            gw_ref[...] = jnp.zeros_like(gw_ref)

        @pl.when(pl.program_id(1) == 0)
        def _():
            dg_ref[...] = jnp.zeros_like(dg_ref)

        d = d_ref[...]
        m = m_ref[...]
        wv = w_ref[...]
        dg_ref[...] += _rowsum(d * _dot(m, wv))
        dout = (d * g_ref[...]).astype(BF16)
        dm_ref[...] = _dot_nt(dout, wv).astype(BF16)
        gw_ref[...] += _dot_tn(m, dout)

    tile = pl.BlockSpec((None, tm, D), lambda b, m: (b, m, 0))
    vec = pl.BlockSpec((None, 1, D), lambda b, m: (b, 0, 0))
    full = pl.BlockSpec((D, D), lambda b, m: (0, 0))
    return pl.pallas_call(
        body, name="out_proj_bwd", grid=(nb, s // tm),
        in_specs=[tile, tile, full, vec], out_specs=[tile, full, vec],
        out_shape=[jax.ShapeDtypeStruct(dxo.shape, BF16), jax.ShapeDtypeStruct((D, D), F32),
                   jax.ShapeDtypeStruct((nb, 1, D), F32)],
        compiler_params=_params(("arbitrary", "arbitrary")),
    )(dxo, merged, wout, gate)


def _in_proj_bwd_h(dproj, wg, dep):
    _, nb, s, _ = dproj.shape
    tm = _row_tile(s, 1024)

    def body(dp0_ref, dp1_ref, w0_ref, w1_ref, w2_ref, w3_ref, dep_ref, dh_ref):
        j = pl.program_id(2)
        part = (_dot_nt(dp0_ref[...], jnp.concatenate([w0_ref[...], w1_ref[...]], axis=1))
                + _dot_nt(dp1_ref[...], jnp.concatenate([w2_ref[...], w3_ref[...]], axis=1)))

        @pl.when(j == 0)
        def _():
            dh_ref[...] = part

        @pl.when(j > 0)
        def _():
            dh_ref[...] += part

    def seg(k):
        return pl.BlockSpec((None, None, tm, D), lambda b, m, j: (2 * j + k, b, m, 0))

    def unit(k):
        return pl.BlockSpec((None, D, UNIT),
                            lambda b, m, j: ((4 * j + k) // UNITS_PER_DEV, 0, (4 * j + k) % UNITS_PER_DEV))

    return pl.pallas_call(
        body, name="in_proj_bwd_h", grid=(nb, s // tm, N_SEG // 2),
        in_specs=[seg(0), seg(1), unit(0), unit(1), unit(2), unit(3),
                  pl.BlockSpec((8, LANES), lambda b, m, j: (0, 0))],
        out_specs=pl.BlockSpec((None, tm, D), lambda b, m, j: (b, m, 0)),
        out_shape=jax.ShapeDtypeStruct((nb, s, D), F32),
        compiler_params=_params(("arbitrary", "arbitrary", "arbitrary")),
    )(dproj, dproj, wg, wg, wg, wg, dep)


def _norm_mod_bwd(dh, x, dxo, gain, scale):
    nb, s, _ = x.shape
    tm = _row_tile(s, 512)

    def body(dh_ref, x_ref, dxo_ref, g_ref, sc_ref, dx_ref, dsh_ref, dsc_ref, dg_ref):
        b, m = pl.program_id(0), pl.program_id(1)

        @pl.when((b == 0) & (m == 0))
        def _():
            dg_ref[...] = jnp.zeros_like(dg_ref)

        @pl.when(m == 0)
        def _():
            dsh_ref[...] = jnp.zeros_like(dsh_ref)
            dsc_ref[...] = jnp.zeros_like(dsc_ref)

        dh = dh_ref[...]
        xv = x_ref[...]
        r = lax.rsqrt(jnp.mean(xv * xv, axis=1, keepdims=True) + EPS)
        xn = xv * r
        g = g_ref[...]
        one_sc = 1.0 + sc_ref[...]
        dsh_ref[...] += _rowsum(dh)
        dsc_ref[...] += _rowsum(dh * (xn * g))
        dg_ref[...] += _rowsum(dh * one_sc * xn)
        dxn = dh * (g * one_sc)
        dx_ref[...] = dxo_ref[...] + r * (dxn - xn * jnp.mean(dxn * xn, axis=1, keepdims=True))

    tile = pl.BlockSpec((None, tm, D), lambda b, m: (b, m, 0))
    vec = pl.BlockSpec((None, 1, D), lambda b, m: (b, 0, 0))
    one = pl.BlockSpec((1, D), lambda b, m: (0, 0))
    return pl.pallas_call(
        body, name="norm_mod_bwd", grid=(nb, s // tm),
        in_specs=[tile, tile, tile, one, vec],
        out_specs=[tile, vec, vec, one],
        out_shape=[jax.ShapeDtypeStruct(x.shape, F32), jax.ShapeDtypeStruct((nb, 1, D), F32),
                   jax.ShapeDtypeStruct((nb, 1, D), F32), jax.ShapeDtypeStruct((1, D), F32)],
        compiler_params=_params(("arbitrary", "arbitrary")),
    )(dh, x, dxo, gain, scale)


def _in_proj_bwd_w(h, dproj, dep):
    nb, s, _ = h.shape
    tm = _row_tile(s, 2048)
    n_m = s // tm

    def body(h_ref, dp_ref, dep_ref, o_ref, acc_ref):
        b, m = pl.program_id(1), pl.program_id(2)

        @pl.when((b == 0) & (m == 0))
        def _():
            acc_ref[...] = jnp.zeros_like(acc_ref)

        acc_ref[...] += _dot_tn(h_ref[...], dp_ref[...])

        @pl.when((b == nb - 1) & (m == n_m - 1))
        def _():
            o_ref[0] = acc_ref[:, :UNIT].astype(BF16)
            o_ref[1] = acc_ref[:, UNIT:].astype(BF16)

    return pl.pallas_call(
        body, name="in_proj_bwd_w", grid=(N_SEG, nb, n_m),
        in_specs=[pl.BlockSpec((None, tm, D), lambda j, b, m: (b, m, 0)),
                  pl.BlockSpec((None, None, tm, D), lambda j, b, m: (j, b, m, 0)),
                  pl.BlockSpec((8, LANES), lambda j, b, m: (0, 0))],
        out_specs=pl.BlockSpec((2, D, UNIT), lambda j, b, m: (j, 0, 0)),
        out_shape=jax.ShapeDtypeStruct((2 * N_SEG, D, UNIT), BF16),
        scratch_shapes=[pltpu.VMEM((D, D), F32)],
        compiler_params=_params(("arbitrary", "arbitrary", "arbitrary")),
    )(h, dproj, dep)


def _mod_proj(c_all, w_mod, b_mod_mine):
    nl, _, ncol = w_mod.shape
    nbg = c_all.shape[0]

    def body(c_ref, w_ref, b_ref, o_ref):
        cv = c_ref[...]
        o_ref[...] = jnp.dot(cv * jax.nn.sigmoid(cv), w_ref[...], preferred_element_type=F32,
                             precision=lax.Precision.HIGHEST) + b_ref[...]

    return pl.pallas_call(
        body, name="mod_proj", grid=(nl,),
        in_specs=[pl.BlockSpec((nbg, D), lambda l: (0, 0)), pl.BlockSpec((None, D, ncol), lambda l: (l, 0, 0)),
                  pl.BlockSpec((None, 1, ncol), lambda l: (l, 0, 0))],
        out_specs=pl.BlockSpec((None, nbg, ncol), lambda l: (l, 0, 0)),
        out_shape=jax.ShapeDtypeStruct((nl, nbg, ncol), F32),
        compiler_params=_params(("arbitrary",)),
    )(c_all, w_mod, b_mod_mine)


def _mod_grad(c_all, dmod_all, dmod_mine):
    nl, nbg, ncol = dmod_mine.shape

    def body(c_ref, da_ref, dm_ref, gw_ref, gb_ref):
        cv = c_ref[...]
        gw_ref[...] = lax.dot_general(cv * jax.nn.sigmoid(cv), dm_ref[...], (((0,), (0,)), ((), ())),
                                      preferred_element_type=F32, precision=lax.Precision.HIGHEST)
        gb_ref[...] = _rowsum(da_ref[...])

    return pl.pallas_call(
        body, name="mod_grad", grid=(nl,),
        in_specs=[pl.BlockSpec((nbg, D), lambda l: (0, 0)), pl.BlockSpec((None, nbg, 3 * D), lambda l: (l, 0, 0)),
                  pl.BlockSpec((None, nbg, ncol), lambda l: (l, 0, 0))],
        out_specs=[pl.BlockSpec((None, D, ncol), lambda l: (l, 0, 0)),
                   pl.BlockSpec((None, 1, 3 * D), lambda l: (l, 0, 0))],
        out_shape=[jax.ShapeDtypeStruct((nl, D, ncol), F32), jax.ShapeDtypeStruct((nl, 1, 3 * D), F32)],
        compiler_params=_params(("arbitrary",)),
    )(c_all, dmod_all, dmod_mine)


def _adamw(parts, w, m, v, name, layer=None, prev=None):
    n_parts, n_u, n_r, cu = parts.shape
    assert w.shape[-2:] == (n_r, n_u * cu), (parts.shape, w.shape)
    tr = n_r
    for cand in (512, 256, 128):
        if n_r > cand and n_r % cand == 0:
            tr = cand
            break
    n_prev = 0 if prev is None else 4

    def body(p_ref, w_ref, m_ref, v_ref, *rest):
        g_ref, d_ref, nm_ref, nv_ref = rest[n_prev:]
        g = p_ref[0].astype(F32)
        for k in range(1, n_parts):
            g = g + p_ref[k].astype(F32)
        m2 = ADAM_B1 * m_ref[...] + (1.0 - ADAM_B1) * g
        v2 = ADAM_B2 * v_ref[...] + (1.0 - ADAM_B2) * (g * g)
        m_hat = m2 / (1.0 - ADAM_B1 ** ADAM_STEP)
        v_hat = v2 / (1.0 - ADAM_B2 ** ADAM_STEP)
        g_ref[...] = g
        d_ref[...] = -ADAM_LR * (m_hat / (jnp.sqrt(v_hat) + ADAM_EPS) + ADAM_WD * w_ref[...])
        nm_ref[...] = m2
        nv_ref[...] = v2

    if layer is None:
        tile = pl.BlockSpec((tr, cu), lambda u, i: (i, u))
    else:
        tile = pl.BlockSpec((None, tr, cu), lambda u, i: (layer, i, u))
    shp = jax.ShapeDtypeStruct(w.shape, F32)
    return pl.pallas_call(
        body, name=name, grid=(n_u, n_r // tr),
        in_specs=[pl.BlockSpec((n_parts, None, tr, cu), lambda u, i: (0, u, i, 0)), tile, tile, tile]
        + [pl.BlockSpec(memory_space=pl.ANY)] * n_prev,
        out_specs=[tile, tile, tile, tile], out_shape=[shp, shp, shp, shp],
        input_output_aliases={4 + k: k for k in range(n_prev)},
        compiler_params=_params(("arbitrary", "arbitrary")),
    )(parts, w, m, v, *(prev or ()))


def _gathered_cols(g, inner):
    k = len(inner)
    perm = tuple(range(1, k + 1)) + (0, k + 1)
    t = jnp.transpose(g, perm)
    return t.reshape(tuple(inner) + (g.shape[0] * g.shape[-1],))


def _pair_blocks(wh):
    z = jnp.zeros((8, 64, 64), wh.dtype)
    w2 = wh.reshape(8, 2, 64, 64)
    top = jnp.concatenate([w2[:, 0], z], axis=2)
    bot = jnp.concatenate([z, w2[:, 1]], axis=2)
    return jnp.concatenate([top, bot], axis=1).astype(BF16)


def _unpair_blocks(g):
    return jnp.stack([g[:, :64, :64], g[:, 64:, 64:]], axis=1).reshape(16, 64, 64)


FLAT_ROWS = 512


def _pack_rows(arrays, lead=0):
    parts = [a.reshape(a.shape[:lead] + (-1, LANES)) for a in arrays]
    rows = jnp.concatenate(parts, axis=lead)
    pad = [(0, 0)] * rows.ndim
    pad[lead] = (0, (-rows.shape[lead]) % FLAT_ROWS)
    return jnp.pad(rows, pad)


def kernel(x, c, norm_gain, w_mod, b_mod, w_in, w_out, conv_a_w, sgu_w, sgu_b, lru_conv_w, lru_conv_b, lru_wa, lru_ba, lru_wx, lru_bx, lru_lambda, final_gain, loss_target, m_norm_gain, m_w_mod, m_b_mod, m_w_in, m_w_out, m_conv_a_w, m_sgu_w, m_sgu_b, m_lru_conv_w, m_lru_conv_b, m_lru_wa, m_lru_ba, m_lru_wx, m_lru_bx, m_lru_lambda, m_final_gain, v_norm_gain, v_w_mod, v_b_mod, v_w_in, v_w_out, v_conv_a_w, v_sgu_w, v_sgu_b, v_lru_conv_w, v_lru_conv_b, v_lru_wa, v_lru_ba, v_lru_wx, v_lru_bx, v_lru_lambda, v_final_gain):
    nl = w_in.shape[0]
    nb, s, _ = x.shape
    me = _my_index()
    mod_cols = w_mod.shape[2]


    small = jnp.concatenate([c.reshape(-1, LANES), conv_a_w.reshape(-1, LANES), lru_conv_w.reshape(-1, LANES)])
    n_c, n_ca = nb * D // LANES, nl * 3
    n_small = small.shape[0]
    small = jnp.pad(small, ((0, (-n_small) % 8), (0, 0)))
    small_all, _ = _all_gather(small, "gather_small")
    c_all = small_all[:, :n_c].reshape(N_DEV * nb, D)

    w_in_b = [w_in[0].astype(BF16)] + list(w_in[1:].astype(BF16))

    def start_w_in(l, dep):
        return _split_start(w_in_b[l], _peers_same_core, False, "gather_w_in_start", dep)

    conv_a_full = _gathered_cols(small_all[:, n_c:n_c + n_ca].reshape(N_DEV, nl, 3, LANES), (nl, 3))
    lru_conv_full = _gathered_cols(small_all[:, n_c + n_ca:n_small].reshape(N_DEV, nl, 4, LANES), (nl, 4))

    def gathered_w_in(started, after):
        block, land = _split_wait(started, after, _peers_same_core, False, False, "gather_w_in_wait")
        return _gather_finish(block, land, "gather_w_in_finish")

    sgu_b_lanes = jnp.broadcast_to(sgu_b[..., None], sgu_b.shape + (LANES,))
    mws = []
    for l in range(nl):
        mws.append((conv_a_full[l], sgu_w[l], sgu_b_lanes[l], lru_conv_full[l], lru_conv_b[l][None, :],
                    _pair_blocks(lru_wa[l]), _pair_blocks(lru_wx[l]), lru_ba[l].reshape(1, D),
                    lru_bx[l].reshape(1, D), lru_lambda[l][None, :]))

    rep_names = ["sgu_w", "sgu_b", "lru_conv_b", "lru_wa", "lru_ba", "lru_wx", "lru_bx", "lru_lambda"]
    rep_w = dict(sgu_w=sgu_w, sgu_b=sgu_b, lru_conv_b=lru_conv_b, lru_wa=lru_wa, lru_ba=lru_ba,
                 lru_wx=lru_wx, lru_bx=lru_bx, lru_lambda=lru_lambda)
    rep_m = dict(sgu_w=m_sgu_w, sgu_b=m_sgu_b, lru_conv_b=m_lru_conv_b, lru_wa=m_lru_wa,
                 lru_ba=m_lru_ba, lru_wx=m_lru_wx, lru_bx=m_lru_bx, lru_lambda=m_lru_lambda)
    rep_v = dict(sgu_w=v_sgu_w, sgu_b=v_sgu_b, lru_conv_b=v_lru_conv_b, lru_wa=v_lru_wa,
                 lru_ba=v_lru_ba, lru_wx=v_lru_wx, lru_bx=v_lru_bx, lru_lambda=v_lru_lambda)

    rep_w_all, rep_m_all, rep_v_all = [_pack_rows([src[n] for n in rep_names], lead=1)
                                       for src in (rep_w, rep_m, rep_v)]
    early = [rep_w_all, rep_m_all, rep_v_all] + w_in_b[1:] + [a for mw in mws for a in mw]

    b_mod_mine = lax.dynamic_slice_in_dim(b_mod, me * mod_cols, mod_cols, axis=1)[:, None, :]
    mod_mine = _mod_proj(c_all, w_mod, b_mod_mine)
    mod_all, mod_token = _all_gather(mod_mine.reshape(nl * N_DEV * nb, mod_cols), "gather_mod", dep=w_in_b[0])
    ici = {0: start_w_in(0, mod_token)}
    mod_full = _gathered_cols(mod_all.reshape(N_DEV, nl, N_DEV * nb, mod_cols), (nl, N_DEV * nb))
    mod_loc = lax.dynamic_slice_in_dim(mod_full, me * nb, nb, axis=1)
    shift, scale, gate = [mod_loc[:, :, j * D:(j + 1) * D][:, :, None, :] for j in range(3)]

    xs, hs_bf, projs, mergeds, states, wg = [], [], [], [], [], []
    xl = x
    d2d = {}
    wo_started = _split_start(w_out.astype(BF16).reshape(nl * (D // N_DEV), D), _peers_all, False,
                              "gather_w_out_start", ici[0][4])
    wo = None
    h = _norm_mod(xl, _after(norm_gain[0][None, :], wo_started[4]), shift[0], scale[0])
    for l in range(nl):
        if l == 0:
            wg_l, token = gathered_w_in(ici[0], [h] + early)
            ici[1] = start_w_in(1, token)
            dep = ici[1][4]
        else:
            wg_l = _forward_wait(d2d[l], h, "gather_w_in_d2d_wait")
            dep = d2d[l][4]
        wg.append(wg_l)
        proj = _in_proj(h, wg_l, dep)
        merged, *st = _mixer_fwd(proj, mws[l])
        xs.append(xl), hs_bf.append(h), projs.append(proj), mergeds.append(merged), states.append(st)
        gate_l = gate[l]
        if l + 1 < nl:
            block, land = _split_wait(ici[l + 1], merged, _peers_same_core, False, False, "gather_w_in_wait")
            d2d[l + 1] = _forward_start(block, land, "gather_w_in_d2d_start")
            gate_l = _after(gate_l, d2d[l + 1][4])
            if l + 2 < nl:
                ici[l + 2] = start_w_in(l + 2, d2d[l + 1][4])
                gate_l = _after(gate_l, ici[l + 2][4])
        if wo is None:
            _, wo_all = _split_wait(wo_started, merged, _peers_all, False, True, "gather_w_out_wait")
            wo = jnp.transpose(wo_all.reshape(N_DEV, nl, D // N_DEV, D), (1, 0, 2, 3)).reshape(nl, D, D)
        if l + 1 < nl:
            xl, h = _out_proj_norm(xl, merged, wo[l], gate_l, norm_gain[l + 1][None, :], shift[l + 1], scale[l + 1])
        else:
            xl = _out_proj(xl, merged, wo[l], gate_l)

    loss_row, dx, g_final = _loss_head(xl, final_gain[None, :], loss_target)

    res_big, g_conv = {}, [None] * nl
    dmods = [None] * nl

    def finish_exchange(pending, after):
        l, h_in, h_out, h_rep = pending
        r_in, r_out, r_rep = _multi_wait([h_in, h_out, h_rep], after, "scatter_wait")
        res_big["w_in"] = _adamw(r_in, w_in, m_w_in, v_w_in, "adamw_w_in", l, res_big.get("w_in"))
        res_big["w_out"] = _adamw(r_out, w_out, m_w_out, v_w_out, "adamw_w_out", l, res_big.get("w_out"))
        res_big["rep"] = _adamw(r_rep[:, None], rep_w_all, rep_m_all, rep_v_all, "adamw_rep", l, res_big.get("rep"))

    pending = None
    g_gains = [None] * nl
    for l in reversed(range(nl)):
        dmerged, gw_out, dgate = _out_proj_bwd(dx, mergeds[l], wo[l], gate[l])
        dproj, g_caw, g_sw, g_sb, g_lcw, g_vec, g_wa, g_wx = _mixer_bwd(projs[l], dmerged, *states[l], mws[l])
        g_conv[l] = (g_caw, g_lcw)
        rep_g = dict(
            sgu_w=g_sw, sgu_b=g_sb[:, :, 0], lru_conv_b=g_vec[0],
            lru_wa=_unpair_blocks(g_wa), lru_ba=g_vec[1].reshape(16, 64), lru_wx=_unpair_blocks(g_wx),
            lru_bx=g_vec[2].reshape(16, 64), lru_lambda=g_vec[3])
        rep_block = _pack_rows([rep_g[n] for n in rep_names])
        (h_out, h_rep), token = _multi_start([(gw_out.reshape(N_DEV, 1, D // N_DEV, D), True), (rep_block, False)],
                                             "scatter_small_start")
        gw_in = _in_proj_bwd_w(hs_bf[l], dproj, token)
        (h_in,), token = _multi_start([(gw_in.reshape(N_DEV, UNITS_PER_DEV, D, UNIT), True)], "scatter_w_in_start")
        started = (l, h_in, h_out, h_rep)
        dh = _in_proj_bwd_h(dproj, wg[l], token)
        dx, dshift, dscale, g_gain = _norm_mod_bwd(dh, xs[l], dx, norm_gain[l][None, :], scale[l])
        g_gains[l] = g_gain
        dmods[l] = jnp.concatenate([dshift, dscale, dgate], axis=2)[:, 0, :]
        if pending is not None:
            finish_exchange(pending, dx)
        pending = started

    conv_parts = jnp.concatenate(
        [jnp.stack([g_conv[l][0] for l in range(nl)]).reshape(nl * 3, N_DEV, LANES),
         jnp.stack([g_conv[l][1] for l in range(nl)]).reshape(nl * 4, N_DEV, LANES)], axis=0)
    conv_parts = jnp.transpose(conv_parts, (1, 0, 2))[:, None]
    conv_recv = _all_to_all(conv_parts, "scatter_conv")

    dmod_loc = jnp.stack(dmods).reshape(nl * nb, 3 * D)
    gain_rows = jnp.pad(jnp.concatenate(g_gains + [g_final, loss_row], axis=0),
                        ((0, (-(nl + 2)) % 8), (0, 2 * D)))
    tail_g, _ = _all_gather(jnp.concatenate([dmod_loc, gain_rows], axis=0), "gather_dmod",
                            dep=[res_big[k][3] for k in ("w_in", "w_out", "rep")])
    loss = jnp.sum(tail_g[:, nl * nb + nl + 1, 0])
    dmod_g = tail_g[:, :nl * nb]
    gain_parts = tail_g[:, nl * nb:nl * nb + nl + 1, :D][:, None]
    gain_cat = lambda a, b: jnp.concatenate([a, b[None, :]], axis=0)
    res_gain = _adamw(gain_parts, gain_cat(norm_gain, final_gain), gain_cat(m_norm_gain, m_final_gain),
                      gain_cat(v_norm_gain, v_final_gain), "adamw_gain")
    dmod_all = jnp.transpose(dmod_g.reshape(N_DEV, nl, nb, 3 * D), (1, 0, 2, 3)).reshape(nl, N_DEV * nb, 3 * D)
    dmod_mine = lax.dynamic_slice_in_dim(dmod_all, me * mod_cols, mod_cols, axis=2)
    gw_mod, gb_mod = _mod_grad(c_all, dmod_all, dmod_mine)
    res_w_mod = _adamw(gw_mod.reshape(1, 1, nl * D, mod_cols), w_mod.reshape(nl * D, mod_cols),
                       m_w_mod.reshape(nl * D, mod_cols), v_w_mod.reshape(nl * D, mod_cols), "adamw_w_mod")
    res_w_mod = [a.reshape(nl, D, mod_cols) for a in res_w_mod]
    res_b_mod = _adamw(gb_mod.reshape(1, 1, nl, 3 * D), b_mod, m_b_mod, v_b_mod, "adamw_b_mod")
    finish_exchange(pending, res_b_mod[1])

    cat = lambda a, b: jnp.concatenate([a.reshape(nl * 3, LANES), b.reshape(nl * 4, LANES)], axis=0)
    res_conv = _adamw(conv_recv, cat(conv_a_w, lru_conv_w), cat(m_conv_a_w, m_lru_conv_w),
                      cat(v_conv_a_w, v_lru_conv_w), "adamw_conv")
    res_conv_a = [a[:nl * 3].reshape(nl, 3, LANES) for a in res_conv]
    res_lru_conv = [a[nl * 3:].reshape(nl, 4, LANES) for a in res_conv]

    res_rep = []
    for k in range(4):
        off, d = 0, {}
        for n in rep_names:
            n_rows = rep_w[n][0].size // LANES
            d[n] = res_big["rep"][k][:, off:off + n_rows].reshape(rep_w[n].shape)
            off += n_rows
        res_rep.append(d)

    def leaf(k, name):
        if name == "norm_gain":
            return res_gain[k][:nl]
        if name == "final_gain":
            return res_gain[k][nl]
        if name == "w_mod":
            return res_w_mod[k]
        if name == "b_mod":
            return res_b_mod[k]
        if name in ("w_in", "w_out"):
            return res_big[name][k]
        if name == "conv_a_w":
            return res_conv_a[k]
        if name == "lru_conv_w":
            return res_lru_conv[k]
        return res_rep[k][name]

    order = ["norm_gain", "w_mod", "b_mod", "w_in", "w_out", "conv_a_w", "sgu_w", "sgu_b", "lru_conv_w",
             "lru_conv_b", "lru_wa", "lru_ba", "lru_wx", "lru_bx", "lru_lambda", "final_gain"]
    outs = [loss, dx]
    for k in range(4):
        outs += [leaf(k, n) for n in order]
    return tuple(outs)
```

```python
import functools

import jax
import jax.numpy as jnp
from jax import lax
from jax.experimental import pallas as pl
from jax.experimental.pallas import tpu as pltpu

F32 = jnp.float32
BF16 = jnp.bfloat16

D = 1024
N_DEV = 8
N_SEG = 12
LANES = 128
SUBLANES = 8
CHUNK = 128
HALO = 16
FWD_CHUNKS_PER_TRIP = 4
BWD_CHUNKS_PER_TRIP = 8
UNIT = 512
UNITS_PER_DEV = 3
EPS = 1e-6
LRU_C = 8.0
ADAM_LR, ADAM_B1, ADAM_B2, ADAM_EPS, ADAM_WD, ADAM_STEP = 0.001, 0.9, 0.999, 1e-08, 0.01, 10
VMEM_LIMIT = 56 * 1024 * 1024

AX, AB, AC, AZ, SU, SV, SZ, RX, RZ, GA, GS, GR = range(N_SEG)
MESH = pl.DeviceIdType.MESH


def _params(sem=None):
    return pltpu.CompilerParams(dimension_semantics=sem, vmem_limit_bytes=VMEM_LIMIT)


def _my_index():
    return 4 * lax.axis_index("x") + 2 * lax.axis_index("y") + lax.axis_index("c")


def _all_gather(block, name, dep=None):
    deps = [] if dep is None else list(dep) if isinstance(dep, (list, tuple)) else [dep]

    def body(x_ref, *refs):
        out_ref, token, send_sems, recv_sems, local_sem = refs[-5:]
        x, y, c = lax.axis_index("x"), lax.axis_index("y"), lax.axis_index("c")
        me, sibling = (x, y, c), (x, y, 1 - c)
        chips = [(1 - x, y), (x, 1 - y), (1 - x, 1 - y)]
        token[...] = jnp.zeros_like(token)

        def rows(px, py, pc):
            return out_ref.at[4 * px + 2 * py + pc]

        def copy(k, blk, to, src=None):
            return pltpu.make_async_remote_copy(
                src_ref=rows(*blk) if src is None else src, dst_ref=rows(*blk),
                send_sem=send_sems.at[k], recv_sem=recv_sems.at[k], device_id=to, device_id_type=MESH)

        mine = pltpu.make_async_copy(x_ref, rows(*me), local_sem)
        mine.start()
        first = [copy(0, me, sibling, src=x_ref)]
        first += [copy(1 + j, me, (*chip, c), src=x_ref) for j, chip in enumerate(chips)]
        for cp in first:
            cp.start()
        passed = [copy(4 + j, (*chip, c), sibling) for j, chip in enumerate(chips)]
        for j, chip in enumerate(chips):
            copy(1 + j, (*chip, c), me).wait_recv()
            passed[j].start()
        copy(0, sibling, me).wait_recv()
        for j, chip in enumerate(chips):
            copy(4 + j, (*chip, 1 - c), me).wait_recv()
        for cp in first + passed:
            cp.wait_send()
        mine.wait()

    return pl.pallas_call(
        body, name=name,
        out_shape=[jax.ShapeDtypeStruct((N_DEV,) + block.shape, block.dtype), jax.ShapeDtypeStruct((8, LANES), F32)],
        in_specs=[pl.BlockSpec(memory_space=pltpu.VMEM)]
        + [pl.BlockSpec(memory_space=pl.ANY)] * len(deps),
        out_specs=[pl.BlockSpec(memory_space=pl.ANY), pl.BlockSpec(memory_space=pltpu.VMEM)],
        scratch_shapes=[pltpu.SemaphoreType.DMA((7,)), pltpu.SemaphoreType.DMA((7,)), pltpu.SemaphoreType.DMA],
    )(block, *deps)


def _all_to_all(blocks, name):
    def body(x_ref, out_ref, send_sems, recv_sems, local_sem):
        x, y, c = lax.axis_index("x"), lax.axis_index("y"), lax.axis_index("c")
        my = 4 * x + 2 * y + c
        mine = pltpu.make_async_copy(x_ref.at[my], out_ref.at[my], local_sem)
        mine.start()
        peers = []
        for r in range(1, N_DEV):
            px = 1 - x if r & 4 else x
            py = 1 - y if r & 2 else y
            pc = 1 - c if r & 1 else c
            peers.append((r - 1, 4 * px + 2 * py + pc, (px, py, pc)))

        def copy(k, src_slot, dst_slot, to):
            return pltpu.make_async_remote_copy(
                src_ref=x_ref.at[src_slot], dst_ref=out_ref.at[dst_slot],
                send_sem=send_sems.at[k], recv_sem=recv_sems.at[k], device_id=to, device_id_type=MESH)

        sends = [copy(k, pid, my, to) for k, pid, to in peers]
        for cp in sends:
            cp.start()
        for k, pid, to in peers:
            copy(k, pid, pid, to).wait_recv()
        for cp in sends:
            cp.wait_send()
        mine.wait()

    return pl.pallas_call(
        body, name=name,
        out_shape=jax.ShapeDtypeStruct(blocks.shape, blocks.dtype),
        in_specs=[pl.BlockSpec(memory_space=pltpu.VMEM)],
        out_specs=pl.BlockSpec(memory_space=pl.ANY),
        scratch_shapes=[pltpu.SemaphoreType.DMA((7,)), pltpu.SemaphoreType.DMA((7,)), pltpu.SemaphoreType.DMA],
    )(blocks)


_HBM = pl.BlockSpec(memory_space=pltpu.HBM)
_SEM = pl.BlockSpec(memory_space=pltpu.SEMAPHORE)
_EFFECT = pltpu.SideEffectType.DATAFLOW_SIDE_EFFECTING


def _peers_all(x, y, c):
    out = []
    for r in range(1, N_DEV):
        px = 1 - x if r & 4 else x
        py = 1 - y if r & 2 else y
        pc = 1 - c if r & 1 else c
        out.append((r - 1, 4 * px + 2 * py + pc, (px, py, pc)))
    return out


def _peers_same_core(x, y, c):
    return [(k, 4 * px + 2 * py + c, (px, py, c))
            for k, (px, py) in enumerate([(1 - x, y), (x, 1 - y), (1 - x, 1 - y)])]


def _split_start(src, peers_fn, scatter, name, dep=None):
    blk = src.shape[1:] if scatter else src.shape
    land_shape = (N_DEV,) + tuple(blk)
    n = len(peers_fn(0, 0, 0))
    deps = [] if dep is None else [dep]

    def body(x_ref, land_ref, *rest):
        send_sems, recv_sems, x_thru, land_thru, token = rest[len(deps):]
        x, y, c = lax.axis_index("x"), lax.axis_index("y"), lax.axis_index("c")
        my = 4 * x + 2 * y + c
        for k, pid, to in peers_fn(x, y, c):
            pltpu.make_async_remote_copy(
                src_ref=x_ref.at[pid] if scatter else x_ref, dst_ref=land_ref.at[my],
                send_sem=send_sems.at[k], recv_sem=recv_sems.at[k], device_id=to, device_id_type=MESH).start()
        token[...] = jnp.zeros_like(token)

    return pl.pallas_call(
        body, name=name,
        out_shape=(pltpu.SemaphoreType.DMA((n,)), pltpu.SemaphoreType.DMA((n,)),
                   pltpu.HBM(src.shape, src.dtype), pltpu.HBM(land_shape, src.dtype),
                   jax.ShapeDtypeStruct((8, LANES), F32)),
        in_specs=(_HBM, _HBM) + (pl.BlockSpec(memory_space=pl.ANY),) * len(deps),
        out_specs=(_SEM, _SEM, _HBM, _HBM, pl.BlockSpec(memory_space=pltpu.VMEM)),
        input_output_aliases={0: 2, 1: 3},
        compiler_params=pltpu.CompilerParams(has_side_effects=_EFFECT),
    )(pltpu.with_memory_space_constraint(src, pltpu.HBM),
      pltpu.with_memory_space_constraint(lax.empty(land_shape, src.dtype), pltpu.HBM), *deps)


def _split_wait(handles, after, peers_fn, scatter, own, name):
    send_sems, recv_sems, src_thru, land_thru, _ = handles
    blk = land_thru.shape[1:]
    after = list(after) if isinstance(after, (list, tuple)) else [after]

    def body(x_ref, land_ref, send_sems, recv_sems, *rest):
        stage = rest[len(after) + 2:]
        x, y, c = lax.axis_index("x"), lax.axis_index("y"), lax.axis_index("c")
        if own:
            my = 4 * x + 2 * y + c
            mine = _staged_copy(x_ref.at[my] if scatter else x_ref, land_ref.at[my], *stage)
        for k, pid, to in peers_fn(x, y, c):
            cp = pltpu.make_async_remote_copy(
                src_ref=x_ref.at[pid] if scatter else x_ref, dst_ref=land_ref.at[pid],
                send_sem=send_sems.at[k], recv_sem=recv_sems.at[k], device_id=to, device_id_type=MESH)
            cp.wait_send()
            cp.wait_recv()
        if own:
            mine.wait()

    return pl.pallas_call(
        body, name=name,
        out_shape=(pltpu.HBM(src_thru.shape, src_thru.dtype), pltpu.HBM(land_thru.shape, land_thru.dtype)),
        in_specs=(_HBM, _HBM, _SEM, _SEM) + (pl.BlockSpec(memory_space=pl.ANY),) * len(after),
        out_specs=(_HBM, _HBM),
        input_output_aliases={0: 0, 1: 1},
        scratch_shapes=[pltpu.VMEM(blk, land_thru.dtype), pltpu.SemaphoreType.DMA((2,))] if own else [],
        compiler_params=pltpu.CompilerParams(has_side_effects=_EFFECT, vmem_limit_bytes=VMEM_LIMIT),
    )(src_thru, land_thru, send_sems, recv_sems, *after)


def _staged_copy(src_ref, dst_ref, buf, sems):
    leg = pltpu.make_async_copy(src_ref, buf, sems.at[0])
    leg.start()
    leg.wait()
    leg = pltpu.make_async_copy(buf, dst_ref, sems.at[1])
    leg.start()
    return leg


def _multi_start(srcs, name, dep=None):
    n_src = len(srcs)
    lands = [(N_DEV,) + tuple(a.shape[1:] if sc else a.shape) for a, sc in srcs]
    deps = [] if dep is None else [dep]

    def body(*refs):
        ins, outs = refs[:2 * n_src], refs[2 * n_src + len(deps):]
        x, y, c = lax.axis_index("x"), lax.axis_index("y"), lax.axis_index("c")
        my = 4 * x + 2 * y + c
        for i, (_, scatter) in enumerate(srcs):
            x_ref, land_ref, send_sems, recv_sems = ins[2 * i], ins[2 * i + 1], outs[4 * i], outs[4 * i + 1]
            for k, pid, to in _peers_all(x, y, c):
                pltpu.make_async_remote_copy(
                    src_ref=x_ref.at[pid] if scatter else x_ref, dst_ref=land_ref.at[my],
                    send_sem=send_sems.at[k], recv_sem=recv_sems.at[k], device_id=to, device_id_type=MESH).start()
        outs[-1][...] = jnp.zeros_like(outs[-1])

    out_shape, out_specs, operands = [], [], []
    for (a, _), land in zip(srcs, lands):
        out_shape += [pltpu.SemaphoreType.DMA((N_DEV - 1,)), pltpu.SemaphoreType.DMA((N_DEV - 1,)),
                      pltpu.HBM(a.shape, a.dtype), pltpu.HBM(land, a.dtype)]
        out_specs += [_SEM, _SEM, _HBM, _HBM]
        operands += [pltpu.with_memory_space_constraint(a, pltpu.HBM),
                     pltpu.with_memory_space_constraint(lax.empty(land, a.dtype), pltpu.HBM)]
    res = pl.pallas_call(
        body, name=name,
        out_shape=tuple(out_shape) + (jax.ShapeDtypeStruct((8, LANES), F32),),
        in_specs=(_HBM,) * (2 * n_src) + (pl.BlockSpec(memory_space=pl.ANY),) * len(deps),
        out_specs=tuple(out_specs) + (pl.BlockSpec(memory_space=pltpu.VMEM),),
        input_output_aliases={2 * i + j: 4 * i + 2 + j for i in range(n_src) for j in range(2)},
        compiler_params=pltpu.CompilerParams(has_side_effects=_EFFECT),
    )(*operands, *deps)
    return [tuple(res[4 * i:4 * i + 4]) + (srcs[i][1],) for i in range(n_src)], res[-1]


def _multi_wait(started, after, name):
    n_src = len(started)
    after = list(after) if isinstance(after, (list, tuple)) else [after]

    def body(*refs):
        ins = refs[:4 * n_src]
        stage = refs[4 * n_src + len(after) + 2 * n_src:]
        x, y, c = lax.axis_index("x"), lax.axis_index("y"), lax.axis_index("c")
        my = 4 * x + 2 * y + c
        pending = []
        for i, h in enumerate(started):
            x_ref, land_ref, send_sems, recv_sems = ins[4 * i:4 * i + 4]
            scatter = h[4]
            pending.append(_staged_copy(x_ref.at[my] if scatter else x_ref, land_ref.at[my],
                                        stage[2 * i], stage[2 * i + 1]))
            for k, pid, to in _peers_all(x, y, c):
                cp = pltpu.make_async_remote_copy(
                    src_ref=x_ref.at[pid] if scatter else x_ref, dst_ref=land_ref.at[pid],
                    send_sem=send_sems.at[k], recv_sem=recv_sems.at[k], device_id=to, device_id_type=MESH)
                cp.wait_send()
                cp.wait_recv()
        for leg in pending:
            leg.wait()

    operands, out_shape, scratch = [], [], []
    for send, recv, src_thru, land_thru, _ in started:
        operands += [src_thru, land_thru, send, recv]
        out_shape += [pltpu.HBM(src_thru.shape, src_thru.dtype), pltpu.HBM(land_thru.shape, land_thru.dtype)]
        scratch += [pltpu.VMEM(land_thru.shape[1:], land_thru.dtype), pltpu.SemaphoreType.DMA((2,))]
    res = pl.pallas_call(
        body, name=name,
        out_shape=tuple(out_shape),
        in_specs=(_HBM, _HBM, _SEM, _SEM) * n_src + (pl.BlockSpec(memory_space=pl.ANY),) * len(after),
        out_specs=(_HBM,) * (2 * n_src),
        input_output_aliases={4 * i + j: 2 * i + j for i in range(n_src) for j in range(2)},
        scratch_shapes=scratch,
        compiler_params=pltpu.CompilerParams(has_side_effects=_EFFECT, vmem_limit_bytes=VMEM_LIMIT),
    )(*operands, *after)
    return [res[2 * i + 1] for i in range(n_src)]


def _gather_finish(block, land, name):
    def body(x_ref, land_ref, out_ref, token, send_sems, recv_sems, buf, local_sems):
        x, y, c = lax.axis_index("x"), lax.axis_index("y"), lax.axis_index("c")
        my, sib_id, sibling = 4 * x + 2 * y + c, 4 * x + 2 * y + 1 - c, (x, y, 1 - c)
        token[...] = jnp.zeros_like(token)

        def copy(k, slot, src=None):
            return pltpu.make_async_remote_copy(
                src_ref=land_ref.at[slot] if src is None else src, dst_ref=out_ref.at[slot],
                send_sem=send_sems.at[k], recv_sem=recv_sems.at[k], device_id=sibling, device_id_type=MESH)

        chips = _peers_same_core(x, y, c)
        sends = [copy(0, my, src=x_ref)] + [copy(1 + k, pid) for k, pid, _ in chips]
        for cp in sends:
            cp.start()
        mine = _staged_copy(x_ref, out_ref.at[my], buf, local_sems)
        copy(0, sib_id).wait_recv()
        for k, pid, _ in chips:
            copy(1 + k, pid + 1 - 2 * c).wait_recv()
        for cp in sends:
            cp.wait_send()
        mine.wait()

    return pl.pallas_call(
        body, name=name,
        out_shape=[jax.ShapeDtypeStruct(land.shape, land.dtype), jax.ShapeDtypeStruct((8, LANES), F32)],
        in_specs=[pl.BlockSpec(memory_space=pl.ANY), pl.BlockSpec(memory_space=pl.ANY)],
        out_specs=[pl.BlockSpec(memory_space=pl.ANY), pl.BlockSpec(memory_space=pltpu.VMEM)],
        input_output_aliases={1: 0},
        scratch_shapes=[pltpu.SemaphoreType.DMA((4,)), pltpu.SemaphoreType.DMA((4,)),
                        pltpu.VMEM(block.shape, block.dtype), pltpu.SemaphoreType.DMA((2,))],
        compiler_params=pltpu.CompilerParams(vmem_limit_bytes=VMEM_LIMIT),
    )(block, land)


def _forward_start(block, land, name):
    def body(x_ref, land_ref, send_sems, recv_sems, x_thru, land_thru, token):
        x, y, c = lax.axis_index("x"), lax.axis_index("y"), lax.axis_index("c")
        my, sibling = 4 * x + 2 * y + c, (x, y, 1 - c)
        slots = [(0, my, x_ref)] + [(1 + k, pid, land_ref.at[pid]) for k, pid, _ in _peers_same_core(x, y, c)]
        for k, slot, src in slots:
            pltpu.make_async_remote_copy(
                src_ref=src, dst_ref=land_ref.at[slot], send_sem=send_sems.at[k], recv_sem=recv_sems.at[k],
                device_id=sibling, device_id_type=MESH).start()
        token[...] = jnp.zeros_like(token)

    return pl.pallas_call(
        body, name=name,
        out_shape=(pltpu.SemaphoreType.DMA((4,)), pltpu.SemaphoreType.DMA((4,)),
                   pltpu.HBM(block.shape, block.dtype), pltpu.HBM(land.shape, land.dtype),
                   jax.ShapeDtypeStruct((8, LANES), F32)),
        in_specs=(_HBM, _HBM),
        out_specs=(_SEM, _SEM, _HBM, _HBM, pl.BlockSpec(memory_space=pltpu.VMEM)),
        input_output_aliases={0: 2, 1: 3},
        compiler_params=pltpu.CompilerParams(has_side_effects=_EFFECT),
    )(block, land)


def _forward_wait(handles, after, name):
    send_sems, recv_sems, block_thru, land_thru, _ = handles

    def body(x_ref, land_ref, send_sems, recv_sems, after_ref, x_dead, got_ref, buf, local_sems):
        x, y, c = lax.axis_index("x"), lax.axis_index("y"), lax.axis_index("c")
        my, sib_id, sibling = 4 * x + 2 * y + c, 4 * x + 2 * y + 1 - c, (x, y, 1 - c)
        mine = _staged_copy(x_ref, land_ref.at[my], buf, local_sems)
        slots = [(0, my, sib_id)] + [(1 + k, pid, pid + 1 - 2 * c) for k, pid, _ in _peers_same_core(x, y, c)]
        for k, sent, got in slots:
            cp = pltpu.make_async_remote_copy(
                src_ref=land_ref.at[sent], dst_ref=land_ref.at[got], send_sem=send_sems.at[k],
                recv_sem=recv_sems.at[k], device_id=sibling, device_id_type=MESH)
            cp.wait_send()
            cp.wait_recv()
        mine.wait()

    return pl.pallas_call(
        body, name=name,
        out_shape=(pltpu.HBM(block_thru.shape, block_thru.dtype), pltpu.HBM(land_thru.shape, land_thru.dtype)),
        in_specs=(_HBM, _HBM, _SEM, _SEM, pl.BlockSpec(memory_space=pl.ANY)),
        out_specs=(_HBM, _HBM),
        input_output_aliases={0: 0, 1: 1},
        scratch_shapes=[pltpu.VMEM(block_thru.shape, block_thru.dtype), pltpu.SemaphoreType.DMA((2,))],
        compiler_params=pltpu.CompilerParams(has_side_effects=_EFFECT, vmem_limit_bytes=VMEM_LIMIT),
    )(block_thru, land_thru, send_sems, recv_sems, after)[1]


def _after(v, token):
    return v + token[0, 0].astype(v.dtype)


def _dsilu(x, s):
    return s * (1.0 + x * (1.0 - s))


def _log1p(x):
    u = 1.0 + x
    d = u - 1.0
    return jnp.where(d == 0.0, x, jnp.log(u) * (x / jnp.where(d == 0.0, 1.0, d)))


def _softplus_neg(lam):
    return jnp.maximum(-lam, 0.0) + _log1p(jnp.exp(-jnp.abs(lam)))


def _neg_expm1(y, exp_y):
    poly = -y * (1.0 + y * (0.5 + y * (1.0 / 6.0 + y * (1.0 / 24.0))))
    return jnp.where(y > -0.05, poly, 1.0 - exp_y)


def _sigmoid(x):
    return 0.5 * jnp.tanh(0.5 * x) + 0.5


def _shift_dn(cur, prev, k):
    ext = jnp.concatenate([prev, cur], axis=0)
    return pltpu.roll(ext, k, 0)[HALO:, :]


def _shift_up(cur, nxt, k):
    n = cur.shape[0]
    ext = jnp.concatenate([cur, nxt], axis=0)
    return pltpu.roll(ext, n + HALO - k, 0)[:n, :]


def _scan_fwd(a, b, h_prev):
    groups = a.shape[0] // SUBLANES
    a3 = a.reshape(groups, SUBLANES, LANES)
    b3 = b.reshape(groups, SUBLANES, LANES)
    row = lax.broadcasted_iota(jnp.int32, a3.shape, 1)
    k = 1
    while k < SUBLANES:
        a_sh = jnp.where(row >= k, pltpu.roll(a3, k, 1), 1.0)
        b_sh = jnp.where(row >= k, pltpu.roll(b3, k, 1), 0.0)
        b3 = a3 * b_sh + b3
        a3 = a3 * a_sh
        k *= 2
    carry = h_prev[HALO - 1:HALO, :]
    out = []
    for i in range(groups):
        hg = b3[i] + a3[i] * carry
        out.append(hg)
        carry = hg[SUBLANES - 1:SUBLANES, :]
    return jnp.concatenate(out, axis=0)


def _scan_rev(a_next, g, lam_next):
    groups = g.shape[0] // SUBLANES
    a3 = a_next.reshape(groups, SUBLANES, LANES)
    g3 = g.reshape(groups, SUBLANES, LANES)
    row = lax.broadcasted_iota(jnp.int32, a3.shape, 1)
    k = 1
    while k < SUBLANES:
        ok = row < SUBLANES - k
        a_sh = jnp.where(ok, pltpu.roll(a3, SUBLANES - k, 1), 1.0)
        g_sh = jnp.where(ok, pltpu.roll(g3, SUBLANES - k, 1), 0.0)
        g3 = g3 + a3 * g_sh
        a3 = a3 * a_sh
        k *= 2
    carry = lam_next[0:1, :]
    out = [None] * groups
    for i in reversed(range(groups)):
        lg = g3[i] + a3[i] * carry
        out[i] = lg
        carry = lg[0:1, :]
    return jnp.concatenate(out, axis=0)


def _rowsum(v):
    return jnp.sum(v, axis=0, keepdims=True)


def _dot(a, b):
    return jnp.dot(a, b, preferred_element_type=F32)


def _dot_nt(a, b):
    return lax.dot_general(a, b, (((1,), (1,)), ((), ())), preferred_element_type=F32)


def _dot_tn(a, b):
    return lax.dot_general(a, b, (((0,), (0,)), ((), ())), preferred_element_type=F32)


class _MixerWeights:
    def __init__(self, caw_ref, sw_ref, sb_ref, lcw_ref, lcb_ref, wa_ref, wx_ref, ba_ref, bx_ref, lam_ref):
        self.caw = [caw_ref[j:j + 1, :] for j in range(3)]
        self.lcw = [lcw_ref[j:j + 1, :] for j in range(4)]
        self.lcb = lcb_ref[...]
        row = lax.broadcasted_iota(jnp.int32, (CHUNK, CHUNK), 0)
        col = lax.broadcasted_iota(jnp.int32, (CHUNK, CHUNK), 1)
        self.tril = col <= row
        self.sw = jnp.where(self.tril, sw_ref[...], 0.0).astype(BF16)
        self.sb = sb_ref[...]
        self.wa = wa_ref[...]
        self.wx = wx_ref[...]
        self.ba = ba_ref[...]
        self.bx = bx_ref[...]
        lam = lam_ref[...]
        self.neg_c_sp = -LRU_C * _softplus_neg(lam)
        self.dsp_dlam = -_sigmoid(-lam)


def _mixer_a(ld, ldp, w, cv=None):
    t = {}
    a_x, a_c = ld(AX), ld(AC)
    t["a_x"], t["a_c"], t["a_b"], t["a_z"] = a_x, a_c, ld(AB), ld(AZ)
    t["ca"] = ca = a_c * a_x
    if cv is None:
        ca_p = ldp(AC) * ldp(AX)
        cv = w.caw[2] * ca + w.caw[1] * _shift_dn(ca, ca_p, 1) + w.caw[0] * _shift_dn(ca, ca_p, 2)
    t["cv"] = cv
    t["sa"] = _sigmoid(t["a_z"])
    t["silu_az"] = t["a_z"] * t["sa"]
    t["y_a"] = t["silu_az"] * t["a_b"] * t["cv"]
    return t


def _mixer_b(ld, w):
    t = {}
    v = ld(SV)
    vc = v - jnp.mean(v, axis=1, keepdims=True)
    t["rstd"] = lax.rsqrt(jnp.mean(vc * vc, axis=1, keepdims=True) + EPS)
    t["vn"] = vc * t["rstd"]
    t["z"] = _dot(w.sw, t["vn"].astype(BF16)) + w.sb
    t["s_u"], t["s_z"] = ld(SU), ld(SZ)
    t["ss"] = _sigmoid(t["s_z"])
    t["silu_sz"] = t["s_z"] * t["ss"]
    t["y_s"] = t["silu_sz"] * t["s_u"] * t["z"]
    return t


def _mixer_c(ld, ldp, w, backward, xc=None):
    t = {}
    t["r_x"] = r_x = ld(RX)
    if xc is None:
        r_xp = ldp(RX)
        xc = (w.lcb + w.lcw[0] * _shift_dn(r_x, r_xp, 3) + w.lcw[1] * _shift_dn(r_x, r_xp, 2)
              + w.lcw[2] * _shift_dn(r_x, r_xp, 1) + w.lcw[3] * r_x)
    t["xc"] = xc
    xcb = xc.astype(BF16)
    t["r"] = _sigmoid(_dot(xcb, w.wa) + w.ba)
    t["i"] = _sigmoid(_dot(xcb, w.wx) + w.bx)
    la = t["r"] * w.neg_c_sp
    t["a"] = jnp.exp(la)
    t["em"] = _neg_expm1(2.0 * la, t["a"] * t["a"])
    if backward:
        t["inv_mult"] = lax.rsqrt(t["em"])
        t["mult"] = t["em"] * t["inv_mult"]
    else:
        t["mult"] = jnp.sqrt(t["em"])
    t["b"] = t["mult"] * (t["i"] * xc)
    t["r_z"] = ld(RZ)
    t["sr"] = _sigmoid(t["r_z"])
    t["silu_rz"] = t["r_z"] * t["sr"]
    return t


def _mixer_pre_scan(ld, ldp, w, backward, cv=None, xc=None):
    t = {**_mixer_a(ld, ldp, w, cv), **_mixer_b(ld, w), **_mixer_c(ld, ldp, w, backward, xc)}
    t["ga"], t["gs"], t["gr"] = _sigmoid(ld(GA)), _sigmoid(ld(GS)), _sigmoid(ld(GR))
    return t


def _weight_specs(n_cb_axis):
    def at(fn):
        return lambda *g: fn(g[n_cb_axis])
    return [
        pl.BlockSpec((3, LANES), at(lambda cb: (0, cb))),
        pl.BlockSpec((None, CHUNK, CHUNK), at(lambda cb: (cb, 0, 0))),
        pl.BlockSpec((None, CHUNK, LANES), at(lambda cb: (cb, 0, 0))),
        pl.BlockSpec((4, LANES), at(lambda cb: (0, cb))),
        pl.BlockSpec((1, LANES), at(lambda cb: (0, cb))),
        pl.BlockSpec((None, LANES, LANES), at(lambda cb: (cb, 0, 0))),
        pl.BlockSpec((None, LANES, LANES), at(lambda cb: (cb, 0, 0))),
        pl.BlockSpec((1, LANES), at(lambda cb: (0, cb))),
        pl.BlockSpec((1, LANES), at(lambda cb: (0, cb))),
        pl.BlockSpec((1, LANES), at(lambda cb: (0, cb))),
    ]


def _chunk_loaders(p_ref, c):
    r0 = pl.multiple_of(c * CHUNK, CHUNK)
    rp = pl.multiple_of(jnp.maximum(c * CHUNK - HALO, 0), HALO)

    def ld(j):
        return p_ref[j, pl.ds(r0, CHUNK), :].astype(F32)

    def ldp(j):
        return jnp.where(c > 0, p_ref[j, pl.ds(rp, HALO), :].astype(F32), 0.0)

    return r0, rp, ld, ldp


def _mixer_fwd(proj, mw):
    _, nb, s, _ = proj.shape
    n_chunks = s // CHUNK

    def body(p_ref, *refs):
        w = _MixerWeights(*refs[:10])
        merged_ref, hs_ref, cv_ref, xc_ref = refs[10:]

        def chunk(c, h_prev):
            r0, _, ld, ldp = _chunk_loaders(p_ref, c)
            t = _mixer_pre_scan(ld, ldp, w, False)
            h = _scan_fwd(t["a"], t["b"], h_prev)
            y_r = t["silu_rz"] * h
            merged = t["ga"] * t["y_a"] + t["gs"] * t["y_s"] + t["gr"] * y_r
            merged_ref[pl.ds(r0, CHUNK), :] = merged.astype(BF16)
            hs_ref[pl.ds(r0, CHUNK), :] = h
            cv_ref[pl.ds(r0, CHUNK), :] = t["cv"].astype(BF16)
            xc_ref[pl.ds(r0, CHUNK), :] = t["xc"].astype(BF16)
            return h[CHUNK - HALO:, :]

        def group(i, carry):
            for k in range(FWD_CHUNKS_PER_TRIP):
                carry = chunk(FWD_CHUNKS_PER_TRIP * i + k, carry)
            return carry

        assert n_chunks % FWD_CHUNKS_PER_TRIP == 0
        lax.fori_loop(0, n_chunks // FWD_CHUNKS_PER_TRIP, group, jnp.zeros((HALO, LANES), F32))

    slab = pl.BlockSpec((None, s, LANES), lambda cb, b: (b, 0, cb))
    half = jax.ShapeDtypeStruct((nb, s, D), BF16)
    return pl.pallas_call(
        body, name="mixer_fwd", grid=(D // LANES, nb),
        in_specs=[pl.BlockSpec((N_SEG, None, s, LANES), lambda cb, b: (0, b, 0, cb))] + _weight_specs(0),
        out_specs=[slab, slab, slab, slab],
        out_shape=[half, jax.ShapeDtypeStruct((nb, s, D), F32), half, half],
        compiler_params=_params(("arbitrary", "arbitrary")),
    )(proj, *mw)


def _mixer_bwd(proj, dmerged, hs, cv, xc, mw):
    _, nb, s, _ = proj.shape
    n_chunks = s // CHUNK

    def body(p_ref, dm_ref, hs_ref, cv_ref, xc_ref, *refs):
        w = _MixerWeights(*refs[:10])
        dp_ref, g_caw, g_sw, g_sb, g_lcw, g_vec, g_wa, g_wx = refs[10:]

        @pl.when(pl.program_id(1) == 0)
        def _():
            for ref in (g_caw, g_sw, g_sb, g_lcw, g_vec, g_wa, g_wx):
                ref[...] = jnp.zeros_like(ref)

        def chunk(i, carry):
            dcv_n, dxc_n, lam_n, a_n = carry
            c = n_chunks - 1 - i
            r0, rp, ld, ldp = _chunk_loaders(p_ref, c)
            t = _mixer_pre_scan(ld, ldp, w, True, cv_ref[pl.ds(r0, CHUNK), :].astype(F32),
                                xc_ref[pl.ds(r0, CHUNK), :].astype(F32))
            h = hs_ref[pl.ds(r0, CHUNK), :]
            h_p = jnp.where(c > 0, hs_ref[pl.ds(rp, HALO), :], 0.0)
            h_prev = _shift_dn(h, h_p, 1)
            dm = dm_ref[pl.ds(r0, CHUNK), :].astype(F32)
            y_r = t["silu_rz"] * h

            def out(j, val):
                dp_ref[j, pl.ds(r0, CHUNK), :] = val.astype(BF16)

            ga, gs, gr = t["ga"], t["gs"], t["gr"]
            out(GA, dm * t["y_a"] * ga * (1.0 - ga))
            out(GS, dm * t["y_s"] * gs * (1.0 - gs))
            out(GR, dm * y_r * gr * (1.0 - gr))

            dy_a = dm * ga
            out(AZ, dy_a * t["a_b"] * t["cv"] * _dsilu(t["a_z"], t["sa"]))
            out(AB, dy_a * t["silu_az"] * t["cv"])
            dcv = dy_a * t["silu_az"] * t["a_b"]
            dcv1, dcv2 = _shift_up(dcv, dcv_n, 1), _shift_up(dcv, dcv_n, 2)
            dca = w.caw[2] * dcv + w.caw[1] * dcv1 + w.caw[0] * dcv2
            out(AC, dca * t["a_x"])
            out(AX, dca * t["a_c"])
            g_caw[2:3, :] += _rowsum(dcv * t["ca"])
            g_caw[1:2, :] += _rowsum(dcv1 * t["ca"])
            g_caw[0:1, :] += _rowsum(dcv2 * t["ca"])

            dy_s = dm * gs
            out(SZ, dy_s * t["s_u"] * t["z"] * _dsilu(t["s_z"], t["ss"]))
            out(SU, dy_s * t["silu_sz"] * t["z"])
            dz = dy_s * t["silu_sz"] * t["s_u"]
            dzb = dz.astype(BF16)
            g_sb[...] += jnp.broadcast_to(jnp.sum(dz, axis=1, keepdims=True), (CHUNK, LANES))
            g_sw[...] += _dot_nt(dzb, t["vn"].astype(BF16))
            dvn = _dot_tn(w.sw, dzb)
            vn = t["vn"]
            out(SV, t["rstd"] * (dvn - jnp.mean(dvn, axis=1, keepdims=True)
                                 - vn * jnp.mean(dvn * vn, axis=1, keepdims=True)))

            dy_r = dm * gr
            out(RZ, dy_r * h * _dsilu(t["r_z"], t["sr"]))
            lam = _scan_rev(_shift_up(t["a"], a_n, 1), dy_r * t["silu_rz"], lam_n)
            a, r, ig, xc, mult = t["a"], t["r"], t["i"], t["xc"], t["mult"]
            d_i = lam * mult * xc
            d_mult = lam * ig * xc
            dxc = lam * mult * ig
            dla = lam * h_prev * a - d_mult * ((1.0 - t["em"]) * t["inv_mult"])
            g_vec[3:4, :] += _rowsum(dla * r) * (-LRU_C * w.dsp_dlam)
            dpr = (dla * w.neg_c_sp) * r * (1.0 - r)
            dpi = d_i * ig * (1.0 - ig)
            dprb, dpib, xcb = dpr.astype(BF16), dpi.astype(BF16), xc.astype(BF16)
            g_wa[...] += _dot_tn(xcb, dprb)
            g_wx[...] += _dot_tn(xcb, dpib)
            g_vec[1:2, :] += _rowsum(dpr)
            g_vec[2:3, :] += _rowsum(dpi)
            dxc = dxc + _dot_nt(dprb, w.wa) + _dot_nt(dpib, w.wx)
            g_vec[0:1, :] += _rowsum(dxc)
            dxcs = [_shift_up(dxc, dxc_n, 3), _shift_up(dxc, dxc_n, 2), _shift_up(dxc, dxc_n, 1), dxc]
            out(RX, w.lcw[3] * dxcs[3] + w.lcw[2] * dxcs[2] + w.lcw[1] * dxcs[1] + w.lcw[0] * dxcs[0])
            for j in range(4):
                g_lcw[j:j + 1, :] += _rowsum(dxcs[j] * t["r_x"])
            return dcv[:HALO, :], dxc[:HALO, :], lam[:HALO, :], a[:HALO, :]

        zero = jnp.zeros((HALO, LANES), F32)
        def group(i, carry):
            for k in range(BWD_CHUNKS_PER_TRIP):
                carry = chunk(BWD_CHUNKS_PER_TRIP * i + k, carry)
            return carry

        assert n_chunks % BWD_CHUNKS_PER_TRIP == 0
        lax.fori_loop(0, n_chunks // BWD_CHUNKS_PER_TRIP, group, (zero, zero, zero, zero))

        @pl.when(pl.program_id(1) == nb - 1)
        def _():
            g_sw[...] = jnp.where(w.tril, g_sw[...], 0.0)

    slab = lambda dt: pl.BlockSpec((None, s, LANES), lambda cb, b: (b, 0, cb))
    seg = pl.BlockSpec((N_SEG, None, s, LANES), lambda cb, b: (0, b, 0, cb))
    rows = lambda n: pl.BlockSpec((n, LANES), lambda cb, b: (0, cb))
    sq = pl.BlockSpec((None, LANES, LANES), lambda cb, b: (cb, 0, 0))
    n_cb = D // LANES
    return pl.pallas_call(
        body, name="mixer_bwd", grid=(n_cb, nb),
        in_specs=[seg, slab(BF16), slab(F32), slab(BF16), slab(BF16)] + _weight_specs(0),
        out_specs=[seg, rows(3), sq, sq, rows(4), rows(8), sq, sq],
        out_shape=[
            jax.ShapeDtypeStruct(proj.shape, BF16),
            jax.ShapeDtypeStruct((3, D), F32),
            jax.ShapeDtypeStruct((n_cb, CHUNK, CHUNK), F32),
            jax.ShapeDtypeStruct((n_cb, CHUNK, LANES), F32),
            jax.ShapeDtypeStruct((4, D), F32),
            jax.ShapeDtypeStruct((8, D), F32),
            jax.ShapeDtypeStruct((n_cb, LANES, LANES), F32),
            jax.ShapeDtypeStruct((n_cb, LANES, LANES), F32),
        ],
        compiler_params=_params(("arbitrary", "arbitrary")),
    )(proj, dmerged, hs, cv, xc, *mw)


def _row_tile(s, want):
    return want if s % want == 0 else s


def _norm_mod(x, gain, shift, scale):
    nb, s, _ = x.shape
    tm = _row_tile(s, 512)

    def body(x_ref, g_ref, sh_ref, sc_ref, h_ref, ht_ref):
        xv = x_ref[...]
        r = lax.rsqrt(jnp.mean(xv * xv, axis=1, keepdims=True) + EPS)
        h = ((xv * r) * g_ref[...] * (1.0 + sc_ref[...]) + sh_ref[...]).astype(BF16)
        h_ref[...] = h
        ht_ref[...] = h.T

    tile = pl.BlockSpec((None, tm, D), lambda b, m: (b, m, 0))
    vec = pl.BlockSpec((None, 1, D), lambda b, m: (b, 0, 0))
    return pl.pallas_call(
        body, name="norm_mod", grid=(nb, s // tm),
        in_specs=[tile, pl.BlockSpec((1, D), lambda b, m: (0, 0)), vec, vec],
        out_specs=[tile, pl.BlockSpec((None, D, tm), lambda b, m: (b, 0, m))],
        out_shape=[jax.ShapeDtypeStruct(x.shape, BF16), jax.ShapeDtypeStruct((nb, D, s), BF16)],
        compiler_params=_params(("arbitrary", "arbitrary")),
    )(x, gain, shift, scale)


def _in_proj(h, wg, dep):
    nb, s, _ = h.shape

    def body(h_ref, w0_ref, w1_ref, dep_ref, o_ref):
        hv = h_ref[...]
        o_ref[:, :UNIT] = _dot(hv, w0_ref[...]).astype(BF16)
        o_ref[:, UNIT:] = _dot(hv, w1_ref[...]).astype(BF16)

    def unit(k):
        return pl.BlockSpec((None, D, UNIT),
                            lambda b, j: ((2 * j + k) // UNITS_PER_DEV, 0, (2 * j + k) % UNITS_PER_DEV))

    return pl.pallas_call(
        body, name="in_proj", grid=(nb, N_SEG),
        in_specs=[pl.BlockSpec((None, s, D), lambda b, j: (b, 0, 0)), unit(0), unit(1),
                  pl.BlockSpec((8, LANES), lambda b, j: (0, 0))],
        out_specs=pl.BlockSpec((None, None, s, D), lambda b, j: (j, b, 0, 0)),
        out_shape=jax.ShapeDtypeStruct((N_SEG, nb, s, D), BF16),
        compiler_params=_params(("arbitrary", "arbitrary")),
    )(h, wg, wg, dep)


def _out_proj(x, merged, wout, gate):
    nb, s, _ = x.shape
    tm = _row_tile(s, 512)

    def body(x_ref, m_ref, w_ref, g_ref, o_ref):
        o_ref[...] = x_ref[...] + g_ref[...] * _dot(m_ref[...], w_ref[...])

    tile = pl.BlockSpec((None, tm, D), lambda b, m: (b, m, 0))
    return pl.pallas_call(
        body, name="out_proj", grid=(nb, s // tm),
        in_specs=[tile, tile, pl.BlockSpec((D, D), lambda b, m: (0, 0)),
                  pl.BlockSpec((None, 1, D), lambda b, m: (b, 0, 0))],
        out_specs=tile, out_shape=jax.ShapeDtypeStruct(x.shape, F32),
        compiler_params=_params(("arbitrary", "arbitrary")),
    )(x, merged, wout, gate)


def _out_proj_norm(x, merged, wout, gate, gain, shift, scale):
    nb, s, _ = x.shape
    tm = _row_tile(s, 512)

    def body(x_ref, m_ref, w_ref, g_ref, gn_ref, sh_ref, sc_ref, o_ref, h_ref, ht_ref):
        xv = x_ref[...] + g_ref[...] * _dot(m_ref[...], w_ref[...])
        o_ref[...] = xv
        r = lax.rsqrt(jnp.mean(xv * xv, axis=1, keepdims=True) + EPS)
        h = ((xv * r) * gn_ref[...] * (1.0 + sc_ref[...]) + sh_ref[...]).astype(BF16)
        h_ref[...] = h
        ht_ref[...] = h.T

    tile = pl.BlockSpec((None, tm, D), lambda b, m: (b, m, 0))
    vec = pl.BlockSpec((None, 1, D), lambda b, m: (b, 0, 0))
    return pl.pallas_call(
        body, name="out_proj_norm", grid=(nb, s // tm),
        in_specs=[tile, tile, pl.BlockSpec((D, D), lambda b, m: (0, 0)), vec,
                  pl.BlockSpec((1, D), lambda b, m: (0, 0)), vec, vec],
        out_specs=[tile, tile, pl.BlockSpec((None, D, tm), lambda b, m: (b, 0, m))],
        out_shape=[jax.ShapeDtypeStruct(x.shape, F32), jax.ShapeDtypeStruct(x.shape, BF16),
                   jax.ShapeDtypeStruct((nb, D, s), BF16)],
        compiler_params=_params(("arbitrary", "arbitrary")),
    )(x, merged, wout, gate, gain, shift, scale)


def _loss_head(x, gain, target):
    nb, s, _ = x.shape
    tm = _row_tile(s, 512)

    def body(x_ref, g_ref, t_ref, loss_ref, dx_ref, dg_ref):
        first = (pl.program_id(0) == 0) & (pl.program_id(1) == 0)
        last = (pl.program_id(0) == nb - 1) & (pl.program_id(1) == s // tm - 1)

        @pl.when(first)
        def _():
            loss_ref[...] = jnp.zeros_like(loss_ref)
            dg_ref[...] = jnp.zeros_like(dg_ref)

        xv = x_ref[...]
        r = lax.rsqrt(jnp.mean(xv * xv, axis=1, keepdims=True) + EPS)
        xn = xv * r
        g = g_ref[...]
        e = xn * g - t_ref[...]
        loss_ref[...] += _rowsum(e * e) * (0.5 / D)
        dy = e * (1.0 / D)
        dg_ref[...] += _rowsum(dy * xn)
        dxn = dy * g
        dx_ref[...] = r * (dxn - xn * jnp.mean(dxn * xn, axis=1, keepdims=True))

        @pl.when(last)
        def _():
            loss_ref[...] = jnp.broadcast_to(jnp.sum(loss_ref[...], axis=1, keepdims=True), (1, D))

    tile = pl.BlockSpec((None, tm, D), lambda b, m: (b, m, 0))
    vec = pl.BlockSpec((1, D), lambda b, m: (0, 0))
    return pl.pallas_call(
        body, name="loss_head", grid=(nb, s // tm),
        in_specs=[tile, vec, tile], out_specs=[vec, tile, vec],
        out_shape=[jax.ShapeDtypeStruct((1, D), F32), jax.ShapeDtypeStruct(x.shape, F32),
                   jax.ShapeDtypeStruct((1, D), F32)],
        compiler_params=_params(("arbitrary", "arbitrary")),
    )(x, gain, target)


def _out_proj_bwd(dxo, merged, wout, gate):
    nb, s, _ = dxo.shape
    tm = _row_tile(s, 512)

    def body(d_ref, m_ref, w_ref, g_ref, dm_ref, gw_ref, dg_ref):
        @pl.when((pl.program_id(0) == 0) & (pl.program_id(1) == 0))
        def _():
            gw_ref[...] = jnp.zeros_like(gw_ref)

        @pl.when(pl.program_id(1) == 0)
        def _():
            dg_ref[...] = jnp.zeros_like(dg_ref)

        d = d_ref[...]
        m = m_ref[...]
        wv = w_ref[...]
        dg_ref[...] += _rowsum(d * _dot(m, wv))
        dout = (d * g_ref[...]).astype(BF16)
        dm_ref[...] = _dot_nt(dout, wv).astype(BF16)
        gw_ref[...] += _dot_tn(m, dout)

    tile = pl.BlockSpec((None, tm, D), lambda b, m: (b, m, 0))
    vec = pl.BlockSpec((None, 1, D), lambda b, m: (b, 0, 0))
    full = pl.BlockSpec((D, D), lambda b, m: (0, 0))
    return pl.pallas_call(
        body, name="out_proj_bwd", grid=(nb, s // tm),
        in_specs=[tile, tile, full, vec], out_specs=[tile, full, vec],
        out_shape=[jax.ShapeDtypeStruct(dxo.shape, BF16), jax.ShapeDtypeStruct((D, D), F32),
                   jax.ShapeDtypeStruct((nb, 1, D), F32)],
        compiler_params=_params(("arbitrary", "arbitrary")),
    )(dxo, merged, wout, gate)


def _in_proj_bwd_h(dproj, wg, dep):
    _, nb, s, _ = dproj.shape
    tm = _row_tile(s, 1024)

    def body(dp0_ref, dp1_ref, w0_ref, w1_ref, w2_ref, w3_ref, dep_ref, dh_ref):
        j = pl.program_id(2)
        part = (_dot_nt(dp0_ref[...], jnp.concatenate([w0_ref[...], w1_ref[...]], axis=1))
                + _dot_nt(dp1_ref[...], jnp.concatenate([w2_ref[...], w3_ref[...]], axis=1)))

        @pl.when(j == 0)
        def _():
            dh_ref[...] = part

        @pl.when(j > 0)
        def _():
            dh_ref[...] += part

    def seg(k):
        return pl.BlockSpec((None, None, tm, D), lambda b, m, j: (2 * j + k, b, m, 0))

    def unit(k):
        return pl.BlockSpec((None, D, UNIT),
                            lambda b, m, j: ((4 * j + k) // UNITS_PER_DEV, 0, (4 * j + k) % UNITS_PER_DEV))

    return pl.pallas_call(
        body, name="in_proj_bwd_h", grid=(nb, s // tm, N_SEG // 2),
        in_specs=[seg(0), seg(1), unit(0), unit(1), unit(2), unit(3),
                  pl.BlockSpec((8, LANES), lambda b, m, j: (0, 0))],
        out_specs=pl.BlockSpec((None, tm, D), lambda b, m, j: (b, m, 0)),
        out_shape=jax.ShapeDtypeStruct((nb, s, D), F32),
        compiler_params=_params(("arbitrary", "arbitrary", "arbitrary")),
    )(dproj, dproj, wg, wg, wg, wg, dep)


def _norm_mod_bwd(dh, x, dxo, gain, scale):
    nb, s, _ = x.shape
    tm = _row_tile(s, 512)

    def body(dh_ref, x_ref, dxo_ref, g_ref, sc_ref, dx_ref, dsh_ref, dsc_ref, dg_ref):
        b, m = pl.program_id(0), pl.program_id(1)

        @pl.when((b == 0) & (m == 0))
        def _():
            dg_ref[...] = jnp.zeros_like(dg_ref)

        @pl.when(m == 0)
        def _():
            dsh_ref[...] = jnp.zeros_like(dsh_ref)
            dsc_ref[...] = jnp.zeros_like(dsc_ref)

        dh = dh_ref[...]
        xv = x_ref[...]
        r = lax.rsqrt(jnp.mean(xv * xv, axis=1, keepdims=True) + EPS)
        xn = xv * r
        g = g_ref[...]
        one_sc = 1.0 + sc_ref[...]
        dsh_ref[...] += _rowsum(dh)
        dsc_ref[...] += _rowsum(dh * (xn * g))
        dg_ref[...] += _rowsum(dh * one_sc * xn)
        dxn = dh * (g * one_sc)
        dx_ref[...] = dxo_ref[...] + r * (dxn - xn * jnp.mean(dxn * xn, axis=1, keepdims=True))

    tile = pl.BlockSpec((None, tm, D), lambda b, m: (b, m, 0))
    vec = pl.BlockSpec((None, 1, D), lambda b, m: (b, 0, 0))
    one = pl.BlockSpec((1, D), lambda b, m: (0, 0))
    return pl.pallas_call(
        body, name="norm_mod_bwd", grid=(nb, s // tm),
        in_specs=[tile, tile, tile, one, vec],
        out_specs=[tile, vec, vec, one],
        out_shape=[jax.ShapeDtypeStruct(x.shape, F32), jax.ShapeDtypeStruct((nb, 1, D), F32),
                   jax.ShapeDtypeStruct((nb, 1, D), F32), jax.ShapeDtypeStruct((1, D), F32)],
        compiler_params=_params(("arbitrary", "arbitrary")),
    )(dh, x, dxo, gain, scale)


def _in_proj_bwd_w(ht, dproj, dep):
    nb, _, s = ht.shape
    tm = _row_tile(s, 2048)
    n_m = s // tm

    def body(ht_ref, dp_ref, dep_ref, o_ref, acc_ref):
        b, m = pl.program_id(1), pl.program_id(2)

        @pl.when((b == 0) & (m == 0))
        def _():
            acc_ref[...] = jnp.zeros_like(acc_ref)

        acc_ref[...] += _dot(ht_ref[...], dp_ref[...])

        @pl.when((b == nb - 1) & (m == n_m - 1))
        def _():
            o_ref[0] = acc_ref[:, :UNIT].astype(BF16)
            o_ref[1] = acc_ref[:, UNIT:].astype(BF16)

    return pl.pallas_call(
        body, name="in_proj_bwd_w", grid=(N_SEG, nb, n_m),
        in_specs=[pl.BlockSpec((None, D, tm), lambda j, b, m: (b, 0, m)),
                  pl.BlockSpec((None, None, tm, D), lambda j, b, m: (j, b, m, 0)),
                  pl.BlockSpec((8, LANES), lambda j, b, m: (0, 0))],
        out_specs=pl.BlockSpec((2, D, UNIT), lambda j, b, m: (j, 0, 0)),
        out_shape=jax.ShapeDtypeStruct((2 * N_SEG, D, UNIT), BF16),
        scratch_shapes=[pltpu.VMEM((D, D), F32)],
        compiler_params=_params(("arbitrary", "arbitrary", "arbitrary")),
    )(ht, dproj, dep)


def _mod_proj(c_all, w_mod, b_mod_mine):
    nl, _, ncol = w_mod.shape
    nbg = c_all.shape[0]

    def body(c_ref, w_ref, b_ref, o_ref):
        cv = c_ref[...]
        o_ref[...] = jnp.dot(cv * jax.nn.sigmoid(cv), w_ref[...], preferred_element_type=F32,
                             precision=lax.Precision.HIGHEST) + b_ref[...]

    return pl.pallas_call(
        body, name="mod_proj", grid=(nl,),
        in_specs=[pl.BlockSpec((nbg, D), lambda l: (0, 0)), pl.BlockSpec((None, D, ncol), lambda l: (l, 0, 0)),
                  pl.BlockSpec((None, 1, ncol), lambda l: (l, 0, 0))],
        out_specs=pl.BlockSpec((None, nbg, ncol), lambda l: (l, 0, 0)),
        out_shape=jax.ShapeDtypeStruct((nl, nbg, ncol), F32),
        compiler_params=_params(("arbitrary",)),
    )(c_all, w_mod, b_mod_mine)


def _mod_grad(c_all, dmod_all, dmod_mine):
    nl, nbg, ncol = dmod_mine.shape

    def body(c_ref, da_ref, dm_ref, gw_ref, gb_ref):
        cv = c_ref[...]
        gw_ref[...] = lax.dot_general(cv * jax.nn.sigmoid(cv), dm_ref[...], (((0,), (0,)), ((), ())),
                                      preferred_element_type=F32, precision=lax.Precision.HIGHEST)
        gb_ref[...] = _rowsum(da_ref[...])

    return pl.pallas_call(
        body, name="mod_grad", grid=(nl,),
        in_specs=[pl.BlockSpec((nbg, D), lambda l: (0, 0)), pl.BlockSpec((None, nbg, 3 * D), lambda l: (l, 0, 0)),
                  pl.BlockSpec((None, nbg, ncol), lambda l: (l, 0, 0))],
        out_specs=[pl.BlockSpec((None, D, ncol), lambda l: (l, 0, 0)),
                   pl.BlockSpec((None, 1, 3 * D), lambda l: (l, 0, 0))],
        out_shape=[jax.ShapeDtypeStruct((nl, D, ncol), F32), jax.ShapeDtypeStruct((nl, 1, 3 * D), F32)],
        compiler_params=_params(("arbitrary",)),
    )(c_all, dmod_all, dmod_mine)


def _adamw(parts, w, m, v, name, layer=None, prev=None):
    n_parts, n_u, n_r, cu = parts.shape
    assert w.shape[-2:] == (n_r, n_u * cu), (parts.shape, w.shape)
    tr = n_r
    for cand in (512, 256, 128):
        if n_r > cand and n_r % cand == 0:
            tr = cand
            break
    n_prev = 0 if prev is None else 4

    def body(p_ref, w_ref, m_ref, v_ref, *rest):
        g_ref, d_ref, nm_ref, nv_ref = rest[n_prev:]
        g = p_ref[0].astype(F32)
        for k in range(1, n_parts):
            g = g + p_ref[k].astype(F32)
        m2 = ADAM_B1 * m_ref[...] + (1.0 - ADAM_B1) * g
        v2 = ADAM_B2 * v_ref[...] + (1.0 - ADAM_B2) * (g * g)
        m_hat = m2 / (1.0 - ADAM_B1 ** ADAM_STEP)
        v_hat = v2 / (1.0 - ADAM_B2 ** ADAM_STEP)
        g_ref[...] = g
        d_ref[...] = -ADAM_LR * (m_hat / (jnp.sqrt(v_hat) + ADAM_EPS) + ADAM_WD * w_ref[...])
        nm_ref[...] = m2
        nv_ref[...] = v2

    if layer is None:
        tile = pl.BlockSpec((tr, cu), lambda u, i: (i, u))
    else:
        tile = pl.BlockSpec((None, tr, cu), lambda u, i: (layer, i, u))
    shp = jax.ShapeDtypeStruct(w.shape, F32)
    return pl.pallas_call(
        body, name=name, grid=(n_u, n_r // tr),
        in_specs=[pl.BlockSpec((n_parts, None, tr, cu), lambda u, i: (0, u, i, 0)), tile, tile, tile]
        + [pl.BlockSpec(memory_space=pl.ANY)] * n_prev,
        out_specs=[tile, tile, tile, tile], out_shape=[shp, shp, shp, shp],
        input_output_aliases={4 + k: k for k in range(n_prev)},
        compiler_params=_params(("arbitrary", "arbitrary")),
    )(parts, w, m, v, *(prev or ()))


def _gathered_cols(g, inner):
    k = len(inner)
    perm = tuple(range(1, k + 1)) + (0, k + 1)
    t = jnp.transpose(g, perm)
    return t.reshape(tuple(inner) + (g.shape[0] * g.shape[-1],))


def _pair_blocks(wh):
    z = jnp.zeros((8, 64, 64), wh.dtype)
    w2 = wh.reshape(8, 2, 64, 64)
    top = jnp.concatenate([w2[:, 0], z], axis=2)
    bot = jnp.concatenate([z, w2[:, 1]], axis=2)
    return jnp.concatenate([top, bot], axis=1).astype(BF16)


def _unpair_blocks(g):
    return jnp.stack([g[:, :64, :64], g[:, 64:, 64:]], axis=1).reshape(16, 64, 64)


FLAT_ROWS = 512


def _pack_rows(arrays, lead=0):
    parts = [a.reshape(a.shape[:lead] + (-1, LANES)) for a in arrays]
    rows = jnp.concatenate(parts, axis=lead)
    pad = [(0, 0)] * rows.ndim
    pad[lead] = (0, (-rows.shape[lead]) % FLAT_ROWS)
    return jnp.pad(rows, pad)


def kernel(x, c, norm_gain, w_mod, b_mod, w_in, w_out, conv_a_w, sgu_w, sgu_b, lru_conv_w, lru_conv_b, lru_wa, lru_ba, lru_wx, lru_bx, lru_lambda, final_gain, loss_target, m_norm_gain, m_w_mod, m_b_mod, m_w_in, m_w_out, m_conv_a_w, m_sgu_w, m_sgu_b, m_lru_conv_w, m_lru_conv_b, m_lru_wa, m_lru_ba, m_lru_wx, m_lru_bx, m_lru_lambda, m_final_gain, v_norm_gain, v_w_mod, v_b_mod, v_w_in, v_w_out, v_conv_a_w, v_sgu_w, v_sgu_b, v_lru_conv_w, v_lru_conv_b, v_lru_wa, v_lru_ba, v_lru_wx, v_lru_bx, v_lru_lambda, v_final_gain):
    nl = w_in.shape[0]
    nb, s, _ = x.shape
    me = _my_index()
    mod_cols = w_mod.shape[2]


    small = jnp.concatenate([c.reshape(-1, LANES), conv_a_w.reshape(-1, LANES), lru_conv_w.reshape(-1, LANES)])
    n_c, n_ca = nb * D // LANES, nl * 3
    n_small = small.shape[0]
    small = jnp.pad(small, ((0, (-n_small) % 8), (0, 0)))
    small_all, _ = _all_gather(small, "gather_small")
    c_all = small_all[:, :n_c].reshape(N_DEV * nb, D)

    w_in_b = [w_in[0].astype(BF16)] + list(w_in[1:].astype(BF16))

    def start_w_in(l, dep):
        return _split_start(w_in_b[l], _peers_same_core, False, "gather_w_in_start", dep)

    conv_a_full = _gathered_cols(small_all[:, n_c:n_c + n_ca].reshape(N_DEV, nl, 3, LANES), (nl, 3))
    lru_conv_full = _gathered_cols(small_all[:, n_c + n_ca:n_small].reshape(N_DEV, nl, 4, LANES), (nl, 4))

    def gathered_w_in(started, after):
        block, land = _split_wait(started, after, _peers_same_core, False, False, "gather_w_in_wait")
        return _gather_finish(block, land, "gather_w_in_finish")

    sgu_b_lanes = jnp.broadcast_to(sgu_b[..., None], sgu_b.shape + (LANES,))
    mws = []
    for l in range(nl):
        mws.append((conv_a_full[l], sgu_w[l], sgu_b_lanes[l], lru_conv_full[l], lru_conv_b[l][None, :],
                    _pair_blocks(lru_wa[l]), _pair_blocks(lru_wx[l]), lru_ba[l].reshape(1, D),
                    lru_bx[l].reshape(1, D), lru_lambda[l][None, :]))

    rep_names = ["sgu_w", "sgu_b", "lru_conv_b", "lru_wa", "lru_ba", "lru_wx", "lru_bx", "lru_lambda"]
    rep_w = dict(sgu_w=sgu_w, sgu_b=sgu_b, lru_conv_b=lru_conv_b, lru_wa=lru_wa, lru_ba=lru_ba,
                 lru_wx=lru_wx, lru_bx=lru_bx, lru_lambda=lru_lambda)
    rep_m = dict(sgu_w=m_sgu_w, sgu_b=m_sgu_b, lru_conv_b=m_lru_conv_b, lru_wa=m_lru_wa,
                 lru_ba=m_lru_ba, lru_wx=m_lru_wx, lru_bx=m_lru_bx, lru_lambda=m_lru_lambda)
    rep_v = dict(sgu_w=v_sgu_w, sgu_b=v_sgu_b, lru_conv_b=v_lru_conv_b, lru_wa=v_lru_wa,
                 lru_ba=v_lru_ba, lru_wx=v_lru_wx, lru_bx=v_lru_bx, lru_lambda=v_lru_lambda)

    rep_w_all, rep_m_all, rep_v_all = [_pack_rows([src[n] for n in rep_names], lead=1)
                                       for src in (rep_w, rep_m, rep_v)]
    early = [rep_w_all, rep_m_all, rep_v_all] + w_in_b[1:] + [a for mw in mws for a in mw]

    b_mod_mine = lax.dynamic_slice_in_dim(b_mod, me * mod_cols, mod_cols, axis=1)[:, None, :]
    mod_mine = _mod_proj(c_all, w_mod, b_mod_mine)
    mod_all, mod_token = _all_gather(mod_mine.reshape(nl * N_DEV * nb, mod_cols), "gather_mod", dep=w_in_b[0])
    ici = {0: start_w_in(0, mod_token)}
    mod_full = _gathered_cols(mod_all.reshape(N_DEV, nl, N_DEV * nb, mod_cols), (nl, N_DEV * nb))
    mod_loc = lax.dynamic_slice_in_dim(mod_full, me * nb, nb, axis=1)
    shift, scale, gate = [mod_loc[:, :, j * D:(j + 1) * D][:, :, None, :] for j in range(3)]

    xs, hts, projs, mergeds, states, wg = [], [], [], [], [], []
    xl = x
    d2d = {}
    wo_started = _split_start(w_out.astype(BF16).reshape(nl * (D // N_DEV), D), _peers_all, False,
                              "gather_w_out_start", ici[0][4])
    wo = None
    h, ht = _norm_mod(xl, _after(norm_gain[0][None, :], wo_started[4]), shift[0], scale[0])
    for l in range(nl):
        if l == 0:
            wg_l, token = gathered_w_in(ici[0], [h] + early)
            ici[1] = start_w_in(1, token)
            dep = ici[1][4]
        else:
            wg_l = _forward_wait(d2d[l], h, "gather_w_in_d2d_wait")
            dep = d2d[l][4]
        wg.append(wg_l)
        proj = _in_proj(h, wg_l, dep)
        merged, *st = _mixer_fwd(proj, mws[l])
        xs.append(xl), hts.append(ht), projs.append(proj), mergeds.append(merged), states.append(st)
        gate_l = gate[l]
        if l + 1 < nl:
            block, land = _split_wait(ici[l + 1], merged, _peers_same_core, False, False, "gather_w_in_wait")
            d2d[l + 1] = _forward_start(block, land, "gather_w_in_d2d_start")
            gate_l = _after(gate_l, d2d[l + 1][4])
            if l + 2 < nl:
                ici[l + 2] = start_w_in(l + 2, d2d[l + 1][4])
                gate_l = _after(gate_l, ici[l + 2][4])
        if wo is None:
            _, wo_all = _split_wait(wo_started, merged, _peers_all, False, True, "gather_w_out_wait")
            wo = jnp.transpose(wo_all.reshape(N_DEV, nl, D // N_DEV, D), (1, 0, 2, 3)).reshape(nl, D, D)
        if l + 1 < nl:
            xl, h, ht = _out_proj_norm(xl, merged, wo[l], gate_l, norm_gain[l + 1][None, :], shift[l + 1], scale[l + 1])
        else:
            xl = _out_proj(xl, merged, wo[l], gate_l)

    loss_row, dx, g_final = _loss_head(xl, final_gain[None, :], loss_target)

    res_big, g_conv = {}, [None] * nl
    dmods = [None] * nl

    def finish_exchange(pending, after):
        l, h_in, h_out, h_rep = pending
        r_in, r_out, r_rep = _multi_wait([h_in, h_out, h_rep], after, "scatter_wait")
        res_big["w_in"] = _adamw(r_in, w_in, m_w_in, v_w_in, "adamw_w_in", l, res_big.get("w_in"))
        res_big["w_out"] = _adamw(r_out, w_out, m_w_out, v_w_out, "adamw_w_out", l, res_big.get("w_out"))
        res_big["rep"] = _adamw(r_rep[:, None], rep_w_all, rep_m_all, rep_v_all, "adamw_rep", l, res_big.get("rep"))

    pending = None
    g_gains = [None] * nl
    for l in reversed(range(nl)):
        dmerged, gw_out, dgate = _out_proj_bwd(dx, mergeds[l], wo[l], gate[l])
        dproj, g_caw, g_sw, g_sb, g_lcw, g_vec, g_wa, g_wx = _mixer_bwd(projs[l], dmerged, *states[l], mws[l])
        g_conv[l] = (g_caw, g_lcw)
        rep_g = dict(
            sgu_w=g_sw, sgu_b=g_sb[:, :, 0], lru_conv_b=g_vec[0],
            lru_wa=_unpair_blocks(g_wa), lru_ba=g_vec[1].reshape(16, 64), lru_wx=_unpair_blocks(g_wx),
            lru_bx=g_vec[2].reshape(16, 64), lru_lambda=g_vec[3])
        rep_block = _pack_rows([rep_g[n] for n in rep_names])
        (h_out, h_rep), token = _multi_start([(gw_out.reshape(N_DEV, 1, D // N_DEV, D), True), (rep_block, False)],
                                             "scatter_small_start")
        gw_in = _in_proj_bwd_w(hts[l], dproj, token)
        (h_in,), token = _multi_start([(gw_in.reshape(N_DEV, UNITS_PER_DEV, D, UNIT), True)], "scatter_w_in_start")
        started = (l, h_in, h_out, h_rep)
        dh = _in_proj_bwd_h(dproj, wg[l], token)
        dx, dshift, dscale, g_gain = _norm_mod_bwd(dh, xs[l], dx, norm_gain[l][None, :], scale[l])
        g_gains[l] = g_gain
        dmods[l] = jnp.concatenate([dshift, dscale, dgate], axis=2)[:, 0, :]
        if pending is not None:
            finish_exchange(pending, dx)
        pending = started

    conv_parts = jnp.concatenate(
        [jnp.stack([g_conv[l][0] for l in range(nl)]).reshape(nl * 3, N_DEV, LANES),
         jnp.stack([g_conv[l][1] for l in range(nl)]).reshape(nl * 4, N_DEV, LANES)], axis=0)
    conv_parts = jnp.transpose(conv_parts, (1, 0, 2))[:, None]
    conv_recv = _all_to_all(conv_parts, "scatter_conv")

    dmod_loc = jnp.stack(dmods).reshape(nl * nb, 3 * D)
    gain_rows = jnp.pad(jnp.concatenate(g_gains + [g_final, loss_row], axis=0),
                        ((0, (-(nl + 2)) % 8), (0, 2 * D)))
    tail_g, _ = _all_gather(jnp.concatenate([dmod_loc, gain_rows], axis=0), "gather_dmod",
                            dep=[res_big[k][3] for k in ("w_in", "w_out", "rep")])
    loss = jnp.sum(tail_g[:, nl * nb + nl + 1, 0])
    dmod_g = tail_g[:, :nl * nb]
    gain_parts = tail_g[:, nl * nb:nl * nb + nl + 1, :D][:, None]
    gain_cat = lambda a, b: jnp.concatenate([a, b[None, :]], axis=0)
    res_gain = _adamw(gain_parts, gain_cat(norm_gain, final_gain), gain_cat(m_norm_gain, m_final_gain),
                      gain_cat(v_norm_gain, v_final_gain), "adamw_gain")
    dmod_all = jnp.transpose(dmod_g.reshape(N_DEV, nl, nb, 3 * D), (1, 0, 2, 3)).reshape(nl, N_DEV * nb, 3 * D)
    dmod_mine = lax.dynamic_slice_in_dim(dmod_all, me * mod_cols, mod_cols, axis=2)
    gw_mod, gb_mod = _mod_grad(c_all, dmod_all, dmod_mine)
    res_w_mod = _adamw(gw_mod.reshape(1, 1, nl * D, mod_cols), w_mod.reshape(nl * D, mod_cols),
                       m_w_mod.reshape(nl * D, mod_cols), v_w_mod.reshape(nl * D, mod_cols), "adamw_w_mod")
    res_w_mod = [a.reshape(nl, D, mod_cols) for a in res_w_mod]
    res_b_mod = _adamw(gb_mod.reshape(1, 1, nl, 3 * D), b_mod, m_b_mod, v_b_mod, "adamw_b_mod")
    finish_exchange(pending, res_b_mod[1])

    cat = lambda a, b: jnp.concatenate([a.reshape(nl * 3, LANES), b.reshape(nl * 4, LANES)], axis=0)
    res_conv = _adamw(conv_recv, cat(conv_a_w, lru_conv_w), cat(m_conv_a_w, m_lru_conv_w),
                      cat(v_conv_a_w, v_lru_conv_w), "adamw_conv")
    res_conv_a = [a[:nl * 3].reshape(nl, 3, LANES) for a in res_conv]
    res_lru_conv = [a[nl * 3:].reshape(nl, 4, LANES) for a in res_conv]

    res_rep = []
    for k in range(4):
        off, d = 0, {}
        for n in rep_names:
            n_rows = rep_w[n][0].size // LANES
            d[n] = res_big["rep"][k][:, off:off + n_rows].reshape(rep_w[n].shape)
            off += n_rows
        res_rep.append(d)

    def leaf(k, name):
        if name == "norm_gain":
            return res_gain[k][:nl]
        if name == "final_gain":
            return res_gain[k][nl]
        if name == "w_mod":
            return res_w_mod[k]
        if name == "b_mod":
            return res_b_mod[k]
        if name in ("w_in", "w_out"):
            return res_big[name][k]
        if name == "conv_a_w":
            return res_conv_a[k]
        if name == "lru_conv_w":
            return res_lru_conv[k]
        return res_rep[k][name]

    order = ["norm_gain", "w_mod", "b_mod", "w_in", "w_out", "conv_a_w", "sgu_w", "sgu_b", "lru_conv_w",
             "lru_conv_b", "lru_wa", "lru_ba", "lru_wx", "lru_bx", "lru_lambda", "final_gain"]
    outs = [loss, dx]
    for k in range(4):
        outs += [leaf(k, n) for n in order]
    return tuple(outs)
```

```python
import functools

import jax
import jax.numpy as jnp
from jax import lax
from jax.experimental import pallas as pl
from jax.experimental.pallas import tpu as pltpu

F32 = jnp.float32
BF16 = jnp.bfloat16

D = 1024
N_DEV = 8
N_SEG = 12
LANES = 128
SUBLANES = 8
CHUNK = 128
HALO = 16
FWD_CHUNKS_PER_TRIP = 4
BWD_CHUNKS_PER_TRIP = 8
UNIT = 512
UNITS_PER_DEV = 3
EPS = 1e-6
LRU_C = 8.0
ADAM_LR, ADAM_B1, ADAM_B2, ADAM_EPS, ADAM_WD, ADAM_STEP = 0.001, 0.9, 0.999, 1e-08, 0.01, 10
VMEM_LIMIT = 56 * 1024 * 1024

AX, AB, AC, AZ, SU, SV, SZ, RX, RZ, GA, GS, GR = range(N_SEG)
MESH = pl.DeviceIdType.MESH


def _params(sem=None):
    return pltpu.CompilerParams(dimension_semantics=sem, vmem_limit_bytes=VMEM_LIMIT)


def _my_index():
    return 4 * lax.axis_index("x") + 2 * lax.axis_index("y") + lax.axis_index("c")


def _all_gather(block, name, dep=None):
    deps = [] if dep is None else list(dep) if isinstance(dep, (list, tuple)) else [dep]

    def body(x_ref, *refs):
        out_ref, token, send_sems, recv_sems, local_sem = refs[-5:]
        x, y, c = lax.axis_index("x"), lax.axis_index("y"), lax.axis_index("c")
        me, sibling = (x, y, c), (x, y, 1 - c)
        chips = [(1 - x, y), (x, 1 - y), (1 - x, 1 - y)]
        token[...] = jnp.zeros_like(token)

        def rows(px, py, pc):
            return out_ref.at[4 * px + 2 * py + pc]

        def copy(k, blk, to, src=None):
            return pltpu.make_async_remote_copy(
                src_ref=rows(*blk) if src is None else src, dst_ref=rows(*blk),
                send_sem=send_sems.at[k], recv_sem=recv_sems.at[k], device_id=to, device_id_type=MESH)

        mine = pltpu.make_async_copy(x_ref, rows(*me), local_sem)
        mine.start()
        first = [copy(0, me, sibling, src=x_ref)]
        first += [copy(1 + j, me, (*chip, c), src=x_ref) for j, chip in enumerate(chips)]
        for cp in first:
            cp.start()
        passed = [copy(4 + j, (*chip, c), sibling) for j, chip in enumerate(chips)]
        for j, chip in enumerate(chips):
            copy(1 + j, (*chip, c), me).wait_recv()
            passed[j].start()
        copy(0, sibling, me).wait_recv()
        for j, chip in enumerate(chips):
            copy(4 + j, (*chip, 1 - c), me).wait_recv()
        for cp in first + passed:
            cp.wait_send()
        mine.wait()

    return pl.pallas_call(
        body, name=name,
        out_shape=[jax.ShapeDtypeStruct((N_DEV,) + block.shape, block.dtype), jax.ShapeDtypeStruct((8, LANES), F32)],
        in_specs=[pl.BlockSpec(memory_space=pltpu.VMEM)]
        + [pl.BlockSpec(memory_space=pl.ANY)] * len(deps),
        out_specs=[pl.BlockSpec(memory_space=pl.ANY), pl.BlockSpec(memory_space=pltpu.VMEM)],
        scratch_shapes=[pltpu.SemaphoreType.DMA((7,)), pltpu.SemaphoreType.DMA((7,)), pltpu.SemaphoreType.DMA],
    )(block, *deps)


def _all_to_all(blocks, name):
    def body(x_ref, out_ref, send_sems, recv_sems, local_sem):
        x, y, c = lax.axis_index("x"), lax.axis_index("y"), lax.axis_index("c")
        my = 4 * x + 2 * y + c
        mine = pltpu.make_async_copy(x_ref.at[my], out_ref.at[my], local_sem)
        mine.start()
        peers = []
        for r in range(1, N_DEV):
            px = 1 - x if r & 4 else x
            py = 1 - y if r & 2 else y
            pc = 1 - c if r & 1 else c
            peers.append((r - 1, 4 * px + 2 * py + pc, (px, py, pc)))

        def copy(k, src_slot, dst_slot, to):
            return pltpu.make_async_remote_copy(
                src_ref=x_ref.at[src_slot], dst_ref=out_ref.at[dst_slot],
                send_sem=send_sems.at[k], recv_sem=recv_sems.at[k], device_id=to, device_id_type=MESH)

        sends = [copy(k, pid, my, to) for k, pid, to in peers]
        for cp in sends:
            cp.start()
        for k, pid, to in peers:
            copy(k, pid, pid, to).wait_recv()
        for cp in sends:
            cp.wait_send()
        mine.wait()

    return pl.pallas_call(
        body, name=name,
        out_shape=jax.ShapeDtypeStruct(blocks.shape, blocks.dtype),
        in_specs=[pl.BlockSpec(memory_space=pltpu.VMEM)],
        out_specs=pl.BlockSpec(memory_space=pl.ANY),
        scratch_shapes=[pltpu.SemaphoreType.DMA((7,)), pltpu.SemaphoreType.DMA((7,)), pltpu.SemaphoreType.DMA],
    )(blocks)


_HBM = pl.BlockSpec(memory_space=pltpu.HBM)
_SEM = pl.BlockSpec(memory_space=pltpu.SEMAPHORE)
_EFFECT = pltpu.SideEffectType.DATAFLOW_SIDE_EFFECTING


def _peers_all(x, y, c):
    out = []
    for r in range(1, N_DEV):
        px = 1 - x if r & 4 else x
        py = 1 - y if r & 2 else y
        pc = 1 - c if r & 1 else c
        out.append((r - 1, 4 * px + 2 * py + pc, (px, py, pc)))
    return out


def _peers_same_core(x, y, c):
    return [(k, 4 * px + 2 * py + c, (px, py, c))
            for k, (px, py) in enumerate([(1 - x, y), (x, 1 - y), (1 - x, 1 - y)])]


def _split_start(src, peers_fn, scatter, name, dep=None):
    blk = src.shape[1:] if scatter else src.shape
    land_shape = (N_DEV,) + tuple(blk)
    n = len(peers_fn(0, 0, 0))
    deps = [] if dep is None else [dep]

    def body(x_ref, land_ref, *rest):
        send_sems, recv_sems, x_thru, land_thru, token = rest[len(deps):]
        x, y, c = lax.axis_index("x"), lax.axis_index("y"), lax.axis_index("c")
        my = 4 * x + 2 * y + c
        for k, pid, to in peers_fn(x, y, c):
            pltpu.make_async_remote_copy(
                src_ref=x_ref.at[pid] if scatter else x_ref, dst_ref=land_ref.at[my],
                send_sem=send_sems.at[k], recv_sem=recv_sems.at[k], device_id=to, device_id_type=MESH).start()
        token[...] = jnp.zeros_like(token)

    return pl.pallas_call(
        body, name=name,
        out_shape=(pltpu.SemaphoreType.DMA((n,)), pltpu.SemaphoreType.DMA((n,)),
                   pltpu.HBM(src.shape, src.dtype), pltpu.HBM(land_shape, src.dtype),
                   jax.ShapeDtypeStruct((8, LANES), F32)),
        in_specs=(_HBM, _HBM) + (pl.BlockSpec(memory_space=pl.ANY),) * len(deps),
        out_specs=(_SEM, _SEM, _HBM, _HBM, pl.BlockSpec(memory_space=pltpu.VMEM)),
        input_output_aliases={0: 2, 1: 3},
        compiler_params=pltpu.CompilerParams(has_side_effects=_EFFECT),
    )(pltpu.with_memory_space_constraint(src, pltpu.HBM),
      pltpu.with_memory_space_constraint(lax.empty(land_shape, src.dtype), pltpu.HBM), *deps)


def _split_wait(handles, after, peers_fn, scatter, own, name):
    send_sems, recv_sems, src_thru, land_thru, _ = handles
    blk = land_thru.shape[1:]
    after = list(after) if isinstance(after, (list, tuple)) else [after]

    def body(x_ref, land_ref, send_sems, recv_sems, *rest):
        stage = rest[len(after) + 2:]
        x, y, c = lax.axis_index("x"), lax.axis_index("y"), lax.axis_index("c")
        if own:
            my = 4 * x + 2 * y + c
            mine = _staged_copy(x_ref.at[my] if scatter else x_ref, land_ref.at[my], *stage)
        for k, pid, to in peers_fn(x, y, c):
            cp = pltpu.make_async_remote_copy(
                src_ref=x_ref.at[pid] if scatter else x_ref, dst_ref=land_ref.at[pid],
                send_sem=send_sems.at[k], recv_sem=recv_sems.at[k], device_id=to, device_id_type=MESH)
            cp.wait_send()
            cp.wait_recv()
        if own:
            mine.wait()

    return pl.pallas_call(
        body, name=name,
        out_shape=(pltpu.HBM(src_thru.shape, src_thru.dtype), pltpu.HBM(land_thru.shape, land_thru.dtype)),
        in_specs=(_HBM, _HBM, _SEM, _SEM) + (pl.BlockSpec(memory_space=pl.ANY),) * len(after),
        out_specs=(_HBM, _HBM),
        input_output_aliases={0: 0, 1: 1},
        scratch_shapes=[pltpu.VMEM(blk, land_thru.dtype), pltpu.SemaphoreType.DMA((2,))] if own else [],
        compiler_params=pltpu.CompilerParams(has_side_effects=_EFFECT, vmem_limit_bytes=VMEM_LIMIT),
    )(src_thru, land_thru, send_sems, recv_sems, *after)


def _staged_copy(src_ref, dst_ref, buf, sems):
    leg = pltpu.make_async_copy(src_ref, buf, sems.at[0])
    leg.start()
    leg.wait()
    leg = pltpu.make_async_copy(buf, dst_ref, sems.at[1])
    leg.start()
    return leg


def _multi_start(srcs, name, dep=None):
    n_src = len(srcs)
    lands = [(N_DEV,) + tuple(a.shape[1:] if sc else a.shape) for a, sc in srcs]
    deps = [] if dep is None else [dep]

    def body(*refs):
        ins, outs = refs[:2 * n_src], refs[2 * n_src + len(deps):]
        x, y, c = lax.axis_index("x"), lax.axis_index("y"), lax.axis_index("c")
        my = 4 * x + 2 * y + c
        for i, (_, scatter) in enumerate(srcs):
            x_ref, land_ref, send_sems, recv_sems = ins[2 * i], ins[2 * i + 1], outs[4 * i], outs[4 * i + 1]
            for k, pid, to in _peers_all(x, y, c):
                pltpu.make_async_remote_copy(
                    src_ref=x_ref.at[pid] if scatter else x_ref, dst_ref=land_ref.at[my],
                    send_sem=send_sems.at[k], recv_sem=recv_sems.at[k], device_id=to, device_id_type=MESH).start()
        outs[-1][...] = jnp.zeros_like(outs[-1])

    out_shape, out_specs, operands = [], [], []
    for (a, _), land in zip(srcs, lands):
        out_shape += [pltpu.SemaphoreType.DMA((N_DEV - 1,)), pltpu.SemaphoreType.DMA((N_DEV - 1,)),
                      pltpu.HBM(a.shape, a.dtype), pltpu.HBM(land, a.dtype)]
        out_specs += [_SEM, _SEM, _HBM, _HBM]
        operands += [pltpu.with_memory_space_constraint(a, pltpu.HBM),
                     pltpu.with_memory_space_constraint(lax.empty(land, a.dtype), pltpu.HBM)]
    res = pl.pallas_call(
        body, name=name,
        out_shape=tuple(out_shape) + (jax.ShapeDtypeStruct((8, LANES), F32),),
        in_specs=(_HBM,) * (2 * n_src) + (pl.BlockSpec(memory_space=pl.ANY),) * len(deps),
        out_specs=tuple(out_specs) + (pl.BlockSpec(memory_space=pltpu.VMEM),),
        input_output_aliases={2 * i + j: 4 * i + 2 + j for i in range(n_src) for j in range(2)},
        compiler_params=pltpu.CompilerParams(has_side_effects=_EFFECT),
    )(*operands, *deps)
    return [tuple(res[4 * i:4 * i + 4]) + (srcs[i][1],) for i in range(n_src)], res[-1]


def _multi_wait(started, after, name):
    n_src = len(started)
    after = list(after) if isinstance(after, (list, tuple)) else [after]

    def body(*refs):
        ins = refs[:4 * n_src]
        stage = refs[4 * n_src + len(after) + 2 * n_src:]
        x, y, c = lax.axis_index("x"), lax.axis_index("y"), lax.axis_index("c")
        my = 4 * x + 2 * y + c
        pending = []
        for i, h in enumerate(started):
            x_ref, land_ref, send_sems, recv_sems = ins[4 * i:4 * i + 4]
            scatter = h[4]
            pending.append(_staged_copy(x_ref.at[my] if scatter else x_ref, land_ref.at[my],
                                        stage[2 * i], stage[2 * i + 1]))
            for k, pid, to in _peers_all(x, y, c):
                cp = pltpu.make_async_remote_copy(
                    src_ref=x_ref.at[pid] if scatter else x_ref, dst_ref=land_ref.at[pid],
                    send_sem=send_sems.at[k], recv_sem=recv_sems.at[k], device_id=to, device_id_type=MESH)
                cp.wait_send()
                cp.wait_recv()
        for leg in pending:
            leg.wait()

    operands, out_shape, scratch = [], [], []
    for send, recv, src_thru, land_thru, _ in started:
        operands += [src_thru, land_thru, send, recv]
        out_shape += [pltpu.HBM(src_thru.shape, src_thru.dtype), pltpu.HBM(land_thru.shape, land_thru.dtype)]
        scratch += [pltpu.VMEM(land_thru.shape[1:], land_thru.dtype), pltpu.SemaphoreType.DMA((2,))]
    res = pl.pallas_call(
        body, name=name,
        out_shape=tuple(out_shape),
        in_specs=(_HBM, _HBM, _SEM, _SEM) * n_src + (pl.BlockSpec(memory_space=pl.ANY),) * len(after),
        out_specs=(_HBM,) * (2 * n_src),
        input_output_aliases={4 * i + j: 2 * i + j for i in range(n_src) for j in range(2)},
        scratch_shapes=scratch,
        compiler_params=pltpu.CompilerParams(has_side_effects=_EFFECT, vmem_limit_bytes=VMEM_LIMIT),
    )(*operands, *after)
    return [res[2 * i + 1] for i in range(n_src)]


def _gather_finish(block, land, name):
    def body(x_ref, land_ref, out_ref, token, send_sems, recv_sems, buf, local_sems):
        x, y, c = lax.axis_index("x"), lax.axis_index("y"), lax.axis_index("c")
        my, sib_id, sibling = 4 * x + 2 * y + c, 4 * x + 2 * y + 1 - c, (x, y, 1 - c)
        token[...] = jnp.zeros_like(token)

        def copy(k, slot, src=None):
            return pltpu.make_async_remote_copy(
                src_ref=land_ref.at[slot] if src is None else src, dst_ref=out_ref.at[slot],
                send_sem=send_sems.at[k], recv_sem=recv_sems.at[k], device_id=sibling, device_id_type=MESH)

        chips = _peers_same_core(x, y, c)
        sends = [copy(0, my, src=x_ref)] + [copy(1 + k, pid) for k, pid, _ in chips]
        for cp in sends:
            cp.start()
        mine = _staged_copy(x_ref, out_ref.at[my], buf, local_sems)
        copy(0, sib_id).wait_recv()
        for k, pid, _ in chips:
            copy(1 + k, pid + 1 - 2 * c).wait_recv()
        for cp in sends:
            cp.wait_send()
        mine.wait()

    return pl.pallas_call(
        body, name=name,
        out_shape=[jax.ShapeDtypeStruct(land.shape, land.dtype), jax.ShapeDtypeStruct((8, LANES), F32)],
        in_specs=[pl.BlockSpec(memory_space=pl.ANY), pl.BlockSpec(memory_space=pl.ANY)],
        out_specs=[pl.BlockSpec(memory_space=pl.ANY), pl.BlockSpec(memory_space=pltpu.VMEM)],
        input_output_aliases={1: 0},
        scratch_shapes=[pltpu.SemaphoreType.DMA((4,)), pltpu.SemaphoreType.DMA((4,)),
                        pltpu.VMEM(block.shape, block.dtype), pltpu.SemaphoreType.DMA((2,))],
        compiler_params=pltpu.CompilerParams(vmem_limit_bytes=VMEM_LIMIT),
    )(block, land)


def _forward_start(block, land, name):
    def body(x_ref, land_ref, send_sems, recv_sems, x_thru, land_thru, token):
        x, y, c = lax.axis_index("x"), lax.axis_index("y"), lax.axis_index("c")
        my, sibling = 4 * x + 2 * y + c, (x, y, 1 - c)
        slots = [(0, my, x_ref)] + [(1 + k, pid, land_ref.at[pid]) for k, pid, _ in _peers_same_core(x, y, c)]
        for k, slot, src in slots:
            pltpu.make_async_remote_copy(
                src_ref=src, dst_ref=land_ref.at[slot], send_sem=send_sems.at[k], recv_sem=recv_sems.at[k],
                device_id=sibling, device_id_type=MESH).start()
        token[...] = jnp.zeros_like(token)

    return pl.pallas_call(
        body, name=name,
        out_shape=(pltpu.SemaphoreType.DMA((4,)), pltpu.SemaphoreType.DMA((4,)),
                   pltpu.HBM(block.shape, block.dtype), pltpu.HBM(land.shape, land.dtype),
                   jax.ShapeDtypeStruct((8, LANES), F32)),
        in_specs=(_HBM, _HBM),
        out_specs=(_SEM, _SEM, _HBM, _HBM, pl.BlockSpec(memory_space=pltpu.VMEM)),
        input_output_aliases={0: 2, 1: 3},
        compiler_params=pltpu.CompilerParams(has_side_effects=_EFFECT),
    )(block, land)


def _forward_wait(handles, after, name):
    send_sems, recv_sems, block_thru, land_thru, _ = handles

    def body(x_ref, land_ref, send_sems, recv_sems, after_ref, x_dead, got_ref, buf, local_sems):
        x, y, c = lax.axis_index("x"), lax.axis_index("y"), lax.axis_index("c")
        my, sib_id, sibling = 4 * x + 2 * y + c, 4 * x + 2 * y + 1 - c, (x, y, 1 - c)
        mine = _staged_copy(x_ref, land_ref.at[my], buf, local_sems)
        slots = [(0, my, sib_id)] + [(1 + k, pid, pid + 1 - 2 * c) for k, pid, _ in _peers_same_core(x, y, c)]
        for k, sent, got in slots:
            cp = pltpu.make_async_remote_copy(
                src_ref=land_ref.at[sent], dst_ref=land_ref.at[got], send_sem=send_sems.at[k],
                recv_sem=recv_sems.at[k], device_id=sibling, device_id_type=MESH)
            cp.wait_send()
            cp.wait_recv()
        mine.wait()

    return pl.pallas_call(
        body, name=name,
        out_shape=(pltpu.HBM(block_thru.shape, block_thru.dtype), pltpu.HBM(land_thru.shape, land_thru.dtype)),
        in_specs=(_HBM, _HBM, _SEM, _SEM, pl.BlockSpec(memory_space=pl.ANY)),
        out_specs=(_HBM, _HBM),
        input_output_aliases={0: 0, 1: 1},
        scratch_shapes=[pltpu.VMEM(block_thru.shape, block_thru.dtype), pltpu.SemaphoreType.DMA((2,))],
        compiler_params=pltpu.CompilerParams(has_side_effects=_EFFECT, vmem_limit_bytes=VMEM_LIMIT),
    )(block_thru, land_thru, send_sems, recv_sems, after)[1]


def _after(v, token):
    return v + token[0, 0].astype(v.dtype)


def _dsilu(s, silu):
    return s + silu * (1.0 - s)


def _log1p(x):
    u = 1.0 + x
    d = u - 1.0
    return jnp.where(d == 0.0, x, jnp.log(u) * (x / jnp.where(d == 0.0, 1.0, d)))


def _softplus_neg(lam):
    return jnp.maximum(-lam, 0.0) + _log1p(jnp.exp(-jnp.abs(lam)))


def _neg_expm1(y, exp_y):
    poly = -y * (1.0 + y * (0.5 + y * (1.0 / 6.0 + y * (1.0 / 24.0))))
    return jnp.where(y > -0.05, poly, 1.0 - exp_y)


def _sigmoid(x):
    return 0.5 * jnp.tanh(0.5 * x) + 0.5


def _shift_dn(cur, prev, k):
    ext = jnp.concatenate([prev, cur], axis=0)
    return pltpu.roll(ext, k, 0)[HALO:, :]


def _shift_up(cur, nxt, k):
    n = cur.shape[0]
    ext = jnp.concatenate([cur, nxt], axis=0)
    return pltpu.roll(ext, n + HALO - k, 0)[:n, :]


def _scan_fwd(a, b, h_prev):
    groups = a.shape[0] // SUBLANES
    a3 = a.reshape(groups, SUBLANES, LANES)
    b3 = b.reshape(groups, SUBLANES, LANES)
    row = lax.broadcasted_iota(jnp.int32, a3.shape, 1)
    k = 1
    while k < SUBLANES:
        a_sh = jnp.where(row >= k, pltpu.roll(a3, k, 1), 1.0)
        b_sh = jnp.where(row >= k, pltpu.roll(b3, k, 1), 0.0)
        b3 = a3 * b_sh + b3
        a3 = a3 * a_sh
        k *= 2
    carry = h_prev[HALO - 1:HALO, :]
    out = []
    for i in range(groups):
        hg = b3[i] + a3[i] * carry
        out.append(hg)
        carry = hg[SUBLANES - 1:SUBLANES, :]
    return jnp.concatenate(out, axis=0)


def _scan_rev(a_next, g, lam_next):
    groups = g.shape[0] // SUBLANES
    a3 = a_next.reshape(groups, SUBLANES, LANES)
    g3 = g.reshape(groups, SUBLANES, LANES)
    row = lax.broadcasted_iota(jnp.int32, a3.shape, 1)
    k = 1
    while k < SUBLANES:
        ok = row < SUBLANES - k
        a_sh = jnp.where(ok, pltpu.roll(a3, SUBLANES - k, 1), 1.0)
        g_sh = jnp.where(ok, pltpu.roll(g3, SUBLANES - k, 1), 0.0)
        g3 = g3 + a3 * g_sh
        a3 = a3 * a_sh
        k *= 2
    carry = lam_next[0:1, :]
    out = [None] * groups
    for i in reversed(range(groups)):
        lg = g3[i] + a3[i] * carry
        out[i] = lg
        carry = lg[0:1, :]
    return jnp.concatenate(out, axis=0)


def _rowsum(v):
    return jnp.sum(v, axis=0, keepdims=True)


def _dot(a, b):
    return jnp.dot(a, b, preferred_element_type=F32)


def _dot_nt(a, b):
    return lax.dot_general(a, b, (((1,), (1,)), ((), ())), preferred_element_type=F32)


def _dot_tn(a, b):
    return lax.dot_general(a, b, (((0,), (0,)), ((), ())), preferred_element_type=F32)


class _MixerWeights:
    def __init__(self, caw_ref, sw_ref, sb_ref, lcw_ref, lcb_ref, wa_ref, wx_ref, ba_ref, bx_ref, lam_ref):
        self.caw = [caw_ref[j:j + 1, :] for j in range(3)]
        self.lcw = [lcw_ref[j:j + 1, :] for j in range(4)]
        self.lcb = lcb_ref[...]
        row = lax.broadcasted_iota(jnp.int32, (CHUNK, CHUNK), 0)
        col = lax.broadcasted_iota(jnp.int32, (CHUNK, CHUNK), 1)
        self.tril = col <= row
        self.sw = jnp.where(self.tril, sw_ref[...], 0.0).astype(BF16)
        self.sb = sb_ref[...]
        self.wa = wa_ref[...]
        self.wx = wx_ref[...]
        self.ba = ba_ref[...]
        self.bx = bx_ref[...]
        lam = lam_ref[...]
        self.neg_c_sp = -LRU_C * _softplus_neg(lam)
        self.dsp_dlam = -_sigmoid(-lam)


def _mixer_a(ld, ldp, w, cv=None):
    t = {}
    a_x, a_c = ld(AX), ld(AC)
    t["a_x"], t["a_c"], t["a_b"], t["a_z"] = a_x, a_c, ld(AB), ld(AZ)
    t["ca"] = ca = a_c * a_x
    if cv is None:
        ca_p = ldp(AC) * ldp(AX)
        cv = w.caw[2] * ca + w.caw[1] * _shift_dn(ca, ca_p, 1) + w.caw[0] * _shift_dn(ca, ca_p, 2)
    t["cv"] = cv
    t["sa"] = _sigmoid(t["a_z"])
    t["silu_az"] = t["a_z"] * t["sa"]
    t["y_a"] = t["silu_az"] * t["a_b"] * t["cv"]
    return t


def _mixer_b(ld, w):
    t = {}
    v = ld(SV)
    vc = v - jnp.mean(v, axis=1, keepdims=True)
    t["rstd"] = lax.rsqrt(jnp.mean(vc * vc, axis=1, keepdims=True) + EPS)
    t["vn"] = vc * t["rstd"]
    t["z"] = _dot(w.sw, t["vn"].astype(BF16)) + w.sb
    t["s_u"], t["s_z"] = ld(SU), ld(SZ)
    t["ss"] = _sigmoid(t["s_z"])
    t["silu_sz"] = t["s_z"] * t["ss"]
    t["y_s"] = t["silu_sz"] * t["s_u"] * t["z"]
    return t


def _mixer_c(ld, ldp, w, backward, xc=None):
    t = {}
    t["r_x"] = r_x = ld(RX)
    if xc is None:
        r_xp = ldp(RX)
        xc = (w.lcb + w.lcw[0] * _shift_dn(r_x, r_xp, 3) + w.lcw[1] * _shift_dn(r_x, r_xp, 2)
              + w.lcw[2] * _shift_dn(r_x, r_xp, 1) + w.lcw[3] * r_x)
    t["xc"] = xc
    xcb = xc.astype(BF16)
    t["r"] = _sigmoid(_dot(xcb, w.wa) + w.ba)
    t["i"] = _sigmoid(_dot(xcb, w.wx) + w.bx)
    la = t["r"] * w.neg_c_sp
    t["a"] = jnp.exp(la)
    t["a2"] = t["a"] * t["a"]
    t["em"] = _neg_expm1(2.0 * la, t["a2"])
    if backward:
        t["inv_mult"] = lax.rsqrt(t["em"])
        t["mult"] = t["em"] * t["inv_mult"]
    else:
        t["mult"] = jnp.sqrt(t["em"])
    t["b"] = t["mult"] * (t["i"] * xc)
    t["r_z"] = ld(RZ)
    t["sr"] = _sigmoid(t["r_z"])
    t["silu_rz"] = t["r_z"] * t["sr"]
    return t


def _mixer_pre_scan(ld, ldp, w, backward, cv=None, xc=None):
    t = {**_mixer_a(ld, ldp, w, cv), **_mixer_b(ld, w), **_mixer_c(ld, ldp, w, backward, xc)}
    t["ga"], t["gs"], t["gr"] = _sigmoid(ld(GA)), _sigmoid(ld(GS)), _sigmoid(ld(GR))
    return t


def _weight_specs(n_cb_axis):
    def at(fn):
        return lambda *g: fn(g[n_cb_axis])
    return [
        pl.BlockSpec((3, LANES), at(lambda cb: (0, cb))),
        pl.BlockSpec((None, CHUNK, CHUNK), at(lambda cb: (cb, 0, 0))),
        pl.BlockSpec((None, CHUNK, LANES), at(lambda cb: (cb, 0, 0))),
        pl.BlockSpec((4, LANES), at(lambda cb: (0, cb))),
        pl.BlockSpec((1, LANES), at(lambda cb: (0, cb))),
        pl.BlockSpec((None, LANES, LANES), at(lambda cb: (cb, 0, 0))),
        pl.BlockSpec((None, LANES, LANES), at(lambda cb: (cb, 0, 0))),
        pl.BlockSpec((1, LANES), at(lambda cb: (0, cb))),
        pl.BlockSpec((1, LANES), at(lambda cb: (0, cb))),
        pl.BlockSpec((1, LANES), at(lambda cb: (0, cb))),
    ]


def _chunk_loaders(p_ref, c):
    r0 = pl.multiple_of(c * CHUNK, CHUNK)
    rp = pl.multiple_of(jnp.maximum(c * CHUNK - HALO, 0), HALO)

    def ld(j):
        return p_ref[j, pl.ds(r0, CHUNK), :].astype(F32)

    def ldp(j):
        return jnp.where(c > 0, p_ref[j, pl.ds(rp, HALO), :].astype(F32), 0.0)

    return r0, rp, ld, ldp


def _mixer_fwd(proj, mw):
    _, nb, s, _ = proj.shape
    n_chunks = s // CHUNK

    def body(p_ref, *refs):
        w = _MixerWeights(*refs[:10])
        merged_ref, hs_ref, cv_ref, xc_ref = refs[10:]

        def chunk(c, h_prev):
            r0, _, ld, ldp = _chunk_loaders(p_ref, c)
            t = _mixer_pre_scan(ld, ldp, w, False)
            h = _scan_fwd(t["a"], t["b"], h_prev)
            y_r = t["silu_rz"] * h
            merged = t["ga"] * t["y_a"] + t["gs"] * t["y_s"] + t["gr"] * y_r
            merged_ref[pl.ds(r0, CHUNK), :] = merged.astype(BF16)
            hs_ref[pl.ds(r0, CHUNK), :] = h
            cv_ref[pl.ds(r0, CHUNK), :] = t["cv"].astype(BF16)
            xc_ref[pl.ds(r0, CHUNK), :] = t["xc"].astype(BF16)
            return h[CHUNK - HALO:, :]

        def group(i, carry):
            for k in range(FWD_CHUNKS_PER_TRIP):
                carry = chunk(FWD_CHUNKS_PER_TRIP * i + k, carry)
            return carry

        assert n_chunks % FWD_CHUNKS_PER_TRIP == 0
        lax.fori_loop(0, n_chunks // FWD_CHUNKS_PER_TRIP, group, jnp.zeros((HALO, LANES), F32))

    slab = pl.BlockSpec((None, s, LANES), lambda cb, b: (b, 0, cb))
    half = jax.ShapeDtypeStruct((nb, s, D), BF16)
    return pl.pallas_call(
        body, name="mixer_fwd", grid=(D // LANES, nb),
        in_specs=[pl.BlockSpec((N_SEG, None, s, LANES), lambda cb, b: (0, b, 0, cb))] + _weight_specs(0),
        out_specs=[slab, slab, slab, slab],
        out_shape=[half, jax.ShapeDtypeStruct((nb, s, D), F32), half, half],
        compiler_params=_params(("arbitrary", "arbitrary")),
    )(proj, *mw)


def _mixer_bwd(proj, dmerged, hs, cv, xc, mw):
    _, nb, s, _ = proj.shape
    n_chunks = s // CHUNK

    def body(p_ref, dm_ref, hs_ref, cv_ref, xc_ref, *refs):
        w = _MixerWeights(*refs[:10])
        dp_ref, g_caw, g_sw, g_sb, g_lcw, g_vec, g_wa, g_wx = refs[10:]

        @pl.when(pl.program_id(1) == 0)
        def _():
            for ref in (g_caw, g_sw, g_sb, g_lcw, g_vec, g_wa, g_wx):
                ref[...] = jnp.zeros_like(ref)

        def chunk(i, carry):
            dcv_n, dxc_n, lam_n, a_n = carry
            c = n_chunks - 1 - i
            r0, rp, ld, ldp = _chunk_loaders(p_ref, c)
            t = _mixer_pre_scan(ld, ldp, w, True, cv_ref[pl.ds(r0, CHUNK), :].astype(F32),
                                xc_ref[pl.ds(r0, CHUNK), :].astype(F32))
            h = hs_ref[pl.ds(r0, CHUNK), :]
            h_p = jnp.where(c > 0, hs_ref[pl.ds(rp, HALO), :], 0.0)
            h_prev = _shift_dn(h, h_p, 1)
            dm = dm_ref[pl.ds(r0, CHUNK), :].astype(F32)
            y_r = t["silu_rz"] * h

            def out(j, val):
                dp_ref[j, pl.ds(r0, CHUNK), :] = val.astype(BF16)

            ga, gs, gr = t["ga"], t["gs"], t["gr"]
            dy_a, dy_s, dy_r = dm * ga, dm * gs, dm * gr
            out(GA, (dy_a * t["y_a"]) * (1.0 - ga))
            out(GS, (dy_s * t["y_s"]) * (1.0 - gs))
            out(GR, (dy_r * y_r) * (1.0 - gr))

            dy_ab = dy_a * t["a_b"]
            out(AZ, dy_ab * (t["cv"] * _dsilu(t["sa"], t["silu_az"])))
            out(AB, (dy_a * t["cv"]) * t["silu_az"])
            dcv = dy_ab * t["silu_az"]
            dcv1, dcv2 = _shift_up(dcv, dcv_n, 1), _shift_up(dcv, dcv_n, 2)
            dca = w.caw[2] * dcv + w.caw[1] * dcv1 + w.caw[0] * dcv2
            out(AC, dca * t["a_x"])
            out(AX, dca * t["a_c"])
            g_caw[2:3, :] += _rowsum(dcv * t["ca"])
            g_caw[1:2, :] += _rowsum(dcv1 * t["ca"])
            g_caw[0:1, :] += _rowsum(dcv2 * t["ca"])

            dy_su = dy_s * t["s_u"]
            out(SZ, dy_su * (t["z"] * _dsilu(t["ss"], t["silu_sz"])))
            out(SU, (dy_s * t["z"]) * t["silu_sz"])
            dz = dy_su * t["silu_sz"]
            dzb = dz.astype(BF16)
            g_sb[...] += jnp.broadcast_to(jnp.sum(dz, axis=1, keepdims=True), (CHUNK, LANES))
            g_sw[...] += _dot_nt(dzb, t["vn"].astype(BF16))
            dvn = _dot_tn(w.sw, dzb)
            vn = t["vn"]
            out(SV, t["rstd"] * (dvn - jnp.mean(dvn, axis=1, keepdims=True)
                                 - vn * jnp.mean(dvn * vn, axis=1, keepdims=True)))

            out(RZ, (dy_r * h) * _dsilu(t["sr"], t["silu_rz"]))
            lam = _scan_rev(_shift_up(t["a"], a_n, 1), dy_r * t["silu_rz"], lam_n)
            a, r, ig, xc = t["a"], t["r"], t["i"], t["xc"]
            lam_mult, lam_i = lam * t["mult"], lam * ig
            d_i = lam_mult * xc
            d_mult = lam_i * xc
            dxc = lam_mult * ig
            dla = lam * h_prev * a - d_mult * (t["a2"] * t["inv_mult"])
            g_vec[3:4, :] += _rowsum(dla * r) * (-LRU_C * w.dsp_dlam)
            dpr = (dla * w.neg_c_sp) * r * (1.0 - r)
            dpi = d_i * ig * (1.0 - ig)
            dprb, dpib, xcb = dpr.astype(BF16), dpi.astype(BF16), xc.astype(BF16)
            g_wa[...] += _dot_tn(xcb, dprb)
            g_wx[...] += _dot_tn(xcb, dpib)
            g_vec[1:2, :] += _rowsum(dpr)
            g_vec[2:3, :] += _rowsum(dpi)
            dxc = dxc + _dot_nt(dprb, w.wa) + _dot_nt(dpib, w.wx)
            g_vec[0:1, :] += _rowsum(dxc)
            dxcs = [_shift_up(dxc, dxc_n, 3), _shift_up(dxc, dxc_n, 2), _shift_up(dxc, dxc_n, 1), dxc]
            out(RX, w.lcw[3] * dxcs[3] + w.lcw[2] * dxcs[2] + w.lcw[1] * dxcs[1] + w.lcw[0] * dxcs[0])
            for j in range(4):
                g_lcw[j:j + 1, :] += _rowsum(dxcs[j] * t["r_x"])
            return dcv[:HALO, :], dxc[:HALO, :], lam[:HALO, :], a[:HALO, :]

        zero = jnp.zeros((HALO, LANES), F32)
        def group(i, carry):
            for k in range(BWD_CHUNKS_PER_TRIP):
                carry = chunk(BWD_CHUNKS_PER_TRIP * i + k, carry)
            return carry

        assert n_chunks % BWD_CHUNKS_PER_TRIP == 0
        lax.fori_loop(0, n_chunks // BWD_CHUNKS_PER_TRIP, group, (zero, zero, zero, zero))

        @pl.when(pl.program_id(1) == nb - 1)
        def _():
            g_sw[...] = jnp.where(w.tril, g_sw[...], 0.0)

    slab = lambda dt: pl.BlockSpec((None, s, LANES), lambda cb, b: (b, 0, cb))
    seg = pl.BlockSpec((N_SEG, None, s, LANES), lambda cb, b: (0, b, 0, cb))
    rows = lambda n: pl.BlockSpec((n, LANES), lambda cb, b: (0, cb))
    sq = pl.BlockSpec((None, LANES, LANES), lambda cb, b: (cb, 0, 0))
    n_cb = D // LANES
    return pl.pallas_call(
        body, name="mixer_bwd", grid=(n_cb, nb),
        in_specs=[seg, slab(BF16), slab(F32), slab(BF16), slab(BF16)] + _weight_specs(0),
        out_specs=[seg, rows(3), sq, sq, rows(4), rows(8), sq, sq],
        out_shape=[
            jax.ShapeDtypeStruct(proj.shape, BF16),
            jax.ShapeDtypeStruct((3, D), F32),
            jax.ShapeDtypeStruct((n_cb, CHUNK, CHUNK), F32),
            jax.ShapeDtypeStruct((n_cb, CHUNK, LANES), F32),
            jax.ShapeDtypeStruct((4, D), F32),
            jax.ShapeDtypeStruct((8, D), F32),
            jax.ShapeDtypeStruct((n_cb, LANES, LANES), F32),
            jax.ShapeDtypeStruct((n_cb, LANES, LANES), F32),
        ],
        compiler_params=_params(("arbitrary", "arbitrary")),
    )(proj, dmerged, hs, cv, xc, *mw)


def _row_tile(s, want):
    return want if s % want == 0 else s


def _norm_mod(x, gain, shift, scale):
    nb, s, _ = x.shape
    tm = _row_tile(s, 512)

    def body(x_ref, g_ref, sh_ref, sc_ref, h_ref, ht_ref):
        xv = x_ref[...]
        r = lax.rsqrt(jnp.mean(xv * xv, axis=1, keepdims=True) + EPS)
        h = ((xv * r) * g_ref[...] * (1.0 + sc_ref[...]) + sh_ref[...]).astype(BF16)
        h_ref[...] = h
        ht_ref[...] = h.T

    tile = pl.BlockSpec((None, tm, D), lambda b, m: (b, m, 0))
    vec = pl.BlockSpec((None, 1, D), lambda b, m: (b, 0, 0))
    return pl.pallas_call(
        body, name="norm_mod", grid=(nb, s // tm),
        in_specs=[tile, pl.BlockSpec((1, D), lambda b, m: (0, 0)), vec, vec],
        out_specs=[tile, pl.BlockSpec((None, D, tm), lambda b, m: (b, 0, m))],
        out_shape=[jax.ShapeDtypeStruct(x.shape, BF16), jax.ShapeDtypeStruct((nb, D, s), BF16)],
        compiler_params=_params(("arbitrary", "arbitrary")),
    )(x, gain, shift, scale)


def _in_proj(h, wg, dep):
    nb, s, _ = h.shape

    def body(h_ref, w0_ref, w1_ref, dep_ref, o_ref):
        hv = h_ref[...]
        o_ref[:, :UNIT] = _dot(hv, w0_ref[...]).astype(BF16)
        o_ref[:, UNIT:] = _dot(hv, w1_ref[...]).astype(BF16)

    def unit(k):
        return pl.BlockSpec((None, D, UNIT),
                            lambda b, j: ((2 * j + k) // UNITS_PER_DEV, 0, (2 * j + k) % UNITS_PER_DEV))

    return pl.pallas_call(
        body, name="in_proj", grid=(nb, N_SEG),
        in_specs=[pl.BlockSpec((None, s, D), lambda b, j: (b, 0, 0)), unit(0), unit(1),
                  pl.BlockSpec((8, LANES), lambda b, j: (0, 0))],
        out_specs=pl.BlockSpec((None, None, s, D), lambda b, j: (j, b, 0, 0)),
        out_shape=jax.ShapeDtypeStruct((N_SEG, nb, s, D), BF16),
        compiler_params=_params(("arbitrary", "arbitrary")),
    )(h, wg, wg, dep)


def _out_proj(x, merged, wout, gate):
    nb, s, _ = x.shape
    tm = _row_tile(s, 512)

    def body(x_ref, m_ref, w_ref, g_ref, o_ref):
        o_ref[...] = x_ref[...] + g_ref[...] * _dot(m_ref[...], w_ref[...])

    tile = pl.BlockSpec((None, tm, D), lambda b, m: (b, m, 0))
    return pl.pallas_call(
        body, name="out_proj", grid=(nb, s // tm),
        in_specs=[tile, tile, pl.BlockSpec((D, D), lambda b, m: (0, 0)),
                  pl.BlockSpec((None, 1, D), lambda b, m: (b, 0, 0))],
        out_specs=tile, out_shape=jax.ShapeDtypeStruct(x.shape, F32),
        compiler_params=_params(("arbitrary", "arbitrary")),
    )(x, merged, wout, gate)


def _out_proj_norm(x, merged, wout, gate, gain, shift, scale):
    nb, s, _ = x.shape
    tm = _row_tile(s, 512)

    def body(x_ref, m_ref, w_ref, g_ref, gn_ref, sh_ref, sc_ref, o_ref, h_ref, ht_ref):
        xv = x_ref[...] + g_ref[...] * _dot(m_ref[...], w_ref[...])
        o_ref[...] = xv
        r = lax.rsqrt(jnp.mean(xv * xv, axis=1, keepdims=True) + EPS)
        h = ((xv * r) * gn_ref[...] * (1.0 + sc_ref[...]) + sh_ref[...]).astype(BF16)
        h_ref[...] = h
        ht_ref[...] = h.T

    tile = pl.BlockSpec((None, tm, D), lambda b, m: (b, m, 0))
    vec = pl.BlockSpec((None, 1, D), lambda b, m: (b, 0, 0))
    return pl.pallas_call(
        body, name="out_proj_norm", grid=(nb, s // tm),
        in_specs=[tile, tile, pl.BlockSpec((D, D), lambda b, m: (0, 0)), vec,
                  pl.BlockSpec((1, D), lambda b, m: (0, 0)), vec, vec],
        out_specs=[tile, tile, pl.BlockSpec((None, D, tm), lambda b, m: (b, 0, m))],
        out_shape=[jax.ShapeDtypeStruct(x.shape, F32), jax.ShapeDtypeStruct(x.shape, BF16),
                   jax.ShapeDtypeStruct((nb, D, s), BF16)],
        compiler_params=_params(("arbitrary", "arbitrary")),
    )(x, merged, wout, gate, gain, shift, scale)


def _loss_head(x, gain, target):
    nb, s, _ = x.shape
    tm = _row_tile(s, 512)

    def body(x_ref, g_ref, t_ref, loss_ref, dx_ref, dg_ref):
        first = (pl.program_id(0) == 0) & (pl.program_id(1) == 0)
        last = (pl.program_id(0) == nb - 1) & (pl.program_id(1) == s // tm - 1)

        @pl.when(first)
        def _():
            loss_ref[...] = jnp.zeros_like(loss_ref)
            dg_ref[...] = jnp.zeros_like(dg_ref)

        xv = x_ref[...]
        r = lax.rsqrt(jnp.mean(xv * xv, axis=1, keepdims=True) + EPS)
        xn = xv * r
        g = g_ref[...]
        e = xn * g - t_ref[...]
        loss_ref[...] += _rowsum(e * e) * (0.5 / D)
        dy = e * (1.0 / D)
        dg_ref[...] += _rowsum(dy * xn)
        dxn = dy * g
        dx_ref[...] = r * (dxn - xn * jnp.mean(dxn * xn, axis=1, keepdims=True))

        @pl.when(last)
        def _():
            loss_ref[...] = jnp.broadcast_to(jnp.sum(loss_ref[...], axis=1, keepdims=True), (1, D))

    tile = pl.BlockSpec((None, tm, D), lambda b, m: (b, m, 0))
    vec = pl.BlockSpec((1, D), lambda b, m: (0, 0))
    return pl.pallas_call(
        body, name="loss_head", grid=(nb, s // tm),
        in_specs=[tile, vec, tile], out_specs=[vec, tile, vec],
        out_shape=[jax.ShapeDtypeStruct((1, D), F32), jax.ShapeDtypeStruct(x.shape, F32),
                   jax.ShapeDtypeStruct((1, D), F32)],
        compiler_params=_params(("arbitrary", "arbitrary")),
    )(x, gain, target)


def _out_proj_bwd(dxo, merged, wout, gate):
    nb, s, _ = dxo.shape
    tm = _row_tile(s, 512)

    def body(d_ref, m_ref, w_ref, g_ref, dm_ref, gw_ref, dg_ref):
        @pl.when((pl.program_id(0) == 0) & (pl.program_id(1) == 0))
        def _():
            gw_ref[...] = jnp.zeros_like(gw_ref)

        @pl.when(pl.program_id(1) == 0)
        def _():
            dg_ref[...] = jnp.zeros_like(dg_ref)

        d = d_ref[...]
        m = m_ref[...]
        wv = w_ref[...]
        dg_ref[...] += _rowsum(d * _dot(m, wv))
        dout = (d * g_ref[...]).astype(BF16)
        dm_ref[...] = _dot_nt(dout, wv).astype(BF16)
        gw_ref[...] += _dot_tn(m, dout)

    tile = pl.BlockSpec((None, tm, D), lambda b, m: (b, m, 0))
    vec = pl.BlockSpec((None, 1, D), lambda b, m: (b, 0, 0))
    full = pl.BlockSpec((D, D), lambda b, m: (0, 0))
    return pl.pallas_call(
        body, name="out_proj_bwd", grid=(nb, s // tm),
        in_specs=[tile, tile, full, vec], out_specs=[tile, full, vec],
        out_shape=[jax.ShapeDtypeStruct(dxo.shape, BF16), jax.ShapeDtypeStruct((D, D), F32),
                   jax.ShapeDtypeStruct((nb, 1, D), F32)],
        compiler_params=_params(("arbitrary", "arbitrary")),
    )(dxo, merged, wout, gate)


def _in_proj_bwd_h(dproj, wg, dep):
    _, nb, s, _ = dproj.shape
    tm = _row_tile(s, 1024)

    def body(dp0_ref, dp1_ref, w0_ref, w1_ref, w2_ref, w3_ref, dep_ref, dh_ref):
        j = pl.program_id(2)
        part = (_dot_nt(dp0_ref[...], jnp.concatenate([w0_ref[...], w1_ref[...]], axis=1))
                + _dot_nt(dp1_ref[...], jnp.concatenate([w2_ref[...], w3_ref[...]], axis=1)))

        @pl.when(j == 0)
        def _():
            dh_ref[...] = part

        @pl.when(j > 0)
        def _():
            dh_ref[...] += part

    def seg(k):
        return pl.BlockSpec((None, None, tm, D), lambda b, m, j: (2 * j + k, b, m, 0))

    def unit(k):
        return pl.BlockSpec((None, D, UNIT),
                            lambda b, m, j: ((4 * j + k) // UNITS_PER_DEV, 0, (4 * j + k) % UNITS_PER_DEV))

    return pl.pallas_call(
        body, name="in_proj_bwd_h", grid=(nb, s // tm, N_SEG // 2),
        in_specs=[seg(0), seg(1), unit(0), unit(1), unit(2), unit(3),
                  pl.BlockSpec((8, LANES), lambda b, m, j: (0, 0))],
        out_specs=pl.BlockSpec((None, tm, D), lambda b, m, j: (b, m, 0)),
        out_shape=jax.ShapeDtypeStruct((nb, s, D), F32),
        compiler_params=_params(("arbitrary", "arbitrary", "arbitrary")),
    )(dproj, dproj, wg, wg, wg, wg, dep)


def _norm_mod_bwd(dh, x, dxo, gain, scale):
    nb, s, _ = x.shape
    tm = _row_tile(s, 512)

    def body(dh_ref, x_ref, dxo_ref, g_ref, sc_ref, dx_ref, dsh_ref, dsc_ref, dg_ref):
        b, m = pl.program_id(0), pl.program_id(1)

        @pl.when((b == 0) & (m == 0))
        def _():
            dg_ref[...] = jnp.zeros_like(dg_ref)

        @pl.when(m == 0)
        def _():
            dsh_ref[...] = jnp.zeros_like(dsh_ref)
            dsc_ref[...] = jnp.zeros_like(dsc_ref)

        dh = dh_ref[...]
        xv = x_ref[...]
        r = lax.rsqrt(jnp.mean(xv * xv, axis=1, keepdims=True) + EPS)
        xn = xv * r
        g = g_ref[...]
        one_sc = 1.0 + sc_ref[...]
        dsh_ref[...] += _rowsum(dh)
        dsc_ref[...] += _rowsum(dh * (xn * g))
        dg_ref[...] += _rowsum(dh * one_sc * xn)
        dxn = dh * (g * one_sc)
        dx_ref[...] = dxo_ref[...] + r * (dxn - xn * jnp.mean(dxn * xn, axis=1, keepdims=True))

    tile = pl.BlockSpec((None, tm, D), lambda b, m: (b, m, 0))
    vec = pl.BlockSpec((None, 1, D), lambda b, m: (b, 0, 0))
    one = pl.BlockSpec((1, D), lambda b, m: (0, 0))
    return pl.pallas_call(
        body, name="norm_mod_bwd", grid=(nb, s // tm),
        in_specs=[tile, tile, tile, one, vec],
        out_specs=[tile, vec, vec, one],
        out_shape=[jax.ShapeDtypeStruct(x.shape, F32), jax.ShapeDtypeStruct((nb, 1, D), F32),
                   jax.ShapeDtypeStruct((nb, 1, D), F32), jax.ShapeDtypeStruct((1, D), F32)],
        compiler_params=_params(("arbitrary", "arbitrary")),
    )(dh, x, dxo, gain, scale)


def _in_proj_bwd_w(ht, dproj, dep):
    nb, _, s = ht.shape
    tm = _row_tile(s, 2048)
    n_m = s // tm

    def body(ht_ref, dp_ref, dep_ref, o_ref, acc_ref):
        b, m = pl.program_id(1), pl.program_id(2)

        @pl.when((b == 0) & (m == 0))
        def _():
            acc_ref[...] = jnp.zeros_like(acc_ref)

        acc_ref[...] += _dot(ht_ref[...], dp_ref[...])

        @pl.when((b == nb - 1) & (m == n_m - 1))
        def _():
            o_ref[0] = acc_ref[:, :UNIT].astype(BF16)
            o_ref[1] = acc_ref[:, UNIT:].astype(BF16)

    return pl.pallas_call(
        body, name="in_proj_bwd_w", grid=(N_SEG, nb, n_m),
        in_specs=[pl.BlockSpec((None, D, tm), lambda j, b, m: (b, 0, m)),
                  pl.BlockSpec((None, None, tm, D), lambda j, b, m: (j, b, m, 0)),
                  pl.BlockSpec((8, LANES), lambda j, b, m: (0, 0))],
        out_specs=pl.BlockSpec((2, D, UNIT), lambda j, b, m: (j, 0, 0)),
        out_shape=jax.ShapeDtypeStruct((2 * N_SEG, D, UNIT), BF16),
        scratch_shapes=[pltpu.VMEM((D, D), F32)],
        compiler_params=_params(("arbitrary", "arbitrary", "arbitrary")),
    )(ht, dproj, dep)


def _mod_proj(c_all, w_mod, b_mod_mine):
    nl, _, ncol = w_mod.shape
    nbg = c_all.shape[0]

    def body(c_ref, w_ref, b_ref, o_ref):
        cv = c_ref[...]
        o_ref[...] = jnp.dot(cv * jax.nn.sigmoid(cv), w_ref[...], preferred_element_type=F32,
                             precision=lax.Precision.HIGHEST) + b_ref[...]

    return pl.pallas_call(
        body, name="mod_proj", grid=(nl,),
        in_specs=[pl.BlockSpec((nbg, D), lambda l: (0, 0)), pl.BlockSpec((None, D, ncol), lambda l: (l, 0, 0)),
                  pl.BlockSpec((None, 1, ncol), lambda l: (l, 0, 0))],
        out_specs=pl.BlockSpec((None, nbg, ncol), lambda l: (l, 0, 0)),
        out_shape=jax.ShapeDtypeStruct((nl, nbg, ncol), F32),
        compiler_params=_params(("arbitrary",)),
    )(c_all, w_mod, b_mod_mine)


def _mod_grad(c_all, dmod_all, dmod_mine):
    nl, nbg, ncol = dmod_mine.shape

    def body(c_ref, da_ref, dm_ref, gw_ref, gb_ref):
        cv = c_ref[...]
        gw_ref[...] = lax.dot_general(cv * jax.nn.sigmoid(cv), dm_ref[...], (((0,), (0,)), ((), ())),
                                      preferred_element_type=F32, precision=lax.Precision.HIGHEST)
        gb_ref[...] = _rowsum(da_ref[...])

    return pl.pallas_call(
        body, name="mod_grad", grid=(nl,),
        in_specs=[pl.BlockSpec((nbg, D), lambda l: (0, 0)), pl.BlockSpec((None, nbg, 3 * D), lambda l: (l, 0, 0)),
                  pl.BlockSpec((None, nbg, ncol), lambda l: (l, 0, 0))],
        out_specs=[pl.BlockSpec((None, D, ncol), lambda l: (l, 0, 0)),
                   pl.BlockSpec((None, 1, 3 * D), lambda l: (l, 0, 0))],
        out_shape=[jax.ShapeDtypeStruct((nl, D, ncol), F32), jax.ShapeDtypeStruct((nl, 1, 3 * D), F32)],
        compiler_params=_params(("arbitrary",)),
    )(c_all, dmod_all, dmod_mine)


def _adamw(parts, w, m, v, name, layer=None, prev=None):
    n_parts, n_u, n_r, cu = parts.shape
    assert w.shape[-2:] == (n_r, n_u * cu), (parts.shape, w.shape)
    tr = n_r
    for cand in (512, 256, 128):
        if n_r > cand and n_r % cand == 0:
            tr = cand
            break
    n_prev = 0 if prev is None else 4

    def body(p_ref, w_ref, m_ref, v_ref, *rest):
        g_ref, d_ref, nm_ref, nv_ref = rest[n_prev:]
        g = p_ref[0].astype(F32)
        for k in range(1, n_parts):
            g = g + p_ref[k].astype(F32)
        m2 = ADAM_B1 * m_ref[...] + (1.0 - ADAM_B1) * g
        v2 = ADAM_B2 * v_ref[...] + (1.0 - ADAM_B2) * (g * g)
        m_hat = m2 / (1.0 - ADAM_B1 ** ADAM_STEP)
        v_hat = v2 / (1.0 - ADAM_B2 ** ADAM_STEP)
        g_ref[...] = g
        d_ref[...] = -ADAM_LR * (m_hat / (jnp.sqrt(v_hat) + ADAM_EPS) + ADAM_WD * w_ref[...])
        nm_ref[...] = m2
        nv_ref[...] = v2

    if layer is None:
        tile = pl.BlockSpec((tr, cu), lambda u, i: (i, u))
    else:
        tile = pl.BlockSpec((None, tr, cu), lambda u, i: (layer, i, u))
    shp = jax.ShapeDtypeStruct(w.shape, F32)
    return pl.pallas_call(
        body, name=name, grid=(n_u, n_r // tr),
        in_specs=[pl.BlockSpec((n_parts, None, tr, cu), lambda u, i: (0, u, i, 0)), tile, tile, tile]
        + [pl.BlockSpec(memory_space=pl.ANY)] * n_prev,
        out_specs=[tile, tile, tile, tile], out_shape=[shp, shp, shp, shp],
        input_output_aliases={4 + k: k for k in range(n_prev)},
        compiler_params=_params(("arbitrary", "arbitrary")),
    )(parts, w, m, v, *(prev or ()))


def _gathered_cols(g, inner):
    k = len(inner)
    perm = tuple(range(1, k + 1)) + (0, k + 1)
    t = jnp.transpose(g, perm)
    return t.reshape(tuple(inner) + (g.shape[0] * g.shape[-1],))


def _pair_blocks(wh):
    z = jnp.zeros((8, 64, 64), wh.dtype)
    w2 = wh.reshape(8, 2, 64, 64)
    top = jnp.concatenate([w2[:, 0], z], axis=2)
    bot = jnp.concatenate([z, w2[:, 1]], axis=2)
    return jnp.concatenate([top, bot], axis=1).astype(BF16)


def _unpair_blocks(g):
    return jnp.stack([g[:, :64, :64], g[:, 64:, 64:]], axis=1).reshape(16, 64, 64)


FLAT_ROWS = 512


def _pack_rows(arrays, lead=0):
    parts = [a.reshape(a.shape[:lead] + (-1, LANES)) for a in arrays]
    rows = jnp.concatenate(parts, axis=lead)
    pad = [(0, 0)] * rows.ndim
    pad[lead] = (0, (-rows.shape[lead]) % FLAT_ROWS)
    return jnp.pad(rows, pad)


def kernel(x, c, norm_gain, w_mod, b_mod, w_in, w_out, conv_a_w, sgu_w, sgu_b, lru_conv_w, lru_conv_b, lru_wa, lru_ba, lru_wx, lru_bx, lru_lambda, final_gain, loss_target, m_norm_gain, m_w_mod, m_b_mod, m_w_in, m_w_out, m_conv_a_w, m_sgu_w, m_sgu_b, m_lru_conv_w, m_lru_conv_b, m_lru_wa, m_lru_ba, m_lru_wx, m_lru_bx, m_lru_lambda, m_final_gain, v_norm_gain, v_w_mod, v_b_mod, v_w_in, v_w_out, v_conv_a_w, v_sgu_w, v_sgu_b, v_lru_conv_w, v_lru_conv_b, v_lru_wa, v_lru_ba, v_lru_wx, v_lru_bx, v_lru_lambda, v_final_gain):
    nl = w_in.shape[0]
    nb, s, _ = x.shape
    me = _my_index()
    mod_cols = w_mod.shape[2]


    small = jnp.concatenate([c.reshape(-1, LANES), conv_a_w.reshape(-1, LANES), lru_conv_w.reshape(-1, LANES)])
    n_c, n_ca = nb * D // LANES, nl * 3
    n_small = small.shape[0]
    small = jnp.pad(small, ((0, (-n_small) % 8), (0, 0)))
    small_all, _ = _all_gather(small, "gather_small")
    c_all = small_all[:, :n_c].reshape(N_DEV * nb, D)

    w_in_b = [w_in[0].astype(BF16)] + list(w_in[1:].astype(BF16))

    def start_w_in(l, dep):
        return _split_start(w_in_b[l], _peers_same_core, False, "gather_w_in_start", dep)

    conv_a_full = _gathered_cols(small_all[:, n_c:n_c + n_ca].reshape(N_DEV, nl, 3, LANES), (nl, 3))
    lru_conv_full = _gathered_cols(small_all[:, n_c + n_ca:n_small].reshape(N_DEV, nl, 4, LANES), (nl, 4))

    def gathered_w_in(started, after):
        block, land = _split_wait(started, after, _peers_same_core, False, False, "gather_w_in_wait")
        return _gather_finish(block, land, "gather_w_in_finish")

    sgu_b_lanes = jnp.broadcast_to(sgu_b[..., None], sgu_b.shape + (LANES,))
    mws = []
    for l in range(nl):
        mws.append((conv_a_full[l], sgu_w[l], sgu_b_lanes[l], lru_conv_full[l], lru_conv_b[l][None, :],
                    _pair_blocks(lru_wa[l]), _pair_blocks(lru_wx[l]), lru_ba[l].reshape(1, D),
                    lru_bx[l].reshape(1, D), lru_lambda[l][None, :]))

    rep_names = ["sgu_w", "sgu_b", "lru_conv_b", "lru_wa", "lru_ba", "lru_wx", "lru_bx", "lru_lambda"]
    rep_w = dict(sgu_w=sgu_w, sgu_b=sgu_b, lru_conv_b=lru_conv_b, lru_wa=lru_wa, lru_ba=lru_ba,
                 lru_wx=lru_wx, lru_bx=lru_bx, lru_lambda=lru_lambda)
    rep_m = dict(sgu_w=m_sgu_w, sgu_b=m_sgu_b, lru_conv_b=m_lru_conv_b, lru_wa=m_lru_wa,
                 lru_ba=m_lru_ba, lru_wx=m_lru_wx, lru_bx=m_lru_bx, lru_lambda=m_lru_lambda)
    rep_v = dict(sgu_w=v_sgu_w, sgu_b=v_sgu_b, lru_conv_b=v_lru_conv_b, lru_wa=v_lru_wa,
                 lru_ba=v_lru_ba, lru_wx=v_lru_wx, lru_bx=v_lru_bx, lru_lambda=v_lru_lambda)

    rep_w_all, rep_m_all, rep_v_all = [_pack_rows([src[n] for n in rep_names], lead=1)
                                       for src in (rep_w, rep_m, rep_v)]
    early = [rep_w_all, rep_m_all, rep_v_all] + w_in_b[1:] + [a for mw in mws for a in mw]

    b_mod_mine = lax.dynamic_slice_in_dim(b_mod, me * mod_cols, mod_cols, axis=1)[:, None, :]
    mod_mine = _mod_proj(c_all, w_mod, b_mod_mine)
    mod_all, mod_token = _all_gather(mod_mine.reshape(nl * N_DEV * nb, mod_cols), "gather_mod", dep=w_in_b[0])
    ici = {0: start_w_in(0, mod_token)}
    mod_full = _gathered_cols(mod_all.reshape(N_DEV, nl, N_DEV * nb, mod_cols), (nl, N_DEV * nb))
    mod_loc = lax.dynamic_slice_in_dim(mod_full, me * nb, nb, axis=1)
    shift, scale, gate = [mod_loc[:, :, j * D:(j + 1) * D][:, :, None, :] for j in range(3)]

    xs, hts, projs, mergeds, states, wg = [], [], [], [], [], []
    xl = x
    d2d = {}
    wo_started = _split_start(w_out.astype(BF16).reshape(nl * (D // N_DEV), D), _peers_all, False,
                              "gather_w_out_start", ici[0][4])
    wo = None
    h, ht = _norm_mod(xl, _after(norm_gain[0][None, :], wo_started[4]), shift[0], scale[0])
    for l in range(nl):
        if l == 0:
            wg_l, token = gathered_w_in(ici[0], [h] + early)
            ici[1] = start_w_in(1, token)
            dep = ici[1][4]
        else:
            wg_l = _forward_wait(d2d[l], h, "gather_w_in_d2d_wait")
            dep = d2d[l][4]
        wg.append(wg_l)
        proj = _in_proj(h, wg_l, dep)
        merged, *st = _mixer_fwd(proj, mws[l])
        xs.append(xl), hts.append(ht), projs.append(proj), mergeds.append(merged), states.append(st)
        gate_l = gate[l]
        if l + 1 < nl:
            block, land = _split_wait(ici[l + 1], merged, _peers_same_core, False, False, "gather_w_in_wait")
            d2d[l + 1] = _forward_start(block, land, "gather_w_in_d2d_start")
            gate_l = _after(gate_l, d2d[l + 1][4])
            if l + 2 < nl:
                ici[l + 2] = start_w_in(l + 2, d2d[l + 1][4])
                gate_l = _after(gate_l, ici[l + 2][4])
        if wo is None:
            _, wo_all = _split_wait(wo_started, merged, _peers_all, False, True, "gather_w_out_wait")
            wo = jnp.transpose(wo_all.reshape(N_DEV, nl, D // N_DEV, D), (1, 0, 2, 3)).reshape(nl, D, D)
        if l + 1 < nl:
            xl, h, ht = _out_proj_norm(xl, merged, wo[l], gate_l, norm_gain[l + 1][None, :], shift[l + 1], scale[l + 1])
        else:
            xl = _out_proj(xl, merged, wo[l], gate_l)

    loss_row, dx, g_final = _loss_head(xl, final_gain[None, :], loss_target)

    res_big, g_conv = {}, [None] * nl
    dmods = [None] * nl

    def finish_exchange(pending, after):
        l, h_in, h_out, h_rep = pending
        r_in, r_out, r_rep = _multi_wait([h_in, h_out, h_rep], after, "scatter_wait")
        res_big["w_in"] = _adamw(r_in, w_in, m_w_in, v_w_in, "adamw_w_in", l, res_big.get("w_in"))
        res_big["w_out"] = _adamw(r_out, w_out, m_w_out, v_w_out, "adamw_w_out", l, res_big.get("w_out"))
        res_big["rep"] = _adamw(r_rep[:, None], rep_w_all, rep_m_all, rep_v_all, "adamw_rep", l, res_big.get("rep"))

    pending = None
    g_gains = [None] * nl
    for l in reversed(range(nl)):
        dmerged, gw_out, dgate = _out_proj_bwd(dx, mergeds[l], wo[l], gate[l])
        dproj, g_caw, g_sw, g_sb, g_lcw, g_vec, g_wa, g_wx = _mixer_bwd(projs[l], dmerged, *states[l], mws[l])
        g_conv[l] = (g_caw, g_lcw)
        rep_g = dict(
            sgu_w=g_sw, sgu_b=g_sb[:, :, 0], lru_conv_b=g_vec[0],
            lru_wa=_unpair_blocks(g_wa), lru_ba=g_vec[1].reshape(16, 64), lru_wx=_unpair_blocks(g_wx),
            lru_bx=g_vec[2].reshape(16, 64), lru_lambda=g_vec[3])
        rep_block = _pack_rows([rep_g[n] for n in rep_names])
        (h_out, h_rep), token = _multi_start([(gw_out.reshape(N_DEV, 1, D // N_DEV, D), True), (rep_block, False)],
                                             "scatter_small_start")
        gw_in = _in_proj_bwd_w(hts[l], dproj, token)
        (h_in,), token = _multi_start([(gw_in.reshape(N_DEV, UNITS_PER_DEV, D, UNIT), True)], "scatter_w_in_start")
        started = (l, h_in, h_out, h_rep)
        dh = _in_proj_bwd_h(dproj, wg[l], token)
        dx, dshift, dscale, g_gain = _norm_mod_bwd(dh, xs[l], dx, norm_gain[l][None, :], scale[l])
        g_gains[l] = g_gain
        dmods[l] = jnp.concatenate([dshift, dscale, dgate], axis=2)[:, 0, :]
        if pending is not None:
            finish_exchange(pending, dx)
        pending = started

    conv_parts = jnp.concatenate(
        [jnp.stack([g_conv[l][0] for l in range(nl)]).reshape(nl * 3, N_DEV, LANES),
         jnp.stack([g_conv[l][1] for l in range(nl)]).reshape(nl * 4, N_DEV, LANES)], axis=0)
    conv_parts = jnp.transpose(conv_parts, (1, 0, 2))[:, None]
    conv_recv = _all_to_all(conv_parts, "scatter_conv")

    dmod_loc = jnp.stack(dmods).reshape(nl * nb, 3 * D)
    gain_rows = jnp.pad(jnp.concatenate(g_gains + [g_final, loss_row], axis=0),
                        ((0, (-(nl + 2)) % 8), (0, 2 * D)))
    tail_g, _ = _all_gather(jnp.concatenate([dmod_loc, gain_rows], axis=0), "gather_dmod",
                            dep=[res_big[k][3] for k in ("w_in", "w_out", "rep")])
    loss = jnp.sum(tail_g[:, nl * nb + nl + 1, 0])
    dmod_g = tail_g[:, :nl * nb]
    gain_parts = tail_g[:, nl * nb:nl * nb + nl + 1, :D][:, None]
    gain_cat = lambda a, b: jnp.concatenate([a, b[None, :]], axis=0)
    res_gain = _adamw(gain_parts, gain_cat(norm_gain, final_gain), gain_cat(m_norm_gain, m_final_gain),
                      gain_cat(v_norm_gain, v_final_gain), "adamw_gain")
    dmod_all = jnp.transpose(dmod_g.reshape(N_DEV, nl, nb, 3 * D), (1, 0, 2, 3)).reshape(nl, N_DEV * nb, 3 * D)
    dmod_mine = lax.dynamic_slice_in_dim(dmod_all, me * mod_cols, mod_cols, axis=2)
    gw_mod, gb_mod = _mod_grad(c_all, dmod_all, dmod_mine)
    res_w_mod = _adamw(gw_mod.reshape(1, 1, nl * D, mod_cols), w_mod.reshape(nl * D, mod_cols),
                       m_w_mod.reshape(nl * D, mod_cols), v_w_mod.reshape(nl * D, mod_cols), "adamw_w_mod")
    res_w_mod = [a.reshape(nl, D, mod_cols) for a in res_w_mod]
    res_b_mod = _adamw(gb_mod.reshape(1, 1, nl, 3 * D), b_mod, m_b_mod, v_b_mod, "adamw_b_mod")
    finish_exchange(pending, res_b_mod[1])

    cat = lambda a, b: jnp.concatenate([a.reshape(nl * 3, LANES), b.reshape(nl * 4, LANES)], axis=0)
    res_conv = _adamw(conv_recv, cat(conv_a_w, lru_conv_w), cat(m_conv_a_w, m_lru_conv_w),
                      cat(v_conv_a_w, v_lru_conv_w), "adamw_conv")
    res_conv_a = [a[:nl * 3].reshape(nl, 3, LANES) for a in res_conv]
    res_lru_conv = [a[nl * 3:].reshape(nl, 4, LANES) for a in res_conv]

    res_rep = []
    for k in range(4):
        off, d = 0, {}
        for n in rep_names:
            n_rows = rep_w[n][0].size // LANES
            d[n] = res_big["rep"][k][:, off:off + n_rows].reshape(rep_w[n].shape)
            off += n_rows
        res_rep.append(d)

    def leaf(k, name):
        if name == "norm_gain":
            return res_gain[k][:nl]
        if name == "final_gain":
            return res_gain[k][nl]
        if name == "w_mod":
            return res_w_mod[k]
        if name == "b_mod":
            return res_b_mod[k]
        if name in ("w_in", "w_out"):
            return res_big[name][k]
        if name == "conv_a_w":
            return res_conv_a[k]
        if name == "lru_conv_w":
            return res_lru_conv[k]
        return res_rep[k][name]

    order = ["norm_gain", "w_mod", "b_mod", "w_in", "w_out", "conv_a_w", "sgu_w", "sgu_b", "lru_conv_w",
             "lru_conv_b", "lru_wa", "lru_ba", "lru_wx", "lru_bx", "lru_lambda", "final_gain"]
    outs = [loss, dx]
    for k in range(4):
        outs += [leaf(k, n) for n in order]
    return tuple(outs)
```

```python
import functools

import jax
import jax.numpy as jnp
from jax import lax
from jax.experimental import pallas as pl
from jax.experimental.pallas import tpu as pltpu

F32 = jnp.float32
BF16 = jnp.bfloat16

D = 1024
N_DEV = 8
N_SEG = 12
LANES = 128
SUBLANES = 8
CHUNK = 128
HALO = 16
FWD_CHUNKS_PER_TRIP = 4
BWD_CHUNKS_PER_TRIP = 8
UNIT = 512
UNITS_PER_DEV = 3
EPS = 1e-6
LRU_C = 8.0
ADAM_LR, ADAM_B1, ADAM_B2, ADAM_EPS, ADAM_WD, ADAM_STEP = 0.001, 0.9, 0.999, 1e-08, 0.01, 10
VMEM_LIMIT = 56 * 1024 * 1024

AX, AB, AC, AZ, SU, SV, SZ, RX, RZ, GA, GS, GR = range(N_SEG)
MESH = pl.DeviceIdType.MESH


def _params(sem=None):
    return pltpu.CompilerParams(dimension_semantics=sem, vmem_limit_bytes=VMEM_LIMIT)


def _my_index():
    return 4 * lax.axis_index("x") + 2 * lax.axis_index("y") + lax.axis_index("c")


def _all_gather(block, name, dep=None):
    deps = [] if dep is None else list(dep) if isinstance(dep, (list, tuple)) else [dep]

    def body(x_ref, *refs):
        out_ref, token, send_sems, recv_sems, local_sem = refs[-5:]
        x, y, c = lax.axis_index("x"), lax.axis_index("y"), lax.axis_index("c")
        me, sibling = (x, y, c), (x, y, 1 - c)
        chips = [(1 - x, y), (x, 1 - y), (1 - x, 1 - y)]
        token[...] = jnp.zeros_like(token)

        def rows(px, py, pc):
            return out_ref.at[4 * px + 2 * py + pc]

        def copy(k, blk, to, src=None):
            return pltpu.make_async_remote_copy(
                src_ref=rows(*blk) if src is None else src, dst_ref=rows(*blk),
                send_sem=send_sems.at[k], recv_sem=recv_sems.at[k], device_id=to, device_id_type=MESH)

        mine = pltpu.make_async_copy(x_ref, rows(*me), local_sem)
        mine.start()
        first = [copy(0, me, sibling, src=x_ref)]
        first += [copy(1 + j, me, (*chip, c), src=x_ref) for j, chip in enumerate(chips)]
        for cp in first:
            cp.start()
        passed = [copy(4 + j, (*chip, c), sibling) for j, chip in enumerate(chips)]
        for j, chip in enumerate(chips):
            copy(1 + j, (*chip, c), me).wait_recv()
            passed[j].start()
        copy(0, sibling, me).wait_recv()
        for j, chip in enumerate(chips):
            copy(4 + j, (*chip, 1 - c), me).wait_recv()
        for cp in first + passed:
            cp.wait_send()
        mine.wait()

    return pl.pallas_call(
        body, name=name,
        out_shape=[jax.ShapeDtypeStruct((N_DEV,) + block.shape, block.dtype), jax.ShapeDtypeStruct((8, LANES), F32)],
        in_specs=[pl.BlockSpec(memory_space=pltpu.VMEM)]
        + [pl.BlockSpec(memory_space=pl.ANY)] * len(deps),
        out_specs=[pl.BlockSpec(memory_space=pl.ANY), pl.BlockSpec(memory_space=pltpu.VMEM)],
        scratch_shapes=[pltpu.SemaphoreType.DMA((7,)), pltpu.SemaphoreType.DMA((7,)), pltpu.SemaphoreType.DMA],
    )(block, *deps)


def _all_to_all(blocks, name):
    def body(x_ref, out_ref, send_sems, recv_sems, local_sem):
        x, y, c = lax.axis_index("x"), lax.axis_index("y"), lax.axis_index("c")
        my = 4 * x + 2 * y + c
        mine = pltpu.make_async_copy(x_ref.at[my], out_ref.at[my], local_sem)
        mine.start()
        peers = []
        for r in range(1, N_DEV):
            px = 1 - x if r & 4 else x
            py = 1 - y if r & 2 else y
            pc = 1 - c if r & 1 else c
            peers.append((r - 1, 4 * px + 2 * py + pc, (px, py, pc)))

        def copy(k, src_slot, dst_slot, to):
            return pltpu.make_async_remote_copy(
                src_ref=x_ref.at[src_slot], dst_ref=out_ref.at[dst_slot],
                send_sem=send_sems.at[k], recv_sem=recv_sems.at[k], device_id=to, device_id_type=MESH)

        sends = [copy(k, pid, my, to) for k, pid, to in peers]
        for cp in sends:
            cp.start()
        for k, pid, to in peers:
            copy(k, pid, pid, to).wait_recv()
        for cp in sends:
            cp.wait_send()
        mine.wait()

    return pl.pallas_call(
        body, name=name,
        out_shape=jax.ShapeDtypeStruct(blocks.shape, blocks.dtype),
        in_specs=[pl.BlockSpec(memory_space=pltpu.VMEM)],
        out_specs=pl.BlockSpec(memory_space=pl.ANY),
        scratch_shapes=[pltpu.SemaphoreType.DMA((7,)), pltpu.SemaphoreType.DMA((7,)), pltpu.SemaphoreType.DMA],
    )(blocks)


_HBM = pl.BlockSpec(memory_space=pltpu.HBM)
_SEM = pl.BlockSpec(memory_space=pltpu.SEMAPHORE)
_EFFECT = pltpu.SideEffectType.DATAFLOW_SIDE_EFFECTING


def _peers_all(x, y, c):
    out = []
    for r in range(1, N_DEV):
        px = 1 - x if r & 4 else x
        py = 1 - y if r & 2 else y
        pc = 1 - c if r & 1 else c
        out.append((r - 1, 4 * px + 2 * py + pc, (px, py, pc)))
    return out


def _peers_same_core(x, y, c):
    return [(k, 4 * px + 2 * py + c, (px, py, c))
            for k, (px, py) in enumerate([(1 - x, y), (x, 1 - y), (1 - x, 1 - y)])]


def _split_start(src, peers_fn, scatter, name, dep=None):
    blk = src.shape[1:] if scatter else src.shape
    land_shape = (N_DEV,) + tuple(blk)
    n = len(peers_fn(0, 0, 0))
    deps = [] if dep is None else [dep]

    def body(x_ref, land_ref, *rest):
        send_sems, recv_sems, x_thru, land_thru, token = rest[len(deps):]
        x, y, c = lax.axis_index("x"), lax.axis_index("y"), lax.axis_index("c")
        my = 4 * x + 2 * y + c
        for k, pid, to in peers_fn(x, y, c):
            pltpu.make_async_remote_copy(
                src_ref=x_ref.at[pid] if scatter else x_ref, dst_ref=land_ref.at[my],
                send_sem=send_sems.at[k], recv_sem=recv_sems.at[k], device_id=to, device_id_type=MESH).start()
        token[...] = jnp.zeros_like(token)

    return pl.pallas_call(
        body, name=name,
        out_shape=(pltpu.SemaphoreType.DMA((n,)), pltpu.SemaphoreType.DMA((n,)),
                   pltpu.HBM(src.shape, src.dtype), pltpu.HBM(land_shape, src.dtype),
                   jax.ShapeDtypeStruct((8, LANES), F32)),
        in_specs=(_HBM, _HBM) + (pl.BlockSpec(memory_space=pl.ANY),) * len(deps),
        out_specs=(_SEM, _SEM, _HBM, _HBM, pl.BlockSpec(memory_space=pltpu.VMEM)),
        input_output_aliases={0: 2, 1: 3},
        compiler_params=pltpu.CompilerParams(has_side_effects=_EFFECT),
    )(pltpu.with_memory_space_constraint(src, pltpu.HBM),
      pltpu.with_memory_space_constraint(lax.empty(land_shape, src.dtype), pltpu.HBM), *deps)


def _split_wait(handles, after, peers_fn, scatter, own, name):
    send_sems, recv_sems, src_thru, land_thru, _ = handles
    blk = land_thru.shape[1:]
    after = list(after) if isinstance(after, (list, tuple)) else [after]

    def body(x_ref, land_ref, send_sems, recv_sems, *rest):
        stage = rest[len(after) + 2:]
        x, y, c = lax.axis_index("x"), lax.axis_index("y"), lax.axis_index("c")
        if own:
            my = 4 * x + 2 * y + c
            mine = _staged_copy(x_ref.at[my] if scatter else x_ref, land_ref.at[my], *stage)
        for k, pid, to in peers_fn(x, y, c):
            _wait_both(x_ref.at[pid] if scatter else x_ref, land_ref.at[pid], send_sems.at[k], recv_sems.at[k])
        if own:
            mine.wait()

    return pl.pallas_call(
        body, name=name,
        out_shape=(pltpu.HBM(src_thru.shape, src_thru.dtype), pltpu.HBM(land_thru.shape, land_thru.dtype)),
        in_specs=(_HBM, _HBM, _SEM, _SEM) + (pl.BlockSpec(memory_space=pl.ANY),) * len(after),
        out_specs=(_HBM, _HBM),
        input_output_aliases={0: 0, 1: 1},
        scratch_shapes=[pltpu.VMEM(blk, land_thru.dtype), pltpu.SemaphoreType.DMA((2,))] if own else [],
        compiler_params=pltpu.CompilerParams(has_side_effects=_EFFECT, vmem_limit_bytes=VMEM_LIMIT),
    )(src_thru, land_thru, send_sems, recv_sems, *after)


def _wait_both(src_ref, dst_ref, send_sem, recv_sem):
    pltpu.make_async_copy(src_ref, dst_ref, send_sem).wait()
    pltpu.make_async_copy(src_ref, dst_ref, recv_sem).wait()


def _staged_copy(src_ref, dst_ref, buf, sems):
    leg = pltpu.make_async_copy(src_ref, buf, sems.at[0])
    leg.start()
    leg.wait()
    leg = pltpu.make_async_copy(buf, dst_ref, sems.at[1])
    leg.start()
    return leg


def _multi_start(srcs, name, dep=None):
    n_src = len(srcs)
    lands = [(N_DEV,) + tuple(a.shape[1:] if sc else a.shape) for a, sc in srcs]
    deps = [] if dep is None else [dep]

    def body(*refs):
        ins, outs = refs[:2 * n_src], refs[2 * n_src + len(deps):]
        x, y, c = lax.axis_index("x"), lax.axis_index("y"), lax.axis_index("c")
        my = 4 * x + 2 * y + c
        for i, (_, scatter) in enumerate(srcs):
            x_ref, land_ref, send_sems, recv_sems = ins[2 * i], ins[2 * i + 1], outs[4 * i], outs[4 * i + 1]
            for k, pid, to in _peers_all(x, y, c):
                pltpu.make_async_remote_copy(
                    src_ref=x_ref.at[pid] if scatter else x_ref, dst_ref=land_ref.at[my],
                    send_sem=send_sems.at[k], recv_sem=recv_sems.at[k], device_id=to, device_id_type=MESH).start()
        outs[-1][...] = jnp.zeros_like(outs[-1])

    out_shape, out_specs, operands = [], [], []
    for (a, _), land in zip(srcs, lands):
        out_shape += [pltpu.SemaphoreType.DMA((N_DEV - 1,)), pltpu.SemaphoreType.DMA((N_DEV - 1,)),
                      pltpu.HBM(a.shape, a.dtype), pltpu.HBM(land, a.dtype)]
        out_specs += [_SEM, _SEM, _HBM, _HBM]
        operands += [pltpu.with_memory_space_constraint(a, pltpu.HBM),
                     pltpu.with_memory_space_constraint(lax.empty(land, a.dtype), pltpu.HBM)]
    res = pl.pallas_call(
        body, name=name,
        out_shape=tuple(out_shape) + (jax.ShapeDtypeStruct((8, LANES), F32),),
        in_specs=(_HBM,) * (2 * n_src) + (pl.BlockSpec(memory_space=pl.ANY),) * len(deps),
        out_specs=tuple(out_specs) + (pl.BlockSpec(memory_space=pltpu.VMEM),),
        input_output_aliases={2 * i + j: 4 * i + 2 + j for i in range(n_src) for j in range(2)},
        compiler_params=pltpu.CompilerParams(has_side_effects=_EFFECT),
    )(*operands, *deps)
    return [tuple(res[4 * i:4 * i + 4]) + (srcs[i][1],) for i in range(n_src)], res[-1]


def _multi_wait(started, after, name):
    n_src = len(started)
    after = list(after) if isinstance(after, (list, tuple)) else [after]

    def body(*refs):
        ins = refs[:4 * n_src]
        stage = refs[4 * n_src + len(after) + 2 * n_src:]
        x, y, c = lax.axis_index("x"), lax.axis_index("y"), lax.axis_index("c")
        my = 4 * x + 2 * y + c
        pending = []
        for i, h in enumerate(started):
            x_ref, land_ref, send_sems, recv_sems = ins[4 * i:4 * i + 4]
            scatter = h[4]
            pending.append(_staged_copy(x_ref.at[my] if scatter else x_ref, land_ref.at[my],
                                        stage[2 * i], stage[2 * i + 1]))
            for k, pid, to in _peers_all(x, y, c):
                _wait_both(x_ref.at[pid] if scatter else x_ref, land_ref.at[pid], send_sems.at[k], recv_sems.at[k])
        for leg in pending:
            leg.wait()

    operands, out_shape, scratch = [], [], []
    for send, recv, src_thru, land_thru, _ in started:
        operands += [src_thru, land_thru, send, recv]
        out_shape += [pltpu.HBM(src_thru.shape, src_thru.dtype), pltpu.HBM(land_thru.shape, land_thru.dtype)]
        scratch += [pltpu.VMEM(land_thru.shape[1:], land_thru.dtype), pltpu.SemaphoreType.DMA((2,))]
    res = pl.pallas_call(
        body, name=name,
        out_shape=tuple(out_shape),
        in_specs=(_HBM, _HBM, _SEM, _SEM) * n_src + (pl.BlockSpec(memory_space=pl.ANY),) * len(after),
        out_specs=(_HBM,) * (2 * n_src),
        input_output_aliases={4 * i + j: 2 * i + j for i in range(n_src) for j in range(2)},
        scratch_shapes=scratch,
        compiler_params=pltpu.CompilerParams(has_side_effects=_EFFECT, vmem_limit_bytes=VMEM_LIMIT),
    )(*operands, *after)
    return [res[2 * i + 1] for i in range(n_src)]


def _gather_finish(block, land, name):
    def body(x_ref, land_ref, out_ref, token, send_sems, recv_sems, buf, local_sems):
        x, y, c = lax.axis_index("x"), lax.axis_index("y"), lax.axis_index("c")
        my, sib_id, sibling = 4 * x + 2 * y + c, 4 * x + 2 * y + 1 - c, (x, y, 1 - c)
        token[...] = jnp.zeros_like(token)

        def copy(k, slot, src=None):
            return pltpu.make_async_remote_copy(
                src_ref=land_ref.at[slot] if src is None else src, dst_ref=out_ref.at[slot],
                send_sem=send_sems.at[k], recv_sem=recv_sems.at[k], device_id=sibling, device_id_type=MESH)

        chips = _peers_same_core(x, y, c)
        sends = [copy(0, my, src=x_ref)] + [copy(1 + k, pid) for k, pid, _ in chips]
        for cp in sends:
            cp.start()
        mine = _staged_copy(x_ref, out_ref.at[my], buf, local_sems)
        copy(0, sib_id).wait_recv()
        for k, pid, _ in chips:
            copy(1 + k, pid + 1 - 2 * c).wait_recv()
        for cp in sends:
            cp.wait_send()
        mine.wait()

    return pl.pallas_call(
        body, name=name,
        out_shape=[jax.ShapeDtypeStruct(land.shape, land.dtype), jax.ShapeDtypeStruct((8, LANES), F32)],
        in_specs=[pl.BlockSpec(memory_space=pl.ANY), pl.BlockSpec(memory_space=pl.ANY)],
        out_specs=[pl.BlockSpec(memory_space=pl.ANY), pl.BlockSpec(memory_space=pltpu.VMEM)],
        input_output_aliases={1: 0},
        scratch_shapes=[pltpu.SemaphoreType.DMA((4,)), pltpu.SemaphoreType.DMA((4,)),
                        pltpu.VMEM(block.shape, block.dtype), pltpu.SemaphoreType.DMA((2,))],
        compiler_params=pltpu.CompilerParams(vmem_limit_bytes=VMEM_LIMIT),
    )(block, land)


def _forward_start(block, land, name):
    def body(x_ref, land_ref, send_sems, recv_sems, x_thru, land_thru, token):
        x, y, c = lax.axis_index("x"), lax.axis_index("y"), lax.axis_index("c")
        my, sibling = 4 * x + 2 * y + c, (x, y, 1 - c)
        slots = [(0, my, x_ref)] + [(1 + k, pid, land_ref.at[pid]) for k, pid, _ in _peers_same_core(x, y, c)]
        for k, slot, src in slots:
            pltpu.make_async_remote_copy(
                src_ref=src, dst_ref=land_ref.at[slot], send_sem=send_sems.at[k], recv_sem=recv_sems.at[k],
                device_id=sibling, device_id_type=MESH).start()
        token[...] = jnp.zeros_like(token)

    return pl.pallas_call(
        body, name=name,
        out_shape=(pltpu.SemaphoreType.DMA((4,)), pltpu.SemaphoreType.DMA((4,)),
                   pltpu.HBM(block.shape, block.dtype), pltpu.HBM(land.shape, land.dtype),
                   jax.ShapeDtypeStruct((8, LANES), F32)),
        in_specs=(_HBM, _HBM),
        out_specs=(_SEM, _SEM, _HBM, _HBM, pl.BlockSpec(memory_space=pltpu.VMEM)),
        input_output_aliases={0: 2, 1: 3},
        compiler_params=pltpu.CompilerParams(has_side_effects=_EFFECT),
    )(block, land)


def _forward_wait(handles, after, name):
    send_sems, recv_sems, block_thru, land_thru, _ = handles

    def body(x_ref, land_ref, send_sems, recv_sems, after_ref, x_dead, got_ref, buf, local_sems):
        x, y, c = lax.axis_index("x"), lax.axis_index("y"), lax.axis_index("c")
        my, sib_id, sibling = 4 * x + 2 * y + c, 4 * x + 2 * y + 1 - c, (x, y, 1 - c)
        mine = _staged_copy(x_ref, land_ref.at[my], buf, local_sems)
        slots = [(0, my, sib_id)] + [(1 + k, pid, pid + 1 - 2 * c) for k, pid, _ in _peers_same_core(x, y, c)]
        for k, sent, got in slots:
            _wait_both(land_ref.at[sent], land_ref.at[got], send_sems.at[k], recv_sems.at[k])
        mine.wait()

    return pl.pallas_call(
        body, name=name,
        out_shape=(pltpu.HBM(block_thru.shape, block_thru.dtype), pltpu.HBM(land_thru.shape, land_thru.dtype)),
        in_specs=(_HBM, _HBM, _SEM, _SEM, pl.BlockSpec(memory_space=pl.ANY)),
        out_specs=(_HBM, _HBM),
        input_output_aliases={0: 0, 1: 1},
        scratch_shapes=[pltpu.VMEM(block_thru.shape, block_thru.dtype), pltpu.SemaphoreType.DMA((2,))],
        compiler_params=pltpu.CompilerParams(has_side_effects=_EFFECT, vmem_limit_bytes=VMEM_LIMIT),
    )(block_thru, land_thru, send_sems, recv_sems, after)[1]


def _after(v, token):
    return v + token[0, 0].astype(v.dtype)


def _dsilu(s, silu):
    return s + silu * (1.0 - s)


def _log1p(x):
    u = 1.0 + x
    d = u - 1.0
    return jnp.where(d == 0.0, x, jnp.log(u) * (x / jnp.where(d == 0.0, 1.0, d)))


def _softplus_neg(lam):
    return jnp.maximum(-lam, 0.0) + _log1p(jnp.exp(-jnp.abs(lam)))


def _neg_expm1(y, exp_y):
    poly = -y * (1.0 + y * (0.5 + y * (1.0 / 6.0 + y * (1.0 / 24.0))))
    return jnp.where(y > -0.05, poly, 1.0 - exp_y)


def _sigmoid(x):
    return 0.5 * jnp.tanh(0.5 * x) + 0.5


def _shift_dn(cur, prev, k):
    ext = jnp.concatenate([prev, cur], axis=0)
    return pltpu.roll(ext, k, 0)[HALO:, :]


def _shift_up(cur, nxt, k):
    n = cur.shape[0]
    ext = jnp.concatenate([cur, nxt], axis=0)
    return pltpu.roll(ext, n + HALO - k, 0)[:n, :]


def _scan_fwd(a, b, h_prev):
    groups = a.shape[0] // SUBLANES
    a3 = a.reshape(groups, SUBLANES, LANES)
    b3 = b.reshape(groups, SUBLANES, LANES)
    row = lax.broadcasted_iota(jnp.int32, a3.shape, 1)
    k = 1
    while k < SUBLANES:
        a_sh = jnp.where(row >= k, pltpu.roll(a3, k, 1), 1.0)
        b_sh = jnp.where(row >= k, pltpu.roll(b3, k, 1), 0.0)
        b3 = a3 * b_sh + b3
        a3 = a3 * a_sh
        k *= 2
    carry = h_prev[HALO - 1:HALO, :]
    out = []
    for i in range(groups):
        hg = b3[i] + a3[i] * carry
        out.append(hg)
        carry = hg[SUBLANES - 1:SUBLANES, :]
    return jnp.concatenate(out, axis=0)


def _scan_rev(a_next, g, lam_next):
    groups = g.shape[0] // SUBLANES
    a3 = a_next.reshape(groups, SUBLANES, LANES)
    g3 = g.reshape(groups, SUBLANES, LANES)
    row = lax.broadcasted_iota(jnp.int32, a3.shape, 1)
    k = 1
    while k < SUBLANES:
        ok = row < SUBLANES - k
        a_sh = jnp.where(ok, pltpu.roll(a3, SUBLANES - k, 1), 1.0)
        g_sh = jnp.where(ok, pltpu.roll(g3, SUBLANES - k, 1), 0.0)
        g3 = g3 + a3 * g_sh
        a3 = a3 * a_sh
        k *= 2
    carry = lam_next[0:1, :]
    out = [None] * groups
    for i in reversed(range(groups)):
        lg = g3[i] + a3[i] * carry
        out[i] = lg
        carry = lg[0:1, :]
    return jnp.concatenate(out, axis=0)


def _rowsum(v):
    return jnp.sum(v, axis=0, keepdims=True)


def _dot(a, b):
    return jnp.dot(a, b, preferred_element_type=F32)


def _dot_nt(a, b):
    return lax.dot_general(a, b, (((1,), (1,)), ((), ())), preferred_element_type=F32)


def _dot_tn(a, b):
    return lax.dot_general(a, b, (((0,), (0,)), ((), ())), preferred_element_type=F32)


class _MixerWeights:
    def __init__(self, caw_ref, sw_ref, sb_ref, lcw_ref, lcb_ref, wa_ref, wx_ref, ba_ref, bx_ref, lam_ref):
        self.caw = [caw_ref[j:j + 1, :] for j in range(3)]
        self.lcw = [lcw_ref[j:j + 1, :] for j in range(4)]
        self.lcb = lcb_ref[...]
        row = lax.broadcasted_iota(jnp.int32, (CHUNK, CHUNK), 0)
        col = lax.broadcasted_iota(jnp.int32, (CHUNK, CHUNK), 1)
        self.tril = col <= row
        self.sw = jnp.where(self.tril, sw_ref[...], 0.0).astype(BF16)
        self.sb = sb_ref[...]
        self.wa = wa_ref[...]
        self.wx = wx_ref[...]
        self.ba = ba_ref[...]
        self.bx = bx_ref[...]
        lam = lam_ref[...]
        self.neg_c_sp = -LRU_C * _softplus_neg(lam)
        self.dsp_dlam = -_sigmoid(-lam)


def _mixer_a(ld, ldp, w, cv=None):
    t = {}
    a_x, a_c = ld(AX), ld(AC)
    t["a_x"], t["a_c"], t["a_b"], t["a_z"] = a_x, a_c, ld(AB), ld(AZ)
    t["ca"] = ca = a_c * a_x
    if cv is None:
        ca_p = ldp(AC) * ldp(AX)
        cv = w.caw[2] * ca + w.caw[1] * _shift_dn(ca, ca_p, 1) + w.caw[0] * _shift_dn(ca, ca_p, 2)
    t["cv"] = cv
    t["sa"] = _sigmoid(t["a_z"])
    t["silu_az"] = t["a_z"] * t["sa"]
    t["y_a"] = t["silu_az"] * t["a_b"] * t["cv"]
    return t


def _mixer_b(ld, w):
    t = {}
    v = ld(SV)
    vc = v - jnp.mean(v, axis=1, keepdims=True)
    t["rstd"] = lax.rsqrt(jnp.mean(vc * vc, axis=1, keepdims=True) + EPS)
    t["vn"] = vc * t["rstd"]
    t["z"] = _dot(w.sw, t["vn"].astype(BF16)) + w.sb
    t["s_u"], t["s_z"] = ld(SU), ld(SZ)
    t["ss"] = _sigmoid(t["s_z"])
    t["silu_sz"] = t["s_z"] * t["ss"]
    t["y_s"] = t["silu_sz"] * t["s_u"] * t["z"]
    return t


def _mixer_c(ld, ldp, w, backward, xc=None):
    t = {}
    t["r_x"] = r_x = ld(RX)
    if xc is None:
        r_xp = ldp(RX)
        xc = (w.lcb + w.lcw[0] * _shift_dn(r_x, r_xp, 3) + w.lcw[1] * _shift_dn(r_x, r_xp, 2)
              + w.lcw[2] * _shift_dn(r_x, r_xp, 1) + w.lcw[3] * r_x)
    t["xc"] = xc
    xcb = xc.astype(BF16)
    t["r"] = _sigmoid(_dot(xcb, w.wa) + w.ba)
    t["i"] = _sigmoid(_dot(xcb, w.wx) + w.bx)
    la = t["r"] * w.neg_c_sp
    t["a"] = jnp.exp(la)
    t["a2"] = t["a"] * t["a"]
    t["em"] = _neg_expm1(2.0 * la, t["a2"])
    if backward:
        t["inv_mult"] = lax.rsqrt(t["em"])
        t["mult"] = t["em"] * t["inv_mult"]
    else:
        t["mult"] = jnp.sqrt(t["em"])
    t["b"] = t["mult"] * (t["i"] * xc)
    t["r_z"] = ld(RZ)
    t["sr"] = _sigmoid(t["r_z"])
    t["silu_rz"] = t["r_z"] * t["sr"]
    return t


def _mixer_pre_scan(ld, ldp, w, backward, cv=None, xc=None):
    t = {**_mixer_a(ld, ldp, w, cv), **_mixer_b(ld, w), **_mixer_c(ld, ldp, w, backward, xc)}
    t["ga"], t["gs"], t["gr"] = _sigmoid(ld(GA)), _sigmoid(ld(GS)), _sigmoid(ld(GR))
    return t


def _weight_specs(n_cb_axis):
    def at(fn):
        return lambda *g: fn(g[n_cb_axis])
    return [
        pl.BlockSpec((3, LANES), at(lambda cb: (0, cb))),
        pl.BlockSpec((None, CHUNK, CHUNK), at(lambda cb: (cb, 0, 0))),
        pl.BlockSpec((None, CHUNK, LANES), at(lambda cb: (cb, 0, 0))),
        pl.BlockSpec((4, LANES), at(lambda cb: (0, cb))),
        pl.BlockSpec((1, LANES), at(lambda cb: (0, cb))),
        pl.BlockSpec((None, LANES, LANES), at(lambda cb: (cb, 0, 0))),
        pl.BlockSpec((None, LANES, LANES), at(lambda cb: (cb, 0, 0))),
        pl.BlockSpec((1, LANES), at(lambda cb: (0, cb))),
        pl.BlockSpec((1, LANES), at(lambda cb: (0, cb))),
        pl.BlockSpec((1, LANES), at(lambda cb: (0, cb))),
    ]


def _chunk_loaders(p_ref, c):
    r0 = pl.multiple_of(c * CHUNK, CHUNK)
    rp = pl.multiple_of(jnp.maximum(c * CHUNK - HALO, 0), HALO)

    def ld(j):
        return p_ref[j, pl.ds(r0, CHUNK), :].astype(F32)

    def ldp(j):
        return jnp.where(c > 0, p_ref[j, pl.ds(rp, HALO), :].astype(F32), 0.0)

    return r0, rp, ld, ldp


def _mixer_fwd(proj, mw):
    _, nb, s, _ = proj.shape
    n_chunks = s // CHUNK

    def body(p_ref, *refs):
        w = _MixerWeights(*refs[:10])
        merged_ref, hs_ref, cv_ref, xc_ref = refs[10:]

        def chunk(c, h_prev):
            r0, _, ld, ldp = _chunk_loaders(p_ref, c)
            t = _mixer_pre_scan(ld, ldp, w, False)
            h = _scan_fwd(t["a"], t["b"], h_prev)
            y_r = t["silu_rz"] * h
            merged = t["ga"] * t["y_a"] + t["gs"] * t["y_s"] + t["gr"] * y_r
            merged_ref[pl.ds(r0, CHUNK), :] = merged.astype(BF16)
            hs_ref[pl.ds(r0, CHUNK), :] = h
            cv_ref[pl.ds(r0, CHUNK), :] = t["cv"].astype(BF16)
            xc_ref[pl.ds(r0, CHUNK), :] = t["xc"].astype(BF16)
            return h[CHUNK - HALO:, :]

        def group(i, carry):
            for k in range(FWD_CHUNKS_PER_TRIP):
                carry = chunk(FWD_CHUNKS_PER_TRIP * i + k, carry)
            return carry

        assert n_chunks % FWD_CHUNKS_PER_TRIP == 0
        lax.fori_loop(0, n_chunks // FWD_CHUNKS_PER_TRIP, group, jnp.zeros((HALO, LANES), F32))

    slab = pl.BlockSpec((None, s, LANES), lambda cb, b: (b, 0, cb))
    half = jax.ShapeDtypeStruct((nb, s, D), BF16)
    return pl.pallas_call(
        body, name="mixer_fwd", grid=(D // LANES, nb),
        in_specs=[pl.BlockSpec((N_SEG, None, s, LANES), lambda cb, b: (0, b, 0, cb))] + _weight_specs(0),
        out_specs=[slab, slab, slab, slab],
        out_shape=[half, jax.ShapeDtypeStruct((nb, s, D), F32), half, half],
        compiler_params=_params(("arbitrary", "arbitrary")),
    )(proj, *mw)


def _mixer_bwd(proj, dmerged, hs, cv, xc, mw):
    _, nb, s, _ = proj.shape
    n_chunks = s // CHUNK

    def body(p_ref, dm_ref, hs_ref, cv_ref, xc_ref, *refs):
        w = _MixerWeights(*refs[:10])
        dp_ref, g_caw, g_sw, g_sb, g_lcw, g_vec, g_wa, g_wx = refs[10:]

        @pl.when(pl.program_id(1) == 0)
        def _():
            for ref in (g_caw, g_sw, g_sb, g_lcw, g_vec, g_wa, g_wx):
                ref[...] = jnp.zeros_like(ref)

        def chunk(i, carry):
            dcv_n, dxc_n, lam_n, a_n = carry
            c = n_chunks - 1 - i
            r0, rp, ld, ldp = _chunk_loaders(p_ref, c)
            t = _mixer_pre_scan(ld, ldp, w, True, cv_ref[pl.ds(r0, CHUNK), :].astype(F32),
                                xc_ref[pl.ds(r0, CHUNK), :].astype(F32))
            h = hs_ref[pl.ds(r0, CHUNK), :]
            h_p = jnp.where(c > 0, hs_ref[pl.ds(rp, HALO), :], 0.0)
            h_prev = _shift_dn(h, h_p, 1)
            dm = dm_ref[pl.ds(r0, CHUNK), :].astype(F32)
            y_r = t["silu_rz"] * h

            def out(j, val):
                dp_ref[j, pl.ds(r0, CHUNK), :] = val.astype(BF16)

            ga, gs, gr = t["ga"], t["gs"], t["gr"]
            dy_a, dy_s, dy_r = dm * ga, dm * gs, dm * gr
            out(GA, (dy_a * t["y_a"]) * (1.0 - ga))
            out(GS, (dy_s * t["y_s"]) * (1.0 - gs))
            out(GR, (dy_r * y_r) * (1.0 - gr))

            dy_ab = dy_a * t["a_b"]
            out(AZ, dy_ab * (t["cv"] * _dsilu(t["sa"], t["silu_az"])))
            out(AB, (dy_a * t["cv"]) * t["silu_az"])
            dcv = dy_ab * t["silu_az"]
            dcv1, dcv2 = _shift_up(dcv, dcv_n, 1), _shift_up(dcv, dcv_n, 2)
            dca = w.caw[2] * dcv + w.caw[1] * dcv1 + w.caw[0] * dcv2
            out(AC, dca * t["a_x"])
            out(AX, dca * t["a_c"])
            g_caw[2:3, :] += _rowsum(dcv * t["ca"])
            g_caw[1:2, :] += _rowsum(dcv1 * t["ca"])
            g_caw[0:1, :] += _rowsum(dcv2 * t["ca"])

            dy_su = dy_s * t["s_u"]
            out(SZ, dy_su * (t["z"] * _dsilu(t["ss"], t["silu_sz"])))
            out(SU, (dy_s * t["z"]) * t["silu_sz"])
            dz = dy_su * t["silu_sz"]
            dzb = dz.astype(BF16)
            g_sb[...] += jnp.broadcast_to(jnp.sum(dz, axis=1, keepdims=True), (CHUNK, LANES))
            g_sw[...] += _dot_nt(dzb, t["vn"].astype(BF16))
            dvn = _dot_tn(w.sw, dzb)
            vn = t["vn"]
            out(SV, t["rstd"] * (dvn - jnp.mean(dvn, axis=1, keepdims=True)
                                 - vn * jnp.mean(dvn * vn, axis=1, keepdims=True)))

            out(RZ, (dy_r * h) * _dsilu(t["sr"], t["silu_rz"]))
            lam = _scan_rev(_shift_up(t["a"], a_n, 1), dy_r * t["silu_rz"], lam_n)
            a, r, ig, xc = t["a"], t["r"], t["i"], t["xc"]
            lam_mult, lam_i = lam * t["mult"], lam * ig
            d_i = lam_mult * xc
            d_mult = lam_i * xc
            dxc = lam_mult * ig
            dla = lam * h_prev * a - d_mult * (t["a2"] * t["inv_mult"])
            g_vec[3:4, :] += _rowsum(dla * r) * (-LRU_C * w.dsp_dlam)
            dpr = (dla * w.neg_c_sp) * r * (1.0 - r)
            dpi = d_i * ig * (1.0 - ig)
            dprb, dpib, xcb = dpr.astype(BF16), dpi.astype(BF16), xc.astype(BF16)
            g_wa[...] += _dot_tn(xcb, dprb)
            g_wx[...] += _dot_tn(xcb, dpib)
            g_vec[1:2, :] += _rowsum(dpr)
            g_vec[2:3, :] += _rowsum(dpi)
            dxc = dxc + _dot_nt(dprb, w.wa) + _dot_nt(dpib, w.wx)
            g_vec[0:1, :] += _rowsum(dxc)
            dxcs = [_shift_up(dxc, dxc_n, 3), _shift_up(dxc, dxc_n, 2), _shift_up(dxc, dxc_n, 1), dxc]
            out(RX, w.lcw[3] * dxcs[3] + w.lcw[2] * dxcs[2] + w.lcw[1] * dxcs[1] + w.lcw[0] * dxcs[0])
            for j in range(4):
                g_lcw[j:j + 1, :] += _rowsum(dxcs[j] * t["r_x"])
            return dcv[:HALO, :], dxc[:HALO, :], lam[:HALO, :], a[:HALO, :]

        zero = jnp.zeros((HALO, LANES), F32)
        def group(i, carry):
            for k in range(BWD_CHUNKS_PER_TRIP):
                carry = chunk(BWD_CHUNKS_PER_TRIP * i + k, carry)
            return carry

        assert n_chunks % BWD_CHUNKS_PER_TRIP == 0
        lax.fori_loop(0, n_chunks // BWD_CHUNKS_PER_TRIP, group, (zero, zero, zero, zero))

        @pl.when(pl.program_id(1) == nb - 1)
        def _():
            g_sw[...] = jnp.where(w.tril, g_sw[...], 0.0)

    slab = lambda dt: pl.BlockSpec((None, s, LANES), lambda cb, b: (b, 0, cb))
    seg = pl.BlockSpec((N_SEG, None, s, LANES), lambda cb, b: (0, b, 0, cb))
    rows = lambda n: pl.BlockSpec((n, LANES), lambda cb, b: (0, cb))
    sq = pl.BlockSpec((None, LANES, LANES), lambda cb, b: (cb, 0, 0))
    n_cb = D // LANES
    return pl.pallas_call(
        body, name="mixer_bwd", grid=(n_cb, nb),
        in_specs=[seg, slab(BF16), slab(F32), slab(BF16), slab(BF16)] + _weight_specs(0),
        out_specs=[seg, rows(3), sq, sq, rows(4), rows(8), sq, sq],
        out_shape=[
            jax.ShapeDtypeStruct(proj.shape, BF16),
            jax.ShapeDtypeStruct((3, D), F32),
            jax.ShapeDtypeStruct((n_cb, CHUNK, CHUNK), F32),
            jax.ShapeDtypeStruct((n_cb, CHUNK, LANES), F32),
            jax.ShapeDtypeStruct((4, D), F32),
            jax.ShapeDtypeStruct((8, D), F32),
            jax.ShapeDtypeStruct((n_cb, LANES, LANES), F32),
            jax.ShapeDtypeStruct((n_cb, LANES, LANES), F32),
        ],
        compiler_params=_params(("arbitrary", "arbitrary")),
    )(proj, dmerged, hs, cv, xc, *mw)


def _row_tile(s, want):
    return want if s % want == 0 else s


def _norm_mod(x, gain, shift, scale):
    nb, s, _ = x.shape
    tm = _row_tile(s, 512)

    def body(x_ref, g_ref, sh_ref, sc_ref, h_ref, ht_ref):
        xv = x_ref[...]
        r = lax.rsqrt(jnp.mean(xv * xv, axis=1, keepdims=True) + EPS)
        h = ((xv * r) * g_ref[...] * (1.0 + sc_ref[...]) + sh_ref[...]).astype(BF16)
        h_ref[...] = h
        ht_ref[...] = h.T

    tile = pl.BlockSpec((None, tm, D), lambda b, m: (b, m, 0))
    vec = pl.BlockSpec((None, 1, D), lambda b, m: (b, 0, 0))
    return pl.pallas_call(
        body, name="norm_mod", grid=(nb, s // tm),
        in_specs=[tile, pl.BlockSpec((1, D), lambda b, m: (0, 0)), vec, vec],
        out_specs=[tile, pl.BlockSpec((None, D, tm), lambda b, m: (b, 0, m))],
        out_shape=[jax.ShapeDtypeStruct(x.shape, BF16), jax.ShapeDtypeStruct((nb, D, s), BF16)],
        compiler_params=_params(("arbitrary", "arbitrary")),
    )(x, gain, shift, scale)


def _in_proj(h, wg, dep):
    nb, s, _ = h.shape

    def body(h_ref, w0_ref, w1_ref, dep_ref, o_ref):
        hv = h_ref[...]
        o_ref[:, :UNIT] = _dot(hv, w0_ref[...]).astype(BF16)
        o_ref[:, UNIT:] = _dot(hv, w1_ref[...]).astype(BF16)

    def unit(k):
        return pl.BlockSpec((None, D, UNIT),
                            lambda b, j: ((2 * j + k) // UNITS_PER_DEV, 0, (2 * j + k) % UNITS_PER_DEV))

    return pl.pallas_call(
        body, name="in_proj", grid=(nb, N_SEG),
        in_specs=[pl.BlockSpec((None, s, D), lambda b, j: (b, 0, 0)), unit(0), unit(1),
                  pl.BlockSpec((8, LANES), lambda b, j: (0, 0))],
        out_specs=pl.BlockSpec((None, None, s, D), lambda b, j: (j, b, 0, 0)),
        out_shape=jax.ShapeDtypeStruct((N_SEG, nb, s, D), BF16),
        compiler_params=_params(("arbitrary", "arbitrary")),
    )(h, wg, wg, dep)


def _out_proj(x, merged, wout, gate):
    nb, s, _ = x.shape
    tm = _row_tile(s, 512)

    def body(x_ref, m_ref, w_ref, g_ref, o_ref):
        o_ref[...] = x_ref[...] + g_ref[...] * _dot(m_ref[...], w_ref[...])

    tile = pl.BlockSpec((None, tm, D), lambda b, m: (b, m, 0))
    return pl.pallas_call(
        body, name="out_proj", grid=(nb, s // tm),
        in_specs=[tile, tile, pl.BlockSpec((D, D), lambda b, m: (0, 0)),
                  pl.BlockSpec((None, 1, D), lambda b, m: (b, 0, 0))],
        out_specs=tile, out_shape=jax.ShapeDtypeStruct(x.shape, F32),
        compiler_params=_params(("arbitrary", "arbitrary")),
    )(x, merged, wout, gate)


def _out_proj_norm(x, merged, wout, gate, gain, shift, scale):
    nb, s, _ = x.shape
    tm = _row_tile(s, 512)

    def body(x_ref, m_ref, w_ref, g_ref, gn_ref, sh_ref, sc_ref, o_ref, h_ref, ht_ref):
        xv = x_ref[...] + g_ref[...] * _dot(m_ref[...], w_ref[...])
        o_ref[...] = xv
        r = lax.rsqrt(jnp.mean(xv * xv, axis=1, keepdims=True) + EPS)
        h = ((xv * r) * gn_ref[...] * (1.0 + sc_ref[...]) + sh_ref[...]).astype(BF16)
        h_ref[...] = h
        ht_ref[...] = h.T

    tile = pl.BlockSpec((None, tm, D), lambda b, m: (b, m, 0))
    vec = pl.BlockSpec((None, 1, D), lambda b, m: (b, 0, 0))
    return pl.pallas_call(
        body, name="out_proj_norm", grid=(nb, s // tm),
        in_specs=[tile, tile, pl.BlockSpec((D, D), lambda b, m: (0, 0)), vec,
                  pl.BlockSpec((1, D), lambda b, m: (0, 0)), vec, vec],
        out_specs=[tile, tile, pl.BlockSpec((None, D, tm), lambda b, m: (b, 0, m))],
        out_shape=[jax.ShapeDtypeStruct(x.shape, F32), jax.ShapeDtypeStruct(x.shape, BF16),
                   jax.ShapeDtypeStruct((nb, D, s), BF16)],
        compiler_params=_params(("arbitrary", "arbitrary")),
    )(x, merged, wout, gate, gain, shift, scale)


def _loss_head(x, gain, target):
    nb, s, _ = x.shape
    tm = _row_tile(s, 512)

    def body(x_ref, g_ref, t_ref, loss_ref, dx_ref, dg_ref):
        first = (pl.program_id(0) == 0) & (pl.program_id(1) == 0)
        last = (pl.program_id(0) == nb - 1) & (pl.program_id(1) == s // tm - 1)

        @pl.when(first)
        def _():
            loss_ref[...] = jnp.zeros_like(loss_ref)
            dg_ref[...] = jnp.zeros_like(dg_ref)

        xv = x_ref[...]
        r = lax.rsqrt(jnp.mean(xv * xv, axis=1, keepdims=True) + EPS)
        xn = xv * r
        g = g_ref[...]
        e = xn * g - t_ref[...]
        loss_ref[...] += _rowsum(e * e) * (0.5 / D)
        dy = e * (1.0 / D)
        dg_ref[...] += _rowsum(dy * xn)
        dxn = dy * g
        dx_ref[...] = r * (dxn - xn * jnp.mean(dxn * xn, axis=1, keepdims=True))

        @pl.when(last)
        def _():
            loss_ref[...] = jnp.broadcast_to(jnp.sum(loss_ref[...], axis=1, keepdims=True), (1, D))

    tile = pl.BlockSpec((None, tm, D), lambda b, m: (b, m, 0))
    vec = pl.BlockSpec((1, D), lambda b, m: (0, 0))
    return pl.pallas_call(
        body, name="loss_head", grid=(nb, s // tm),
        in_specs=[tile, vec, tile], out_specs=[vec, tile, vec],
        out_shape=[jax.ShapeDtypeStruct((1, D), F32), jax.ShapeDtypeStruct(x.shape, F32),
                   jax.ShapeDtypeStruct((1, D), F32)],
        compiler_params=_params(("arbitrary", "arbitrary")),
    )(x, gain, target)


def _out_proj_bwd(dxo, merged, wout, gate):
    nb, s, _ = dxo.shape
    tm = _row_tile(s, 512)

    def body(d_ref, m_ref, w_ref, g_ref, dm_ref, gw_ref, dg_ref):
        @pl.when((pl.program_id(0) == 0) & (pl.program_id(1) == 0))
        def _():
            gw_ref[...] = jnp.zeros_like(gw_ref)

        @pl.when(pl.program_id(1) == 0)
        def _():
            dg_ref[...] = jnp.zeros_like(dg_ref)

        d = d_ref[...]
        m = m_ref[...]
        wv = w_ref[...]
        dg_ref[...] += _rowsum(d * _dot(m, wv))
        dout = (d * g_ref[...]).astype(BF16)
        dm_ref[...] = _dot_nt(dout, wv).astype(BF16)
        gw_ref[...] += _dot_tn(m, dout)

    tile = pl.BlockSpec((None, tm, D), lambda b, m: (b, m, 0))
    vec = pl.BlockSpec((None, 1, D), lambda b, m: (b, 0, 0))
    full = pl.BlockSpec((D, D), lambda b, m: (0, 0))
    return pl.pallas_call(
        body, name="out_proj_bwd", grid=(nb, s // tm),
        in_specs=[tile, tile, full, vec], out_specs=[tile, full, vec],
        out_shape=[jax.ShapeDtypeStruct(dxo.shape, BF16), jax.ShapeDtypeStruct((D, D), F32),
                   jax.ShapeDtypeStruct((nb, 1, D), F32)],
        compiler_params=_params(("arbitrary", "arbitrary")),
    )(dxo, merged, wout, gate)


def _in_proj_bwd_h(dproj, wg, dep):
    _, nb, s, _ = dproj.shape
    tm = _row_tile(s, 1024)

    def body(dp0_ref, dp1_ref, w0_ref, w1_ref, w2_ref, w3_ref, dep_ref, dh_ref):
        j = pl.program_id(2)
        part = (_dot_nt(dp0_ref[...], jnp.concatenate([w0_ref[...], w1_ref[...]], axis=1))
                + _dot_nt(dp1_ref[...], jnp.concatenate([w2_ref[...], w3_ref[...]], axis=1)))

        @pl.when(j == 0)
        def _():
            dh_ref[...] = part

        @pl.when(j > 0)
        def _():
            dh_ref[...] += part

    def seg(k):
        return pl.BlockSpec((None, None, tm, D), lambda b, m, j: (2 * j + k, b, m, 0))

    def unit(k):
        return pl.BlockSpec((None, D, UNIT),
                            lambda b, m, j: ((4 * j + k) // UNITS_PER_DEV, 0, (4 * j + k) % UNITS_PER_DEV))

    return pl.pallas_call(
        body, name="in_proj_bwd_h", grid=(nb, s // tm, N_SEG // 2),
        in_specs=[seg(0), seg(1), unit(0), unit(1), unit(2), unit(3),
                  pl.BlockSpec((8, LANES), lambda b, m, j: (0, 0))],
        out_specs=pl.BlockSpec((None, tm, D), lambda b, m, j: (b, m, 0)),
        out_shape=jax.ShapeDtypeStruct((nb, s, D), F32),
        compiler_params=_params(("arbitrary", "arbitrary", "arbitrary")),
    )(dproj, dproj, wg, wg, wg, wg, dep)


def _norm_mod_bwd(dh, x, dxo, gain, scale):
    nb, s, _ = x.shape
    tm = _row_tile(s, 512)

    def body(dh_ref, x_ref, dxo_ref, g_ref, sc_ref, dx_ref, dsh_ref, dsc_ref, dg_ref):
        b, m = pl.program_id(0), pl.program_id(1)

        @pl.when((b == 0) & (m == 0))
        def _():
            dg_ref[...] = jnp.zeros_like(dg_ref)

        @pl.when(m == 0)
        def _():
            dsh_ref[...] = jnp.zeros_like(dsh_ref)
            dsc_ref[...] = jnp.zeros_like(dsc_ref)

        dh = dh_ref[...]
        xv = x_ref[...]
        r = lax.rsqrt(jnp.mean(xv * xv, axis=1, keepdims=True) + EPS)
        xn = xv * r
        g = g_ref[...]
        one_sc = 1.0 + sc_ref[...]
        dsh_ref[...] += _rowsum(dh)
        dsc_ref[...] += _rowsum(dh * (xn * g))
        dg_ref[...] += _rowsum(dh * one_sc * xn)
        dxn = dh * (g * one_sc)
        dx_ref[...] = dxo_ref[...] + r * (dxn - xn * jnp.mean(dxn * xn, axis=1, keepdims=True))

    tile = pl.BlockSpec((None, tm, D), lambda b, m: (b, m, 0))
    vec = pl.BlockSpec((None, 1, D), lambda b, m: (b, 0, 0))
    one = pl.BlockSpec((1, D), lambda b, m: (0, 0))
    return pl.pallas_call(
        body, name="norm_mod_bwd", grid=(nb, s // tm),
        in_specs=[tile, tile, tile, one, vec],
        out_specs=[tile, vec, vec, one],
        out_shape=[jax.ShapeDtypeStruct(x.shape, F32), jax.ShapeDtypeStruct((nb, 1, D), F32),
                   jax.ShapeDtypeStruct((nb, 1, D), F32), jax.ShapeDtypeStruct((1, D), F32)],
        compiler_params=_params(("arbitrary", "arbitrary")),
    )(dh, x, dxo, gain, scale)


def _in_proj_bwd_w(ht, dproj, dep):
    nb, _, s = ht.shape
    tm = _row_tile(s, 2048)
    n_m = s // tm

    def body(ht_ref, dp_ref, dep_ref, o_ref, acc_ref):
        b, m = pl.program_id(1), pl.program_id(2)

        @pl.when((b == 0) & (m == 0))
        def _():
            acc_ref[...] = jnp.zeros_like(acc_ref)

        acc_ref[...] += _dot(ht_ref[...], dp_ref[...])

        @pl.when((b == nb - 1) & (m == n_m - 1))
        def _():
            o_ref[0] = acc_ref[:, :UNIT].astype(BF16)
            o_ref[1] = acc_ref[:, UNIT:].astype(BF16)

    return pl.pallas_call(
        body, name="in_proj_bwd_w", grid=(N_SEG, nb, n_m),
        in_specs=[pl.BlockSpec((None, D, tm), lambda j, b, m: (b, 0, m)),
                  pl.BlockSpec((None, None, tm, D), lambda j, b, m: (j, b, m, 0)),
                  pl.BlockSpec((8, LANES), lambda j, b, m: (0, 0))],
        out_specs=pl.BlockSpec((2, D, UNIT), lambda j, b, m: (j, 0, 0)),
        out_shape=jax.ShapeDtypeStruct((2 * N_SEG, D, UNIT), BF16),
        scratch_shapes=[pltpu.VMEM((D, D), F32)],
        compiler_params=_params(("arbitrary", "arbitrary", "arbitrary")),
    )(ht, dproj, dep)


def _mod_proj(c_all, w_mod, b_mod_mine):
    nl, _, ncol = w_mod.shape
    nbg = c_all.shape[0]

    def body(c_ref, w_ref, b_ref, o_ref):
        cv = c_ref[...]
        o_ref[...] = jnp.dot(cv * jax.nn.sigmoid(cv), w_ref[...], preferred_element_type=F32,
                             precision=lax.Precision.HIGHEST) + b_ref[...]

    return pl.pallas_call(
        body, name="mod_proj", grid=(nl,),
        in_specs=[pl.BlockSpec((nbg, D), lambda l: (0, 0)), pl.BlockSpec((None, D, ncol), lambda l: (l, 0, 0)),
                  pl.BlockSpec((None, 1, ncol), lambda l: (l, 0, 0))],
        out_specs=pl.BlockSpec((None, nbg, ncol), lambda l: (l, 0, 0)),
        out_shape=jax.ShapeDtypeStruct((nl, nbg, ncol), F32),
        compiler_params=_params(("arbitrary",)),
    )(c_all, w_mod, b_mod_mine)


def _mod_grad(c_all, dmod_all, dmod_mine):
    nl, nbg, ncol = dmod_mine.shape

    def body(c_ref, da_ref, dm_ref, gw_ref, gb_ref):
        cv = c_ref[...]
        gw_ref[...] = lax.dot_general(cv * jax.nn.sigmoid(cv), dm_ref[...], (((0,), (0,)), ((), ())),
                                      preferred_element_type=F32, precision=lax.Precision.HIGHEST)
        gb_ref[...] = _rowsum(da_ref[...])

    return pl.pallas_call(
        body, name="mod_grad", grid=(nl,),
        in_specs=[pl.BlockSpec((nbg, D), lambda l: (0, 0)), pl.BlockSpec((None, nbg, 3 * D), lambda l: (l, 0, 0)),
                  pl.BlockSpec((None, nbg, ncol), lambda l: (l, 0, 0))],
        out_specs=[pl.BlockSpec((None, D, ncol), lambda l: (l, 0, 0)),
                   pl.BlockSpec((None, 1, 3 * D), lambda l: (l, 0, 0))],
        out_shape=[jax.ShapeDtypeStruct((nl, D, ncol), F32), jax.ShapeDtypeStruct((nl, 1, 3 * D), F32)],
        compiler_params=_params(("arbitrary",)),
    )(c_all, dmod_all, dmod_mine)


def _adamw(parts, w, m, v, name, layer=None, prev=None):
    n_parts, n_u, n_r, cu = parts.shape
    assert w.shape[-2:] == (n_r, n_u * cu), (parts.shape, w.shape)
    tr = n_r
    for cand in (512, 256, 128):
        if n_r > cand and n_r % cand == 0:
            tr = cand
            break
    n_prev = 0 if prev is None else 4

    def body(p_ref, w_ref, m_ref, v_ref, *rest):
        g_ref, d_ref, nm_ref, nv_ref = rest[n_prev:]
        g = p_ref[0].astype(F32)
        for k in range(1, n_parts):
            g = g + p_ref[k].astype(F32)
        m2 = ADAM_B1 * m_ref[...] + (1.0 - ADAM_B1) * g
        v2 = ADAM_B2 * v_ref[...] + (1.0 - ADAM_B2) * (g * g)
        m_hat = m2 / (1.0 - ADAM_B1 ** ADAM_STEP)
        v_hat = v2 / (1.0 - ADAM_B2 ** ADAM_STEP)
        g_ref[...] = g
        d_ref[...] = -ADAM_LR * (m_hat / (jnp.sqrt(v_hat) + ADAM_EPS) + ADAM_WD * w_ref[...])
        nm_ref[...] = m2
        nv_ref[...] = v2

    if layer is None:
        tile = pl.BlockSpec((tr, cu), lambda u, i: (i, u))
    else:
        tile = pl.BlockSpec((None, tr, cu), lambda u, i: (layer, i, u))
    shp = jax.ShapeDtypeStruct(w.shape, F32)
    return pl.pallas_call(
        body, name=name, grid=(n_u, n_r // tr),
        in_specs=[pl.BlockSpec((n_parts, None, tr, cu), lambda u, i: (0, u, i, 0)), tile, tile, tile]
        + [pl.BlockSpec(memory_space=pl.ANY)] * n_prev,
        out_specs=[tile, tile, tile, tile], out_shape=[shp, shp, shp, shp],
        input_output_aliases={4 + k: k for k in range(n_prev)},
        compiler_params=_params(("arbitrary", "arbitrary")),
    )(parts, w, m, v, *(prev or ()))


def _gathered_cols(g, inner):
    k = len(inner)
    perm = tuple(range(1, k + 1)) + (0, k + 1)
    t = jnp.transpose(g, perm)
    return t.reshape(tuple(inner) + (g.shape[0] * g.shape[-1],))


def _pair_blocks(wh):
    z = jnp.zeros((8, 64, 64), wh.dtype)
    w2 = wh.reshape(8, 2, 64, 64)
    top = jnp.concatenate([w2[:, 0], z], axis=2)
    bot = jnp.concatenate([z, w2[:, 1]], axis=2)
    return jnp.concatenate([top, bot], axis=1).astype(BF16)


def _unpair_blocks(g):
    return jnp.stack([g[:, :64, :64], g[:, 64:, 64:]], axis=1).reshape(16, 64, 64)


FLAT_ROWS = 512


def _pack_rows(arrays, lead=0):
    parts = [a.reshape(a.shape[:lead] + (-1, LANES)) for a in arrays]
    rows = jnp.concatenate(parts, axis=lead)
    pad = [(0, 0)] * rows.ndim
    pad[lead] = (0, (-rows.shape[lead]) % FLAT_ROWS)
    return jnp.pad(rows, pad)


def kernel(x, c, norm_gain, w_mod, b_mod, w_in, w_out, conv_a_w, sgu_w, sgu_b, lru_conv_w, lru_conv_b, lru_wa, lru_ba, lru_wx, lru_bx, lru_lambda, final_gain, loss_target, m_norm_gain, m_w_mod, m_b_mod, m_w_in, m_w_out, m_conv_a_w, m_sgu_w, m_sgu_b, m_lru_conv_w, m_lru_conv_b, m_lru_wa, m_lru_ba, m_lru_wx, m_lru_bx, m_lru_lambda, m_final_gain, v_norm_gain, v_w_mod, v_b_mod, v_w_in, v_w_out, v_conv_a_w, v_sgu_w, v_sgu_b, v_lru_conv_w, v_lru_conv_b, v_lru_wa, v_lru_ba, v_lru_wx, v_lru_bx, v_lru_lambda, v_final_gain):
    nl = w_in.shape[0]
    nb, s, _ = x.shape
    me = _my_index()
    mod_cols = w_mod.shape[2]


    small = jnp.concatenate([c.reshape(-1, LANES), conv_a_w.reshape(-1, LANES), lru_conv_w.reshape(-1, LANES)])
    n_c, n_ca = nb * D // LANES, nl * 3
    n_small = small.shape[0]
    small = jnp.pad(small, ((0, (-n_small) % 8), (0, 0)))
    small_all, _ = _all_gather(small, "gather_small")
    c_all = small_all[:, :n_c].reshape(N_DEV * nb, D)

    w_in_b = [w_in[0].astype(BF16)] + list(w_in[1:].astype(BF16))

    def start_w_in(l, dep):
        return _split_start(w_in_b[l], _peers_same_core, False, "gather_w_in_start", dep)

    conv_a_full = _gathered_cols(small_all[:, n_c:n_c + n_ca].reshape(N_DEV, nl, 3, LANES), (nl, 3))
    lru_conv_full = _gathered_cols(small_all[:, n_c + n_ca:n_small].reshape(N_DEV, nl, 4, LANES), (nl, 4))

    def gathered_w_in(started, after):
        block, land = _split_wait(started, after, _peers_same_core, False, False, "gather_w_in_wait")
        return _gather_finish(block, land, "gather_w_in_finish")

    sgu_b_lanes = jnp.broadcast_to(sgu_b[..., None], sgu_b.shape + (LANES,))
    mws = []
    for l in range(nl):
        mws.append((conv_a_full[l], sgu_w[l], sgu_b_lanes[l], lru_conv_full[l], lru_conv_b[l][None, :],
                    _pair_blocks(lru_wa[l]), _pair_blocks(lru_wx[l]), lru_ba[l].reshape(1, D),
                    lru_bx[l].reshape(1, D), lru_lambda[l][None, :]))

    rep_names = ["sgu_w", "sgu_b", "lru_conv_b", "lru_wa", "lru_ba", "lru_wx", "lru_bx", "lru_lambda"]
    rep_w = dict(sgu_w=sgu_w, sgu_b=sgu_b, lru_conv_b=lru_conv_b, lru_wa=lru_wa, lru_ba=lru_ba,
                 lru_wx=lru_wx, lru_bx=lru_bx, lru_lambda=lru_lambda)
    rep_m = dict(sgu_w=m_sgu_w, sgu_b=m_sgu_b, lru_conv_b=m_lru_conv_b, lru_wa=m_lru_wa,
                 lru_ba=m_lru_ba, lru_wx=m_lru_wx, lru_bx=m_lru_bx, lru_lambda=m_lru_lambda)
    rep_v = dict(sgu_w=v_sgu_w, sgu_b=v_sgu_b, lru_conv_b=v_lru_conv_b, lru_wa=v_lru_wa,
                 lru_ba=v_lru_ba, lru_wx=v_lru_wx, lru_bx=v_lru_bx, lru_lambda=v_lru_lambda)

    rep_w_all, rep_m_all, rep_v_all = [_pack_rows([src[n] for n in rep_names], lead=1)
                                       for src in (rep_w, rep_m, rep_v)]
    early = [rep_w_all, rep_m_all, rep_v_all] + w_in_b[1:] + [a for mw in mws for a in mw]

    b_mod_mine = lax.dynamic_slice_in_dim(b_mod, me * mod_cols, mod_cols, axis=1)[:, None, :]
    mod_mine = _mod_proj(c_all, w_mod, b_mod_mine)
    mod_all, mod_token = _all_gather(mod_mine.reshape(nl * N_DEV * nb, mod_cols), "gather_mod", dep=w_in_b[0])
    ici = {0: start_w_in(0, mod_token)}
    mod_full = _gathered_cols(mod_all.reshape(N_DEV, nl, N_DEV * nb, mod_cols), (nl, N_DEV * nb))
    mod_loc = lax.dynamic_slice_in_dim(mod_full, me * nb, nb, axis=1)
    shift, scale, gate = [mod_loc[:, :, j * D:(j + 1) * D][:, :, None, :] for j in range(3)]

    xs, hts, projs, mergeds, states, wg = [], [], [], [], [], []
    xl = x
    d2d = {}
    wo_started = _split_start(w_out.astype(BF16).reshape(nl * (D // N_DEV), D), _peers_all, False,
                              "gather_w_out_start", ici[0][4])
    wo = None
    h, ht = _norm_mod(xl, _after(norm_gain[0][None, :], wo_started[4]), shift[0], scale[0])
    for l in range(nl):
        if l == 0:
            wg_l, token = gathered_w_in(ici[0], [h] + early)
            ici[1] = start_w_in(1, token)
            dep = ici[1][4]
        else:
            wg_l = _forward_wait(d2d[l], h, "gather_w_in_d2d_wait")
            dep = d2d[l][4]
        wg.append(wg_l)
        proj = _in_proj(h, wg_l, dep)
        merged, *st = _mixer_fwd(proj, mws[l])
        xs.append(xl), hts.append(ht), projs.append(proj), mergeds.append(merged), states.append(st)
        gate_l = gate[l]
        if l + 1 < nl:
            block, land = _split_wait(ici[l + 1], merged, _peers_same_core, False, False, "gather_w_in_wait")
            d2d[l + 1] = _forward_start(block, land, "gather_w_in_d2d_start")
            gate_l = _after(gate_l, d2d[l + 1][4])
            if l + 2 < nl:
                ici[l + 2] = start_w_in(l + 2, d2d[l + 1][4])
                gate_l = _after(gate_l, ici[l + 2][4])
        if wo is None:
            _, wo_all = _split_wait(wo_started, merged, _peers_all, False, True, "gather_w_out_wait")
            wo = jnp.transpose(wo_all.reshape(N_DEV, nl, D // N_DEV, D), (1, 0, 2, 3)).reshape(nl, D, D)
        if l + 1 < nl:
            xl, h, ht = _out_proj_norm(xl, merged, wo[l], gate_l, norm_gain[l + 1][None, :], shift[l + 1], scale[l + 1])
        else:
            xl = _out_proj(xl, merged, wo[l], gate_l)

    loss_row, dx, g_final = _loss_head(xl, final_gain[None, :], loss_target)

    res_big, g_conv = {}, [None] * nl
    dmods = [None] * nl

    def finish_exchange(pending, after):
        l, h_in, h_out, h_rep = pending
        r_in, r_out, r_rep = _multi_wait([h_in, h_out, h_rep], after, "scatter_wait")
        res_big["w_in"] = _adamw(r_in, w_in, m_w_in, v_w_in, "adamw_w_in", l, res_big.get("w_in"))
        res_big["w_out"] = _adamw(r_out, w_out, m_w_out, v_w_out, "adamw_w_out", l, res_big.get("w_out"))
        res_big["rep"] = _adamw(r_rep[:, None], rep_w_all, rep_m_all, rep_v_all, "adamw_rep", l, res_big.get("rep"))

    pending = None
    g_gains = [None] * nl
    for l in reversed(range(nl)):
        dmerged, gw_out, dgate = _out_proj_bwd(dx, mergeds[l], wo[l], gate[l])
        dproj, g_caw, g_sw, g_sb, g_lcw, g_vec, g_wa, g_wx = _mixer_bwd(projs[l], dmerged, *states[l], mws[l])
        g_conv[l] = (g_caw, g_lcw)
        rep_g = dict(
            sgu_w=g_sw, sgu_b=g_sb[:, :, 0], lru_conv_b=g_vec[0],
            lru_wa=_unpair_blocks(g_wa), lru_ba=g_vec[1].reshape(16, 64), lru_wx=_unpair_blocks(g_wx),
            lru_bx=g_vec[2].reshape(16, 64), lru_lambda=g_vec[3])
        rep_block = _pack_rows([rep_g[n] for n in rep_names])
        (h_out, h_rep), token = _multi_start([(gw_out.reshape(N_DEV, 1, D // N_DEV, D), True), (rep_block, False)],
                                             "scatter_small_start")
        gw_in = _in_proj_bwd_w(hts[l], dproj, token)
        (h_in,), token = _multi_start([(gw_in.reshape(N_DEV, UNITS_PER_DEV, D, UNIT), True)], "scatter_w_in_start")
        started = (l, h_in, h_out, h_rep)
        dh = _in_proj_bwd_h(dproj, wg[l], token)
        dx, dshift, dscale, g_gain = _norm_mod_bwd(dh, xs[l], dx, norm_gain[l][None, :], scale[l])
        g_gains[l] = g_gain
        dmods[l] = jnp.concatenate([dshift, dscale, dgate], axis=2)[:, 0, :]
        if pending is not None:
            finish_exchange(pending, dx)
        pending = started

    conv_parts = jnp.concatenate(
        [jnp.stack([g_conv[l][0] for l in range(nl)]).reshape(nl * 3, N_DEV, LANES),
         jnp.stack([g_conv[l][1] for l in range(nl)]).reshape(nl * 4, N_DEV, LANES)], axis=0)
    conv_parts = jnp.transpose(conv_parts, (1, 0, 2))[:, None]
    conv_recv = _all_to_all(conv_parts, "scatter_conv")

    dmod_loc = jnp.stack(dmods).reshape(nl * nb, 3 * D)
    gain_rows = jnp.pad(jnp.concatenate(g_gains + [g_final, loss_row], axis=0),
                        ((0, (-(nl + 2)) % 8), (0, 2 * D)))
    tail_g, _ = _all_gather(jnp.concatenate([dmod_loc, gain_rows], axis=0), "gather_dmod",
                            dep=[res_big[k][3] for k in ("w_in", "w_out", "rep")])
    loss = jnp.sum(tail_g[:, nl * nb + nl + 1, 0])
    dmod_g = tail_g[:, :nl * nb]
    gain_parts = tail_g[:, nl * nb:nl * nb + nl + 1, :D][:, None]
    gain_cat = lambda a, b: jnp.concatenate([a, b[None, :]], axis=0)
    res_gain = _adamw(gain_parts, gain_cat(norm_gain, final_gain), gain_cat(m_norm_gain, m_final_gain),
                      gain_cat(v_norm_gain, v_final_gain), "adamw_gain")
    dmod_all = jnp.transpose(dmod_g.reshape(N_DEV, nl, nb, 3 * D), (1, 0, 2, 3)).reshape(nl, N_DEV * nb, 3 * D)
    dmod_mine = lax.dynamic_slice_in_dim(dmod_all, me * mod_cols, mod_cols, axis=2)
    gw_mod, gb_mod = _mod_grad(c_all, dmod_all, dmod_mine)
    res_w_mod = _adamw(gw_mod.reshape(1, 1, nl * D, mod_cols), w_mod.reshape(nl * D, mod_cols),
                       m_w_mod.reshape(nl * D, mod_cols), v_w_mod.reshape(nl * D, mod_cols), "adamw_w_mod")
    res_w_mod = [a.reshape(nl, D, mod_cols) for a in res_w_mod]
    res_b_mod = _adamw(gb_mod.reshape(1, 1, nl, 3 * D), b_mod, m_b_mod, v_b_mod, "adamw_b_mod")
    finish_exchange(pending, res_b_mod[1])

    cat = lambda a, b: jnp.concatenate([a.reshape(nl * 3, LANES), b.reshape(nl * 4, LANES)], axis=0)
    res_conv = _adamw(conv_recv, cat(conv_a_w, lru_conv_w), cat(m_conv_a_w, m_lru_conv_w),
                      cat(v_conv_a_w, v_lru_conv_w), "adamw_conv")
    res_conv_a = [a[:nl * 3].reshape(nl, 3, LANES) for a in res_conv]
    res_lru_conv = [a[nl * 3:].reshape(nl, 4, LANES) for a in res_conv]

    res_rep = []
    for k in range(4):
        off, d = 0, {}
        for n in rep_names:
            n_rows = rep_w[n][0].size // LANES
            d[n] = res_big["rep"][k][:, off:off + n_rows].reshape(rep_w[n].shape)
            off += n_rows
        res_rep.append(d)

    def leaf(k, name):
        if name == "norm_gain":
            return res_gain[k][:nl]
        if name == "final_gain":
            return res_gain[k][nl]
        if name == "w_mod":
            return res_w_mod[k]
        if name == "b_mod":
            return res_b_mod[k]
        if name in ("w_in", "w_out"):
            return res_big[name][k]
        if name == "conv_a_w":
            return res_conv_a[k]
        if name == "lru_conv_w":
            return res_lru_conv[k]
        return res_rep[k][name]

    order = ["norm_gain", "w_mod", "b_mod", "w_in", "w_out", "conv_a_w", "sgu_w", "sgu_b", "lru_conv_w",
             "lru_conv_b", "lru_wa", "lru_ba", "lru_wx", "lru_bx", "lru_lambda", "final_gain"]
    outs = [loss, dx]
    for k in range(4):
        outs += [leaf(k, n) for n in order]
    return tuple(outs)
```

```python
import functools

import jax
import jax.numpy as jnp
from jax import lax
from jax.experimental import pallas as pl
from jax.experimental.pallas import tpu as pltpu

F32 = jnp.float32
BF16 = jnp.bfloat16

D = 1024
N_DEV = 8
N_SEG = 12
LANES = 128
SUBLANES = 8
CHUNK = 128
HALO = 16
FWD_CHUNKS_PER_TRIP = 8
BWD_CHUNKS_PER_TRIP = 16
UNIT = 512
UNITS_PER_DEV = 3
EPS = 1e-6
LRU_C = 8.0
ADAM_LR, ADAM_B1, ADAM_B2, ADAM_EPS, ADAM_WD, ADAM_STEP = 0.001, 0.9, 0.999, 1e-08, 0.01, 10
VMEM_LIMIT = 56 * 1024 * 1024

AX, AB, AC, AZ, SU, SV, SZ, RX, RZ, GA, GS, GR = range(N_SEG)
MESH = pl.DeviceIdType.MESH


def _params(sem=None):
    return pltpu.CompilerParams(dimension_semantics=sem, vmem_limit_bytes=VMEM_LIMIT)


def _my_index():
    return 4 * lax.axis_index("x") + 2 * lax.axis_index("y") + lax.axis_index("c")


def _all_gather(block, name, dep=None):
    deps = [] if dep is None else list(dep) if isinstance(dep, (list, tuple)) else [dep]

    def body(x_ref, *refs):
        out_ref, token, send_sems, recv_sems, local_sem = refs[-5:]
        x, y, c = lax.axis_index("x"), lax.axis_index("y"), lax.axis_index("c")
        me, sibling = (x, y, c), (x, y, 1 - c)
        chips = [(1 - x, y), (x, 1 - y), (1 - x, 1 - y)]
        token[...] = jnp.zeros_like(token)

        def rows(px, py, pc):
            return out_ref.at[4 * px + 2 * py + pc]

        def copy(k, blk, to, src=None):
            return pltpu.make_async_remote_copy(
                src_ref=rows(*blk) if src is None else src, dst_ref=rows(*blk),
                send_sem=send_sems.at[k], recv_sem=recv_sems.at[k], device_id=to, device_id_type=MESH)

        mine = pltpu.make_async_copy(x_ref, rows(*me), local_sem)
        mine.start()
        first = [copy(0, me, sibling, src=x_ref)]
        first += [copy(1 + j, me, (*chip, c), src=x_ref) for j, chip in enumerate(chips)]
        for cp in first:
            cp.start()
        passed = [copy(4 + j, (*chip, c), sibling) for j, chip in enumerate(chips)]
        for j, chip in enumerate(chips):
            copy(1 + j, (*chip, c), me).wait_recv()
            passed[j].start()
        copy(0, sibling, me).wait_recv()
        for j, chip in enumerate(chips):
            copy(4 + j, (*chip, 1 - c), me).wait_recv()
        for cp in first + passed:
            cp.wait_send()
        mine.wait()

    return pl.pallas_call(
        body, name=name,
        out_shape=[jax.ShapeDtypeStruct((N_DEV,) + block.shape, block.dtype), jax.ShapeDtypeStruct((8, LANES), F32)],
        in_specs=[pl.BlockSpec(memory_space=pltpu.VMEM)]
        + [pl.BlockSpec(memory_space=pl.ANY)] * len(deps),
        out_specs=[pl.BlockSpec(memory_space=pl.ANY), pl.BlockSpec(memory_space=pltpu.VMEM)],
        scratch_shapes=[pltpu.SemaphoreType.DMA((7,)), pltpu.SemaphoreType.DMA((7,)), pltpu.SemaphoreType.DMA],
    )(block, *deps)


def _all_to_all(blocks, name):
    def body(x_ref, out_ref, send_sems, recv_sems, local_sem):
        x, y, c = lax.axis_index("x"), lax.axis_index("y"), lax.axis_index("c")
        my = 4 * x + 2 * y + c
        mine = pltpu.make_async_copy(x_ref.at[my], out_ref.at[my], local_sem)
        mine.start()
        peers = []
        for r in range(1, N_DEV):
            px = 1 - x if r & 4 else x
            py = 1 - y if r & 2 else y
            pc = 1 - c if r & 1 else c
            peers.append((r - 1, 4 * px + 2 * py + pc, (px, py, pc)))

        def copy(k, src_slot, dst_slot, to):
            return pltpu.make_async_remote_copy(
                src_ref=x_ref.at[src_slot], dst_ref=out_ref.at[dst_slot],
                send_sem=send_sems.at[k], recv_sem=recv_sems.at[k], device_id=to, device_id_type=MESH)

        sends = [copy(k, pid, my, to) for k, pid, to in peers]
        for cp in sends:
            cp.start()
        for k, pid, to in peers:
            copy(k, pid, pid, to).wait_recv()
        for cp in sends:
            cp.wait_send()
        mine.wait()

    return pl.pallas_call(
        body, name=name,
        out_shape=jax.ShapeDtypeStruct(blocks.shape, blocks.dtype),
        in_specs=[pl.BlockSpec(memory_space=pltpu.VMEM)],
        out_specs=pl.BlockSpec(memory_space=pl.ANY),
        scratch_shapes=[pltpu.SemaphoreType.DMA((7,)), pltpu.SemaphoreType.DMA((7,)), pltpu.SemaphoreType.DMA],
    )(blocks)


_HBM = pl.BlockSpec(memory_space=pltpu.HBM)
_SEM = pl.BlockSpec(memory_space=pltpu.SEMAPHORE)
_EFFECT = pltpu.SideEffectType.DATAFLOW_SIDE_EFFECTING


def _peers_all(x, y, c):
    out = []
    for r in range(1, N_DEV):
        px = 1 - x if r & 4 else x
        py = 1 - y if r & 2 else y
        pc = 1 - c if r & 1 else c
        out.append((r - 1, 4 * px + 2 * py + pc, (px, py, pc)))
    return out


def _peers_same_core(x, y, c):
    return [(k, 4 * px + 2 * py + c, (px, py, c))
            for k, (px, py) in enumerate([(1 - x, y), (x, 1 - y), (1 - x, 1 - y)])]


def _split_start(src, peers_fn, scatter, name, dep=None):
    blk = src.shape[1:] if scatter else src.shape
    land_shape = (N_DEV,) + tuple(blk)
    n = len(peers_fn(0, 0, 0))
    deps = [] if dep is None else [dep]

    def body(x_ref, land_ref, *rest):
        send_sems, recv_sems, x_thru, land_thru, token = rest[len(deps):]
        x, y, c = lax.axis_index("x"), lax.axis_index("y"), lax.axis_index("c")
        my = 4 * x + 2 * y + c
        for k, pid, to in peers_fn(x, y, c):
            pltpu.make_async_remote_copy(
                src_ref=x_ref.at[pid] if scatter else x_ref, dst_ref=land_ref.at[my],
                send_sem=send_sems.at[k], recv_sem=recv_sems.at[k], device_id=to, device_id_type=MESH).start()
        token[...] = jnp.zeros_like(token)

    return pl.pallas_call(
        body, name=name,
        out_shape=(pltpu.SemaphoreType.DMA((n,)), pltpu.SemaphoreType.DMA((n,)),
                   pltpu.HBM(src.shape, src.dtype), pltpu.HBM(land_shape, src.dtype),
                   jax.ShapeDtypeStruct((8, LANES), F32)),
        in_specs=(_HBM, _HBM) + (pl.BlockSpec(memory_space=pl.ANY),) * len(deps),
        out_specs=(_SEM, _SEM, _HBM, _HBM, pl.BlockSpec(memory_space=pltpu.VMEM)),
        input_output_aliases={0: 2, 1: 3},
        compiler_params=pltpu.CompilerParams(has_side_effects=_EFFECT),
    )(pltpu.with_memory_space_constraint(src, pltpu.HBM),
      pltpu.with_memory_space_constraint(lax.empty(land_shape, src.dtype), pltpu.HBM), *deps)


def _split_wait(handles, after, peers_fn, scatter, own, name):
    send_sems, recv_sems, src_thru, land_thru, _ = handles
    blk = land_thru.shape[1:]
    after = list(after) if isinstance(after, (list, tuple)) else [after]

    def body(x_ref, land_ref, send_sems, recv_sems, *rest):
        stage = rest[len(after) + 2:]
        x, y, c = lax.axis_index("x"), lax.axis_index("y"), lax.axis_index("c")
        if own:
            my = 4 * x + 2 * y + c
            mine = _staged_copy(x_ref.at[my] if scatter else x_ref, land_ref.at[my], *stage)
        for k, pid, to in peers_fn(x, y, c):
            cp = pltpu.make_async_remote_copy(
                src_ref=x_ref.at[pid] if scatter else x_ref, dst_ref=land_ref.at[pid],
                send_sem=send_sems.at[k], recv_sem=recv_sems.at[k], device_id=to, device_id_type=MESH)
            cp.wait_send()
            cp.wait_recv()
        if own:
            mine.wait()

    return pl.pallas_call(
        body, name=name,
        out_shape=(pltpu.HBM(src_thru.shape, src_thru.dtype), pltpu.HBM(land_thru.shape, land_thru.dtype)),
        in_specs=(_HBM, _HBM, _SEM, _SEM) + (pl.BlockSpec(memory_space=pl.ANY),) * len(after),
        out_specs=(_HBM, _HBM),
        input_output_aliases={0: 0, 1: 1},
        scratch_shapes=[pltpu.VMEM(blk, land_thru.dtype), pltpu.SemaphoreType.DMA((2,))] if own else [],
        compiler_params=pltpu.CompilerParams(has_side_effects=_EFFECT, vmem_limit_bytes=VMEM_LIMIT),
    )(src_thru, land_thru, send_sems, recv_sems, *after)


def _staged_copy(src_ref, dst_ref, buf, sems):
    leg = pltpu.make_async_copy(src_ref, buf, sems.at[0])
    leg.start()
    leg.wait()
    leg = pltpu.make_async_copy(buf, dst_ref, sems.at[1])
    leg.start()
    return leg


def _multi_start(srcs, name, dep=None):
    n_src = len(srcs)
    lands = [(N_DEV,) + tuple(a.shape[1:] if sc else a.shape) for a, sc in srcs]
    deps = [] if dep is None else [dep]

    def body(*refs):
        ins, outs = refs[:2 * n_src], refs[2 * n_src + len(deps):]
        x, y, c = lax.axis_index("x"), lax.axis_index("y"), lax.axis_index("c")
        my = 4 * x + 2 * y + c
        for i, (_, scatter) in enumerate(srcs):
            x_ref, land_ref, send_sems, recv_sems = ins[2 * i], ins[2 * i + 1], outs[4 * i], outs[4 * i + 1]
            for k, pid, to in _peers_all(x, y, c):
                pltpu.make_async_remote_copy(
                    src_ref=x_ref.at[pid] if scatter else x_ref, dst_ref=land_ref.at[my],
                    send_sem=send_sems.at[k], recv_sem=recv_sems.at[k], device_id=to, device_id_type=MESH).start()
        outs[-1][...] = jnp.zeros_like(outs[-1])

    out_shape, out_specs, operands = [], [], []
    for (a, _), land in zip(srcs, lands):
        out_shape += [pltpu.SemaphoreType.DMA((N_DEV - 1,)), pltpu.SemaphoreType.DMA((N_DEV - 1,)),
                      pltpu.HBM(a.shape, a.dtype), pltpu.HBM(land, a.dtype)]
        out_specs += [_SEM, _SEM, _HBM, _HBM]
        operands += [pltpu.with_memory_space_constraint(a, pltpu.HBM),
                     pltpu.with_memory_space_constraint(lax.empty(land, a.dtype), pltpu.HBM)]
    res = pl.pallas_call(
        body, name=name,
        out_shape=tuple(out_shape) + (jax.ShapeDtypeStruct((8, LANES), F32),),
        in_specs=(_HBM,) * (2 * n_src) + (pl.BlockSpec(memory_space=pl.ANY),) * len(deps),
        out_specs=tuple(out_specs) + (pl.BlockSpec(memory_space=pltpu.VMEM),),
        input_output_aliases={2 * i + j: 4 * i + 2 + j for i in range(n_src) for j in range(2)},
        compiler_params=pltpu.CompilerParams(has_side_effects=_EFFECT),
    )(*operands, *deps)
    return [tuple(res[4 * i:4 * i + 4]) + (srcs[i][1],) for i in range(n_src)], res[-1]


def _multi_wait(started, after, name):
    n_src = len(started)
    after = list(after) if isinstance(after, (list, tuple)) else [after]

    def body(*refs):
        ins = refs[:4 * n_src]
        stage = refs[4 * n_src + len(after) + 2 * n_src:]
        x, y, c = lax.axis_index("x"), lax.axis_index("y"), lax.axis_index("c")
        my = 4 * x + 2 * y + c
        pending = []
        for i, h in enumerate(started):
            x_ref, land_ref, send_sems, recv_sems = ins[4 * i:4 * i + 4]
            scatter = h[4]
            pending.append(_staged_copy(x_ref.at[my] if scatter else x_ref, land_ref.at[my],
                                        stage[2 * i], stage[2 * i + 1]))
            for k, pid, to in _peers_all(x, y, c):
                cp = pltpu.make_async_remote_copy(
                    src_ref=x_ref.at[pid] if scatter else x_ref, dst_ref=land_ref.at[pid],
                    send_sem=send_sems.at[k], recv_sem=recv_sems.at[k], device_id=to, device_id_type=MESH)
                cp.wait_send()
                cp.wait_recv()
        for leg in pending:
            leg.wait()

    operands, out_shape, scratch = [], [], []
    for send, recv, src_thru, land_thru, _ in started:
        operands += [src_thru, land_thru, send, recv]
        out_shape += [pltpu.HBM(src_thru.shape, src_thru.dtype), pltpu.HBM(land_thru.shape, land_thru.dtype)]
        scratch += [pltpu.VMEM(land_thru.shape[1:], land_thru.dtype), pltpu.SemaphoreType.DMA((2,))]
    res = pl.pallas_call(
        body, name=name,
        out_shape=tuple(out_shape),
        in_specs=(_HBM, _HBM, _SEM, _SEM) * n_src + (pl.BlockSpec(memory_space=pl.ANY),) * len(after),
        out_specs=(_HBM,) * (2 * n_src),
        input_output_aliases={4 * i + j: 2 * i + j for i in range(n_src) for j in range(2)},
        scratch_shapes=scratch,
        compiler_params=pltpu.CompilerParams(has_side_effects=_EFFECT, vmem_limit_bytes=VMEM_LIMIT),
    )(*operands, *after)
    return [res[2 * i + 1] for i in range(n_src)]


def _gather_finish(block, land, name):
    def body(x_ref, land_ref, out_ref, token, send_sems, recv_sems, buf, local_sems):
        x, y, c = lax.axis_index("x"), lax.axis_index("y"), lax.axis_index("c")
        my, sib_id, sibling = 4 * x + 2 * y + c, 4 * x + 2 * y + 1 - c, (x, y, 1 - c)
        token[...] = jnp.zeros_like(token)

        def copy(k, slot, src=None):
            return pltpu.make_async_remote_copy(
                src_ref=land_ref.at[slot] if src is None else src, dst_ref=out_ref.at[slot],
                send_sem=send_sems.at[k], recv_sem=recv_sems.at[k], device_id=sibling, device_id_type=MESH)

        chips = _peers_same_core(x, y, c)
        sends = [copy(0, my, src=x_ref)] + [copy(1 + k, pid) for k, pid, _ in chips]
        for cp in sends:
            cp.start()
        mine = _staged_copy(x_ref, out_ref.at[my], buf, local_sems)
        copy(0, sib_id).wait_recv()
        for k, pid, _ in chips:
            copy(1 + k, pid + 1 - 2 * c).wait_recv()
        for cp in sends:
            cp.wait_send()
        mine.wait()

    return pl.pallas_call(
        body, name=name,
        out_shape=[jax.ShapeDtypeStruct(land.shape, land.dtype), jax.ShapeDtypeStruct((8, LANES), F32)],
        in_specs=[pl.BlockSpec(memory_space=pl.ANY), pl.BlockSpec(memory_space=pl.ANY)],
        out_specs=[pl.BlockSpec(memory_space=pl.ANY), pl.BlockSpec(memory_space=pltpu.VMEM)],
        input_output_aliases={1: 0},
        scratch_shapes=[pltpu.SemaphoreType.DMA((4,)), pltpu.SemaphoreType.DMA((4,)),
                        pltpu.VMEM(block.shape, block.dtype), pltpu.SemaphoreType.DMA((2,))],
        compiler_params=pltpu.CompilerParams(vmem_limit_bytes=VMEM_LIMIT),
    )(block, land)


def _forward_start(block, land, name):
    def body(x_ref, land_ref, send_sems, recv_sems, x_thru, land_thru, token):
        x, y, c = lax.axis_index("x"), lax.axis_index("y"), lax.axis_index("c")
        my, sibling = 4 * x + 2 * y + c, (x, y, 1 - c)
        slots = [(0, my, x_ref)] + [(1 + k, pid, land_ref.at[pid]) for k, pid, _ in _peers_same_core(x, y, c)]
        for k, slot, src in slots:
            pltpu.make_async_remote_copy(
                src_ref=src, dst_ref=land_ref.at[slot], send_sem=send_sems.at[k], recv_sem=recv_sems.at[k],
                device_id=sibling, device_id_type=MESH).start()
        token[...] = jnp.zeros_like(token)

    return pl.pallas_call(
        body, name=name,
        out_shape=(pltpu.SemaphoreType.DMA((4,)), pltpu.SemaphoreType.DMA((4,)),
                   pltpu.HBM(block.shape, block.dtype), pltpu.HBM(land.shape, land.dtype),
                   jax.ShapeDtypeStruct((8, LANES), F32)),
        in_specs=(_HBM, _HBM),
        out_specs=(_SEM, _SEM, _HBM, _HBM, pl.BlockSpec(memory_space=pltpu.VMEM)),
        input_output_aliases={0: 2, 1: 3},
        compiler_params=pltpu.CompilerParams(has_side_effects=_EFFECT),
    )(block, land)


def _forward_wait(handles, after, name):
    send_sems, recv_sems, block_thru, land_thru, _ = handles

    def body(x_ref, land_ref, send_sems, recv_sems, after_ref, x_dead, got_ref, buf, local_sems):
        x, y, c = lax.axis_index("x"), lax.axis_index("y"), lax.axis_index("c")
        my, sib_id, sibling = 4 * x + 2 * y + c, 4 * x + 2 * y + 1 - c, (x, y, 1 - c)
        mine = _staged_copy(x_ref, land_ref.at[my], buf, local_sems)
        slots = [(0, my, sib_id)] + [(1 + k, pid, pid + 1 - 2 * c) for k, pid, _ in _peers_same_core(x, y, c)]
        for k, sent, got in slots:
            cp = pltpu.make_async_remote_copy(
                src_ref=land_ref.at[sent], dst_ref=land_ref.at[got], send_sem=send_sems.at[k],
                recv_sem=recv_sems.at[k], device_id=sibling, device_id_type=MESH)
            cp.wait_send()
            cp.wait_recv()
        mine.wait()

    return pl.pallas_call(
        body, name=name,
        out_shape=(pltpu.HBM(block_thru.shape, block_thru.dtype), pltpu.HBM(land_thru.shape, land_thru.dtype)),
        in_specs=(_HBM, _HBM, _SEM, _SEM, pl.BlockSpec(memory_space=pl.ANY)),
        out_specs=(_HBM, _HBM),
        input_output_aliases={0: 0, 1: 1},
        scratch_shapes=[pltpu.VMEM(block_thru.shape, block_thru.dtype), pltpu.SemaphoreType.DMA((2,))],
        compiler_params=pltpu.CompilerParams(has_side_effects=_EFFECT, vmem_limit_bytes=VMEM_LIMIT),
    )(block_thru, land_thru, send_sems, recv_sems, after)[1]


def _after(v, token):
    return v + token[0, 0].astype(v.dtype)


def _dsilu(s, silu):
    return s + silu * (1.0 - s)


def _log1p(x):
    u = 1.0 + x
    d = u - 1.0
    return jnp.where(d == 0.0, x, jnp.log(u) * (x / jnp.where(d == 0.0, 1.0, d)))


def _softplus_neg(lam):
    return jnp.maximum(-lam, 0.0) + _log1p(jnp.exp(-jnp.abs(lam)))


def _neg_expm1(y, exp_y):
    poly = -y * (1.0 + y * (0.5 + y * (1.0 / 6.0 + y * (1.0 / 24.0))))
    return jnp.where(y > -0.05, poly, 1.0 - exp_y)


def _sigmoid(x):
    return 0.5 * jnp.tanh(0.5 * x) + 0.5


def _shift_dn(cur, prev, k):
    ext = jnp.concatenate([prev, cur], axis=0)
    return pltpu.roll(ext, k, 0)[HALO:, :]


def _shift_up(cur, nxt, k):
    n = cur.shape[0]
    ext = jnp.concatenate([cur, nxt], axis=0)
    return pltpu.roll(ext, n + HALO - k, 0)[:n, :]


def _scan_fwd(a, b, h_prev):
    groups = a.shape[0] // SUBLANES
    a3 = a.reshape(groups, SUBLANES, LANES)
    b3 = b.reshape(groups, SUBLANES, LANES)
    row = lax.broadcasted_iota(jnp.int32, a3.shape, 1)
    k = 1
    while k < SUBLANES:
        a_sh = jnp.where(row >= k, pltpu.roll(a3, k, 1), 1.0)
        b_sh = jnp.where(row >= k, pltpu.roll(b3, k, 1), 0.0)
        b3 = a3 * b_sh + b3
        a3 = a3 * a_sh
        k *= 2
    carry = h_prev[HALO - 1:HALO, :]
    out = []
    for i in range(groups):
        hg = b3[i] + a3[i] * carry
        out.append(hg)
        carry = hg[SUBLANES - 1:SUBLANES, :]
    return jnp.concatenate(out, axis=0)


def _scan_rev(a_next, g, lam_next):
    groups = g.shape[0] // SUBLANES
    a3 = a_next.reshape(groups, SUBLANES, LANES)
    g3 = g.reshape(groups, SUBLANES, LANES)
    row = lax.broadcasted_iota(jnp.int32, a3.shape, 1)
    k = 1
    while k < SUBLANES:
        ok = row < SUBLANES - k
        a_sh = jnp.where(ok, pltpu.roll(a3, SUBLANES - k, 1), 1.0)
        g_sh = jnp.where(ok, pltpu.roll(g3, SUBLANES - k, 1), 0.0)
        g3 = g3 + a3 * g_sh
        a3 = a3 * a_sh
        k *= 2
    carry = lam_next[0:1, :]
    out = [None] * groups
    for i in reversed(range(groups)):
        lg = g3[i] + a3[i] * carry
        out[i] = lg
        carry = lg[0:1, :]
    return jnp.concatenate(out, axis=0)


def _rowsum(v):
    return jnp.sum(v, axis=0, keepdims=True)


def _dot(a, b):
    return jnp.dot(a, b, preferred_element_type=F32)


def _dot_nt(a, b):
    return lax.dot_general(a, b, (((1,), (1,)), ((), ())), preferred_element_type=F32)


def _dot_tn(a, b):
    return lax.dot_general(a, b, (((0,), (0,)), ((), ())), preferred_element_type=F32)


class _MixerWeights:
    def __init__(self, caw_ref, sw_ref, sb_ref, lcw_ref, lcb_ref, wa_ref, wx_ref, ba_ref, bx_ref, lam_ref):
        self.caw = [caw_ref[j:j + 1, :] for j in range(3)]
        self.lcw = [lcw_ref[j:j + 1, :] for j in range(4)]
        self.lcb = lcb_ref[...]
        row = lax.broadcasted_iota(jnp.int32, (CHUNK, CHUNK), 0)
        col = lax.broadcasted_iota(jnp.int32, (CHUNK, CHUNK), 1)
        self.tril = col <= row
        self.sw = jnp.where(self.tril, sw_ref[...], 0.0).astype(BF16)
        self.sb = sb_ref[...]
        self.wa = wa_ref[...]
        self.wx = wx_ref[...]
        self.ba = ba_ref[...]
        self.bx = bx_ref[...]
        lam = lam_ref[...]
        self.neg_c_sp = -LRU_C * _softplus_neg(lam)
        self.dsp_dlam = -_sigmoid(-lam)


def _mixer_a(ld, ldp, w, cv=None):
    t = {}
    a_x, a_c = ld(AX), ld(AC)
    t["a_x"], t["a_c"], t["a_b"], t["a_z"] = a_x, a_c, ld(AB), ld(AZ)
    t["ca"] = ca = a_c * a_x
    if cv is None:
        ca_p = ldp(AC) * ldp(AX)
        cv = w.caw[2] * ca + w.caw[1] * _shift_dn(ca, ca_p, 1) + w.caw[0] * _shift_dn(ca, ca_p, 2)
    t["cv"] = cv
    t["sa"] = _sigmoid(t["a_z"])
    t["silu_az"] = t["a_z"] * t["sa"]
    t["y_a"] = t["silu_az"] * t["a_b"] * t["cv"]
    return t


def _mixer_b(ld, w):
    t = {}
    v = ld(SV)
    vc = v - jnp.mean(v, axis=1, keepdims=True)
    t["rstd"] = lax.rsqrt(jnp.mean(vc * vc, axis=1, keepdims=True) + EPS)
    t["vn"] = vc * t["rstd"]
    t["z"] = _dot(w.sw, t["vn"].astype(BF16)) + w.sb
    t["s_u"], t["s_z"] = ld(SU), ld(SZ)
    t["ss"] = _sigmoid(t["s_z"])
    t["silu_sz"] = t["s_z"] * t["ss"]
    t["y_s"] = t["silu_sz"] * t["s_u"] * t["z"]
    return t


def _mixer_c(ld, ldp, w, backward, xc=None):
    t = {}
    t["r_x"] = r_x = ld(RX)
    if xc is None:
        r_xp = ldp(RX)
        xc = (w.lcb + w.lcw[0] * _shift_dn(r_x, r_xp, 3) + w.lcw[1] * _shift_dn(r_x, r_xp, 2)
              + w.lcw[2] * _shift_dn(r_x, r_xp, 1) + w.lcw[3] * r_x)
    t["xc"] = xc
    xcb = xc.astype(BF16)
    t["r"] = _sigmoid(_dot(xcb, w.wa) + w.ba)
    t["i"] = _sigmoid(_dot(xcb, w.wx) + w.bx)
    la = t["r"] * w.neg_c_sp
    t["a"] = jnp.exp(la)
    t["a2"] = t["a"] * t["a"]
    t["em"] = _neg_expm1(2.0 * la, t["a2"])
    if backward:
        t["inv_mult"] = lax.rsqrt(t["em"])
        t["mult"] = t["em"] * t["inv_mult"]
    else:
        t["mult"] = jnp.sqrt(t["em"])
    t["b"] = t["mult"] * (t["i"] * xc)
    t["r_z"] = ld(RZ)
    t["sr"] = _sigmoid(t["r_z"])
    t["silu_rz"] = t["r_z"] * t["sr"]
    return t


def _mixer_pre_scan(ld, ldp, w, backward, cv=None, xc=None):
    t = {**_mixer_a(ld, ldp, w, cv), **_mixer_b(ld, w), **_mixer_c(ld, ldp, w, backward, xc)}
    t["ga"], t["gs"], t["gr"] = _sigmoid(ld(GA)), _sigmoid(ld(GS)), _sigmoid(ld(GR))
    return t


def _weight_specs(n_cb_axis):
    def at(fn):
        return lambda *g: fn(g[n_cb_axis])
    return [
        pl.BlockSpec((3, LANES), at(lambda cb: (0, cb))),
        pl.BlockSpec((None, CHUNK, CHUNK), at(lambda cb: (cb, 0, 0))),
        pl.BlockSpec((None, CHUNK, LANES), at(lambda cb: (cb, 0, 0))),
        pl.BlockSpec((4, LANES), at(lambda cb: (0, cb))),
        pl.BlockSpec((1, LANES), at(lambda cb: (0, cb))),
        pl.BlockSpec((None, LANES, LANES), at(lambda cb: (cb, 0, 0))),
        pl.BlockSpec((None, LANES, LANES), at(lambda cb: (cb, 0, 0))),
        pl.BlockSpec((1, LANES), at(lambda cb: (0, cb))),
        pl.BlockSpec((1, LANES), at(lambda cb: (0, cb))),
        pl.BlockSpec((1, LANES), at(lambda cb: (0, cb))),
    ]


def _chunk_loaders(p_ref, c):
    r0 = pl.multiple_of(c * CHUNK, CHUNK)
    rp = pl.multiple_of(jnp.maximum(c * CHUNK - HALO, 0), HALO)

    def ld(j):
        return p_ref[j, pl.ds(r0, CHUNK), :].astype(F32)

    def ldp(j):
        return jnp.where(c > 0, p_ref[j, pl.ds(rp, HALO), :].astype(F32), 0.0)

    return r0, rp, ld, ldp


def _mixer_fwd(proj, mw):
    _, nb, s, _ = proj.shape
    n_chunks = s // CHUNK

    def body(p_ref, *refs):
        w = _MixerWeights(*refs[:10])
        merged_ref, hs_ref, cv_ref, xc_ref = refs[10:]

        def chunk(c, h_prev):
            r0, _, ld, ldp = _chunk_loaders(p_ref, c)
            t = _mixer_pre_scan(ld, ldp, w, False)
            h = _scan_fwd(t["a"], t["b"], h_prev)
            y_r = t["silu_rz"] * h
            merged = t["ga"] * t["y_a"] + t["gs"] * t["y_s"] + t["gr"] * y_r
            merged_ref[pl.ds(r0, CHUNK), :] = merged.astype(BF16)
            hs_ref[pl.ds(r0, CHUNK), :] = h
            cv_ref[pl.ds(r0, CHUNK), :] = t["cv"].astype(BF16)
            xc_ref[pl.ds(r0, CHUNK), :] = t["xc"].astype(BF16)
            return h[CHUNK - HALO:, :]

        def group(i, carry):
            for k in range(FWD_CHUNKS_PER_TRIP):
                carry = chunk(FWD_CHUNKS_PER_TRIP * i + k, carry)
            return carry

        assert n_chunks % FWD_CHUNKS_PER_TRIP == 0
        lax.fori_loop(0, n_chunks // FWD_CHUNKS_PER_TRIP, group, jnp.zeros((HALO, LANES), F32))

    slab = pl.BlockSpec((None, s, LANES), lambda cb, b: (b, 0, cb))
    half = jax.ShapeDtypeStruct((nb, s, D), BF16)
    return pl.pallas_call(
        body, name="mixer_fwd", grid=(D // LANES, nb),
        in_specs=[pl.BlockSpec((N_SEG, None, s, LANES), lambda cb, b: (0, b, 0, cb))] + _weight_specs(0),
        out_specs=[slab, slab, slab, slab],
        out_shape=[half, jax.ShapeDtypeStruct((nb, s, D), F32), half, half],
        compiler_params=_params(("arbitrary", "arbitrary")),
    )(proj, *mw)


def _mixer_bwd(proj, dmerged, hs, cv, xc, mw):
    _, nb, s, _ = proj.shape
    n_chunks = s // CHUNK

    def body(p_ref, dm_ref, hs_ref, cv_ref, xc_ref, *refs):
        w = _MixerWeights(*refs[:10])
        dp_ref, g_caw, g_sw, g_sb, g_lcw, g_vec, g_wa, g_wx = refs[10:]

        @pl.when(pl.program_id(1) == 0)
        def _():
            for ref in (g_caw, g_sw, g_sb, g_lcw, g_vec, g_wa, g_wx):
                ref[...] = jnp.zeros_like(ref)

        def chunk(i, carry):
            dcv_n, dxc_n, lam_n, a_n = carry
            c = n_chunks - 1 - i
            r0, rp, ld, ldp = _chunk_loaders(p_ref, c)
            t = _mixer_pre_scan(ld, ldp, w, True, cv_ref[pl.ds(r0, CHUNK), :].astype(F32),
                                xc_ref[pl.ds(r0, CHUNK), :].astype(F32))
            h = hs_ref[pl.ds(r0, CHUNK), :]
            h_p = jnp.where(c > 0, hs_ref[pl.ds(rp, HALO), :], 0.0)
            h_prev = _shift_dn(h, h_p, 1)
            dm = dm_ref[pl.ds(r0, CHUNK), :].astype(F32)
            y_r = t["silu_rz"] * h

            def out(j, val):
                dp_ref[j, pl.ds(r0, CHUNK), :] = val.astype(BF16)

            ga, gs, gr = t["ga"], t["gs"], t["gr"]
            dy_a, dy_s, dy_r = dm * ga, dm * gs, dm * gr
            out(GA, (dy_a * t["y_a"]) * (1.0 - ga))
            out(GS, (dy_s * t["y_s"]) * (1.0 - gs))
            out(GR, (dy_r * y_r) * (1.0 - gr))

            dy_ab = dy_a * t["a_b"]
            out(AZ, dy_ab * (t["cv"] * _dsilu(t["sa"], t["silu_az"])))
            out(AB, (dy_a * t["cv"]) * t["silu_az"])
            dcv = dy_ab * t["silu_az"]
            dcv1, dcv2 = _shift_up(dcv, dcv_n, 1), _shift_up(dcv, dcv_n, 2)
            dca = w.caw[2] * dcv + w.caw[1] * dcv1 + w.caw[0] * dcv2
            out(AC, dca * t["a_x"])
            out(AX, dca * t["a_c"])
            g_caw[2:3, :] += _rowsum(dcv * t["ca"])
            g_caw[1:2, :] += _rowsum(dcv1 * t["ca"])
            g_caw[0:1, :] += _rowsum(dcv2 * t["ca"])

            dy_su = dy_s * t["s_u"]
            out(SZ, dy_su * (t["z"] * _dsilu(t["ss"], t["silu_sz"])))
            out(SU, (dy_s * t["z"]) * t["silu_sz"])
            dz = dy_su * t["silu_sz"]
            dzb = dz.astype(BF16)
            g_sb[...] += jnp.broadcast_to(jnp.sum(dz, axis=1, keepdims=True), (CHUNK, LANES))
            g_sw[...] += _dot_nt(dzb, t["vn"].astype(BF16))
            dvn = _dot_tn(w.sw, dzb)
            vn = t["vn"]
            out(SV, t["rstd"] * (dvn - jnp.mean(dvn, axis=1, keepdims=True)
                                 - vn * jnp.mean(dvn * vn, axis=1, keepdims=True)))

            out(RZ, (dy_r * h) * _dsilu(t["sr"], t["silu_rz"]))
            lam = _scan_rev(_shift_up(t["a"], a_n, 1), dy_r * t["silu_rz"], lam_n)
            a, r, ig, xc = t["a"], t["r"], t["i"], t["xc"]
            lam_mult, lam_i = lam * t["mult"], lam * ig
            d_i = lam_mult * xc
            d_mult = lam_i * xc
            dxc = lam_mult * ig
            dla = lam * h_prev * a - d_mult * (t["a2"] * t["inv_mult"])
            g_vec[3:4, :] += _rowsum(dla * r) * (-LRU_C * w.dsp_dlam)
            dpr = (dla * w.neg_c_sp) * r * (1.0 - r)
            dpi = d_i * ig * (1.0 - ig)
            dprb, dpib, xcb = dpr.astype(BF16), dpi.astype(BF16), xc.astype(BF16)
            g_wa[...] += _dot_tn(xcb, dprb)
            g_wx[...] += _dot_tn(xcb, dpib)
            g_vec[1:2, :] += _rowsum(dpr)
            g_vec[2:3, :] += _rowsum(dpi)
            dxc = dxc + _dot_nt(dprb, w.wa) + _dot_nt(dpib, w.wx)
            g_vec[0:1, :] += _rowsum(dxc)
            dxcs = [_shift_up(dxc, dxc_n, 3), _shift_up(dxc, dxc_n, 2), _shift_up(dxc, dxc_n, 1), dxc]
            out(RX, w.lcw[3] * dxcs[3] + w.lcw[2] * dxcs[2] + w.lcw[1] * dxcs[1] + w.lcw[0] * dxcs[0])
            for j in range(4):
                g_lcw[j:j + 1, :] += _rowsum(dxcs[j] * t["r_x"])
            return dcv[:HALO, :], dxc[:HALO, :], lam[:HALO, :], a[:HALO, :]

        zero = jnp.zeros((HALO, LANES), F32)
        def group(i, carry):
            for k in range(BWD_CHUNKS_PER_TRIP):
                carry = chunk(BWD_CHUNKS_PER_TRIP * i + k, carry)
            return carry

        assert n_chunks % BWD_CHUNKS_PER_TRIP == 0
        lax.fori_loop(0, n_chunks // BWD_CHUNKS_PER_TRIP, group, (zero, zero, zero, zero))

        @pl.when(pl.program_id(1) == nb - 1)
        def _():
            g_sw[...] = jnp.where(w.tril, g_sw[...], 0.0)

    slab = lambda dt: pl.BlockSpec((None, s, LANES), lambda cb, b: (b, 0, cb))
    seg = pl.BlockSpec((N_SEG, None, s, LANES), lambda cb, b: (0, b, 0, cb))
    rows = lambda n: pl.BlockSpec((n, LANES), lambda cb, b: (0, cb))
    sq = pl.BlockSpec((None, LANES, LANES), lambda cb, b: (cb, 0, 0))
    n_cb = D // LANES
    return pl.pallas_call(
        body, name="mixer_bwd", grid=(n_cb, nb),
        in_specs=[seg, slab(BF16), slab(F32), slab(BF16), slab(BF16)] + _weight_specs(0),
        out_specs=[seg, rows(3), sq, sq, rows(4), rows(8), sq, sq],
        out_shape=[
            jax.ShapeDtypeStruct(proj.shape, BF16),
            jax.ShapeDtypeStruct((3, D), F32),
            jax.ShapeDtypeStruct((n_cb, CHUNK, CHUNK), F32),
            jax.ShapeDtypeStruct((n_cb, CHUNK, LANES), F32),
            jax.ShapeDtypeStruct((4, D), F32),
            jax.ShapeDtypeStruct((8, D), F32),
            jax.ShapeDtypeStruct((n_cb, LANES, LANES), F32),
            jax.ShapeDtypeStruct((n_cb, LANES, LANES), F32),
        ],
        compiler_params=_params(("arbitrary", "arbitrary")),
    )(proj, dmerged, hs, cv, xc, *mw)


def _row_tile(s, want):
    return want if s % want == 0 else s


def _norm_mod(x, gain, shift, scale):
    nb, s, _ = x.shape
    tm = _row_tile(s, 512)

    def body(x_ref, g_ref, sh_ref, sc_ref, h_ref, ht_ref):
        xv = x_ref[...]
        r = lax.rsqrt(jnp.mean(xv * xv, axis=1, keepdims=True) + EPS)
        h = ((xv * r) * g_ref[...] * (1.0 + sc_ref[...]) + sh_ref[...]).astype(BF16)
        h_ref[...] = h
        ht_ref[...] = h.T

    tile = pl.BlockSpec((None, tm, D), lambda b, m: (b, m, 0))
    vec = pl.BlockSpec((None, 1, D), lambda b, m: (b, 0, 0))
    return pl.pallas_call(
        body, name="norm_mod", grid=(nb, s // tm),
        in_specs=[tile, pl.BlockSpec((1, D), lambda b, m: (0, 0)), vec, vec],
        out_specs=[tile, pl.BlockSpec((None, D, tm), lambda b, m: (b, 0, m))],
        out_shape=[jax.ShapeDtypeStruct(x.shape, BF16), jax.ShapeDtypeStruct((nb, D, s), BF16)],
        compiler_params=_params(("arbitrary", "arbitrary")),
    )(x, gain, shift, scale)


def _in_proj(h, wg, dep):
    nb, s, _ = h.shape

    def body(h_ref, w0_ref, w1_ref, dep_ref, o_ref):
        hv = h_ref[...]
        o_ref[:, :UNIT] = _dot(hv, w0_ref[...]).astype(BF16)
        o_ref[:, UNIT:] = _dot(hv, w1_ref[...]).astype(BF16)

    def unit(k):
        return pl.BlockSpec((None, D, UNIT),
                            lambda b, j: ((2 * j + k) // UNITS_PER_DEV, 0, (2 * j + k) % UNITS_PER_DEV))

    return pl.pallas_call(
        body, name="in_proj", grid=(nb, N_SEG),
        in_specs=[pl.BlockSpec((None, s, D), lambda b, j: (b, 0, 0)), unit(0), unit(1),
                  pl.BlockSpec((8, LANES), lambda b, j: (0, 0))],
        out_specs=pl.BlockSpec((None, None, s, D), lambda b, j: (j, b, 0, 0)),
        out_shape=jax.ShapeDtypeStruct((N_SEG, nb, s, D), BF16),
        compiler_params=_params(("arbitrary", "arbitrary")),
    )(h, wg, wg, dep)


def _out_proj(x, merged, wout, gate):
    nb, s, _ = x.shape
    tm = _row_tile(s, 512)

    def body(x_ref, m_ref, w_ref, g_ref, o_ref):
        o_ref[...] = x_ref[...] + g_ref[...] * _dot(m_ref[...], w_ref[...])

    tile = pl.BlockSpec((None, tm, D), lambda b, m: (b, m, 0))
    return pl.pallas_call(
        body, name="out_proj", grid=(nb, s // tm),
        in_specs=[tile, tile, pl.BlockSpec((D, D), lambda b, m: (0, 0)),
                  pl.BlockSpec((None, 1, D), lambda b, m: (b, 0, 0))],
        out_specs=tile, out_shape=jax.ShapeDtypeStruct(x.shape, F32),
        compiler_params=_params(("arbitrary", "arbitrary")),
    )(x, merged, wout, gate)


def _out_proj_norm(x, merged, wout, gate, gain, shift, scale):
    nb, s, _ = x.shape
    tm = _row_tile(s, 512)

    def body(x_ref, m_ref, w_ref, g_ref, gn_ref, sh_ref, sc_ref, o_ref, h_ref, ht_ref):
        xv = x_ref[...] + g_ref[...] * _dot(m_ref[...], w_ref[...])
        o_ref[...] = xv
        r = lax.rsqrt(jnp.mean(xv * xv, axis=1, keepdims=True) + EPS)
        h = ((xv * r) * gn_ref[...] * (1.0 + sc_ref[...]) + sh_ref[...]).astype(BF16)
        h_ref[...] = h
        ht_ref[...] = h.T

    tile = pl.BlockSpec((None, tm, D), lambda b, m: (b, m, 0))
    vec = pl.BlockSpec((None, 1, D), lambda b, m: (b, 0, 0))
    return pl.pallas_call(
        body, name="out_proj_norm", grid=(nb, s // tm),
        in_specs=[tile, tile, pl.BlockSpec((D, D), lambda b, m: (0, 0)), vec,
                  pl.BlockSpec((1, D), lambda b, m: (0, 0)), vec, vec],
        out_specs=[tile, tile, pl.BlockSpec((None, D, tm), lambda b, m: (b, 0, m))],
        out_shape=[jax.ShapeDtypeStruct(x.shape, F32), jax.ShapeDtypeStruct(x.shape, BF16),
                   jax.ShapeDtypeStruct((nb, D, s), BF16)],
        compiler_params=_params(("arbitrary", "arbitrary")),
    )(x, merged, wout, gate, gain, shift, scale)


def _loss_head(x, gain, target):
    nb, s, _ = x.shape
    tm = _row_tile(s, 512)

    def body(x_ref, g_ref, t_ref, loss_ref, dx_ref, dg_ref):
        first = (pl.program_id(0) == 0) & (pl.program_id(1) == 0)
        last = (pl.program_id(0) == nb - 1) & (pl.program_id(1) == s // tm - 1)

        @pl.when(first)
        def _():
            loss_ref[...] = jnp.zeros_like(loss_ref)
            dg_ref[...] = jnp.zeros_like(dg_ref)

        xv = x_ref[...]
        r = lax.rsqrt(jnp.mean(xv * xv, axis=1, keepdims=True) + EPS)
        xn = xv * r
        g = g_ref[...]
        e = xn * g - t_ref[...]
        loss_ref[...] += _rowsum(e * e) * (0.5 / D)
        dy = e * (1.0 / D)
        dg_ref[...] += _rowsum(dy * xn)
        dxn = dy * g
        dx_ref[...] = r * (dxn - xn * jnp.mean(dxn * xn, axis=1, keepdims=True))

        @pl.when(last)
        def _():
            loss_ref[...] = jnp.broadcast_to(jnp.sum(loss_ref[...], axis=1, keepdims=True), (1, D))

    tile = pl.BlockSpec((None, tm, D), lambda b, m: (b, m, 0))
    vec = pl.BlockSpec((1, D), lambda b, m: (0, 0))
    return pl.pallas_call(
        body, name="loss_head", grid=(nb, s // tm),
        in_specs=[tile, vec, tile], out_specs=[vec, tile, vec],
        out_shape=[jax.ShapeDtypeStruct((1, D), F32), jax.ShapeDtypeStruct(x.shape, F32),
                   jax.ShapeDtypeStruct((1, D), F32)],
        compiler_params=_params(("arbitrary", "arbitrary")),
    )(x, gain, target)


def _out_proj_bwd(dxo, merged, wout, gate):
    nb, s, _ = dxo.shape
    tm = _row_tile(s, 512)

    def body(d_ref, m_ref, w_ref, g_ref, dm_ref, gw_ref, dg_ref):
        @pl.when((pl.program_id(0) == 0) & (pl.program_id(1) == 0))
        def _():
            gw_ref[...] = jnp.zeros_like(gw_ref)

        @pl.when(pl.program_id(1) == 0)
        def _():
            dg_ref[...] = jnp.zeros_like(dg_ref)

        d = d_ref[...]
        m = m_ref[...]
        wv = w_ref[...]
        dg_ref[...] += _rowsum(d * _dot(m, wv))
        dout = (d * g_ref[...]).astype(BF16)
        dm_ref[...] = _dot_nt(dout, wv).astype(BF16)
        gw_ref[...] += _dot_tn(m, dout)

    tile = pl.BlockSpec((None, tm, D), lambda b, m: (b, m, 0))
    vec = pl.BlockSpec((None, 1, D), lambda b, m: (b, 0, 0))
    full = pl.BlockSpec((D, D), lambda b, m: (0, 0))
    return pl.pallas_call(
        body, name="out_proj_bwd", grid=(nb, s // tm),
        in_specs=[tile, tile, full, vec], out_specs=[tile, full, vec],
        out_shape=[jax.ShapeDtypeStruct(dxo.shape, BF16), jax.ShapeDtypeStruct((D, D), F32),
                   jax.ShapeDtypeStruct((nb, 1, D), F32)],
        compiler_params=_params(("arbitrary", "arbitrary")),
    )(dxo, merged, wout, gate)


def _in_proj_bwd_h(dproj, wg, dep):
    _, nb, s, _ = dproj.shape
    tm = _row_tile(s, 1024)

    def body(dp0_ref, dp1_ref, w0_ref, w1_ref, w2_ref, w3_ref, dep_ref, dh_ref):
        j = pl.program_id(2)
        part = (_dot_nt(dp0_ref[...], jnp.concatenate([w0_ref[...], w1_ref[...]], axis=1))
                + _dot_nt(dp1_ref[...], jnp.concatenate([w2_ref[...], w3_ref[...]], axis=1)))

        @pl.when(j == 0)
        def _():
            dh_ref[...] = part

        @pl.when(j > 0)
        def _():
            dh_ref[...] += part

    def seg(k):
        return pl.BlockSpec((None, None, tm, D), lambda b, m, j: (2 * j + k, b, m, 0))

    def unit(k):
        return pl.BlockSpec((None, D, UNIT),
                            lambda b, m, j: ((4 * j + k) // UNITS_PER_DEV, 0, (4 * j + k) % UNITS_PER_DEV))

    return pl.pallas_call(
        body, name="in_proj_bwd_h", grid=(nb, s // tm, N_SEG // 2),
        in_specs=[seg(0), seg(1), unit(0), unit(1), unit(2), unit(3),
                  pl.BlockSpec((8, LANES), lambda b, m, j: (0, 0))],
        out_specs=pl.BlockSpec((None, tm, D), lambda b, m, j: (b, m, 0)),
        out_shape=jax.ShapeDtypeStruct((nb, s, D), F32),
        compiler_params=_params(("arbitrary", "arbitrary", "arbitrary")),
    )(dproj, dproj, wg, wg, wg, wg, dep)


def _norm_mod_bwd(dh, x, dxo, gain, scale):
    nb, s, _ = x.shape
    tm = _row_tile(s, 512)

    def body(dh_ref, x_ref, dxo_ref, g_ref, sc_ref, dx_ref, dsh_ref, dsc_ref, dg_ref):
        b, m = pl.program_id(0), pl.program_id(1)

        @pl.when((b == 0) & (m == 0))
        def _():
            dg_ref[...] = jnp.zeros_like(dg_ref)

        @pl.when(m == 0)
        def _():
            dsh_ref[...] = jnp.zeros_like(dsh_ref)
            dsc_ref[...] = jnp.zeros_like(dsc_ref)

        dh = dh_ref[...]
        xv = x_ref[...]
        r = lax.rsqrt(jnp.mean(xv * xv, axis=1, keepdims=True) + EPS)
        xn = xv * r
        g = g_ref[...]
        one_sc = 1.0 + sc_ref[...]
        dsh_ref[...] += _rowsum(dh)
        dsc_ref[...] += _rowsum(dh * (xn * g))
        dg_ref[...] += _rowsum(dh * one_sc * xn)
        dxn = dh * (g * one_sc)
        dx_ref[...] = dxo_ref[...] + r * (dxn - xn * jnp.mean(dxn * xn, axis=1, keepdims=True))

    tile = pl.BlockSpec((None, tm, D), lambda b, m: (b, m, 0))
    vec = pl.BlockSpec((None, 1, D), lambda b, m: (b, 0, 0))
    one = pl.BlockSpec((1, D), lambda b, m: (0, 0))
    return pl.pallas_call(
        body, name="norm_mod_bwd", grid=(nb, s // tm),
        in_specs=[tile, tile, tile, one, vec],
        out_specs=[tile, vec, vec, one],
        out_shape=[jax.ShapeDtypeStruct(x.shape, F32), jax.ShapeDtypeStruct((nb, 1, D), F32),
                   jax.ShapeDtypeStruct((nb, 1, D), F32), jax.ShapeDtypeStruct((1, D), F32)],
        compiler_params=_params(("arbitrary", "arbitrary")),
    )(dh, x, dxo, gain, scale)


def _in_proj_bwd_w(ht, dproj, dep):
    nb, _, s = ht.shape
    tm = _row_tile(s, 2048)
    n_m = s // tm

    def body(ht_ref, dp_ref, dep_ref, o_ref, acc_ref):
        b, m = pl.program_id(1), pl.program_id(2)

        @pl.when((b == 0) & (m == 0))
        def _():
            acc_ref[...] = jnp.zeros_like(acc_ref)

        acc_ref[...] += _dot(ht_ref[...], dp_ref[...])

        @pl.when((b == nb - 1) & (m == n_m - 1))
        def _():
            o_ref[0] = acc_ref[:, :UNIT].astype(BF16)
            o_ref[1] = acc_ref[:, UNIT:].astype(BF16)

    return pl.pallas_call(
        body, name="in_proj_bwd_w", grid=(N_SEG, nb, n_m),
        in_specs=[pl.BlockSpec((None, D, tm), lambda j, b, m: (b, 0, m)),
                  pl.BlockSpec((None, None, tm, D), lambda j, b, m: (j, b, m, 0)),
                  pl.BlockSpec((8, LANES), lambda j, b, m: (0, 0))],
        out_specs=pl.BlockSpec((2, D, UNIT), lambda j, b, m: (j, 0, 0)),
        out_shape=jax.ShapeDtypeStruct((2 * N_SEG, D, UNIT), BF16),
        scratch_shapes=[pltpu.VMEM((D, D), F32)],
        compiler_params=_params(("arbitrary", "arbitrary", "arbitrary")),
    )(ht, dproj, dep)


def _mod_proj(c_all, w_mod, b_mod_mine):
    nl, _, ncol = w_mod.shape
    nbg = c_all.shape[0]

    def body(c_ref, w_ref, b_ref, o_ref):
        cv = c_ref[...]
        o_ref[...] = jnp.dot(cv * jax.nn.sigmoid(cv), w_ref[...], preferred_element_type=F32,
                             precision=lax.Precision.HIGHEST) + b_ref[...]

    return pl.pallas_call(
        body, name="mod_proj", grid=(nl,),
        in_specs=[pl.BlockSpec((nbg, D), lambda l: (0, 0)), pl.BlockSpec((None, D, ncol), lambda l: (l, 0, 0)),
                  pl.BlockSpec((None, 1, ncol), lambda l: (l, 0, 0))],
        out_specs=pl.BlockSpec((None, nbg, ncol), lambda l: (l, 0, 0)),
        out_shape=jax.ShapeDtypeStruct((nl, nbg, ncol), F32),
        compiler_params=_params(("arbitrary",)),
    )(c_all, w_mod, b_mod_mine)


def _mod_grad(c_all, dmod_all, dmod_mine):
    nl, nbg, ncol = dmod_mine.shape

    def body(c_ref, da_ref, dm_ref, gw_ref, gb_ref):
        cv = c_ref[...]
        gw_ref[...] = lax.dot_general(cv * jax.nn.sigmoid(cv), dm_ref[...], (((0,), (0,)), ((), ())),
                                      preferred_element_type=F32, precision=lax.Precision.HIGHEST)
        gb_ref[...] = _rowsum(da_ref[...])

    return pl.pallas_call(
        body, name="mod_grad", grid=(nl,),
        in_specs=[pl.BlockSpec((nbg, D), lambda l: (0, 0)), pl.BlockSpec((None, nbg, 3 * D), lambda l: (l, 0, 0)),
                  pl.BlockSpec((None, nbg, ncol), lambda l: (l, 0, 0))],
        out_specs=[pl.BlockSpec((None, D, ncol), lambda l: (l, 0, 0)),
                   pl.BlockSpec((None, 1, 3 * D), lambda l: (l, 0, 0))],
        out_shape=[jax.ShapeDtypeStruct((nl, D, ncol), F32), jax.ShapeDtypeStruct((nl, 1, 3 * D), F32)],
        compiler_params=_params(("arbitrary",)),
    )(c_all, dmod_all, dmod_mine)


def _adamw(parts, w, m, v, name, layer=None, prev=None):
    n_parts, n_u, n_r, cu = parts.shape
    assert w.shape[-2:] == (n_r, n_u * cu), (parts.shape, w.shape)
    tr = n_r
    for cand in (512, 256, 128):
        if n_r > cand and n_r % cand == 0:
            tr = cand
            break
    n_prev = 0 if prev is None else 4

    def body(p_ref, w_ref, m_ref, v_ref, *rest):
        g_ref, d_ref, nm_ref, nv_ref = rest[n_prev:]
        g = p_ref[0].astype(F32)
        for k in range(1, n_parts):
            g = g + p_ref[k].astype(F32)
        m2 = ADAM_B1 * m_ref[...] + (1.0 - ADAM_B1) * g
        v2 = ADAM_B2 * v_ref[...] + (1.0 - ADAM_B2) * (g * g)
        m_hat = m2 / (1.0 - ADAM_B1 ** ADAM_STEP)
        v_hat = v2 / (1.0 - ADAM_B2 ** ADAM_STEP)
        g_ref[...] = g
        d_ref[...] = -ADAM_LR * (m_hat / (jnp.sqrt(v_hat) + ADAM_EPS) + ADAM_WD * w_ref[...])
        nm_ref[...] = m2
        nv_ref[...] = v2

    if layer is None:
        tile = pl.BlockSpec((tr, cu), lambda u, i: (i, u))
    else:
        tile = pl.BlockSpec((None, tr, cu), lambda u, i: (layer, i, u))
    shp = jax.ShapeDtypeStruct(w.shape, F32)
    return pl.pallas_call(
        body, name=name, grid=(n_u, n_r // tr),
        in_specs=[pl.BlockSpec((n_parts, None, tr, cu), lambda u, i: (0, u, i, 0)), tile, tile, tile]
        + [pl.BlockSpec(memory_space=pl.ANY)] * n_prev,
        out_specs=[tile, tile, tile, tile], out_shape=[shp, shp, shp, shp],
        input_output_aliases={4 + k: k for k in range(n_prev)},
        compiler_params=_params(("arbitrary", "arbitrary")),
    )(parts, w, m, v, *(prev or ()))


def _gathered_cols(g, inner):
    k = len(inner)
    perm = tuple(range(1, k + 1)) + (0, k + 1)
    t = jnp.transpose(g, perm)
    return t.reshape(tuple(inner) + (g.shape[0] * g.shape[-1],))


def _pair_blocks(wh):
    z = jnp.zeros((8, 64, 64), wh.dtype)
    w2 = wh.reshape(8, 2, 64, 64)
    top = jnp.concatenate([w2[:, 0], z], axis=2)
    bot = jnp.concatenate([z, w2[:, 1]], axis=2)
    return jnp.concatenate([top, bot], axis=1).astype(BF16)


def _unpair_blocks(g):
    return jnp.stack([g[:, :64, :64], g[:, 64:, 64:]], axis=1).reshape(16, 64, 64)


FLAT_ROWS = 512


def _pack_rows(arrays, lead=0):
    parts = [a.reshape(a.shape[:lead] + (-1, LANES)) for a in arrays]
    rows = jnp.concatenate(parts, axis=lead)
    pad = [(0, 0)] * rows.ndim
    pad[lead] = (0, (-rows.shape[lead]) % FLAT_ROWS)
    return jnp.pad(rows, pad)


def kernel(x, c, norm_gain, w_mod, b_mod, w_in, w_out, conv_a_w, sgu_w, sgu_b, lru_conv_w, lru_conv_b, lru_wa, lru_ba, lru_wx, lru_bx, lru_lambda, final_gain, loss_target, m_norm_gain, m_w_mod, m_b_mod, m_w_in, m_w_out, m_conv_a_w, m_sgu_w, m_sgu_b, m_lru_conv_w, m_lru_conv_b, m_lru_wa, m_lru_ba, m_lru_wx, m_lru_bx, m_lru_lambda, m_final_gain, v_norm_gain, v_w_mod, v_b_mod, v_w_in, v_w_out, v_conv_a_w, v_sgu_w, v_sgu_b, v_lru_conv_w, v_lru_conv_b, v_lru_wa, v_lru_ba, v_lru_wx, v_lru_bx, v_lru_lambda, v_final_gain):
    nl = w_in.shape[0]
    nb, s, _ = x.shape
    me = _my_index()
    mod_cols = w_mod.shape[2]


    small = jnp.concatenate([c.reshape(-1, LANES), conv_a_w.reshape(-1, LANES), lru_conv_w.reshape(-1, LANES)])
    n_c, n_ca = nb * D // LANES, nl * 3
    n_small = small.shape[0]
    small = jnp.pad(small, ((0, (-n_small) % 8), (0, 0)))
    small_all, _ = _all_gather(small, "gather_small")
    c_all = small_all[:, :n_c].reshape(N_DEV * nb, D)

    w_in_b = [w_in[0].astype(BF16)] + list(w_in[1:].astype(BF16))

    def start_w_in(l, dep):
        return _split_start(w_in_b[l], _peers_same_core, False, "gather_w_in_start", dep)

    conv_a_full = _gathered_cols(small_all[:, n_c:n_c + n_ca].reshape(N_DEV, nl, 3, LANES), (nl, 3))
    lru_conv_full = _gathered_cols(small_all[:, n_c + n_ca:n_small].reshape(N_DEV, nl, 4, LANES), (nl, 4))

    def gathered_w_in(started, after):
        block, land = _split_wait(started, after, _peers_same_core, False, False, "gather_w_in_wait")
        return _gather_finish(block, land, "gather_w_in_finish")

    sgu_b_lanes = jnp.broadcast_to(sgu_b[..., None], sgu_b.shape + (LANES,))
    mws = []
    for l in range(nl):
        mws.append((conv_a_full[l], sgu_w[l], sgu_b_lanes[l], lru_conv_full[l], lru_conv_b[l][None, :],
                    _pair_blocks(lru_wa[l]), _pair_blocks(lru_wx[l]), lru_ba[l].reshape(1, D),
                    lru_bx[l].reshape(1, D), lru_lambda[l][None, :]))

    rep_names = ["sgu_w", "sgu_b", "lru_conv_b", "lru_wa", "lru_ba", "lru_wx", "lru_bx", "lru_lambda"]
    rep_w = dict(sgu_w=sgu_w, sgu_b=sgu_b, lru_conv_b=lru_conv_b, lru_wa=lru_wa, lru_ba=lru_ba,
                 lru_wx=lru_wx, lru_bx=lru_bx, lru_lambda=lru_lambda)
    rep_m = dict(sgu_w=m_sgu_w, sgu_b=m_sgu_b, lru_conv_b=m_lru_conv_b, lru_wa=m_lru_wa,
                 lru_ba=m_lru_ba, lru_wx=m_lru_wx, lru_bx=m_lru_bx, lru_lambda=m_lru_lambda)
    rep_v = dict(sgu_w=v_sgu_w, sgu_b=v_sgu_b, lru_conv_b=v_lru_conv_b, lru_wa=v_lru_wa,
                 lru_ba=v_lru_ba, lru_wx=v_lru_wx, lru_bx=v_lru_bx, lru_lambda=v_lru_lambda)

    rep_w_all, rep_m_all, rep_v_all = [_pack_rows([src[n] for n in rep_names], lead=1)
                                       for src in (rep_w, rep_m, rep_v)]
    early = [rep_w_all, rep_m_all, rep_v_all] + w_in_b[1:] + [a for mw in mws for a in mw]

    b_mod_mine = lax.dynamic_slice_in_dim(b_mod, me * mod_cols, mod_cols, axis=1)[:, None, :]
    mod_mine = _mod_proj(c_all, w_mod, b_mod_mine)
    mod_all, mod_token = _all_gather(mod_mine.reshape(nl * N_DEV * nb, mod_cols), "gather_mod", dep=w_in_b[0])
    ici = {0: start_w_in(0, mod_token)}
    mod_full = _gathered_cols(mod_all.reshape(N_DEV, nl, N_DEV * nb, mod_cols), (nl, N_DEV * nb))
    mod_loc = lax.dynamic_slice_in_dim(mod_full, me * nb, nb, axis=1)
    shift, scale, gate = [mod_loc[:, :, j * D:(j + 1) * D][:, :, None, :] for j in range(3)]

    xs, hts, projs, mergeds, states, wg = [], [], [], [], [], []
    xl = x
    d2d = {}
    wo_started = _split_start(w_out.astype(BF16).reshape(nl * (D // N_DEV), D), _peers_all, False,
                              "gather_w_out_start", ici[0][4])
    wo = None
    h, ht = _norm_mod(xl, _after(norm_gain[0][None, :], wo_started[4]), shift[0], scale[0])
    for l in range(nl):
        if l == 0:
            wg_l, token = gathered_w_in(ici[0], [h] + early)
            ici[1] = start_w_in(1, token)
            dep = ici[1][4]
        else:
            wg_l = _forward_wait(d2d[l], h, "gather_w_in_d2d_wait")
            dep = d2d[l][4]
        wg.append(wg_l)
        proj = _in_proj(h, wg_l, dep)
        merged, *st = _mixer_fwd(proj, mws[l])
        xs.append(xl), hts.append(ht), projs.append(proj), mergeds.append(merged), states.append(st)
        gate_l = gate[l]
        if l + 1 < nl:
            block, land = _split_wait(ici[l + 1], merged, _peers_same_core, False, False, "gather_w_in_wait")
            d2d[l + 1] = _forward_start(block, land, "gather_w_in_d2d_start")
            gate_l = _after(gate_l, d2d[l + 1][4])
            if l + 2 < nl:
                ici[l + 2] = start_w_in(l + 2, d2d[l + 1][4])
                gate_l = _after(gate_l, ici[l + 2][4])
        if wo is None:
            _, wo_all = _split_wait(wo_started, merged, _peers_all, False, True, "gather_w_out_wait")
            wo = jnp.transpose(wo_all.reshape(N_DEV, nl, D // N_DEV, D), (1, 0, 2, 3)).reshape(nl, D, D)
        if l + 1 < nl:
            xl, h, ht = _out_proj_norm(xl, merged, wo[l], gate_l, norm_gain[l + 1][None, :], shift[l + 1], scale[l + 1])
        else:
            xl = _out_proj(xl, merged, wo[l], gate_l)

    loss_row, dx, g_final = _loss_head(xl, final_gain[None, :], loss_target)

    res_big, g_conv = {}, [None] * nl
    dmods = [None] * nl

    def finish_exchange(pending, after):
        l, h_in, h_out, h_rep = pending
        r_in, r_out, r_rep = _multi_wait([h_in, h_out, h_rep], after, "scatter_wait")
        res_big["w_in"] = _adamw(r_in, w_in, m_w_in, v_w_in, "adamw_w_in", l, res_big.get("w_in"))
        res_big["w_out"] = _adamw(r_out, w_out, m_w_out, v_w_out, "adamw_w_out", l, res_big.get("w_out"))
        res_big["rep"] = _adamw(r_rep[:, None], rep_w_all, rep_m_all, rep_v_all, "adamw_rep", l, res_big.get("rep"))

    pending = None
    g_gains = [None] * nl
    for l in reversed(range(nl)):
        dmerged, gw_out, dgate = _out_proj_bwd(dx, mergeds[l], wo[l], gate[l])
        dproj, g_caw, g_sw, g_sb, g_lcw, g_vec, g_wa, g_wx = _mixer_bwd(projs[l], dmerged, *states[l], mws[l])
        g_conv[l] = (g_caw, g_lcw)
        rep_g = dict(
            sgu_w=g_sw, sgu_b=g_sb[:, :, 0], lru_conv_b=g_vec[0],
            lru_wa=_unpair_blocks(g_wa), lru_ba=g_vec[1].reshape(16, 64), lru_wx=_unpair_blocks(g_wx),
            lru_bx=g_vec[2].reshape(16, 64), lru_lambda=g_vec[3])
        rep_block = _pack_rows([rep_g[n] for n in rep_names])
        (h_out, h_rep), token = _multi_start([(gw_out.reshape(N_DEV, 1, D // N_DEV, D), True), (rep_block, False)],
                                             "scatter_small_start")
        gw_in = _in_proj_bwd_w(hts[l], dproj, token)
        (h_in,), token = _multi_start([(gw_in.reshape(N_DEV, UNITS_PER_DEV, D, UNIT), True)], "scatter_w_in_start")
        started = (l, h_in, h_out, h_rep)
        dh = _in_proj_bwd_h(dproj, wg[l], token)
        dx, dshift, dscale, g_gain = _norm_mod_bwd(dh, xs[l], dx, norm_gain[l][None, :], scale[l])
        g_gains[l] = g_gain
        dmods[l] = jnp.concatenate([dshift, dscale, dgate], axis=2)[:, 0, :]
        if pending is not None:
            finish_exchange(pending, dx)
        pending = started

    conv_parts = jnp.concatenate(
        [jnp.stack([g_conv[l][0] for l in range(nl)]).reshape(nl * 3, N_DEV, LANES),
         jnp.stack([g_conv[l][1] for l in range(nl)]).reshape(nl * 4, N_DEV, LANES)], axis=0)
    conv_parts = jnp.transpose(conv_parts, (1, 0, 2))[:, None]
    conv_recv = _all_to_all(conv_parts, "scatter_conv")

    dmod_loc = jnp.stack(dmods).reshape(nl * nb, 3 * D)
    gain_rows = jnp.pad(jnp.concatenate(g_gains + [g_final, loss_row], axis=0),
                        ((0, (-(nl + 2)) % 8), (0, 2 * D)))
    tail_g, _ = _all_gather(jnp.concatenate([dmod_loc, gain_rows], axis=0), "gather_dmod",
                            dep=[res_big[k][3] for k in ("w_in", "w_out", "rep")])
    loss = jnp.sum(tail_g[:, nl * nb + nl + 1, 0])
    dmod_g = tail_g[:, :nl * nb]
    gain_parts = tail_g[:, nl * nb:nl * nb + nl + 1, :D][:, None]
    gain_cat = lambda a, b: jnp.concatenate([a, b[None, :]], axis=0)
    res_gain = _adamw(gain_parts, gain_cat(norm_gain, final_gain), gain_cat(m_norm_gain, m_final_gain),
                      gain_cat(v_norm_gain, v_final_gain), "adamw_gain")
    dmod_all = jnp.transpose(dmod_g.reshape(N_DEV, nl, nb, 3 * D), (1, 0, 2, 3)).reshape(nl, N_DEV * nb, 3 * D)
    dmod_mine = lax.dynamic_slice_in_dim(dmod_all, me * mod_cols, mod_cols, axis=2)
    gw_mod, gb_mod = _mod_grad(c_all, dmod_all, dmod_mine)
    res_w_mod = _adamw(gw_mod.reshape(1, 1, nl * D, mod_cols), w_mod.reshape(nl * D, mod_cols),
                       m_w_mod.reshape(nl * D, mod_cols), v_w_mod.reshape(nl * D, mod_cols), "adamw_w_mod")
    res_w_mod = [a.reshape(nl, D, mod_cols) for a in res_w_mod]
    res_b_mod = _adamw(gb_mod.reshape(1, 1, nl, 3 * D), b_mod, m_b_mod, v_b_mod, "adamw_b_mod")
    finish_exchange(pending, res_b_mod[1])

    cat = lambda a, b: jnp.concatenate([a.reshape(nl * 3, LANES), b.reshape(nl * 4, LANES)], axis=0)
    res_conv = _adamw(conv_recv, cat(conv_a_w, lru_conv_w), cat(m_conv_a_w, m_lru_conv_w),
                      cat(v_conv_a_w, v_lru_conv_w), "adamw_conv")
    res_conv_a = [a[:nl * 3].reshape(nl, 3, LANES) for a in res_conv]
    res_lru_conv = [a[nl * 3:].reshape(nl, 4, LANES) for a in res_conv]

    res_rep = []
    for k in range(4):
        off, d = 0, {}
        for n in rep_names:
            n_rows = rep_w[n][0].size // LANES
            d[n] = res_big["rep"][k][:, off:off + n_rows].reshape(rep_w[n].shape)
            off += n_rows
        res_rep.append(d)

    def leaf(k, name):
        if name == "norm_gain":
            return res_gain[k][:nl]
        if name == "final_gain":
            return res_gain[k][nl]
        if name == "w_mod":
            return res_w_mod[k]
        if name == "b_mod":
            return res_b_mod[k]
        if name in ("w_in", "w_out"):
            return res_big[name][k]
        if name == "conv_a_w":
            return res_conv_a[k]
        if name == "lru_conv_w":
            return res_lru_conv[k]
        return res_rep[k][name]

    order = ["norm_gain", "w_mod", "b_mod", "w_in", "w_out", "conv_a_w", "sgu_w", "sgu_b", "lru_conv_w",
             "lru_conv_b", "lru_wa", "lru_ba", "lru_wx", "lru_bx", "lru_lambda", "final_gain"]
    outs = [loss, dx]
    for k in range(4):
        outs += [leaf(k, n) for n in order]
    return tuple(outs)
```

```python
import functools

import jax
import jax.numpy as jnp
from jax import lax
from jax.experimental import pallas as pl
from jax.experimental.pallas import tpu as pltpu

F32 = jnp.float32
BF16 = jnp.bfloat16

D = 1024
N_DEV = 8
N_SEG = 12
LANES = 128
SUBLANES = 8
CHUNK = 128
HALO = 16
FWD_CHUNKS_PER_TRIP = 16
BWD_CHUNKS_PER_TRIP = 16
UNIT = 512
UNITS_PER_DEV = 3
EPS = 1e-6
LRU_C = 8.0
ADAM_LR, ADAM_B1, ADAM_B2, ADAM_EPS, ADAM_WD, ADAM_STEP = 0.001, 0.9, 0.999, 1e-08, 0.01, 10
VMEM_LIMIT = 56 * 1024 * 1024

AX, AB, AC, AZ, SU, SV, SZ, RX, RZ, GA, GS, GR = range(N_SEG)
MESH = pl.DeviceIdType.MESH


def _params(sem=None):
    return pltpu.CompilerParams(dimension_semantics=sem, vmem_limit_bytes=VMEM_LIMIT)


def _my_index():
    return 4 * lax.axis_index("x") + 2 * lax.axis_index("y") + lax.axis_index("c")


def _all_gather(block, name, dep=None):
    deps = [] if dep is None else list(dep) if isinstance(dep, (list, tuple)) else [dep]

    def body(x_ref, *refs):
        out_ref, token, send_sems, recv_sems, local_sem = refs[-5:]
        x, y, c = lax.axis_index("x"), lax.axis_index("y"), lax.axis_index("c")
        me, sibling = (x, y, c), (x, y, 1 - c)
        chips = [(1 - x, y), (x, 1 - y), (1 - x, 1 - y)]
        token[...] = jnp.zeros_like(token)

        def rows(px, py, pc):
            return out_ref.at[4 * px + 2 * py + pc]

        def copy(k, blk, to, src=None):
            return pltpu.make_async_remote_copy(
                src_ref=rows(*blk) if src is None else src, dst_ref=rows(*blk),
                send_sem=send_sems.at[k], recv_sem=recv_sems.at[k], device_id=to, device_id_type=MESH)

        mine = pltpu.make_async_copy(x_ref, rows(*me), local_sem)
        mine.start()
        first = [copy(0, me, sibling, src=x_ref)]
        first += [copy(1 + j, me, (*chip, c), src=x_ref) for j, chip in enumerate(chips)]
        for cp in first:
            cp.start()
        passed = [copy(4 + j, (*chip, c), sibling) for j, chip in enumerate(chips)]
        for j, chip in enumerate(chips):
            copy(1 + j, (*chip, c), me).wait_recv()
            passed[j].start()
        copy(0, sibling, me).wait_recv()
        for j, chip in enumerate(chips):
            copy(4 + j, (*chip, 1 - c), me).wait_recv()
        for cp in first + passed:
            cp.wait_send()
        mine.wait()

    return pl.pallas_call(
        body, name=name,
        out_shape=[jax.ShapeDtypeStruct((N_DEV,) + block.shape, block.dtype), jax.ShapeDtypeStruct((8, LANES), F32)],
        in_specs=[pl.BlockSpec(memory_space=pltpu.VMEM)]
        + [pl.BlockSpec(memory_space=pl.ANY)] * len(deps),
        out_specs=[pl.BlockSpec(memory_space=pl.ANY), pl.BlockSpec(memory_space=pltpu.VMEM)],
        scratch_shapes=[pltpu.SemaphoreType.DMA((7,)), pltpu.SemaphoreType.DMA((7,)), pltpu.SemaphoreType.DMA],
    )(block, *deps)


def _all_to_all(blocks, name):
    def body(x_ref, out_ref, send_sems, recv_sems, local_sem):
        x, y, c = lax.axis_index("x"), lax.axis_index("y"), lax.axis_index("c")
        my = 4 * x + 2 * y + c
        mine = pltpu.make_async_copy(x_ref.at[my], out_ref.at[my], local_sem)
        mine.start()
        peers = []
        for r in range(1, N_DEV):
            px = 1 - x if r & 4 else x
            py = 1 - y if r & 2 else y
            pc = 1 - c if r & 1 else c
            peers.append((r - 1, 4 * px + 2 * py + pc, (px, py, pc)))

        def copy(k, src_slot, dst_slot, to):
            return pltpu.make_async_remote_copy(
                src_ref=x_ref.at[src_slot], dst_ref=out_ref.at[dst_slot],
                send_sem=send_sems.at[k], recv_sem=recv_sems.at[k], device_id=to, device_id_type=MESH)

        sends = [copy(k, pid, my, to) for k, pid, to in peers]
        for cp in sends:
            cp.start()
        for k, pid, to in peers:
            copy(k, pid, pid, to).wait_recv()
        for cp in sends:
            cp.wait_send()
        mine.wait()

    return pl.pallas_call(
        body, name=name,
        out_shape=jax.ShapeDtypeStruct(blocks.shape, blocks.dtype),
        in_specs=[pl.BlockSpec(memory_space=pltpu.VMEM)],
        out_specs=pl.BlockSpec(memory_space=pl.ANY),
        scratch_shapes=[pltpu.SemaphoreType.DMA((7,)), pltpu.SemaphoreType.DMA((7,)), pltpu.SemaphoreType.DMA],
    )(blocks)


_HBM = pl.BlockSpec(memory_space=pltpu.HBM)
_SEM = pl.BlockSpec(memory_space=pltpu.SEMAPHORE)
_EFFECT = pltpu.SideEffectType.DATAFLOW_SIDE_EFFECTING


def _peers_all(x, y, c):
    out = []
    for r in range(1, N_DEV):
        px = 1 - x if r & 4 else x
        py = 1 - y if r & 2 else y
        pc = 1 - c if r & 1 else c
        out.append((r - 1, 4 * px + 2 * py + pc, (px, py, pc)))
    return out


def _peers_same_core(x, y, c):
    return [(k, 4 * px + 2 * py + c, (px, py, c))
            for k, (px, py) in enumerate([(1 - x, y), (x, 1 - y), (1 - x, 1 - y)])]


def _split_start(src, peers_fn, scatter, name, dep=None):
    blk = src.shape[1:] if scatter else src.shape
    land_shape = (N_DEV,) + tuple(blk)
    n = len(peers_fn(0, 0, 0))
    deps = [] if dep is None else [dep]

    def body(x_ref, land_ref, *rest):
        send_sems, recv_sems, x_thru, land_thru, token = rest[len(deps):]
        x, y, c = lax.axis_index("x"), lax.axis_index("y"), lax.axis_index("c")
        my = 4 * x + 2 * y + c
        for k, pid, to in peers_fn(x, y, c):
            pltpu.make_async_remote_copy(
                src_ref=x_ref.at[pid] if scatter else x_ref, dst_ref=land_ref.at[my],
                send_sem=send_sems.at[k], recv_sem=recv_sems.at[k], device_id=to, device_id_type=MESH).start()
        token[...] = jnp.zeros_like(token)

    return pl.pallas_call(
        body, name=name,
        out_shape=(pltpu.SemaphoreType.DMA((n,)), pltpu.SemaphoreType.DMA((n,)),
                   pltpu.HBM(src.shape, src.dtype), pltpu.HBM(land_shape, src.dtype),
                   jax.ShapeDtypeStruct((8, LANES), F32)),
        in_specs=(_HBM, _HBM) + (pl.BlockSpec(memory_space=pl.ANY),) * len(deps),
        out_specs=(_SEM, _SEM, _HBM, _HBM, pl.BlockSpec(memory_space=pltpu.VMEM)),
        input_output_aliases={0: 2, 1: 3},
        compiler_params=pltpu.CompilerParams(has_side_effects=_EFFECT),
    )(pltpu.with_memory_space_constraint(src, pltpu.HBM),
      pltpu.with_memory_space_constraint(lax.empty(land_shape, src.dtype), pltpu.HBM), *deps)


def _split_wait(handles, after, peers_fn, scatter, own, name):
    send_sems, recv_sems, src_thru, land_thru, _ = handles
    blk = land_thru.shape[1:]
    after = list(after) if isinstance(after, (list, tuple)) else [after]

    def body(x_ref, land_ref, send_sems, recv_sems, *rest):
        stage = rest[len(after) + 2:]
        x, y, c = lax.axis_index("x"), lax.axis_index("y"), lax.axis_index("c")
        if own:
            my = 4 * x + 2 * y + c
            mine = _staged_copy(x_ref.at[my] if scatter else x_ref, land_ref.at[my], *stage)
        for k, pid, to in peers_fn(x, y, c):
            cp = pltpu.make_async_remote_copy(
                src_ref=x_ref.at[pid] if scatter else x_ref, dst_ref=land_ref.at[pid],
                send_sem=send_sems.at[k], recv_sem=recv_sems.at[k], device_id=to, device_id_type=MESH)
            cp.wait_send()
            cp.wait_recv()
        if own:
            mine.wait()

    return pl.pallas_call(
        body, name=name,
        out_shape=(pltpu.HBM(src_thru.shape, src_thru.dtype), pltpu.HBM(land_thru.shape, land_thru.dtype)),
        in_specs=(_HBM, _HBM, _SEM, _SEM) + (pl.BlockSpec(memory_space=pl.ANY),) * len(after),
        out_specs=(_HBM, _HBM),
        input_output_aliases={0: 0, 1: 1},
        scratch_shapes=[pltpu.VMEM(blk, land_thru.dtype), pltpu.SemaphoreType.DMA((2,))] if own else [],
        compiler_params=pltpu.CompilerParams(has_side_effects=_EFFECT, vmem_limit_bytes=VMEM_LIMIT),
    )(src_thru, land_thru, send_sems, recv_sems, *after)


def _staged_copy(src_ref, dst_ref, buf, sems):
    leg = pltpu.make_async_copy(src_ref, buf, sems.at[0])
    leg.start()
    leg.wait()
    leg = pltpu.make_async_copy(buf, dst_ref, sems.at[1])
    leg.start()
    return leg


def _multi_start(srcs, name, dep=None):
    n_src = len(srcs)
    lands = [(N_DEV,) + tuple(a.shape[1:] if sc else a.shape) for a, sc in srcs]
    deps = [] if dep is None else [dep]

    def body(*refs):
        ins, outs = refs[:2 * n_src], refs[2 * n_src + len(deps):]
        x, y, c = lax.axis_index("x"), lax.axis_index("y"), lax.axis_index("c")
        my = 4 * x + 2 * y + c
        for i, (_, scatter) in enumerate(srcs):
            x_ref, land_ref, send_sems, recv_sems = ins[2 * i], ins[2 * i + 1], outs[4 * i], outs[4 * i + 1]
            for k, pid, to in _peers_all(x, y, c):
                pltpu.make_async_remote_copy(
                    src_ref=x_ref.at[pid] if scatter else x_ref, dst_ref=land_ref.at[my],
                    send_sem=send_sems.at[k], recv_sem=recv_sems.at[k], device_id=to, device_id_type=MESH).start()
        outs[-1][...] = jnp.zeros_like(outs[-1])

    out_shape, out_specs, operands = [], [], []
    for (a, _), land in zip(srcs, lands):
        out_shape += [pltpu.SemaphoreType.DMA((N_DEV - 1,)), pltpu.SemaphoreType.DMA((N_DEV - 1,)),
                      pltpu.HBM(a.shape, a.dtype), pltpu.HBM(land, a.dtype)]
        out_specs += [_SEM, _SEM, _HBM, _HBM]
        operands += [pltpu.with_memory_space_constraint(a, pltpu.HBM),
                     pltpu.with_memory_space_constraint(lax.empty(land, a.dtype), pltpu.HBM)]
    res = pl.pallas_call(
        body, name=name,
        out_shape=tuple(out_shape) + (jax.ShapeDtypeStruct((8, LANES), F32),),
        in_specs=(_HBM,) * (2 * n_src) + (pl.BlockSpec(memory_space=pl.ANY),) * len(deps),
        out_specs=tuple(out_specs) + (pl.BlockSpec(memory_space=pltpu.VMEM),),
        input_output_aliases={2 * i + j: 4 * i + 2 + j for i in range(n_src) for j in range(2)},
        compiler_params=pltpu.CompilerParams(has_side_effects=_EFFECT),
    )(*operands, *deps)
    return [tuple(res[4 * i:4 * i + 4]) + (srcs[i][1],) for i in range(n_src)], res[-1]


def _multi_wait(started, after, name):
    n_src = len(started)
    after = list(after) if isinstance(after, (list, tuple)) else [after]

    def body(*refs):
        ins = refs[:4 * n_src]
        stage = refs[4 * n_src + len(after) + 2 * n_src:]
        x, y, c = lax.axis_index("x"), lax.axis_index("y"), lax.axis_index("c")
        my = 4 * x + 2 * y + c
        pending = []
        for i, h in enumerate(started):
            x_ref, land_ref, send_sems, recv_sems = ins[4 * i:4 * i + 4]
            scatter = h[4]
            pending.append(_staged_copy(x_ref.at[my] if scatter else x_ref, land_ref.at[my],
                                        stage[2 * i], stage[2 * i + 1]))
            for k, pid, to in _peers_all(x, y, c):
                cp = pltpu.make_async_remote_copy(
                    src_ref=x_ref.at[pid] if scatter else x_ref, dst_ref=land_ref.at[pid],
                    send_sem=send_sems.at[k], recv_sem=recv_sems.at[k], device_id=to, device_id_type=MESH)
                cp.wait_send()
                cp.wait_recv()
        for leg in pending:
            leg.wait()

    operands, out_shape, scratch = [], [], []
    for send, recv, src_thru, land_thru, _ in started:
        operands += [src_thru, land_thru, send, recv]
        out_shape += [pltpu.HBM(src_thru.shape, src_thru.dtype), pltpu.HBM(land_thru.shape, land_thru.dtype)]
        scratch += [pltpu.VMEM(land_thru.shape[1:], land_thru.dtype), pltpu.SemaphoreType.DMA((2,))]
    res = pl.pallas_call(
        body, name=name,
        out_shape=tuple(out_shape),
        in_specs=(_HBM, _HBM, _SEM, _SEM) * n_src + (pl.BlockSpec(memory_space=pl.ANY),) * len(after),
        out_specs=(_HBM,) * (2 * n_src),
        input_output_aliases={4 * i + j: 2 * i + j for i in range(n_src) for j in range(2)},
        scratch_shapes=scratch,
        compiler_params=pltpu.CompilerParams(has_side_effects=_EFFECT, vmem_limit_bytes=VMEM_LIMIT),
    )(*operands, *after)
    return [res[2 * i + 1] for i in range(n_src)]


def _gather_finish(block, land, name):
    def body(x_ref, land_ref, out_ref, token, send_sems, recv_sems, buf, local_sems):
        x, y, c = lax.axis_index("x"), lax.axis_index("y"), lax.axis_index("c")
        my, sib_id, sibling = 4 * x + 2 * y + c, 4 * x + 2 * y + 1 - c, (x, y, 1 - c)
        token[...] = jnp.zeros_like(token)

        def copy(k, slot, src=None):
            return pltpu.make_async_remote_copy(
                src_ref=land_ref.at[slot] if src is None else src, dst_ref=out_ref.at[slot],
                send_sem=send_sems.at[k], recv_sem=recv_sems.at[k], device_id=sibling, device_id_type=MESH)

        chips = _peers_same_core(x, y, c)
        sends = [copy(0, my, src=x_ref)] + [copy(1 + k, pid) for k, pid, _ in chips]
        for cp in sends:
            cp.start()
        mine = _staged_copy(x_ref, out_ref.at[my], buf, local_sems)
        copy(0, sib_id).wait_recv()
        for k, pid, _ in chips:
            copy(1 + k, pid + 1 - 2 * c).wait_recv()
        for cp in sends:
            cp.wait_send()
        mine.wait()

    return pl.pallas_call(
        body, name=name,
        out_shape=[jax.ShapeDtypeStruct(land.shape, land.dtype), jax.ShapeDtypeStruct((8, LANES), F32)],
        in_specs=[pl.BlockSpec(memory_space=pl.ANY), pl.BlockSpec(memory_space=pl.ANY)],
        out_specs=[pl.BlockSpec(memory_space=pl.ANY), pl.BlockSpec(memory_space=pltpu.VMEM)],
        input_output_aliases={1: 0},
        scratch_shapes=[pltpu.SemaphoreType.DMA((4,)), pltpu.SemaphoreType.DMA((4,)),
                        pltpu.VMEM(block.shape, block.dtype), pltpu.SemaphoreType.DMA((2,))],
        compiler_params=pltpu.CompilerParams(vmem_limit_bytes=VMEM_LIMIT),
    )(block, land)


def _forward_start(block, land, name):
    def body(x_ref, land_ref, send_sems, recv_sems, x_thru, land_thru, token):
        x, y, c = lax.axis_index("x"), lax.axis_index("y"), lax.axis_index("c")
        my, sibling = 4 * x + 2 * y + c, (x, y, 1 - c)
        slots = [(0, my, x_ref)] + [(1 + k, pid, land_ref.at[pid]) for k, pid, _ in _peers_same_core(x, y, c)]
        for k, slot, src in slots:
            pltpu.make_async_remote_copy(
                src_ref=src, dst_ref=land_ref.at[slot], send_sem=send_sems.at[k], recv_sem=recv_sems.at[k],
                device_id=sibling, device_id_type=MESH).start()
        token[...] = jnp.zeros_like(token)

    return pl.pallas_call(
        body, name=name,
        out_shape=(pltpu.SemaphoreType.DMA((4,)), pltpu.SemaphoreType.DMA((4,)),
                   pltpu.HBM(block.shape, block.dtype), pltpu.HBM(land.shape, land.dtype),
                   jax.ShapeDtypeStruct((8, LANES), F32)),
        in_specs=(_HBM, _HBM),
        out_specs=(_SEM, _SEM, _HBM, _HBM, pl.BlockSpec(memory_space=pltpu.VMEM)),
        input_output_aliases={0: 2, 1: 3},
        compiler_params=pltpu.CompilerParams(has_side_effects=_EFFECT),
    )(block, land)


def _forward_wait(handles, after, name):
    send_sems, recv_sems, block_thru, land_thru, _ = handles

    def body(x_ref, land_ref, send_sems, recv_sems, after_ref, x_dead, got_ref, buf, local_sems):
        x, y, c = lax.axis_index("x"), lax.axis_index("y"), lax.axis_index("c")
        my, sib_id, sibling = 4 * x + 2 * y + c, 4 * x + 2 * y + 1 - c, (x, y, 1 - c)
        mine = _staged_copy(x_ref, land_ref.at[my], buf, local_sems)
        slots = [(0, my, sib_id)] + [(1 + k, pid, pid + 1 - 2 * c) for k, pid, _ in _peers_same_core(x, y, c)]
        for k, sent, got in slots:
            cp = pltpu.make_async_remote_copy(
                src_ref=land_ref.at[sent], dst_ref=land_ref.at[got], send_sem=send_sems.at[k],
                recv_sem=recv_sems.at[k], device_id=sibling, device_id_type=MESH)
            cp.wait_send()
            cp.wait_recv()
        mine.wait()

    return pl.pallas_call(
        body, name=name,
        out_shape=(pltpu.HBM(block_thru.shape, block_thru.dtype), pltpu.HBM(land_thru.shape, land_thru.dtype)),
        in_specs=(_HBM, _HBM, _SEM, _SEM, pl.BlockSpec(memory_space=pl.ANY)),
        out_specs=(_HBM, _HBM),
        input_output_aliases={0: 0, 1: 1},
        scratch_shapes=[pltpu.VMEM(block_thru.shape, block_thru.dtype), pltpu.SemaphoreType.DMA((2,))],
        compiler_params=pltpu.CompilerParams(has_side_effects=_EFFECT, vmem_limit_bytes=VMEM_LIMIT),
    )(block_thru, land_thru, send_sems, recv_sems, after)[1]


def _after(v, token):
    return v + token[0, 0].astype(v.dtype)


def _dsilu(s, silu):
    return s + silu * (1.0 - s)


def _log1p(x):
    u = 1.0 + x
    d = u - 1.0
    return jnp.where(d == 0.0, x, jnp.log(u) * (x / jnp.where(d == 0.0, 1.0, d)))


def _softplus_neg(lam):
    return jnp.maximum(-lam, 0.0) + _log1p(jnp.exp(-jnp.abs(lam)))


def _neg_expm1(y, exp_y):
    poly = -y * (1.0 + y * (0.5 + y * (1.0 / 6.0 + y * (1.0 / 24.0))))
    return jnp.where(y > -0.05, poly, 1.0 - exp_y)


def _sigmoid(x):
    return 0.5 * jnp.tanh(0.5 * x) + 0.5


def _shift_dn(cur, prev, k):
    ext = jnp.concatenate([prev, cur], axis=0)
    return pltpu.roll(ext, k, 0)[HALO:, :]


def _shift_up(cur, nxt, k):
    n = cur.shape[0]
    ext = jnp.concatenate([cur, nxt], axis=0)
    return pltpu.roll(ext, n + HALO - k, 0)[:n, :]


def _scan_fwd(a, b, h_prev):
    groups = a.shape[0] // SUBLANES
    a3 = a.reshape(groups, SUBLANES, LANES)
    b3 = b.reshape(groups, SUBLANES, LANES)
    row = lax.broadcasted_iota(jnp.int32, a3.shape, 1)
    k = 1
    while k < SUBLANES:
        a_sh = jnp.where(row >= k, pltpu.roll(a3, k, 1), 1.0)
        b_sh = jnp.where(row >= k, pltpu.roll(b3, k, 1), 0.0)
        b3 = a3 * b_sh + b3
        a3 = a3 * a_sh
        k *= 2
    carry = h_prev[HALO - 1:HALO, :]
    out = []
    for i in range(groups):
        hg = b3[i] + a3[i] * carry
        out.append(hg)
        carry = hg[SUBLANES - 1:SUBLANES, :]
    return jnp.concatenate(out, axis=0)


def _scan_rev(a_next, g, lam_next):
    groups = g.shape[0] // SUBLANES
    a3 = a_next.reshape(groups, SUBLANES, LANES)
    g3 = g.reshape(groups, SUBLANES, LANES)
    row = lax.broadcasted_iota(jnp.int32, a3.shape, 1)
    k = 1
    while k < SUBLANES:
        ok = row < SUBLANES - k
        a_sh = jnp.where(ok, pltpu.roll(a3, SUBLANES - k, 1), 1.0)
        g_sh = jnp.where(ok, pltpu.roll(g3, SUBLANES - k, 1), 0.0)
        g3 = g3 + a3 * g_sh
        a3 = a3 * a_sh
        k *= 2
    carry = lam_next[0:1, :]
    out = [None] * groups
    for i in reversed(range(groups)):
        lg = g3[i] + a3[i] * carry
        out[i] = lg
        carry = lg[0:1, :]
    return jnp.concatenate(out, axis=0)


def _rowsum(v):
    return jnp.sum(v, axis=0, keepdims=True)


def _dot(a, b):
    return jnp.dot(a, b, preferred_element_type=F32)


def _dot_nt(a, b):
    return lax.dot_general(a, b, (((1,), (1,)), ((), ())), preferred_element_type=F32)


def _dot_tn(a, b):
    return lax.dot_general(a, b, (((0,), (0,)), ((), ())), preferred_element_type=F32)


class _MixerWeights:
    def __init__(self, caw_ref, sw_ref, sb_ref, lcw_ref, lcb_ref, wa_ref, wx_ref, ba_ref, bx_ref, lam_ref):
        self.caw = [caw_ref[j:j + 1, :] for j in range(3)]
        self.lcw = [lcw_ref[j:j + 1, :] for j in range(4)]
        self.lcb = lcb_ref[...]
        row = lax.broadcasted_iota(jnp.int32, (CHUNK, CHUNK), 0)
        col = lax.broadcasted_iota(jnp.int32, (CHUNK, CHUNK), 1)
        self.tril = col <= row
        self.sw = jnp.where(self.tril, sw_ref[...], 0.0).astype(BF16)
        self.sb = sb_ref[...]
        self.wa = wa_ref[...]
        self.wx = wx_ref[...]
        self.ba = ba_ref[...]
        self.bx = bx_ref[...]
        lam = lam_ref[...]
        self.neg_c_sp = -LRU_C * _softplus_neg(lam)
        self.dsp_dlam = -_sigmoid(-lam)


def _mixer_a(ld, ldp, w, cv=None):
    t = {}
    a_x, a_c = ld(AX), ld(AC)
    t["a_x"], t["a_c"], t["a_b"], t["a_z"] = a_x, a_c, ld(AB), ld(AZ)
    t["ca"] = ca = a_c * a_x
    if cv is None:
        ca_p = ldp(AC) * ldp(AX)
        cv = w.caw[2] * ca + w.caw[1] * _shift_dn(ca, ca_p, 1) + w.caw[0] * _shift_dn(ca, ca_p, 2)
    t["cv"] = cv
    t["sa"] = _sigmoid(t["a_z"])
    t["silu_az"] = t["a_z"] * t["sa"]
    t["y_a"] = t["silu_az"] * t["a_b"] * t["cv"]
    return t


def _mixer_b(ld, w):
    t = {}
    v = ld(SV)
    vc = v - jnp.mean(v, axis=1, keepdims=True)
    t["rstd"] = lax.rsqrt(jnp.mean(vc * vc, axis=1, keepdims=True) + EPS)
    t["vn"] = vc * t["rstd"]
    t["z"] = _dot(w.sw, t["vn"].astype(BF16)) + w.sb
    t["s_u"], t["s_z"] = ld(SU), ld(SZ)
    t["ss"] = _sigmoid(t["s_z"])
    t["silu_sz"] = t["s_z"] * t["ss"]
    t["y_s"] = t["silu_sz"] * t["s_u"] * t["z"]
    return t


def _mixer_c(ld, ldp, w, backward, xc=None):
    t = {}
    t["r_x"] = r_x = ld(RX)
    if xc is None:
        r_xp = ldp(RX)
        xc = (w.lcb + w.lcw[0] * _shift_dn(r_x, r_xp, 3) + w.lcw[1] * _shift_dn(r_x, r_xp, 2)
              + w.lcw[2] * _shift_dn(r_x, r_xp, 1) + w.lcw[3] * r_x)
    t["xc"] = xc
    xcb = xc.astype(BF16)
    t["r"] = _sigmoid(_dot(xcb, w.wa) + w.ba)
    t["i"] = _sigmoid(_dot(xcb, w.wx) + w.bx)
    la = t["r"] * w.neg_c_sp
    t["a"] = jnp.exp(la)
    t["a2"] = t["a"] * t["a"]
    t["em"] = _neg_expm1(2.0 * la, t["a2"])
    if backward:
        t["inv_mult"] = lax.rsqrt(t["em"])
        t["mult"] = t["em"] * t["inv_mult"]
    else:
        t["mult"] = jnp.sqrt(t["em"])
    t["b"] = t["mult"] * (t["i"] * xc)
    t["r_z"] = ld(RZ)
    t["sr"] = _sigmoid(t["r_z"])
    t["silu_rz"] = t["r_z"] * t["sr"]
    return t


def _mixer_pre_scan(ld, ldp, w, backward, cv=None, xc=None):
    t = {**_mixer_a(ld, ldp, w, cv), **_mixer_b(ld, w), **_mixer_c(ld, ldp, w, backward, xc)}
    t["ga"], t["gs"], t["gr"] = _sigmoid(ld(GA)), _sigmoid(ld(GS)), _sigmoid(ld(GR))
    return t


def _weight_specs(n_cb_axis):
    def at(fn):
        return lambda *g: fn(g[n_cb_axis])
    return [
        pl.BlockSpec((3, LANES), at(lambda cb: (0, cb))),
        pl.BlockSpec((None, CHUNK, CHUNK), at(lambda cb: (cb, 0, 0))),
        pl.BlockSpec((None, CHUNK, LANES), at(lambda cb: (cb, 0, 0))),
        pl.BlockSpec((4, LANES), at(lambda cb: (0, cb))),
        pl.BlockSpec((1, LANES), at(lambda cb: (0, cb))),
        pl.BlockSpec((None, LANES, LANES), at(lambda cb: (cb, 0, 0))),
        pl.BlockSpec((None, LANES, LANES), at(lambda cb: (cb, 0, 0))),
        pl.BlockSpec((1, LANES), at(lambda cb: (0, cb))),
        pl.BlockSpec((1, LANES), at(lambda cb: (0, cb))),
        pl.BlockSpec((1, LANES), at(lambda cb: (0, cb))),
    ]


def _chunk_loaders(p_ref, c):
    r0 = pl.multiple_of(c * CHUNK, CHUNK)
    rp = pl.multiple_of(jnp.maximum(c * CHUNK - HALO, 0), HALO)

    def ld(j):
        return p_ref[j, pl.ds(r0, CHUNK), :].astype(F32)

    def ldp(j):
        return jnp.where(c > 0, p_ref[j, pl.ds(rp, HALO), :].astype(F32), 0.0)

    return r0, rp, ld, ldp


def _mixer_fwd(proj, mw):
    _, nb, s, _ = proj.shape
    n_chunks = s // CHUNK

    def body(p_ref, *refs):
        w = _MixerWeights(*refs[:10])
        merged_ref, hs_ref, cv_ref, xc_ref = refs[10:]

        def chunk(c, h_prev):
            r0, _, ld, ldp = _chunk_loaders(p_ref, c)
            t = _mixer_pre_scan(ld, ldp, w, False)
            h = _scan_fwd(t["a"], t["b"], h_prev)
            y_r = t["silu_rz"] * h
            merged = t["ga"] * t["y_a"] + t["gs"] * t["y_s"] + t["gr"] * y_r
            merged_ref[pl.ds(r0, CHUNK), :] = merged.astype(BF16)
            hs_ref[pl.ds(r0, CHUNK), :] = h
            cv_ref[pl.ds(r0, CHUNK), :] = t["cv"].astype(BF16)
            xc_ref[pl.ds(r0, CHUNK), :] = t["xc"].astype(BF16)
            return h[CHUNK - HALO:, :]

        def group(i, carry):
            for k in range(FWD_CHUNKS_PER_TRIP):
                carry = chunk(FWD_CHUNKS_PER_TRIP * i + k, carry)
            return carry

        assert n_chunks % FWD_CHUNKS_PER_TRIP == 0
        lax.fori_loop(0, n_chunks // FWD_CHUNKS_PER_TRIP, group, jnp.zeros((HALO, LANES), F32))

    slab = pl.BlockSpec((None, s, LANES), lambda cb, b: (b, 0, cb))
    half = jax.ShapeDtypeStruct((nb, s, D), BF16)
    return pl.pallas_call(
        body, name="mixer_fwd", grid=(D // LANES, nb),
        in_specs=[pl.BlockSpec((N_SEG, None, s, LANES), lambda cb, b: (0, b, 0, cb))] + _weight_specs(0),
        out_specs=[slab, slab, slab, slab],
        out_shape=[half, jax.ShapeDtypeStruct((nb, s, D), F32), half, half],
        compiler_params=_params(("arbitrary", "arbitrary")),
    )(proj, *mw)


def _mixer_bwd(proj, dmerged, hs, cv, xc, mw):
    _, nb, s, _ = proj.shape
    n_chunks = s // CHUNK

    def body(p_ref, dm_ref, hs_ref, cv_ref, xc_ref, *refs):
        w = _MixerWeights(*refs[:10])
        dp_ref, g_caw, g_sw, g_sb, g_lcw, g_vec, g_wa, g_wx = refs[10:]

        @pl.when(pl.program_id(1) == 0)
        def _():
            for ref in (g_caw, g_sw, g_sb, g_lcw, g_vec, g_wa, g_wx):
                ref[...] = jnp.zeros_like(ref)

        def chunk(i, carry):
            dcv_n, dxc_n, lam_n, a_n = carry
            c = n_chunks - 1 - i
            r0, rp, ld, ldp = _chunk_loaders(p_ref, c)
            t = _mixer_pre_scan(ld, ldp, w, True, cv_ref[pl.ds(r0, CHUNK), :].astype(F32),
                                xc_ref[pl.ds(r0, CHUNK), :].astype(F32))
            h = hs_ref[pl.ds(r0, CHUNK), :]
            h_p = jnp.where(c > 0, hs_ref[pl.ds(rp, HALO), :], 0.0)
            h_prev = _shift_dn(h, h_p, 1)
            dm = dm_ref[pl.ds(r0, CHUNK), :].astype(F32)
            y_r = t["silu_rz"] * h

            def out(j, val):
                dp_ref[j, pl.ds(r0, CHUNK), :] = val.astype(BF16)

            ga, gs, gr = t["ga"], t["gs"], t["gr"]
            dy_a, dy_s, dy_r = dm * ga, dm * gs, dm * gr
            out(GA, (dy_a * t["y_a"]) * (1.0 - ga))
            out(GS, (dy_s * t["y_s"]) * (1.0 - gs))
            out(GR, (dy_r * y_r) * (1.0 - gr))

            dy_ab = dy_a * t["a_b"]
            out(AZ, dy_ab * (t["cv"] * _dsilu(t["sa"], t["silu_az"])))
            out(AB, (dy_a * t["cv"]) * t["silu_az"])
            dcv = dy_ab * t["silu_az"]
            dcv1, dcv2 = _shift_up(dcv, dcv_n, 1), _shift_up(dcv, dcv_n, 2)
            dca = w.caw[2] * dcv + w.caw[1] * dcv1 + w.caw[0] * dcv2
            out(AC, dca * t["a_x"])
            out(AX, dca * t["a_c"])
            g_caw[2:3, :] += _rowsum(dcv * t["ca"])
            g_caw[1:2, :] += _rowsum(dcv1 * t["ca"])
            g_caw[0:1, :] += _rowsum(dcv2 * t["ca"])

            dy_su = dy_s * t["s_u"]
            out(SZ, dy_su * (t["z"] * _dsilu(t["ss"], t["silu_sz"])))
            out(SU, (dy_s * t["z"]) * t["silu_sz"])
            dz = dy_su * t["silu_sz"]
            dzb = dz.astype(BF16)
            g_sb[...] += jnp.broadcast_to(jnp.sum(dz, axis=1, keepdims=True), (CHUNK, LANES))
            g_sw[...] += _dot_nt(dzb, t["vn"].astype(BF16))
            dvn = _dot_tn(w.sw, dzb)
            vn = t["vn"]
            out(SV, t["rstd"] * (dvn - jnp.mean(dvn, axis=1, keepdims=True)
                                 - vn * jnp.mean(dvn * vn, axis=1, keepdims=True)))

            out(RZ, (dy_r * h) * _dsilu(t["sr"], t["silu_rz"]))
            lam = _scan_rev(_shift_up(t["a"], a_n, 1), dy_r * t["silu_rz"], lam_n)
            a, r, ig, xc = t["a"], t["r"], t["i"], t["xc"]
            lam_mult, lam_i = lam * t["mult"], lam * ig
            d_i = lam_mult * xc
            d_mult = lam_i * xc
            dxc = lam_mult * ig
            dla = lam * h_prev * a - d_mult * (t["a2"] * t["inv_mult"])
            g_vec[3:4, :] += _rowsum(dla * r) * (-LRU_C * w.dsp_dlam)
            dpr = (dla * w.neg_c_sp) * r * (1.0 - r)
            dpi = d_i * ig * (1.0 - ig)
            dprb, dpib, xcb = dpr.astype(BF16), dpi.astype(BF16), xc.astype(BF16)
            g_wa[...] += _dot_tn(xcb, dprb)
            g_wx[...] += _dot_tn(xcb, dpib)
            g_vec[1:2, :] += _rowsum(dpr)
            g_vec[2:3, :] += _rowsum(dpi)
            dxc = dxc + _dot_nt(dprb, w.wa) + _dot_nt(dpib, w.wx)
            g_vec[0:1, :] += _rowsum(dxc)
            dxcs = [_shift_up(dxc, dxc_n, 3), _shift_up(dxc, dxc_n, 2), _shift_up(dxc, dxc_n, 1), dxc]
            out(RX, w.lcw[3] * dxcs[3] + w.lcw[2] * dxcs[2] + w.lcw[1] * dxcs[1] + w.lcw[0] * dxcs[0])
            for j in range(4):
                g_lcw[j:j + 1, :] += _rowsum(dxcs[j] * t["r_x"])
            return dcv[:HALO, :], dxc[:HALO, :], lam[:HALO, :], a[:HALO, :]

        zero = jnp.zeros((HALO, LANES), F32)
        def group(i, carry):
            for k in range(BWD_CHUNKS_PER_TRIP):
                carry = chunk(BWD_CHUNKS_PER_TRIP * i + k, carry)
            return carry

        assert n_chunks % BWD_CHUNKS_PER_TRIP == 0
        lax.fori_loop(0, n_chunks // BWD_CHUNKS_PER_TRIP, group, (zero, zero, zero, zero))

        @pl.when(pl.program_id(1) == nb - 1)
        def _():
            g_sw[...] = jnp.where(w.tril, g_sw[...], 0.0)

    slab = lambda dt: pl.BlockSpec((None, s, LANES), lambda cb, b: (b, 0, cb))
    seg = pl.BlockSpec((N_SEG, None, s, LANES), lambda cb, b: (0, b, 0, cb))
    rows = lambda n: pl.BlockSpec((n, LANES), lambda cb, b: (0, cb))
    sq = pl.BlockSpec((None, LANES, LANES), lambda cb, b: (cb, 0, 0))
    n_cb = D // LANES
    return pl.pallas_call(
        body, name="mixer_bwd", grid=(n_cb, nb),
        in_specs=[seg, slab(BF16), slab(F32), slab(BF16), slab(BF16)] + _weight_specs(0),
        out_specs=[seg, rows(3), sq, sq, rows(4), rows(8), sq, sq],
        out_shape=[
            jax.ShapeDtypeStruct(proj.shape, BF16),
            jax.ShapeDtypeStruct((3, D), F32),
            jax.ShapeDtypeStruct((n_cb, CHUNK, CHUNK), F32),
            jax.ShapeDtypeStruct((n_cb, CHUNK, LANES), F32),
            jax.ShapeDtypeStruct((4, D), F32),
            jax.ShapeDtypeStruct((8, D), F32),
            jax.ShapeDtypeStruct((n_cb, LANES, LANES), F32),
            jax.ShapeDtypeStruct((n_cb, LANES, LANES), F32),
        ],
        compiler_params=_params(("arbitrary", "arbitrary")),
    )(proj, dmerged, hs, cv, xc, *mw)


def _row_tile(s, want):
    return want if s % want == 0 else s


def _norm_mod(x, gain, shift, scale):
    nb, s, _ = x.shape
    tm = _row_tile(s, 512)

    def body(x_ref, g_ref, sh_ref, sc_ref, h_ref, ht_ref):
        xv = x_ref[...]
        r = lax.rsqrt(jnp.mean(xv * xv, axis=1, keepdims=True) + EPS)
        h = ((xv * r) * g_ref[...] * (1.0 + sc_ref[...]) + sh_ref[...]).astype(BF16)
        h_ref[...] = h
        ht_ref[...] = h.T

    tile = pl.BlockSpec((None, tm, D), lambda b, m: (b, m, 0))
    vec = pl.BlockSpec((None, 1, D), lambda b, m: (b, 0, 0))
    return pl.pallas_call(
        body, name="norm_mod", grid=(nb, s // tm),
        in_specs=[tile, pl.BlockSpec((1, D), lambda b, m: (0, 0)), vec, vec],
        out_specs=[tile, pl.BlockSpec((None, D, tm), lambda b, m: (b, 0, m))],
        out_shape=[jax.ShapeDtypeStruct(x.shape, BF16), jax.ShapeDtypeStruct((nb, D, s), BF16)],
        compiler_params=_params(("arbitrary", "arbitrary")),
    )(x, gain, shift, scale)


def _in_proj(h, wg, dep):
    nb, s, _ = h.shape

    def body(h_ref, w0_ref, w1_ref, dep_ref, o_ref):
        hv = h_ref[...]
        o_ref[:, :UNIT] = _dot(hv, w0_ref[...]).astype(BF16)
        o_ref[:, UNIT:] = _dot(hv, w1_ref[...]).astype(BF16)

    def unit(k):
        return pl.BlockSpec((None, D, UNIT),
                            lambda b, j: ((2 * j + k) // UNITS_PER_DEV, 0, (2 * j + k) % UNITS_PER_DEV))

    return pl.pallas_call(
        body, name="in_proj", grid=(nb, N_SEG),
        in_specs=[pl.BlockSpec((None, s, D), lambda b, j: (b, 0, 0)), unit(0), unit(1),
                  pl.BlockSpec((8, LANES), lambda b, j: (0, 0))],
        out_specs=pl.BlockSpec((None, None, s, D), lambda b, j: (j, b, 0, 0)),
        out_shape=jax.ShapeDtypeStruct((N_SEG, nb, s, D), BF16),
        compiler_params=_params(("arbitrary", "arbitrary")),
    )(h, wg, wg, dep)


def _out_proj(x, merged, wout, gate):
    nb, s, _ = x.shape
    tm = _row_tile(s, 512)

    def body(x_ref, m_ref, w_ref, g_ref, o_ref):
        o_ref[...] = x_ref[...] + g_ref[...] * _dot(m_ref[...], w_ref[...])

    tile = pl.BlockSpec((None, tm, D), lambda b, m: (b, m, 0))
    return pl.pallas_call(
        body, name="out_proj", grid=(nb, s // tm),
        in_specs=[tile, tile, pl.BlockSpec((D, D), lambda b, m: (0, 0)),
                  pl.BlockSpec((None, 1, D), lambda b, m: (b, 0, 0))],
        out_specs=tile, out_shape=jax.ShapeDtypeStruct(x.shape, F32),
        compiler_params=_params(("arbitrary", "arbitrary")),
    )(x, merged, wout, gate)


def _out_proj_norm(x, merged, wout, gate, gain, shift, scale):
    nb, s, _ = x.shape
    tm = _row_tile(s, 512)

    def body(x_ref, m_ref, w_ref, g_ref, gn_ref, sh_ref, sc_ref, o_ref, h_ref, ht_ref):
        xv = x_ref[...] + g_ref[...] * _dot(m_ref[...], w_ref[...])
        o_ref[...] = xv
        r = lax.rsqrt(jnp.mean(xv * xv, axis=1, keepdims=True) + EPS)
        h = ((xv * r) * gn_ref[...] * (1.0 + sc_ref[...]) + sh_ref[...]).astype(BF16)
        h_ref[...] = h
        ht_ref[...] = h.T

    tile = pl.BlockSpec((None, tm, D), lambda b, m: (b, m, 0))
    vec = pl.BlockSpec((None, 1, D), lambda b, m: (b, 0, 0))
    return pl.pallas_call(
        body, name="out_proj_norm", grid=(nb, s // tm),
        in_specs=[tile, tile, pl.BlockSpec((D, D), lambda b, m: (0, 0)), vec,
                  pl.BlockSpec((1, D), lambda b, m: (0, 0)), vec, vec],
        out_specs=[tile, tile, pl.BlockSpec((None, D, tm), lambda b, m: (b, 0, m))],
        out_shape=[jax.ShapeDtypeStruct(x.shape, F32), jax.ShapeDtypeStruct(x.shape, BF16),
                   jax.ShapeDtypeStruct((nb, D, s), BF16)],
        compiler_params=_params(("arbitrary", "arbitrary")),
    )(x, merged, wout, gate, gain, shift, scale)


def _loss_head(x, gain, target):
    nb, s, _ = x.shape
    tm = _row_tile(s, 512)

    def body(x_ref, g_ref, t_ref, loss_ref, dx_ref, dg_ref):
        first = (pl.program_id(0) == 0) & (pl.program_id(1) == 0)
        last = (pl.program_id(0) == nb - 1) & (pl.program_id(1) == s // tm - 1)

        @pl.when(first)
        def _():
            loss_ref[...] = jnp.zeros_like(loss_ref)
            dg_ref[...] = jnp.zeros_like(dg_ref)

        xv = x_ref[...]
        r = lax.rsqrt(jnp.mean(xv * xv, axis=1, keepdims=True) + EPS)
        xn = xv * r
        g = g_ref[...]
        e = xn * g - t_ref[...]
        loss_ref[...] += _rowsum(e * e) * (0.5 / D)
        dy = e * (1.0 / D)
        dg_ref[...] += _rowsum(dy * xn)
        dxn = dy * g
        dx_ref[...] = r * (dxn - xn * jnp.mean(dxn * xn, axis=1, keepdims=True))

        @pl.when(last)
        def _():
            loss_ref[...] = jnp.broadcast_to(jnp.sum(loss_ref[...], axis=1, keepdims=True), (1, D))

    tile = pl.BlockSpec((None, tm, D), lambda b, m: (b, m, 0))
    vec = pl.BlockSpec((1, D), lambda b, m: (0, 0))
    return pl.pallas_call(
        body, name="loss_head", grid=(nb, s // tm),
        in_specs=[tile, vec, tile], out_specs=[vec, tile, vec],
        out_shape=[jax.ShapeDtypeStruct((1, D), F32), jax.ShapeDtypeStruct(x.shape, F32),
                   jax.ShapeDtypeStruct((1, D), F32)],
        compiler_params=_params(("arbitrary", "arbitrary")),
    )(x, gain, target)


def _out_proj_bwd(dxo, merged, wout, gate):
    nb, s, _ = dxo.shape
    tm = _row_tile(s, 512)

    def body(d_ref, m_ref, w_ref, g_ref, dm_ref, gw_ref, dg_ref):
        @pl.when((pl.program_id(0) == 0) & (pl.program_id(1) == 0))
        def _():
            gw_ref[...] = jnp.zeros_like(gw_ref)

        @pl.when(pl.program_id(1) == 0)
        def _():
            dg_ref[...] = jnp.zeros_like(dg_ref)

        d = d_ref[...]
        m = m_ref[...]
        wv = w_ref[...]
        dg_ref[...] += _rowsum(d * _dot(m, wv))
        dout = (d * g_ref[...]).astype(BF16)
        dm_ref[...] = _dot_nt(dout, wv).astype(BF16)
        gw_ref[...] += _dot_tn(m, dout)

    tile = pl.BlockSpec((None, tm, D), lambda b, m: (b, m, 0))
    vec = pl.BlockSpec((None, 1, D), lambda b, m: (b, 0, 0))
    full = pl.BlockSpec((D, D), lambda b, m: (0, 0))
    return pl.pallas_call(
        body, name="out_proj_bwd", grid=(nb, s // tm),
        in_specs=[tile, tile, full, vec], out_specs=[tile, full, vec],
        out_shape=[jax.ShapeDtypeStruct(dxo.shape, BF16), jax.ShapeDtypeStruct((D, D), F32),
                   jax.ShapeDtypeStruct((nb, 1, D), F32)],
        compiler_params=_params(("arbitrary", "arbitrary")),
    )(dxo, merged, wout, gate)


def _in_proj_bwd_h(dproj, wg, dep):
    _, nb, s, _ = dproj.shape
    tm = _row_tile(s, 1024)

    def body(dp0_ref, dp1_ref, w0_ref, w1_ref, w2_ref, w3_ref, dep_ref, dh_ref):
        j = pl.program_id(2)
        part = (_dot_nt(dp0_ref[...], jnp.concatenate([w0_ref[...], w1_ref[...]], axis=1))
                + _dot_nt(dp1_ref[...], jnp.concatenate([w2_ref[...], w3_ref[...]], axis=1)))

        @pl.when(j == 0)
        def _():
            dh_ref[...] = part

        @pl.when(j > 0)
        def _():
            dh_ref[...] += part

    def seg(k):
        return pl.BlockSpec((None, None, tm, D), lambda b, m, j: (2 * j + k, b, m, 0))

    def unit(k):
        return pl.BlockSpec((None, D, UNIT),
                            lambda b, m, j: ((4 * j + k) // UNITS_PER_DEV, 0, (4 * j + k) % UNITS_PER_DEV))

    return pl.pallas_call(
        body, name="in_proj_bwd_h", grid=(nb, s // tm, N_SEG // 2),
        in_specs=[seg(0), seg(1), unit(0), unit(1), unit(2), unit(3),
                  pl.BlockSpec((8, LANES), lambda b, m, j: (0, 0))],
        out_specs=pl.BlockSpec((None, tm, D), lambda b, m, j: (b, m, 0)),
        out_shape=jax.ShapeDtypeStruct((nb, s, D), F32),
        compiler_params=_params(("arbitrary", "arbitrary", "arbitrary")),
    )(dproj, dproj, wg, wg, wg, wg, dep)


def _norm_mod_bwd(dh, x, dxo, gain, scale):
    nb, s, _ = x.shape
    tm = _row_tile(s, 512)

    def body(dh_ref, x_ref, dxo_ref, g_ref, sc_ref, dx_ref, dsh_ref, dsc_ref, dg_ref):
        b, m = pl.program_id(0), pl.program_id(1)

        @pl.when((b == 0) & (m == 0))
        def _():
            dg_ref[...] = jnp.zeros_like(dg_ref)

        @pl.when(m == 0)
        def _():
            dsh_ref[...] = jnp.zeros_like(dsh_ref)
            dsc_ref[...] = jnp.zeros_like(dsc_ref)

        dh = dh_ref[...]
        xv = x_ref[...]
        r = lax.rsqrt(jnp.mean(xv * xv, axis=1, keepdims=True) + EPS)
        xn = xv * r
        g = g_ref[...]
        one_sc = 1.0 + sc_ref[...]
        dsh_ref[...] += _rowsum(dh)
        dsc_ref[...] += _rowsum(dh * (xn * g))
        dg_ref[...] += _rowsum(dh * one_sc * xn)
        dxn = dh * (g * one_sc)
        dx_ref[...] = dxo_ref[...] + r * (dxn - xn * jnp.mean(dxn * xn, axis=1, keepdims=True))

    tile = pl.BlockSpec((None, tm, D), lambda b, m: (b, m, 0))
    vec = pl.BlockSpec((None, 1, D), lambda b, m: (b, 0, 0))
    one = pl.BlockSpec((1, D), lambda b, m: (0, 0))
    return pl.pallas_call(
        body, name="norm_mod_bwd", grid=(nb, s // tm),
        in_specs=[tile, tile, tile, one, vec],
        out_specs=[tile, vec, vec, one],
        out_shape=[jax.ShapeDtypeStruct(x.shape, F32), jax.ShapeDtypeStruct((nb, 1, D), F32),
                   jax.ShapeDtypeStruct((nb, 1, D), F32), jax.ShapeDtypeStruct((1, D), F32)],
        compiler_params=_params(("arbitrary", "arbitrary")),
    )(dh, x, dxo, gain, scale)


def _in_proj_bwd_w(ht, dproj, dep):
    nb, _, s = ht.shape
    tm = _row_tile(s, 2048)
    n_m = s // tm

    def body(ht_ref, dp_ref, dep_ref, o_ref, acc_ref):
        b, m = pl.program_id(1), pl.program_id(2)

        @pl.when((b == 0) & (m == 0))
        def _():
            acc_ref[...] = jnp.zeros_like(acc_ref)

        acc_ref[...] += _dot(ht_ref[...], dp_ref[...])

        @pl.when((b == nb - 1) & (m == n_m - 1))
        def _():
            o_ref[0] = acc_ref[:, :UNIT].astype(BF16)
            o_ref[1] = acc_ref[:, UNIT:].astype(BF16)

    return pl.pallas_call(
        body, name="in_proj_bwd_w", grid=(N_SEG, nb, n_m),
        in_specs=[pl.BlockSpec((None, D, tm), lambda j, b, m: (b, 0, m)),
                  pl.BlockSpec((None, None, tm, D), lambda j, b, m: (j, b, m, 0)),
                  pl.BlockSpec((8, LANES), lambda j, b, m: (0, 0))],
        out_specs=pl.BlockSpec((2, D, UNIT), lambda j, b, m: (j, 0, 0)),
        out_shape=jax.ShapeDtypeStruct((2 * N_SEG, D, UNIT), BF16),
        scratch_shapes=[pltpu.VMEM((D, D), F32)],
        compiler_params=_params(("arbitrary", "arbitrary", "arbitrary")),
    )(ht, dproj, dep)


def _mod_proj(c_all, w_mod, b_mod_mine):
    nl, _, ncol = w_mod.shape
    nbg = c_all.shape[0]

    def body(c_ref, w_ref, b_ref, o_ref):
        cv = c_ref[...]
        o_ref[...] = jnp.dot(cv * jax.nn.sigmoid(cv), w_ref[...], preferred_element_type=F32,
                             precision=lax.Precision.HIGHEST) + b_ref[...]

    return pl.pallas_call(
        body, name="mod_proj", grid=(nl,),
        in_specs=[pl.BlockSpec((nbg, D), lambda l: (0, 0)), pl.BlockSpec((None, D, ncol), lambda l: (l, 0, 0)),
                  pl.BlockSpec((None, 1, ncol), lambda l: (l, 0, 0))],
        out_specs=pl.BlockSpec((None, nbg, ncol), lambda l: (l, 0, 0)),
        out_shape=jax.ShapeDtypeStruct((nl, nbg, ncol), F32),
        compiler_params=_params(("arbitrary",)),
    )(c_all, w_mod, b_mod_mine)


def _mod_grad(c_all, dmod_all, dmod_mine):
    nl, nbg, ncol = dmod_mine.shape

    def body(c_ref, da_ref, dm_ref, gw_ref, gb_ref):
        cv = c_ref[...]
        gw_ref[...] = lax.dot_general(cv * jax.nn.sigmoid(cv), dm_ref[...], (((0,), (0,)), ((), ())),
                                      preferred_element_type=F32, precision=lax.Precision.HIGHEST)
        gb_ref[...] = _rowsum(da_ref[...])

    return pl.pallas_call(
        body, name="mod_grad", grid=(nl,),
        in_specs=[pl.BlockSpec((nbg, D), lambda l: (0, 0)), pl.BlockSpec((None, nbg, 3 * D), lambda l: (l, 0, 0)),
                  pl.BlockSpec((None, nbg, ncol), lambda l: (l, 0, 0))],
        out_specs=[pl.BlockSpec((None, D, ncol), lambda l: (l, 0, 0)),
                   pl.BlockSpec((None, 1, 3 * D), lambda l: (l, 0, 0))],
        out_shape=[jax.ShapeDtypeStruct((nl, D, ncol), F32), jax.ShapeDtypeStruct((nl, 1, 3 * D), F32)],
        compiler_params=_params(("arbitrary",)),
    )(c_all, dmod_all, dmod_mine)


def _adamw(parts, w, m, v, name, layer=None, prev=None):
    n_parts, n_u, n_r, cu = parts.shape
    assert w.shape[-2:] == (n_r, n_u * cu), (parts.shape, w.shape)
    tr = n_r
    for cand in (512, 256, 128):
        if n_r > cand and n_r % cand == 0:
            tr = cand
            break
    n_prev = 0 if prev is None else 4

    def body(p_ref, w_ref, m_ref, v_ref, *rest):
        g_ref, d_ref, nm_ref, nv_ref = rest[n_prev:]
        g = p_ref[0].astype(F32)
        for k in range(1, n_parts):
            g = g + p_ref[k].astype(F32)
        m2 = ADAM_B1 * m_ref[...] + (1.0 - ADAM_B1) * g
        v2 = ADAM_B2 * v_ref[...] + (1.0 - ADAM_B2) * (g * g)
        m_hat = m2 / (1.0 - ADAM_B1 ** ADAM_STEP)
        v_hat = v2 / (1.0 - ADAM_B2 ** ADAM_STEP)
        g_ref[...] = g
        d_ref[...] = -ADAM_LR * (m_hat / (jnp.sqrt(v_hat) + ADAM_EPS) + ADAM_WD * w_ref[...])
        nm_ref[...] = m2
        nv_ref[...] = v2

    if layer is None:
        tile = pl.BlockSpec((tr, cu), lambda u, i: (i, u))
    else:
        tile = pl.BlockSpec((None, tr, cu), lambda u, i: (layer, i, u))
    shp = jax.ShapeDtypeStruct(w.shape, F32)
    return pl.pallas_call(
        body, name=name, grid=(n_u, n_r // tr),
        in_specs=[pl.BlockSpec((n_parts, None, tr, cu), lambda u, i: (0, u, i, 0)), tile, tile, tile]
        + [pl.BlockSpec(memory_space=pl.ANY)] * n_prev,
        out_specs=[tile, tile, tile, tile], out_shape=[shp, shp, shp, shp],
        input_output_aliases={4 + k: k for k in range(n_prev)},
        compiler_params=_params(("arbitrary", "arbitrary")),
    )(parts, w, m, v, *(prev or ()))


def _gathered_cols(g, inner):
    k = len(inner)
    perm = tuple(range(1, k + 1)) + (0, k + 1)
    t = jnp.transpose(g, perm)
    return t.reshape(tuple(inner) + (g.shape[0] * g.shape[-1],))


def _pair_blocks(wh):
    z = jnp.zeros((8, 64, 64), wh.dtype)
    w2 = wh.reshape(8, 2, 64, 64)
    top = jnp.concatenate([w2[:, 0], z], axis=2)
    bot = jnp.concatenate([z, w2[:, 1]], axis=2)
    return jnp.concatenate([top, bot], axis=1).astype(BF16)


def _unpair_blocks(g):
    return jnp.stack([g[:, :64, :64], g[:, 64:, 64:]], axis=1).reshape(16, 64, 64)


FLAT_ROWS = 512


def _pack_rows(arrays, lead=0):
    parts = [a.reshape(a.shape[:lead] + (-1, LANES)) for a in arrays]
    rows = jnp.concatenate(parts, axis=lead)
    pad = [(0, 0)] * rows.ndim
    pad[lead] = (0, (-rows.shape[lead]) % FLAT_ROWS)
    return jnp.pad(rows, pad)


def kernel(x, c, norm_gain, w_mod, b_mod, w_in, w_out, conv_a_w, sgu_w, sgu_b, lru_conv_w, lru_conv_b, lru_wa, lru_ba, lru_wx, lru_bx, lru_lambda, final_gain, loss_target, m_norm_gain, m_w_mod, m_b_mod, m_w_in, m_w_out, m_conv_a_w, m_sgu_w, m_sgu_b, m_lru_conv_w, m_lru_conv_b, m_lru_wa, m_lru_ba, m_lru_wx, m_lru_bx, m_lru_lambda, m_final_gain, v_norm_gain, v_w_mod, v_b_mod, v_w_in, v_w_out, v_conv_a_w, v_sgu_w, v_sgu_b, v_lru_conv_w, v_lru_conv_b, v_lru_wa, v_lru_ba, v_lru_wx, v_lru_bx, v_lru_lambda, v_final_gain):
    nl = w_in.shape[0]
    nb, s, _ = x.shape
    me = _my_index()
    mod_cols = w_mod.shape[2]


    small = jnp.concatenate([c.reshape(-1, LANES), conv_a_w.reshape(-1, LANES), lru_conv_w.reshape(-1, LANES)])
    n_c, n_ca = nb * D // LANES, nl * 3
    n_small = small.shape[0]
    small = jnp.pad(small, ((0, (-n_small) % 8), (0, 0)))
    small_all, _ = _all_gather(small, "gather_small")
    c_all = small_all[:, :n_c].reshape(N_DEV * nb, D)

    w_in_b = [w_in[0].astype(BF16)] + list(w_in[1:].astype(BF16))

    def start_w_in(l, dep):
        return _split_start(w_in_b[l], _peers_same_core, False, "gather_w_in_start", dep)

    conv_a_full = _gathered_cols(small_all[:, n_c:n_c + n_ca].reshape(N_DEV, nl, 3, LANES), (nl, 3))
    lru_conv_full = _gathered_cols(small_all[:, n_c + n_ca:n_small].reshape(N_DEV, nl, 4, LANES), (nl, 4))

    def gathered_w_in(started, after):
        block, land = _split_wait(started, after, _peers_same_core, False, False, "gather_w_in_wait")
        return _gather_finish(block, land, "gather_w_in_finish")

    sgu_b_lanes = jnp.broadcast_to(sgu_b[..., None], sgu_b.shape + (LANES,))
    mws = []
    for l in range(nl):
        mws.append((conv_a_full[l], sgu_w[l], sgu_b_lanes[l], lru_conv_full[l], lru_conv_b[l][None, :],
                    _pair_blocks(lru_wa[l]), _pair_blocks(lru_wx[l]), lru_ba[l].reshape(1, D),
                    lru_bx[l].reshape(1, D), lru_lambda[l][None, :]))

    rep_names = ["sgu_w", "sgu_b", "lru_conv_b", "lru_wa", "lru_ba", "lru_wx", "lru_bx", "lru_lambda"]
    rep_w = dict(sgu_w=sgu_w, sgu_b=sgu_b, lru_conv_b=lru_conv_b, lru_wa=lru_wa, lru_ba=lru_ba,
                 lru_wx=lru_wx, lru_bx=lru_bx, lru_lambda=lru_lambda)
    rep_m = dict(sgu_w=m_sgu_w, sgu_b=m_sgu_b, lru_conv_b=m_lru_conv_b, lru_wa=m_lru_wa,
                 lru_ba=m_lru_ba, lru_wx=m_lru_wx, lru_bx=m_lru_bx, lru_lambda=m_lru_lambda)
    rep_v = dict(sgu_w=v_sgu_w, sgu_b=v_sgu_b, lru_conv_b=v_lru_conv_b, lru_wa=v_lru_wa,
                 lru_ba=v_lru_ba, lru_wx=v_lru_wx, lru_bx=v_lru_bx, lru_lambda=v_lru_lambda)

    rep_w_all, rep_m_all, rep_v_all = [_pack_rows([src[n] for n in rep_names], lead=1)
                                       for src in (rep_w, rep_m, rep_v)]
    early = [rep_w_all, rep_m_all, rep_v_all] + w_in_b[1:] + [a for mw in mws for a in mw]

    b_mod_mine = lax.dynamic_slice_in_dim(b_mod, me * mod_cols, mod_cols, axis=1)[:, None, :]
    mod_mine = _mod_proj(c_all, w_mod, b_mod_mine)
    mod_all, mod_token = _all_gather(mod_mine.reshape(nl * N_DEV * nb, mod_cols), "gather_mod", dep=w_in_b[0])
    ici = {0: start_w_in(0, mod_token)}
    mod_full = _gathered_cols(mod_all.reshape(N_DEV, nl, N_DEV * nb, mod_cols), (nl, N_DEV * nb))
    mod_loc = lax.dynamic_slice_in_dim(mod_full, me * nb, nb, axis=1)
    shift, scale, gate = [mod_loc[:, :, j * D:(j + 1) * D][:, :, None, :] for j in range(3)]

    xs, hts, projs, mergeds, states, wg = [], [], [], [], [], []
    xl = x
    d2d = {}
    wo_started = _split_start(w_out.astype(BF16).reshape(nl * (D // N_DEV), D), _peers_all, False,
                              "gather_w_out_start", ici[0][4])
    wo = None
    h, ht = _norm_mod(xl, _after(norm_gain[0][None, :], wo_started[4]), shift[0], scale[0])
    for l in range(nl):
        if l == 0:
            wg_l, token = gathered_w_in(ici[0], [h] + early)
            ici[1] = start_w_in(1, token)
            dep = ici[1][4]
        else:
            wg_l = _forward_wait(d2d[l], h, "gather_w_in_d2d_wait")
            dep = d2d[l][4]
        wg.append(wg_l)
        proj = _in_proj(h, wg_l, dep)
        merged, *st = _mixer_fwd(proj, mws[l])
        xs.append(xl), hts.append(ht), projs.append(proj), mergeds.append(merged), states.append(st)
        gate_l = gate[l]
        if l + 1 < nl:
            block, land = _split_wait(ici[l + 1], merged, _peers_same_core, False, False, "gather_w_in_wait")
            d2d[l + 1] = _forward_start(block, land, "gather_w_in_d2d_start")
            gate_l = _after(gate_l, d2d[l + 1][4])
            if l + 2 < nl:
                ici[l + 2] = start_w_in(l + 2, d2d[l + 1][4])
                gate_l = _after(gate_l, ici[l + 2][4])
        if wo is None:
            _, wo_all = _split_wait(wo_started, merged, _peers_all, False, True, "gather_w_out_wait")
            wo = jnp.transpose(wo_all.reshape(N_DEV, nl, D // N_DEV, D), (1, 0, 2, 3)).reshape(nl, D, D)
        if l + 1 < nl:
            xl, h, ht = _out_proj_norm(xl, merged, wo[l], gate_l, norm_gain[l + 1][None, :], shift[l + 1], scale[l + 1])
        else:
            xl = _out_proj(xl, merged, wo[l], gate_l)

    loss_row, dx, g_final = _loss_head(xl, final_gain[None, :], loss_target)

    res_big, g_conv = {}, [None] * nl
    dmods = [None] * nl

    def finish_exchange(pending, after):
        l, h_in, h_out, h_rep = pending
        r_in, r_out, r_rep = _multi_wait([h_in, h_out, h_rep], after, "scatter_wait")
        res_big["w_in"] = _adamw(r_in, w_in, m_w_in, v_w_in, "adamw_w_in", l, res_big.get("w_in"))
        res_big["w_out"] = _adamw(r_out, w_out, m_w_out, v_w_out, "adamw_w_out", l, res_big.get("w_out"))
        res_big["rep"] = _adamw(r_rep[:, None], rep_w_all, rep_m_all, rep_v_all, "adamw_rep", l, res_big.get("rep"))

    pending = None
    g_gains = [None] * nl
    for l in reversed(range(nl)):
        dmerged, gw_out, dgate = _out_proj_bwd(dx, mergeds[l], wo[l], gate[l])
        dproj, g_caw, g_sw, g_sb, g_lcw, g_vec, g_wa, g_wx = _mixer_bwd(projs[l], dmerged, *states[l], mws[l])
        g_conv[l] = (g_caw, g_lcw)
        rep_g = dict(
            sgu_w=g_sw, sgu_b=g_sb[:, :, 0], lru_conv_b=g_vec[0],
            lru_wa=_unpair_blocks(g_wa), lru_ba=g_vec[1].reshape(16, 64), lru_wx=_unpair_blocks(g_wx),
            lru_bx=g_vec[2].reshape(16, 64), lru_lambda=g_vec[3])
        rep_block = _pack_rows([rep_g[n] for n in rep_names])
        (h_out, h_rep), token = _multi_start([(gw_out.reshape(N_DEV, 1, D // N_DEV, D), True), (rep_block, False)],
                                             "scatter_small_start")
        gw_in = _in_proj_bwd_w(hts[l], dproj, token)
        (h_in,), token = _multi_start([(gw_in.reshape(N_DEV, UNITS_PER_DEV, D, UNIT), True)], "scatter_w_in_start")
        started = (l, h_in, h_out, h_rep)
        dh = _in_proj_bwd_h(dproj, wg[l], token)
        dx, dshift, dscale, g_gain = _norm_mod_bwd(dh, xs[l], dx, norm_gain[l][None, :], scale[l])
        g_gains[l] = g_gain
        dmods[l] = jnp.concatenate([dshift, dscale, dgate], axis=2)[:, 0, :]
        if pending is not None:
            finish_exchange(pending, dx)
        pending = started

    conv_parts = jnp.concatenate(
        [jnp.stack([g_conv[l][0] for l in range(nl)]).reshape(nl * 3, N_DEV, LANES),
         jnp.stack([g_conv[l][1] for l in range(nl)]).reshape(nl * 4, N_DEV, LANES)], axis=0)
    conv_parts = jnp.transpose(conv_parts, (1, 0, 2))[:, None]
    conv_recv = _all_to_all(conv_parts, "scatter_conv")

    dmod_loc = jnp.stack(dmods).reshape(nl * nb, 3 * D)
    gain_rows = jnp.pad(jnp.concatenate(g_gains + [g_final, loss_row], axis=0),
                        ((0, (-(nl + 2)) % 8), (0, 2 * D)))
    tail_g, _ = _all_gather(jnp.concatenate([dmod_loc, gain_rows], axis=0), "gather_dmod",
                            dep=[res_big[k][3] for k in ("w_in", "w_out", "rep")])
    loss = jnp.sum(tail_g[:, nl * nb + nl + 1, 0])
    dmod_g = tail_g[:, :nl * nb]
    gain_parts = tail_g[:, nl * nb:nl * nb + nl + 1, :D][:, None]
    gain_cat = lambda a, b: jnp.concatenate([a, b[None, :]], axis=0)
    res_gain = _adamw(gain_parts, gain_cat(norm_gain, final_gain), gain_cat(m_norm_gain, m_final_gain),
                      gain_cat(v_norm_gain, v_final_gain), "adamw_gain")
    dmod_all = jnp.transpose(dmod_g.reshape(N_DEV, nl, nb, 3 * D), (1, 0, 2, 3)).reshape(nl, N_DEV * nb, 3 * D)
    dmod_mine = lax.dynamic_slice_in_dim(dmod_all, me * mod_cols, mod_cols, axis=2)
    gw_mod, gb_mod = _mod_grad(c_all, dmod_all, dmod_mine)
    res_w_mod = _adamw(gw_mod.reshape(1, 1, nl * D, mod_cols), w_mod.reshape(nl * D, mod_cols),
                       m_w_mod.reshape(nl * D, mod_cols), v_w_mod.reshape(nl * D, mod_cols), "adamw_w_mod")
    res_w_mod = [a.reshape(nl, D, mod_cols) for a in res_w_mod]
    res_b_mod = _adamw(gb_mod.reshape(1, 1, nl, 3 * D), b_mod, m_b_mod, v_b_mod, "adamw_b_mod")
    finish_exchange(pending, res_b_mod[1])

    cat = lambda a, b: jnp.concatenate([a.reshape(nl * 3, LANES), b.reshape(nl * 4, LANES)], axis=0)
    res_conv = _adamw(conv_recv, cat(conv_a_w, lru_conv_w), cat(m_conv_a_w, m_lru_conv_w),
                      cat(v_conv_a_w, v_lru_conv_w), "adamw_conv")
    res_conv_a = [a[:nl * 3].reshape(nl, 3, LANES) for a in res_conv]
    res_lru_conv = [a[nl * 3:].reshape(nl, 4, LANES) for a in res_conv]

    res_rep = []
    for k in range(4):
        off, d = 0, {}
        for n in rep_names:
            n_rows = rep_w[n][0].size // LANES
            d[n] = res_big["rep"][k][:, off:off + n_rows].reshape(rep_w[n].shape)
            off += n_rows
        res_rep.append(d)

    def leaf(k, name):
        if name == "norm_gain":
            return res_gain[k][:nl]
        if name == "final_gain":
            return res_gain[k][nl]
        if name == "w_mod":
            return res_w_mod[k]
        if name == "b_mod":
            return res_b_mod[k]
        if name in ("w_in", "w_out"):
            return res_big[name][k]
        if name == "conv_a_w":
            return res_conv_a[k]
        if name == "lru_conv_w":
            return res_lru_conv[k]
        return res_rep[k][name]

    order = ["norm_gain", "w_mod", "b_mod", "w_in", "w_out", "conv_a_w", "sgu_w", "sgu_b", "lru_conv_w",
             "lru_conv_b", "lru_wa", "lru_ba", "lru_wx", "lru_bx", "lru_lambda", "final_gain"]
    outs = [loss, dx]
    for k in range(4):
        outs += [leaf(k, n) for n in order]
    return tuple(outs)
```

```python
import functools

import jax
import jax.numpy as jnp
from jax import lax
from jax.experimental import pallas as pl
from jax.experimental.pallas import tpu as pltpu

F32 = jnp.float32
BF16 = jnp.bfloat16

D = 1024
N_DEV = 8
N_SEG = 12
LANES = 128
SUBLANES = 8
CHUNK = 128
HALO = 16
FWD_CHUNKS_PER_TRIP = 16
BWD_CHUNKS_PER_TRIP = 16
UNIT = 512
UNITS_PER_DEV = 3
EPS = 1e-6
LRU_C = 8.0
ADAM_LR, ADAM_B1, ADAM_B2, ADAM_EPS, ADAM_WD, ADAM_STEP = 0.001, 0.9, 0.999, 1e-08, 0.01, 10
VMEM_LIMIT = 56 * 1024 * 1024

AX, AB, AC, AZ, SU, SV, SZ, RX, RZ, GA, GS, GR = range(N_SEG)
MESH = pl.DeviceIdType.MESH


def _params(sem=None):
    return pltpu.CompilerParams(dimension_semantics=sem, vmem_limit_bytes=VMEM_LIMIT)


def _my_index():
    return 4 * lax.axis_index("x") + 2 * lax.axis_index("y") + lax.axis_index("c")


def _all_gather(block, name, dep=None):
    deps = [] if dep is None else list(dep) if isinstance(dep, (list, tuple)) else [dep]

    def body(x_ref, *refs):
        out_ref, token, send_sems, recv_sems, local_sem = refs[-5:]
        x, y, c = lax.axis_index("x"), lax.axis_index("y"), lax.axis_index("c")
        me, sibling = (x, y, c), (x, y, 1 - c)
        chips = [(1 - x, y), (x, 1 - y), (1 - x, 1 - y)]
        token[...] = jnp.zeros_like(token)

        def rows(px, py, pc):
            return out_ref.at[4 * px + 2 * py + pc]

        def copy(k, blk, to, src=None):
            return pltpu.make_async_remote_copy(
                src_ref=rows(*blk) if src is None else src, dst_ref=rows(*blk),
                send_sem=send_sems.at[k], recv_sem=recv_sems.at[k], device_id=to, device_id_type=MESH)

        mine = pltpu.make_async_copy(x_ref, rows(*me), local_sem)
        mine.start()
        first = [copy(0, me, sibling, src=x_ref)]
        first += [copy(1 + j, me, (*chip, c), src=x_ref) for j, chip in enumerate(chips)]
        for cp in first:
            cp.start()
        passed = [copy(4 + j, (*chip, c), sibling) for j, chip in enumerate(chips)]
        for j, chip in enumerate(chips):
            copy(1 + j, (*chip, c), me).wait_recv()
            passed[j].start()
        copy(0, sibling, me).wait_recv()
        for j, chip in enumerate(chips):
            copy(4 + j, (*chip, 1 - c), me).wait_recv()
        for cp in first + passed:
            cp.wait_send()
        mine.wait()

    return pl.pallas_call(
        body, name=name,
        out_shape=[jax.ShapeDtypeStruct((N_DEV,) + block.shape, block.dtype), jax.ShapeDtypeStruct((8, LANES), F32)],
        in_specs=[pl.BlockSpec(memory_space=pltpu.VMEM)]
        + [pl.BlockSpec(memory_space=pl.ANY)] * len(deps),
        out_specs=[pl.BlockSpec(memory_space=pl.ANY), pl.BlockSpec(memory_space=pltpu.VMEM)],
        scratch_shapes=[pltpu.SemaphoreType.DMA((7,)), pltpu.SemaphoreType.DMA((7,)), pltpu.SemaphoreType.DMA],
    )(block, *deps)


def _all_to_all(blocks, name):
    def body(x_ref, out_ref, send_sems, recv_sems, local_sem):
        x, y, c = lax.axis_index("x"), lax.axis_index("y"), lax.axis_index("c")
        my = 4 * x + 2 * y + c
        mine = pltpu.make_async_copy(x_ref.at[my], out_ref.at[my], local_sem)
        mine.start()
        peers = []
        for r in range(1, N_DEV):
            px = 1 - x if r & 4 else x
            py = 1 - y if r & 2 else y
            pc = 1 - c if r & 1 else c
            peers.append((r - 1, 4 * px + 2 * py + pc, (px, py, pc)))

        def copy(k, src_slot, dst_slot, to):
            return pltpu.make_async_remote_copy(
                src_ref=x_ref.at[src_slot], dst_ref=out_ref.at[dst_slot],
                send_sem=send_sems.at[k], recv_sem=recv_sems.at[k], device_id=to, device_id_type=MESH)

        sends = [copy(k, pid, my, to) for k, pid, to in peers]
        for cp in sends:
            cp.start()
        for k, pid, to in peers:
            copy(k, pid, pid, to).wait_recv()
        for cp in sends:
            cp.wait_send()
        mine.wait()

    return pl.pallas_call(
        body, name=name,
        out_shape=jax.ShapeDtypeStruct(blocks.shape, blocks.dtype),
        in_specs=[pl.BlockSpec(memory_space=pltpu.VMEM)],
        out_specs=pl.BlockSpec(memory_space=pl.ANY),
        scratch_shapes=[pltpu.SemaphoreType.DMA((7,)), pltpu.SemaphoreType.DMA((7,)), pltpu.SemaphoreType.DMA],
    )(blocks)


_HBM = pl.BlockSpec(memory_space=pltpu.HBM)
_SEM = pl.BlockSpec(memory_space=pltpu.SEMAPHORE)
_EFFECT = pltpu.SideEffectType.DATAFLOW_SIDE_EFFECTING


def _peers_all(x, y, c):
    out = []
    for r in range(1, N_DEV):
        px = 1 - x if r & 4 else x
        py = 1 - y if r & 2 else y
        pc = 1 - c if r & 1 else c
        out.append((r - 1, 4 * px + 2 * py + pc, (px, py, pc)))
    return out


def _peers_same_core(x, y, c):
    return [(k, 4 * px + 2 * py + c, (px, py, c))
            for k, (px, py) in enumerate([(1 - x, y), (x, 1 - y), (1 - x, 1 - y)])]


def _split_start(src, peers_fn, scatter, name, dep=None):
    blk = src.shape[1:] if scatter else src.shape
    land_shape = (N_DEV,) + tuple(blk)
    n = len(peers_fn(0, 0, 0))
    deps = [] if dep is None else [dep]

    def body(x_ref, land_ref, *rest):
        send_sems, recv_sems, x_thru, land_thru, token = rest[len(deps):]
        x, y, c = lax.axis_index("x"), lax.axis_index("y"), lax.axis_index("c")
        my = 4 * x + 2 * y + c
        for k, pid, to in peers_fn(x, y, c):
            pltpu.make_async_remote_copy(
                src_ref=x_ref.at[pid] if scatter else x_ref, dst_ref=land_ref.at[my],
                send_sem=send_sems.at[k], recv_sem=recv_sems.at[k], device_id=to, device_id_type=MESH).start()
        token[...] = jnp.zeros_like(token)

    return pl.pallas_call(
        body, name=name,
        out_shape=(pltpu.SemaphoreType.DMA((n,)), pltpu.SemaphoreType.DMA((n,)),
                   pltpu.HBM(src.shape, src.dtype), pltpu.HBM(land_shape, src.dtype),
                   jax.ShapeDtypeStruct((8, LANES), F32)),
        in_specs=(_HBM, _HBM) + (pl.BlockSpec(memory_space=pl.ANY),) * len(deps),
        out_specs=(_SEM, _SEM, _HBM, _HBM, pl.BlockSpec(memory_space=pltpu.VMEM)),
        input_output_aliases={0: 2, 1: 3},
        compiler_params=pltpu.CompilerParams(has_side_effects=_EFFECT),
    )(pltpu.with_memory_space_constraint(src, pltpu.HBM),
      pltpu.with_memory_space_constraint(lax.empty(land_shape, src.dtype), pltpu.HBM), *deps)


def _split_wait(handles, after, peers_fn, scatter, own, name):
    send_sems, recv_sems, src_thru, land_thru, _ = handles
    blk = land_thru.shape[1:]
    after = list(after) if isinstance(after, (list, tuple)) else [after]

    def body(x_ref, land_ref, send_sems, recv_sems, *rest):
        stage = rest[len(after) + 2:]
        x, y, c = lax.axis_index("x"), lax.axis_index("y"), lax.axis_index("c")
        if own:
            my = 4 * x + 2 * y + c
            mine = _staged_copy(x_ref.at[my] if scatter else x_ref, land_ref.at[my], *stage)
        for k, pid, to in peers_fn(x, y, c):
            cp = pltpu.make_async_remote_copy(
                src_ref=x_ref.at[pid] if scatter else x_ref, dst_ref=land_ref.at[pid],
                send_sem=send_sems.at[k], recv_sem=recv_sems.at[k], device_id=to, device_id_type=MESH)
            cp.wait_send()
            cp.wait_recv()
        if own:
            mine.wait()

    return pl.pallas_call(
        body, name=name,
        out_shape=(pltpu.HBM(src_thru.shape, src_thru.dtype), pltpu.HBM(land_thru.shape, land_thru.dtype)),
        in_specs=(_HBM, _HBM, _SEM, _SEM) + (pl.BlockSpec(memory_space=pl.ANY),) * len(after),
        out_specs=(_HBM, _HBM),
        input_output_aliases={0: 0, 1: 1},
        scratch_shapes=[pltpu.VMEM(blk, land_thru.dtype), pltpu.SemaphoreType.DMA((2,))] if own else [],
        compiler_params=pltpu.CompilerParams(has_side_effects=_EFFECT, vmem_limit_bytes=VMEM_LIMIT),
    )(src_thru, land_thru, send_sems, recv_sems, *after)


def _staged_copy(src_ref, dst_ref, buf, sems):
    leg = pltpu.make_async_copy(src_ref, buf, sems.at[0])
    leg.start()
    leg.wait()
    leg = pltpu.make_async_copy(buf, dst_ref, sems.at[1])
    leg.start()
    return leg


def _multi_start(srcs, name, dep=None):
    n_src = len(srcs)
    lands = [(N_DEV,) + tuple(a.shape[1:] if sc else a.shape) for a, sc in srcs]
    deps = [] if dep is None else [dep]

    def body(*refs):
        ins, outs = refs[:2 * n_src], refs[2 * n_src + len(deps):]
        x, y, c = lax.axis_index("x"), lax.axis_index("y"), lax.axis_index("c")
        my = 4 * x + 2 * y + c
        for i, (_, scatter) in enumerate(srcs):
            x_ref, land_ref, send_sems, recv_sems = ins[2 * i], ins[2 * i + 1], outs[4 * i], outs[4 * i + 1]
            for k, pid, to in _peers_all(x, y, c):
                pltpu.make_async_remote_copy(
                    src_ref=x_ref.at[pid] if scatter else x_ref, dst_ref=land_ref.at[my],
                    send_sem=send_sems.at[k], recv_sem=recv_sems.at[k], device_id=to, device_id_type=MESH).start()
        outs[-1][...] = jnp.zeros_like(outs[-1])

    out_shape, out_specs, operands = [], [], []
    for (a, _), land in zip(srcs, lands):
        out_shape += [pltpu.SemaphoreType.DMA((N_DEV - 1,)), pltpu.SemaphoreType.DMA((N_DEV - 1,)),
                      pltpu.HBM(a.shape, a.dtype), pltpu.HBM(land, a.dtype)]
        out_specs += [_SEM, _SEM, _HBM, _HBM]
        operands += [pltpu.with_memory_space_constraint(a, pltpu.HBM),
                     pltpu.with_memory_space_constraint(lax.empty(land, a.dtype), pltpu.HBM)]
    res = pl.pallas_call(
        body, name=name,
        out_shape=tuple(out_shape) + (jax.ShapeDtypeStruct((8, LANES), F32),),
        in_specs=(_HBM,) * (2 * n_src) + (pl.BlockSpec(memory_space=pl.ANY),) * len(deps),
        out_specs=tuple(out_specs) + (pl.BlockSpec(memory_space=pltpu.VMEM),),
        input_output_aliases={2 * i + j: 4 * i + 2 + j for i in range(n_src) for j in range(2)},
        compiler_params=pltpu.CompilerParams(has_side_effects=_EFFECT),
    )(*operands, *deps)
    return [tuple(res[4 * i:4 * i + 4]) + (srcs[i][1],) for i in range(n_src)], res[-1]


def _multi_wait(started, after, name):
    n_src = len(started)
    after = list(after) if isinstance(after, (list, tuple)) else [after]

    def body(*refs):
        ins = refs[:4 * n_src]
        stage = refs[4 * n_src + len(after) + 2 * n_src:]
        x, y, c = lax.axis_index("x"), lax.axis_index("y"), lax.axis_index("c")
        my = 4 * x + 2 * y + c
        pending = []
        for i, h in enumerate(started):
            x_ref, land_ref, send_sems, recv_sems = ins[4 * i:4 * i + 4]
            scatter = h[4]
            pending.append(_staged_copy(x_ref.at[my] if scatter else x_ref, land_ref.at[my],
                                        stage[2 * i], stage[2 * i + 1]))
            for k, pid, to in _peers_all(x, y, c):
                cp = pltpu.make_async_remote_copy(
                    src_ref=x_ref.at[pid] if scatter else x_ref, dst_ref=land_ref.at[pid],
                    send_sem=send_sems.at[k], recv_sem=recv_sems.at[k], device_id=to, device_id_type=MESH)
                cp.wait_send()
                cp.wait_recv()
        for leg in pending:
            leg.wait()

    operands, out_shape, scratch = [], [], []
    for send, recv, src_thru, land_thru, _ in started:
        operands += [src_thru, land_thru, send, recv]
        out_shape += [pltpu.HBM(src_thru.shape, src_thru.dtype), pltpu.HBM(land_thru.shape, land_thru.dtype)]
        scratch += [pltpu.VMEM(land_thru.shape[1:], land_thru.dtype), pltpu.SemaphoreType.DMA((2,))]
    res = pl.pallas_call(
        body, name=name,
        out_shape=tuple(out_shape),
        in_specs=(_HBM, _HBM, _SEM, _SEM) * n_src + (pl.BlockSpec(memory_space=pl.ANY),) * len(after),
        out_specs=(_HBM,) * (2 * n_src),
        input_output_aliases={4 * i + j: 2 * i + j for i in range(n_src) for j in range(2)},
        scratch_shapes=scratch,
        compiler_params=pltpu.CompilerParams(has_side_effects=_EFFECT, vmem_limit_bytes=VMEM_LIMIT),
    )(*operands, *after)
    return [res[2 * i + 1] for i in range(n_src)]


def _gather_finish(block, land, name):
    def body(x_ref, land_ref, out_ref, token, send_sems, recv_sems, buf, local_sems):
        x, y, c = lax.axis_index("x"), lax.axis_index("y"), lax.axis_index("c")
        my, sib_id, sibling = 4 * x + 2 * y + c, 4 * x + 2 * y + 1 - c, (x, y, 1 - c)
        token[...] = jnp.zeros_like(token)

        def copy(k, slot, src=None):
            return pltpu.make_async_remote_copy(
                src_ref=land_ref.at[slot] if src is None else src, dst_ref=out_ref.at[slot],
                send_sem=send_sems.at[k], recv_sem=recv_sems.at[k], device_id=sibling, device_id_type=MESH)

        chips = _peers_same_core(x, y, c)
        sends = [copy(0, my, src=x_ref)] + [copy(1 + k, pid) for k, pid, _ in chips]
        for cp in sends:
            cp.start()
        mine = _staged_copy(x_ref, out_ref.at[my], buf, local_sems)
        copy(0, sib_id).wait_recv()
        for k, pid, _ in chips:
            copy(1 + k, pid + 1 - 2 * c).wait_recv()
        for cp in sends:
            cp.wait_send()
        mine.wait()

    return pl.pallas_call(
        body, name=name,
        out_shape=[jax.ShapeDtypeStruct(land.shape, land.dtype), jax.ShapeDtypeStruct((8, LANES), F32)],
        in_specs=[pl.BlockSpec(memory_space=pl.ANY), pl.BlockSpec(memory_space=pl.ANY)],
        out_specs=[pl.BlockSpec(memory_space=pl.ANY), pl.BlockSpec(memory_space=pltpu.VMEM)],
        input_output_aliases={1: 0},
        scratch_shapes=[pltpu.SemaphoreType.DMA((4,)), pltpu.SemaphoreType.DMA((4,)),
                        pltpu.VMEM(block.shape, block.dtype), pltpu.SemaphoreType.DMA((2,))],
        compiler_params=pltpu.CompilerParams(vmem_limit_bytes=VMEM_LIMIT),
    )(block, land)


def _forward_start(block, land, name):
    def body(x_ref, land_ref, send_sems, recv_sems, x_thru, land_thru, token):
        x, y, c = lax.axis_index("x"), lax.axis_index("y"), lax.axis_index("c")
        my, sibling = 4 * x + 2 * y + c, (x, y, 1 - c)
        slots = [(0, my, x_ref)] + [(1 + k, pid, land_ref.at[pid]) for k, pid, _ in _peers_same_core(x, y, c)]
        for k, slot, src in slots:
            pltpu.make_async_remote_copy(
                src_ref=src, dst_ref=land_ref.at[slot], send_sem=send_sems.at[k], recv_sem=recv_sems.at[k],
                device_id=sibling, device_id_type=MESH).start()
        token[...] = jnp.zeros_like(token)

    return pl.pallas_call(
        body, name=name,
        out_shape=(pltpu.SemaphoreType.DMA((4,)), pltpu.SemaphoreType.DMA((4,)),
                   pltpu.HBM(block.shape, block.dtype), pltpu.HBM(land.shape, land.dtype),
                   jax.ShapeDtypeStruct((8, LANES), F32)),
        in_specs=(_HBM, _HBM),
        out_specs=(_SEM, _SEM, _HBM, _HBM, pl.BlockSpec(memory_space=pltpu.VMEM)),
        input_output_aliases={0: 2, 1: 3},
        compiler_params=pltpu.CompilerParams(has_side_effects=_EFFECT),
    )(block, land)


def _forward_wait(handles, after, name):
    send_sems, recv_sems, block_thru, land_thru, _ = handles

    def body(x_ref, land_ref, send_sems, recv_sems, after_ref, x_dead, got_ref, buf, local_sems):
        x, y, c = lax.axis_index("x"), lax.axis_index("y"), lax.axis_index("c")
        my, sib_id, sibling = 4 * x + 2 * y + c, 4 * x + 2 * y + 1 - c, (x, y, 1 - c)
        mine = _staged_copy(x_ref, land_ref.at[my], buf, local_sems)
        slots = [(0, my, sib_id)] + [(1 + k, pid, pid + 1 - 2 * c) for k, pid, _ in _peers_same_core(x, y, c)]
        for k, sent, got in slots:
            cp = pltpu.make_async_remote_copy(
                src_ref=land_ref.at[sent], dst_ref=land_ref.at[got], send_sem=send_sems.at[k],
                recv_sem=recv_sems.at[k], device_id=sibling, device_id_type=MESH)
            cp.wait_send()
            cp.wait_recv()
        mine.wait()

    return pl.pallas_call(
        body, name=name,
        out_shape=(pltpu.HBM(block_thru.shape, block_thru.dtype), pltpu.HBM(land_thru.shape, land_thru.dtype)),
        in_specs=(_HBM, _HBM, _SEM, _SEM, pl.BlockSpec(memory_space=pl.ANY)),
        out_specs=(_HBM, _HBM),
        input_output_aliases={0: 0, 1: 1},
        scratch_shapes=[pltpu.VMEM(block_thru.shape, block_thru.dtype), pltpu.SemaphoreType.DMA((2,))],
        compiler_params=pltpu.CompilerParams(has_side_effects=_EFFECT, vmem_limit_bytes=VMEM_LIMIT),
    )(block_thru, land_thru, send_sems, recv_sems, after)[1]


def _after(v, token):
    return v + token[0, 0].astype(v.dtype)


def _dsilu(s, silu):
    return s + silu * (1.0 - s)


def _log1p(x):
    u = 1.0 + x
    d = u - 1.0
    return jnp.where(d == 0.0, x, jnp.log(u) * (x / jnp.where(d == 0.0, 1.0, d)))


def _softplus_neg(lam):
    return jnp.maximum(-lam, 0.0) + _log1p(jnp.exp(-jnp.abs(lam)))


def _neg_expm1(y, exp_y):
    poly = -y * (1.0 + y * (0.5 + y * (1.0 / 6.0 + y * (1.0 / 24.0))))
    return jnp.where(y > -0.05, poly, 1.0 - exp_y)


def _sigmoid(x):
    return 0.5 * jnp.tanh(0.5 * x) + 0.5


def _shift_dn(cur, prev, k):
    ext = jnp.concatenate([prev, cur], axis=0)
    return pltpu.roll(ext, k, 0)[HALO:, :]


def _shift_up(cur, nxt, k):
    n = cur.shape[0]
    ext = jnp.concatenate([cur, nxt], axis=0)
    return pltpu.roll(ext, n + HALO - k, 0)[:n, :]


def _scan_fwd(a, b, h_prev):
    groups = a.shape[0] // SUBLANES
    a3 = a.reshape(groups, SUBLANES, LANES)
    b3 = b.reshape(groups, SUBLANES, LANES)
    row = lax.broadcasted_iota(jnp.int32, a3.shape, 1)
    k = 1
    while k < SUBLANES:
        a_sh = jnp.where(row >= k, pltpu.roll(a3, k, 1), 1.0)
        b_sh = jnp.where(row >= k, pltpu.roll(b3, k, 1), 0.0)
        b3 = a3 * b_sh + b3
        a3 = a3 * a_sh
        k *= 2
    carry = h_prev[HALO - 1:HALO, :]
    out = []
    for i in range(groups):
        hg = b3[i] + a3[i] * carry
        out.append(hg)
        carry = hg[SUBLANES - 1:SUBLANES, :]
    return jnp.concatenate(out, axis=0)


def _scan_rev(a_next, g, lam_next):
    groups = g.shape[0] // SUBLANES
    a3 = a_next.reshape(groups, SUBLANES, LANES)
    g3 = g.reshape(groups, SUBLANES, LANES)
    row = lax.broadcasted_iota(jnp.int32, a3.shape, 1)
    k = 1
    while k < SUBLANES:
        ok = row < SUBLANES - k
        a_sh = jnp.where(ok, pltpu.roll(a3, SUBLANES - k, 1), 1.0)
        g_sh = jnp.where(ok, pltpu.roll(g3, SUBLANES - k, 1), 0.0)
        g3 = g3 + a3 * g_sh
        a3 = a3 * a_sh
        k *= 2
    carry = lam_next[0:1, :]
    out = [None] * groups
    for i in reversed(range(groups)):
        lg = g3[i] + a3[i] * carry
        out[i] = lg
        carry = lg[0:1, :]
    return jnp.concatenate(out, axis=0)


def _rowsum(v):
    return jnp.sum(v, axis=0, keepdims=True)


def _dot(a, b):
    return jnp.dot(a, b, preferred_element_type=F32)


def _dot_nt(a, b):
    return lax.dot_general(a, b, (((1,), (1,)), ((), ())), preferred_element_type=F32)


def _dot_tn(a, b):
    return lax.dot_general(a, b, (((0,), (0,)), ((), ())), preferred_element_type=F32)


class _MixerWeights:
    def __init__(self, caw_ref, sw_ref, sb_ref, lcw_ref, lcb_ref, wa_ref, wx_ref, ba_ref, bx_ref, lam_ref):
        self.caw = [caw_ref[j:j + 1, :] for j in range(3)]
        self.lcw = [lcw_ref[j:j + 1, :] for j in range(4)]
        self.lcb = lcb_ref[...]
        row = lax.broadcasted_iota(jnp.int32, (CHUNK, CHUNK), 0)
        col = lax.broadcasted_iota(jnp.int32, (CHUNK, CHUNK), 1)
        self.tril = col <= row
        self.sw = jnp.where(self.tril, sw_ref[...], 0.0).astype(BF16)
        self.sb = sb_ref[...]
        self.wa = wa_ref[...]
        self.wx = wx_ref[...]
        self.ba = ba_ref[...]
        self.bx = bx_ref[...]
        lam = lam_ref[...]
        self.neg_c_sp = -LRU_C * _softplus_neg(lam)
        self.dsp_dlam = -_sigmoid(-lam)


def _mixer_a(ld, ldp, w, cv=None):
    t = {}
    a_x, a_c = ld(AX), ld(AC)
    t["a_x"], t["a_c"], t["a_b"], t["a_z"] = a_x, a_c, ld(AB), ld(AZ)
    t["ca"] = ca = a_c * a_x
    if cv is None:
        ca_p = ldp(AC) * ldp(AX)
        cv = w.caw[2] * ca + w.caw[1] * _shift_dn(ca, ca_p, 1) + w.caw[0] * _shift_dn(ca, ca_p, 2)
    t["cv"] = cv
    t["sa"] = _sigmoid(t["a_z"])
    t["silu_az"] = t["a_z"] * t["sa"]
    t["y_a"] = t["silu_az"] * t["a_b"] * t["cv"]
    return t


def _mixer_b(ld, w):
    t = {}
    v = ld(SV)
    vc = v - jnp.mean(v, axis=1, keepdims=True)
    t["rstd"] = lax.rsqrt(jnp.mean(vc * vc, axis=1, keepdims=True) + EPS)
    t["vn"] = vc * t["rstd"]
    t["z"] = _dot(w.sw, t["vn"].astype(BF16)) + w.sb
    t["s_u"], t["s_z"] = ld(SU), ld(SZ)
    t["ss"] = _sigmoid(t["s_z"])
    t["silu_sz"] = t["s_z"] * t["ss"]
    t["y_s"] = t["silu_sz"] * t["s_u"] * t["z"]
    return t


def _mixer_c(ld, ldp, w, backward, xc=None):
    t = {}
    t["r_x"] = r_x = ld(RX)
    if xc is None:
        r_xp = ldp(RX)
        xc = (w.lcb + w.lcw[0] * _shift_dn(r_x, r_xp, 3) + w.lcw[1] * _shift_dn(r_x, r_xp, 2)
              + w.lcw[2] * _shift_dn(r_x, r_xp, 1) + w.lcw[3] * r_x)
    t["xc"] = xc
    xcb = xc.astype(BF16)
    t["r"] = _sigmoid(_dot(xcb, w.wa) + w.ba)
    t["i"] = _sigmoid(_dot(xcb, w.wx) + w.bx)
    la = t["r"] * w.neg_c_sp
    t["a"] = jnp.exp(la)
    t["a2"] = t["a"] * t["a"]
    t["em"] = _neg_expm1(2.0 * la, t["a2"])
    if backward:
        t["inv_mult"] = lax.rsqrt(t["em"])
        t["mult"] = t["em"] * t["inv_mult"]
    else:
        t["mult"] = jnp.sqrt(t["em"])
    t["b"] = t["mult"] * (t["i"] * xc)
    t["r_z"] = ld(RZ)
    t["sr"] = _sigmoid(t["r_z"])
    t["silu_rz"] = t["r_z"] * t["sr"]
    return t


def _mixer_pre_scan(ld, ldp, w, backward, cv=None, xc=None):
    t = {**_mixer_a(ld, ldp, w, cv), **_mixer_b(ld, w), **_mixer_c(ld, ldp, w, backward, xc)}
    t["ga"], t["gs"], t["gr"] = _sigmoid(ld(GA)), _sigmoid(ld(GS)), _sigmoid(ld(GR))
    return t


def _weight_specs(n_cb_axis):
    def at(fn):
        return lambda *g: fn(g[n_cb_axis])
    return [
        pl.BlockSpec((3, LANES), at(lambda cb: (0, cb))),
        pl.BlockSpec((None, CHUNK, CHUNK), at(lambda cb: (cb, 0, 0))),
        pl.BlockSpec((None, CHUNK, LANES), at(lambda cb: (cb, 0, 0))),
        pl.BlockSpec((4, LANES), at(lambda cb: (0, cb))),
        pl.BlockSpec((1, LANES), at(lambda cb: (0, cb))),
        pl.BlockSpec((None, LANES, LANES), at(lambda cb: (cb, 0, 0))),
        pl.BlockSpec((None, LANES, LANES), at(lambda cb: (cb, 0, 0))),
        pl.BlockSpec((1, LANES), at(lambda cb: (0, cb))),
        pl.BlockSpec((1, LANES), at(lambda cb: (0, cb))),
        pl.BlockSpec((1, LANES), at(lambda cb: (0, cb))),
    ]


def _chunk_loaders(p_ref, c):
    r0 = pl.multiple_of(c * CHUNK, CHUNK)
    rp = pl.multiple_of(jnp.maximum(c * CHUNK - HALO, 0), HALO)

    def ld(j):
        return p_ref[j, pl.ds(r0, CHUNK), :].astype(F32)

    def ldp(j):
        return jnp.where(c > 0, p_ref[j, pl.ds(rp, HALO), :].astype(F32), 0.0)

    return r0, rp, ld, ldp


def _mixer_fwd(proj, mw):
    _, nb, s, _ = proj.shape
    n_chunks = s // CHUNK

    def body(p_ref, *refs):
        w = _MixerWeights(*refs[:10])
        merged_ref, hs_ref, cv_ref, xc_ref = refs[10:]

        def chunk(c, h_prev):
            r0, _, ld, ldp = _chunk_loaders(p_ref, c)
            t = _mixer_pre_scan(ld, ldp, w, False)
            h = _scan_fwd(t["a"], t["b"], h_prev)
            y_r = t["silu_rz"] * h
            merged = t["ga"] * t["y_a"] + t["gs"] * t["y_s"] + t["gr"] * y_r
            merged_ref[pl.ds(r0, CHUNK), :] = merged.astype(BF16)
            hs_ref[pl.ds(r0, CHUNK), :] = h
            cv_ref[pl.ds(r0, CHUNK), :] = t["cv"].astype(BF16)
            xc_ref[pl.ds(r0, CHUNK), :] = t["xc"].astype(BF16)
            return h[CHUNK - HALO:, :]

        def group(i, carry):
            for k in range(FWD_CHUNKS_PER_TRIP):
                carry = chunk(FWD_CHUNKS_PER_TRIP * i + k, carry)
            return carry

        assert n_chunks % FWD_CHUNKS_PER_TRIP == 0
        lax.fori_loop(0, n_chunks // FWD_CHUNKS_PER_TRIP, group, jnp.zeros((HALO, LANES), F32))

    slab = pl.BlockSpec((None, s, LANES), lambda cb, b: (b, 0, cb))
    half = jax.ShapeDtypeStruct((nb, s, D), BF16)
    return pl.pallas_call(
        body, name="mixer_fwd", grid=(D // LANES, nb),
        in_specs=[pl.BlockSpec((N_SEG, None, s, LANES), lambda cb, b: (0, b, 0, cb))] + _weight_specs(0),
        out_specs=[slab, slab, slab, slab],
        out_shape=[half, jax.ShapeDtypeStruct((nb, s, D), F32), half, half],
        compiler_params=_params(("arbitrary", "arbitrary")),
    )(proj, *mw)


def _mixer_bwd(proj, dmerged, hs, cv, xc, mw):
    _, nb, s, _ = proj.shape
    n_chunks = s // CHUNK

    def body(p_ref, dm_ref, hs_ref, cv_ref, xc_ref, *refs):
        w = _MixerWeights(*refs[:10])
        dp_ref, g_caw, g_sw, g_sb, g_lcw, g_vec, g_wa, g_wx = refs[10:]

        @pl.when(pl.program_id(1) == 0)
        def _():
            for ref in (g_caw, g_sw, g_sb, g_lcw, g_vec, g_wa, g_wx):
                ref[...] = jnp.zeros_like(ref)

        def chunk(i, carry):
            dcv_n, dxc_n, lam_n, a_n = carry
            c = n_chunks - 1 - i
            r0, rp, ld, ldp = _chunk_loaders(p_ref, c)
            t = _mixer_pre_scan(ld, ldp, w, True, cv_ref[pl.ds(r0, CHUNK), :].astype(F32),
                                xc_ref[pl.ds(r0, CHUNK), :].astype(F32))
            h = hs_ref[pl.ds(r0, CHUNK), :]
            h_p = jnp.where(c > 0, hs_ref[pl.ds(rp, HALO), :], 0.0)
            h_prev = _shift_dn(h, h_p, 1)
            dm = dm_ref[pl.ds(r0, CHUNK), :].astype(F32)
            y_r = t["silu_rz"] * h

            def out(j, val):
                dp_ref[j, pl.ds(r0, CHUNK), :] = val.astype(BF16)

            ga, gs, gr = t["ga"], t["gs"], t["gr"]
            dy_a, dy_s, dy_r = dm * ga, dm * gs, dm * gr
            out(GA, (dy_a * t["y_a"]) * (1.0 - ga))
            out(GS, (dy_s * t["y_s"]) * (1.0 - gs))
            out(GR, (dy_r * y_r) * (1.0 - gr))

            dy_ab = dy_a * t["a_b"]
            out(AZ, dy_ab * (t["cv"] * _dsilu(t["sa"], t["silu_az"])))
            out(AB, (dy_a * t["cv"]) * t["silu_az"])
            dcv = dy_ab * t["silu_az"]
            dcv1, dcv2 = _shift_up(dcv, dcv_n, 1), _shift_up(dcv, dcv_n, 2)
            dca = w.caw[2] * dcv + w.caw[1] * dcv1 + w.caw[0] * dcv2
            out(AC, dca * t["a_x"])
            out(AX, dca * t["a_c"])
            g_caw[2:3, :] += _rowsum(dcv * t["ca"])
            g_caw[1:2, :] += _rowsum(dcv1 * t["ca"])
            g_caw[0:1, :] += _rowsum(dcv2 * t["ca"])

            dy_su = dy_s * t["s_u"]
            out(SZ, dy_su * (t["z"] * _dsilu(t["ss"], t["silu_sz"])))
            out(SU, (dy_s * t["z"]) * t["silu_sz"])
            dz = dy_su * t["silu_sz"]
            dzb = dz.astype(BF16)
            g_sb[...] += jnp.broadcast_to(jnp.sum(dz, axis=1, keepdims=True), (CHUNK, LANES))
            g_sw[...] += _dot_nt(dzb, t["vn"].astype(BF16))
            dvn = _dot_tn(w.sw, dzb)
            vn = t["vn"]
            out(SV, t["rstd"] * (dvn - jnp.mean(dvn, axis=1, keepdims=True)
                                 - vn * jnp.mean(dvn * vn, axis=1, keepdims=True)))

            out(RZ, (dy_r * h) * _dsilu(t["sr"], t["silu_rz"]))
            lam = _scan_rev(_shift_up(t["a"], a_n, 1), dy_r * t["silu_rz"], lam_n)
            a, r, ig, xc = t["a"], t["r"], t["i"], t["xc"]
            lam_mult, lam_i = lam * t["mult"], lam * ig
            d_i = lam_mult * xc
            d_mult = lam_i * xc
            dxc = lam_mult * ig
            dla = lam * h_prev * a - d_mult * (t["a2"] * t["inv_mult"])
            g_vec[3:4, :] += _rowsum(dla * r) * (-LRU_C * w.dsp_dlam)
            dpr = (dla * w.neg_c_sp) * r * (1.0 - r)
            dpi = d_i * ig * (1.0 - ig)
            dprb, dpib, xcb = dpr.astype(BF16), dpi.astype(BF16), xc.astype(BF16)
            g_wa[...] += _dot_tn(xcb, dprb)
            g_wx[...] += _dot_tn(xcb, dpib)
            g_vec[1:2, :] += _rowsum(dpr)
            g_vec[2:3, :] += _rowsum(dpi)
            dxc = dxc + _dot_nt(dprb, w.wa) + _dot_nt(dpib, w.wx)
            g_vec[0:1, :] += _rowsum(dxc)
            dxcs = [_shift_up(dxc, dxc_n, 3), _shift_up(dxc, dxc_n, 2), _shift_up(dxc, dxc_n, 1), dxc]
            out(RX, w.lcw[3] * dxcs[3] + w.lcw[2] * dxcs[2] + w.lcw[1] * dxcs[1] + w.lcw[0] * dxcs[0])
            for j in range(4):
                g_lcw[j:j + 1, :] += _rowsum(dxcs[j] * t["r_x"])
            return dcv[:HALO, :], dxc[:HALO, :], lam[:HALO, :], a[:HALO, :]

        zero = jnp.zeros((HALO, LANES), F32)
        def group(i, carry):
            for k in range(BWD_CHUNKS_PER_TRIP):
                carry = chunk(BWD_CHUNKS_PER_TRIP * i + k, carry)
            return carry

        assert n_chunks % BWD_CHUNKS_PER_TRIP == 0
        lax.fori_loop(0, n_chunks // BWD_CHUNKS_PER_TRIP, group, (zero, zero, zero, zero))

        @pl.when(pl.program_id(1) == nb - 1)
        def _():
            g_sw[...] = jnp.where(w.tril, g_sw[...], 0.0)

    slab = lambda dt: pl.BlockSpec((None, s, LANES), lambda cb, b: (b, 0, cb))
    seg = pl.BlockSpec((N_SEG, None, s, LANES), lambda cb, b: (0, b, 0, cb))
    rows = lambda n: pl.BlockSpec((n, LANES), lambda cb, b: (0, cb))
    sq = pl.BlockSpec((None, LANES, LANES), lambda cb, b: (cb, 0, 0))
    n_cb = D // LANES
    return pl.pallas_call(
        body, name="mixer_bwd", grid=(n_cb, nb),
        in_specs=[seg, slab(BF16), slab(F32), slab(BF16), slab(BF16)] + _weight_specs(0),
        out_specs=[seg, rows(3), sq, sq, rows(4), rows(8), sq, sq],
        out_shape=[
            jax.ShapeDtypeStruct(proj.shape, BF16),
            jax.ShapeDtypeStruct((3, D), F32),
            jax.ShapeDtypeStruct((n_cb, CHUNK, CHUNK), F32),
            jax.ShapeDtypeStruct((n_cb, CHUNK, LANES), F32),
            jax.ShapeDtypeStruct((4, D), F32),
            jax.ShapeDtypeStruct((8, D), F32),
            jax.ShapeDtypeStruct((n_cb, LANES, LANES), F32),
            jax.ShapeDtypeStruct((n_cb, LANES, LANES), F32),
        ],
        compiler_params=_params(("arbitrary", "arbitrary")),
    )(proj, dmerged, hs, cv, xc, *mw)


def _row_tile(s, want):
    return want if s % want == 0 else s


def _norm_mod(x, gain, shift, scale):
    nb, s, _ = x.shape
    tm = _row_tile(s, 512)

    def body(x_ref, g_ref, sh_ref, sc_ref, h_ref, ht_ref):
        xv = x_ref[...]
        r = lax.rsqrt(jnp.mean(xv * xv, axis=1, keepdims=True) + EPS)
        h = ((xv * r) * g_ref[...] * (1.0 + sc_ref[...]) + sh_ref[...]).astype(BF16)
        h_ref[...] = h
        ht_ref[...] = h.T

    tile = pl.BlockSpec((None, tm, D), lambda b, m: (b, m, 0))
    vec = pl.BlockSpec((None, 1, D), lambda b, m: (b, 0, 0))
    return pl.pallas_call(
        body, name="norm_mod", grid=(nb, s // tm),
        in_specs=[tile, pl.BlockSpec((1, D), lambda b, m: (0, 0)), vec, vec],
        out_specs=[tile, pl.BlockSpec((None, D, tm), lambda b, m: (b, 0, m))],
        out_shape=[jax.ShapeDtypeStruct(x.shape, BF16), jax.ShapeDtypeStruct((nb, D, s), BF16)],
        compiler_params=_params(("arbitrary", "arbitrary")),
    )(x, gain, shift, scale)


def _in_proj(h, wg, dep):
    nb, s, _ = h.shape

    def body(h_ref, w0_ref, w1_ref, dep_ref, o_ref):
        hv = h_ref[...]
        o_ref[:, :UNIT] = _dot(hv, w0_ref[...]).astype(BF16)
        o_ref[:, UNIT:] = _dot(hv, w1_ref[...]).astype(BF16)

    def unit(k):
        return pl.BlockSpec((None, D, UNIT),
                            lambda b, j: ((2 * j + k) // UNITS_PER_DEV, 0, (2 * j + k) % UNITS_PER_DEV))

    return pl.pallas_call(
        body, name="in_proj", grid=(nb, N_SEG),
        in_specs=[pl.BlockSpec((None, s, D), lambda b, j: (b, 0, 0)), unit(0), unit(1),
                  pl.BlockSpec((8, LANES), lambda b, j: (0, 0))],
        out_specs=pl.BlockSpec((None, None, s, D), lambda b, j: (j, b, 0, 0)),
        out_shape=jax.ShapeDtypeStruct((N_SEG, nb, s, D), BF16),
        compiler_params=_params(("arbitrary", "arbitrary")),
    )(h, wg, wg, dep)


def _out_proj_norm(x, merged, wout, gate, gain, shift, scale):
    nb, s, _ = x.shape
    tm = _row_tile(s, 512)

    def body(x_ref, m_ref, w_ref, g_ref, gn_ref, sh_ref, sc_ref, o_ref, h_ref, ht_ref):
        xv = x_ref[...] + g_ref[...] * _dot(m_ref[...], w_ref[...])
        o_ref[...] = xv
        r = lax.rsqrt(jnp.mean(xv * xv, axis=1, keepdims=True) + EPS)
        h = ((xv * r) * gn_ref[...] * (1.0 + sc_ref[...]) + sh_ref[...]).astype(BF16)
        h_ref[...] = h
        ht_ref[...] = h.T

    tile = pl.BlockSpec((None, tm, D), lambda b, m: (b, m, 0))
    vec = pl.BlockSpec((None, 1, D), lambda b, m: (b, 0, 0))
    return pl.pallas_call(
        body, name="out_proj_norm", grid=(nb, s // tm),
        in_specs=[tile, tile, pl.BlockSpec((D, D), lambda b, m: (0, 0)), vec,
                  pl.BlockSpec((1, D), lambda b, m: (0, 0)), vec, vec],
        out_specs=[tile, tile, pl.BlockSpec((None, D, tm), lambda b, m: (b, 0, m))],
        out_shape=[jax.ShapeDtypeStruct(x.shape, F32), jax.ShapeDtypeStruct(x.shape, BF16),
                   jax.ShapeDtypeStruct((nb, D, s), BF16)],
        compiler_params=_params(("arbitrary", "arbitrary")),
    )(x, merged, wout, gate, gain, shift, scale)


def _out_proj_loss(x, merged, wout, gate, gain, target):
    nb, s, _ = x.shape
    tm = _row_tile(s, 512)

    def body(x_ref, m_ref, w_ref, gt_ref, g_ref, t_ref, loss_ref, dx_ref, dg_ref):
        first = (pl.program_id(0) == 0) & (pl.program_id(1) == 0)
        last = (pl.program_id(0) == nb - 1) & (pl.program_id(1) == s // tm - 1)

        @pl.when(first)
        def _():
            loss_ref[...] = jnp.zeros_like(loss_ref)
            dg_ref[...] = jnp.zeros_like(dg_ref)

        xv = x_ref[...] + gt_ref[...] * _dot(m_ref[...], w_ref[...])
        r = lax.rsqrt(jnp.mean(xv * xv, axis=1, keepdims=True) + EPS)
        xn = xv * r
        g = g_ref[...]
        e = xn * g - t_ref[...]
        loss_ref[...] += _rowsum(e * e) * (0.5 / D)
        dy = e * (1.0 / D)
        dg_ref[...] += _rowsum(dy * xn)
        dxn = dy * g
        dx_ref[...] = r * (dxn - xn * jnp.mean(dxn * xn, axis=1, keepdims=True))

        @pl.when(last)
        def _():
            loss_ref[...] = jnp.broadcast_to(jnp.sum(loss_ref[...], axis=1, keepdims=True), (1, D))

    tile = pl.BlockSpec((None, tm, D), lambda b, m: (b, m, 0))
    vec = pl.BlockSpec((1, D), lambda b, m: (0, 0))
    return pl.pallas_call(
        body, name="out_proj_loss", grid=(nb, s // tm),
        in_specs=[tile, tile, pl.BlockSpec((D, D), lambda b, m: (0, 0)),
                  pl.BlockSpec((None, 1, D), lambda b, m: (b, 0, 0)), vec, tile],
        out_specs=[vec, tile, vec],
        out_shape=[jax.ShapeDtypeStruct((1, D), F32), jax.ShapeDtypeStruct(x.shape, F32),
                   jax.ShapeDtypeStruct((1, D), F32)],
        compiler_params=_params(("arbitrary", "arbitrary")),
    )(x, merged, wout, gate, gain, target)


def _out_proj_bwd(dxo, merged, wout, gate):
    nb, s, _ = dxo.shape
    tm = _row_tile(s, 512)

    def body(d_ref, m_ref, w_ref, g_ref, dm_ref, gw_ref, dg_ref):
        @pl.when((pl.program_id(0) == 0) & (pl.program_id(1) == 0))
        def _():
            gw_ref[...] = jnp.zeros_like(gw_ref)

        @pl.when(pl.program_id(1) == 0)
        def _():
            dg_ref[...] = jnp.zeros_like(dg_ref)

        d = d_ref[...]
        m = m_ref[...]
        wv = w_ref[...]
        dg_ref[...] += _rowsum(d * _dot(m, wv))
        dout = (d * g_ref[...]).astype(BF16)
        dm_ref[...] = _dot_nt(dout, wv).astype(BF16)
        gw_ref[...] += _dot_tn(m, dout)

    tile = pl.BlockSpec((None, tm, D), lambda b, m: (b, m, 0))
    vec = pl.BlockSpec((None, 1, D), lambda b, m: (b, 0, 0))
    full = pl.BlockSpec((D, D), lambda b, m: (0, 0))
    return pl.pallas_call(
        body, name="out_proj_bwd", grid=(nb, s // tm),
        in_specs=[tile, tile, full, vec], out_specs=[tile, full, vec],
        out_shape=[jax.ShapeDtypeStruct(dxo.shape, BF16), jax.ShapeDtypeStruct((D, D), F32),
                   jax.ShapeDtypeStruct((nb, 1, D), F32)],
        compiler_params=_params(("arbitrary", "arbitrary")),
    )(dxo, merged, wout, gate)


def _in_proj_bwd_h(dproj, wg, dep):
    _, nb, s, _ = dproj.shape
    tm = _row_tile(s, 1024)

    def body(dp0_ref, dp1_ref, w0_ref, w1_ref, w2_ref, w3_ref, dep_ref, dh_ref):
        j = pl.program_id(2)
        part = (_dot_nt(dp0_ref[...], jnp.concatenate([w0_ref[...], w1_ref[...]], axis=1))
                + _dot_nt(dp1_ref[...], jnp.concatenate([w2_ref[...], w3_ref[...]], axis=1)))

        @pl.when(j == 0)
        def _():
            dh_ref[...] = part

        @pl.when(j > 0)
        def _():
            dh_ref[...] += part

    def seg(k):
        return pl.BlockSpec((None, None, tm, D), lambda b, m, j: (2 * j + k, b, m, 0))

    def unit(k):
        return pl.BlockSpec((None, D, UNIT),
                            lambda b, m, j: ((4 * j + k) // UNITS_PER_DEV, 0, (4 * j + k) % UNITS_PER_DEV))

    return pl.pallas_call(
        body, name="in_proj_bwd_h", grid=(nb, s // tm, N_SEG // 2),
        in_specs=[seg(0), seg(1), unit(0), unit(1), unit(2), unit(3),
                  pl.BlockSpec((8, LANES), lambda b, m, j: (0, 0))],
        out_specs=pl.BlockSpec((None, tm, D), lambda b, m, j: (b, m, 0)),
        out_shape=jax.ShapeDtypeStruct((nb, s, D), F32),
        compiler_params=_params(("arbitrary", "arbitrary", "arbitrary")),
    )(dproj, dproj, wg, wg, wg, wg, dep)


def _norm_mod_bwd(dh, x, dxo, gain, scale):
    nb, s, _ = x.shape
    tm = _row_tile(s, 512)

    def body(dh_ref, x_ref, dxo_ref, g_ref, sc_ref, dx_ref, dsh_ref, dsc_ref, dg_ref):
        b, m = pl.program_id(0), pl.program_id(1)

        @pl.when((b == 0) & (m == 0))
        def _():
            dg_ref[...] = jnp.zeros_like(dg_ref)

        @pl.when(m == 0)
        def _():
            dsh_ref[...] = jnp.zeros_like(dsh_ref)
            dsc_ref[...] = jnp.zeros_like(dsc_ref)

        dh = dh_ref[...]
        xv = x_ref[...]
        r = lax.rsqrt(jnp.mean(xv * xv, axis=1, keepdims=True) + EPS)
        xn = xv * r
        g = g_ref[...]
        one_sc = 1.0 + sc_ref[...]
        dsh_ref[...] += _rowsum(dh)
        dsc_ref[...] += _rowsum(dh * (xn * g))
        dg_ref[...] += _rowsum(dh * one_sc * xn)
        dxn = dh * (g * one_sc)
        dx_ref[...] = dxo_ref[...] + r * (dxn - xn * jnp.mean(dxn * xn, axis=1, keepdims=True))

    tile = pl.BlockSpec((None, tm, D), lambda b, m: (b, m, 0))
    vec = pl.BlockSpec((None, 1, D), lambda b, m: (b, 0, 0))
    one = pl.BlockSpec((1, D), lambda b, m: (0, 0))
    return pl.pallas_call(
        body, name="norm_mod_bwd", grid=(nb, s // tm),
        in_specs=[tile, tile, tile, one, vec],
        out_specs=[tile, vec, vec, one],
        out_shape=[jax.ShapeDtypeStruct(x.shape, F32), jax.ShapeDtypeStruct((nb, 1, D), F32),
                   jax.ShapeDtypeStruct((nb, 1, D), F32), jax.ShapeDtypeStruct((1, D), F32)],
        compiler_params=_params(("arbitrary", "arbitrary")),
    )(dh, x, dxo, gain, scale)


def _in_proj_bwd_w(ht, dproj, dep):
    nb, _, s = ht.shape
    tm = _row_tile(s, 2048)
    n_m = s // tm

    def body(ht_ref, dp_ref, dep_ref, o_ref, acc_ref):
        b, m = pl.program_id(1), pl.program_id(2)

        @pl.when((b == 0) & (m == 0))
        def _():
            acc_ref[...] = jnp.zeros_like(acc_ref)

        acc_ref[...] += _dot(ht_ref[...], dp_ref[...])

        @pl.when((b == nb - 1) & (m == n_m - 1))
        def _():
            o_ref[0] = acc_ref[:, :UNIT].astype(BF16)
            o_ref[1] = acc_ref[:, UNIT:].astype(BF16)

    return pl.pallas_call(
        body, name="in_proj_bwd_w", grid=(N_SEG, nb, n_m),
        in_specs=[pl.BlockSpec((None, D, tm), lambda j, b, m: (b, 0, m)),
                  pl.BlockSpec((None, None, tm, D), lambda j, b, m: (j, b, m, 0)),
                  pl.BlockSpec((8, LANES), lambda j, b, m: (0, 0))],
        out_specs=pl.BlockSpec((2, D, UNIT), lambda j, b, m: (j, 0, 0)),
        out_shape=jax.ShapeDtypeStruct((2 * N_SEG, D, UNIT), BF16),
        scratch_shapes=[pltpu.VMEM((D, D), F32)],
        compiler_params=_params(("arbitrary", "arbitrary", "arbitrary")),
    )(ht, dproj, dep)


def _mod_proj(c_all, w_mod, b_mod_mine):
    nl, _, ncol = w_mod.shape
    nbg = c_all.shape[0]

    def body(c_ref, w_ref, b_ref, o_ref):
        cv = c_ref[...]
        o_ref[...] = jnp.dot(cv * jax.nn.sigmoid(cv), w_ref[...], preferred_element_type=F32,
                             precision=lax.Precision.HIGHEST) + b_ref[...]

    return pl.pallas_call(
        body, name="mod_proj", grid=(nl,),
        in_specs=[pl.BlockSpec((nbg, D), lambda l: (0, 0)), pl.BlockSpec((None, D, ncol), lambda l: (l, 0, 0)),
                  pl.BlockSpec((None, 1, ncol), lambda l: (l, 0, 0))],
        out_specs=pl.BlockSpec((None, nbg, ncol), lambda l: (l, 0, 0)),
        out_shape=jax.ShapeDtypeStruct((nl, nbg, ncol), F32),
        compiler_params=_params(("arbitrary",)),
    )(c_all, w_mod, b_mod_mine)


def _mod_grad(c_all, dmod_all, dmod_mine):
    nl, nbg, ncol = dmod_mine.shape

    def body(c_ref, da_ref, dm_ref, gw_ref, gb_ref):
        cv = c_ref[...]
        gw_ref[...] = lax.dot_general(cv * jax.nn.sigmoid(cv), dm_ref[...], (((0,), (0,)), ((), ())),
                                      preferred_element_type=F32, precision=lax.Precision.HIGHEST)
        gb_ref[...] = _rowsum(da_ref[...])

    return pl.pallas_call(
        body, name="mod_grad", grid=(nl,),
        in_specs=[pl.BlockSpec((nbg, D), lambda l: (0, 0)), pl.BlockSpec((None, nbg, 3 * D), lambda l: (l, 0, 0)),
                  pl.BlockSpec((None, nbg, ncol), lambda l: (l, 0, 0))],
        out_specs=[pl.BlockSpec((None, D, ncol), lambda l: (l, 0, 0)),
                   pl.BlockSpec((None, 1, 3 * D), lambda l: (l, 0, 0))],
        out_shape=[jax.ShapeDtypeStruct((nl, D, ncol), F32), jax.ShapeDtypeStruct((nl, 1, 3 * D), F32)],
        compiler_params=_params(("arbitrary",)),
    )(c_all, dmod_all, dmod_mine)


def _adamw(parts, w, m, v, name, layer=None, prev=None):
    n_parts, n_u, n_r, cu = parts.shape
    assert w.shape[-2:] == (n_r, n_u * cu), (parts.shape, w.shape)
    tr = n_r
    for cand in (512, 256, 128):
        if n_r > cand and n_r % cand == 0:
            tr = cand
            break
    n_prev = 0 if prev is None else 4

    def body(p_ref, w_ref, m_ref, v_ref, *rest):
        g_ref, d_ref, nm_ref, nv_ref = rest[n_prev:]
        g = p_ref[0].astype(F32)
        for k in range(1, n_parts):
            g = g + p_ref[k].astype(F32)
        m2 = ADAM_B1 * m_ref[...] + (1.0 - ADAM_B1) * g
        v2 = ADAM_B2 * v_ref[...] + (1.0 - ADAM_B2) * (g * g)
        m_hat = m2 / (1.0 - ADAM_B1 ** ADAM_STEP)
        v_hat = v2 / (1.0 - ADAM_B2 ** ADAM_STEP)
        g_ref[...] = g
        d_ref[...] = -ADAM_LR * (m_hat / (jnp.sqrt(v_hat) + ADAM_EPS) + ADAM_WD * w_ref[...])
        nm_ref[...] = m2
        nv_ref[...] = v2

    if layer is None:
        tile = pl.BlockSpec((tr, cu), lambda u, i: (i, u))
    else:
        tile = pl.BlockSpec((None, tr, cu), lambda u, i: (layer, i, u))
    shp = jax.ShapeDtypeStruct(w.shape, F32)
    return pl.pallas_call(
        body, name=name, grid=(n_u, n_r // tr),
        in_specs=[pl.BlockSpec((n_parts, None, tr, cu), lambda u, i: (0, u, i, 0)), tile, tile, tile]
        + [pl.BlockSpec(memory_space=pl.ANY)] * n_prev,
        out_specs=[tile, tile, tile, tile], out_shape=[shp, shp, shp, shp],
        input_output_aliases={4 + k: k for k in range(n_prev)},
        compiler_params=_params(("arbitrary", "arbitrary")),
    )(parts, w, m, v, *(prev or ()))


def _gathered_cols(g, inner):
    k = len(inner)
    perm = tuple(range(1, k + 1)) + (0, k + 1)
    t = jnp.transpose(g, perm)
    return t.reshape(tuple(inner) + (g.shape[0] * g.shape[-1],))


def _pair_blocks(wh):
    z = jnp.zeros((8, 64, 64), wh.dtype)
    w2 = wh.reshape(8, 2, 64, 64)
    top = jnp.concatenate([w2[:, 0], z], axis=2)
    bot = jnp.concatenate([z, w2[:, 1]], axis=2)
    return jnp.concatenate([top, bot], axis=1).astype(BF16)


def _unpair_blocks(g):
    return jnp.stack([g[:, :64, :64], g[:, 64:, 64:]], axis=1).reshape(16, 64, 64)


FLAT_ROWS = 512


def _pack_rows(arrays, lead=0):
    parts = [a.reshape(a.shape[:lead] + (-1, LANES)) for a in arrays]
    rows = jnp.concatenate(parts, axis=lead)
    pad = [(0, 0)] * rows.ndim
    pad[lead] = (0, (-rows.shape[lead]) % FLAT_ROWS)
    return jnp.pad(rows, pad)


def kernel(x, c, norm_gain, w_mod, b_mod, w_in, w_out, conv_a_w, sgu_w, sgu_b, lru_conv_w, lru_conv_b, lru_wa, lru_ba, lru_wx, lru_bx, lru_lambda, final_gain, loss_target, m_norm_gain, m_w_mod, m_b_mod, m_w_in, m_w_out, m_conv_a_w, m_sgu_w, m_sgu_b, m_lru_conv_w, m_lru_conv_b, m_lru_wa, m_lru_ba, m_lru_wx, m_lru_bx, m_lru_lambda, m_final_gain, v_norm_gain, v_w_mod, v_b_mod, v_w_in, v_w_out, v_conv_a_w, v_sgu_w, v_sgu_b, v_lru_conv_w, v_lru_conv_b, v_lru_wa, v_lru_ba, v_lru_wx, v_lru_bx, v_lru_lambda, v_final_gain):
    nl = w_in.shape[0]
    nb, s, _ = x.shape
    me = _my_index()
    mod_cols = w_mod.shape[2]


    small = jnp.concatenate([c.reshape(-1, LANES), conv_a_w.reshape(-1, LANES), lru_conv_w.reshape(-1, LANES)])
    n_c, n_ca = nb * D // LANES, nl * 3
    n_small = small.shape[0]
    small = jnp.pad(small, ((0, (-n_small) % 8), (0, 0)))
    small_all, _ = _all_gather(small, "gather_small")
    c_all = small_all[:, :n_c].reshape(N_DEV * nb, D)

    w_in_b = [w_in[0].astype(BF16)] + list(w_in[1:].astype(BF16))

    def start_w_in(l, dep):
        return _split_start(w_in_b[l], _peers_same_core, False, "gather_w_in_start", dep)

    conv_a_full = _gathered_cols(small_all[:, n_c:n_c + n_ca].reshape(N_DEV, nl, 3, LANES), (nl, 3))
    lru_conv_full = _gathered_cols(small_all[:, n_c + n_ca:n_small].reshape(N_DEV, nl, 4, LANES), (nl, 4))

    def gathered_w_in(started, after):
        block, land = _split_wait(started, after, _peers_same_core, False, False, "gather_w_in_wait")
        return _gather_finish(block, land, "gather_w_in_finish")

    sgu_b_lanes = jnp.broadcast_to(sgu_b[..., None], sgu_b.shape + (LANES,))
    mws = []
    for l in range(nl):
        mws.append((conv_a_full[l], sgu_w[l], sgu_b_lanes[l], lru_conv_full[l], lru_conv_b[l][None, :],
                    _pair_blocks(lru_wa[l]), _pair_blocks(lru_wx[l]), lru_ba[l].reshape(1, D),
                    lru_bx[l].reshape(1, D), lru_lambda[l][None, :]))

    rep_names = ["sgu_w", "sgu_b", "lru_conv_b", "lru_wa", "lru_ba", "lru_wx", "lru_bx", "lru_lambda"]
    rep_w = dict(sgu_w=sgu_w, sgu_b=sgu_b, lru_conv_b=lru_conv_b, lru_wa=lru_wa, lru_ba=lru_ba,
                 lru_wx=lru_wx, lru_bx=lru_bx, lru_lambda=lru_lambda)
    rep_m = dict(sgu_w=m_sgu_w, sgu_b=m_sgu_b, lru_conv_b=m_lru_conv_b, lru_wa=m_lru_wa,
                 lru_ba=m_lru_ba, lru_wx=m_lru_wx, lru_bx=m_lru_bx, lru_lambda=m_lru_lambda)
    rep_v = dict(sgu_w=v_sgu_w, sgu_b=v_sgu_b, lru_conv_b=v_lru_conv_b, lru_wa=v_lru_wa,
                 lru_ba=v_lru_ba, lru_wx=v_lru_wx, lru_bx=v_lru_bx, lru_lambda=v_lru_lambda)

    rep_w_all, rep_m_all, rep_v_all = [_pack_rows([src[n] for n in rep_names], lead=1)
                                       for src in (rep_w, rep_m, rep_v)]
    early = [rep_w_all, rep_m_all, rep_v_all] + w_in_b[1:] + [a for mw in mws for a in mw]

    b_mod_mine = lax.dynamic_slice_in_dim(b_mod, me * mod_cols, mod_cols, axis=1)[:, None, :]
    mod_mine = _mod_proj(c_all, w_mod, b_mod_mine)
    mod_all, mod_token = _all_gather(mod_mine.reshape(nl * N_DEV * nb, mod_cols), "gather_mod", dep=w_in_b[0])
    ici = {0: start_w_in(0, mod_token)}
    mod_full = _gathered_cols(mod_all.reshape(N_DEV, nl, N_DEV * nb, mod_cols), (nl, N_DEV * nb))
    mod_loc = lax.dynamic_slice_in_dim(mod_full, me * nb, nb, axis=1)
    shift, scale, gate = [mod_loc[:, :, j * D:(j + 1) * D][:, :, None, :] for j in range(3)]

    xs, hts, projs, mergeds, states, wg = [], [], [], [], [], []
    xl = x
    d2d = {}
    wo_started = _split_start(w_out.astype(BF16).reshape(nl * (D // N_DEV), D), _peers_all, False,
                              "gather_w_out_start", ici[0][4])
    wo = None
    h, ht = _norm_mod(xl, _after(norm_gain[0][None, :], wo_started[4]), shift[0], scale[0])
    for l in range(nl):
        if l == 0:
            wg_l, token = gathered_w_in(ici[0], [h] + early)
            ici[1] = start_w_in(1, token)
            dep = ici[1][4]
        else:
            wg_l = _forward_wait(d2d[l], h, "gather_w_in_d2d_wait")
            dep = d2d[l][4]
        wg.append(wg_l)
        proj = _in_proj(h, wg_l, dep)
        merged, *st = _mixer_fwd(proj, mws[l])
        xs.append(xl), hts.append(ht), projs.append(proj), mergeds.append(merged), states.append(st)
        gate_l = gate[l]
        if l + 1 < nl:
            block, land = _split_wait(ici[l + 1], merged, _peers_same_core, False, False, "gather_w_in_wait")
            d2d[l + 1] = _forward_start(block, land, "gather_w_in_d2d_start")
            gate_l = _after(gate_l, d2d[l + 1][4])
            if l + 2 < nl:
                ici[l + 2] = start_w_in(l + 2, d2d[l + 1][4])
                gate_l = _after(gate_l, ici[l + 2][4])
        if wo is None:
            _, wo_all = _split_wait(wo_started, merged, _peers_all, False, True, "gather_w_out_wait")
            wo = jnp.transpose(wo_all.reshape(N_DEV, nl, D // N_DEV, D), (1, 0, 2, 3)).reshape(nl, D, D)
        if l + 1 < nl:
            xl, h, ht = _out_proj_norm(xl, merged, wo[l], gate_l, norm_gain[l + 1][None, :], shift[l + 1], scale[l + 1])
        else:
            loss_row, dx, g_final = _out_proj_loss(xl, merged, wo[l], gate_l, final_gain[None, :], loss_target)

    res_big, g_conv = {}, [None] * nl
    dmods = [None] * nl

    def finish_exchange(pending, after):
        l, h_in, h_out, h_rep = pending
        r_in, r_out, r_rep = _multi_wait([h_in, h_out, h_rep], after, "scatter_wait")
        res_big["w_in"] = _adamw(r_in, w_in, m_w_in, v_w_in, "adamw_w_in", l, res_big.get("w_in"))
        res_big["w_out"] = _adamw(r_out, w_out, m_w_out, v_w_out, "adamw_w_out", l, res_big.get("w_out"))
        res_big["rep"] = _adamw(r_rep[:, None], rep_w_all, rep_m_all, rep_v_all, "adamw_rep", l, res_big.get("rep"))

    pending = None
    g_gains = [None] * nl
    for l in reversed(range(nl)):
        dmerged, gw_out, dgate = _out_proj_bwd(dx, mergeds[l], wo[l], gate[l])
        dproj, g_caw, g_sw, g_sb, g_lcw, g_vec, g_wa, g_wx = _mixer_bwd(projs[l], dmerged, *states[l], mws[l])
        g_conv[l] = (g_caw, g_lcw)
        rep_g = dict(
            sgu_w=g_sw, sgu_b=g_sb[:, :, 0], lru_conv_b=g_vec[0],
            lru_wa=_unpair_blocks(g_wa), lru_ba=g_vec[1].reshape(16, 64), lru_wx=_unpair_blocks(g_wx),
            lru_bx=g_vec[2].reshape(16, 64), lru_lambda=g_vec[3])
        rep_block = _pack_rows([rep_g[n] for n in rep_names])
        (h_out, h_rep), token = _multi_start([(gw_out.reshape(N_DEV, 1, D // N_DEV, D), True), (rep_block, False)],
                                             "scatter_small_start")
        gw_in = _in_proj_bwd_w(hts[l], dproj, token)
        (h_in,), token = _multi_start([(gw_in.reshape(N_DEV, UNITS_PER_DEV, D, UNIT), True)], "scatter_w_in_start")
        started = (l, h_in, h_out, h_rep)
        dh = _in_proj_bwd_h(dproj, wg[l], token)
        dx, dshift, dscale, g_gain = _norm_mod_bwd(dh, xs[l], dx, norm_gain[l][None, :], scale[l])
        g_gains[l] = g_gain
        dmods[l] = jnp.concatenate([dshift, dscale, dgate], axis=2)[:, 0, :]
        if pending is not None:
            finish_exchange(pending, dx)
        pending = started

    conv_parts = jnp.concatenate(
        [jnp.stack([g_conv[l][0] for l in range(nl)]).reshape(nl * 3, N_DEV, LANES),
         jnp.stack([g_conv[l][1] for l in range(nl)]).reshape(nl * 4, N_DEV, LANES)], axis=0)
    conv_parts = jnp.transpose(conv_parts, (1, 0, 2))[:, None]
    conv_recv = _all_to_all(conv_parts, "scatter_conv")

    dmod_loc = jnp.stack(dmods).reshape(nl * nb, 3 * D)
    gain_rows = jnp.pad(jnp.concatenate(g_gains + [g_final, loss_row], axis=0),
                        ((0, (-(nl + 2)) % 8), (0, 2 * D)))
    tail_g, _ = _all_gather(jnp.concatenate([dmod_loc, gain_rows], axis=0), "gather_dmod",
                            dep=[res_big[k][3] for k in ("w_in", "w_out", "rep")])
    loss = jnp.sum(tail_g[:, nl * nb + nl + 1, 0])
    dmod_g = tail_g[:, :nl * nb]
    gain_parts = tail_g[:, nl * nb:nl * nb + nl + 1, :D][:, None]
    gain_cat = lambda a, b: jnp.concatenate([a, b[None, :]], axis=0)
    res_gain = _adamw(gain_parts, gain_cat(norm_gain, final_gain), gain_cat(m_norm_gain, m_final_gain),
                      gain_cat(v_norm_gain, v_final_gain), "adamw_gain")
    dmod_all = jnp.transpose(dmod_g.reshape(N_DEV, nl, nb, 3 * D), (1, 0, 2, 3)).reshape(nl, N_DEV * nb, 3 * D)
    dmod_mine = lax.dynamic_slice_in_dim(dmod_all, me * mod_cols, mod_cols, axis=2)
    gw_mod, gb_mod = _mod_grad(c_all, dmod_all, dmod_mine)
    res_w_mod = _adamw(gw_mod.reshape(1, 1, nl * D, mod_cols), w_mod.reshape(nl * D, mod_cols),
                       m_w_mod.reshape(nl * D, mod_cols), v_w_mod.reshape(nl * D, mod_cols), "adamw_w_mod")
    res_w_mod = [a.reshape(nl, D, mod_cols) for a in res_w_mod]
    res_b_mod = _adamw(gb_mod.reshape(1, 1, nl, 3 * D), b_mod, m_b_mod, v_b_mod, "adamw_b_mod")
    finish_exchange(pending, res_b_mod[1])

    cat = lambda a, b: jnp.concatenate([a.reshape(nl * 3, LANES), b.reshape(nl * 4, LANES)], axis=0)
    res_conv = _adamw(conv_recv, cat(conv_a_w, lru_conv_w), cat(m_conv_a_w, m_lru_conv_w),
                      cat(v_conv_a_w, v_lru_conv_w), "adamw_conv")
    res_conv_a = [a[:nl * 3].reshape(nl, 3, LANES) for a in res_conv]
    res_lru_conv = [a[nl * 3:].reshape(nl, 4, LANES) for a in res_conv]

    res_rep = []
    for k in range(4):
        off, d = 0, {}
        for n in rep_names:
            n_rows = rep_w[n][0].size // LANES
            d[n] = res_big["rep"][k][:, off:off + n_rows].reshape(rep_w[n].shape)
            off += n_rows
        res_rep.append(d)

    def leaf(k, name):
        if name == "norm_gain":
            return res_gain[k][:nl]
        if name == "final_gain":
            return res_gain[k][nl]
        if name == "w_mod":
            return res_w_mod[k]
        if name == "b_mod":
            return res_b_mod[k]
        if name in ("w_in", "w_out"):
            return res_big[name][k]
        if name == "conv_a_w":
            return res_conv_a[k]
        if name == "lru_conv_w":
            return res_lru_conv[k]
        return res_rep[k][name]

    order = ["norm_gain", "w_mod", "b_mod", "w_in", "w_out", "conv_a_w", "sgu_w", "sgu_b", "lru_conv_w",
             "lru_conv_b", "lru_wa", "lru_ba", "lru_wx", "lru_bx", "lru_lambda", "final_gain"]
    outs = [loss, dx]
    for k in range(4):
        outs += [leaf(k, n) for n in order]
    return tuple(outs)
```
